```python
import math
import jax, jax.numpy as jnp
from jax import lax
import numpy as np

D_MODEL = 1024
BATCH = 8
SEQ = 8192
DEPTH = 1

PLE_DIM = 256
W_CONV = D_MODEL
CONV_GROUPS = 8
CONV_K = 31
HG_HEAD_DIM = 128
W_HGRN = D_MODEL
HG_HEADS = W_HGRN // HG_HEAD_DIM
W_MIX = W_CONV + W_HGRN
CHUNK = 64
EPS = 1e-6
W_IN_COLS = 3 * W_CONV + 4 * W_HGRN

kernel_name = "hymba_conformer_hgrn2_hybrid"


def rms_norm(x, g):
    xf = x.astype(jnp.float32)
    y = xf * lax.rsqrt(jnp.mean(xf * xf, axis=-1, keepdims=True) + EPS)
    return (y * g.astype(jnp.float32)).astype(x.dtype)


def group_layer_norm(x, g, b, n_groups):
    shp = x.shape
    xf = x.astype(jnp.float32).reshape(shp[:-1] + (n_groups, shp[-1] // n_groups))
    mu = jnp.mean(xf, axis=-1, keepdims=True)
    var = jnp.mean(jnp.square(xf - mu), axis=-1, keepdims=True)
    y = ((xf - mu) * lax.rsqrt(var + EPS)).reshape(shp)
    return (y * g.astype(jnp.float32) + b.astype(jnp.float32)).astype(x.dtype)


def conformer_conv_branch(z_val, z_glu, z_gate, conv_w, conv_b, cn_g, cn_b, w_pw2, b_pw2):
    v = z_val * jax.nn.sigmoid(z_glu)
    y = lax.conv_general_dilated(
        v, conv_w[:, None, :].astype(v.dtype),
        window_strides=(1,), padding=[(CONV_K - 1, 0)],
        dimension_numbers=("NWC", "WIO", "NWC"),
        feature_group_count=W_CONV) + conv_b.astype(v.dtype)
    y = jax.nn.silu(group_layer_norm(y, cn_g, cn_b, CONV_GROUPS))
    y = y @ w_pw2.astype(y.dtype) + b_pw2.astype(y.dtype)
    return y * jax.nn.silu(z_gate)


def _gla_chunk_step(S, inp):
    q, k, v, lf = inp
    b = jnp.cumsum(lf, axis=2)
    o_inter = jnp.einsum("bhck,bhkv->bhcv", q * jnp.exp(b), S)
    t_idx = jnp.arange(CHUNK)
    causal = (t_idx[:, None] >= t_idx[None, :])[None, None, :, :, None]
    diff = b[:, :, :, None, :] - b[:, :, None, :, :]
    decay = jnp.exp(jnp.where(causal, diff, -jnp.inf))
    A = jnp.einsum("bhtsk,bhsk->bhts", q[:, :, :, None, :] * decay, k)
    o_intra = jnp.einsum("bhts,bhsv->bhtv", A, v)
    b_last = b[:, :, -1, :]
    k_dec = k * jnp.exp(b_last[:, :, None, :] - b)
    S_new = jnp.exp(b_last)[..., None] * S + jnp.einsum("bhsk,bhsv->bhkv", k_dec, v)
    return S_new, o_inter + o_intra


def hgrn2_branch(zq, zf, zi, zg, lb, onorm_g):
    B, T, _ = zq.shape
    n_chunks = T // CHUNK
    lbf = lb.astype(jnp.float32)
    zf32 = zf.astype(jnp.float32)
    f = lbf + (1.0 - lbf) * jax.nn.sigmoid(zf32)
    log_f = jnp.log(f)
    k = (1.0 - lbf) * jax.nn.sigmoid(-zf32)
    q = jax.nn.silu(zq.astype(jnp.float32))
    v = zi.astype(jnp.float32)

    def to_chunks(t):
        return t.reshape(B, n_chunks, CHUNK, HG_HEADS, HG_HEAD_DIM).transpose(1, 0, 3, 2, 4)

    S0 = jnp.zeros((B, HG_HEADS, HG_HEAD_DIM, HG_HEAD_DIM), jnp.float32)
    _, o = lax.scan(_gla_chunk_step, S0,
                    (to_chunks(q), to_chunks(k), to_chunks(v), to_chunks(log_f)))
    o = o.transpose(1, 0, 3, 2, 4).reshape(B, T, HG_HEADS, HG_HEAD_DIM)
    o = rms_norm(o, onorm_g.reshape(HG_HEADS, HG_HEAD_DIM))
    o = o.reshape(B, T, W_HGRN).astype(zq.dtype)
    return o * jax.nn.silu(zg)


def _fwd_setup_inputs(seed: int = 0) -> dict:
    key = jax.random.key(seed)
    ks = jax.random.split(key, 20)
    f32 = jnp.float32
    nrm = lambda k, s, sc: (jax.random.normal(k, s, f32) * sc).astype(f32)
    return {
        "x": jax.random.normal(ks[0], (BATCH, SEQ, D_MODEL), f32),
        "p": jax.random.normal(ks[1], (DEPTH, BATCH, SEQ, PLE_DIM), f32),
        "ln_g": 1.0 + nrm(ks[2], (DEPTH, D_MODEL), 0.02),
        "w_in": nrm(ks[3], (DEPTH, D_MODEL, W_IN_COLS), D_MODEL ** -0.5),
        "conv_w": nrm(ks[4], (DEPTH, CONV_K, W_CONV), CONV_K ** -0.5),
        "conv_b": nrm(ks[5], (DEPTH, W_CONV), 0.01),
        "cnorm_g": 1.0 + nrm(ks[6], (DEPTH, W_CONV), 0.02),
        "cnorm_b": nrm(ks[7], (DEPTH, W_CONV), 0.01),
        "w_pw2": nrm(ks[8], (DEPTH, W_CONV, W_CONV), W_CONV ** -0.5),
        "b_pw2": nrm(ks[9], (DEPTH, W_CONV), 0.01),
        "lb_logits": nrm(ks[10], (DEPTH + 1, W_HGRN), 0.1),
        "onorm_g": 1.0 + nrm(ks[11], (DEPTH, W_HGRN), 0.02),
        "w_out": nrm(ks[12], (DEPTH, W_MIX, D_MODEL), W_MIX ** -0.5),
        "pe_norm_g": 1.0 + nrm(ks[13], (DEPTH, D_MODEL), 0.02),
        "w_pg": nrm(ks[14], (DEPTH, D_MODEL, D_MODEL), D_MODEL ** -0.5),
        "w_pp": nrm(ks[15], (DEPTH, PLE_DIM, D_MODEL), PLE_DIM ** -0.5),
        "final_g": 1.0 + nrm(ks[16], (D_MODEL,), 0.02),
    }


def _fwd_reference(x, p, ln_g, w_in, conv_w, conv_b, cnorm_g, cnorm_b, w_pw2, b_pw2,
              lb_logits, onorm_g, w_out, pe_norm_g, w_pg, w_pp, final_g):
    lbs = jnp.cumsum(jax.nn.softmax(lb_logits.astype(jnp.float32), axis=0), axis=0)
    h = x
    split_pts = [W_CONV, 2 * W_CONV, 3 * W_CONV,
                 3 * W_CONV + W_HGRN, 3 * W_CONV + 2 * W_HGRN, 3 * W_CONV + 3 * W_HGRN]
    for i in range(DEPTH):
        u = rms_norm(h, ln_g[i])
        z = u @ w_in[i].astype(u.dtype)
        c_val, c_glu, c_gate, hq, hf, hi, hg = jnp.split(z, split_pts, axis=-1)
        y_conv = conformer_conv_branch(c_val, c_glu, c_gate, conv_w[i], conv_b[i],
                                       cnorm_g[i], cnorm_b[i], w_pw2[i], b_pw2[i])
        y_hgrn = hgrn2_branch(hq, hf, hi, hg, lbs[i], onorm_g[i])
        y = jnp.concatenate([y_conv, y_hgrn.astype(y_conv.dtype)], axis=-1)
        h = h + y @ w_out[i].astype(y.dtype)
        pe = p[i] @ w_pp[i].astype(p.dtype)
        gate = jax.nn.sigmoid(rms_norm(h, pe_norm_g[i]) @ w_pg[i].astype(h.dtype))
        h = h + gate * pe.astype(h.dtype)
    return rms_norm(h, final_g)


import jax as _jax
import jax.numpy as _jnp

TWIN_FORMAT = 'train_step'
FWD_PARAMS = ['x', 'p', 'ln_g', 'w_in', 'conv_w', 'conv_b', 'cnorm_g', 'cnorm_b', 'w_pw2', 'b_pw2', 'lb_logits', 'onorm_g', 'w_out', 'pe_norm_g', 'w_pg', 'w_pp', 'final_g']
TWIN_WEIGHTS = ['ln_g', 'w_in', 'conv_w', 'conv_b', 'cnorm_g', 'cnorm_b', 'w_pw2', 'b_pw2', 'lb_logits', 'onorm_g', 'w_out', 'pe_norm_g', 'w_pg', 'w_pp', 'final_g']
TWIN_DIFF_INPUT = 'x'
TWIN_INPUTS = ['x', 'p', 'ln_g', 'w_in', 'conv_w', 'conv_b', 'cnorm_g', 'cnorm_b', 'w_pw2', 'b_pw2', 'lb_logits', 'onorm_g', 'w_out', 'pe_norm_g', 'w_pg', 'w_pp', 'final_g', 'loss_target', 'm_ln_g', 'm_w_in', 'm_conv_w', 'm_conv_b', 'm_cnorm_g', 'm_cnorm_b', 'm_w_pw2', 'm_b_pw2', 'm_lb_logits', 'm_onorm_g', 'm_w_out', 'm_pe_norm_g', 'm_w_pg', 'm_w_pp', 'm_final_g', 'v_ln_g', 'v_w_in', 'v_conv_w', 'v_conv_b', 'v_cnorm_g', 'v_cnorm_b', 'v_w_pw2', 'v_b_pw2', 'v_lb_logits', 'v_onorm_g', 'v_w_out', 'v_pe_norm_g', 'v_w_pg', 'v_w_pp', 'v_final_g']
TWIN_OUTPUTS = ['loss', 'grad_x', 'grad_ln_g', 'grad_w_in', 'grad_conv_w', 'grad_conv_b', 'grad_cnorm_g', 'grad_cnorm_b', 'grad_w_pw2', 'grad_b_pw2', 'grad_lb_logits', 'grad_onorm_g', 'grad_w_out', 'grad_pe_norm_g', 'grad_w_pg', 'grad_w_pp', 'grad_final_g', 'delta_ln_g', 'delta_w_in', 'delta_conv_w', 'delta_conv_b', 'delta_cnorm_g', 'delta_cnorm_b', 'delta_w_pw2', 'delta_b_pw2', 'delta_lb_logits', 'delta_onorm_g', 'delta_w_out', 'delta_pe_norm_g', 'delta_w_pg', 'delta_w_pp', 'delta_final_g', 'new_m_ln_g', 'new_m_w_in', 'new_m_conv_w', 'new_m_conv_b', 'new_m_cnorm_g', 'new_m_cnorm_b', 'new_m_w_pw2', 'new_m_b_pw2', 'new_m_lb_logits', 'new_m_onorm_g', 'new_m_w_out', 'new_m_pe_norm_g', 'new_m_w_pg', 'new_m_w_pp', 'new_m_final_g', 'new_v_ln_g', 'new_v_w_in', 'new_v_conv_w', 'new_v_conv_b', 'new_v_cnorm_g', 'new_v_cnorm_b', 'new_v_w_pw2', 'new_v_b_pw2', 'new_v_lb_logits', 'new_v_onorm_g', 'new_v_w_out', 'new_v_pe_norm_g', 'new_v_w_pg', 'new_v_w_pp', 'new_v_final_g']
TWIN_LEAF_KINDS = {'loss': 'loss', 'grad_x': 'grad_x', 'grad_ln_g': 'grad_w', 'grad_w_in': 'grad_w', 'grad_conv_w': 'grad_w', 'grad_conv_b': 'grad_w', 'grad_cnorm_g': 'grad_w', 'grad_cnorm_b': 'grad_w', 'grad_w_pw2': 'grad_w', 'grad_b_pw2': 'grad_w', 'grad_lb_logits': 'grad_w', 'grad_onorm_g': 'grad_w', 'grad_w_out': 'grad_w', 'grad_pe_norm_g': 'grad_w', 'grad_w_pg': 'grad_w', 'grad_w_pp': 'grad_w', 'grad_final_g': 'grad_w', 'delta_ln_g': 'delta_w', 'delta_w_in': 'delta_w', 'delta_conv_w': 'delta_w', 'delta_conv_b': 'delta_w', 'delta_cnorm_g': 'delta_w', 'delta_cnorm_b': 'delta_w', 'delta_w_pw2': 'delta_w', 'delta_b_pw2': 'delta_w', 'delta_lb_logits': 'delta_w', 'delta_onorm_g': 'delta_w', 'delta_w_out': 'delta_w', 'delta_pe_norm_g': 'delta_w', 'delta_w_pg': 'delta_w', 'delta_w_pp': 'delta_w', 'delta_final_g': 'delta_w', 'new_m_ln_g': 'new_m', 'new_m_w_in': 'new_m', 'new_m_conv_w': 'new_m', 'new_m_conv_b': 'new_m', 'new_m_cnorm_g': 'new_m', 'new_m_cnorm_b': 'new_m', 'new_m_w_pw2': 'new_m', 'new_m_b_pw2': 'new_m', 'new_m_lb_logits': 'new_m', 'new_m_onorm_g': 'new_m', 'new_m_w_out': 'new_m', 'new_m_pe_norm_g': 'new_m', 'new_m_w_pg': 'new_m', 'new_m_w_pp': 'new_m', 'new_m_final_g': 'new_m', 'new_v_ln_g': 'new_v', 'new_v_w_in': 'new_v', 'new_v_conv_w': 'new_v', 'new_v_conv_b': 'new_v', 'new_v_cnorm_g': 'new_v', 'new_v_cnorm_b': 'new_v', 'new_v_w_pw2': 'new_v', 'new_v_b_pw2': 'new_v', 'new_v_lb_logits': 'new_v', 'new_v_onorm_g': 'new_v', 'new_v_w_out': 'new_v', 'new_v_pe_norm_g': 'new_v', 'new_v_w_pg': 'new_v', 'new_v_w_pp': 'new_v', 'new_v_final_g': 'new_v'}


def _forward(args):
    return _fwd_reference(*[args[k] for k in FWD_PARAMS])


def _output_shape():
    out = _jax.eval_shape(lambda: _forward(_fwd_setup_inputs(0)))
    return out.shape, out.dtype

N_MICROBATCH = 1
ADAM_LR = 0.001
ADAM_B1 = 0.9
ADAM_B2 = 0.999
ADAM_EPS = 1e-08
ADAM_WD = 0.01
ADAM_STEP = 10
PER_EXAMPLE_BATCH_AXIS = {'x': 0, 'p': 1, 'loss_target': 0}
SHARED_INPUTS = []
_WEIGHT_DTYPES = {'ln_g': _jnp.float32, 'w_in': _jnp.float32, 'conv_w': _jnp.float32, 'conv_b': _jnp.float32, 'cnorm_g': _jnp.float32, 'cnorm_b': _jnp.float32, 'w_pw2': _jnp.float32, 'b_pw2': _jnp.float32, 'lb_logits': _jnp.float32, 'onorm_g': _jnp.float32, 'w_out': _jnp.float32, 'pe_norm_g': _jnp.float32, 'w_pg': _jnp.float32, 'w_pp': _jnp.float32, 'final_g': _jnp.float32}
MOMENT_SCALE = {'ln_g': 1.503625e-01, 'w_in': 5.623827e-02, 'conv_w': 5.381041e-02, 'conv_b': 1.149156e-01, 'cnorm_g': 6.875157e-02, 'cnorm_b': 5.423694e-02, 'w_pw2': 5.251118e-02, 'b_pw2': 8.965300e-02, 'lb_logits': 8.389353e-03, 'onorm_g': 9.507112e-02, 'w_out': 1.012777e-01, 'pe_norm_g': 4.388127e-02, 'w_pg': 4.289369e-02, 'w_pp': 1.099468e-01, 'final_g': 6.396483e+01}


def _to_microbatches(a, axis):
    t = _jnp.moveaxis(a, axis, 0)
    t = t.reshape((N_MICROBATCH, t.shape[0] // N_MICROBATCH) + t.shape[1:])
    return _jnp.moveaxis(t, 1, axis + 1)


def setup_inputs(seed: int = 0) -> dict:
    inp = _fwd_setup_inputs(seed)
    key = _jax.random.fold_in(_jax.random.key(seed), 7919)
    shape, _ = _output_shape()
    out = dict(inp)
    out["loss_target"] = _jax.random.normal(_jax.random.fold_in(key, 0), shape, _jnp.float32)
    for i, name in enumerate(TWIN_WEIGHTS):
        w = inp[name].astype(_jnp.float32)
        if MOMENT_SCALE is None:
            s = _jnp.sqrt(_jnp.mean(_jnp.square(w)) + 1e-30)
        else:
            s = MOMENT_SCALE[name]
        km, kv = _jax.random.split(_jax.random.fold_in(key, i + 1))
        out[name] = w
        out["m_" + name] = s * _jax.random.normal(km, w.shape, _jnp.float32)
        out["v_" + name] = (s * s) * _jax.random.uniform(kv, w.shape, _jnp.float32, 0.5, 1.5)
    if N_MICROBATCH > 1:
        for name, axis in PER_EXAMPLE_BATCH_AXIS.items():
            out[name] = _to_microbatches(out[name], axis)
    return {'x': out['x'], 'p': out['p'], 'ln_g': out['ln_g'], 'w_in': out['w_in'], 'conv_w': out['conv_w'], 'conv_b': out['conv_b'], 'cnorm_g': out['cnorm_g'], 'cnorm_b': out['cnorm_b'], 'w_pw2': out['w_pw2'], 'b_pw2': out['b_pw2'], 'lb_logits': out['lb_logits'], 'onorm_g': out['onorm_g'], 'w_out': out['w_out'], 'pe_norm_g': out['pe_norm_g'], 'w_pg': out['w_pg'], 'w_pp': out['w_pp'], 'final_g': out['final_g'], 'loss_target': out['loss_target'], 'm_ln_g': out['m_ln_g'], 'm_w_in': out['m_w_in'], 'm_conv_w': out['m_conv_w'], 'm_conv_b': out['m_conv_b'], 'm_cnorm_g': out['m_cnorm_g'], 'm_cnorm_b': out['m_cnorm_b'], 'm_w_pw2': out['m_w_pw2'], 'm_b_pw2': out['m_b_pw2'], 'm_lb_logits': out['m_lb_logits'], 'm_onorm_g': out['m_onorm_g'], 'm_w_out': out['m_w_out'], 'm_pe_norm_g': out['m_pe_norm_g'], 'm_w_pg': out['m_w_pg'], 'm_w_pp': out['m_w_pp'], 'm_final_g': out['m_final_g'], 'v_ln_g': out['v_ln_g'], 'v_w_in': out['v_w_in'], 'v_conv_w': out['v_conv_w'], 'v_conv_b': out['v_conv_b'], 'v_cnorm_g': out['v_cnorm_g'], 'v_cnorm_b': out['v_cnorm_b'], 'v_w_pw2': out['v_w_pw2'], 'v_b_pw2': out['v_b_pw2'], 'v_lb_logits': out['v_lb_logits'], 'v_onorm_g': out['v_onorm_g'], 'v_w_out': out['v_w_out'], 'v_pe_norm_g': out['v_pe_norm_g'], 'v_w_pg': out['v_w_pg'], 'v_w_pp': out['v_w_pp'], 'v_final_g': out['v_final_g']}


def _loss(weights, diff, rest, loss_target):
    with _jax.named_scope("forward"):
        args = {**rest, TWIN_DIFF_INPUT: diff, **{k: w.astype(_WEIGHT_DTYPES[k]) for k, w in weights.items()}}
        y = _forward(args)
    with _jax.named_scope("loss_head"):
        err = _jnp.square(y.astype(_jnp.float32) - loss_target)
        return 0.5 * _jnp.sum(_jnp.mean(err, axis=-1)) if err.ndim else 0.5 * err


def _adamw(w, g, m, v):
    m = ADAM_B1 * m + (1.0 - ADAM_B1) * g
    v = ADAM_B2 * v + (1.0 - ADAM_B2) * _jnp.square(g)
    m_hat = m / (1.0 - ADAM_B1 ** ADAM_STEP)
    v_hat = v / (1.0 - ADAM_B2 ** ADAM_STEP)
    delta = -ADAM_LR * (m_hat / (_jnp.sqrt(v_hat) + ADAM_EPS) + ADAM_WD * w)
    return delta, m, v


def reference(x, p, ln_g, w_in, conv_w, conv_b, cnorm_g, cnorm_b, w_pw2, b_pw2, lb_logits, onorm_g, w_out, pe_norm_g, w_pg, w_pp, final_g, loss_target, m_ln_g, m_w_in, m_conv_w, m_conv_b, m_cnorm_g, m_cnorm_b, m_w_pw2, m_b_pw2, m_lb_logits, m_onorm_g, m_w_out, m_pe_norm_g, m_w_pg, m_w_pp, m_final_g, v_ln_g, v_w_in, v_conv_w, v_conv_b, v_cnorm_g, v_cnorm_b, v_w_pw2, v_b_pw2, v_lb_logits, v_onorm_g, v_w_out, v_pe_norm_g, v_w_pg, v_w_pp, v_final_g):
    given = dict(x=x, p=p, ln_g=ln_g, w_in=w_in, conv_w=conv_w, conv_b=conv_b, cnorm_g=cnorm_g, cnorm_b=cnorm_b, w_pw2=w_pw2, b_pw2=b_pw2, lb_logits=lb_logits, onorm_g=onorm_g, w_out=w_out, pe_norm_g=pe_norm_g, w_pg=w_pg, w_pp=w_pp, final_g=final_g, loss_target=loss_target, m_ln_g=m_ln_g, m_w_in=m_w_in, m_conv_w=m_conv_w, m_conv_b=m_conv_b, m_cnorm_g=m_cnorm_g, m_cnorm_b=m_cnorm_b, m_w_pw2=m_w_pw2, m_b_pw2=m_b_pw2, m_lb_logits=m_lb_logits, m_onorm_g=m_onorm_g, m_w_out=m_w_out, m_pe_norm_g=m_pe_norm_g, m_w_pg=m_w_pg, m_w_pp=m_w_pp, m_final_g=m_final_g, v_ln_g=v_ln_g, v_w_in=v_w_in, v_conv_w=v_conv_w, v_conv_b=v_conv_b, v_cnorm_g=v_cnorm_g, v_cnorm_b=v_cnorm_b, v_w_pw2=v_w_pw2, v_b_pw2=v_b_pw2, v_lb_logits=v_lb_logits, v_onorm_g=v_onorm_g, v_w_out=v_w_out, v_pe_norm_g=v_pe_norm_g, v_w_pg=v_w_pg, v_w_pp=v_w_pp, v_final_g=v_final_g)
    weights = {n: given[n] for n in TWIN_WEIGHTS}
    shared = {n: given[n] for n in SHARED_INPUTS}
    per_example = {n: given[n] for n in ['x', 'p']}
    grad_fn = _jax.value_and_grad(_loss, argnums=(0, 1))

    def one_microbatch(ex, loss_target):
        ex = dict(ex)
        diff = ex.pop(TWIN_DIFF_INPUT)
        return grad_fn(weights, diff, {**shared, **ex}, loss_target)

    if N_MICROBATCH == 1:
        loss, (grad_w, grad_x) = one_microbatch(per_example, given["loss_target"])
    else:
        def body(carry, xs):
            loss_sum, grad_sum = carry
            l_k, (gw_k, gx_k) = one_microbatch(xs[0], xs[1])
            with _jax.named_scope("update"):
                return (loss_sum + l_k, _jax.tree.map(_jnp.add, grad_sum, gw_k)), gx_k

        init = (_jnp.zeros((), _jnp.float32), _jax.tree.map(_jnp.zeros_like, weights))
        (loss, grad_w), grad_x = _jax.lax.scan(body, init, (per_example, given["loss_target"]))
    with _jax.named_scope("update"):
        delta_w, new_m, new_v = {}, {}, {}
        for n in TWIN_WEIGHTS:
            delta_w[n], new_m[n], new_v[n] = _adamw(weights[n], grad_w[n], given["m_" + n], given["v_" + n])
    return (loss, grad_x, *[grad_w[n] for n in TWIN_WEIGHTS], *[delta_w[n] for n in TWIN_WEIGHTS],
            *[new_m[n] for n in TWIN_WEIGHTS], *[new_v[n] for n in TWIN_WEIGHTS])
```

```python
import functools

import jax
import jax.numpy as jnp
from jax import lax
from jax.experimental import pallas as pl
from jax.experimental.pallas import tpu as pltpu

F32 = jnp.float32
BF16 = jnp.bfloat16
_MXU = jnp.bfloat16

D = 1024
NPART = 7
PLE = 256
HEADS = 8
HD = 128
CHUNK = 64
CONV_K = 31
HALO = 32
EPS = 1e-6
N_CHIPS = 4
N_DEV = 8
VEC_ROWS = 64

ADAM_LR = 0.001
ADAM_B1 = 0.9
ADAM_B2 = 0.999
ADAM_EPS = 1e-08
ADAM_WD = 0.01
ADAM_STEP = 10

V7X_VMEM_LIMIT = 60000 * 1024
MESH_ID = pl.DeviceIdType.MESH
ANY = pl.BlockSpec(memory_space=pl.ANY)


def _cparams(block_bytes, n_grid_dims):
    limit = min(V7X_VMEM_LIMIT, 2 * block_bytes + (24 << 20))
    return pltpu.CompilerParams(vmem_limit_bytes=int(limit), dimension_semantics=("arbitrary",) * n_grid_dims)


def _nbytes(shape, dtype):
    n = 1
    for s in shape:
        n *= s
    return n * jnp.dtype(dtype).itemsize


def _dot(a, b):
    return jnp.dot(a.astype(_MXU), b.astype(_MXU), preferred_element_type=F32)


def _dot_nt(a, b):
    return lax.dot_general(a.astype(_MXU), b.astype(_MXU), (((1,), (1,)), ((), ())), preferred_element_type=F32)


def _dot_tn(a, b):
    return lax.dot_general(a.astype(_MXU), b.astype(_MXU), (((0,), (0,)), ((), ())), preferred_element_type=F32)


def _tri_dot(tri_bf, x):
    x1 = x.astype(BF16)
    r1 = x - x1.astype(F32)
    x2 = r1.astype(BF16)
    x3 = (r1 - x2.astype(F32)).astype(BF16)
    d = lambda t: jnp.dot(tri_bf, t, preferred_element_type=F32)
    return d(x1) + d(x2) + d(x3)


def _split2(x):
    hi = x.astype(BF16)
    return hi, (x - hi.astype(F32)).astype(BF16)


def _dot3(dims, a, b):
    d = lambda p, q: lax.dot_general(p, q, (dims, ((), ())), preferred_element_type=F32)
    return d(a[0], b[0]) + d(a[0], b[1]) + d(a[1], b[0])


def _sigmoid(x):
    return jax.nn.sigmoid(x)


def _mean_lanes(x):
    return jnp.mean(x, axis=-1, keepdims=True)


def _sum_rows(x):
    return jnp.sum(x, axis=0, keepdims=True)


def _group_ln(y):
    yn, rs = [], []
    for g in range(D // HD):
        blk = y[:, g * HD:(g + 1) * HD]
        xc = blk - _mean_lanes(blk)
        r = lax.rsqrt(_mean_lanes(xc * xc) + EPS)
        yn.append(xc * r)
        rs.append(jnp.broadcast_to(r, blk.shape))
    return jnp.concatenate(yn, axis=1), jnp.concatenate(rs, axis=1)


def _group_ln_bwd(dyn, yn, rstd):
    out = []
    for g in range(D // HD):
        sl = slice(g * HD, (g + 1) * HD)
        d, n = dyn[:, sl], yn[:, sl]
        out.append(rstd[:, sl] * (d - _mean_lanes(d) - n * _mean_lanes(d * n)))
    return jnp.concatenate(out, axis=1)


def _conv_taps(ext):
    return lambda j: pltpu.roll(ext, CONV_K - 1 - j, axis=0)[HALO:, :] if j < CONV_K - 1 else ext[HALO:, :]


def _softmax_row0(lbl):
    m = jnp.max(lbl, axis=0, keepdims=True)
    e = jnp.exp(lbl - m)
    return e[0:1, :] / jnp.sum(e, axis=0, keepdims=True)


def _inproj_fwd(x, ln_g, w_in, tT):
    T = x.shape[0]

    def body(x_ref, g_ref, w_ref, z_ref, u_ref, u_scr):
        @pl.when(pl.program_id(1) == 0)
        def _():
            xv = x_ref[...]
            r = lax.rsqrt(_mean_lanes(xv * xv) + EPS)
            u = (xv * r * g_ref[...]).astype(_MXU)
            u_scr[...] = u
            u_ref[...] = u
        z_ref[...] = jnp.dot(u_scr[...], w_ref[...], preferred_element_type=F32)

    blk = _nbytes((tT, D), F32) * 2 + _nbytes((D, D), _MXU) + _nbytes((tT, D), _MXU) * 2
    return pl.pallas_call(
        body, name="inproj_fwd", grid=(T // tT, NPART),
        in_specs=[pl.BlockSpec((tT, D), lambda i, j: (i, 0)), pl.BlockSpec((1, D), lambda i, j: (0, 0)),
                  pl.BlockSpec((D, D), lambda i, j: (0, j))],
        out_specs=[pl.BlockSpec((tT, D), lambda i, j: (i, j)), pl.BlockSpec((tT, D), lambda i, j: (i, 0))],
        out_shape=[jax.ShapeDtypeStruct((T, NPART * D), F32), jax.ShapeDtypeStruct((T, D), _MXU)],
        scratch_shapes=[pltpu.VMEM((tT, D), _MXU)],
        compiler_params=_cparams(blk, 2),
    )(x, ln_g, w_in)


def _conv_fwd(z, conv_w, conv_b, cn_g, cn_b, w_pw2, b_pw2, tT):
    T = z.shape[0]

    def body(cv_ref, cg_ref, ct_ref, cw_ref, cb_ref, ng_ref, nb_ref, wp_ref, bp_ref, yc_ref, halo):
        @pl.when(pl.program_id(0) == 0)
        def _():
            halo[...] = jnp.zeros_like(halo)
        v = cv_ref[...] * _sigmoid(cg_ref[...])
        tap = _conv_taps(jnp.concatenate([halo[...], v], axis=0))
        halo[...] = v[tT - HALO:, :]
        cw = cw_ref[...]
        y1 = cb_ref[...] + tap(0) * cw[0:1, :]
        for j in range(1, CONV_K):
            y1 = y1 + tap(j) * cw[j:j + 1, :]
        yn, _ = _group_ln(y1)
        apre = yn * ng_ref[...] + nb_ref[...]
        a = apre * _sigmoid(apre)
        y2 = _dot(a, wp_ref[...]) + bp_ref[...]
        ct = ct_ref[...]
        yc_ref[...] = (y2 * (ct * _sigmoid(ct))).astype(_MXU)

    part = lambda p: pl.BlockSpec((tT, D), lambda i: (i, p))
    row = pl.BlockSpec((1, D), lambda i: (0, 0))
    blk = 3 * _nbytes((tT, D), F32) + _nbytes((D, D), _MXU) + _nbytes((tT, D), _MXU) + 12 * _nbytes((tT + HALO, D), F32)
    return pl.pallas_call(
        body, name="conv_fwd", grid=(T // tT,),
        in_specs=[part(0), part(1), part(2), pl.BlockSpec((HALO, D), lambda i: (0, 0)), row, row, row,
                  pl.BlockSpec((D, D), lambda i: (0, 0)), row],
        out_specs=pl.BlockSpec((tT, D), lambda i: (i, 0)),
        out_shape=jax.ShapeDtypeStruct((T, D), _MXU),
        scratch_shapes=[pltpu.VMEM((HALO, D), F32)],
        compiler_params=_cparams(blk, 1),
    )(z, z, z, conv_w, conv_b, cn_g, cn_b, w_pw2, b_pw2)


def _hgrn_gates(lb, hq, hf):
    sq = _sigmoid(hq)
    sg = _sigmoid(hf)
    f = lb + (1.0 - lb) * sg
    return sq, sg, f, hq * sq, (1.0 - lb) * (1.0 - sg), jnp.log(f)


def _chunk_decays(lf, q, k):
    r = lax.broadcasted_iota(jnp.int32, (CHUNK, CHUNK), 0)
    c = lax.broadcasted_iota(jnp.int32, (CHUNK, CHUNK), 1)
    b = _tri_dot((r >= c).astype(BF16), lf)
    bm = b[CHUNK // 2 - 1:CHUNK // 2, :]
    bl = b[CHUNK - 1:CHUNK, :]
    eb = jnp.exp(b)
    eqm = jnp.exp(b - bm)
    ekm = jnp.exp(bm - b)
    ekd = jnp.exp(bl - b)
    return dict(causal=r >= c, eb=eb, eqm=eqm, ekm=ekm, ekd=ekd, ebl=jnp.exp(bl),
                qd=q * eb, qm=q * eqm, km=k * ekm, kd=k * ekd)


def _hgrn_fwd(z, lb_logits, onorm_g, tT):
    T = z.shape[0]
    nc = tT // CHUNK

    def body(lbl_ref, og_ref, hq_ref, hf_ref, hi_ref, hg_ref, o_ref, yh_ref, sc_ref, st):
        @pl.when(pl.program_id(1) == 0)
        def _():
            st[...] = jnp.zeros_like(st)
        lb = _softmax_row0(lbl_ref[...])
        og = og_ref[...]

        def chunk(c, carry):
            sl = pl.ds(pl.multiple_of(c * CHUNK, CHUNK), CHUNK)
            _, _, _, q, k, lf = _hgrn_gates(lb, hq_ref[sl, :], hf_ref[sl, :])
            v = hi_ref[sl, :]
            dc = _chunk_decays(lf, q, k)
            s_t = st[...]
            sc_ref[0, c] = s_t
            a = jnp.where(dc["causal"], _dot_nt(dc["qm"], dc["km"]), 0.0)
            o = _dot_nt(dc["qd"], s_t) + _dot(a, v)
            st[...] = s_t * dc["ebl"] + _dot_tn(v, dc["kd"])
            o_ref[sl, :] = o
            n = o * lax.rsqrt(_mean_lanes(o * o) + EPS)
            hg = hg_ref[sl, :]
            yh_ref[sl, :] = ((n * og) * (hg * _sigmoid(hg))).astype(_MXU)
            return carry

        lax.fori_loop(0, nc, chunk, 0)

    zpart = lambda p: pl.BlockSpec((tT, HD), lambda h, i: (i, p * HEADS + h))
    blk = 6 * _nbytes((tT, HD), F32) + _nbytes((nc, HD, HD), F32)
    return pl.pallas_call(
        body, name="hgrn_fwd", grid=(HEADS, T // tT),
        in_specs=[pl.BlockSpec((2, HD), lambda h, i: (0, h)), pl.BlockSpec((1, HD), lambda h, i: (0, h)),
                  zpart(3), zpart(4), zpart(5), zpart(6)],
        out_specs=[pl.BlockSpec((tT, HD), lambda h, i: (i, h)), pl.BlockSpec((tT, HD), lambda h, i: (i, h)),
                   pl.BlockSpec((1, nc, HD, HD), lambda h, i: (h, i, 0, 0))],
        out_shape=[jax.ShapeDtypeStruct((T, D), F32), jax.ShapeDtypeStruct((T, D), _MXU),
                   jax.ShapeDtypeStruct((HEADS, T // CHUNK, HD, HD), F32)],
        scratch_shapes=[pltpu.VMEM((HD, HD), F32)],
        compiler_params=_cparams(blk, 2),
    )(lb_logits, onorm_g, z, z, z, z)


def _hgrn_bwd(z, lb_logits, onorm_g, o_raw, dyh, s_chunks, tT):
    T = z.shape[0]
    nc = tT // CHUNK
    nI = T // tT

    def body(lbl_ref, og_ref, hq_ref, hf_ref, hi_ref, hg_ref, o_ref, dy_ref, sc_ref, dz_ref, vec_ref, dst):
        @pl.when(pl.program_id(1) == 0)
        def _():
            dst[...] = jnp.zeros_like(dst)
            vec_ref[...] = jnp.zeros_like(vec_ref)
        lb = _softmax_row0(lbl_ref[...])
        og = og_ref[...]
        last_row = lax.broadcasted_iota(jnp.int32, (CHUNK, HD), 0) == CHUNK - 1
        r64 = lax.broadcasted_iota(jnp.int32, (CHUNK, CHUNK), 0)
        c64 = lax.broadcasted_iota(jnp.int32, (CHUNK, CHUNK), 1)
        upper = (c64 >= r64).astype(BF16)

        def chunk(cc, carry):
            c = nc - 1 - cc
            sl = pl.ds(pl.multiple_of(c * CHUNK, CHUNK), CHUNK)
            hq, hf, hg = hq_ref[sl, :], hf_ref[sl, :], hg_ref[sl, :]
            sq, sg, f, q, k, lf = _hgrn_gates(lb, hq, hf)
            v = hi_ref[sl, :]
            dc = _chunk_decays(lf, q, k)
            s_t = sc_ref[0, c]
            ds_t = dst[...]
            o, dy = o_ref[sl, :], dy_ref[sl, :]
            r = lax.rsqrt(_mean_lanes(o * o) + EPS)
            n = o * r
            sgg = _sigmoid(hg)
            silu_g = hg * sgg
            dhg = dy * (n * og) * (sgg * (1.0 + hg * (1.0 - sgg)))
            dn = dy * og * silu_g
            g_og = _sum_rows(dy * n * silu_g)
            do = r * (dn - n * _mean_lanes(dn * n))
            a = jnp.where(dc["causal"], _dot_nt(dc["qm"], dc["km"]), 0.0)
            dqd = _dot(do, s_t)
            dam = jnp.where(dc["causal"], _dot_nt(do, v), 0.0)
            dv = _dot_tn(a, do) + _dot_nt(dc["kd"], ds_t)
            dam2 = _split2(dam)
            dqm = _dot3(((1,), (0,)), dam2, _split2(dc["km"]))
            dkm = _dot3(((0,), (0,)), dam2, _split2(dc["qm"]))
            dkd = _dot(v, ds_t)
            debl = _sum_rows(ds_t * s_t)
            dst[...] = ds_t * dc["ebl"] + _dot_tn(do, dc["qd"])
            dq = dqd * dc["eb"] + dqm * dc["eqm"]
            dk = dkm * dc["ekm"] + dkd * dc["ekd"]
            dbl = _sum_rows(dkd * dc["kd"]) + debl * dc["ebl"]
            db = dq * q - dk * k + jnp.where(last_row, dbl, 0.0)
            dlf = _tri_dot(upper, db)
            dsig = (1.0 - lb) * sg * (1.0 - sg)
            dz_ref[0, sl, :] = (dq * (sq * (1.0 + hq * (1.0 - sq)))).astype(_MXU)
            dz_ref[1, sl, :] = ((dlf / f - dk) * dsig).astype(_MXU)
            dz_ref[2, sl, :] = dv.astype(_MXU)
            dz_ref[3, sl, :] = dhg.astype(_MXU)
            vec_ref[0:1, :] += g_og
            vec_ref[1:2, :] += _sum_rows((dlf / f - dk) * (1.0 - sg))
            return carry

        lax.fori_loop(0, nc, chunk, 0)

    zpart = lambda p: pl.BlockSpec((tT, HD), lambda h, i: (nI - 1 - i, p * HEADS + h))
    act = pl.BlockSpec((tT, HD), lambda h, i: (nI - 1 - i, h))
    blk = 6 * _nbytes((tT, HD), F32) + _nbytes((nc, HD, HD), F32) + 4 * _nbytes((tT, HD), _MXU)
    return pl.pallas_call(
        body, name="hgrn_bwd", grid=(HEADS, nI),
        in_specs=[pl.BlockSpec((2, HD), lambda h, i: (0, h)), pl.BlockSpec((1, HD), lambda h, i: (0, h)),
                  zpart(3), zpart(4), zpart(5), zpart(6), act, act,
                  pl.BlockSpec((1, nc, HD, HD), lambda h, i: (h, nI - 1 - i, 0, 0))],
        out_specs=[pl.BlockSpec((4, tT, HD), lambda h, i: (0, nI - 1 - i, h)),
                   pl.BlockSpec((8, HD), lambda h, i: (0, h))],
        out_shape=[jax.ShapeDtypeStruct((4, T, D), _MXU), jax.ShapeDtypeStruct((8, D), F32)],
        scratch_shapes=[pltpu.VMEM((HD, HD), F32)],
        compiler_params=_cparams(blk, 2),
    )(lb_logits, onorm_g, z, z, z, z, o_raw, dyh, s_chunks)


def _tail(x, yc, yh, p, target, w_out, w_pg, w_pp, pe_g, fin_g, tT):
    T = x.shape[0]

    def body(x_ref, yc_ref, yh_ref, p_ref, t_ref, wo_ref, wg_ref, wp_ref, pg_ref, fg_ref,
             dyc_ref, dyh_ref, dh_ref, n2_ref, ds_ref, dpe_ref, dhb_ref, pb_ref, vec_ref, loss_ref):
        @pl.when(pl.program_id(0) == 0)
        def _():
            vec_ref[...] = jnp.zeros_like(vec_ref)
            loss_ref[...] = jnp.zeros_like(loss_ref)
        wo_c, wo_h = wo_ref[0:D, :], wo_ref[D:2 * D, :]
        h = x_ref[...] + _dot(yc_ref[...], wo_c) + _dot(yh_ref[...], wo_h)
        pb = p_ref[...].astype(_MXU)
        pe = _dot(pb, wp_ref[...])
        r2 = lax.rsqrt(_mean_lanes(h * h) + EPS)
        hn = h * r2
        n2 = (hn * pg_ref[...]).astype(_MXU)
        gate = _sigmoid(_dot(n2, wg_ref[...]))
        h2 = h + gate * pe
        r3 = lax.rsqrt(_mean_lanes(h2 * h2) + EPS)
        h2n = h2 * r3
        err = h2n * fg_ref[...] - t_ref[...]
        loss_ref[...] += 0.5 * jnp.sum(_mean_lanes(err * err))
        dout = err * (1.0 / D)
        vec_ref[0:1, :] += _sum_rows(dout * h2n)
        dn3 = dout * fg_ref[...]
        dh2 = r3 * (dn3 - h2n * _mean_lanes(dn3 * h2n))
        ds = (dh2 * pe * gate * (1.0 - gate)).astype(_MXU)
        dn2 = _dot_nt(ds, wg_ref[...])
        vec_ref[1:2, :] += _sum_rows(dn2 * hn)
        dnn = dn2 * pg_ref[...]
        dh = dh2 + r2 * (dnn - hn * _mean_lanes(dnn * hn))
        dhb = dh.astype(_MXU)
        dyc_ref[...] = _dot_nt(dhb, wo_c)
        dyh_ref[...] = _dot_nt(dhb, wo_h)
        dh_ref[...] = dh
        n2_ref[...] = n2
        ds_ref[...] = ds
        dpe_ref[...] = (dh2 * gate).astype(_MXU)
        dhb_ref[...] = dhb
        pb_ref[...] = pb

    tok = lambda w: pl.BlockSpec((tT, w), lambda i: (i, 0))
    full = lambda r, c: pl.BlockSpec((r, c), lambda i: (0, 0))
    tokshape = lambda w, dt: jax.ShapeDtypeStruct((T, w), dt)
    blk = (5 * _nbytes((tT, D), F32) + 7 * _nbytes((tT, D), _MXU) + _nbytes((4 * D + PLE, D), _MXU)
           + 12 * _nbytes((tT, D), F32))
    return pl.pallas_call(
        body, name="tail_fwd_bwd", grid=(T // tT,),
        in_specs=[tok(D), tok(D), tok(D), tok(PLE), tok(D), full(2 * D, D), full(D, D), full(PLE, D), full(1, D), full(1, D)],
        out_specs=[tok(D), tok(D), tok(D), tok(D), tok(D), tok(D), tok(D), tok(PLE), full(8, D), full(8, HD)],
        out_shape=[tokshape(D, F32), tokshape(D, F32), tokshape(D, F32), tokshape(D, _MXU), tokshape(D, _MXU),
                   tokshape(D, _MXU), tokshape(D, _MXU), tokshape(PLE, _MXU),
                   jax.ShapeDtypeStruct((8, D), F32), jax.ShapeDtypeStruct((8, HD), F32)],
        compiler_params=_cparams(blk, 1),
    )(x, yc, yh, p, target, w_out, w_pg, w_pp, pe_g, fin_g)


def _conv_bwd(z, dyc, conv_w, conv_b, cn_g, cn_b, w_pw2, b_pw2, tT):
    T = z.shape[0]
    nI = T // tT
    hb = tT // HALO

    def body(cv_ref, cg_ref, ct_ref, hv_ref, hg_ref, dyc_ref, cw_ref, cb_ref, ng_ref, nb_ref, wp_ref, bp_ref,
             dz_ref, a_ref, dy2_ref, vec_ref, gcw_ref, dhalo):
        i = pl.program_id(0)

        @pl.when(i == 0)
        def _():
            dhalo[...] = jnp.zeros_like(dhalo)
            vec_ref[...] = jnp.zeros_like(vec_ref)
            gcw_ref[...] = jnp.zeros_like(gcw_ref)
        cv, cg, ct = cv_ref[...], cg_ref[...], ct_ref[...]
        sg = _sigmoid(cg)
        v = cv * sg
        has_hist = (i < nI - 1).astype(F32)
        vh = hv_ref[...] * _sigmoid(hg_ref[...]) * has_hist
        tap = _conv_taps(jnp.concatenate([vh, v], axis=0))
        cw = cw_ref[...]
        y1 = cb_ref[...] + tap(0) * cw[0:1, :]
        for j in range(1, CONV_K):
            y1 = y1 + tap(j) * cw[j:j + 1, :]
        yn, rstd = _group_ln(y1)
        apre = yn * ng_ref[...] + nb_ref[...]
        sa = _sigmoid(apre)
        a = (apre * sa).astype(_MXU)
        y2 = _dot(a, wp_ref[...]) + bp_ref[...]
        st = _sigmoid(ct)
        dyc_v = dyc_ref[...]
        dy2 = dyc_v * (ct * st)
        dy2b = dy2.astype(_MXU)
        da = _dot_nt(dy2b, wp_ref[...])
        dapre = da * (sa * (1.0 + apre * (1.0 - sa)))
        dy1 = _group_ln_bwd(dapre * ng_ref[...], yn, rstd)
        vec_ref[0:1, :] += _sum_rows(dy1)
        vec_ref[1:2, :] += _sum_rows(dapre * yn)
        vec_ref[2:3, :] += _sum_rows(dapre)
        vec_ref[3:4, :] += _sum_rows(dy2)
        for j in range(CONV_K):
            gcw_ref[j:j + 1, :] += _sum_rows(dy1 * tap(j))
        ext2 = jnp.concatenate([dy1, dhalo[...]], axis=0)
        dhalo[...] = dy1[0:HALO, :]
        dv = dy1 * cw[CONV_K - 1:CONV_K, :]
        for j in range(CONV_K - 1):
            dv = dv + pltpu.roll(ext2, tT + HALO - (CONV_K - 1 - j), axis=0)[0:tT, :] * cw[j:j + 1, :]
        dz_ref[0] = (dv * sg).astype(_MXU)
        dz_ref[1] = (dv * cv * sg * (1.0 - sg)).astype(_MXU)
        dz_ref[2] = (dyc_v * y2 * (st * (1.0 + ct * (1.0 - st)))).astype(_MXU)
        a_ref[...] = a
        dy2_ref[...] = dy2b

    part = lambda p: pl.BlockSpec((tT, D), lambda i: (nI - 1 - i, p))
    hist = lambda p: pl.BlockSpec((HALO, D), lambda i: (jnp.maximum((nI - 1 - i) * hb - 1, 0), p))
    tok = pl.BlockSpec((tT, D), lambda i: (nI - 1 - i, 0))
    row = pl.BlockSpec((1, D), lambda i: (0, 0))
    blk = (4 * _nbytes((tT, D), F32) + _nbytes((D, D), _MXU) + 5 * _nbytes((tT, D), _MXU)
           + 16 * _nbytes((tT + HALO, D), F32))
    return pl.pallas_call(
        body, name="conv_bwd", grid=(nI,),
        in_specs=[part(0), part(1), part(2), hist(0), hist(1), tok, pl.BlockSpec((HALO, D), lambda i: (0, 0)),
                  row, row, row, pl.BlockSpec((D, D), lambda i: (0, 0)), row],
        out_specs=[pl.BlockSpec((3, tT, D), lambda i: (0, nI - 1 - i, 0)), tok, tok,
                   pl.BlockSpec((8, D), lambda i: (0, 0)), pl.BlockSpec((HALO, D), lambda i: (0, 0))],
        out_shape=[jax.ShapeDtypeStruct((3, T, D), _MXU), jax.ShapeDtypeStruct((T, D), _MXU),
                   jax.ShapeDtypeStruct((T, D), _MXU), jax.ShapeDtypeStruct((8, D), F32),
                   jax.ShapeDtypeStruct((HALO, D), F32)],
        scratch_shapes=[pltpu.VMEM((HALO, D), F32)],
        compiler_params=_cparams(blk, 1),
    )(z, z, z, z, z, dyc, conv_w, conv_b, cn_g, cn_b, w_pw2, b_pw2)


def _inproj_bwd_x(x, ln_g, dzc, dzh, w_in, dh, tT):
    T = x.shape[0]

    def body(x_ref, g_ref, dzc_ref, dzh_ref, w_ref, dh_ref, gx_ref, vec_ref, du):
        i, j = pl.program_id(0), pl.program_id(1)

        @pl.when(j == 0)
        def _():
            du[...] = jnp.zeros_like(du)

        @pl.when(jnp.logical_and(i == 0, j == 0))
        def _():
            vec_ref[...] = jnp.zeros_like(vec_ref)

        @pl.when(j < 3)
        def _():
            du[...] += _dot_nt(dzc_ref[0], w_ref[...])

        @pl.when(j >= 3)
        def _():
            du[...] += _dot_nt(dzh_ref[0], w_ref[...])

        @pl.when(j == NPART - 1)
        def _():
            xv = x_ref[...]
            r = lax.rsqrt(_mean_lanes(xv * xv) + EPS)
            xn = xv * r
            duv = du[...]
            vec_ref[0:1, :] += _sum_rows(duv * xn)
            dun = duv * g_ref[...]
            gx_ref[...] = dh_ref[...] + r * (dun - xn * _mean_lanes(dun * xn))

    tok = pl.BlockSpec((tT, D), lambda i, j: (i, 0))
    blk = 3 * _nbytes((tT, D), F32) + 2 * _nbytes((tT, D), _MXU) + _nbytes((D, D), _MXU) + 4 * _nbytes((tT, D), F32)
    return pl.pallas_call(
        body, name="inproj_bwd_x", grid=(T // tT, NPART),
        in_specs=[tok, pl.BlockSpec((1, D), lambda i, j: (0, 0)),
                  pl.BlockSpec((1, tT, D), lambda i, j: (jnp.minimum(j, 2), i, 0)),
                  pl.BlockSpec((1, tT, D), lambda i, j: (jnp.maximum(j - 3, 0), i, 0)),
                  pl.BlockSpec((D, D), lambda i, j: (0, j)), tok],
        out_specs=[tok, pl.BlockSpec((8, D), lambda i, j: (0, 0))],
        out_shape=[jax.ShapeDtypeStruct((T, D), F32), jax.ShapeDtypeStruct((8, D), F32)],
        scratch_shapes=[pltpu.VMEM((tT, D), F32)],
        compiler_params=_cparams(blk, 2),
    )(x, ln_g, dzc, dzh, w_in, dh)


def _inproj_bwd_w(u, dzc, dzh, tk):
    T = u.shape[0]
    nK = T // tk

    def body(u_ref, dzc_ref, dzh_ref, gw_ref):
        j, k = pl.program_id(0), pl.program_id(1)

        @pl.when(k == 0)
        def _():
            gw_ref[...] = jnp.zeros_like(gw_ref)

        @pl.when(j < 3)
        def _():
            gw_ref[...] += _dot_tn(u_ref[...], dzc_ref[0])

        @pl.when(j >= 3)
        def _():
            gw_ref[...] += _dot_tn(u_ref[...], dzh_ref[0])

    blk = 3 * _nbytes((tk, D), _MXU) + 2 * _nbytes((D, D), F32)
    return pl.pallas_call(
        body, name="inproj_bwd_w", grid=(NPART, nK),
        in_specs=[pl.BlockSpec((tk, D), lambda j, k: (k, 0)),
                  pl.BlockSpec((1, tk, D), lambda j, k: (jnp.minimum(j, 2), jnp.where(j < 3, k, nK - 1), 0)),
                  pl.BlockSpec((1, tk, D), lambda j, k: (jnp.maximum(j - 3, 0), jnp.where(j < 3, 0, k), 0))],
        out_specs=pl.BlockSpec((D, D), lambda j, k: (0, j)),
        out_shape=jax.ShapeDtypeStruct((D, NPART * D), F32),
        compiler_params=_cparams(blk, 2),
    )(u, dzc, dzh)


def _tn_matmul(a, b, tk, name):
    T, M = a.shape
    N = b.shape[1]

    def body(a_ref, b_ref, o_ref):
        @pl.when(pl.program_id(0) == 0)
        def _():
            o_ref[...] = jnp.zeros_like(o_ref)
        o_ref[...] += _dot_tn(a_ref[...], b_ref[...])

    blk = _nbytes((tk, M), _MXU) + _nbytes((tk, N), _MXU) + 2 * _nbytes((M, N), F32)
    return pl.pallas_call(
        body, name=name, grid=(T // tk,),
        in_specs=[pl.BlockSpec((tk, M), lambda k: (k, 0)), pl.BlockSpec((tk, N), lambda k: (k, 0))],
        out_specs=pl.BlockSpec((M, N), lambda k: (0, 0)),
        out_shape=jax.ShapeDtypeStruct((M, N), F32),
        compiler_params=_cparams(blk, 1),
    )(a, b)


def _place():
    return lax.axis_index("x"), lax.axis_index("y"), lax.axis_index("c")


def _flip(v, d):
    return 1 - v if d else v


CHIP_MOVES = [(1, 0), (0, 1), (1, 1)]
DEV_MOVES = [(dx, dy, dc) for dx in (0, 1) for dy in (0, 1) for dc in (0, 1)][1:]


def _shard_slice(ref, axis, size, s):
    start = pl.multiple_of(s * size, size)
    return ref.at[pl.ds(start, size), :] if axis == 0 else ref.at[:, pl.ds(start, size)]


def _all_gather_shards(shards, axes):
    n = len(shards)
    full_shapes = [tuple(d * (N_CHIPS if a == ax else 1) for a, d in enumerate(s.shape)) for s, ax in zip(shards, axes)]

    def body(*refs):
        ins, outs = refs[:n], refs[n:2 * n]
        send_sems, recv_sems, loc_sems = refs[2 * n:]
        x, y, c = _place()
        me = 2 * x + y
        locs, sends = [], []
        for k in range(n):
            size = shards[k].shape[axes[k]]
            loc = pltpu.make_async_copy(ins[k], _shard_slice(outs[k], axes[k], size, me), loc_sems.at[k])
            loc.start()
            locs.append(loc)
            for j, (dx, dy) in enumerate(CHIP_MOVES):
                cp = pltpu.make_async_remote_copy(
                    src_ref=ins[k], dst_ref=_shard_slice(outs[k], axes[k], size, me),
                    send_sem=send_sems.at[3 * k + j], recv_sem=recv_sems.at[3 * k + j],
                    device_id=(_flip(x, dx), _flip(y, dy), c), device_id_type=MESH_ID)
                cp.start()
                sends.append(cp)
        for k in range(n):
            size = shards[k].shape[axes[k]]
            for j, (dx, dy) in enumerate(CHIP_MOVES):
                peer = 2 * _flip(x, dx) + _flip(y, dy)
                pltpu.make_async_remote_copy(
                    src_ref=ins[k], dst_ref=_shard_slice(outs[k], axes[k], size, peer),
                    send_sem=send_sems.at[3 * k + j], recv_sem=recv_sems.at[3 * k + j],
                    device_id=(_flip(x, dx), _flip(y, dy), c), device_id_type=MESH_ID).wait_recv()
        for cp in sends:
            cp.wait_send()
        for loc in locs:
            loc.wait()

    return pl.pallas_call(
        body, name="gather_weights",
        in_specs=[ANY] * n, out_specs=[ANY] * n,
        out_shape=[jax.ShapeDtypeStruct(fs, s.dtype) for fs, s in zip(full_shapes, shards)],
        scratch_shapes=[pltpu.SemaphoreType.DMA((3 * n,)), pltpu.SemaphoreType.DMA((3 * n,)), pltpu.SemaphoreType.DMA((n,))],
    )(*shards)


class _Slab:
    def __init__(self, arrays, pick, shard_shape):
        self.arrays = arrays
        self.pick = pick
        self.rows, self.cols = shard_shape
        self.half = self.rows // 2


def _pair_exchange(slabs):
    n = len(slabs)
    n_in = sum(len(sl.arrays) for sl in slabs)

    def body(*refs):
        ins = refs[:n_in]
        mine, got = refs[n_in:n_in + n], refs[n_in + n:n_in + 2 * n]
        send_sems, recv_sems, loc_sems = refs[n_in + 2 * n:]
        x, y, c = _place()
        started = []
        base = 0
        for k, sl in enumerate(slabs):
            for s in range(N_CHIPS):
                ai, r0, c0 = sl.pick(s)
                src = ins[base + ai]

                def half(hc):
                    return src.at[pl.ds(pl.multiple_of(r0 + hc * sl.half, 8), sl.half), pl.ds(c0, sl.cols)]
                q = N_CHIPS * k + s
                loc = pltpu.make_async_copy(half(c), mine[k].at[s], loc_sems.at[q])
                loc.start()
                cp = pltpu.make_async_remote_copy(
                    src_ref=half(1 - c), dst_ref=got[k].at[s], send_sem=send_sems.at[q], recv_sem=recv_sems.at[q],
                    device_id=(x, y, 1 - c), device_id_type=MESH_ID)
                cp.start()
                started.append((loc, cp))
            base += len(sl.arrays)
        for loc, cp in started:
            cp.wait_recv()
        for loc, cp in started:
            cp.wait_send()
            loc.wait()

    flat_in = [a for sl in slabs for a in sl.arrays]
    compact = [jax.ShapeDtypeStruct((N_CHIPS, sl.half, sl.cols), F32) for sl in slabs]
    outs = pl.pallas_call(
        body, name="grad_pair_exchange",
        in_specs=[ANY] * n_in, out_specs=[ANY] * (2 * n), out_shape=compact + compact,
        scratch_shapes=[pltpu.SemaphoreType.DMA((N_CHIPS * n,))] * 3,
    )(*flat_in)
    return outs[:n], outs[n:]


def _chip_exchange(partials, vec):
    n = len(partials)

    def body(*refs):
        ins, vec_ref = refs[:n], refs[n]
        outs, vec_out = refs[n + 1:2 * n + 1], refs[2 * n + 1]
        send_sems, recv_sems, loc_sems = refs[2 * n + 2:]
        x, y, c = _place()
        me = 2 * x + y
        dev = 2 * me + c
        locs, sends = [], []
        for k in range(n):
            loc = pltpu.make_async_copy(ins[k].at[me], outs[k].at[me], loc_sems.at[k])
            loc.start()
            locs.append(loc)
            for j, (dx, dy) in enumerate(CHIP_MOVES):
                px, py = _flip(x, dx), _flip(y, dy)
                cp = pltpu.make_async_remote_copy(
                    src_ref=ins[k].at[2 * px + py], dst_ref=outs[k].at[me],
                    send_sem=send_sems.at[3 * k + j], recv_sem=recv_sems.at[3 * k + j],
                    device_id=(px, py, c), device_id_type=MESH_ID)
                cp.start()
                sends.append(cp)
        loc = pltpu.make_async_copy(vec_ref, vec_out.at[dev], loc_sems.at[n])
        loc.start()
        locs.append(loc)
        for j, (dx, dy, dc) in enumerate(DEV_MOVES):
            cp = pltpu.make_async_remote_copy(
                src_ref=vec_ref, dst_ref=vec_out.at[dev],
                send_sem=send_sems.at[3 * n + j], recv_sem=recv_sems.at[3 * n + j],
                device_id=(_flip(x, dx), _flip(y, dy), _flip(c, dc)), device_id_type=MESH_ID)
            cp.start()
            sends.append(cp)
        for k in range(n):
            for j, (dx, dy) in enumerate(CHIP_MOVES):
                px, py = _flip(x, dx), _flip(y, dy)
                pltpu.make_async_remote_copy(
                    src_ref=ins[k].at[me], dst_ref=outs[k].at[2 * px + py],
                    send_sem=send_sems.at[3 * k + j], recv_sem=recv_sems.at[3 * k + j],
                    device_id=(px, py, c), device_id_type=MESH_ID).wait_recv()
        for j, (dx, dy, dc) in enumerate(DEV_MOVES):
            px, py, pc = _flip(x, dx), _flip(y, dy), _flip(c, dc)
            pltpu.make_async_remote_copy(
                src_ref=vec_ref, dst_ref=vec_out.at[4 * px + 2 * py + pc],
                send_sem=send_sems.at[3 * n + j], recv_sem=recv_sems.at[3 * n + j],
                device_id=(px, py, pc), device_id_type=MESH_ID).wait_recv()
        for cp in sends:
            cp.wait_send()
        for loc in locs:
            loc.wait()

    n_sem = 3 * n + len(DEV_MOVES)
    outs = pl.pallas_call(
        body, name="grad_chip_exchange",
        in_specs=[ANY] * (n + 1), out_specs=[ANY] * (n + 1),
        out_shape=[jax.ShapeDtypeStruct(p.shape, F32) for p in partials] + [jax.ShapeDtypeStruct((N_DEV,) + vec.shape, F32)],
        scratch_shapes=[pltpu.SemaphoreType.DMA((n_sem,)), pltpu.SemaphoreType.DMA((n_sem,)), pltpu.SemaphoreType.DMA((n + 1,))],
    )(*partials, vec)
    return outs[:n], outs[n]


def _pair_share(halves):
    n = len(halves)

    def body(*refs):
        ins, outs = refs[:n], refs[n:2 * n]
        send_sems, recv_sems, loc_sems = refs[2 * n:]
        x, y, c = _place()
        started = []
        for k in range(n):
            hr = halves[k].shape[0]
            rows = lambda hc: outs[k].at[pl.ds(pl.multiple_of(hc * hr, 8), hr), :]
            loc = pltpu.make_async_copy(ins[k], rows(c), loc_sems.at[k])
            loc.start()
            cp = pltpu.make_async_remote_copy(
                src_ref=ins[k], dst_ref=rows(c), send_sem=send_sems.at[k], recv_sem=recv_sems.at[k],
                device_id=(x, y, 1 - c), device_id_type=MESH_ID)
            cp.start()
            recv = pltpu.make_async_remote_copy(
                src_ref=ins[k], dst_ref=rows(1 - c), send_sem=send_sems.at[k], recv_sem=recv_sems.at[k],
                device_id=(x, y, 1 - c), device_id_type=MESH_ID)
            started.append((loc, cp, recv))
        for loc, cp, recv in started:
            recv.wait_recv()
        for loc, cp, recv in started:
            cp.wait_send()
            loc.wait()

    return pl.pallas_call(
        body, name="grad_pair_share",
        in_specs=[ANY] * n, out_specs=[ANY] * n,
        out_shape=[jax.ShapeDtypeStruct((2 * h.shape[0], h.shape[1]), F32) for h in halves],
        scratch_shapes=[pltpu.SemaphoreType.DMA((n,))] * 3,
    )(*halves)


def _row_block(rows, cols, n_arrays):
    br = rows
    while br % 16 == 0 and 2 * n_arrays * br * cols * 4 > (16 << 20):
        br //= 2
    return br


def _add2(a, b, name):
    rows, cols = a.shape
    br = _row_block(rows, cols, 3)

    def body(a_ref, b_ref, o_ref):
        o_ref[...] = a_ref[...] + b_ref[...]

    spec = pl.BlockSpec((br, cols), lambda i: (i, 0))
    return pl.pallas_call(body, name=name, grid=(rows // br,), in_specs=[spec, spec], out_specs=spec,
                          out_shape=jax.ShapeDtypeStruct(a.shape, F32),
                          compiler_params=_cparams(3 * br * cols * 4, 1))(a, b)


def _sum_slots(a, name):
    n, rows, cols = a.shape
    br = _row_block(rows, cols, n + 1)

    def body(a_ref, o_ref):
        acc = a_ref[0]
        for s in range(1, n):
            acc = acc + a_ref[s]
        o_ref[...] = acc

    return pl.pallas_call(body, name=name, grid=(rows // br,),
                          in_specs=[pl.BlockSpec((n, br, cols), lambda i: (0, i, 0))],
                          out_specs=pl.BlockSpec((br, cols), lambda i: (i, 0)),
                          out_shape=jax.ShapeDtypeStruct((rows, cols), F32),
                          compiler_params=_cparams((n + 1) * br * cols * 4, 1))(a)


def _adamw_math(w, g, m, v):
    m = ADAM_B1 * m + (1.0 - ADAM_B1) * g
    v = ADAM_B2 * v + (1.0 - ADAM_B2) * (g * g)
    m_hat = m / (1.0 - ADAM_B1 ** ADAM_STEP)
    v_hat = v / (1.0 - ADAM_B2 ** ADAM_STEP)
    delta = -ADAM_LR * (m_hat / (jnp.sqrt(v_hat) + ADAM_EPS) + ADAM_WD * w)
    return delta, m, v


def _adamw(g, w, m, v, name):
    rows, cols = g.shape
    br = _row_block(rows, cols, 7)

    def body(g_ref, w_ref, m_ref, v_ref, d_ref, nm_ref, nv_ref):
        d_ref[...], nm_ref[...], nv_ref[...] = _adamw_math(w_ref[...], g_ref[...], m_ref[...], v_ref[...])

    spec = pl.BlockSpec((br, cols), lambda i: (i, 0))
    return pl.pallas_call(body, name=name, grid=(rows // br,), in_specs=[spec] * 4, out_specs=[spec] * 3,
                          out_shape=[jax.ShapeDtypeStruct(g.shape, F32)] * 3,
                          compiler_params=_cparams(7 * br * cols * 4, 1))(g, w, m, v)


ROW_FINAL_G, ROW_PE_G = 0, 1
ROW_CONV_B, ROW_CN_G, ROW_CN_B, ROW_B_PW2 = 8, 9, 10, 11
ROW_LN_G = 16
ROW_ONORM_G, ROW_LB = 24, 25
ROW_CONV_W = 32
SMALL = ["ln_g", "conv_b", "cnorm_g", "cnorm_b", "b_pw2", "onorm_g", "pe_norm_g", "final_g"]
SMALL_ROW = dict(ln_g=ROW_LN_G, conv_b=ROW_CONV_B, cnorm_g=ROW_CN_G, cnorm_b=ROW_CN_B, b_pw2=ROW_B_PW2,
                 onorm_g=ROW_ONORM_G, pe_norm_g=ROW_PE_G, final_g=ROW_FINAL_G)


def _adamw_small(vsum, gcw, lb_logits, params):
    names = SMALL + ["lb_logits", "conv_w"]
    flat = [t for nm in names for t in params[nm]]

    def body(*refs):
        vs_ref, gcw_ref, lbl_ref = refs[:3]
        ins = refs[3:3 + 3 * len(names)]
        outs = refs[3 + 3 * len(names):]
        for q, nm in enumerate(names):
            w_ref, m_ref, v_ref = ins[3 * q:3 * q + 3]
            g_ref, d_ref, nm_ref, nv_ref = outs[4 * q:4 * q + 4]
            if nm == "conv_w":
                g = gcw_ref[...]
            elif nm == "lb_logits":
                lb = _softmax_row0(lbl_ref[...])
                g0 = vs_ref[ROW_LB:ROW_LB + 1, :] * lb * (1.0 - lb)
                g = jnp.concatenate([g0, -g0], axis=0)
            else:
                g = vs_ref[SMALL_ROW[nm]:SMALL_ROW[nm] + 1, :]
            g_ref[...] = g
            d_ref[...], nm_ref[...], nv_ref[...] = _adamw_math(w_ref[...], g, m_ref[...], v_ref[...])

    out_shape = [jax.ShapeDtypeStruct(params[nm][0].shape, F32) for nm in names for _ in range(4)]
    outs = pl.pallas_call(body, name="adamw_small", out_shape=out_shape)(vsum, gcw, lb_logits, *flat)
    return {nm: tuple(outs[4 * q:4 * q + 4]) for q, nm in enumerate(names)}


def _tile(T, want):
    return min(T, want)


def kernel(x, p, ln_g, w_in, conv_w, conv_b, cnorm_g, cnorm_b, w_pw2, b_pw2, lb_logits, onorm_g, w_out, pe_norm_g, w_pg, w_pp, final_g, loss_target, m_ln_g, m_w_in, m_conv_w, m_conv_b, m_cnorm_g, m_cnorm_b, m_w_pw2, m_b_pw2, m_lb_logits, m_onorm_g, m_w_out, m_pe_norm_g, m_w_pg, m_w_pp, m_final_g, v_ln_g, v_w_in, v_conv_w, v_conv_b, v_cnorm_g, v_cnorm_b, v_w_pw2, v_b_pw2, v_lb_logits, v_onorm_g, v_w_out, v_pe_norm_g, v_w_pg, v_w_pp, v_final_g):
    given = dict(locals())
    x2, p2, tgt = x[0], p[0, 0], loss_target[0]
    T = x2.shape[0]
    fin_g = final_g.reshape(1, D)

    conv_w_pad = jnp.pad(conv_w[0], ((0, HALO - CONV_K), (0, 0)))
    w_in_f, w_pw2_f, w_out_f, w_pg_f, w_pp_f, conv_w_f = _all_gather_shards(
        [w_in[0].astype(_MXU), w_pw2[0].astype(_MXU), w_out[0].astype(_MXU), w_pg[0].astype(_MXU),
         w_pp[0].astype(_MXU), conv_w_pad],
        [1, 0, 0, 0, 1, 1])

    z, u = _inproj_fwd(x2, ln_g, w_in_f, _tile(T, 512))
    yc = _conv_fwd(z, conv_w_f, conv_b, cnorm_g, cnorm_b, w_pw2_f, b_pw2, _tile(T, 256))
    o_raw, yh, s_chunks = _hgrn_fwd(z, lb_logits, onorm_g, _tile(T, 512))
    dyc, dyh, dh, n2, ds, dpe, dhb, pb, vec_tail, loss_part = _tail(
        x2, yc, yh, p2, tgt, w_out_f, w_pg_f, w_pp_f, pe_norm_g, fin_g, _tile(T, 256))
    dzh, vec_hgrn = _hgrn_bwd(z, lb_logits, onorm_g, o_raw, dyh, s_chunks, _tile(T, 512))
    dzc, a_act, dy2, vec_conv, g_conv_w = _conv_bwd(z, dyc, conv_w_f, conv_b, cnorm_g, cnorm_b, w_pw2_f, b_pw2, _tile(T, 256))
    grad_x, vec_in = _inproj_bwd_x(x2, ln_g, dzc, dzh, w_in_f, dh, _tile(T, 512))
    tk = _tile(T, 512)
    g_w_in = _inproj_bwd_w(u, dzc, dzh, tk)
    g_w_pw2 = _tn_matmul(a_act, dy2, tk, "grad_w_pw2")
    g_w_out_c = _tn_matmul(yc, dhb, tk, "grad_w_out_conv")
    g_w_out_h = _tn_matmul(yh, dhb, tk, "grad_w_out_hgrn")
    g_w_pg = _tn_matmul(n2, ds, tk, "grad_w_pg")
    g_w_pp = _tn_matmul(pb, dpe, tk, "grad_w_pp")

    big = ["w_in", "w_pw2", "w_out", "w_pg", "w_pp"]
    slabs = [
        _Slab([g_w_in], lambda s: (0, 0, s * (NPART * D // N_CHIPS)), (D, NPART * D // N_CHIPS)),
        _Slab([g_w_pw2], lambda s: (0, s * (D // N_CHIPS), 0), (D // N_CHIPS, D)),
        _Slab([g_w_out_c, g_w_out_h], lambda s: (s // 2, (s % 2) * (D // 2), 0), (D // 2, D)),
        _Slab([g_w_pg], lambda s: (0, s * (D // N_CHIPS), 0), (D // N_CHIPS, D)),
        _Slab([g_w_pp], lambda s: (0, 0, s * (D // N_CHIPS)), (PLE, D // N_CHIPS)),
    ]
    mine, got = _pair_exchange(slabs)
    partial = [_add2(a.reshape(-1, a.shape[-1]), b.reshape(-1, b.shape[-1]), "pair_sum_" + nm).reshape(a.shape)
               for a, b, nm in zip(mine, got, big)]
    vec = jnp.concatenate([vec_tail, vec_conv, vec_in, vec_hgrn, g_conv_w], axis=0)
    slots, vec_slots = _chip_exchange(partial, vec)
    halves = [_sum_slots(s, "chip_sum_" + nm) for s, nm in zip(slots, big)]
    vsum = _sum_slots(vec_slots, "vec_sum")
    grads_big = _pair_share(halves)

    out = {}
    for nm, g in zip(big, grads_big):
        w2, m2, v2 = given[nm][0], given["m_" + nm][0], given["v_" + nm][0]
        d, nm_, nv_ = _adamw(g, w2, m2, v2, "adamw_" + nm)
        out[nm] = tuple(t[None] for t in (g, d, nm_, nv_))
    chip = 2 * lax.axis_index("x") + lax.axis_index("y")
    gcw = lax.dynamic_slice(vsum, (ROW_CONV_W, chip * (D // N_CHIPS)), (CONV_K, D // N_CHIPS))
    params = {nm: (given[nm].reshape(-1, D), given["m_" + nm].reshape(-1, D), given["v_" + nm].reshape(-1, D))
              for nm in SMALL + ["lb_logits"]}
    params["conv_w"] = (conv_w[0], m_conv_w[0], v_conv_w[0])
    small = _adamw_small(vsum, gcw, lb_logits, params)
    for nm, ts in small.items():
        out[nm] = tuple(t.reshape(given[nm].shape) for t in ts)

    loss = lax.psum(loss_part[0, 0], ("x", "y", "c"))
    order = ["ln_g", "w_in", "conv_w", "conv_b", "cnorm_g", "cnorm_b", "w_pw2", "b_pw2", "lb_logits", "onorm_g",
             "w_out", "pe_norm_g", "w_pg", "w_pp", "final_g"]
    return (loss, grad_x[None], *[out[nm][0] for nm in order], *[out[nm][1] for nm in order],
            *[out[nm][2] for nm in order], *[out[nm][3] for nm in order])
```

```python
import functools

import jax
import jax.numpy as jnp
from jax import lax
from jax.experimental import pallas as pl
from jax.experimental.pallas import tpu as pltpu

F32 = jnp.float32
BF16 = jnp.bfloat16
_MXU = jnp.bfloat16
_WIRE = jnp.bfloat16

D = 1024
NPART = 7
PLE = 256
HEADS = 8
HD = 128
CHUNK = 64
CONV_K = 31
HALO = 32
EPS = 1e-6
N_CHIPS = 4
N_DEV = 8
VEC_ROWS = 64

ADAM_LR = 0.001
ADAM_B1 = 0.9
ADAM_B2 = 0.999
ADAM_EPS = 1e-08
ADAM_WD = 0.01
ADAM_STEP = 10

V7X_VMEM_LIMIT = 60000 * 1024
MESH_ID = pl.DeviceIdType.MESH
ANY = pl.BlockSpec(memory_space=pl.ANY)


def _cparams(block_bytes, n_grid_dims):
    limit = min(V7X_VMEM_LIMIT, 2 * block_bytes + (24 << 20))
    return pltpu.CompilerParams(vmem_limit_bytes=int(limit), dimension_semantics=("arbitrary",) * n_grid_dims)


def _nbytes(shape, dtype):
    n = 1
    for s in shape:
        n *= s
    return n * jnp.dtype(dtype).itemsize


def _dot(a, b):
    return jnp.dot(a.astype(_MXU), b.astype(_MXU), preferred_element_type=F32)


def _dot_nt(a, b):
    return lax.dot_general(a.astype(_MXU), b.astype(_MXU), (((1,), (1,)), ((), ())), preferred_element_type=F32)


def _dot_tn(a, b):
    return lax.dot_general(a.astype(_MXU), b.astype(_MXU), (((0,), (0,)), ((), ())), preferred_element_type=F32)


def _tri_dot(tri_bf, x):
    x1 = x.astype(BF16)
    r1 = x - x1.astype(F32)
    x2 = r1.astype(BF16)
    x3 = (r1 - x2.astype(F32)).astype(BF16)
    d = lambda t: jnp.dot(tri_bf, t, preferred_element_type=F32)
    return d(x1) + d(x2) + d(x3)


def _split2(x):
    hi = x.astype(BF16)
    return hi, (x - hi.astype(F32)).astype(BF16)


def _dot3(dims, a, b):
    d = lambda p, q: lax.dot_general(p, q, (dims, ((), ())), preferred_element_type=F32)
    return d(a[0], b[0]) + d(a[0], b[1]) + d(a[1], b[0])


def _sigmoid(x):
    return jax.nn.sigmoid(x)


def _mean_lanes(x):
    return jnp.mean(x, axis=-1, keepdims=True)


def _sum_rows(x):
    return jnp.sum(x, axis=0, keepdims=True)


def _group_ln(y):
    yn, rs = [], []
    for g in range(D // HD):
        blk = y[:, g * HD:(g + 1) * HD]
        xc = blk - _mean_lanes(blk)
        r = lax.rsqrt(_mean_lanes(xc * xc) + EPS)
        yn.append(xc * r)
        rs.append(jnp.broadcast_to(r, blk.shape))
    return jnp.concatenate(yn, axis=1), jnp.concatenate(rs, axis=1)


def _group_ln_bwd(dyn, yn, rstd):
    out = []
    for g in range(D // HD):
        sl = slice(g * HD, (g + 1) * HD)
        d, n = dyn[:, sl], yn[:, sl]
        out.append(rstd[:, sl] * (d - _mean_lanes(d) - n * _mean_lanes(d * n)))
    return jnp.concatenate(out, axis=1)


def _conv_taps(ext):
    return lambda j: pltpu.roll(ext, CONV_K - 1 - j, axis=0)[HALO:, :] if j < CONV_K - 1 else ext[HALO:, :]


def _softmax_row0(lbl):
    m = jnp.max(lbl, axis=0, keepdims=True)
    e = jnp.exp(lbl - m)
    return e[0:1, :] / jnp.sum(e, axis=0, keepdims=True)


def _inproj_fwd(x, ln_g, w_in, tT):
    T = x.shape[0]

    def body(x_ref, g_ref, w_ref, z_ref, u_ref, u_scr):
        @pl.when(pl.program_id(1) == 0)
        def _():
            xv = x_ref[...]
            r = lax.rsqrt(_mean_lanes(xv * xv) + EPS)
            u = (xv * r * g_ref[...]).astype(_MXU)
            u_scr[...] = u
            u_ref[...] = u
        z_ref[...] = jnp.dot(u_scr[...], w_ref[...], preferred_element_type=F32)

    blk = _nbytes((tT, D), F32) * 2 + _nbytes((D, D), _MXU) + _nbytes((tT, D), _MXU) * 2
    return pl.pallas_call(
        body, name="inproj_fwd", grid=(T // tT, NPART),
        in_specs=[pl.BlockSpec((tT, D), lambda i, j: (i, 0)), pl.BlockSpec((1, D), lambda i, j: (0, 0)),
                  pl.BlockSpec((D, D), lambda i, j: (0, j))],
        out_specs=[pl.BlockSpec((tT, D), lambda i, j: (i, j)), pl.BlockSpec((tT, D), lambda i, j: (i, 0))],
        out_shape=[jax.ShapeDtypeStruct((T, NPART * D), F32), jax.ShapeDtypeStruct((T, D), _MXU)],
        scratch_shapes=[pltpu.VMEM((tT, D), _MXU)],
        compiler_params=_cparams(blk, 2),
    )(x, ln_g, w_in)


def _conv_fwd(z, conv_w, conv_b, cn_g, cn_b, w_pw2, b_pw2, tT):
    T = z.shape[0]

    def body(cv_ref, cg_ref, ct_ref, cw_ref, cb_ref, ng_ref, nb_ref, wp_ref, bp_ref, yc_ref, halo):
        @pl.when(pl.program_id(0) == 0)
        def _():
            halo[...] = jnp.zeros_like(halo)
        v = cv_ref[...] * _sigmoid(cg_ref[...])
        tap = _conv_taps(jnp.concatenate([halo[...], v], axis=0))
        halo[...] = v[tT - HALO:, :]
        cw = cw_ref[...]
        y1 = cb_ref[...] + tap(0) * cw[0:1, :]
        for j in range(1, CONV_K):
            y1 = y1 + tap(j) * cw[j:j + 1, :]
        yn, _ = _group_ln(y1)
        apre = yn * ng_ref[...] + nb_ref[...]
        a = apre * _sigmoid(apre)
        y2 = _dot(a, wp_ref[...]) + bp_ref[...]
        ct = ct_ref[...]
        yc_ref[...] = (y2 * (ct * _sigmoid(ct))).astype(_MXU)

    part = lambda p: pl.BlockSpec((tT, D), lambda i: (i, p))
    row = pl.BlockSpec((1, D), lambda i: (0, 0))
    blk = 3 * _nbytes((tT, D), F32) + _nbytes((D, D), _MXU) + _nbytes((tT, D), _MXU) + 12 * _nbytes((tT + HALO, D), F32)
    return pl.pallas_call(
        body, name="conv_fwd", grid=(T // tT,),
        in_specs=[part(0), part(1), part(2), pl.BlockSpec((HALO, D), lambda i: (0, 0)), row, row, row,
                  pl.BlockSpec((D, D), lambda i: (0, 0)), row],
        out_specs=pl.BlockSpec((tT, D), lambda i: (i, 0)),
        out_shape=jax.ShapeDtypeStruct((T, D), _MXU),
        scratch_shapes=[pltpu.VMEM((HALO, D), F32)],
        compiler_params=_cparams(blk, 1),
    )(z, z, z, conv_w, conv_b, cn_g, cn_b, w_pw2, b_pw2)


def _hgrn_gates(lb, hq, hf):
    sq = _sigmoid(hq)
    sg = _sigmoid(hf)
    f = lb + (1.0 - lb) * sg
    return sq, sg, f, hq * sq, (1.0 - lb) * (1.0 - sg), jnp.log(f)


def _chunk_decays(lf, q, k):
    r = lax.broadcasted_iota(jnp.int32, (CHUNK, CHUNK), 0)
    c = lax.broadcasted_iota(jnp.int32, (CHUNK, CHUNK), 1)
    b = _tri_dot((r >= c).astype(BF16), lf)
    bm = b[CHUNK // 2 - 1:CHUNK // 2, :]
    bl = b[CHUNK - 1:CHUNK, :]
    eb = jnp.exp(b)
    eqm = jnp.exp(b - bm)
    ekm = jnp.exp(bm - b)
    ekd = jnp.exp(bl - b)
    return dict(causal=r >= c, eb=eb, eqm=eqm, ekm=ekm, ekd=ekd, ebl=jnp.exp(bl),
                qd=q * eb, qm=q * eqm, km=k * ekm, kd=k * ekd)


def _hgrn_fwd(z, lb_logits, onorm_g, tT):
    T = z.shape[0]
    nc = tT // CHUNK

    def body(lbl_ref, og_ref, hq_ref, hf_ref, hi_ref, hg_ref, o_ref, yh_ref, sc_ref, st):
        @pl.when(pl.program_id(1) == 0)
        def _():
            st[...] = jnp.zeros_like(st)
        lb = _softmax_row0(lbl_ref[...])
        og = og_ref[...]

        def chunk(c, carry):
            sl = pl.ds(pl.multiple_of(c * CHUNK, CHUNK), CHUNK)
            _, _, _, q, k, lf = _hgrn_gates(lb, hq_ref[sl, :], hf_ref[sl, :])
            v = hi_ref[sl, :]
            dc = _chunk_decays(lf, q, k)
            s_t = st[...]
            sc_ref[0, c] = s_t
            a = jnp.where(dc["causal"], _dot_nt(dc["qm"], dc["km"]), 0.0)
            o = _dot_nt(dc["qd"], s_t) + _dot(a, v)
            st[...] = s_t * dc["ebl"] + _dot_tn(v, dc["kd"])
            o_ref[sl, :] = o
            n = o * lax.rsqrt(_mean_lanes(o * o) + EPS)
            hg = hg_ref[sl, :]
            yh_ref[sl, :] = ((n * og) * (hg * _sigmoid(hg))).astype(_MXU)
            return carry

        lax.fori_loop(0, nc, chunk, 0)

    zpart = lambda p: pl.BlockSpec((tT, HD), lambda h, i: (i, p * HEADS + h))
    blk = 6 * _nbytes((tT, HD), F32) + _nbytes((nc, HD, HD), F32)
    return pl.pallas_call(
        body, name="hgrn_fwd", grid=(HEADS, T // tT),
        in_specs=[pl.BlockSpec((2, HD), lambda h, i: (0, h)), pl.BlockSpec((1, HD), lambda h, i: (0, h)),
                  zpart(3), zpart(4), zpart(5), zpart(6)],
        out_specs=[pl.BlockSpec((tT, HD), lambda h, i: (i, h)), pl.BlockSpec((tT, HD), lambda h, i: (i, h)),
                   pl.BlockSpec((1, nc, HD, HD), lambda h, i: (h, i, 0, 0))],
        out_shape=[jax.ShapeDtypeStruct((T, D), F32), jax.ShapeDtypeStruct((T, D), _MXU),
                   jax.ShapeDtypeStruct((HEADS, T // CHUNK, HD, HD), F32)],
        scratch_shapes=[pltpu.VMEM((HD, HD), F32)],
        compiler_params=_cparams(blk, 2),
    )(lb_logits, onorm_g, z, z, z, z)


def _hgrn_bwd(z, lb_logits, onorm_g, o_raw, dyh, s_chunks, tT):
    T = z.shape[0]
    nc = tT // CHUNK
    nI = T // tT

    def body(lbl_ref, og_ref, hq_ref, hf_ref, hi_ref, hg_ref, o_ref, dy_ref, sc_ref, dz_ref, vec_ref, dst):
        @pl.when(pl.program_id(1) == 0)
        def _():
            dst[...] = jnp.zeros_like(dst)
            vec_ref[...] = jnp.zeros_like(vec_ref)
        lb = _softmax_row0(lbl_ref[...])
        og = og_ref[...]
        last_row = lax.broadcasted_iota(jnp.int32, (CHUNK, HD), 0) == CHUNK - 1
        r64 = lax.broadcasted_iota(jnp.int32, (CHUNK, CHUNK), 0)
        c64 = lax.broadcasted_iota(jnp.int32, (CHUNK, CHUNK), 1)
        upper = (c64 >= r64).astype(BF16)

        def chunk(cc, carry):
            c = nc - 1 - cc
            sl = pl.ds(pl.multiple_of(c * CHUNK, CHUNK), CHUNK)
            hq, hf, hg = hq_ref[sl, :], hf_ref[sl, :], hg_ref[sl, :]
            sq, sg, f, q, k, lf = _hgrn_gates(lb, hq, hf)
            v = hi_ref[sl, :]
            dc = _chunk_decays(lf, q, k)
            s_t = sc_ref[0, c]
            ds_t = dst[...]
            o, dy = o_ref[sl, :], dy_ref[sl, :]
            r = lax.rsqrt(_mean_lanes(o * o) + EPS)
            n = o * r
            sgg = _sigmoid(hg)
            silu_g = hg * sgg
            dhg = dy * (n * og) * (sgg * (1.0 + hg * (1.0 - sgg)))
            dn = dy * og * silu_g
            g_og = _sum_rows(dy * n * silu_g)
            do = r * (dn - n * _mean_lanes(dn * n))
            a = jnp.where(dc["causal"], _dot_nt(dc["qm"], dc["km"]), 0.0)
            dqd = _dot(do, s_t)
            dam = jnp.where(dc["causal"], _dot_nt(do, v), 0.0)
            dv = _dot_tn(a, do) + _dot_nt(dc["kd"], ds_t)
            dam2 = _split2(dam)
            dqm = _dot3(((1,), (0,)), dam2, _split2(dc["km"]))
            dkm = _dot3(((0,), (0,)), dam2, _split2(dc["qm"]))
            dkd = _dot(v, ds_t)
            debl = _sum_rows(ds_t * s_t)
            dst[...] = ds_t * dc["ebl"] + _dot_tn(do, dc["qd"])
            dq = dqd * dc["eb"] + dqm * dc["eqm"]
            dk = dkm * dc["ekm"] + dkd * dc["ekd"]
            dbl = _sum_rows(dkd * dc["kd"]) + debl * dc["ebl"]
            db = dq * q - dk * k + jnp.where(last_row, dbl, 0.0)
            dlf = _tri_dot(upper, db)
            dsig = (1.0 - lb) * sg * (1.0 - sg)
            dz_ref[0, sl, :] = (dq * (sq * (1.0 + hq * (1.0 - sq)))).astype(_MXU)
            dz_ref[1, sl, :] = ((dlf / f - dk) * dsig).astype(_MXU)
            dz_ref[2, sl, :] = dv.astype(_MXU)
            dz_ref[3, sl, :] = dhg.astype(_MXU)
            vec_ref[0:1, :] += g_og
            vec_ref[1:2, :] += _sum_rows((dlf / f - dk) * (1.0 - sg))
            return carry

        lax.fori_loop(0, nc, chunk, 0)

    zpart = lambda p: pl.BlockSpec((tT, HD), lambda h, i: (nI - 1 - i, p * HEADS + h))
    act = pl.BlockSpec((tT, HD), lambda h, i: (nI - 1 - i, h))
    blk = 6 * _nbytes((tT, HD), F32) + _nbytes((nc, HD, HD), F32) + 4 * _nbytes((tT, HD), _MXU)
    return pl.pallas_call(
        body, name="hgrn_bwd", grid=(HEADS, nI),
        in_specs=[pl.BlockSpec((2, HD), lambda h, i: (0, h)), pl.BlockSpec((1, HD), lambda h, i: (0, h)),
                  zpart(3), zpart(4), zpart(5), zpart(6), act, act,
                  pl.BlockSpec((1, nc, HD, HD), lambda h, i: (h, nI - 1 - i, 0, 0))],
        out_specs=[pl.BlockSpec((4, tT, HD), lambda h, i: (0, nI - 1 - i, h)),
                   pl.BlockSpec((8, HD), lambda h, i: (0, h))],
        out_shape=[jax.ShapeDtypeStruct((4, T, D), _MXU), jax.ShapeDtypeStruct((8, D), F32)],
        scratch_shapes=[pltpu.VMEM((HD, HD), F32)],
        compiler_params=_cparams(blk, 2),
    )(lb_logits, onorm_g, z, z, z, z, o_raw, dyh, s_chunks)


def _tail(x, yc, yh, p, target, w_out, w_pg, w_pp, pe_g, fin_g, tT):
    T = x.shape[0]

    def body(x_ref, yc_ref, yh_ref, p_ref, t_ref, wo_ref, wg_ref, wp_ref, pg_ref, fg_ref,
             dyc_ref, dyh_ref, dh_ref, n2_ref, ds_ref, dpe_ref, dhb_ref, pb_ref, vec_ref, loss_ref):
        @pl.when(pl.program_id(0) == 0)
        def _():
            vec_ref[...] = jnp.zeros_like(vec_ref)
            loss_ref[...] = jnp.zeros_like(loss_ref)
        wo_c, wo_h = wo_ref[0:D, :], wo_ref[D:2 * D, :]
        h = x_ref[...] + _dot(yc_ref[...], wo_c) + _dot(yh_ref[...], wo_h)
        pb = p_ref[...].astype(_MXU)
        pe = _dot(pb, wp_ref[...])
        r2 = lax.rsqrt(_mean_lanes(h * h) + EPS)
        hn = h * r2
        n2 = (hn * pg_ref[...]).astype(_MXU)
        gate = _sigmoid(_dot(n2, wg_ref[...]))
        h2 = h + gate * pe
        r3 = lax.rsqrt(_mean_lanes(h2 * h2) + EPS)
        h2n = h2 * r3
        err = h2n * fg_ref[...] - t_ref[...]
        loss_ref[...] += 0.5 * jnp.sum(_mean_lanes(err * err))
        dout = err * (1.0 / D)
        vec_ref[0:1, :] += _sum_rows(dout * h2n)
        dn3 = dout * fg_ref[...]
        dh2 = r3 * (dn3 - h2n * _mean_lanes(dn3 * h2n))
        ds = (dh2 * pe * gate * (1.0 - gate)).astype(_MXU)
        dn2 = _dot_nt(ds, wg_ref[...])
        vec_ref[1:2, :] += _sum_rows(dn2 * hn)
        dnn = dn2 * pg_ref[...]
        dh = dh2 + r2 * (dnn - hn * _mean_lanes(dnn * hn))
        dhb = dh.astype(_MXU)
        dyc_ref[...] = _dot_nt(dhb, wo_c)
        dyh_ref[...] = _dot_nt(dhb, wo_h)
        dh_ref[...] = dh
        n2_ref[...] = n2
        ds_ref[...] = ds
        dpe_ref[...] = (dh2 * gate).astype(_MXU)
        dhb_ref[...] = dhb
        pb_ref[...] = pb

    tok = lambda w: pl.BlockSpec((tT, w), lambda i: (i, 0))
    full = lambda r, c: pl.BlockSpec((r, c), lambda i: (0, 0))
    tokshape = lambda w, dt: jax.ShapeDtypeStruct((T, w), dt)
    blk = (5 * _nbytes((tT, D), F32) + 7 * _nbytes((tT, D), _MXU) + _nbytes((4 * D + PLE, D), _MXU)
           + 12 * _nbytes((tT, D), F32))
    return pl.pallas_call(
        body, name="tail_fwd_bwd", grid=(T // tT,),
        in_specs=[tok(D), tok(D), tok(D), tok(PLE), tok(D), full(2 * D, D), full(D, D), full(PLE, D), full(1, D), full(1, D)],
        out_specs=[tok(D), tok(D), tok(D), tok(D), tok(D), tok(D), tok(D), tok(PLE), full(8, D), full(8, HD)],
        out_shape=[tokshape(D, F32), tokshape(D, F32), tokshape(D, F32), tokshape(D, _MXU), tokshape(D, _MXU),
                   tokshape(D, _MXU), tokshape(D, _MXU), tokshape(PLE, _MXU),
                   jax.ShapeDtypeStruct((8, D), F32), jax.ShapeDtypeStruct((8, HD), F32)],
        compiler_params=_cparams(blk, 1),
    )(x, yc, yh, p, target, w_out, w_pg, w_pp, pe_g, fin_g)


def _conv_bwd(z, dyc, conv_w, conv_b, cn_g, cn_b, w_pw2, b_pw2, tT):
    T = z.shape[0]
    nI = T // tT
    hb = tT // HALO

    def body(cv_ref, cg_ref, ct_ref, hv_ref, hg_ref, dyc_ref, cw_ref, cb_ref, ng_ref, nb_ref, wp_ref, bp_ref,
             dz_ref, a_ref, dy2_ref, vec_ref, gcw_ref, dhalo):
        i = pl.program_id(0)

        @pl.when(i == 0)
        def _():
            dhalo[...] = jnp.zeros_like(dhalo)
            vec_ref[...] = jnp.zeros_like(vec_ref)
            gcw_ref[...] = jnp.zeros_like(gcw_ref)
        cv, cg, ct = cv_ref[...], cg_ref[...], ct_ref[...]
        sg = _sigmoid(cg)
        v = cv * sg
        has_hist = (i < nI - 1).astype(F32)
        vh = hv_ref[...] * _sigmoid(hg_ref[...]) * has_hist
        tap = _conv_taps(jnp.concatenate([vh, v], axis=0))
        cw = cw_ref[...]
        y1 = cb_ref[...] + tap(0) * cw[0:1, :]
        for j in range(1, CONV_K):
            y1 = y1 + tap(j) * cw[j:j + 1, :]
        yn, rstd = _group_ln(y1)
        apre = yn * ng_ref[...] + nb_ref[...]
        sa = _sigmoid(apre)
        a = (apre * sa).astype(_MXU)
        y2 = _dot(a, wp_ref[...]) + bp_ref[...]
        st = _sigmoid(ct)
        dyc_v = dyc_ref[...]
        dy2 = dyc_v * (ct * st)
        dy2b = dy2.astype(_MXU)
        da = _dot_nt(dy2b, wp_ref[...])
        dapre = da * (sa * (1.0 + apre * (1.0 - sa)))
        dy1 = _group_ln_bwd(dapre * ng_ref[...], yn, rstd)
        vec_ref[0:1, :] += _sum_rows(dy1)
        vec_ref[1:2, :] += _sum_rows(dapre * yn)
        vec_ref[2:3, :] += _sum_rows(dapre)
        vec_ref[3:4, :] += _sum_rows(dy2)
        for j in range(CONV_K):
            gcw_ref[j:j + 1, :] += _sum_rows(dy1 * tap(j))
        ext2 = jnp.concatenate([dy1, dhalo[...]], axis=0)
        dhalo[...] = dy1[0:HALO, :]
        dv = dy1 * cw[CONV_K - 1:CONV_K, :]
        for j in range(CONV_K - 1):
            dv = dv + pltpu.roll(ext2, tT + HALO - (CONV_K - 1 - j), axis=0)[0:tT, :] * cw[j:j + 1, :]
        dz_ref[0] = (dv * sg).astype(_MXU)
        dz_ref[1] = (dv * cv * sg * (1.0 - sg)).astype(_MXU)
        dz_ref[2] = (dyc_v * y2 * (st * (1.0 + ct * (1.0 - st)))).astype(_MXU)
        a_ref[...] = a
        dy2_ref[...] = dy2b

    part = lambda p: pl.BlockSpec((tT, D), lambda i: (nI - 1 - i, p))
    hist = lambda p: pl.BlockSpec((HALO, D), lambda i: (jnp.maximum((nI - 1 - i) * hb - 1, 0), p))
    tok = pl.BlockSpec((tT, D), lambda i: (nI - 1 - i, 0))
    row = pl.BlockSpec((1, D), lambda i: (0, 0))
    blk = (4 * _nbytes((tT, D), F32) + _nbytes((D, D), _MXU) + 5 * _nbytes((tT, D), _MXU)
           + 16 * _nbytes((tT + HALO, D), F32))
    return pl.pallas_call(
        body, name="conv_bwd", grid=(nI,),
        in_specs=[part(0), part(1), part(2), hist(0), hist(1), tok, pl.BlockSpec((HALO, D), lambda i: (0, 0)),
                  row, row, row, pl.BlockSpec((D, D), lambda i: (0, 0)), row],
        out_specs=[pl.BlockSpec((3, tT, D), lambda i: (0, nI - 1 - i, 0)), tok, tok,
                   pl.BlockSpec((8, D), lambda i: (0, 0)), pl.BlockSpec((HALO, D), lambda i: (0, 0))],
        out_shape=[jax.ShapeDtypeStruct((3, T, D), _MXU), jax.ShapeDtypeStruct((T, D), _MXU),
                   jax.ShapeDtypeStruct((T, D), _MXU), jax.ShapeDtypeStruct((8, D), F32),
                   jax.ShapeDtypeStruct((HALO, D), F32)],
        scratch_shapes=[pltpu.VMEM((HALO, D), F32)],
        compiler_params=_cparams(blk, 1),
    )(z, z, z, z, z, dyc, conv_w, conv_b, cn_g, cn_b, w_pw2, b_pw2)


def _inproj_bwd_x(x, ln_g, dzc, dzh, w_in, dh, tT):
    T = x.shape[0]

    def body(x_ref, g_ref, dzc_ref, dzh_ref, w_ref, dh_ref, gx_ref, vec_ref, du):
        i, j = pl.program_id(0), pl.program_id(1)

        @pl.when(j == 0)
        def _():
            du[...] = jnp.zeros_like(du)

        @pl.when(jnp.logical_and(i == 0, j == 0))
        def _():
            vec_ref[...] = jnp.zeros_like(vec_ref)

        @pl.when(j < 3)
        def _():
            du[...] += _dot_nt(dzc_ref[0], w_ref[...])

        @pl.when(j >= 3)
        def _():
            du[...] += _dot_nt(dzh_ref[0], w_ref[...])

        @pl.when(j == NPART - 1)
        def _():
            xv = x_ref[...]
            r = lax.rsqrt(_mean_lanes(xv * xv) + EPS)
            xn = xv * r
            duv = du[...]
            vec_ref[0:1, :] += _sum_rows(duv * xn)
            dun = duv * g_ref[...]
            gx_ref[...] = dh_ref[...] + r * (dun - xn * _mean_lanes(dun * xn))

    tok = pl.BlockSpec((tT, D), lambda i, j: (i, 0))
    blk = 3 * _nbytes((tT, D), F32) + 2 * _nbytes((tT, D), _MXU) + _nbytes((D, D), _MXU) + 4 * _nbytes((tT, D), F32)
    return pl.pallas_call(
        body, name="inproj_bwd_x", grid=(T // tT, NPART),
        in_specs=[tok, pl.BlockSpec((1, D), lambda i, j: (0, 0)),
                  pl.BlockSpec((1, tT, D), lambda i, j: (jnp.minimum(j, 2), i, 0)),
                  pl.BlockSpec((1, tT, D), lambda i, j: (jnp.maximum(j - 3, 0), i, 0)),
                  pl.BlockSpec((D, D), lambda i, j: (0, j)), tok],
        out_specs=[tok, pl.BlockSpec((8, D), lambda i, j: (0, 0))],
        out_shape=[jax.ShapeDtypeStruct((T, D), F32), jax.ShapeDtypeStruct((8, D), F32)],
        scratch_shapes=[pltpu.VMEM((tT, D), F32)],
        compiler_params=_cparams(blk, 2),
    )(x, ln_g, dzc, dzh, w_in, dh)


def _inproj_bwd_w(u, dzc, dzh, tk):
    T = u.shape[0]
    nK = T // tk

    def body(u_ref, dzc_ref, dzh_ref, gw_ref):
        j, k = pl.program_id(0), pl.program_id(1)

        @pl.when(k == 0)
        def _():
            gw_ref[...] = jnp.zeros_like(gw_ref)

        @pl.when(j < 3)
        def _():
            gw_ref[...] += _dot_tn(u_ref[...], dzc_ref[0])

        @pl.when(j >= 3)
        def _():
            gw_ref[...] += _dot_tn(u_ref[...], dzh_ref[0])

    blk = 3 * _nbytes((tk, D), _MXU) + 2 * _nbytes((D, D), F32)
    return pl.pallas_call(
        body, name="inproj_bwd_w", grid=(NPART, nK),
        in_specs=[pl.BlockSpec((tk, D), lambda j, k: (k, 0)),
                  pl.BlockSpec((1, tk, D), lambda j, k: (jnp.minimum(j, 2), jnp.where(j < 3, k, nK - 1), 0)),
                  pl.BlockSpec((1, tk, D), lambda j, k: (jnp.maximum(j - 3, 0), jnp.where(j < 3, 0, k), 0))],
        out_specs=pl.BlockSpec((D, D), lambda j, k: (0, j)),
        out_shape=jax.ShapeDtypeStruct((D, NPART * D), F32),
        compiler_params=_cparams(blk, 2),
    )(u, dzc, dzh)


def _tn_matmul(a, b, tk, name):
    T, M = a.shape
    N = b.shape[1]

    def body(a_ref, b_ref, o_ref):
        @pl.when(pl.program_id(0) == 0)
        def _():
            o_ref[...] = jnp.zeros_like(o_ref)
        o_ref[...] += _dot_tn(a_ref[...], b_ref[...])

    blk = _nbytes((tk, M), _MXU) + _nbytes((tk, N), _MXU) + 2 * _nbytes((M, N), F32)
    return pl.pallas_call(
        body, name=name, grid=(T // tk,),
        in_specs=[pl.BlockSpec((tk, M), lambda k: (k, 0)), pl.BlockSpec((tk, N), lambda k: (k, 0))],
        out_specs=pl.BlockSpec((M, N), lambda k: (0, 0)),
        out_shape=jax.ShapeDtypeStruct((M, N), F32),
        compiler_params=_cparams(blk, 1),
    )(a, b)


def _place():
    return lax.axis_index("x"), lax.axis_index("y"), lax.axis_index("c")


def _flip(v, d):
    return 1 - v if d else v


CHIP_MOVES = [(1, 0), (0, 1), (1, 1)]
DEV_MOVES = [(dx, dy, dc) for dx in (0, 1) for dy in (0, 1) for dc in (0, 1)][1:]


def _shard_slice(ref, axis, size, s):
    start = pl.multiple_of(s * size, size)
    return ref.at[pl.ds(start, size), :] if axis == 0 else ref.at[:, pl.ds(start, size)]


class _Bounce:
    def __init__(self, src, buf, dst, sem_in, sem_out):
        self.load = pltpu.make_async_copy(src, buf, sem_in)
        self.store = pltpu.make_async_copy(buf, dst, sem_out)

    def start(self):
        self.load.start()

    def turn(self):
        self.load.wait()
        self.store.start()

    def wait(self):
        self.store.wait()


def _comm_params(scratch_bytes):
    return pltpu.CompilerParams(vmem_limit_bytes=int(min(V7X_VMEM_LIMIT, scratch_bytes + (8 << 20))))


def _all_gather_shards(shards, axes):
    n = len(shards)
    full_shapes = [tuple(d * (N_CHIPS if a == ax else 1) for a, d in enumerate(s.shape)) for s, ax in zip(shards, axes)]

    def body(*refs):
        ins, outs, bufs = refs[:n], refs[n:2 * n], refs[2 * n:3 * n]
        ici_send, ici_recv, d2d_send, d2d_recv, in_sems, out_sems = refs[3 * n:]
        x, y, c = _place()
        me = 2 * x + y

        def own_half(k, hc):
            half = shards[k].shape[0] // 2
            return ins[k].at[pl.ds(pl.multiple_of(hc * half, 16), half), :]

        def region(k, who, hc):
            rows, cols = shards[k].shape
            half = rows // 2
            if axes[k] == 0:
                return outs[k].at[pl.ds(pl.multiple_of(who * rows + hc * half, 16), half), :]
            return outs[k].at[pl.ds(pl.multiple_of(hc * half, 16), half), pl.ds(pl.multiple_of(who * cols, HD), cols)]

        def ici(k, j, who, hc):
            dx, dy = CHIP_MOVES[j]
            return pltpu.make_async_remote_copy(
                src_ref=own_half(k, hc), dst_ref=region(k, who, hc),
                send_sem=ici_send.at[3 * k + j], recv_sem=ici_recv.at[3 * k + j],
                device_id=(_flip(x, dx), _flip(y, dy), c), device_id_type=MESH_ID)

        def d2d(k, j, who, hc):
            return pltpu.make_async_remote_copy(
                src_ref=region(k, who, hc), dst_ref=region(k, who, hc),
                send_sem=d2d_send.at[3 * k + j], recv_sem=d2d_recv.at[3 * k + j],
                device_id=(x, y, 1 - c), device_id_type=MESH_ID)

        peer = lambda j: 2 * _flip(x, CHIP_MOVES[j][0]) + _flip(y, CHIP_MOVES[j][1])
        locs, sends = [], []
        for k in range(n):
            loc = _Bounce(ins[k], bufs[k], _shard_slice(outs[k], axes[k], shards[k].shape[axes[k]], me),
                          in_sems.at[k], out_sems.at[k])
            loc.start()
            locs.append(loc)
        for k in range(n):
            for j in range(3):
                cp = ici(k, j, me, c)
                cp.start()
                sends.append(cp)
        for loc in locs:
            loc.turn()
        for k in range(n):
            for j in range(3):
                ici(k, j, peer(j), c).wait_recv()
                fw = d2d(k, j, peer(j), c)
                fw.start()
                sends.append(fw)
        for k in range(n):
            for j in range(3):
                d2d(k, j, peer(j), 1 - c).wait_recv()
        for cp in sends:
            cp.wait_send()
        for loc in locs:
            loc.wait()

    sems = [pltpu.SemaphoreType.DMA((3 * n,))] * 4 + [pltpu.SemaphoreType.DMA((n,))] * 2
    return pl.pallas_call(
        body, name="gather_weights",
        in_specs=[ANY] * n, out_specs=[ANY] * n,
        out_shape=[jax.ShapeDtypeStruct(fs, s.dtype) for fs, s in zip(full_shapes, shards)],
        scratch_shapes=[pltpu.VMEM(s.shape, s.dtype) for s in shards] + sems,
        compiler_params=_comm_params(sum(_nbytes(s.shape, s.dtype) for s in shards)),
    )(*shards)


class _Slab:
    def __init__(self, arrays, pick, shard_shape):
        self.arrays = arrays
        self.pick = pick
        self.rows, self.cols = shard_shape
        self.half = self.rows // 2


def _pair_exchange(slabs):
    n = len(slabs)
    n_in = sum(len(sl.arrays) for sl in slabs)

    def body(*refs):
        ins = refs[:n_in]
        mine, got = refs[n_in:n_in + n], refs[n_in + n:n_in + 2 * n]
        bufs = refs[n_in + 2 * n:n_in + 3 * n]
        send_sems, recv_sems, in_sems, out_sems = refs[n_in + 3 * n:]
        x, y, c = _place()
        started = []
        base = 0
        for k, sl in enumerate(slabs):
            for s in range(N_CHIPS):
                ai, r0, c0 = sl.pick(s)
                src = ins[base + ai]

                def half(hc):
                    return src.at[pl.ds(pl.multiple_of(r0 + hc * sl.half, 8), sl.half), pl.ds(c0, sl.cols)]
                q = N_CHIPS * k + s
                loc = _Bounce(half(c), bufs[k].at[s], mine[k].at[s], in_sems.at[q], out_sems.at[q])
                loc.start()
                cp = pltpu.make_async_remote_copy(
                    src_ref=half(1 - c), dst_ref=got[k].at[s], send_sem=send_sems.at[q], recv_sem=recv_sems.at[q],
                    device_id=(x, y, 1 - c), device_id_type=MESH_ID)
                cp.start()
                started.append((loc, cp))
            base += len(sl.arrays)
        for loc, cp in started:
            loc.turn()
        for loc, cp in started:
            cp.wait_recv()
        for loc, cp in started:
            cp.wait_send()
            loc.wait()

    flat_in = [a for sl in slabs for a in sl.arrays]
    compact = [jax.ShapeDtypeStruct((N_CHIPS, sl.half, sl.cols), F32) for sl in slabs]
    outs = pl.pallas_call(
        body, name="grad_pair_exchange",
        in_specs=[ANY] * n_in, out_specs=[ANY] * (2 * n), out_shape=compact + compact,
        scratch_shapes=[pltpu.VMEM(s.shape, F32) for s in compact] + [pltpu.SemaphoreType.DMA((N_CHIPS * n,))] * 4,
        compiler_params=_comm_params(sum(_nbytes(s.shape, F32) for s in compact)),
    )(*flat_in)
    return outs[:n], outs[n:]


def _chip_exchange(partials, vec):
    n = len(partials)

    def body(*refs):
        ins, vec_ref = refs[:n], refs[n]
        outs, vec_out = refs[n + 1:2 * n + 1], refs[2 * n + 1]
        bufs = refs[2 * n + 2:3 * n + 3]
        send_sems, recv_sems, in_sems, out_sems = refs[3 * n + 3:]
        x, y, c = _place()
        me = 2 * x + y
        dev = 2 * me + c
        locs, sends = [], []
        for k in range(n):
            loc = _Bounce(ins[k].at[me], bufs[k], outs[k].at[me], in_sems.at[k], out_sems.at[k])
            loc.start()
            locs.append(loc)
            for j, (dx, dy) in enumerate(CHIP_MOVES):
                px, py = _flip(x, dx), _flip(y, dy)
                cp = pltpu.make_async_remote_copy(
                    src_ref=ins[k].at[2 * px + py], dst_ref=outs[k].at[me],
                    send_sem=send_sems.at[3 * k + j], recv_sem=recv_sems.at[3 * k + j],
                    device_id=(px, py, c), device_id_type=MESH_ID)
                cp.start()
                sends.append(cp)
        loc = _Bounce(vec_ref, bufs[n], vec_out.at[dev], in_sems.at[n], out_sems.at[n])
        loc.start()
        locs.append(loc)
        for j, (dx, dy, dc) in enumerate(DEV_MOVES):
            cp = pltpu.make_async_remote_copy(
                src_ref=vec_ref, dst_ref=vec_out.at[dev],
                send_sem=send_sems.at[3 * n + j], recv_sem=recv_sems.at[3 * n + j],
                device_id=(_flip(x, dx), _flip(y, dy), _flip(c, dc)), device_id_type=MESH_ID)
            cp.start()
            sends.append(cp)
        for loc in locs:
            loc.turn()
        for k in range(n):
            for j, (dx, dy) in enumerate(CHIP_MOVES):
                px, py = _flip(x, dx), _flip(y, dy)
                pltpu.make_async_remote_copy(
                    src_ref=ins[k].at[me], dst_ref=outs[k].at[2 * px + py],
                    send_sem=send_sems.at[3 * k + j], recv_sem=recv_sems.at[3 * k + j],
                    device_id=(px, py, c), device_id_type=MESH_ID).wait_recv()
        for j, (dx, dy, dc) in enumerate(DEV_MOVES):
            px, py, pc = _flip(x, dx), _flip(y, dy), _flip(c, dc)
            pltpu.make_async_remote_copy(
                src_ref=vec_ref, dst_ref=vec_out.at[4 * px + 2 * py + pc],
                send_sem=send_sems.at[3 * n + j], recv_sem=recv_sems.at[3 * n + j],
                device_id=(px, py, pc), device_id_type=MESH_ID).wait_recv()
        for cp in sends:
            cp.wait_send()
        for loc in locs:
            loc.wait()

    n_sem = 3 * n + len(DEV_MOVES)
    bufs = [pltpu.VMEM(p.shape[1:], p.dtype) for p in partials] + [pltpu.VMEM(vec.shape, F32)]
    outs = pl.pallas_call(
        body, name="grad_chip_exchange",
        in_specs=[ANY] * (n + 1), out_specs=[ANY] * (n + 1),
        out_shape=[jax.ShapeDtypeStruct(p.shape, p.dtype) for p in partials] + [jax.ShapeDtypeStruct((N_DEV,) + vec.shape, F32)],
        scratch_shapes=bufs + [pltpu.SemaphoreType.DMA((n_sem,)), pltpu.SemaphoreType.DMA((n_sem,)),
                               pltpu.SemaphoreType.DMA((n + 1,)), pltpu.SemaphoreType.DMA((n + 1,))],
        compiler_params=_comm_params(sum(_nbytes(p.shape[1:], p.dtype) for p in partials) + _nbytes(vec.shape, F32)),
    )(*partials, vec)
    return outs[:n], outs[n]


def _pair_share(halves):
    n = len(halves)

    def body(*refs):
        ins, outs, bufs = refs[:n], refs[n:2 * n], refs[2 * n:3 * n]
        send_sems, recv_sems, in_sems, out_sems = refs[3 * n:]
        x, y, c = _place()
        started = []
        for k in range(n):
            hr = halves[k].shape[0]
            rows = lambda hc, k=k, hr=hr: outs[k].at[pl.ds(pl.multiple_of(hc * hr, 8), hr), :]
            loc = _Bounce(ins[k], bufs[k], rows(c), in_sems.at[k], out_sems.at[k])
            loc.start()
            cp = pltpu.make_async_remote_copy(
                src_ref=ins[k], dst_ref=rows(c), send_sem=send_sems.at[k], recv_sem=recv_sems.at[k],
                device_id=(x, y, 1 - c), device_id_type=MESH_ID)
            cp.start()
            recv = pltpu.make_async_remote_copy(
                src_ref=ins[k], dst_ref=rows(1 - c), send_sem=send_sems.at[k], recv_sem=recv_sems.at[k],
                device_id=(x, y, 1 - c), device_id_type=MESH_ID)
            started.append((loc, cp, recv))
        for loc, cp, recv in started:
            loc.turn()
        for loc, cp, recv in started:
            recv.wait_recv()
        for loc, cp, recv in started:
            cp.wait_send()
            loc.wait()

    return pl.pallas_call(
        body, name="grad_pair_share",
        in_specs=[ANY] * n, out_specs=[ANY] * n,
        out_shape=[jax.ShapeDtypeStruct((2 * h.shape[0], h.shape[1]), F32) for h in halves],
        scratch_shapes=[pltpu.VMEM(h.shape, F32) for h in halves] + [pltpu.SemaphoreType.DMA((n,))] * 4,
        compiler_params=_comm_params(sum(_nbytes(h.shape, F32) for h in halves)),
    )(*halves)


def _row_block(rows, cols, n_arrays):
    br = rows
    while br % 16 == 0 and 2 * n_arrays * br * cols * 4 > (16 << 20):
        br //= 2
    return br


def _add2(a, b, out_dtype, name):
    rows, cols = a.shape
    br = _row_block(rows, cols, 3)

    def body(a_ref, b_ref, o_ref):
        o_ref[...] = (a_ref[...] + b_ref[...]).astype(out_dtype)

    spec = pl.BlockSpec((br, cols), lambda i: (i, 0))
    return pl.pallas_call(body, name=name, grid=(rows // br,), in_specs=[spec, spec], out_specs=spec,
                          out_shape=jax.ShapeDtypeStruct(a.shape, out_dtype),
                          compiler_params=_cparams(3 * br * cols * 4, 1))(a, b)


def _sum_slots(a, name):
    n, rows, cols = a.shape
    br = _row_block(rows, cols, n + 1)

    def body(a_ref, o_ref):
        acc = a_ref[0].astype(F32)
        for s in range(1, n):
            acc = acc + a_ref[s].astype(F32)
        o_ref[...] = acc

    return pl.pallas_call(body, name=name, grid=(rows // br,),
                          in_specs=[pl.BlockSpec((n, br, cols), lambda i: (0, i, 0))],
                          out_specs=pl.BlockSpec((br, cols), lambda i: (i, 0)),
                          out_shape=jax.ShapeDtypeStruct((rows, cols), F32),
                          compiler_params=_cparams((n + 1) * br * cols * 4, 1))(a)


def _adamw_math(w, g, m, v):
    m = ADAM_B1 * m + (1.0 - ADAM_B1) * g
    v = ADAM_B2 * v + (1.0 - ADAM_B2) * (g * g)
    m_hat = m / (1.0 - ADAM_B1 ** ADAM_STEP)
    v_hat = v / (1.0 - ADAM_B2 ** ADAM_STEP)
    delta = -ADAM_LR * (m_hat / (jnp.sqrt(v_hat) + ADAM_EPS) + ADAM_WD * w)
    return delta, m, v


def _adamw(g, w, m, v, name):
    rows, cols = g.shape
    br = _row_block(rows, cols, 7)

    def body(g_ref, w_ref, m_ref, v_ref, d_ref, nm_ref, nv_ref):
        d_ref[...], nm_ref[...], nv_ref[...] = _adamw_math(w_ref[...], g_ref[...], m_ref[...], v_ref[...])

    spec = pl.BlockSpec((br, cols), lambda i: (i, 0))
    return pl.pallas_call(body, name=name, grid=(rows // br,), in_specs=[spec] * 4, out_specs=[spec] * 3,
                          out_shape=[jax.ShapeDtypeStruct(g.shape, F32)] * 3,
                          compiler_params=_cparams(7 * br * cols * 4, 1))(g, w, m, v)


ROW_FINAL_G, ROW_PE_G = 0, 1
ROW_CONV_B, ROW_CN_G, ROW_CN_B, ROW_B_PW2 = 8, 9, 10, 11
ROW_LN_G = 16
ROW_ONORM_G, ROW_LB = 24, 25
ROW_CONV_W = 32
SMALL = ["ln_g", "conv_b", "cnorm_g", "cnorm_b", "b_pw2", "onorm_g", "pe_norm_g", "final_g"]
SMALL_ROW = dict(ln_g=ROW_LN_G, conv_b=ROW_CONV_B, cnorm_g=ROW_CN_G, cnorm_b=ROW_CN_B, b_pw2=ROW_B_PW2,
                 onorm_g=ROW_ONORM_G, pe_norm_g=ROW_PE_G, final_g=ROW_FINAL_G)


def _adamw_small(vsum, gcw, lb_logits, params):
    names = SMALL + ["lb_logits", "conv_w"]
    flat = [t for nm in names for t in params[nm]]

    def body(*refs):
        vs_ref, gcw_ref, lbl_ref = refs[:3]
        ins = refs[3:3 + 3 * len(names)]
        outs = refs[3 + 3 * len(names):]
        for q, nm in enumerate(names):
            w_ref, m_ref, v_ref = ins[3 * q:3 * q + 3]
            g_ref, d_ref, nm_ref, nv_ref = outs[4 * q:4 * q + 4]
            if nm == "conv_w":
                g = gcw_ref[...]
            elif nm == "lb_logits":
                lb = _softmax_row0(lbl_ref[...])
                g0 = vs_ref[ROW_LB:ROW_LB + 1, :] * lb * (1.0 - lb)
                g = jnp.concatenate([g0, -g0], axis=0)
            else:
                g = vs_ref[SMALL_ROW[nm]:SMALL_ROW[nm] + 1, :]
            g_ref[...] = g
            d_ref[...], nm_ref[...], nv_ref[...] = _adamw_math(w_ref[...], g, m_ref[...], v_ref[...])

    out_shape = [jax.ShapeDtypeStruct(params[nm][0].shape, F32) for nm in names for _ in range(4)]
    outs = pl.pallas_call(body, name="adamw_small", out_shape=out_shape)(vsum, gcw, lb_logits, *flat)
    return {nm: tuple(outs[4 * q:4 * q + 4]) for q, nm in enumerate(names)}


def _tile(T, want):
    return min(T, want)


def kernel(x, p, ln_g, w_in, conv_w, conv_b, cnorm_g, cnorm_b, w_pw2, b_pw2, lb_logits, onorm_g, w_out, pe_norm_g, w_pg, w_pp, final_g, loss_target, m_ln_g, m_w_in, m_conv_w, m_conv_b, m_cnorm_g, m_cnorm_b, m_w_pw2, m_b_pw2, m_lb_logits, m_onorm_g, m_w_out, m_pe_norm_g, m_w_pg, m_w_pp, m_final_g, v_ln_g, v_w_in, v_conv_w, v_conv_b, v_cnorm_g, v_cnorm_b, v_w_pw2, v_b_pw2, v_lb_logits, v_onorm_g, v_w_out, v_pe_norm_g, v_w_pg, v_w_pp, v_final_g):
    given = dict(locals())
    x2, p2, tgt = x[0], p[0, 0], loss_target[0]
    T = x2.shape[0]
    fin_g = final_g.reshape(1, D)

    conv_w_pad = jnp.pad(conv_w[0], ((0, HALO - CONV_K), (0, 0)))
    w_in_f, w_pw2_f, w_out_f, w_pg_f, w_pp_f, conv_w_f = _all_gather_shards(
        [w_in[0].astype(_MXU), w_pw2[0].astype(_MXU), w_out[0].astype(_MXU), w_pg[0].astype(_MXU),
         w_pp[0].astype(_MXU), conv_w_pad],
        [1, 0, 0, 0, 1, 1])

    z, u = _inproj_fwd(x2, ln_g, w_in_f, _tile(T, 512))
    yc = _conv_fwd(z, conv_w_f, conv_b, cnorm_g, cnorm_b, w_pw2_f, b_pw2, _tile(T, 256))
    o_raw, yh, s_chunks = _hgrn_fwd(z, lb_logits, onorm_g, _tile(T, 512))
    dyc, dyh, dh, n2, ds, dpe, dhb, pb, vec_tail, loss_part = _tail(
        x2, yc, yh, p2, tgt, w_out_f, w_pg_f, w_pp_f, pe_norm_g, fin_g, _tile(T, 256))
    dzh, vec_hgrn = _hgrn_bwd(z, lb_logits, onorm_g, o_raw, dyh, s_chunks, _tile(T, 512))
    dzc, a_act, dy2, vec_conv, g_conv_w = _conv_bwd(z, dyc, conv_w_f, conv_b, cnorm_g, cnorm_b, w_pw2_f, b_pw2, _tile(T, 256))
    grad_x, vec_in = _inproj_bwd_x(x2, ln_g, dzc, dzh, w_in_f, dh, _tile(T, 512))
    tk = _tile(T, 512)
    g_w_in = _inproj_bwd_w(u, dzc, dzh, tk)
    g_w_pw2 = _tn_matmul(a_act, dy2, tk, "grad_w_pw2")
    g_w_out_c = _tn_matmul(yc, dhb, tk, "grad_w_out_conv")
    g_w_out_h = _tn_matmul(yh, dhb, tk, "grad_w_out_hgrn")
    g_w_pg = _tn_matmul(n2, ds, tk, "grad_w_pg")
    g_w_pp = _tn_matmul(pb, dpe, tk, "grad_w_pp")

    big = ["w_in", "w_pw2", "w_out", "w_pg", "w_pp"]
    slabs = [
        _Slab([g_w_in], lambda s: (0, 0, s * (NPART * D // N_CHIPS)), (D, NPART * D // N_CHIPS)),
        _Slab([g_w_pw2], lambda s: (0, s * (D // N_CHIPS), 0), (D // N_CHIPS, D)),
        _Slab([g_w_out_c, g_w_out_h], lambda s: (s // 2, (s % 2) * (D // 2), 0), (D // 2, D)),
        _Slab([g_w_pg], lambda s: (0, s * (D // N_CHIPS), 0), (D // N_CHIPS, D)),
        _Slab([g_w_pp], lambda s: (0, 0, s * (D // N_CHIPS)), (PLE, D // N_CHIPS)),
    ]
    mine, got = _pair_exchange(slabs)
    partial = [_add2(a.reshape(-1, a.shape[-1]), b.reshape(-1, b.shape[-1]), _WIRE, "pair_sum_" + nm).reshape(a.shape)
               for a, b, nm in zip(mine, got, big)]
    vec = jnp.concatenate([vec_tail, vec_conv, vec_in, vec_hgrn, g_conv_w], axis=0)
    slots, vec_slots = _chip_exchange(partial, vec)
    halves = [_sum_slots(s, "chip_sum_" + nm) for s, nm in zip(slots, big)]
    vsum = _sum_slots(vec_slots, "vec_sum")
    grads_big = _pair_share(halves)

    out = {}
    for nm, g in zip(big, grads_big):
        w2, m2, v2 = given[nm][0], given["m_" + nm][0], given["v_" + nm][0]
        d, nm_, nv_ = _adamw(g, w2, m2, v2, "adamw_" + nm)
        out[nm] = tuple(t[None] for t in (g, d, nm_, nv_))
    chip = 2 * lax.axis_index("x") + lax.axis_index("y")
    gcw = lax.dynamic_slice(vsum, (ROW_CONV_W, chip * (D // N_CHIPS)), (CONV_K, D // N_CHIPS))
    params = {nm: (given[nm].reshape(-1, D), given["m_" + nm].reshape(-1, D), given["v_" + nm].reshape(-1, D))
              for nm in SMALL + ["lb_logits"]}
    params["conv_w"] = (conv_w[0], m_conv_w[0], v_conv_w[0])
    small = _adamw_small(vsum, gcw, lb_logits, params)
    for nm, ts in small.items():
        out[nm] = tuple(t.reshape(given[nm].shape) for t in ts)

    loss = lax.psum(loss_part[0, 0], ("x", "y", "c"))
    order = ["ln_g", "w_in", "conv_w", "conv_b", "cnorm_g", "cnorm_b", "w_pw2", "b_pw2", "lb_logits", "onorm_g",
             "w_out", "pe_norm_g", "w_pg", "w_pp", "final_g"]
    return (loss, grad_x[None], *[out[nm][0] for nm in order], *[out[nm][1] for nm in order],
            *[out[nm][2] for nm in order], *[out[nm][3] for nm in order])
```

```python
import functools

import jax
import jax.numpy as jnp
from jax import lax
from jax.experimental import pallas as pl
from jax.experimental.pallas import tpu as pltpu

F32 = jnp.float32
BF16 = jnp.bfloat16
_MXU = jnp.bfloat16
_WIRE = jnp.bfloat16

D = 1024
NPART = 7
PLE = 256
HEADS = 8
HD = 128
CHUNK = 64
CONV_K = 31
HALO = 32
EPS = 1e-6
N_CHIPS = 4
N_DEV = 8
HB = 8
VEC_ROWS = 64

ADAM_LR = 0.001
ADAM_B1 = 0.9
ADAM_B2 = 0.999
ADAM_EPS = 1e-08
ADAM_WD = 0.01
ADAM_STEP = 10

V7X_VMEM_LIMIT = 60000 * 1024
MESH_ID = pl.DeviceIdType.MESH
ANY = pl.BlockSpec(memory_space=pl.ANY)


def _cparams(block_bytes, n_grid_dims):
    limit = min(V7X_VMEM_LIMIT, 2 * block_bytes + (24 << 20))
    return pltpu.CompilerParams(vmem_limit_bytes=int(limit), dimension_semantics=("arbitrary",) * n_grid_dims)


def _nbytes(shape, dtype):
    n = 1
    for s in shape:
        n *= s
    return n * jnp.dtype(dtype).itemsize


def _dot(a, b):
    return jnp.dot(a.astype(_MXU), b.astype(_MXU), preferred_element_type=F32)


def _dot_nt(a, b):
    return lax.dot_general(a.astype(_MXU), b.astype(_MXU), (((1,), (1,)), ((), ())), preferred_element_type=F32)


def _dot_tn(a, b):
    return lax.dot_general(a.astype(_MXU), b.astype(_MXU), (((0,), (0,)), ((), ())), preferred_element_type=F32)


def _tri_dot(tri_bf, x):
    x1 = x.astype(BF16)
    r1 = x - x1.astype(F32)
    x2 = r1.astype(BF16)
    x3 = (r1 - x2.astype(F32)).astype(BF16)
    d = lambda t: jnp.dot(tri_bf, t, preferred_element_type=F32)
    return d(x1) + d(x2) + d(x3)


def _split2(x):
    hi = x.astype(BF16)
    return hi, (x - hi.astype(F32)).astype(BF16)


def _dot3(dims, a, b):
    d = lambda p, q: lax.dot_general(p, q, (dims, ((), ())), preferred_element_type=F32)
    return d(a[0], b[0]) + d(a[0], b[1]) + d(a[1], b[0])


def _sigmoid(x):
    return jax.nn.sigmoid(x)


def _mean_lanes(x):
    return jnp.mean(x, axis=-1, keepdims=True)


def _sum_rows(x):
    return jnp.sum(x, axis=0, keepdims=True)


def _group_ln(y):
    yn, rs = [], []
    for g in range(D // HD):
        blk = y[:, g * HD:(g + 1) * HD]
        xc = blk - _mean_lanes(blk)
        r = lax.rsqrt(_mean_lanes(xc * xc) + EPS)
        yn.append(xc * r)
        rs.append(jnp.broadcast_to(r, blk.shape))
    return jnp.concatenate(yn, axis=1), jnp.concatenate(rs, axis=1)


def _group_ln_bwd(dyn, yn, rstd):
    out = []
    for g in range(D // HD):
        sl = slice(g * HD, (g + 1) * HD)
        d, n = dyn[:, sl], yn[:, sl]
        out.append(rstd[:, sl] * (d - _mean_lanes(d) - n * _mean_lanes(d * n)))
    return jnp.concatenate(out, axis=1)


def _conv_taps(ext):
    return lambda j: pltpu.roll(ext, CONV_K - 1 - j, axis=0)[HALO:, :] if j < CONV_K - 1 else ext[HALO:, :]


def _head_means(x, hb, fn=lambda m: m):
    return jnp.concatenate([jnp.broadcast_to(fn(_mean_lanes(x[:, hh * HD:(hh + 1) * HD])), (x.shape[0], HD))
                            for hh in range(hb)], axis=1)


def _head_rsqrt_mean(x, hb):
    return _head_means(x, hb, lambda m: lax.rsqrt(m + EPS))


def _softmax_row0(lbl):
    m = jnp.max(lbl, axis=0, keepdims=True)
    e = jnp.exp(lbl - m)
    return e[0:1, :] / jnp.sum(e, axis=0, keepdims=True)


def _inproj_fwd(x, ln_g, w_in, tT):
    T = x.shape[0]

    def body(x_ref, g_ref, w_ref, z_ref, u_ref, u_scr):
        @pl.when(pl.program_id(1) == 0)
        def _():
            xv = x_ref[...]
            r = lax.rsqrt(_mean_lanes(xv * xv) + EPS)
            u = (xv * r * g_ref[...]).astype(_MXU)
            u_scr[...] = u
            u_ref[...] = u
        z_ref[...] = jnp.dot(u_scr[...], w_ref[...], preferred_element_type=F32)

    blk = _nbytes((tT, D), F32) * 2 + _nbytes((D, D), _MXU) + _nbytes((tT, D), _MXU) * 2
    return pl.pallas_call(
        body, name="inproj_fwd", grid=(T // tT, NPART),
        in_specs=[pl.BlockSpec((tT, D), lambda i, j: (i, 0)), pl.BlockSpec((1, D), lambda i, j: (0, 0)),
                  pl.BlockSpec((D, D), lambda i, j: (0, j))],
        out_specs=[pl.BlockSpec((tT, D), lambda i, j: (i, j)), pl.BlockSpec((tT, D), lambda i, j: (i, 0))],
        out_shape=[jax.ShapeDtypeStruct((T, NPART * D), F32), jax.ShapeDtypeStruct((T, D), _MXU)],
        scratch_shapes=[pltpu.VMEM((tT, D), _MXU)],
        compiler_params=_cparams(blk, 2),
    )(x, ln_g, w_in)


def _conv_fwd(z, conv_w, conv_b, cn_g, cn_b, w_pw2, b_pw2, tT):
    T = z.shape[0]

    def body(cv_ref, cg_ref, ct_ref, cw_ref, cb_ref, ng_ref, nb_ref, wp_ref, bp_ref, yc_ref, halo):
        @pl.when(pl.program_id(0) == 0)
        def _():
            halo[...] = jnp.zeros_like(halo)
        v = cv_ref[...] * _sigmoid(cg_ref[...])
        tap = _conv_taps(jnp.concatenate([halo[...], v], axis=0))
        halo[...] = v[tT - HALO:, :]
        cw = cw_ref[...]
        y1 = cb_ref[...] + tap(0) * cw[0:1, :]
        for j in range(1, CONV_K):
            y1 = y1 + tap(j) * cw[j:j + 1, :]
        yn, _ = _group_ln(y1)
        apre = yn * ng_ref[...] + nb_ref[...]
        a = apre * _sigmoid(apre)
        y2 = _dot(a, wp_ref[...]) + bp_ref[...]
        ct = ct_ref[...]
        yc_ref[...] = (y2 * (ct * _sigmoid(ct))).astype(_MXU)

    part = lambda p: pl.BlockSpec((tT, D), lambda i: (i, p))
    row = pl.BlockSpec((1, D), lambda i: (0, 0))
    blk = 3 * _nbytes((tT, D), F32) + _nbytes((D, D), _MXU) + _nbytes((tT, D), _MXU) + 12 * _nbytes((tT + HALO, D), F32)
    return pl.pallas_call(
        body, name="conv_fwd", grid=(T // tT,),
        in_specs=[part(0), part(1), part(2), pl.BlockSpec((HALO, D), lambda i: (0, 0)), row, row, row,
                  pl.BlockSpec((D, D), lambda i: (0, 0)), row],
        out_specs=pl.BlockSpec((tT, D), lambda i: (i, 0)),
        out_shape=jax.ShapeDtypeStruct((T, D), _MXU),
        scratch_shapes=[pltpu.VMEM((HALO, D), F32)],
        compiler_params=_cparams(blk, 1),
    )(z, z, z, conv_w, conv_b, cn_g, cn_b, w_pw2, b_pw2)


def _hgrn_gates(lb, hq, hf):
    sq = _sigmoid(hq)
    sg = _sigmoid(hf)
    f = lb + (1.0 - lb) * sg
    return sq, sg, f, hq * sq, (1.0 - lb) * (1.0 - sg), jnp.log(f)


def _chunk_decays(lf, q, k):
    r = lax.broadcasted_iota(jnp.int32, (CHUNK, CHUNK), 0)
    c = lax.broadcasted_iota(jnp.int32, (CHUNK, CHUNK), 1)
    b = _tri_dot((r >= c).astype(BF16), lf)
    bm = b[CHUNK // 2 - 1:CHUNK // 2, :]
    bl = b[CHUNK - 1:CHUNK, :]
    eb = jnp.exp(b)
    eqm = jnp.exp(b - bm)
    ekm = jnp.exp(bm - b)
    ekd = jnp.exp(bl - b)
    return dict(causal=r >= c, eb=eb, eqm=eqm, ekm=ekm, ekd=ekd, ebl=jnp.exp(bl),
                qd=q * eb, qm=q * eqm, km=k * ekm, kd=k * ekd)


def _hgrn_fwd(z, lb_logits, onorm_g, tT, hb):
    T = z.shape[0]
    nc = tT // CHUNK
    w = hb * HD

    def body(lbl_ref, og_ref, hq_ref, hf_ref, hi_ref, hg_ref, o_ref, yh_ref, sc_ref, st):
        @pl.when(pl.program_id(1) == 0)
        def _():
            st[...] = jnp.zeros_like(st)
        lb_all = _softmax_row0(lbl_ref[...])
        og_all = og_ref[...]

        def chunk(c, carry):
            sl = pl.ds(pl.multiple_of(c * CHUNK, CHUNK), CHUNK)
            lanes = [slice(hh * HD, (hh + 1) * HD) for hh in range(hb)]
            heads = lambda fn: [fn(hh, ln) for hh, ln in enumerate(lanes)]
            hg, v = hg_ref[sl, :], hi_ref[sl, :]
            _, _, _, q, k, lf = _hgrn_gates(lb_all, hq_ref[sl, :], hf_ref[sl, :])
            dc = _chunk_decays(lf, q, k)
            s_t = heads(lambda hh, ln: st[hh])
            a = heads(lambda hh, ln: jnp.where(dc["causal"], _dot_nt(dc["qm"][:, ln], dc["km"][:, ln]), 0.0))
            o_inter = heads(lambda hh, ln: _dot_nt(dc["qd"][:, ln], s_t[hh]))
            kv = heads(lambda hh, ln: _dot_tn(v[:, ln], dc["kd"][:, ln]))
            o_intra = heads(lambda hh, ln: _dot(a[hh], v[:, ln]))
            for hh, ln in enumerate(lanes):
                sc_ref[hh, c] = s_t[hh]
                st[hh] = s_t[hh] * dc["ebl"][:, ln] + kv[hh]
            o = jnp.concatenate([o_inter[hh] + o_intra[hh] for hh in range(hb)], axis=1)
            o_ref[sl, :] = o
            n = o * _head_rsqrt_mean(o * o, hb)
            yh_ref[sl, :] = ((n * og_all) * (hg * _sigmoid(hg))).astype(_MXU)
            return carry

        lax.fori_loop(0, nc, chunk, 0)

    zpart = lambda p: pl.BlockSpec((tT, w), lambda h, i: (i, p * (HEADS // hb) + h))
    blk = 6 * _nbytes((tT, w), F32) + _nbytes((hb, nc, HD, HD), F32)
    return pl.pallas_call(
        body, name="hgrn_fwd", grid=(HEADS // hb, T // tT),
        in_specs=[pl.BlockSpec((2, w), lambda h, i: (0, h)), pl.BlockSpec((1, w), lambda h, i: (0, h)),
                  zpart(3), zpart(4), zpart(5), zpart(6)],
        out_specs=[pl.BlockSpec((tT, w), lambda h, i: (i, h)), pl.BlockSpec((tT, w), lambda h, i: (i, h)),
                   pl.BlockSpec((hb, nc, HD, HD), lambda h, i: (h, i, 0, 0))],
        out_shape=[jax.ShapeDtypeStruct((T, D), F32), jax.ShapeDtypeStruct((T, D), _MXU),
                   jax.ShapeDtypeStruct((HEADS, T // CHUNK, HD, HD), F32)],
        scratch_shapes=[pltpu.VMEM((hb, HD, HD), F32)],
        compiler_params=_cparams(blk, 2),
    )(lb_logits, onorm_g, z, z, z, z)


def _hgrn_bwd(z, lb_logits, onorm_g, o_raw, dyh, s_chunks, tT, hb):
    T = z.shape[0]
    nc = tT // CHUNK
    nI = T // tT
    w = hb * HD

    def body(lbl_ref, og_ref, hq_ref, hf_ref, hi_ref, hg_ref, o_ref, dy_ref, sc_ref, dz_ref, vec_ref, dst):
        @pl.when(pl.program_id(1) == 0)
        def _():
            dst[...] = jnp.zeros_like(dst)
            vec_ref[...] = jnp.zeros_like(vec_ref)
        lb_all = _softmax_row0(lbl_ref[...])
        og_all = og_ref[...]
        last_row = lax.broadcasted_iota(jnp.int32, (CHUNK, w), 0) == CHUNK - 1
        r64 = lax.broadcasted_iota(jnp.int32, (CHUNK, CHUNK), 0)
        c64 = lax.broadcasted_iota(jnp.int32, (CHUNK, CHUNK), 1)
        upper = (c64 >= r64).astype(BF16)
        lanes = [slice(hh * HD, (hh + 1) * HD) for hh in range(hb)]
        heads = lambda fn: [fn(hh, ln) for hh, ln in enumerate(lanes)]
        wide = lambda parts: jnp.concatenate(parts, axis=1)

        def chunk(cc, carry):
            c = nc - 1 - cc
            sl = pl.ds(pl.multiple_of(c * CHUNK, CHUNK), CHUNK)
            hq, hg, v = hq_ref[sl, :], hg_ref[sl, :], hi_ref[sl, :]
            sq, sg, f, q, k, lf = _hgrn_gates(lb_all, hq, hf_ref[sl, :])
            dc = _chunk_decays(lf, q, k)
            s_t = heads(lambda hh, ln: sc_ref[hh, c])
            ds_t = heads(lambda hh, ln: dst[hh])
            o, dy = o_ref[sl, :], dy_ref[sl, :]
            r = _head_rsqrt_mean(o * o, hb)
            n = o * r
            sgg = _sigmoid(hg)
            silu_g = hg * sgg
            dhg = dy * (n * og_all) * (sgg * (1.0 + hg * (1.0 - sgg)))
            dn = dy * og_all * silu_g
            g_og = _sum_rows(dy * n * silu_g)
            do = r * (dn - n * _head_means(dn * n, hb))
            a = heads(lambda hh, ln: jnp.where(dc["causal"], _dot_nt(dc["qm"][:, ln], dc["km"][:, ln]), 0.0))
            dam = heads(lambda hh, ln: jnp.where(dc["causal"], _dot_nt(do[:, ln], v[:, ln]), 0.0))
            dqd = wide(heads(lambda hh, ln: _dot(do[:, ln], s_t[hh])))
            dkd = wide(heads(lambda hh, ln: _dot(v[:, ln], ds_t[hh])))
            dv_inter = heads(lambda hh, ln: _dot_nt(dc["kd"][:, ln], ds_t[hh]))
            dqs = heads(lambda hh, ln: _dot_tn(do[:, ln], dc["qd"][:, ln]))
            dv = wide(heads(lambda hh, ln: _dot_tn(a[hh], do[:, ln]) + dv_inter[hh]))
            dam2 = [_split2(t) for t in dam]
            km2, qm2 = _split2(dc["km"]), _split2(dc["qm"])
            dqm = wide(heads(lambda hh, ln: _dot3(((1,), (0,)), dam2[hh], (km2[0][:, ln], km2[1][:, ln]))))
            dkm = wide(heads(lambda hh, ln: _dot3(((0,), (0,)), dam2[hh], (qm2[0][:, ln], qm2[1][:, ln]))))
            debl = wide(heads(lambda hh, ln: _sum_rows(ds_t[hh] * s_t[hh])))
            for hh, ln in enumerate(lanes):
                dst[hh] = ds_t[hh] * dc["ebl"][:, ln] + dqs[hh]
            dq = dqd * dc["eb"] + dqm * dc["eqm"]
            dk = dkm * dc["ekm"] + dkd * dc["ekd"]
            dbl = _sum_rows(dkd * dc["kd"]) + debl * dc["ebl"]
            db = dq * q - dk * k + jnp.where(last_row, dbl, 0.0)
            dlf = _tri_dot(upper, db)
            dfk = dlf / f - dk
            dz_ref[0, sl, :] = (dq * (sq * (1.0 + hq * (1.0 - sq)))).astype(_MXU)
            dz_ref[1, sl, :] = (dfk * ((1.0 - lb_all) * sg * (1.0 - sg))).astype(_MXU)
            dz_ref[2, sl, :] = dv.astype(_MXU)
            dz_ref[3, sl, :] = dhg.astype(_MXU)
            vec_ref[0:1, :] += g_og
            vec_ref[1:2, :] += _sum_rows(dfk * (1.0 - sg))
            return carry

        lax.fori_loop(0, nc, chunk, 0)

    zpart = lambda p: pl.BlockSpec((tT, w), lambda h, i: (nI - 1 - i, p * (HEADS // hb) + h))
    act = pl.BlockSpec((tT, w), lambda h, i: (nI - 1 - i, h))
    blk = 6 * _nbytes((tT, w), F32) + _nbytes((hb, nc, HD, HD), F32) + 4 * _nbytes((tT, w), _MXU)
    return pl.pallas_call(
        body, name="hgrn_bwd", grid=(HEADS // hb, nI),
        in_specs=[pl.BlockSpec((2, w), lambda h, i: (0, h)), pl.BlockSpec((1, w), lambda h, i: (0, h)),
                  zpart(3), zpart(4), zpart(5), zpart(6), act, act,
                  pl.BlockSpec((hb, nc, HD, HD), lambda h, i: (h, nI - 1 - i, 0, 0))],
        out_specs=[pl.BlockSpec((4, tT, w), lambda h, i: (0, nI - 1 - i, h)),
                   pl.BlockSpec((8, w), lambda h, i: (0, h))],
        out_shape=[jax.ShapeDtypeStruct((4, T, D), _MXU), jax.ShapeDtypeStruct((8, D), F32)],
        scratch_shapes=[pltpu.VMEM((hb, HD, HD), F32)],
        compiler_params=_cparams(blk, 2),
    )(lb_logits, onorm_g, z, z, z, z, o_raw, dyh, s_chunks)


def _tail(x, yc, yh, p, target, w_out, w_pg, w_pp, pe_g, fin_g, tT):
    T = x.shape[0]

    def body(x_ref, yc_ref, yh_ref, p_ref, t_ref, wo_ref, wg_ref, wp_ref, pg_ref, fg_ref,
             dyc_ref, dyh_ref, dh_ref, n2_ref, ds_ref, dpe_ref, dhb_ref, pb_ref, vec_ref, loss_ref):
        @pl.when(pl.program_id(0) == 0)
        def _():
            vec_ref[...] = jnp.zeros_like(vec_ref)
            loss_ref[...] = jnp.zeros_like(loss_ref)
        wo_c, wo_h = wo_ref[0:D, :], wo_ref[D:2 * D, :]
        h = x_ref[...] + _dot(yc_ref[...], wo_c) + _dot(yh_ref[...], wo_h)
        pb = p_ref[...].astype(_MXU)
        pe = _dot(pb, wp_ref[...])
        r2 = lax.rsqrt(_mean_lanes(h * h) + EPS)
        hn = h * r2
        n2 = (hn * pg_ref[...]).astype(_MXU)
        gate = _sigmoid(_dot(n2, wg_ref[...]))
        h2 = h + gate * pe
        r3 = lax.rsqrt(_mean_lanes(h2 * h2) + EPS)
        h2n = h2 * r3
        err = h2n * fg_ref[...] - t_ref[...]
        loss_ref[...] += 0.5 * jnp.sum(_mean_lanes(err * err))
        dout = err * (1.0 / D)
        vec_ref[0:1, :] += _sum_rows(dout * h2n)
        dn3 = dout * fg_ref[...]
        dh2 = r3 * (dn3 - h2n * _mean_lanes(dn3 * h2n))
        ds = (dh2 * pe * gate * (1.0 - gate)).astype(_MXU)
        dn2 = _dot_nt(ds, wg_ref[...])
        vec_ref[1:2, :] += _sum_rows(dn2 * hn)
        dnn = dn2 * pg_ref[...]
        dh = dh2 + r2 * (dnn - hn * _mean_lanes(dnn * hn))
        dhb = dh.astype(_MXU)
        dyc_ref[...] = _dot_nt(dhb, wo_c)
        dyh_ref[...] = _dot_nt(dhb, wo_h)
        dh_ref[...] = dh
        n2_ref[...] = n2
        ds_ref[...] = ds
        dpe_ref[...] = (dh2 * gate).astype(_MXU)
        dhb_ref[...] = dhb
        pb_ref[...] = pb

    tok = lambda w: pl.BlockSpec((tT, w), lambda i: (i, 0))
    full = lambda r, c: pl.BlockSpec((r, c), lambda i: (0, 0))
    tokshape = lambda w, dt: jax.ShapeDtypeStruct((T, w), dt)
    blk = (5 * _nbytes((tT, D), F32) + 7 * _nbytes((tT, D), _MXU) + _nbytes((4 * D + PLE, D), _MXU)
           + 12 * _nbytes((tT, D), F32))
    return pl.pallas_call(
        body, name="tail_fwd_bwd", grid=(T // tT,),
        in_specs=[tok(D), tok(D), tok(D), tok(PLE), tok(D), full(2 * D, D), full(D, D), full(PLE, D), full(1, D), full(1, D)],
        out_specs=[tok(D), tok(D), tok(D), tok(D), tok(D), tok(D), tok(D), tok(PLE), full(8, D), full(8, HD)],
        out_shape=[tokshape(D, F32), tokshape(D, F32), tokshape(D, F32), tokshape(D, _MXU), tokshape(D, _MXU),
                   tokshape(D, _MXU), tokshape(D, _MXU), tokshape(PLE, _MXU),
                   jax.ShapeDtypeStruct((8, D), F32), jax.ShapeDtypeStruct((8, HD), F32)],
        compiler_params=_cparams(blk, 1),
    )(x, yc, yh, p, target, w_out, w_pg, w_pp, pe_g, fin_g)


def _conv_bwd(z, dyc, conv_w, conv_b, cn_g, cn_b, w_pw2, b_pw2, tT):
    T = z.shape[0]
    nI = T // tT
    hb = tT // HALO

    def body(cv_ref, cg_ref, ct_ref, hv_ref, hg_ref, dyc_ref, cw_ref, cb_ref, ng_ref, nb_ref, wp_ref, bp_ref,
             dz_ref, a_ref, dy2_ref, vec_ref, gcw_ref, dhalo):
        i = pl.program_id(0)

        @pl.when(i == 0)
        def _():
            dhalo[...] = jnp.zeros_like(dhalo)
            vec_ref[...] = jnp.zeros_like(vec_ref)
            gcw_ref[...] = jnp.zeros_like(gcw_ref)
        cv, cg, ct = cv_ref[...], cg_ref[...], ct_ref[...]
        sg = _sigmoid(cg)
        v = cv * sg
        has_hist = (i < nI - 1).astype(F32)
        vh = hv_ref[...] * _sigmoid(hg_ref[...]) * has_hist
        tap = _conv_taps(jnp.concatenate([vh, v], axis=0))
        cw = cw_ref[...]
        y1 = cb_ref[...] + tap(0) * cw[0:1, :]
        for j in range(1, CONV_K):
            y1 = y1 + tap(j) * cw[j:j + 1, :]
        yn, rstd = _group_ln(y1)
        apre = yn * ng_ref[...] + nb_ref[...]
        sa = _sigmoid(apre)
        a = (apre * sa).astype(_MXU)
        y2 = _dot(a, wp_ref[...]) + bp_ref[...]
        st = _sigmoid(ct)
        dyc_v = dyc_ref[...]
        dy2 = dyc_v * (ct * st)
        dy2b = dy2.astype(_MXU)
        da = _dot_nt(dy2b, wp_ref[...])
        dapre = da * (sa * (1.0 + apre * (1.0 - sa)))
        dy1 = _group_ln_bwd(dapre * ng_ref[...], yn, rstd)
        vec_ref[0:1, :] += _sum_rows(dy1)
        vec_ref[1:2, :] += _sum_rows(dapre * yn)
        vec_ref[2:3, :] += _sum_rows(dapre)
        vec_ref[3:4, :] += _sum_rows(dy2)
        for j in range(CONV_K):
            gcw_ref[j:j + 1, :] += _sum_rows(dy1 * tap(j))
        ext2 = jnp.concatenate([dy1, dhalo[...]], axis=0)
        dhalo[...] = dy1[0:HALO, :]
        dv = dy1 * cw[CONV_K - 1:CONV_K, :]
        for j in range(CONV_K - 1):
            dv = dv + pltpu.roll(ext2, tT + HALO - (CONV_K - 1 - j), axis=0)[0:tT, :] * cw[j:j + 1, :]
        dz_ref[0] = (dv * sg).astype(_MXU)
        dz_ref[1] = (dv * cv * sg * (1.0 - sg)).astype(_MXU)
        dz_ref[2] = (dyc_v * y2 * (st * (1.0 + ct * (1.0 - st)))).astype(_MXU)
        a_ref[...] = a
        dy2_ref[...] = dy2b

    part = lambda p: pl.BlockSpec((tT, D), lambda i: (nI - 1 - i, p))
    hist = lambda p: pl.BlockSpec((HALO, D), lambda i: (jnp.maximum((nI - 1 - i) * hb - 1, 0), p))
    tok = pl.BlockSpec((tT, D), lambda i: (nI - 1 - i, 0))
    row = pl.BlockSpec((1, D), lambda i: (0, 0))
    blk = (4 * _nbytes((tT, D), F32) + _nbytes((D, D), _MXU) + 5 * _nbytes((tT, D), _MXU)
           + 16 * _nbytes((tT + HALO, D), F32))
    return pl.pallas_call(
        body, name="conv_bwd", grid=(nI,),
        in_specs=[part(0), part(1), part(2), hist(0), hist(1), tok, pl.BlockSpec((HALO, D), lambda i: (0, 0)),
                  row, row, row, pl.BlockSpec((D, D), lambda i: (0, 0)), row],
        out_specs=[pl.BlockSpec((3, tT, D), lambda i: (0, nI - 1 - i, 0)), tok, tok,
                   pl.BlockSpec((8, D), lambda i: (0, 0)), pl.BlockSpec((HALO, D), lambda i: (0, 0))],
        out_shape=[jax.ShapeDtypeStruct((3, T, D), _MXU), jax.ShapeDtypeStruct((T, D), _MXU),
                   jax.ShapeDtypeStruct((T, D), _MXU), jax.ShapeDtypeStruct((8, D), F32),
                   jax.ShapeDtypeStruct((HALO, D), F32)],
        scratch_shapes=[pltpu.VMEM((HALO, D), F32)],
        compiler_params=_cparams(blk, 1),
    )(z, z, z, z, z, dyc, conv_w, conv_b, cn_g, cn_b, w_pw2, b_pw2)


def _inproj_bwd_x(x, ln_g, dzc, dzh, w_in, dh, tT):
    T = x.shape[0]

    def body(x_ref, g_ref, dzc_ref, dzh_ref, w_ref, dh_ref, gx_ref, vec_ref, du):
        i, j = pl.program_id(0), pl.program_id(1)

        @pl.when(j == 0)
        def _():
            du[...] = jnp.zeros_like(du)

        @pl.when(jnp.logical_and(i == 0, j == 0))
        def _():
            vec_ref[...] = jnp.zeros_like(vec_ref)

        @pl.when(j < 3)
        def _():
            du[...] += _dot_nt(dzc_ref[0], w_ref[...])

        @pl.when(j >= 3)
        def _():
            du[...] += _dot_nt(dzh_ref[0], w_ref[...])

        @pl.when(j == NPART - 1)
        def _():
            xv = x_ref[...]
            r = lax.rsqrt(_mean_lanes(xv * xv) + EPS)
            xn = xv * r
            duv = du[...]
            vec_ref[0:1, :] += _sum_rows(duv * xn)
            dun = duv * g_ref[...]
            gx_ref[...] = dh_ref[...] + r * (dun - xn * _mean_lanes(dun * xn))

    tok = pl.BlockSpec((tT, D), lambda i, j: (i, 0))
    blk = 3 * _nbytes((tT, D), F32) + 2 * _nbytes((tT, D), _MXU) + _nbytes((D, D), _MXU) + 4 * _nbytes((tT, D), F32)
    return pl.pallas_call(
        body, name="inproj_bwd_x", grid=(T // tT, NPART),
        in_specs=[tok, pl.BlockSpec((1, D), lambda i, j: (0, 0)),
                  pl.BlockSpec((1, tT, D), lambda i, j: (jnp.minimum(j, 2), i, 0)),
                  pl.BlockSpec((1, tT, D), lambda i, j: (jnp.maximum(j - 3, 0), i, 0)),
                  pl.BlockSpec((D, D), lambda i, j: (0, j)), tok],
        out_specs=[tok, pl.BlockSpec((8, D), lambda i, j: (0, 0))],
        out_shape=[jax.ShapeDtypeStruct((T, D), F32), jax.ShapeDtypeStruct((8, D), F32)],
        scratch_shapes=[pltpu.VMEM((tT, D), F32)],
        compiler_params=_cparams(blk, 2),
    )(x, ln_g, dzc, dzh, w_in, dh)


def _inproj_bwd_w(u, dzc, dzh, tk):
    T = u.shape[0]
    nK = T // tk

    def body(u_ref, dzc_ref, dzh_ref, gw_ref):
        j, k = pl.program_id(0), pl.program_id(1)

        @pl.when(k == 0)
        def _():
            gw_ref[...] = jnp.zeros_like(gw_ref)

        @pl.when(j < 3)
        def _():
            gw_ref[...] += _dot_tn(u_ref[...], dzc_ref[0])

        @pl.when(j >= 3)
        def _():
            gw_ref[...] += _dot_tn(u_ref[...], dzh_ref[0])

    blk = 3 * _nbytes((tk, D), _MXU) + 2 * _nbytes((D, D), F32)
    return pl.pallas_call(
        body, name="inproj_bwd_w", grid=(NPART, nK),
        in_specs=[pl.BlockSpec((tk, D), lambda j, k: (k, 0)),
                  pl.BlockSpec((1, tk, D), lambda j, k: (jnp.minimum(j, 2), jnp.where(j < 3, k, nK - 1), 0)),
                  pl.BlockSpec((1, tk, D), lambda j, k: (jnp.maximum(j - 3, 0), jnp.where(j < 3, 0, k), 0))],
        out_specs=pl.BlockSpec((D, D), lambda j, k: (0, j)),
        out_shape=jax.ShapeDtypeStruct((D, NPART * D), F32),
        compiler_params=_cparams(blk, 2),
    )(u, dzc, dzh)


def _tn_matmul(a, b, tk, name):
    T, M = a.shape
    N = b.shape[1]

    def body(a_ref, b_ref, o_ref):
        @pl.when(pl.program_id(0) == 0)
        def _():
            o_ref[...] = jnp.zeros_like(o_ref)
        o_ref[...] += _dot_tn(a_ref[...], b_ref[...])

    blk = _nbytes((tk, M), _MXU) + _nbytes((tk, N), _MXU) + 2 * _nbytes((M, N), F32)
    return pl.pallas_call(
        body, name=name, grid=(T // tk,),
        in_specs=[pl.BlockSpec((tk, M), lambda k: (k, 0)), pl.BlockSpec((tk, N), lambda k: (k, 0))],
        out_specs=pl.BlockSpec((M, N), lambda k: (0, 0)),
        out_shape=jax.ShapeDtypeStruct((M, N), F32),
        compiler_params=_cparams(blk, 1),
    )(a, b)


def _place():
    return lax.axis_index("x"), lax.axis_index("y"), lax.axis_index("c")


def _flip(v, d):
    return 1 - v if d else v


CHIP_MOVES = [(1, 0), (0, 1), (1, 1)]
DEV_MOVES = [(dx, dy, dc) for dx in (0, 1) for dy in (0, 1) for dc in (0, 1)][1:]


def _shard_slice(ref, axis, size, s):
    start = pl.multiple_of(s * size, size)
    return ref.at[pl.ds(start, size), :] if axis == 0 else ref.at[:, pl.ds(start, size)]


class _Bounce:
    def __init__(self, src, buf, dst, sem_in, sem_out):
        self.load = pltpu.make_async_copy(src, buf, sem_in)
        self.store = pltpu.make_async_copy(buf, dst, sem_out)

    def start(self):
        self.load.start()

    def turn(self):
        self.load.wait()
        self.store.start()

    def wait(self):
        self.store.wait()


def _comm_params(scratch_bytes):
    return pltpu.CompilerParams(vmem_limit_bytes=int(min(V7X_VMEM_LIMIT, scratch_bytes + (8 << 20))))


def _all_gather_shards(shards, axes):
    n = len(shards)
    full_shapes = [tuple(d * (N_CHIPS if a == ax else 1) for a, d in enumerate(s.shape)) for s, ax in zip(shards, axes)]

    def body(*refs):
        ins, outs, bufs = refs[:n], refs[n:2 * n], refs[2 * n:3 * n]
        ici_send, ici_recv, d2d_send, d2d_recv, in_sems, out_sems = refs[3 * n:]
        x, y, c = _place()
        me = 2 * x + y

        def own_half(k, hc):
            half = shards[k].shape[0] // 2
            return ins[k].at[pl.ds(pl.multiple_of(hc * half, 16), half), :]

        def region(k, who, hc):
            rows, cols = shards[k].shape
            half = rows // 2
            if axes[k] == 0:
                return outs[k].at[pl.ds(pl.multiple_of(who * rows + hc * half, 16), half), :]
            return outs[k].at[pl.ds(pl.multiple_of(hc * half, 16), half), pl.ds(pl.multiple_of(who * cols, HD), cols)]

        def ici(k, j, who, hc):
            dx, dy = CHIP_MOVES[j]
            return pltpu.make_async_remote_copy(
                src_ref=own_half(k, hc), dst_ref=region(k, who, hc),
                send_sem=ici_send.at[3 * k + j], recv_sem=ici_recv.at[3 * k + j],
                device_id=(_flip(x, dx), _flip(y, dy), c), device_id_type=MESH_ID)

        def d2d(k, j, who, hc):
            return pltpu.make_async_remote_copy(
                src_ref=region(k, who, hc), dst_ref=region(k, who, hc),
                send_sem=d2d_send.at[3 * k + j], recv_sem=d2d_recv.at[3 * k + j],
                device_id=(x, y, 1 - c), device_id_type=MESH_ID)

        peer = lambda j: 2 * _flip(x, CHIP_MOVES[j][0]) + _flip(y, CHIP_MOVES[j][1])
        locs, sends = [], []
        for k in range(n):
            loc = _Bounce(ins[k], bufs[k], _shard_slice(outs[k], axes[k], shards[k].shape[axes[k]], me),
                          in_sems.at[k], out_sems.at[k])
            loc.start()
            locs.append(loc)
        for k in range(n):
            for j in range(3):
                cp = ici(k, j, me, c)
                cp.start()
                sends.append(cp)
        for loc in locs:
            loc.turn()
        for k in range(n):
            for j in range(3):
                ici(k, j, peer(j), c).wait_recv()
                fw = d2d(k, j, peer(j), c)
                fw.start()
                sends.append(fw)
        for k in range(n):
            for j in range(3):
                d2d(k, j, peer(j), 1 - c).wait_recv()
        for cp in sends:
            cp.wait_send()
        for loc in locs:
            loc.wait()

    sems = [pltpu.SemaphoreType.DMA((3 * n,))] * 4 + [pltpu.SemaphoreType.DMA((n,))] * 2
    return pl.pallas_call(
        body, name="gather_weights",
        in_specs=[ANY] * n, out_specs=[ANY] * n,
        out_shape=[jax.ShapeDtypeStruct(fs, s.dtype) for fs, s in zip(full_shapes, shards)],
        scratch_shapes=[pltpu.VMEM(s.shape, s.dtype) for s in shards] + sems,
        compiler_params=_comm_params(sum(_nbytes(s.shape, s.dtype) for s in shards)),
    )(*shards)


class _Slab:
    def __init__(self, arrays, pick, shard_shape):
        self.arrays = arrays
        self.pick = pick
        self.rows, self.cols = shard_shape
        self.half = self.rows // 2


def _pair_exchange(slabs):
    n = len(slabs)
    n_in = sum(len(sl.arrays) for sl in slabs)

    def body(*refs):
        ins = refs[:n_in]
        mine, got = refs[n_in:n_in + n], refs[n_in + n:n_in + 2 * n]
        bufs = refs[n_in + 2 * n:n_in + 3 * n]
        send_sems, recv_sems, in_sems, out_sems = refs[n_in + 3 * n:]
        x, y, c = _place()
        started = []
        base = 0
        for k, sl in enumerate(slabs):
            for s in range(N_CHIPS):
                ai, r0, c0 = sl.pick(s)
                src = ins[base + ai]

                def half(hc):
                    return src.at[pl.ds(pl.multiple_of(r0 + hc * sl.half, 8), sl.half), pl.ds(c0, sl.cols)]
                q = N_CHIPS * k + s
                loc = _Bounce(half(c), bufs[k].at[s], mine[k].at[s], in_sems.at[q], out_sems.at[q])
                loc.start()
                cp = pltpu.make_async_remote_copy(
                    src_ref=half(1 - c), dst_ref=got[k].at[s], send_sem=send_sems.at[q], recv_sem=recv_sems.at[q],
                    device_id=(x, y, 1 - c), device_id_type=MESH_ID)
                cp.start()
                started.append((loc, cp))
            base += len(sl.arrays)
        for loc, cp in started:
            loc.turn()
        for loc, cp in started:
            cp.wait_recv()
        for loc, cp in started:
            cp.wait_send()
            loc.wait()

    flat_in = [a for sl in slabs for a in sl.arrays]
    compact = [jax.ShapeDtypeStruct((N_CHIPS, sl.half, sl.cols), F32) for sl in slabs]
    outs = pl.pallas_call(
        body, name="grad_pair_exchange",
        in_specs=[ANY] * n_in, out_specs=[ANY] * (2 * n), out_shape=compact + compact,
        scratch_shapes=[pltpu.VMEM(s.shape, F32) for s in compact] + [pltpu.SemaphoreType.DMA((N_CHIPS * n,))] * 4,
        compiler_params=_comm_params(sum(_nbytes(s.shape, F32) for s in compact)),
    )(*flat_in)
    return outs[:n], outs[n:]


def _chip_exchange(partials, vec):
    n = len(partials)

    def body(*refs):
        ins, vec_ref = refs[:n], refs[n]
        outs, vec_out = refs[n + 1:2 * n + 1], refs[2 * n + 1]
        bufs = refs[2 * n + 2:3 * n + 3]
        send_sems, recv_sems, in_sems, out_sems = refs[3 * n + 3:]
        x, y, c = _place()
        me = 2 * x + y
        dev = 2 * me + c
        locs, sends = [], []
        for k in range(n):
            loc = _Bounce(ins[k].at[me], bufs[k], outs[k].at[me], in_sems.at[k], out_sems.at[k])
            loc.start()
            locs.append(loc)
            for j, (dx, dy) in enumerate(CHIP_MOVES):
                px, py = _flip(x, dx), _flip(y, dy)
                cp = pltpu.make_async_remote_copy(
                    src_ref=ins[k].at[2 * px + py], dst_ref=outs[k].at[me],
                    send_sem=send_sems.at[3 * k + j], recv_sem=recv_sems.at[3 * k + j],
                    device_id=(px, py, c), device_id_type=MESH_ID)
                cp.start()
                sends.append(cp)
        loc = _Bounce(vec_ref, bufs[n], vec_out.at[dev], in_sems.at[n], out_sems.at[n])
        loc.start()
        locs.append(loc)
        for j, (dx, dy, dc) in enumerate(DEV_MOVES):
            cp = pltpu.make_async_remote_copy(
                src_ref=vec_ref, dst_ref=vec_out.at[dev],
                send_sem=send_sems.at[3 * n + j], recv_sem=recv_sems.at[3 * n + j],
                device_id=(_flip(x, dx), _flip(y, dy), _flip(c, dc)), device_id_type=MESH_ID)
            cp.start()
            sends.append(cp)
        for loc in locs:
            loc.turn()
        for k in range(n):
            for j, (dx, dy) in enumerate(CHIP_MOVES):
                px, py = _flip(x, dx), _flip(y, dy)
                pltpu.make_async_remote_copy(
                    src_ref=ins[k].at[me], dst_ref=outs[k].at[2 * px + py],
                    send_sem=send_sems.at[3 * k + j], recv_sem=recv_sems.at[3 * k + j],
                    device_id=(px, py, c), device_id_type=MESH_ID).wait_recv()
        for j, (dx, dy, dc) in enumerate(DEV_MOVES):
            px, py, pc = _flip(x, dx), _flip(y, dy), _flip(c, dc)
            pltpu.make_async_remote_copy(
                src_ref=vec_ref, dst_ref=vec_out.at[4 * px + 2 * py + pc],
                send_sem=send_sems.at[3 * n + j], recv_sem=recv_sems.at[3 * n + j],
                device_id=(px, py, pc), device_id_type=MESH_ID).wait_recv()
        for cp in sends:
            cp.wait_send()
        for loc in locs:
            loc.wait()

    n_sem = 3 * n + len(DEV_MOVES)
    bufs = [pltpu.VMEM(p.shape[1:], p.dtype) for p in partials] + [pltpu.VMEM(vec.shape, F32)]
    outs = pl.pallas_call(
        body, name="grad_chip_exchange",
        in_specs=[ANY] * (n + 1), out_specs=[ANY] * (n + 1),
        out_shape=[jax.ShapeDtypeStruct(p.shape, p.dtype) for p in partials] + [jax.ShapeDtypeStruct((N_DEV,) + vec.shape, F32)],
        scratch_shapes=bufs + [pltpu.SemaphoreType.DMA((n_sem,)), pltpu.SemaphoreType.DMA((n_sem,)),
                               pltpu.SemaphoreType.DMA((n + 1,)), pltpu.SemaphoreType.DMA((n + 1,))],
        compiler_params=_comm_params(sum(_nbytes(p.shape[1:], p.dtype) for p in partials) + _nbytes(vec.shape, F32)),
    )(*partials, vec)
    return outs[:n], outs[n]


def _pair_share(halves):
    n = len(halves)

    def body(*refs):
        ins, outs, bufs = refs[:n], refs[n:2 * n], refs[2 * n:3 * n]
        send_sems, recv_sems, in_sems, out_sems = refs[3 * n:]
        x, y, c = _place()
        started = []
        for k in range(n):
            hr = halves[k].shape[0]
            rows = lambda hc, k=k, hr=hr: outs[k].at[pl.ds(pl.multiple_of(hc * hr, 8), hr), :]
            loc = _Bounce(ins[k], bufs[k], rows(c), in_sems.at[k], out_sems.at[k])
            loc.start()
            cp = pltpu.make_async_remote_copy(
                src_ref=ins[k], dst_ref=rows(c), send_sem=send_sems.at[k], recv_sem=recv_sems.at[k],
                device_id=(x, y, 1 - c), device_id_type=MESH_ID)
            cp.start()
            recv = pltpu.make_async_remote_copy(
                src_ref=ins[k], dst_ref=rows(1 - c), send_sem=send_sems.at[k], recv_sem=recv_sems.at[k],
                device_id=(x, y, 1 - c), device_id_type=MESH_ID)
            started.append((loc, cp, recv))
        for loc, cp, recv in started:
            loc.turn()
        for loc, cp, recv in started:
            recv.wait_recv()
        for loc, cp, recv in started:
            cp.wait_send()
            loc.wait()

    return pl.pallas_call(
        body, name="grad_pair_share",
        in_specs=[ANY] * n, out_specs=[ANY] * n,
        out_shape=[jax.ShapeDtypeStruct((2 * h.shape[0], h.shape[1]), F32) for h in halves],
        scratch_shapes=[pltpu.VMEM(h.shape, F32) for h in halves] + [pltpu.SemaphoreType.DMA((n,))] * 4,
        compiler_params=_comm_params(sum(_nbytes(h.shape, F32) for h in halves)),
    )(*halves)


def _row_block(rows, cols, n_arrays):
    br = rows
    while br % 16 == 0 and 2 * n_arrays * br * cols * 4 > (16 << 20):
        br //= 2
    return br


def _add2(a, b, out_dtype, name):
    rows, cols = a.shape
    br = _row_block(rows, cols, 3)

    def body(a_ref, b_ref, o_ref):
        o_ref[...] = (a_ref[...] + b_ref[...]).astype(out_dtype)

    spec = pl.BlockSpec((br, cols), lambda i: (i, 0))
    return pl.pallas_call(body, name=name, grid=(rows // br,), in_specs=[spec, spec], out_specs=spec,
                          out_shape=jax.ShapeDtypeStruct(a.shape, out_dtype),
                          compiler_params=_cparams(3 * br * cols * 4, 1))(a, b)


def _sum_slots(a, name):
    n, rows, cols = a.shape
    br = _row_block(rows, cols, n + 1)

    def body(a_ref, o_ref):
        acc = a_ref[0].astype(F32)
        for s in range(1, n):
            acc = acc + a_ref[s].astype(F32)
        o_ref[...] = acc

    return pl.pallas_call(body, name=name, grid=(rows // br,),
                          in_specs=[pl.BlockSpec((n, br, cols), lambda i: (0, i, 0))],
                          out_specs=pl.BlockSpec((br, cols), lambda i: (i, 0)),
                          out_shape=jax.ShapeDtypeStruct((rows, cols), F32),
                          compiler_params=_cparams((n + 1) * br * cols * 4, 1))(a)


def _adamw_math(w, g, m, v):
    m = ADAM_B1 * m + (1.0 - ADAM_B1) * g
    v = ADAM_B2 * v + (1.0 - ADAM_B2) * (g * g)
    m_hat = m / (1.0 - ADAM_B1 ** ADAM_STEP)
    v_hat = v / (1.0 - ADAM_B2 ** ADAM_STEP)
    delta = -ADAM_LR * (m_hat / (jnp.sqrt(v_hat) + ADAM_EPS) + ADAM_WD * w)
    return delta, m, v


def _adamw(g, w, m, v, name):
    rows, cols = g.shape
    br = _row_block(rows, cols, 7)

    def body(g_ref, w_ref, m_ref, v_ref, d_ref, nm_ref, nv_ref):
        d_ref[...], nm_ref[...], nv_ref[...] = _adamw_math(w_ref[...], g_ref[...], m_ref[...], v_ref[...])

    spec = pl.BlockSpec((br, cols), lambda i: (i, 0))
    return pl.pallas_call(body, name=name, grid=(rows // br,), in_specs=[spec] * 4, out_specs=[spec] * 3,
                          out_shape=[jax.ShapeDtypeStruct(g.shape, F32)] * 3,
                          compiler_params=_cparams(7 * br * cols * 4, 1))(g, w, m, v)


ROW_FINAL_G, ROW_PE_G = 0, 1
ROW_CONV_B, ROW_CN_G, ROW_CN_B, ROW_B_PW2 = 8, 9, 10, 11
ROW_LN_G = 16
ROW_ONORM_G, ROW_LB = 24, 25
ROW_CONV_W = 32
SMALL = ["ln_g", "conv_b", "cnorm_g", "cnorm_b", "b_pw2", "onorm_g", "pe_norm_g", "final_g"]
SMALL_ROW = dict(ln_g=ROW_LN_G, conv_b=ROW_CONV_B, cnorm_g=ROW_CN_G, cnorm_b=ROW_CN_B, b_pw2=ROW_B_PW2,
                 onorm_g=ROW_ONORM_G, pe_norm_g=ROW_PE_G, final_g=ROW_FINAL_G)


def _adamw_small(vsum, gcw, lb_logits, params):
    names = SMALL + ["lb_logits", "conv_w"]
    flat = [t for nm in names for t in params[nm]]

    def body(*refs):
        vs_ref, gcw_ref, lbl_ref = refs[:3]
        ins = refs[3:3 + 3 * len(names)]
        outs = refs[3 + 3 * len(names):]
        for q, nm in enumerate(names):
            w_ref, m_ref, v_ref = ins[3 * q:3 * q + 3]
            g_ref, d_ref, nm_ref, nv_ref = outs[4 * q:4 * q + 4]
            if nm == "conv_w":
                g = gcw_ref[...]
            elif nm == "lb_logits":
                lb = _softmax_row0(lbl_ref[...])
                g0 = vs_ref[ROW_LB:ROW_LB + 1, :] * lb * (1.0 - lb)
                g = jnp.concatenate([g0, -g0], axis=0)
            else:
                g = vs_ref[SMALL_ROW[nm]:SMALL_ROW[nm] + 1, :]
            g_ref[...] = g
            d_ref[...], nm_ref[...], nv_ref[...] = _adamw_math(w_ref[...], g, m_ref[...], v_ref[...])

    out_shape = [jax.ShapeDtypeStruct(params[nm][0].shape, F32) for nm in names for _ in range(4)]
    outs = pl.pallas_call(body, name="adamw_small", out_shape=out_shape)(vsum, gcw, lb_logits, *flat)
    return {nm: tuple(outs[4 * q:4 * q + 4]) for q, nm in enumerate(names)}


def _tile(T, want):
    return min(T, want)


def kernel(x, p, ln_g, w_in, conv_w, conv_b, cnorm_g, cnorm_b, w_pw2, b_pw2, lb_logits, onorm_g, w_out, pe_norm_g, w_pg, w_pp, final_g, loss_target, m_ln_g, m_w_in, m_conv_w, m_conv_b, m_cnorm_g, m_cnorm_b, m_w_pw2, m_b_pw2, m_lb_logits, m_onorm_g, m_w_out, m_pe_norm_g, m_w_pg, m_w_pp, m_final_g, v_ln_g, v_w_in, v_conv_w, v_conv_b, v_cnorm_g, v_cnorm_b, v_w_pw2, v_b_pw2, v_lb_logits, v_onorm_g, v_w_out, v_pe_norm_g, v_w_pg, v_w_pp, v_final_g):
    given = dict(locals())
    x2, p2, tgt = x[0], p[0, 0], loss_target[0]
    T = x2.shape[0]
    fin_g = final_g.reshape(1, D)

    conv_w_pad = jnp.pad(conv_w[0], ((0, HALO - CONV_K), (0, 0)))
    w_in_f, w_pw2_f, w_out_f, w_pg_f, w_pp_f, conv_w_f = _all_gather_shards(
        [w_in[0].astype(_MXU), w_pw2[0].astype(_MXU), w_out[0].astype(_MXU), w_pg[0].astype(_MXU),
         w_pp[0].astype(_MXU), conv_w_pad],
        [1, 0, 0, 0, 1, 1])

    z, u = _inproj_fwd(x2, ln_g, w_in_f, _tile(T, 512))
    yc = _conv_fwd(z, conv_w_f, conv_b, cnorm_g, cnorm_b, w_pw2_f, b_pw2, _tile(T, 256))
    o_raw, yh, s_chunks = _hgrn_fwd(z, lb_logits, onorm_g, _tile(T, 512), HB)
    dyc, dyh, dh, n2, ds, dpe, dhb, pb, vec_tail, loss_part = _tail(
        x2, yc, yh, p2, tgt, w_out_f, w_pg_f, w_pp_f, pe_norm_g, fin_g, _tile(T, 256))
    dzh, vec_hgrn = _hgrn_bwd(z, lb_logits, onorm_g, o_raw, dyh, s_chunks, _tile(T, 512), HB)
    dzc, a_act, dy2, vec_conv, g_conv_w = _conv_bwd(z, dyc, conv_w_f, conv_b, cnorm_g, cnorm_b, w_pw2_f, b_pw2, _tile(T, 256))
    grad_x, vec_in = _inproj_bwd_x(x2, ln_g, dzc, dzh, w_in_f, dh, _tile(T, 512))
    tk = _tile(T, 512)
    g_w_in = _inproj_bwd_w(u, dzc, dzh, tk)
    g_w_pw2 = _tn_matmul(a_act, dy2, tk, "grad_w_pw2")
    g_w_out_c = _tn_matmul(yc, dhb, tk, "grad_w_out_conv")
    g_w_out_h = _tn_matmul(yh, dhb, tk, "grad_w_out_hgrn")
    g_w_pg = _tn_matmul(n2, ds, tk, "grad_w_pg")
    g_w_pp = _tn_matmul(pb, dpe, tk, "grad_w_pp")

    big = ["w_in", "w_pw2", "w_out", "w_pg", "w_pp"]
    slabs = [
        _Slab([g_w_in], lambda s: (0, 0, s * (NPART * D // N_CHIPS)), (D, NPART * D // N_CHIPS)),
        _Slab([g_w_pw2], lambda s: (0, s * (D // N_CHIPS), 0), (D // N_CHIPS, D)),
        _Slab([g_w_out_c, g_w_out_h], lambda s: (s // 2, (s % 2) * (D // 2), 0), (D // 2, D)),
        _Slab([g_w_pg], lambda s: (0, s * (D // N_CHIPS), 0), (D // N_CHIPS, D)),
        _Slab([g_w_pp], lambda s: (0, 0, s * (D // N_CHIPS)), (PLE, D // N_CHIPS)),
    ]
    mine, got = _pair_exchange(slabs)
    partial = [_add2(a.reshape(-1, a.shape[-1]), b.reshape(-1, b.shape[-1]), _WIRE, "pair_sum_" + nm).reshape(a.shape)
               for a, b, nm in zip(mine, got, big)]
    vec = jnp.concatenate([vec_tail, vec_conv, vec_in, vec_hgrn, g_conv_w], axis=0)
    slots, vec_slots = _chip_exchange(partial, vec)
    halves = [_sum_slots(s, "chip_sum_" + nm) for s, nm in zip(slots, big)]
    vsum = _sum_slots(vec_slots, "vec_sum")
    grads_big = _pair_share(halves)

    out = {}
    for nm, g in zip(big, grads_big):
        w2, m2, v2 = given[nm][0], given["m_" + nm][0], given["v_" + nm][0]
        d, nm_, nv_ = _adamw(g, w2, m2, v2, "adamw_" + nm)
        out[nm] = tuple(t[None] for t in (g, d, nm_, nv_))
    chip = 2 * lax.axis_index("x") + lax.axis_index("y")
    gcw = lax.dynamic_slice(vsum, (ROW_CONV_W, chip * (D // N_CHIPS)), (CONV_K, D // N_CHIPS))
    params = {nm: (given[nm].reshape(-1, D), given["m_" + nm].reshape(-1, D), given["v_" + nm].reshape(-1, D))
              for nm in SMALL + ["lb_logits"]}
    params["conv_w"] = (conv_w[0], m_conv_w[0], v_conv_w[0])
    small = _adamw_small(vsum, gcw, lb_logits, params)
    for nm, ts in small.items():
        out[nm] = tuple(t.reshape(given[nm].shape) for t in ts)

    loss = lax.psum(loss_part[0, 0], ("x", "y", "c"))
    order = ["ln_g", "w_in", "conv_w", "conv_b", "cnorm_g", "cnorm_b", "w_pw2", "b_pw2", "lb_logits", "onorm_g",
             "w_out", "pe_norm_g", "w_pg", "w_pp", "final_g"]
    return (loss, grad_x[None], *[out[nm][0] for nm in order], *[out[nm][1] for nm in order],
            *[out[nm][2] for nm in order], *[out[nm][3] for nm in order])
```

```python
import functools

import jax
import jax.numpy as jnp
from jax import lax
from jax.experimental import pallas as pl
from jax.experimental.pallas import tpu as pltpu

F32 = jnp.float32
BF16 = jnp.bfloat16
_MXU = jnp.bfloat16
_WIRE = jnp.bfloat16

D = 1024
NPART = 7
PLE = 256
HEADS = 8
HD = 128
CHUNK = 64
CONV_K = 31
HALO = 32
EPS = 1e-6
N_CHIPS = 4
N_DEV = 8
HB = 8
VEC_ROWS = 64

ADAM_LR = 0.001
ADAM_B1 = 0.9
ADAM_B2 = 0.999
ADAM_EPS = 1e-08
ADAM_WD = 0.01
ADAM_STEP = 10

V7X_VMEM_LIMIT = 60000 * 1024
MESH_ID = pl.DeviceIdType.MESH
ANY = pl.BlockSpec(memory_space=pl.ANY)


def _cparams(block_bytes, n_grid_dims):
    limit = min(V7X_VMEM_LIMIT, 2 * block_bytes + (24 << 20))
    return pltpu.CompilerParams(vmem_limit_bytes=int(limit), dimension_semantics=("arbitrary",) * n_grid_dims)


def _nbytes(shape, dtype):
    n = 1
    for s in shape:
        n *= s
    return n * jnp.dtype(dtype).itemsize


def _dot(a, b):
    return jnp.dot(a.astype(_MXU), b.astype(_MXU), preferred_element_type=F32)


def _dot_nt(a, b):
    return lax.dot_general(a.astype(_MXU), b.astype(_MXU), (((1,), (1,)), ((), ())), preferred_element_type=F32)


def _dot_tn(a, b):
    return lax.dot_general(a.astype(_MXU), b.astype(_MXU), (((0,), (0,)), ((), ())), preferred_element_type=F32)


def _tri_dot(tri_bf, x):
    x1 = x.astype(BF16)
    r1 = x - x1.astype(F32)
    x2 = r1.astype(BF16)
    x3 = (r1 - x2.astype(F32)).astype(BF16)
    d = lambda t: jnp.dot(tri_bf, t, preferred_element_type=F32)
    return d(x1) + d(x2) + d(x3)


def _split2(x):
    hi = x.astype(BF16)
    return hi, (x - hi.astype(F32)).astype(BF16)


def _dot3(dims, a, b):
    d = lambda p, q: lax.dot_general(p, q, (dims, ((), ())), preferred_element_type=F32)
    return d(a[0], b[0]) + d(a[0], b[1]) + d(a[1], b[0])


def _sigmoid(x):
    return jax.nn.sigmoid(x)


def _mean_lanes(x):
    return jnp.mean(x, axis=-1, keepdims=True)


def _sum_rows(x):
    return jnp.sum(x, axis=0, keepdims=True)


def _group_ln(y):
    yn, rs = [], []
    for g in range(D // HD):
        blk = y[:, g * HD:(g + 1) * HD]
        xc = blk - _mean_lanes(blk)
        r = lax.rsqrt(_mean_lanes(xc * xc) + EPS)
        yn.append(xc * r)
        rs.append(jnp.broadcast_to(r, blk.shape))
    return jnp.concatenate(yn, axis=1), jnp.concatenate(rs, axis=1)


def _group_ln_bwd(dyn, yn, rstd):
    out = []
    for g in range(D // HD):
        sl = slice(g * HD, (g + 1) * HD)
        d, n = dyn[:, sl], yn[:, sl]
        out.append(rstd[:, sl] * (d - _mean_lanes(d) - n * _mean_lanes(d * n)))
    return jnp.concatenate(out, axis=1)


def _head_means(x, hb, fn=lambda m: m):
    return jnp.concatenate([jnp.broadcast_to(fn(_mean_lanes(x[:, hh * HD:(hh + 1) * HD])), (x.shape[0], HD))
                            for hh in range(hb)], axis=1)


def _head_rsqrt_mean(x, hb):
    return _head_means(x, hb, lambda m: lax.rsqrt(m + EPS))


def _softmax_row0(lbl):
    m = jnp.max(lbl, axis=0, keepdims=True)
    e = jnp.exp(lbl - m)
    return e[0:1, :] / jnp.sum(e, axis=0, keepdims=True)


def _inproj_fwd(x, ln_g, w_in, tT):
    T = x.shape[0]

    def body(x_ref, g_ref, w_ref, z_ref, u_ref, u_scr):
        @pl.when(pl.program_id(1) == 0)
        def _():
            xv = x_ref[...]
            r = lax.rsqrt(_mean_lanes(xv * xv) + EPS)
            u = (xv * r * g_ref[...]).astype(_MXU)
            u_scr[...] = u
            u_ref[...] = u
        z_ref[...] = jnp.dot(u_scr[...], w_ref[...], preferred_element_type=F32)

    blk = _nbytes((tT, D), F32) * 2 + _nbytes((D, D), _MXU) + _nbytes((tT, D), _MXU) * 2
    return pl.pallas_call(
        body, name="inproj_fwd", grid=(T // tT, NPART),
        in_specs=[pl.BlockSpec((tT, D), lambda i, j: (i, 0)), pl.BlockSpec((1, D), lambda i, j: (0, 0)),
                  pl.BlockSpec((D, D), lambda i, j: (0, j))],
        out_specs=[pl.BlockSpec((tT, D), lambda i, j: (i, j)), pl.BlockSpec((tT, D), lambda i, j: (i, 0))],
        out_shape=[jax.ShapeDtypeStruct((T, NPART * D), F32), jax.ShapeDtypeStruct((T, D), _MXU)],
        scratch_shapes=[pltpu.VMEM((tT, D), _MXU)],
        compiler_params=_cparams(blk, 2),
    )(x, ln_g, w_in)


def _shifted_windows(ext, first, visit):
    n = ext.shape[0]
    for m in range(first, first + CONV_K):
        visit(m, (ext if m == 0 else pltpu.roll(ext, n - m, axis=0))[0:n - HALO, :])


def _conv_fwd(z, conv_w, conv_b, cn_g, cn_b, w_pw2, b_pw2, tT):
    T = z.shape[0]

    def body(cv_ref, cg_ref, ct_ref, cw_ref, cb_ref, ng_ref, nb_ref, wp_ref, bp_ref, yc_ref, y1_ref, ext):
        @pl.when(pl.program_id(0) == 0)
        def _():
            ext[...] = jnp.zeros_like(ext)
        ext[0:HALO, :] = ext[tT:tT + HALO, :]
        ext[HALO:, :] = cv_ref[...] * _sigmoid(cg_ref[...])
        cw = cw_ref[...]
        acc = [cb_ref[...]]

        def tap(m, win):
            acc[0] = acc[0] + win * cw[m - 2:m - 1, :]
        _shifted_windows(ext[...], 2, tap)
        y1 = acc[0]
        y1_ref[...] = y1
        yn, _ = _group_ln(y1)
        apre = yn * ng_ref[...] + nb_ref[...]
        a = apre * _sigmoid(apre)
        y2 = _dot(a, wp_ref[...]) + bp_ref[...]
        ct = ct_ref[...]
        yc_ref[...] = (y2 * (ct * _sigmoid(ct))).astype(_MXU)

    part = lambda p: pl.BlockSpec((tT, D), lambda i: (i, p))
    row = pl.BlockSpec((1, D), lambda i: (0, 0))
    tok = pl.BlockSpec((tT, D), lambda i: (i, 0))
    blk = 4 * _nbytes((tT, D), F32) + _nbytes((D, D), _MXU) + _nbytes((tT, D), _MXU) + 8 * _nbytes((tT + HALO, D), F32)
    return pl.pallas_call(
        body, name="conv_fwd", grid=(T // tT,),
        in_specs=[part(0), part(1), part(2), pl.BlockSpec((HALO, D), lambda i: (0, 0)), row, row, row,
                  pl.BlockSpec((D, D), lambda i: (0, 0)), row],
        out_specs=[tok, tok],
        out_shape=[jax.ShapeDtypeStruct((T, D), _MXU), jax.ShapeDtypeStruct((T, D), F32)],
        scratch_shapes=[pltpu.VMEM((tT + HALO, D), F32)],
        compiler_params=_cparams(blk, 1),
    )(z, z, z, conv_w, conv_b, cn_g, cn_b, w_pw2, b_pw2)


def _hgrn_gates(lb, hq, hf):
    sq = _sigmoid(hq)
    sg = _sigmoid(hf)
    f = lb + (1.0 - lb) * sg
    return sq, sg, f, hq * sq, (1.0 - lb) * (1.0 - sg), jnp.log(f)


def _chunk_decays(lf, q, k):
    r = lax.broadcasted_iota(jnp.int32, (CHUNK, CHUNK), 0)
    c = lax.broadcasted_iota(jnp.int32, (CHUNK, CHUNK), 1)
    b = _tri_dot((r >= c).astype(BF16), lf)
    bm = b[CHUNK // 2 - 1:CHUNK // 2, :]
    bl = b[CHUNK - 1:CHUNK, :]
    eb = jnp.exp(b)
    eqm = jnp.exp(b - bm)
    ekm = jnp.exp(bm - b)
    ekd = jnp.exp(bl - b)
    return dict(causal=r >= c, eb=eb, eqm=eqm, ekm=ekm, ekd=ekd, ebl=jnp.exp(bl),
                qd=q * eb, qm=q * eqm, km=k * ekm, kd=k * ekd)


def _hgrn_fwd(z, lb_logits, onorm_g, tT, hb):
    T = z.shape[0]
    nc = tT // CHUNK
    w = hb * HD

    def body(lbl_ref, og_ref, hq_ref, hf_ref, hi_ref, hg_ref, o_ref, yh_ref, sc_ref, st):
        @pl.when(pl.program_id(1) == 0)
        def _():
            st[...] = jnp.zeros_like(st)
        lb_all = _softmax_row0(lbl_ref[...])
        og_all = og_ref[...]

        def chunk(c, carry):
            sl = pl.ds(pl.multiple_of(c * CHUNK, CHUNK), CHUNK)
            lanes = [slice(hh * HD, (hh + 1) * HD) for hh in range(hb)]
            heads = lambda fn: [fn(hh, ln) for hh, ln in enumerate(lanes)]
            hg, v = hg_ref[sl, :], hi_ref[sl, :]
            _, _, _, q, k, lf = _hgrn_gates(lb_all, hq_ref[sl, :], hf_ref[sl, :])
            dc = _chunk_decays(lf, q, k)
            s_t = heads(lambda hh, ln: st[hh])
            a = heads(lambda hh, ln: jnp.where(dc["causal"], _dot_nt(dc["qm"][:, ln], dc["km"][:, ln]), 0.0))
            o_inter = heads(lambda hh, ln: _dot_nt(dc["qd"][:, ln], s_t[hh]))
            kv = heads(lambda hh, ln: _dot_tn(v[:, ln], dc["kd"][:, ln]))
            o_intra = heads(lambda hh, ln: _dot(a[hh], v[:, ln]))
            for hh, ln in enumerate(lanes):
                sc_ref[hh, c] = s_t[hh]
                st[hh] = s_t[hh] * dc["ebl"][:, ln] + kv[hh]
            o = jnp.concatenate([o_inter[hh] + o_intra[hh] for hh in range(hb)], axis=1)
            o_ref[sl, :] = o
            n = o * _head_rsqrt_mean(o * o, hb)
            yh_ref[sl, :] = ((n * og_all) * (hg * _sigmoid(hg))).astype(_MXU)
            return carry

        lax.fori_loop(0, nc, chunk, 0)

    zpart = lambda p: pl.BlockSpec((tT, w), lambda h, i: (i, p * (HEADS // hb) + h))
    blk = 6 * _nbytes((tT, w), F32) + _nbytes((hb, nc, HD, HD), F32)
    return pl.pallas_call(
        body, name="hgrn_fwd", grid=(HEADS // hb, T // tT),
        in_specs=[pl.BlockSpec((2, w), lambda h, i: (0, h)), pl.BlockSpec((1, w), lambda h, i: (0, h)),
                  zpart(3), zpart(4), zpart(5), zpart(6)],
        out_specs=[pl.BlockSpec((tT, w), lambda h, i: (i, h)), pl.BlockSpec((tT, w), lambda h, i: (i, h)),
                   pl.BlockSpec((hb, nc, HD, HD), lambda h, i: (h, i, 0, 0))],
        out_shape=[jax.ShapeDtypeStruct((T, D), F32), jax.ShapeDtypeStruct((T, D), _MXU),
                   jax.ShapeDtypeStruct((HEADS, T // CHUNK, HD, HD), F32)],
        scratch_shapes=[pltpu.VMEM((hb, HD, HD), F32)],
        compiler_params=_cparams(blk, 2),
    )(lb_logits, onorm_g, z, z, z, z)


def _hgrn_bwd(z, lb_logits, onorm_g, o_raw, dyh, s_chunks, tT, hb):
    T = z.shape[0]
    nc = tT // CHUNK
    nI = T // tT
    w = hb * HD

    def body(lbl_ref, og_ref, hq_ref, hf_ref, hi_ref, hg_ref, o_ref, dy_ref, sc_ref, dz_ref, vec_ref, dst):
        @pl.when(pl.program_id(1) == 0)
        def _():
            dst[...] = jnp.zeros_like(dst)
            vec_ref[...] = jnp.zeros_like(vec_ref)
        lb_all = _softmax_row0(lbl_ref[...])
        og_all = og_ref[...]
        last_row = lax.broadcasted_iota(jnp.int32, (CHUNK, w), 0) == CHUNK - 1
        r64 = lax.broadcasted_iota(jnp.int32, (CHUNK, CHUNK), 0)
        c64 = lax.broadcasted_iota(jnp.int32, (CHUNK, CHUNK), 1)
        upper = (c64 >= r64).astype(BF16)
        lanes = [slice(hh * HD, (hh + 1) * HD) for hh in range(hb)]
        heads = lambda fn: [fn(hh, ln) for hh, ln in enumerate(lanes)]
        wide = lambda parts: jnp.concatenate(parts, axis=1)

        def chunk(cc, carry):
            c = nc - 1 - cc
            sl = pl.ds(pl.multiple_of(c * CHUNK, CHUNK), CHUNK)
            hq, hg, v = hq_ref[sl, :], hg_ref[sl, :], hi_ref[sl, :]
            sq, sg, f, q, k, lf = _hgrn_gates(lb_all, hq, hf_ref[sl, :])
            dc = _chunk_decays(lf, q, k)
            s_t = heads(lambda hh, ln: sc_ref[hh, c])
            ds_t = heads(lambda hh, ln: dst[hh])
            o, dy = o_ref[sl, :], dy_ref[sl, :]
            r = _head_rsqrt_mean(o * o, hb)
            n = o * r
            sgg = _sigmoid(hg)
            silu_g = hg * sgg
            dhg = dy * (n * og_all) * (sgg * (1.0 + hg * (1.0 - sgg)))
            dn = dy * og_all * silu_g
            g_og = _sum_rows(dy * n * silu_g)
            do = r * (dn - n * _head_means(dn * n, hb))
            a = heads(lambda hh, ln: jnp.where(dc["causal"], _dot_nt(dc["qm"][:, ln], dc["km"][:, ln]), 0.0))
            dam = heads(lambda hh, ln: jnp.where(dc["causal"], _dot_nt(do[:, ln], v[:, ln]), 0.0))
            dqd = wide(heads(lambda hh, ln: _dot(do[:, ln], s_t[hh])))
            dkd = wide(heads(lambda hh, ln: _dot(v[:, ln], ds_t[hh])))
            dv_inter = heads(lambda hh, ln: _dot_nt(dc["kd"][:, ln], ds_t[hh]))
            dqs = heads(lambda hh, ln: _dot_tn(do[:, ln], dc["qd"][:, ln]))
            dv = wide(heads(lambda hh, ln: _dot_tn(a[hh], do[:, ln]) + dv_inter[hh]))
            dam2 = [_split2(t) for t in dam]
            km2, qm2 = _split2(dc["km"]), _split2(dc["qm"])
            dqm = wide(heads(lambda hh, ln: _dot3(((1,), (0,)), dam2[hh], (km2[0][:, ln], km2[1][:, ln]))))
            dkm = wide(heads(lambda hh, ln: _dot3(((0,), (0,)), dam2[hh], (qm2[0][:, ln], qm2[1][:, ln]))))
            debl = wide(heads(lambda hh, ln: _sum_rows(ds_t[hh] * s_t[hh])))
            for hh, ln in enumerate(lanes):
                dst[hh] = ds_t[hh] * dc["ebl"][:, ln] + dqs[hh]
            dq = dqd * dc["eb"] + dqm * dc["eqm"]
            dk = dkm * dc["ekm"] + dkd * dc["ekd"]
            dbl = _sum_rows(dkd * dc["kd"]) + debl * dc["ebl"]
            db = dq * q - dk * k + jnp.where(last_row, dbl, 0.0)
            dlf = _tri_dot(upper, db)
            dfk = dlf / f - dk
            dz_ref[0, sl, :] = (dq * (sq * (1.0 + hq * (1.0 - sq)))).astype(_MXU)
            dz_ref[1, sl, :] = (dfk * ((1.0 - lb_all) * sg * (1.0 - sg))).astype(_MXU)
            dz_ref[2, sl, :] = dv.astype(_MXU)
            dz_ref[3, sl, :] = dhg.astype(_MXU)
            vec_ref[0:1, :] += g_og
            vec_ref[1:2, :] += _sum_rows(dfk * (1.0 - sg))
            return carry

        lax.fori_loop(0, nc, chunk, 0)

    zpart = lambda p: pl.BlockSpec((tT, w), lambda h, i: (nI - 1 - i, p * (HEADS // hb) + h))
    act = pl.BlockSpec((tT, w), lambda h, i: (nI - 1 - i, h))
    blk = 6 * _nbytes((tT, w), F32) + _nbytes((hb, nc, HD, HD), F32) + 4 * _nbytes((tT, w), _MXU)
    return pl.pallas_call(
        body, name="hgrn_bwd", grid=(HEADS // hb, nI),
        in_specs=[pl.BlockSpec((2, w), lambda h, i: (0, h)), pl.BlockSpec((1, w), lambda h, i: (0, h)),
                  zpart(3), zpart(4), zpart(5), zpart(6), act, act,
                  pl.BlockSpec((hb, nc, HD, HD), lambda h, i: (h, nI - 1 - i, 0, 0))],
        out_specs=[pl.BlockSpec((4, tT, w), lambda h, i: (0, nI - 1 - i, h)),
                   pl.BlockSpec((8, w), lambda h, i: (0, h))],
        out_shape=[jax.ShapeDtypeStruct((4, T, D), _MXU), jax.ShapeDtypeStruct((8, D), F32)],
        scratch_shapes=[pltpu.VMEM((hb, HD, HD), F32)],
        compiler_params=_cparams(blk, 2),
    )(lb_logits, onorm_g, z, z, z, z, o_raw, dyh, s_chunks)


def _tail(x, yc, yh, p, target, w_out, w_out_t, w_pg, w_pg_t, w_pp, pe_g, fin_g, tT):
    T = x.shape[0]

    def body(x_ref, yc_ref, yh_ref, p_ref, t_ref, wo_ref, wot_ref, wg_ref, wgt_ref, wp_ref, pg_ref, fg_ref,
             dyc_ref, dyh_ref, dh_ref, n2_ref, ds_ref, dpe_ref, dhb_ref, pb_ref, vec_ref, loss_ref):
        @pl.when(pl.program_id(0) == 0)
        def _():
            vec_ref[...] = jnp.zeros_like(vec_ref)
            loss_ref[...] = jnp.zeros_like(loss_ref)
        wo_c, wo_h = wo_ref[0:D, :], wo_ref[D:2 * D, :]
        h = x_ref[...] + _dot(yc_ref[...], wo_c) + _dot(yh_ref[...], wo_h)
        pb = p_ref[...].astype(_MXU)
        pe = _dot(pb, wp_ref[...])
        r2 = lax.rsqrt(_mean_lanes(h * h) + EPS)
        hn = h * r2
        n2 = (hn * pg_ref[...]).astype(_MXU)
        gate = _sigmoid(_dot(n2, wg_ref[...]))
        h2 = h + gate * pe
        r3 = lax.rsqrt(_mean_lanes(h2 * h2) + EPS)
        h2n = h2 * r3
        err = h2n * fg_ref[...] - t_ref[...]
        loss_ref[...] += 0.5 * jnp.sum(_mean_lanes(err * err))
        dout = err * (1.0 / D)
        vec_ref[0:1, :] += _sum_rows(dout * h2n)
        dn3 = dout * fg_ref[...]
        dh2 = r3 * (dn3 - h2n * _mean_lanes(dn3 * h2n))
        ds = (dh2 * pe * gate * (1.0 - gate)).astype(_MXU)
        dn2 = _dot(ds, wgt_ref[...])
        vec_ref[1:2, :] += _sum_rows(dn2 * hn)
        dnn = dn2 * pg_ref[...]
        dh = dh2 + r2 * (dnn - hn * _mean_lanes(dnn * hn))
        dhb = dh.astype(_MXU)
        dyc_ref[...] = _dot(dhb, wot_ref[:, 0:D])
        dyh_ref[...] = _dot(dhb, wot_ref[:, D:2 * D])
        dh_ref[...] = dh
        n2_ref[...] = n2
        ds_ref[...] = ds
        dpe_ref[...] = (dh2 * gate).astype(_MXU)
        dhb_ref[...] = dhb
        pb_ref[...] = pb

    tok = lambda w: pl.BlockSpec((tT, w), lambda i: (i, 0))
    full = lambda r, c: pl.BlockSpec((r, c), lambda i: (0, 0))
    tokshape = lambda w, dt: jax.ShapeDtypeStruct((T, w), dt)
    blk = (5 * _nbytes((tT, D), F32) + 7 * _nbytes((tT, D), _MXU) + _nbytes((7 * D + PLE, D), _MXU)
           + 12 * _nbytes((tT, D), F32))
    return pl.pallas_call(
        body, name="tail_fwd_bwd", grid=(T // tT,),
        in_specs=[tok(D), tok(D), tok(D), tok(PLE), tok(D), full(2 * D, D), full(D, 2 * D), full(D, D), full(D, D),
                  full(PLE, D), full(1, D), full(1, D)],
        out_specs=[tok(D), tok(D), tok(D), tok(D), tok(D), tok(D), tok(D), tok(PLE), full(8, D), full(8, HD)],
        out_shape=[tokshape(D, F32), tokshape(D, F32), tokshape(D, F32), tokshape(D, _MXU), tokshape(D, _MXU),
                   tokshape(D, _MXU), tokshape(D, _MXU), tokshape(PLE, _MXU),
                   jax.ShapeDtypeStruct((8, D), F32), jax.ShapeDtypeStruct((8, HD), F32)],
        compiler_params=_cparams(blk, 1),
    )(x, yc, yh, p, target, w_out, w_out_t, w_pg, w_pg_t, w_pp, pe_g, fin_g)


def _conv_bwd(z, y1, dyc, conv_w, cn_g, cn_b, w_pw2, w_pw2_t, b_pw2, tT):
    T = z.shape[0]
    nI = T // tT
    hb = tT // HALO

    def body(cv_ref, cg_ref, ct_ref, hv_ref, hg_ref, y1_ref, dyc_ref, cw_ref, ng_ref, nb_ref, wp_ref, wpt_ref, bp_ref,
             dz_ref, a_ref, dy2_ref, vec_ref, gcw_ref, ext, ext2, gpart):
        i = pl.program_id(0)

        @pl.when(i == 0)
        def _():
            ext2[...] = jnp.zeros_like(ext2)
            gpart[...] = jnp.zeros_like(gpart)
            vec_ref[...] = jnp.zeros_like(vec_ref)
        cv, cg, ct = cv_ref[...], cg_ref[...], ct_ref[...]
        sg = _sigmoid(cg)
        has_hist = (i < nI - 1).astype(F32)
        ext[0:HALO, :] = hv_ref[...] * _sigmoid(hg_ref[...]) * has_hist
        ext[HALO:, :] = cv * sg
        yn, rstd = _group_ln(y1_ref[...])
        apre = yn * ng_ref[...] + nb_ref[...]
        sa = _sigmoid(apre)
        a = (apre * sa).astype(_MXU)
        y2 = _dot(a, wp_ref[...]) + bp_ref[...]
        st = _sigmoid(ct)
        dyc_v = dyc_ref[...]
        dy2 = dyc_v * (ct * st)
        dy2b = dy2.astype(_MXU)
        da = _dot(dy2b, wpt_ref[...])
        dapre = da * (sa * (1.0 + apre * (1.0 - sa)))
        dy1 = _group_ln_bwd(dapre * ng_ref[...], yn, rstd)
        vec_ref[0:1, :] += _sum_rows(dy1)
        vec_ref[1:2, :] += _sum_rows(dapre * yn)
        vec_ref[2:3, :] += _sum_rows(dapre)
        vec_ref[3:4, :] += _sum_rows(dy2)
        dz_ref[2] = (dyc_v * y2 * (st * (1.0 + ct * (1.0 - st)))).astype(_MXU)
        a_ref[...] = a
        dy2_ref[...] = dy2b
        ext2[tT:tT + HALO, :] = ext2[0:HALO, :]
        ext2[0:tT, :] = dy1
        def grad_tap(m, win):
            p = dy1 * win
            part = p[0:8, :]
            for q in range(1, tT // 8):
                part = part + p[8 * q:8 * q + 8, :]
            gpart[m - 2] += part
        _shifted_windows(ext[...], 2, grad_tap)
        cw = cw_ref[...]
        acc = [None]

        def dv_tap(m, win):
            term = win * cw[CONV_K - 1 - m:CONV_K - m, :]
            acc[0] = term if acc[0] is None else acc[0] + term
        _shifted_windows(ext2[...], 0, dv_tap)
        dv = acc[0]
        dz_ref[0] = (dv * sg).astype(_MXU)
        dz_ref[1] = (dv * cv * sg * (1.0 - sg)).astype(_MXU)

        @pl.when(i == nI - 1)
        def _():
            gcw_ref[...] = jnp.sum(gpart[...], axis=1)

    part = lambda p: pl.BlockSpec((tT, D), lambda i: (nI - 1 - i, p))
    hist = lambda p: pl.BlockSpec((HALO, D), lambda i: (jnp.maximum((nI - 1 - i) * hb - 1, 0), p))
    tok = pl.BlockSpec((tT, D), lambda i: (nI - 1 - i, 0))
    row = pl.BlockSpec((1, D), lambda i: (0, 0))
    blk = (5 * _nbytes((tT, D), F32) + 2 * _nbytes((D, D), _MXU) + 5 * _nbytes((tT, D), _MXU)
           + 10 * _nbytes((tT + HALO, D), F32))
    return pl.pallas_call(
        body, name="conv_bwd", grid=(nI,),
        in_specs=[part(0), part(1), part(2), hist(0), hist(1), tok, tok, pl.BlockSpec((HALO, D), lambda i: (0, 0)),
                  row, row, pl.BlockSpec((D, D), lambda i: (0, 0)), pl.BlockSpec((D, D), lambda i: (0, 0)), row],
        out_specs=[pl.BlockSpec((3, tT, D), lambda i: (0, nI - 1 - i, 0)), tok, tok,
                   pl.BlockSpec((8, D), lambda i: (0, 0)), pl.BlockSpec((HALO, D), lambda i: (0, 0))],
        out_shape=[jax.ShapeDtypeStruct((3, T, D), _MXU), jax.ShapeDtypeStruct((T, D), _MXU),
                   jax.ShapeDtypeStruct((T, D), _MXU), jax.ShapeDtypeStruct((8, D), F32),
                   jax.ShapeDtypeStruct((HALO, D), F32)],
        scratch_shapes=[pltpu.VMEM((tT + HALO, D), F32), pltpu.VMEM((tT + HALO, D), F32), pltpu.VMEM((HALO, 8, D), F32)],
        compiler_params=_cparams(blk, 1),
    )(z, z, z, z, z, y1, dyc, conv_w, cn_g, cn_b, w_pw2, w_pw2_t, b_pw2)


def _inproj_bwd_x(x, ln_g, dzc, dzh, w_in_t, dh, tT):
    T = x.shape[0]

    def body(x_ref, g_ref, dzc_ref, dzh_ref, w_ref, dh_ref, gx_ref, vec_ref, du):
        i, j = pl.program_id(0), pl.program_id(1)

        @pl.when(j == 0)
        def _():
            du[...] = jnp.zeros_like(du)

        @pl.when(jnp.logical_and(i == 0, j == 0))
        def _():
            vec_ref[...] = jnp.zeros_like(vec_ref)

        @pl.when(j < 3)
        def _():
            du[...] += _dot(dzc_ref[0], w_ref[...])

        @pl.when(j >= 3)
        def _():
            du[...] += _dot(dzh_ref[0], w_ref[...])

        @pl.when(j == NPART - 1)
        def _():
            xv = x_ref[...]
            r = lax.rsqrt(_mean_lanes(xv * xv) + EPS)
            xn = xv * r
            duv = du[...]
            vec_ref[0:1, :] += _sum_rows(duv * xn)
            dun = duv * g_ref[...]
            gx_ref[...] = dh_ref[...] + r * (dun - xn * _mean_lanes(dun * xn))

    tok = pl.BlockSpec((tT, D), lambda i, j: (i, 0))
    blk = 3 * _nbytes((tT, D), F32) + 2 * _nbytes((tT, D), _MXU) + _nbytes((D, D), _MXU) + 4 * _nbytes((tT, D), F32)
    return pl.pallas_call(
        body, name="inproj_bwd_x", grid=(T // tT, NPART),
        in_specs=[tok, pl.BlockSpec((1, D), lambda i, j: (0, 0)),
                  pl.BlockSpec((1, tT, D), lambda i, j: (jnp.minimum(j, 2), i, 0)),
                  pl.BlockSpec((1, tT, D), lambda i, j: (jnp.maximum(j - 3, 0), i, 0)),
                  pl.BlockSpec((D, D), lambda i, j: (j, 0)), tok],
        out_specs=[tok, pl.BlockSpec((8, D), lambda i, j: (0, 0))],
        out_shape=[jax.ShapeDtypeStruct((T, D), F32), jax.ShapeDtypeStruct((8, D), F32)],
        scratch_shapes=[pltpu.VMEM((tT, D), F32)],
        compiler_params=_cparams(blk, 2),
    )(x, ln_g, dzc, dzh, w_in_t, dh)


def _inproj_bwd_w(u, dzc, dzh, tk):
    T = u.shape[0]
    nK = T // tk

    def body(u_ref, dzc_ref, dzh_ref, gw_ref):
        j, k = pl.program_id(0), pl.program_id(1)

        @pl.when(k == 0)
        def _():
            gw_ref[...] = jnp.zeros_like(gw_ref)

        @pl.when(j < 3)
        def _():
            gw_ref[...] += _dot_tn(u_ref[...], dzc_ref[0])

        @pl.when(j >= 3)
        def _():
            gw_ref[...] += _dot_tn(u_ref[...], dzh_ref[0])

    blk = 3 * _nbytes((tk, D), _MXU) + 2 * _nbytes((D, D), F32)
    return pl.pallas_call(
        body, name="inproj_bwd_w", grid=(NPART, nK),
        in_specs=[pl.BlockSpec((tk, D), lambda j, k: (k, 0)),
                  pl.BlockSpec((1, tk, D), lambda j, k: (jnp.minimum(j, 2), jnp.where(j < 3, k, nK - 1), 0)),
                  pl.BlockSpec((1, tk, D), lambda j, k: (jnp.maximum(j - 3, 0), jnp.where(j < 3, 0, k), 0))],
        out_specs=pl.BlockSpec((D, D), lambda j, k: (0, j)),
        out_shape=jax.ShapeDtypeStruct((D, NPART * D), F32),
        compiler_params=_cparams(blk, 2),
    )(u, dzc, dzh)


def _tn_matmul(a, b, tk, name):
    T, M = a.shape
    N = b.shape[1]

    def body(a_ref, b_ref, o_ref):
        @pl.when(pl.program_id(0) == 0)
        def _():
            o_ref[...] = jnp.zeros_like(o_ref)
        o_ref[...] += _dot_tn(a_ref[...], b_ref[...])

    blk = _nbytes((tk, M), _MXU) + _nbytes((tk, N), _MXU) + 2 * _nbytes((M, N), F32)
    return pl.pallas_call(
        body, name=name, grid=(T // tk,),
        in_specs=[pl.BlockSpec((tk, M), lambda k: (k, 0)), pl.BlockSpec((tk, N), lambda k: (k, 0))],
        out_specs=pl.BlockSpec((M, N), lambda k: (0, 0)),
        out_shape=jax.ShapeDtypeStruct((M, N), F32),
        compiler_params=_cparams(blk, 1),
    )(a, b)


def _place():
    return lax.axis_index("x"), lax.axis_index("y"), lax.axis_index("c")


def _flip(v, d):
    return 1 - v if d else v


CHIP_MOVES = [(1, 0), (0, 1), (1, 1)]
DEV_MOVES = [(dx, dy, dc) for dx in (0, 1) for dy in (0, 1) for dc in (0, 1)][1:]


def _shard_slice(ref, axis, size, s):
    start = pl.multiple_of(s * size, size)
    return ref.at[pl.ds(start, size), :] if axis == 0 else ref.at[:, pl.ds(start, size)]


class _Bounce:
    def __init__(self, src, buf, dst, sem_in, sem_out):
        self.load = pltpu.make_async_copy(src, buf, sem_in)
        self.store = pltpu.make_async_copy(buf, dst, sem_out)

    def start(self):
        self.load.start()

    def turn(self):
        self.load.wait()
        self.store.start()

    def wait(self):
        self.store.wait()


def _comm_params(scratch_bytes):
    return pltpu.CompilerParams(vmem_limit_bytes=int(min(V7X_VMEM_LIMIT, scratch_bytes + (8 << 20))))


def _all_gather_shards(shards, axes):
    n = len(shards)
    full_shapes = [tuple(d * (N_CHIPS if a == ax else 1) for a, d in enumerate(s.shape)) for s, ax in zip(shards, axes)]

    def body(*refs):
        ins, outs, bufs = refs[:n], refs[n:2 * n], refs[2 * n:3 * n]
        ici_send, ici_recv, d2d_send, d2d_recv, in_sems, out_sems = refs[3 * n:]
        x, y, c = _place()
        me = 2 * x + y

        def own_half(k, hc):
            half = shards[k].shape[0] // 2
            return ins[k].at[pl.ds(pl.multiple_of(hc * half, 16), half), :]

        def region(k, who, hc):
            rows, cols = shards[k].shape
            half = rows // 2
            if axes[k] == 0:
                return outs[k].at[pl.ds(pl.multiple_of(who * rows + hc * half, 16), half), :]
            return outs[k].at[pl.ds(pl.multiple_of(hc * half, 16), half), pl.ds(pl.multiple_of(who * cols, HD), cols)]

        def ici(k, j, who, hc):
            dx, dy = CHIP_MOVES[j]
            return pltpu.make_async_remote_copy(
                src_ref=own_half(k, hc), dst_ref=region(k, who, hc),
                send_sem=ici_send.at[3 * k + j], recv_sem=ici_recv.at[3 * k + j],
                device_id=(_flip(x, dx), _flip(y, dy), c), device_id_type=MESH_ID)

        def d2d(k, j, who, hc):
            return pltpu.make_async_remote_copy(
                src_ref=region(k, who, hc), dst_ref=region(k, who, hc),
                send_sem=d2d_send.at[3 * k + j], recv_sem=d2d_recv.at[3 * k + j],
                device_id=(x, y, 1 - c), device_id_type=MESH_ID)

        peer = lambda j: 2 * _flip(x, CHIP_MOVES[j][0]) + _flip(y, CHIP_MOVES[j][1])
        locs, sends = [], []
        for k in range(n):
            loc = _Bounce(ins[k], bufs[k], _shard_slice(outs[k], axes[k], shards[k].shape[axes[k]], me),
                          in_sems.at[k], out_sems.at[k])
            loc.start()
            locs.append(loc)
        for k in range(n):
            for j in range(3):
                cp = ici(k, j, me, c)
                cp.start()
                sends.append(cp)
        for loc in locs:
            loc.turn()
        for k in range(n):
            for j in range(3):
                ici(k, j, peer(j), c).wait_recv()
                fw = d2d(k, j, peer(j), c)
                fw.start()
                sends.append(fw)
        for k in range(n):
            for j in range(3):
                d2d(k, j, peer(j), 1 - c).wait_recv()
        for cp in sends:
            cp.wait_send()
        for loc in locs:
            loc.wait()

    sems = [pltpu.SemaphoreType.DMA((3 * n,))] * 4 + [pltpu.SemaphoreType.DMA((n,))] * 2
    return pl.pallas_call(
        body, name="gather_weights",
        in_specs=[ANY] * n, out_specs=[ANY] * n,
        out_shape=[jax.ShapeDtypeStruct(fs, s.dtype) for fs, s in zip(full_shapes, shards)],
        scratch_shapes=[pltpu.VMEM(s.shape, s.dtype) for s in shards] + sems,
        compiler_params=_comm_params(sum(_nbytes(s.shape, s.dtype) for s in shards)),
    )(*shards)


class _Slab:
    def __init__(self, arrays, pick, shard_shape):
        self.arrays = arrays
        self.pick = pick
        self.rows, self.cols = shard_shape
        self.half = self.rows // 2


def _pair_exchange(slabs):
    n = len(slabs)
    n_in = sum(len(sl.arrays) for sl in slabs)

    def body(*refs):
        ins = refs[:n_in]
        mine, got = refs[n_in:n_in + n], refs[n_in + n:n_in + 2 * n]
        bufs = refs[n_in + 2 * n:n_in + 3 * n]
        send_sems, recv_sems, in_sems, out_sems = refs[n_in + 3 * n:]
        x, y, c = _place()
        started = []
        base = 0
        for k, sl in enumerate(slabs):
            for s in range(N_CHIPS):
                ai, r0, c0 = sl.pick(s)
                src = ins[base + ai]

                def half(hc):
                    return src.at[pl.ds(pl.multiple_of(r0 + hc * sl.half, 8), sl.half), pl.ds(c0, sl.cols)]
                q = N_CHIPS * k + s
                loc = _Bounce(half(c), bufs[k].at[s], mine[k].at[s], in_sems.at[q], out_sems.at[q])
                loc.start()
                cp = pltpu.make_async_remote_copy(
                    src_ref=half(1 - c), dst_ref=got[k].at[s], send_sem=send_sems.at[q], recv_sem=recv_sems.at[q],
                    device_id=(x, y, 1 - c), device_id_type=MESH_ID)
                cp.start()
                started.append((loc, cp))
            base += len(sl.arrays)
        for loc, cp in started:
            loc.turn()
        for loc, cp in started:
            cp.wait_recv()
        for loc, cp in started:
            cp.wait_send()
            loc.wait()

    flat_in = [a for sl in slabs for a in sl.arrays]
    compact = [jax.ShapeDtypeStruct((N_CHIPS, sl.half, sl.cols), F32) for sl in slabs]
    outs = pl.pallas_call(
        body, name="grad_pair_exchange",
        in_specs=[ANY] * n_in, out_specs=[ANY] * (2 * n), out_shape=compact + compact,
        scratch_shapes=[pltpu.VMEM(s.shape, F32) for s in compact] + [pltpu.SemaphoreType.DMA((N_CHIPS * n,))] * 4,
        compiler_params=_comm_params(sum(_nbytes(s.shape, F32) for s in compact)),
    )(*flat_in)
    return outs[:n], outs[n:]


def _chip_exchange(partials, vec):
    n = len(partials)

    def body(*refs):
        ins, vec_ref = refs[:n], refs[n]
        outs, vec_out = refs[n + 1:2 * n + 1], refs[2 * n + 1]
        bufs = refs[2 * n + 2:3 * n + 3]
        send_sems, recv_sems, in_sems, out_sems = refs[3 * n + 3:]
        x, y, c = _place()
        me = 2 * x + y
        dev = 2 * me + c
        locs, sends = [], []
        for k in range(n):
            loc = _Bounce(ins[k].at[me], bufs[k], outs[k].at[me], in_sems.at[k], out_sems.at[k])
            loc.start()
            locs.append(loc)
            for j, (dx, dy) in enumerate(CHIP_MOVES):
                px, py = _flip(x, dx), _flip(y, dy)
                cp = pltpu.make_async_remote_copy(
                    src_ref=ins[k].at[2 * px + py], dst_ref=outs[k].at[me],
                    send_sem=send_sems.at[3 * k + j], recv_sem=recv_sems.at[3 * k + j],
                    device_id=(px, py, c), device_id_type=MESH_ID)
                cp.start()
                sends.append(cp)
        loc = _Bounce(vec_ref, bufs[n], vec_out.at[dev], in_sems.at[n], out_sems.at[n])
        loc.start()
        locs.append(loc)
        for j, (dx, dy, dc) in enumerate(DEV_MOVES):
            cp = pltpu.make_async_remote_copy(
                src_ref=vec_ref, dst_ref=vec_out.at[dev],
                send_sem=send_sems.at[3 * n + j], recv_sem=recv_sems.at[3 * n + j],
                device_id=(_flip(x, dx), _flip(y, dy), _flip(c, dc)), device_id_type=MESH_ID)
            cp.start()
            sends.append(cp)
        for loc in locs:
            loc.turn()
        for k in range(n):
            for j, (dx, dy) in enumerate(CHIP_MOVES):
                px, py = _flip(x, dx), _flip(y, dy)
                pltpu.make_async_remote_copy(
                    src_ref=ins[k].at[me], dst_ref=outs[k].at[2 * px + py],
                    send_sem=send_sems.at[3 * k + j], recv_sem=recv_sems.at[3 * k + j],
                    device_id=(px, py, c), device_id_type=MESH_ID).wait_recv()
        for j, (dx, dy, dc) in enumerate(DEV_MOVES):
            px, py, pc = _flip(x, dx), _flip(y, dy), _flip(c, dc)
            pltpu.make_async_remote_copy(
                src_ref=vec_ref, dst_ref=vec_out.at[4 * px + 2 * py + pc],
                send_sem=send_sems.at[3 * n + j], recv_sem=recv_sems.at[3 * n + j],
                device_id=(px, py, pc), device_id_type=MESH_ID).wait_recv()
        for cp in sends:
            cp.wait_send()
        for loc in locs:
            loc.wait()

    n_sem = 3 * n + len(DEV_MOVES)
    bufs = [pltpu.VMEM(p.shape[1:], p.dtype) for p in partials] + [pltpu.VMEM(vec.shape, F32)]
    outs = pl.pallas_call(
        body, name="grad_chip_exchange",
        in_specs=[ANY] * (n + 1), out_specs=[ANY] * (n + 1),
        out_shape=[jax.ShapeDtypeStruct(p.shape, p.dtype) for p in partials] + [jax.ShapeDtypeStruct((N_DEV,) + vec.shape, F32)],
        scratch_shapes=bufs + [pltpu.SemaphoreType.DMA((n_sem,)), pltpu.SemaphoreType.DMA((n_sem,)),
                               pltpu.SemaphoreType.DMA((n + 1,)), pltpu.SemaphoreType.DMA((n + 1,))],
        compiler_params=_comm_params(sum(_nbytes(p.shape[1:], p.dtype) for p in partials) + _nbytes(vec.shape, F32)),
    )(*partials, vec)
    return outs[:n], outs[n]


def _pair_share(halves):
    n = len(halves)

    def body(*refs):
        ins, outs, bufs = refs[:n], refs[n:2 * n], refs[2 * n:3 * n]
        send_sems, recv_sems, in_sems, out_sems = refs[3 * n:]
        x, y, c = _place()
        started = []
        for k in range(n):
            hr = halves[k].shape[0]
            rows = lambda hc, k=k, hr=hr: outs[k].at[pl.ds(pl.multiple_of(hc * hr, 8), hr), :]
            loc = _Bounce(ins[k], bufs[k], rows(c), in_sems.at[k], out_sems.at[k])
            loc.start()
            cp = pltpu.make_async_remote_copy(
                src_ref=ins[k], dst_ref=rows(c), send_sem=send_sems.at[k], recv_sem=recv_sems.at[k],
                device_id=(x, y, 1 - c), device_id_type=MESH_ID)
            cp.start()
            recv = pltpu.make_async_remote_copy(
                src_ref=ins[k], dst_ref=rows(1 - c), send_sem=send_sems.at[k], recv_sem=recv_sems.at[k],
                device_id=(x, y, 1 - c), device_id_type=MESH_ID)
            started.append((loc, cp, recv))
        for loc, cp, recv in started:
            loc.turn()
        for loc, cp, recv in started:
            recv.wait_recv()
        for loc, cp, recv in started:
            cp.wait_send()
            loc.wait()

    return pl.pallas_call(
        body, name="grad_pair_share",
        in_specs=[ANY] * n, out_specs=[ANY] * n,
        out_shape=[jax.ShapeDtypeStruct((2 * h.shape[0], h.shape[1]), F32) for h in halves],
        scratch_shapes=[pltpu.VMEM(h.shape, F32) for h in halves] + [pltpu.SemaphoreType.DMA((n,))] * 4,
        compiler_params=_comm_params(sum(_nbytes(h.shape, F32) for h in halves)),
    )(*halves)


def _row_block(rows, cols, n_arrays):
    br = rows
    while br % 16 == 0 and 2 * n_arrays * br * cols * 4 > (16 << 20):
        br //= 2
    return br


def _add2(a, b, out_dtype, name):
    rows, cols = a.shape
    br = _row_block(rows, cols, 3)

    def body(a_ref, b_ref, o_ref):
        o_ref[...] = (a_ref[...] + b_ref[...]).astype(out_dtype)

    spec = pl.BlockSpec((br, cols), lambda i: (i, 0))
    return pl.pallas_call(body, name=name, grid=(rows // br,), in_specs=[spec, spec], out_specs=spec,
                          out_shape=jax.ShapeDtypeStruct(a.shape, out_dtype),
                          compiler_params=_cparams(3 * br * cols * 4, 1))(a, b)


def _sum_slots(a, name):
    n, rows, cols = a.shape
    br = _row_block(rows, cols, n + 1)

    def body(a_ref, o_ref):
        acc = a_ref[0].astype(F32)
        for s in range(1, n):
            acc = acc + a_ref[s].astype(F32)
        o_ref[...] = acc

    return pl.pallas_call(body, name=name, grid=(rows // br,),
                          in_specs=[pl.BlockSpec((n, br, cols), lambda i: (0, i, 0))],
                          out_specs=pl.BlockSpec((br, cols), lambda i: (i, 0)),
                          out_shape=jax.ShapeDtypeStruct((rows, cols), F32),
                          compiler_params=_cparams((n + 1) * br * cols * 4, 1))(a)


def _adamw_math(w, g, m, v):
    m = ADAM_B1 * m + (1.0 - ADAM_B1) * g
    v = ADAM_B2 * v + (1.0 - ADAM_B2) * (g * g)
    m_hat = m / (1.0 - ADAM_B1 ** ADAM_STEP)
    v_hat = v / (1.0 - ADAM_B2 ** ADAM_STEP)
    delta = -ADAM_LR * (m_hat / (jnp.sqrt(v_hat) + ADAM_EPS) + ADAM_WD * w)
    return delta, m, v


def _adamw(g, w, m, v, name):
    rows, cols = g.shape
    br = _row_block(rows, cols, 7)

    def body(g_ref, w_ref, m_ref, v_ref, d_ref, nm_ref, nv_ref):
        d_ref[...], nm_ref[...], nv_ref[...] = _adamw_math(w_ref[...], g_ref[...], m_ref[...], v_ref[...])

    spec = pl.BlockSpec((br, cols), lambda i: (i, 0))
    return pl.pallas_call(body, name=name, grid=(rows // br,), in_specs=[spec] * 4, out_specs=[spec] * 3,
                          out_shape=[jax.ShapeDtypeStruct(g.shape, F32)] * 3,
                          compiler_params=_cparams(7 * br * cols * 4, 1))(g, w, m, v)


ROW_FINAL_G, ROW_PE_G = 0, 1
ROW_CONV_B, ROW_CN_G, ROW_CN_B, ROW_B_PW2 = 8, 9, 10, 11
ROW_LN_G = 16
ROW_ONORM_G, ROW_LB = 24, 25
ROW_CONV_W = 32
SMALL = ["ln_g", "conv_b", "cnorm_g", "cnorm_b", "b_pw2", "onorm_g", "pe_norm_g", "final_g"]
SMALL_ROW = dict(ln_g=ROW_LN_G, conv_b=ROW_CONV_B, cnorm_g=ROW_CN_G, cnorm_b=ROW_CN_B, b_pw2=ROW_B_PW2,
                 onorm_g=ROW_ONORM_G, pe_norm_g=ROW_PE_G, final_g=ROW_FINAL_G)


def _adamw_small(vsum, gcw, lb_logits, params):
    names = SMALL + ["lb_logits", "conv_w"]
    flat = [t for nm in names for t in params[nm]]

    def body(*refs):
        vs_ref, gcw_ref, lbl_ref = refs[:3]
        ins = refs[3:3 + 3 * len(names)]
        outs = refs[3 + 3 * len(names):]
        for q, nm in enumerate(names):
            w_ref, m_ref, v_ref = ins[3 * q:3 * q + 3]
            g_ref, d_ref, nm_ref, nv_ref = outs[4 * q:4 * q + 4]
            if nm == "conv_w":
                g = gcw_ref[...]
            elif nm == "lb_logits":
                lb = _softmax_row0(lbl_ref[...])
                g0 = vs_ref[ROW_LB:ROW_LB + 1, :] * lb * (1.0 - lb)
                g = jnp.concatenate([g0, -g0], axis=0)
            else:
                g = vs_ref[SMALL_ROW[nm]:SMALL_ROW[nm] + 1, :]
            g_ref[...] = g
            d_ref[...], nm_ref[...], nv_ref[...] = _adamw_math(w_ref[...], g, m_ref[...], v_ref[...])

    out_shape = [jax.ShapeDtypeStruct(params[nm][0].shape, F32) for nm in names for _ in range(4)]
    outs = pl.pallas_call(body, name="adamw_small", out_shape=out_shape)(vsum, gcw, lb_logits, *flat)
    return {nm: tuple(outs[4 * q:4 * q + 4]) for q, nm in enumerate(names)}


TOKEN_TILE = dict(inproj_fwd=1024, conv=256, hgrn=512, tail=256, inproj_bwd_x=1024, weight_grad=1024)


def _tile(T, family):
    return min(T, TOKEN_TILE[family])


def kernel(x, p, ln_g, w_in, conv_w, conv_b, cnorm_g, cnorm_b, w_pw2, b_pw2, lb_logits, onorm_g, w_out, pe_norm_g, w_pg, w_pp, final_g, loss_target, m_ln_g, m_w_in, m_conv_w, m_conv_b, m_cnorm_g, m_cnorm_b, m_w_pw2, m_b_pw2, m_lb_logits, m_onorm_g, m_w_out, m_pe_norm_g, m_w_pg, m_w_pp, m_final_g, v_ln_g, v_w_in, v_conv_w, v_conv_b, v_cnorm_g, v_cnorm_b, v_w_pw2, v_b_pw2, v_lb_logits, v_onorm_g, v_w_out, v_pe_norm_g, v_w_pg, v_w_pp, v_final_g):
    given = dict(locals())
    x2, p2, tgt = x[0], p[0, 0], loss_target[0]
    T = x2.shape[0]
    fin_g = final_g.reshape(1, D)

    conv_w_pad = jnp.pad(conv_w[0], ((0, HALO - CONV_K), (0, 0)))
    w_in_f, w_pw2_f, w_out_f, w_pg_f, w_pp_f, conv_w_f = _all_gather_shards(
        [w_in[0].astype(_MXU), w_pw2[0].astype(_MXU), w_out[0].astype(_MXU), w_pg[0].astype(_MXU),
         w_pp[0].astype(_MXU), conv_w_pad],
        [1, 0, 0, 0, 1, 1])

    w_in_t, w_pw2_t, w_out_t, w_pg_t = w_in_f.T, w_pw2_f.T, w_out_f.T, w_pg_f.T

    z, u = _inproj_fwd(x2, ln_g, w_in_f, _tile(T, "inproj_fwd"))
    yc, y1 = _conv_fwd(z, conv_w_f, conv_b, cnorm_g, cnorm_b, w_pw2_f, b_pw2, _tile(T, "conv"))
    o_raw, yh, s_chunks = _hgrn_fwd(z, lb_logits, onorm_g, _tile(T, "hgrn"), HB)
    dyc, dyh, dh, n2, ds, dpe, dhb, pb, vec_tail, loss_part = _tail(
        x2, yc, yh, p2, tgt, w_out_f, w_out_t, w_pg_f, w_pg_t, w_pp_f, pe_norm_g, fin_g, _tile(T, "tail"))
    dzh, vec_hgrn = _hgrn_bwd(z, lb_logits, onorm_g, o_raw, dyh, s_chunks, _tile(T, "hgrn"), HB)
    dzc, a_act, dy2, vec_conv, g_conv_w = _conv_bwd(z, y1, dyc, conv_w_f, cnorm_g, cnorm_b, w_pw2_f, w_pw2_t, b_pw2, _tile(T, "conv"))
    grad_x, vec_in = _inproj_bwd_x(x2, ln_g, dzc, dzh, w_in_t, dh, _tile(T, "inproj_bwd_x"))
    tk = _tile(T, "weight_grad")
    g_w_in = _inproj_bwd_w(u, dzc, dzh, tk)
    g_w_pw2 = _tn_matmul(a_act, dy2, tk, "grad_w_pw2")
    g_w_out_c = _tn_matmul(yc, dhb, tk, "grad_w_out_conv")
    g_w_out_h = _tn_matmul(yh, dhb, tk, "grad_w_out_hgrn")
    g_w_pg = _tn_matmul(n2, ds, tk, "grad_w_pg")
    g_w_pp = _tn_matmul(pb, dpe, tk, "grad_w_pp")

    big = ["w_in", "w_pw2", "w_out", "w_pg", "w_pp"]
    slabs = [
        _Slab([g_w_in], lambda s: (0, 0, s * (NPART * D // N_CHIPS)), (D, NPART * D // N_CHIPS)),
        _Slab([g_w_pw2], lambda s: (0, s * (D // N_CHIPS), 0), (D // N_CHIPS, D)),
        _Slab([g_w_out_c, g_w_out_h], lambda s: (s // 2, (s % 2) * (D // 2), 0), (D // 2, D)),
        _Slab([g_w_pg], lambda s: (0, s * (D // N_CHIPS), 0), (D // N_CHIPS, D)),
        _Slab([g_w_pp], lambda s: (0, 0, s * (D // N_CHIPS)), (PLE, D // N_CHIPS)),
    ]
    mine, got = _pair_exchange(slabs)
    partial = [_add2(a.reshape(-1, a.shape[-1]), b.reshape(-1, b.shape[-1]), _WIRE, "pair_sum_" + nm).reshape(a.shape)
               for a, b, nm in zip(mine, got, big)]
    vec = jnp.concatenate([vec_tail, vec_conv, vec_in, vec_hgrn, g_conv_w], axis=0)
    slots, vec_slots = _chip_exchange(partial, vec)
    halves = [_sum_slots(s, "chip_sum_" + nm) for s, nm in zip(slots, big)]
    vsum = _sum_slots(vec_slots, "vec_sum")
    grads_big = _pair_share(halves)

    out = {}
    for nm, g in zip(big, grads_big):
        w2, m2, v2 = given[nm][0], given["m_" + nm][0], given["v_" + nm][0]
        d, nm_, nv_ = _adamw(g, w2, m2, v2, "adamw_" + nm)
        out[nm] = tuple(t[None] for t in (g, d, nm_, nv_))
    chip = 2 * lax.axis_index("x") + lax.axis_index("y")
    gcw = lax.dynamic_slice(vsum, (ROW_CONV_W, chip * (D // N_CHIPS)), (CONV_K, D // N_CHIPS))
    params = {nm: (given[nm].reshape(-1, D), given["m_" + nm].reshape(-1, D), given["v_" + nm].reshape(-1, D))
              for nm in SMALL + ["lb_logits"]}
    params["conv_w"] = (conv_w[0], m_conv_w[0], v_conv_w[0])
    small = _adamw_small(vsum, gcw, lb_logits, params)
    for nm, ts in small.items():
        out[nm] = tuple(t.reshape(given[nm].shape) for t in ts)

    loss = lax.psum(loss_part[0, 0], ("x", "y", "c"))
    order = ["ln_g", "w_in", "conv_w", "conv_b", "cnorm_g", "cnorm_b", "w_pw2", "b_pw2", "lb_logits", "onorm_g",
             "w_out", "pe_norm_g", "w_pg", "w_pp", "final_g"]
    return (loss, grad_x[None], *[out[nm][0] for nm in order], *[out[nm][1] for nm in order],
            *[out[nm][2] for nm in order], *[out[nm][3] for nm in order])
```

```python
import functools

import jax
import jax.numpy as jnp
from jax import lax
from jax.experimental import pallas as pl
from jax.experimental.pallas import tpu as pltpu

F32 = jnp.float32
BF16 = jnp.bfloat16
_MXU = jnp.bfloat16
_WIRE = jnp.bfloat16

D = 1024
NPART = 7
PLE = 256
HEADS = 8
HD = 128
CHUNK = 64
CONV_K = 31
HALO = 32
EPS = 1e-6
N_CHIPS = 4
N_DEV = 8
HB = 8
VEC_ROWS = 64

ADAM_LR = 0.001
ADAM_B1 = 0.9
ADAM_B2 = 0.999
ADAM_EPS = 1e-08
ADAM_WD = 0.01
ADAM_STEP = 10

V7X_VMEM_LIMIT = 60000 * 1024
MESH_ID = pl.DeviceIdType.MESH
ANY = pl.BlockSpec(memory_space=pl.ANY)


def _cparams(block_bytes, n_grid_dims):
    limit = min(V7X_VMEM_LIMIT, 2 * block_bytes + (24 << 20))
    return pltpu.CompilerParams(vmem_limit_bytes=int(limit), dimension_semantics=("arbitrary",) * n_grid_dims)


def _nbytes(shape, dtype):
    n = 1
    for s in shape:
        n *= s
    return n * jnp.dtype(dtype).itemsize


def _dot(a, b):
    return jnp.dot(a.astype(_MXU), b.astype(_MXU), preferred_element_type=F32)


def _dot_nt(a, b):
    return lax.dot_general(a.astype(_MXU), b.astype(_MXU), (((1,), (1,)), ((), ())), preferred_element_type=F32)


def _dot_tn(a, b):
    return lax.dot_general(a.astype(_MXU), b.astype(_MXU), (((0,), (0,)), ((), ())), preferred_element_type=F32)


def _tri_dot(tri_bf, x):
    x1 = x.astype(BF16)
    r1 = x - x1.astype(F32)
    x2 = r1.astype(BF16)
    x3 = (r1 - x2.astype(F32)).astype(BF16)
    d = lambda t: jnp.dot(tri_bf, t, preferred_element_type=F32)
    return d(x1) + d(x2) + d(x3)


def _split2(x):
    hi = x.astype(BF16)
    return hi, (x - hi.astype(F32)).astype(BF16)


def _dot3(dims, a, b):
    d = lambda p, q: lax.dot_general(p, q, (dims, ((), ())), preferred_element_type=F32)
    return d(a[0], b[0]) + d(a[0], b[1]) + d(a[1], b[0])


def _sigmoid(x):
    return jax.nn.sigmoid(x)


def _mean_lanes(x):
    return jnp.mean(x, axis=-1, keepdims=True)


def _sum_rows(x):
    return jnp.sum(x, axis=0, keepdims=True)


def _group_ln(y):
    yn, rs = [], []
    for g in range(D // HD):
        blk = y[:, g * HD:(g + 1) * HD]
        xc = blk - _mean_lanes(blk)
        r = lax.rsqrt(_mean_lanes(xc * xc) + EPS)
        yn.append(xc * r)
        rs.append(jnp.broadcast_to(r, blk.shape))
    return jnp.concatenate(yn, axis=1), jnp.concatenate(rs, axis=1)


def _group_ln_bwd(dyn, yn, rstd):
    out = []
    for g in range(D // HD):
        sl = slice(g * HD, (g + 1) * HD)
        d, n = dyn[:, sl], yn[:, sl]
        out.append(rstd[:, sl] * (d - _mean_lanes(d) - n * _mean_lanes(d * n)))
    return jnp.concatenate(out, axis=1)


def _head_means(x, hb, fn=lambda m: m):
    return jnp.concatenate([jnp.broadcast_to(fn(_mean_lanes(x[:, hh * HD:(hh + 1) * HD])), (x.shape[0], HD))
                            for hh in range(hb)], axis=1)


def _head_rsqrt_mean(x, hb):
    return _head_means(x, hb, lambda m: lax.rsqrt(m + EPS))


def _softmax_row0(lbl):
    m = jnp.max(lbl, axis=0, keepdims=True)
    e = jnp.exp(lbl - m)
    return e[0:1, :] / jnp.sum(e, axis=0, keepdims=True)


def _inproj_fwd(x, ln_g, w_in, shards, axes, tT):
    T = x.shape[0]
    n = len(shards)
    steps = (T // tT) * NPART
    at = dict(start=0, turn=steps // 4, forward=steps // 2, finish=steps - 1)

    def body(x_ref, g_ref, w_ref, *rest):
        ins, (z_ref, u_ref), outs = rest[:n], rest[n:n + 2], rest[n + 2:2 * n + 2]
        u_scr, bufs, sems = rest[2 * n + 2], rest[2 * n + 3:3 * n + 3], rest[3 * n + 3:]
        step = pl.program_id(0) * NPART + pl.program_id(1)
        gather = _Gather([s.shape for s in shards], axes, ins, outs, bufs, sems)
        for phase in ("start", "turn", "forward"):
            pl.when(step == at[phase])(getattr(gather, phase))

        @pl.when(pl.program_id(1) == 0)
        def _():
            xv = x_ref[...]
            r = lax.rsqrt(_mean_lanes(xv * xv) + EPS)
            u = (xv * r * g_ref[...]).astype(_MXU)
            u_scr[...] = u
            u_ref[...] = u
        z_ref[...] = jnp.dot(u_scr[...], w_ref[...], preferred_element_type=F32)
        pl.when(step == at["finish"])(gather.finish)

    blk = (_nbytes((tT, D), F32) * 2 + _nbytes((D, D), _MXU) + _nbytes((tT, D), _MXU) * 2
           + sum(_nbytes(s.shape, s.dtype) for s in shards))
    outs = pl.pallas_call(
        body, name="inproj_fwd", grid=(T // tT, NPART),
        in_specs=[pl.BlockSpec((tT, D), lambda i, j: (i, 0)), pl.BlockSpec((1, D), lambda i, j: (0, 0)),
                  pl.BlockSpec((D, D), lambda i, j: (0, j))] + [ANY] * n,
        out_specs=[pl.BlockSpec((tT, D), lambda i, j: (i, j)), pl.BlockSpec((tT, D), lambda i, j: (i, 0))] + [ANY] * n,
        out_shape=[jax.ShapeDtypeStruct((T, NPART * D), F32), jax.ShapeDtypeStruct((T, D), _MXU)]
        + [jax.ShapeDtypeStruct(fs, s.dtype) for fs, s in zip(_full_shapes(shards, axes), shards)],
        scratch_shapes=[pltpu.VMEM((tT, D), _MXU)] + _Gather.scratch(shards),
        compiler_params=_cparams(blk, 2),
    )(x, ln_g, w_in, *shards)
    return outs[0], outs[1], outs[2:]


def _shifted_windows(ext, first, visit):
    n = ext.shape[0]
    for m in range(first, first + CONV_K):
        visit(m, (ext if m == 0 else pltpu.roll(ext, n - m, axis=0))[0:n - HALO, :])


def _conv_fwd(z, conv_w, conv_b, cn_g, cn_b, w_pw2, b_pw2, tT):
    T = z.shape[0]

    def body(cv_ref, cg_ref, ct_ref, cw_ref, cb_ref, ng_ref, nb_ref, wp_ref, bp_ref, yc_ref, y1_ref, ext):
        @pl.when(pl.program_id(0) == 0)
        def _():
            ext[...] = jnp.zeros_like(ext)
        ext[0:HALO, :] = ext[tT:tT + HALO, :]
        ext[HALO:, :] = cv_ref[...] * _sigmoid(cg_ref[...])
        cw = cw_ref[...]
        acc = [cb_ref[...]]

        def tap(m, win):
            acc[0] = acc[0] + win * cw[m - 2:m - 1, :]
        _shifted_windows(ext[...], 2, tap)
        y1 = acc[0]
        y1_ref[...] = y1
        yn, _ = _group_ln(y1)
        apre = yn * ng_ref[...] + nb_ref[...]
        a = apre * _sigmoid(apre)
        y2 = _dot(a, wp_ref[...]) + bp_ref[...]
        ct = ct_ref[...]
        yc_ref[...] = (y2 * (ct * _sigmoid(ct))).astype(_MXU)

    part = lambda p: pl.BlockSpec((tT, D), lambda i: (i, p))
    row = pl.BlockSpec((1, D), lambda i: (0, 0))
    tok = pl.BlockSpec((tT, D), lambda i: (i, 0))
    blk = 4 * _nbytes((tT, D), F32) + _nbytes((D, D), _MXU) + _nbytes((tT, D), _MXU) + 8 * _nbytes((tT + HALO, D), F32)
    return pl.pallas_call(
        body, name="conv_fwd", grid=(T // tT,),
        in_specs=[part(0), part(1), part(2), pl.BlockSpec((HALO, D), lambda i: (0, 0)), row, row, row,
                  pl.BlockSpec((D, D), lambda i: (0, 0)), row],
        out_specs=[tok, tok],
        out_shape=[jax.ShapeDtypeStruct((T, D), _MXU), jax.ShapeDtypeStruct((T, D), F32)],
        scratch_shapes=[pltpu.VMEM((tT + HALO, D), F32)],
        compiler_params=_cparams(blk, 1),
    )(z, z, z, conv_w, conv_b, cn_g, cn_b, w_pw2, b_pw2)


def _hgrn_gates(lb, hq, hf):
    sq = _sigmoid(hq)
    sg = _sigmoid(hf)
    f = lb + (1.0 - lb) * sg
    return sq, sg, f, hq * sq, (1.0 - lb) * (1.0 - sg), jnp.log(f)


def _chunk_decays(lf, q, k):
    r = lax.broadcasted_iota(jnp.int32, (CHUNK, CHUNK), 0)
    c = lax.broadcasted_iota(jnp.int32, (CHUNK, CHUNK), 1)
    b = _tri_dot((r >= c).astype(BF16), lf)
    bm = b[CHUNK // 2 - 1:CHUNK // 2, :]
    bl = b[CHUNK - 1:CHUNK, :]
    eb = jnp.exp(b)
    eqm = jnp.exp(b - bm)
    ekm = jnp.exp(bm - b)
    ekd = jnp.exp(bl - b)
    return dict(causal=r >= c, eb=eb, eqm=eqm, ekm=ekm, ekd=ekd, ebl=jnp.exp(bl),
                qd=q * eb, qm=q * eqm, km=k * ekm, kd=k * ekd)


def _hgrn_fwd(z, lb_logits, onorm_g, tT, hb):
    T = z.shape[0]
    nc = tT // CHUNK
    w = hb * HD

    def body(lbl_ref, og_ref, hq_ref, hf_ref, hi_ref, hg_ref, o_ref, yh_ref, sc_ref, st):
        @pl.when(pl.program_id(1) == 0)
        def _():
            st[...] = jnp.zeros_like(st)
        lb_all = _softmax_row0(lbl_ref[...])
        og_all = og_ref[...]

        def chunk(c, carry):
            sl = pl.ds(pl.multiple_of(c * CHUNK, CHUNK), CHUNK)
            lanes = [slice(hh * HD, (hh + 1) * HD) for hh in range(hb)]
            heads = lambda fn: [fn(hh, ln) for hh, ln in enumerate(lanes)]
            hg, v = hg_ref[sl, :], hi_ref[sl, :]
            _, _, _, q, k, lf = _hgrn_gates(lb_all, hq_ref[sl, :], hf_ref[sl, :])
            dc = _chunk_decays(lf, q, k)
            s_t = heads(lambda hh, ln: st[hh])
            a = heads(lambda hh, ln: jnp.where(dc["causal"], _dot_nt(dc["qm"][:, ln], dc["km"][:, ln]), 0.0))
            o_inter = heads(lambda hh, ln: _dot_nt(dc["qd"][:, ln], s_t[hh]))
            kv = heads(lambda hh, ln: _dot_tn(v[:, ln], dc["kd"][:, ln]))
            o_intra = heads(lambda hh, ln: _dot(a[hh], v[:, ln]))
            for hh, ln in enumerate(lanes):
                sc_ref[hh, c] = s_t[hh]
                st[hh] = s_t[hh] * dc["ebl"][:, ln] + kv[hh]
            o = jnp.concatenate([o_inter[hh] + o_intra[hh] for hh in range(hb)], axis=1)
            o_ref[sl, :] = o
            n = o * _head_rsqrt_mean(o * o, hb)
            yh_ref[sl, :] = ((n * og_all) * (hg * _sigmoid(hg))).astype(_MXU)
            return carry

        lax.fori_loop(0, nc, chunk, 0)

    zpart = lambda p: pl.BlockSpec((tT, w), lambda h, i: (i, p * (HEADS // hb) + h))
    blk = 6 * _nbytes((tT, w), F32) + _nbytes((hb, nc, HD, HD), F32)
    return pl.pallas_call(
        body, name="hgrn_fwd", grid=(HEADS // hb, T // tT),
        in_specs=[pl.BlockSpec((2, w), lambda h, i: (0, h)), pl.BlockSpec((1, w), lambda h, i: (0, h)),
                  zpart(3), zpart(4), zpart(5), zpart(6)],
        out_specs=[pl.BlockSpec((tT, w), lambda h, i: (i, h)), pl.BlockSpec((tT, w), lambda h, i: (i, h)),
                   pl.BlockSpec((hb, nc, HD, HD), lambda h, i: (h, i, 0, 0))],
        out_shape=[jax.ShapeDtypeStruct((T, D), F32), jax.ShapeDtypeStruct((T, D), _MXU),
                   jax.ShapeDtypeStruct((HEADS, T // CHUNK, HD, HD), F32)],
        scratch_shapes=[pltpu.VMEM((hb, HD, HD), F32)],
        compiler_params=_cparams(blk, 2),
    )(lb_logits, onorm_g, z, z, z, z)


def _hgrn_bwd(z, lb_logits, onorm_g, o_raw, dyh, s_chunks, tT, hb):
    T = z.shape[0]
    nc = tT // CHUNK
    nI = T // tT
    w = hb * HD

    def body(lbl_ref, og_ref, hq_ref, hf_ref, hi_ref, hg_ref, o_ref, dy_ref, sc_ref, dz_ref, vec_ref, dst):
        @pl.when(pl.program_id(1) == 0)
        def _():
            dst[...] = jnp.zeros_like(dst)
            vec_ref[...] = jnp.zeros_like(vec_ref)
        lb_all = _softmax_row0(lbl_ref[...])
        og_all = og_ref[...]
        last_row = lax.broadcasted_iota(jnp.int32, (CHUNK, w), 0) == CHUNK - 1
        r64 = lax.broadcasted_iota(jnp.int32, (CHUNK, CHUNK), 0)
        c64 = lax.broadcasted_iota(jnp.int32, (CHUNK, CHUNK), 1)
        upper = (c64 >= r64).astype(BF16)
        lanes = [slice(hh * HD, (hh + 1) * HD) for hh in range(hb)]
        heads = lambda fn: [fn(hh, ln) for hh, ln in enumerate(lanes)]
        wide = lambda parts: jnp.concatenate(parts, axis=1)

        def chunk(cc, carry):
            c = nc - 1 - cc
            sl = pl.ds(pl.multiple_of(c * CHUNK, CHUNK), CHUNK)
            hq, hg, v = hq_ref[sl, :], hg_ref[sl, :], hi_ref[sl, :]
            sq, sg, f, q, k, lf = _hgrn_gates(lb_all, hq, hf_ref[sl, :])
            dc = _chunk_decays(lf, q, k)
            s_t = heads(lambda hh, ln: sc_ref[hh, c])
            ds_t = heads(lambda hh, ln: dst[hh])
            o, dy = o_ref[sl, :], dy_ref[sl, :]
            r = _head_rsqrt_mean(o * o, hb)
            n = o * r
            sgg = _sigmoid(hg)
            silu_g = hg * sgg
            dhg = dy * (n * og_all) * (sgg * (1.0 + hg * (1.0 - sgg)))
            dn = dy * og_all * silu_g
            g_og = _sum_rows(dy * n * silu_g)
            do = r * (dn - n * _head_means(dn * n, hb))
            a = heads(lambda hh, ln: jnp.where(dc["causal"], _dot_nt(dc["qm"][:, ln], dc["km"][:, ln]), 0.0))
            dam = heads(lambda hh, ln: jnp.where(dc["causal"], _dot_nt(do[:, ln], v[:, ln]), 0.0))
            dqd = wide(heads(lambda hh, ln: _dot(do[:, ln], s_t[hh])))
            dkd = wide(heads(lambda hh, ln: _dot(v[:, ln], ds_t[hh])))
            dv_inter = heads(lambda hh, ln: _dot_nt(dc["kd"][:, ln], ds_t[hh]))
            dqs = heads(lambda hh, ln: _dot_tn(do[:, ln], dc["qd"][:, ln]))
            dv = wide(heads(lambda hh, ln: _dot_tn(a[hh], do[:, ln]) + dv_inter[hh]))
            dam2 = [_split2(t) for t in dam]
            km2, qm2 = _split2(dc["km"]), _split2(dc["qm"])
            dqm = wide(heads(lambda hh, ln: _dot3(((1,), (0,)), dam2[hh], (km2[0][:, ln], km2[1][:, ln]))))
            dkm = wide(heads(lambda hh, ln: _dot3(((0,), (0,)), dam2[hh], (qm2[0][:, ln], qm2[1][:, ln]))))
            debl = wide(heads(lambda hh, ln: _sum_rows(ds_t[hh] * s_t[hh])))
            for hh, ln in enumerate(lanes):
                dst[hh] = ds_t[hh] * dc["ebl"][:, ln] + dqs[hh]
            dq = dqd * dc["eb"] + dqm * dc["eqm"]
            dk = dkm * dc["ekm"] + dkd * dc["ekd"]
            dbl = _sum_rows(dkd * dc["kd"]) + debl * dc["ebl"]
            db = dq * q - dk * k + jnp.where(last_row, dbl, 0.0)
            dlf = _tri_dot(upper, db)
            dfk = dlf / f - dk
            dz_ref[0, sl, :] = (dq * (sq * (1.0 + hq * (1.0 - sq)))).astype(_MXU)
            dz_ref[1, sl, :] = (dfk * ((1.0 - lb_all) * sg * (1.0 - sg))).astype(_MXU)
            dz_ref[2, sl, :] = dv.astype(_MXU)
            dz_ref[3, sl, :] = dhg.astype(_MXU)
            vec_ref[0:1, :] += g_og
            vec_ref[1:2, :] += _sum_rows(dfk * (1.0 - sg))
            return carry

        lax.fori_loop(0, nc, chunk, 0)

    zpart = lambda p: pl.BlockSpec((tT, w), lambda h, i: (nI - 1 - i, p * (HEADS // hb) + h))
    act = pl.BlockSpec((tT, w), lambda h, i: (nI - 1 - i, h))
    blk = 6 * _nbytes((tT, w), F32) + _nbytes((hb, nc, HD, HD), F32) + 4 * _nbytes((tT, w), _MXU)
    return pl.pallas_call(
        body, name="hgrn_bwd", grid=(HEADS // hb, nI),
        in_specs=[pl.BlockSpec((2, w), lambda h, i: (0, h)), pl.BlockSpec((1, w), lambda h, i: (0, h)),
                  zpart(3), zpart(4), zpart(5), zpart(6), act, act,
                  pl.BlockSpec((hb, nc, HD, HD), lambda h, i: (h, nI - 1 - i, 0, 0))],
        out_specs=[pl.BlockSpec((4, tT, w), lambda h, i: (0, nI - 1 - i, h)),
                   pl.BlockSpec((8, w), lambda h, i: (0, h))],
        out_shape=[jax.ShapeDtypeStruct((4, T, D), _MXU), jax.ShapeDtypeStruct((8, D), F32)],
        scratch_shapes=[pltpu.VMEM((hb, HD, HD), F32)],
        compiler_params=_cparams(blk, 2),
    )(lb_logits, onorm_g, z, z, z, z, o_raw, dyh, s_chunks)


def _tail(x, yc, yh, p, target, w_out, w_pg, w_pp, pe_g, fin_g, tT):
    T = x.shape[0]

    def body(x_ref, yc_ref, yh_ref, p_ref, t_ref, wo_ref, wg_ref, wp_ref, pg_ref, fg_ref,
             dyc_ref, dyh_ref, dh_ref, n2_ref, ds_ref, dpe_ref, dhb_ref, pb_ref, vec_ref):
        @pl.when(pl.program_id(0) == 0)
        def _():
            vec_ref[...] = jnp.zeros_like(vec_ref)
        wo_c, wo_h = wo_ref[0:D, :], wo_ref[D:2 * D, :]
        h = x_ref[...] + _dot(yc_ref[...], wo_c) + _dot(yh_ref[...], wo_h)
        pb = p_ref[...].astype(_MXU)
        pe = _dot(pb, wp_ref[...])
        r2 = lax.rsqrt(_mean_lanes(h * h) + EPS)
        hn = h * r2
        n2 = (hn * pg_ref[...]).astype(_MXU)
        gate = _sigmoid(_dot(n2, wg_ref[...]))
        h2 = h + gate * pe
        r3 = lax.rsqrt(_mean_lanes(h2 * h2) + EPS)
        h2n = h2 * r3
        err = h2n * fg_ref[...] - t_ref[...]
        vec_ref[ROW_LOSS:ROW_LOSS + 1, :] += 0.5 * jnp.sum(_mean_lanes(err * err))
        dout = err * (1.0 / D)
        vec_ref[0:1, :] += _sum_rows(dout * h2n)
        dn3 = dout * fg_ref[...]
        dh2 = r3 * (dn3 - h2n * _mean_lanes(dn3 * h2n))
        ds = (dh2 * pe * gate * (1.0 - gate)).astype(_MXU)
        dn2 = _dot_nt(ds, wg_ref[...])
        vec_ref[1:2, :] += _sum_rows(dn2 * hn)
        dnn = dn2 * pg_ref[...]
        dh = dh2 + r2 * (dnn - hn * _mean_lanes(dnn * hn))
        dhb = dh.astype(_MXU)
        dyc_ref[...] = _dot_nt(dhb, wo_c)
        dyh_ref[...] = _dot_nt(dhb, wo_h)
        dh_ref[...] = dh
        n2_ref[...] = n2
        ds_ref[...] = ds
        dpe_ref[...] = (dh2 * gate).astype(_MXU)
        dhb_ref[...] = dhb
        pb_ref[...] = pb

    tok = lambda w: pl.BlockSpec((tT, w), lambda i: (i, 0))
    full = lambda r, c: pl.BlockSpec((r, c), lambda i: (0, 0))
    tokshape = lambda w, dt: jax.ShapeDtypeStruct((T, w), dt)
    blk = (5 * _nbytes((tT, D), F32) + 7 * _nbytes((tT, D), _MXU) + _nbytes((4 * D + PLE, D), _MXU)
           + 12 * _nbytes((tT, D), F32))
    return pl.pallas_call(
        body, name="tail_fwd_bwd", grid=(T // tT,),
        in_specs=[tok(D), tok(D), tok(D), tok(PLE), tok(D), full(2 * D, D), full(D, D), full(PLE, D), full(1, D), full(1, D)],
        out_specs=[tok(D), tok(D), tok(D), tok(D), tok(D), tok(D), tok(D), tok(PLE), full(8, D)],
        out_shape=[tokshape(D, F32), tokshape(D, F32), tokshape(D, F32), tokshape(D, _MXU), tokshape(D, _MXU),
                   tokshape(D, _MXU), tokshape(D, _MXU), tokshape(PLE, _MXU),
                   jax.ShapeDtypeStruct((8, D), F32)],
        compiler_params=_cparams(blk, 1),
    )(x, yc, yh, p, target, w_out, w_pg, w_pp, pe_g, fin_g)


def _conv_bwd(z, y1, dyc, conv_w, cn_g, cn_b, w_pw2, b_pw2, tT):
    T = z.shape[0]
    nI = T // tT
    hb = tT // HALO

    def body(cv_ref, cg_ref, ct_ref, hv_ref, hg_ref, y1_ref, dyc_ref, cw_ref, ng_ref, nb_ref, wp_ref, bp_ref,
             dz_ref, a_ref, dy2_ref, vec_ref, gcw_ref, ext, ext2, gpart):
        i = pl.program_id(0)

        @pl.when(i == 0)
        def _():
            ext2[...] = jnp.zeros_like(ext2)
            gpart[...] = jnp.zeros_like(gpart)
            vec_ref[...] = jnp.zeros_like(vec_ref)
        cv, cg, ct = cv_ref[...], cg_ref[...], ct_ref[...]
        sg = _sigmoid(cg)
        has_hist = (i < nI - 1).astype(F32)
        ext[0:HALO, :] = hv_ref[...] * _sigmoid(hg_ref[...]) * has_hist
        ext[HALO:, :] = cv * sg
        yn, rstd = _group_ln(y1_ref[...])
        apre = yn * ng_ref[...] + nb_ref[...]
        sa = _sigmoid(apre)
        a = (apre * sa).astype(_MXU)
        y2 = _dot(a, wp_ref[...]) + bp_ref[...]
        st = _sigmoid(ct)
        dyc_v = dyc_ref[...]
        dy2 = dyc_v * (ct * st)
        dy2b = dy2.astype(_MXU)
        da = _dot_nt(dy2b, wp_ref[...])
        dapre = da * (sa * (1.0 + apre * (1.0 - sa)))
        dy1 = _group_ln_bwd(dapre * ng_ref[...], yn, rstd)
        vec_ref[0:1, :] += _sum_rows(dy1)
        vec_ref[1:2, :] += _sum_rows(dapre * yn)
        vec_ref[2:3, :] += _sum_rows(dapre)
        vec_ref[3:4, :] += _sum_rows(dy2)
        dz_ref[2] = (dyc_v * y2 * (st * (1.0 + ct * (1.0 - st)))).astype(_MXU)
        a_ref[...] = a
        dy2_ref[...] = dy2b
        ext2[tT:tT + HALO, :] = ext2[0:HALO, :]
        ext2[0:tT, :] = dy1
        def grad_tap(m, win):
            p = dy1 * win
            part = p[0:8, :]
            for q in range(1, tT // 8):
                part = part + p[8 * q:8 * q + 8, :]
            gpart[m - 2] += part
        _shifted_windows(ext[...], 2, grad_tap)
        cw = cw_ref[...]
        acc = [None]

        def dv_tap(m, win):
            term = win * cw[CONV_K - 1 - m:CONV_K - m, :]
            acc[0] = term if acc[0] is None else acc[0] + term
        _shifted_windows(ext2[...], 0, dv_tap)
        dv = acc[0]
        dz_ref[0] = (dv * sg).astype(_MXU)
        dz_ref[1] = (dv * cv * sg * (1.0 - sg)).astype(_MXU)

        @pl.when(i == nI - 1)
        def _():
            gcw_ref[...] = jnp.sum(gpart[...], axis=1)

    part = lambda p: pl.BlockSpec((tT, D), lambda i: (nI - 1 - i, p))
    hist = lambda p: pl.BlockSpec((HALO, D), lambda i: (jnp.maximum((nI - 1 - i) * hb - 1, 0), p))
    tok = pl.BlockSpec((tT, D), lambda i: (nI - 1 - i, 0))
    row = pl.BlockSpec((1, D), lambda i: (0, 0))
    blk = (5 * _nbytes((tT, D), F32) + _nbytes((D, D), _MXU) + 5 * _nbytes((tT, D), _MXU)
           + 10 * _nbytes((tT + HALO, D), F32))
    return pl.pallas_call(
        body, name="conv_bwd", grid=(nI,),
        in_specs=[part(0), part(1), part(2), hist(0), hist(1), tok, tok, pl.BlockSpec((HALO, D), lambda i: (0, 0)),
                  row, row, pl.BlockSpec((D, D), lambda i: (0, 0)), row],
        out_specs=[pl.BlockSpec((3, tT, D), lambda i: (0, nI - 1 - i, 0)), tok, tok,
                   pl.BlockSpec((8, D), lambda i: (0, 0)), pl.BlockSpec((HALO, D), lambda i: (0, 0))],
        out_shape=[jax.ShapeDtypeStruct((3, T, D), _MXU), jax.ShapeDtypeStruct((T, D), _MXU),
                   jax.ShapeDtypeStruct((T, D), _MXU), jax.ShapeDtypeStruct((8, D), F32),
                   jax.ShapeDtypeStruct((HALO, D), F32)],
        scratch_shapes=[pltpu.VMEM((tT + HALO, D), F32), pltpu.VMEM((tT + HALO, D), F32), pltpu.VMEM((HALO, 8, D), F32)],
        compiler_params=_cparams(blk, 1),
    )(z, z, z, z, z, y1, dyc, conv_w, cn_g, cn_b, w_pw2, b_pw2)


def _inproj_bwd_x(x, ln_g, dzc, dzh, w_in, dh, tT):
    T = x.shape[0]

    def body(x_ref, g_ref, dzc_ref, dzh_ref, w_ref, dh_ref, gx_ref, vec_ref, du):
        i, j = pl.program_id(0), pl.program_id(1)

        @pl.when(j == 0)
        def _():
            du[...] = jnp.zeros_like(du)

        @pl.when(jnp.logical_and(i == 0, j == 0))
        def _():
            vec_ref[...] = jnp.zeros_like(vec_ref)

        @pl.when(j < 3)
        def _():
            du[...] += _dot_nt(dzc_ref[0], w_ref[...])

        @pl.when(j >= 3)
        def _():
            du[...] += _dot_nt(dzh_ref[0], w_ref[...])

        @pl.when(j == NPART - 1)
        def _():
            xv = x_ref[...]
            r = lax.rsqrt(_mean_lanes(xv * xv) + EPS)
            xn = xv * r
            duv = du[...]
            vec_ref[0:1, :] += _sum_rows(duv * xn)
            dun = duv * g_ref[...]
            gx_ref[...] = dh_ref[...] + r * (dun - xn * _mean_lanes(dun * xn))

    tok = pl.BlockSpec((tT, D), lambda i, j: (i, 0))
    blk = 3 * _nbytes((tT, D), F32) + 2 * _nbytes((tT, D), _MXU) + _nbytes((D, D), _MXU) + 4 * _nbytes((tT, D), F32)
    return pl.pallas_call(
        body, name="inproj_bwd_x", grid=(T // tT, NPART),
        in_specs=[tok, pl.BlockSpec((1, D), lambda i, j: (0, 0)),
                  pl.BlockSpec((1, tT, D), lambda i, j: (jnp.minimum(j, 2), i, 0)),
                  pl.BlockSpec((1, tT, D), lambda i, j: (jnp.maximum(j - 3, 0), i, 0)),
                  pl.BlockSpec((D, D), lambda i, j: (0, j)), tok],
        out_specs=[tok, pl.BlockSpec((8, D), lambda i, j: (0, 0))],
        out_shape=[jax.ShapeDtypeStruct((T, D), F32), jax.ShapeDtypeStruct((8, D), F32)],
        scratch_shapes=[pltpu.VMEM((tT, D), F32)],
        compiler_params=_cparams(blk, 2),
    )(x, ln_g, dzc, dzh, w_in, dh)


def _inproj_bwd_w(u, dzc, dzh, tk):
    T = u.shape[0]
    nK = T // tk

    def body(u_ref, dzc_ref, dzh_ref, gw_ref):
        j, k = pl.program_id(0), pl.program_id(1)

        @pl.when(k == 0)
        def _():
            gw_ref[...] = jnp.zeros_like(gw_ref)

        @pl.when(j < 3)
        def _():
            gw_ref[...] += _dot_tn(u_ref[...], dzc_ref[0])

        @pl.when(j >= 3)
        def _():
            gw_ref[...] += _dot_tn(u_ref[...], dzh_ref[0])

    blk = 3 * _nbytes((tk, D), _MXU) + 2 * _nbytes((D, D), F32)
    return pl.pallas_call(
        body, name="inproj_bwd_w", grid=(NPART, nK),
        in_specs=[pl.BlockSpec((tk, D), lambda j, k: (k, 0)),
                  pl.BlockSpec((1, tk, D), lambda j, k: (jnp.minimum(j, 2), jnp.where(j < 3, k, nK - 1), 0)),
                  pl.BlockSpec((1, tk, D), lambda j, k: (jnp.maximum(j - 3, 0), jnp.where(j < 3, 0, k), 0))],
        out_specs=pl.BlockSpec((D, D), lambda j, k: (0, j)),
        out_shape=jax.ShapeDtypeStruct((D, NPART * D), F32),
        compiler_params=_cparams(blk, 2),
    )(u, dzc, dzh)


def _tn_matmul(a, b, tk, name):
    T, M = a.shape
    N = b.shape[1]

    def body(a_ref, b_ref, o_ref):
        @pl.when(pl.program_id(0) == 0)
        def _():
            o_ref[...] = jnp.zeros_like(o_ref)
        o_ref[...] += _dot_tn(a_ref[...], b_ref[...])

    blk = _nbytes((tk, M), _MXU) + _nbytes((tk, N), _MXU) + 2 * _nbytes((M, N), F32)
    return pl.pallas_call(
        body, name=name, grid=(T // tk,),
        in_specs=[pl.BlockSpec((tk, M), lambda k: (k, 0)), pl.BlockSpec((tk, N), lambda k: (k, 0))],
        out_specs=pl.BlockSpec((M, N), lambda k: (0, 0)),
        out_shape=jax.ShapeDtypeStruct((M, N), F32),
        compiler_params=_cparams(blk, 1),
    )(a, b)


def _place():
    return lax.axis_index("x"), lax.axis_index("y"), lax.axis_index("c")


def _flip(v, d):
    return 1 - v if d else v


CHIP_MOVES = [(1, 0), (0, 1), (1, 1)]
DEV_MOVES = [(dx, dy, dc) for dx in (0, 1) for dy in (0, 1) for dc in (0, 1)][1:]


def _shard_slice(ref, axis, size, s):
    start = pl.multiple_of(s * size, size)
    return ref.at[pl.ds(start, size), :] if axis == 0 else ref.at[:, pl.ds(start, size)]


class _Bounce:
    def __init__(self, src, buf, dst, sem_in, sem_out):
        self.load = pltpu.make_async_copy(src, buf, sem_in)
        self.store = pltpu.make_async_copy(buf, dst, sem_out)

    def start(self):
        self.load.start()

    def turn(self):
        self.load.wait()
        self.store.start()

    def wait(self):
        self.store.wait()


def _comm_params(scratch_bytes):
    return pltpu.CompilerParams(vmem_limit_bytes=int(min(V7X_VMEM_LIMIT, scratch_bytes + (8 << 20))))


class _Gather:
    def __init__(self, shapes, axes, ins, outs, bufs, sems):
        self.shapes, self.axes, self.ins, self.outs, self.bufs = shapes, axes, ins, outs, bufs
        self.ici_send, self.ici_recv, self.d2d_send, self.d2d_recv, self.in_sems, self.out_sems = sems
        self.x, self.y, self.c = _place()
        self.me = 2 * self.x + self.y
        self.pairs = [(k, j) for k in range(len(shapes)) for j in range(3)]

    @staticmethod
    def scratch(shards):
        n = len(shards)
        return ([pltpu.VMEM(s.shape, s.dtype) for s in shards]
                + [pltpu.SemaphoreType.DMA((3 * n,))] * 4 + [pltpu.SemaphoreType.DMA((n,))] * 2)

    def _own_half(self, k, hc):
        half = self.shapes[k][0] // 2
        return self.ins[k].at[pl.ds(pl.multiple_of(hc * half, 16), half), :]

    def _region(self, k, who, hc):
        rows, cols = self.shapes[k]
        half = rows // 2
        if self.axes[k] == 0:
            return self.outs[k].at[pl.ds(pl.multiple_of(who * rows + hc * half, 16), half), :]
        return self.outs[k].at[pl.ds(pl.multiple_of(hc * half, 16), half), pl.ds(pl.multiple_of(who * cols, HD), cols)]

    def _peer(self, j):
        return 2 * _flip(self.x, CHIP_MOVES[j][0]) + _flip(self.y, CHIP_MOVES[j][1])

    def _ici(self, k, j, who, hc):
        dx, dy = CHIP_MOVES[j]
        return pltpu.make_async_remote_copy(
            src_ref=self._own_half(k, hc), dst_ref=self._region(k, who, hc),
            send_sem=self.ici_send.at[3 * k + j], recv_sem=self.ici_recv.at[3 * k + j],
            device_id=(_flip(self.x, dx), _flip(self.y, dy), self.c), device_id_type=MESH_ID)

    def _d2d(self, k, j, who, hc):
        return pltpu.make_async_remote_copy(
            src_ref=self._region(k, who, hc), dst_ref=self._region(k, who, hc),
            send_sem=self.d2d_send.at[3 * k + j], recv_sem=self.d2d_recv.at[3 * k + j],
            device_id=(self.x, self.y, 1 - self.c), device_id_type=MESH_ID)

    def _local(self, k):
        size = self.shapes[k][self.axes[k]]
        return _Bounce(self.ins[k], self.bufs[k], _shard_slice(self.outs[k], self.axes[k], size, self.me),
                       self.in_sems.at[k], self.out_sems.at[k])

    def start(self):
        for k in range(len(self.shapes)):
            self._local(k).start()
        for k, j in self.pairs:
            self._ici(k, j, self.me, self.c).start()

    def turn(self):
        for k in range(len(self.shapes)):
            self._local(k).turn()

    def forward(self):
        for k, j in self.pairs:
            self._ici(k, j, self._peer(j), self.c).wait_recv()
            self._d2d(k, j, self._peer(j), self.c).start()

    def finish(self):
        for k, j in self.pairs:
            self._d2d(k, j, self._peer(j), 1 - self.c).wait_recv()
        for k, j in self.pairs:
            self._ici(k, j, self.me, self.c).wait_send()
            self._d2d(k, j, self._peer(j), self.c).wait_send()
        for k in range(len(self.shapes)):
            self._local(k).wait()


def _full_shapes(shards, axes):
    return [tuple(d * (N_CHIPS if a == ax else 1) for a, d in enumerate(s.shape)) for s, ax in zip(shards, axes)]


def _all_gather_shards(shards, axes):
    n = len(shards)

    def body(*refs):
        g = _Gather([s.shape for s in shards], axes, refs[:n], refs[n:2 * n], refs[2 * n:3 * n], refs[3 * n:])
        g.start()
        g.turn()
        g.forward()
        g.finish()

    return pl.pallas_call(
        body, name="gather_weights",
        in_specs=[ANY] * n, out_specs=[ANY] * n,
        out_shape=[jax.ShapeDtypeStruct(fs, s.dtype) for fs, s in zip(_full_shapes(shards, axes), shards)],
        scratch_shapes=_Gather.scratch(shards),
        compiler_params=_comm_params(sum(_nbytes(s.shape, s.dtype) for s in shards)),
    )(*shards)


class _Slab:
    def __init__(self, arrays, pick, shard_shape):
        self.arrays = arrays
        self.pick = pick
        self.rows, self.cols = shard_shape
        self.half = self.rows // 2


def _pair_exchange(slabs):
    n = len(slabs)
    n_in = sum(len(sl.arrays) for sl in slabs)

    def body(*refs):
        ins = refs[:n_in]
        mine, got = refs[n_in:n_in + n], refs[n_in + n:n_in + 2 * n]
        bufs = refs[n_in + 2 * n:n_in + 3 * n]
        send_sems, recv_sems, in_sems, out_sems = refs[n_in + 3 * n:]
        x, y, c = _place()
        started = []
        base = 0
        for k, sl in enumerate(slabs):
            for s in range(N_CHIPS):
                ai, r0, c0 = sl.pick(s)
                src = ins[base + ai]

                def half(hc):
                    return src.at[pl.ds(pl.multiple_of(r0 + hc * sl.half, 8), sl.half), pl.ds(c0, sl.cols)]
                q = N_CHIPS * k + s
                loc = _Bounce(half(c), bufs[k].at[s], mine[k].at[s], in_sems.at[q], out_sems.at[q])
                loc.start()
                cp = pltpu.make_async_remote_copy(
                    src_ref=half(1 - c), dst_ref=got[k].at[s], send_sem=send_sems.at[q], recv_sem=recv_sems.at[q],
                    device_id=(x, y, 1 - c), device_id_type=MESH_ID)
                cp.start()
                started.append((loc, cp))
            base += len(sl.arrays)
        for loc, cp in started:
            loc.turn()
        for loc, cp in started:
            cp.wait_recv()
        for loc, cp in started:
            cp.wait_send()
            loc.wait()

    flat_in = [a for sl in slabs for a in sl.arrays]
    compact = [jax.ShapeDtypeStruct((N_CHIPS, sl.half, sl.cols), F32) for sl in slabs]
    outs = pl.pallas_call(
        body, name="grad_pair_exchange",
        in_specs=[ANY] * n_in, out_specs=[ANY] * (2 * n), out_shape=compact + compact,
        scratch_shapes=[pltpu.VMEM(s.shape, F32) for s in compact] + [pltpu.SemaphoreType.DMA((N_CHIPS * n,))] * 4,
        compiler_params=_comm_params(sum(_nbytes(s.shape, F32) for s in compact)),
    )(*flat_in)
    return outs[:n], outs[n:]


def _chip_exchange(partials, vec):
    n = len(partials)

    def body(*refs):
        ins, vec_ref = refs[:n], refs[n]
        outs, vec_out = refs[n + 1:2 * n + 1], refs[2 * n + 1]
        bufs = refs[2 * n + 2:3 * n + 3]
        send_sems, recv_sems, in_sems, out_sems = refs[3 * n + 3:]
        x, y, c = _place()
        me = 2 * x + y
        dev = 2 * me + c
        locs, sends = [], []
        for k in range(n):
            loc = _Bounce(ins[k].at[me], bufs[k], outs[k].at[me], in_sems.at[k], out_sems.at[k])
            loc.start()
            locs.append(loc)
            for j, (dx, dy) in enumerate(CHIP_MOVES):
                px, py = _flip(x, dx), _flip(y, dy)
                cp = pltpu.make_async_remote_copy(
                    src_ref=ins[k].at[2 * px + py], dst_ref=outs[k].at[me],
                    send_sem=send_sems.at[3 * k + j], recv_sem=recv_sems.at[3 * k + j],
                    device_id=(px, py, c), device_id_type=MESH_ID)
                cp.start()
                sends.append(cp)
        loc = _Bounce(vec_ref, bufs[n], vec_out.at[dev], in_sems.at[n], out_sems.at[n])
        loc.start()
        locs.append(loc)
        for j, (dx, dy, dc) in enumerate(DEV_MOVES):
            cp = pltpu.make_async_remote_copy(
                src_ref=vec_ref, dst_ref=vec_out.at[dev],
                send_sem=send_sems.at[3 * n + j], recv_sem=recv_sems.at[3 * n + j],
                device_id=(_flip(x, dx), _flip(y, dy), _flip(c, dc)), device_id_type=MESH_ID)
            cp.start()
            sends.append(cp)
        for loc in locs:
            loc.turn()
        for k in range(n):
            for j, (dx, dy) in enumerate(CHIP_MOVES):
                px, py = _flip(x, dx), _flip(y, dy)
                pltpu.make_async_remote_copy(
                    src_ref=ins[k].at[me], dst_ref=outs[k].at[2 * px + py],
                    send_sem=send_sems.at[3 * k + j], recv_sem=recv_sems.at[3 * k + j],
                    device_id=(px, py, c), device_id_type=MESH_ID).wait_recv()
        for j, (dx, dy, dc) in enumerate(DEV_MOVES):
            px, py, pc = _flip(x, dx), _flip(y, dy), _flip(c, dc)
            pltpu.make_async_remote_copy(
                src_ref=vec_ref, dst_ref=vec_out.at[4 * px + 2 * py + pc],
                send_sem=send_sems.at[3 * n + j], recv_sem=recv_sems.at[3 * n + j],
                device_id=(px, py, pc), device_id_type=MESH_ID).wait_recv()
        for cp in sends:
            cp.wait_send()
        for loc in locs:
            loc.wait()

    n_sem = 3 * n + len(DEV_MOVES)
    bufs = [pltpu.VMEM(p.shape[1:], p.dtype) for p in partials] + [pltpu.VMEM(vec.shape, F32)]
    outs = pl.pallas_call(
        body, name="grad_chip_exchange",
        in_specs=[ANY] * (n + 1), out_specs=[ANY] * (n + 1),
        out_shape=[jax.ShapeDtypeStruct(p.shape, p.dtype) for p in partials] + [jax.ShapeDtypeStruct((N_DEV,) + vec.shape, F32)],
        scratch_shapes=bufs + [pltpu.SemaphoreType.DMA((n_sem,)), pltpu.SemaphoreType.DMA((n_sem,)),
                               pltpu.SemaphoreType.DMA((n + 1,)), pltpu.SemaphoreType.DMA((n + 1,))],
        compiler_params=_comm_params(sum(_nbytes(p.shape[1:], p.dtype) for p in partials) + _nbytes(vec.shape, F32)),
    )(*partials, vec)
    return outs[:n], outs[n]


def _pair_share(halves):
    n = len(halves)

    def body(*refs):
        ins, outs, bufs = refs[:n], refs[n:2 * n], refs[2 * n:3 * n]
        send_sems, recv_sems, in_sems, out_sems = refs[3 * n:]
        x, y, c = _place()
        started = []
        for k in range(n):
            hr = halves[k].shape[0]
            rows = lambda hc, k=k, hr=hr: outs[k].at[pl.ds(pl.multiple_of(hc * hr, 8), hr), :]
            loc = _Bounce(ins[k], bufs[k], rows(c), in_sems.at[k], out_sems.at[k])
            loc.start()
            cp = pltpu.make_async_remote_copy(
                src_ref=ins[k], dst_ref=rows(c), send_sem=send_sems.at[k], recv_sem=recv_sems.at[k],
                device_id=(x, y, 1 - c), device_id_type=MESH_ID)
            cp.start()
            recv = pltpu.make_async_remote_copy(
                src_ref=ins[k], dst_ref=rows(1 - c), send_sem=send_sems.at[k], recv_sem=recv_sems.at[k],
                device_id=(x, y, 1 - c), device_id_type=MESH_ID)
            started.append((loc, cp, recv))
        for loc, cp, recv in started:
            loc.turn()
        for loc, cp, recv in started:
            recv.wait_recv()
        for loc, cp, recv in started:
            cp.wait_send()
            loc.wait()

    return pl.pallas_call(
        body, name="grad_pair_share",
        in_specs=[ANY] * n, out_specs=[ANY] * n,
        out_shape=[jax.ShapeDtypeStruct((2 * h.shape[0], h.shape[1]), F32) for h in halves],
        scratch_shapes=[pltpu.VMEM(h.shape, F32) for h in halves] + [pltpu.SemaphoreType.DMA((n,))] * 4,
        compiler_params=_comm_params(sum(_nbytes(h.shape, F32) for h in halves)),
    )(*halves)


def _row_block(rows, cols, n_arrays):
    br = rows
    while br % 16 == 0 and 2 * n_arrays * br * cols * 4 > (16 << 20):
        br //= 2
    return br


def _add2(a, b, out_dtype, name):
    rows, cols = a.shape
    br = _row_block(rows, cols, 3)

    def body(a_ref, b_ref, o_ref):
        o_ref[...] = (a_ref[...] + b_ref[...]).astype(out_dtype)

    spec = pl.BlockSpec((br, cols), lambda i: (i, 0))
    return pl.pallas_call(body, name=name, grid=(rows // br,), in_specs=[spec, spec], out_specs=spec,
                          out_shape=jax.ShapeDtypeStruct(a.shape, out_dtype),
                          compiler_params=_cparams(3 * br * cols * 4, 1))(a, b)


def _sum_slots(a, name):
    n, rows, cols = a.shape
    br = _row_block(rows, cols, n + 1)

    def body(a_ref, o_ref):
        acc = a_ref[0].astype(F32)
        for s in range(1, n):
            acc = acc + a_ref[s].astype(F32)
        o_ref[...] = acc

    return pl.pallas_call(body, name=name, grid=(rows // br,),
                          in_specs=[pl.BlockSpec((n, br, cols), lambda i: (0, i, 0))],
                          out_specs=pl.BlockSpec((br, cols), lambda i: (i, 0)),
                          out_shape=jax.ShapeDtypeStruct((rows, cols), F32),
                          compiler_params=_cparams((n + 1) * br * cols * 4, 1))(a)


def _adamw_math(w, g, m, v):
    m = ADAM_B1 * m + (1.0 - ADAM_B1) * g
    v = ADAM_B2 * v + (1.0 - ADAM_B2) * (g * g)
    m_hat = m / (1.0 - ADAM_B1 ** ADAM_STEP)
    v_hat = v / (1.0 - ADAM_B2 ** ADAM_STEP)
    delta = -ADAM_LR * (m_hat / (jnp.sqrt(v_hat) + ADAM_EPS) + ADAM_WD * w)
    return delta, m, v


def _adamw(g, w, m, v, name):
    rows, cols = g.shape
    br = _row_block(rows, cols, 7)

    def body(g_ref, w_ref, m_ref, v_ref, d_ref, nm_ref, nv_ref):
        d_ref[...], nm_ref[...], nv_ref[...] = _adamw_math(w_ref[...], g_ref[...], m_ref[...], v_ref[...])

    spec = pl.BlockSpec((br, cols), lambda i: (i, 0))
    return pl.pallas_call(body, name=name, grid=(rows // br,), in_specs=[spec] * 4, out_specs=[spec] * 3,
                          out_shape=[jax.ShapeDtypeStruct(g.shape, F32)] * 3,
                          compiler_params=_cparams(7 * br * cols * 4, 1))(g, w, m, v)


ROW_FINAL_G, ROW_PE_G, ROW_LOSS = 0, 1, 2
ROW_CONV_B, ROW_CN_G, ROW_CN_B, ROW_B_PW2 = 8, 9, 10, 11
ROW_LN_G = 16
ROW_ONORM_G, ROW_LB = 24, 25
ROW_CONV_W = 32
SMALL = ["ln_g", "conv_b", "cnorm_g", "cnorm_b", "b_pw2", "onorm_g", "pe_norm_g", "final_g"]
SMALL_ROW = dict(ln_g=ROW_LN_G, conv_b=ROW_CONV_B, cnorm_g=ROW_CN_G, cnorm_b=ROW_CN_B, b_pw2=ROW_B_PW2,
                 onorm_g=ROW_ONORM_G, pe_norm_g=ROW_PE_G, final_g=ROW_FINAL_G)


def _adamw_small(vsum, gcw, lb_logits, params):
    names = SMALL + ["lb_logits", "conv_w"]
    flat = [t for nm in names for t in params[nm]]

    def body(*refs):
        vs_ref, gcw_ref, lbl_ref = refs[:3]
        ins = refs[3:3 + 3 * len(names)]
        outs = refs[3 + 3 * len(names):]
        for q, nm in enumerate(names):
            w_ref, m_ref, v_ref = ins[3 * q:3 * q + 3]
            g_ref, d_ref, nm_ref, nv_ref = outs[4 * q:4 * q + 4]
            if nm == "conv_w":
                g = gcw_ref[...]
            elif nm == "lb_logits":
                lb = _softmax_row0(lbl_ref[...])
                g0 = vs_ref[ROW_LB:ROW_LB + 1, :] * lb * (1.0 - lb)
                g = jnp.concatenate([g0, -g0], axis=0)
            else:
                g = vs_ref[SMALL_ROW[nm]:SMALL_ROW[nm] + 1, :]
            g_ref[...] = g
            d_ref[...], nm_ref[...], nv_ref[...] = _adamw_math(w_ref[...], g, m_ref[...], v_ref[...])

    out_shape = [jax.ShapeDtypeStruct(params[nm][0].shape, F32) for nm in names for _ in range(4)]
    outs = pl.pallas_call(body, name="adamw_small", out_shape=out_shape)(vsum, gcw, lb_logits, *flat)
    return {nm: tuple(outs[4 * q:4 * q + 4]) for q, nm in enumerate(names)}


TOKEN_TILE = dict(inproj_fwd=1024, conv=256, hgrn=512, tail=256, inproj_bwd_x=1024, weight_grad=1024)


def _tile(T, family):
    return min(T, TOKEN_TILE[family])


def kernel(x, p, ln_g, w_in, conv_w, conv_b, cnorm_g, cnorm_b, w_pw2, b_pw2, lb_logits, onorm_g, w_out, pe_norm_g, w_pg, w_pp, final_g, loss_target, m_ln_g, m_w_in, m_conv_w, m_conv_b, m_cnorm_g, m_cnorm_b, m_w_pw2, m_b_pw2, m_lb_logits, m_onorm_g, m_w_out, m_pe_norm_g, m_w_pg, m_w_pp, m_final_g, v_ln_g, v_w_in, v_conv_w, v_conv_b, v_cnorm_g, v_cnorm_b, v_w_pw2, v_b_pw2, v_lb_logits, v_onorm_g, v_w_out, v_pe_norm_g, v_w_pg, v_w_pp, v_final_g):
    given = dict(locals())
    x2, p2, tgt = x[0], p[0, 0], loss_target[0]
    T = x2.shape[0]
    fin_g = final_g.reshape(1, D)

    conv_w_pad = jnp.pad(conv_w[0], ((0, HALO - CONV_K), (0, 0)))
    (w_in_f,) = _all_gather_shards([w_in[0].astype(_MXU)], [1])

    z, u, (w_pw2_f, w_out_f, w_pg_f, w_pp_f, conv_w_f) = _inproj_fwd(
        x2, ln_g, w_in_f,
        [w_pw2[0].astype(_MXU), w_out[0].astype(_MXU), w_pg[0].astype(_MXU), w_pp[0].astype(_MXU), conv_w_pad],
        [0, 0, 0, 1, 1], _tile(T, "inproj_fwd"))
    yc, y1 = _conv_fwd(z, conv_w_f, conv_b, cnorm_g, cnorm_b, w_pw2_f, b_pw2, _tile(T, "conv"))
    o_raw, yh, s_chunks = _hgrn_fwd(z, lb_logits, onorm_g, _tile(T, "hgrn"), HB)
    dyc, dyh, dh, n2, ds, dpe, dhb, pb, vec_tail = _tail(
        x2, yc, yh, p2, tgt, w_out_f, w_pg_f, w_pp_f, pe_norm_g, fin_g, _tile(T, "tail"))
    dzh, vec_hgrn = _hgrn_bwd(z, lb_logits, onorm_g, o_raw, dyh, s_chunks, _tile(T, "hgrn"), HB)
    dzc, a_act, dy2, vec_conv, g_conv_w = _conv_bwd(z, y1, dyc, conv_w_f, cnorm_g, cnorm_b, w_pw2_f, b_pw2, _tile(T, "conv"))
    grad_x, vec_in = _inproj_bwd_x(x2, ln_g, dzc, dzh, w_in_f, dh, _tile(T, "inproj_bwd_x"))
    tk = _tile(T, "weight_grad")
    g_w_in = _inproj_bwd_w(u, dzc, dzh, tk)
    g_w_pw2 = _tn_matmul(a_act, dy2, tk, "grad_w_pw2")
    g_w_out_c = _tn_matmul(yc, dhb, tk, "grad_w_out_conv")
    g_w_out_h = _tn_matmul(yh, dhb, tk, "grad_w_out_hgrn")
    g_w_pg = _tn_matmul(n2, ds, tk, "grad_w_pg")
    g_w_pp = _tn_matmul(pb, dpe, tk, "grad_w_pp")

    big = ["w_in", "w_pw2", "w_out", "w_pg", "w_pp"]
    slabs = [
        _Slab([g_w_in], lambda s: (0, 0, s * (NPART * D // N_CHIPS)), (D, NPART * D // N_CHIPS)),
        _Slab([g_w_pw2], lambda s: (0, s * (D // N_CHIPS), 0), (D // N_CHIPS, D)),
        _Slab([g_w_out_c, g_w_out_h], lambda s: (s // 2, (s % 2) * (D // 2), 0), (D // 2, D)),
        _Slab([g_w_pg], lambda s: (0, s * (D // N_CHIPS), 0), (D // N_CHIPS, D)),
        _Slab([g_w_pp], lambda s: (0, 0, s * (D // N_CHIPS)), (PLE, D // N_CHIPS)),
    ]
    mine, got = _pair_exchange(slabs)
    partial = [_add2(a.reshape(-1, a.shape[-1]), b.reshape(-1, b.shape[-1]), _WIRE, "pair_sum_" + nm).reshape(a.shape)
               for a, b, nm in zip(mine, got, big)]
    vec = jnp.concatenate([vec_tail, vec_conv, vec_in, vec_hgrn, g_conv_w], axis=0)
    slots, vec_slots = _chip_exchange(partial, vec)
    halves = [_sum_slots(s, "chip_sum_" + nm) for s, nm in zip(slots, big)]
    vsum = _sum_slots(vec_slots, "vec_sum")
    grads_big = _pair_share(halves)

    out = {}
    for nm, g in zip(big, grads_big):
        w2, m2, v2 = given[nm][0], given["m_" + nm][0], given["v_" + nm][0]
        d, nm_, nv_ = _adamw(g, w2, m2, v2, "adamw_" + nm)
        out[nm] = tuple(t[None] for t in (g, d, nm_, nv_))
    chip = 2 * lax.axis_index("x") + lax.axis_index("y")
    gcw = lax.dynamic_slice(vsum, (ROW_CONV_W, chip * (D // N_CHIPS)), (CONV_K, D // N_CHIPS))
    params = {nm: (given[nm].reshape(-1, D), given["m_" + nm].reshape(-1, D), given["v_" + nm].reshape(-1, D))
              for nm in SMALL + ["lb_logits"]}
    params["conv_w"] = (conv_w[0], m_conv_w[0], v_conv_w[0])
    small = _adamw_small(vsum, gcw, lb_logits, params)
    for nm, ts in small.items():
        out[nm] = tuple(t.reshape(given[nm].shape) for t in ts)

    loss = vsum[ROW_LOSS, 0]
    order = ["ln_g", "w_in", "conv_w", "conv_b", "cnorm_g", "cnorm_b", "w_pw2", "b_pw2", "lb_logits", "onorm_g",
             "w_out", "pe_norm_g", "w_pg", "w_pp", "final_g"]
    return (loss, grad_x[None], *[out[nm][0] for nm in order], *[out[nm][1] for nm in order],
            *[out[nm][2] for nm in order], *[out[nm][3] for nm in order])
```

```python
import functools

import jax
import jax.numpy as jnp
from jax import lax
from jax.experimental import pallas as pl
from jax.experimental.pallas import tpu as pltpu

F32 = jnp.float32
BF16 = jnp.bfloat16
_MXU = jnp.bfloat16
_WIRE = jnp.bfloat16

D = 1024
NPART = 7
PLE = 256
HEADS = 8
HD = 128
CHUNK = 64
CONV_K = 31
HALO = 32
EPS = 1e-6
N_CHIPS = 4
N_DEV = 8
HB = 8
VEC_ROWS = 64

ADAM_LR = 0.001
ADAM_B1 = 0.9
ADAM_B2 = 0.999
ADAM_EPS = 1e-08
ADAM_WD = 0.01
ADAM_STEP = 10

V7X_VMEM_LIMIT = 60000 * 1024
MESH_ID = pl.DeviceIdType.MESH
ANY = pl.BlockSpec(memory_space=pl.ANY)


def _cparams(block_bytes, n_grid_dims):
    limit = min(V7X_VMEM_LIMIT, 2 * block_bytes + (24 << 20))
    return pltpu.CompilerParams(vmem_limit_bytes=int(limit), dimension_semantics=("arbitrary",) * n_grid_dims)


def _nbytes(shape, dtype):
    n = 1
    for s in shape:
        n *= s
    return n * jnp.dtype(dtype).itemsize


def _dot(a, b):
    return jnp.dot(a.astype(_MXU), b.astype(_MXU), preferred_element_type=F32)


def _dot_nt(a, b):
    return lax.dot_general(a.astype(_MXU), b.astype(_MXU), (((1,), (1,)), ((), ())), preferred_element_type=F32)


def _dot_tn(a, b):
    return lax.dot_general(a.astype(_MXU), b.astype(_MXU), (((0,), (0,)), ((), ())), preferred_element_type=F32)


def _tri_dot(tri_bf, x):
    x1 = x.astype(BF16)
    r1 = x - x1.astype(F32)
    x2 = r1.astype(BF16)
    x3 = (r1 - x2.astype(F32)).astype(BF16)
    d = lambda t: jnp.dot(tri_bf, t, preferred_element_type=F32)
    return d(x1) + d(x2) + d(x3)


def _split2(x):
    hi = x.astype(BF16)
    return hi, (x - hi.astype(F32)).astype(BF16)


def _dot3(dims, a, b):
    d = lambda p, q: lax.dot_general(p, q, (dims, ((), ())), preferred_element_type=F32)
    return d(a[0], b[0]) + d(a[0], b[1]) + d(a[1], b[0])


def _sigmoid(x):
    return jax.nn.sigmoid(x)


def _mean_lanes(x):
    return jnp.mean(x, axis=-1, keepdims=True)


def _sum_rows(x):
    return jnp.sum(x, axis=0, keepdims=True)


def _group_ln(y):
    yn, rs = [], []
    for g in range(D // HD):
        blk = y[:, g * HD:(g + 1) * HD]
        xc = blk - _mean_lanes(blk)
        r = lax.rsqrt(_mean_lanes(xc * xc) + EPS)
        yn.append(xc * r)
        rs.append(jnp.broadcast_to(r, blk.shape))
    return jnp.concatenate(yn, axis=1), jnp.concatenate(rs, axis=1)


def _group_ln_bwd(dyn, yn, rstd):
    out = []
    for g in range(D // HD):
        sl = slice(g * HD, (g + 1) * HD)
        d, n = dyn[:, sl], yn[:, sl]
        out.append(rstd[:, sl] * (d - _mean_lanes(d) - n * _mean_lanes(d * n)))
    return jnp.concatenate(out, axis=1)


def _head_means(x, hb, fn=lambda m: m):
    return jnp.concatenate([jnp.broadcast_to(fn(_mean_lanes(x[:, hh * HD:(hh + 1) * HD])), (x.shape[0], HD))
                            for hh in range(hb)], axis=1)


def _head_rsqrt_mean(x, hb):
    return _head_means(x, hb, lambda m: lax.rsqrt(m + EPS))


def _softmax_row0(lbl):
    m = jnp.max(lbl, axis=0, keepdims=True)
    e = jnp.exp(lbl - m)
    return e[0:1, :] / jnp.sum(e, axis=0, keepdims=True)


def _inproj_fwd(x, ln_g, w_in, shards, axes, tT):
    T = x.shape[0]
    n = len(shards)
    steps = (T // tT) * NPART
    at = _hosted_steps(steps)

    def body(x_ref, g_ref, w_ref, *rest):
        ins, (z_ref, u_ref), outs = rest[:n], rest[n:n + 2], rest[n + 2:2 * n + 2]
        u_scr, bufs, sems = rest[2 * n + 2], rest[2 * n + 3:3 * n + 3], rest[3 * n + 3:]
        step = pl.program_id(0) * NPART + pl.program_id(1)
        gather = _Gather([s.shape for s in shards], axes, ins, outs, bufs, sems)
        for phase in ("start", "turn", "forward"):
            pl.when(step == at[phase])(getattr(gather, phase))

        @pl.when(pl.program_id(1) == 0)
        def _():
            xv = x_ref[...]
            r = lax.rsqrt(_mean_lanes(xv * xv) + EPS)
            u = (xv * r * g_ref[...]).astype(_MXU)
            u_scr[...] = u
            u_ref[...] = u
        z_ref[...] = jnp.dot(u_scr[...], w_ref[...], preferred_element_type=F32)
        pl.when(step == at["finish"])(gather.finish)

    blk = (_nbytes((tT, D), F32) * 2 + _nbytes((D, D), _MXU) + _nbytes((tT, D), _MXU) * 2
           + sum(_nbytes(s.shape, s.dtype) for s in shards))
    outs = pl.pallas_call(
        body, name="inproj_fwd", grid=(T // tT, NPART),
        in_specs=[pl.BlockSpec((tT, D), lambda i, j: (i, 0)), pl.BlockSpec((1, D), lambda i, j: (0, 0)),
                  pl.BlockSpec((D, D), lambda i, j: (0, j))] + [ANY] * n,
        out_specs=[pl.BlockSpec((tT, D), lambda i, j: (i, j)), pl.BlockSpec((tT, D), lambda i, j: (i, 0))] + [ANY] * n,
        out_shape=[jax.ShapeDtypeStruct((T, NPART * D), F32), jax.ShapeDtypeStruct((T, D), _MXU)]
        + [jax.ShapeDtypeStruct(fs, s.dtype) for fs, s in zip(_full_shapes(shards, axes), shards)],
        scratch_shapes=[pltpu.VMEM((tT, D), _MXU)] + _Gather.scratch(shards),
        compiler_params=_cparams(blk, 2),
    )(x, ln_g, w_in, *shards)
    return outs[0], outs[1], outs[2:]


def _shifted_windows(ext, first, visit):
    n = ext.shape[0]
    for m in range(first, first + CONV_K):
        visit(m, (ext if m == 0 else pltpu.roll(ext, n - m, axis=0))[0:n - HALO, :])


def _conv_fwd(z, conv_w, conv_b, cn_g, cn_b, w_pw2, b_pw2, tT):
    T = z.shape[0]

    def body(cv_ref, cg_ref, ct_ref, cw_ref, cb_ref, ng_ref, nb_ref, wp_ref, bp_ref, yc_ref, y1_ref, ext):
        @pl.when(pl.program_id(0) == 0)
        def _():
            ext[...] = jnp.zeros_like(ext)
        ext[0:HALO, :] = ext[tT:tT + HALO, :]
        ext[HALO:, :] = cv_ref[...] * _sigmoid(cg_ref[...])
        cw = cw_ref[...]
        acc = [cb_ref[...]]

        def tap(m, win):
            acc[0] = acc[0] + win * cw[m - 2:m - 1, :]
        _shifted_windows(ext[...], 2, tap)
        y1 = acc[0]
        y1_ref[...] = y1
        yn, _ = _group_ln(y1)
        apre = yn * ng_ref[...] + nb_ref[...]
        a = apre * _sigmoid(apre)
        y2 = _dot(a, wp_ref[...]) + bp_ref[...]
        ct = ct_ref[...]
        yc_ref[...] = (y2 * (ct * _sigmoid(ct))).astype(_MXU)

    part = lambda p: pl.BlockSpec((tT, D), lambda i: (i, p))
    row = pl.BlockSpec((1, D), lambda i: (0, 0))
    tok = pl.BlockSpec((tT, D), lambda i: (i, 0))
    blk = 4 * _nbytes((tT, D), F32) + _nbytes((D, D), _MXU) + _nbytes((tT, D), _MXU) + 8 * _nbytes((tT + HALO, D), F32)
    return pl.pallas_call(
        body, name="conv_fwd", grid=(T // tT,),
        in_specs=[part(0), part(1), part(2), pl.BlockSpec((HALO, D), lambda i: (0, 0)), row, row, row,
                  pl.BlockSpec((D, D), lambda i: (0, 0)), row],
        out_specs=[tok, tok],
        out_shape=[jax.ShapeDtypeStruct((T, D), _MXU), jax.ShapeDtypeStruct((T, D), F32)],
        scratch_shapes=[pltpu.VMEM((tT + HALO, D), F32)],
        compiler_params=_cparams(blk, 1),
    )(z, z, z, conv_w, conv_b, cn_g, cn_b, w_pw2, b_pw2)


def _hgrn_gates(lb, hq, hf):
    sq = _sigmoid(hq)
    sg = _sigmoid(hf)
    f = lb + (1.0 - lb) * sg
    return sq, sg, f, hq * sq, (1.0 - lb) * (1.0 - sg), jnp.log(f)


def _chunk_decays(lf, q, k):
    r = lax.broadcasted_iota(jnp.int32, (CHUNK, CHUNK), 0)
    c = lax.broadcasted_iota(jnp.int32, (CHUNK, CHUNK), 1)
    b = _tri_dot((r >= c).astype(BF16), lf)
    bm = b[CHUNK // 2 - 1:CHUNK // 2, :]
    bl = b[CHUNK - 1:CHUNK, :]
    eb = jnp.exp(b)
    eqm = jnp.exp(b - bm)
    ekm = jnp.exp(bm - b)
    ekd = jnp.exp(bl - b)
    return dict(causal=r >= c, eb=eb, eqm=eqm, ekm=ekm, ekd=ekd, ebl=jnp.exp(bl),
                qd=q * eb, qm=q * eqm, km=k * ekm, kd=k * ekd)


def _hgrn_fwd(z, lb_logits, onorm_g, tT, hb):
    T = z.shape[0]
    nc = tT // CHUNK
    w = hb * HD

    def body(lbl_ref, og_ref, hq_ref, hf_ref, hi_ref, hg_ref, o_ref, yh_ref, sc_ref, st):
        @pl.when(pl.program_id(1) == 0)
        def _():
            st[...] = jnp.zeros_like(st)
        lb_all = _softmax_row0(lbl_ref[...])
        og_all = og_ref[...]

        def chunk(c, carry):
            sl = pl.ds(pl.multiple_of(c * CHUNK, CHUNK), CHUNK)
            lanes = [slice(hh * HD, (hh + 1) * HD) for hh in range(hb)]
            heads = lambda fn: [fn(hh, ln) for hh, ln in enumerate(lanes)]
            hg, v = hg_ref[sl, :], hi_ref[sl, :]
            _, _, _, q, k, lf = _hgrn_gates(lb_all, hq_ref[sl, :], hf_ref[sl, :])
            dc = _chunk_decays(lf, q, k)
            s_t = heads(lambda hh, ln: st[hh])
            a = heads(lambda hh, ln: jnp.where(dc["causal"], _dot_nt(dc["qm"][:, ln], dc["km"][:, ln]), 0.0))
            o_inter = heads(lambda hh, ln: _dot_nt(dc["qd"][:, ln], s_t[hh]))
            kv = heads(lambda hh, ln: _dot_tn(v[:, ln], dc["kd"][:, ln]))
            o_intra = heads(lambda hh, ln: _dot(a[hh], v[:, ln]))
            for hh, ln in enumerate(lanes):
                sc_ref[hh, c] = s_t[hh]
                st[hh] = s_t[hh] * dc["ebl"][:, ln] + kv[hh]
            o = jnp.concatenate([o_inter[hh] + o_intra[hh] for hh in range(hb)], axis=1)
            o_ref[sl, :] = o
            n = o * _head_rsqrt_mean(o * o, hb)
            yh_ref[sl, :] = ((n * og_all) * (hg * _sigmoid(hg))).astype(_MXU)
            return carry

        lax.fori_loop(0, nc, chunk, 0)

    zpart = lambda p: pl.BlockSpec((tT, w), lambda h, i: (i, p * (HEADS // hb) + h))
    blk = 6 * _nbytes((tT, w), F32) + _nbytes((hb, nc, HD, HD), F32)
    return pl.pallas_call(
        body, name="hgrn_fwd", grid=(HEADS // hb, T // tT),
        in_specs=[pl.BlockSpec((2, w), lambda h, i: (0, h)), pl.BlockSpec((1, w), lambda h, i: (0, h)),
                  zpart(3), zpart(4), zpart(5), zpart(6)],
        out_specs=[pl.BlockSpec((tT, w), lambda h, i: (i, h)), pl.BlockSpec((tT, w), lambda h, i: (i, h)),
                   pl.BlockSpec((hb, nc, HD, HD), lambda h, i: (h, i, 0, 0))],
        out_shape=[jax.ShapeDtypeStruct((T, D), F32), jax.ShapeDtypeStruct((T, D), _MXU),
                   jax.ShapeDtypeStruct((HEADS, T // CHUNK, HD, HD), F32)],
        scratch_shapes=[pltpu.VMEM((hb, HD, HD), F32)],
        compiler_params=_cparams(blk, 2),
    )(lb_logits, onorm_g, z, z, z, z)


def _hgrn_bwd(z, lb_logits, onorm_g, o_raw, dyh, s_chunks, partials, tT, hb):
    T = z.shape[0]
    nc = tT // CHUNK
    nI = T // tT
    w = hb * HD
    n = len(partials)
    at = _hosted_steps((HEADS // hb) * nI)

    def body(lbl_ref, og_ref, hq_ref, hf_ref, hi_ref, hg_ref, o_ref, dy_ref, sc_ref, *rest):
        (dz_ref, vec_ref), dst = rest[n:n + 2], rest[2 * n + 2]
        exchange = _ChipExchange(n, rest[:n], rest[n + 2:2 * n + 2], rest[2 * n + 3:3 * n + 3], rest[3 * n + 3:])
        step = pl.program_id(0) * nI + pl.program_id(1)
        pl.when(step == at["start"])(exchange.start)
        pl.when(step == at["turn"])(exchange.turn)

        @pl.when(pl.program_id(1) == 0)
        def _():
            dst[...] = jnp.zeros_like(dst)
            vec_ref[...] = jnp.zeros_like(vec_ref)
        lb_all = _softmax_row0(lbl_ref[...])
        og_all = og_ref[...]
        last_row = lax.broadcasted_iota(jnp.int32, (CHUNK, w), 0) == CHUNK - 1
        r64 = lax.broadcasted_iota(jnp.int32, (CHUNK, CHUNK), 0)
        c64 = lax.broadcasted_iota(jnp.int32, (CHUNK, CHUNK), 1)
        upper = (c64 >= r64).astype(BF16)
        lanes = [slice(hh * HD, (hh + 1) * HD) for hh in range(hb)]
        heads = lambda fn: [fn(hh, ln) for hh, ln in enumerate(lanes)]
        wide = lambda parts: jnp.concatenate(parts, axis=1)

        def chunk(cc, carry):
            c = nc - 1 - cc
            sl = pl.ds(pl.multiple_of(c * CHUNK, CHUNK), CHUNK)
            hq, hg, v = hq_ref[sl, :], hg_ref[sl, :], hi_ref[sl, :]
            sq, sg, f, q, k, lf = _hgrn_gates(lb_all, hq, hf_ref[sl, :])
            dc = _chunk_decays(lf, q, k)
            s_t = heads(lambda hh, ln: sc_ref[hh, c])
            ds_t = heads(lambda hh, ln: dst[hh])
            o, dy = o_ref[sl, :], dy_ref[sl, :]
            r = _head_rsqrt_mean(o * o, hb)
            n = o * r
            sgg = _sigmoid(hg)
            silu_g = hg * sgg
            dhg = dy * (n * og_all) * (sgg * (1.0 + hg * (1.0 - sgg)))
            dn = dy * og_all * silu_g
            g_og = _sum_rows(dy * n * silu_g)
            do = r * (dn - n * _head_means(dn * n, hb))
            a = heads(lambda hh, ln: jnp.where(dc["causal"], _dot_nt(dc["qm"][:, ln], dc["km"][:, ln]), 0.0))
            dam = heads(lambda hh, ln: jnp.where(dc["causal"], _dot_nt(do[:, ln], v[:, ln]), 0.0))
            dqd = wide(heads(lambda hh, ln: _dot(do[:, ln], s_t[hh])))
            dkd = wide(heads(lambda hh, ln: _dot(v[:, ln], ds_t[hh])))
            dv_inter = heads(lambda hh, ln: _dot_nt(dc["kd"][:, ln], ds_t[hh]))
            dqs = heads(lambda hh, ln: _dot_tn(do[:, ln], dc["qd"][:, ln]))
            dv = wide(heads(lambda hh, ln: _dot_tn(a[hh], do[:, ln]) + dv_inter[hh]))
            dam2 = [_split2(t) for t in dam]
            km2, qm2 = _split2(dc["km"]), _split2(dc["qm"])
            dqm = wide(heads(lambda hh, ln: _dot3(((1,), (0,)), dam2[hh], (km2[0][:, ln], km2[1][:, ln]))))
            dkm = wide(heads(lambda hh, ln: _dot3(((0,), (0,)), dam2[hh], (qm2[0][:, ln], qm2[1][:, ln]))))
            debl = wide(heads(lambda hh, ln: _sum_rows(ds_t[hh] * s_t[hh])))
            for hh, ln in enumerate(lanes):
                dst[hh] = ds_t[hh] * dc["ebl"][:, ln] + dqs[hh]
            dq = dqd * dc["eb"] + dqm * dc["eqm"]
            dk = dkm * dc["ekm"] + dkd * dc["ekd"]
            dbl = _sum_rows(dkd * dc["kd"]) + debl * dc["ebl"]
            db = dq * q - dk * k + jnp.where(last_row, dbl, 0.0)
            dlf = _tri_dot(upper, db)
            dfk = dlf / f - dk
            dz_ref[0, sl, :] = (dq * (sq * (1.0 + hq * (1.0 - sq)))).astype(_MXU)
            dz_ref[1, sl, :] = (dfk * ((1.0 - lb_all) * sg * (1.0 - sg))).astype(_MXU)
            dz_ref[2, sl, :] = dv.astype(_MXU)
            dz_ref[3, sl, :] = dhg.astype(_MXU)
            vec_ref[0:1, :] += g_og
            vec_ref[1:2, :] += _sum_rows(dfk * (1.0 - sg))
            return carry

        lax.fori_loop(0, nc, chunk, 0)
        pl.when(step == at["finish"])(exchange.finish)

    zpart = lambda p: pl.BlockSpec((tT, w), lambda h, i: (nI - 1 - i, p * (HEADS // hb) + h))
    act = pl.BlockSpec((tT, w), lambda h, i: (nI - 1 - i, h))
    blk = (6 * _nbytes((tT, w), F32) + _nbytes((hb, nc, HD, HD), F32) + 4 * _nbytes((tT, w), _MXU)
           + _ChipExchange.scratch_bytes(partials))
    outs = pl.pallas_call(
        body, name="hgrn_bwd", grid=(HEADS // hb, nI),
        in_specs=[pl.BlockSpec((2, w), lambda h, i: (0, h)), pl.BlockSpec((1, w), lambda h, i: (0, h)),
                  zpart(3), zpart(4), zpart(5), zpart(6), act, act,
                  pl.BlockSpec((hb, nc, HD, HD), lambda h, i: (h, nI - 1 - i, 0, 0))] + [ANY] * n,
        out_specs=[pl.BlockSpec((4, tT, w), lambda h, i: (0, nI - 1 - i, h)),
                   pl.BlockSpec((8, w), lambda h, i: (0, h))] + [ANY] * n,
        out_shape=[jax.ShapeDtypeStruct((4, T, D), _MXU), jax.ShapeDtypeStruct((8, D), F32)]
        + [jax.ShapeDtypeStruct(p.shape, p.dtype) for p in partials],
        scratch_shapes=[pltpu.VMEM((hb, HD, HD), F32)] + _ChipExchange.scratch(partials),
        compiler_params=_cparams(blk, 2),
    )(lb_logits, onorm_g, z, z, z, z, o_raw, dyh, s_chunks, *partials)
    return outs[0], outs[1], outs[2:]


def _tail(x, yc, yh, p, target, w_out, w_pg, w_pp, pe_g, fin_g, tT):
    T = x.shape[0]

    def body(x_ref, yc_ref, yh_ref, p_ref, t_ref, wo_ref, wg_ref, wp_ref, pg_ref, fg_ref,
             dyc_ref, dyh_ref, dh_ref, n2_ref, ds_ref, dpe_ref, dhb_ref, pb_ref, vec_ref):
        @pl.when(pl.program_id(0) == 0)
        def _():
            vec_ref[...] = jnp.zeros_like(vec_ref)
        wo_c, wo_h = wo_ref[0:D, :], wo_ref[D:2 * D, :]
        h = x_ref[...] + _dot(yc_ref[...], wo_c) + _dot(yh_ref[...], wo_h)
        pb = p_ref[...].astype(_MXU)
        pe = _dot(pb, wp_ref[...])
        r2 = lax.rsqrt(_mean_lanes(h * h) + EPS)
        hn = h * r2
        n2 = (hn * pg_ref[...]).astype(_MXU)
        gate = _sigmoid(_dot(n2, wg_ref[...]))
        h2 = h + gate * pe
        r3 = lax.rsqrt(_mean_lanes(h2 * h2) + EPS)
        h2n = h2 * r3
        err = h2n * fg_ref[...] - t_ref[...]
        vec_ref[ROW_LOSS:ROW_LOSS + 1, :] += 0.5 * jnp.sum(_mean_lanes(err * err))
        dout = err * (1.0 / D)
        vec_ref[0:1, :] += _sum_rows(dout * h2n)
        dn3 = dout * fg_ref[...]
        dh2 = r3 * (dn3 - h2n * _mean_lanes(dn3 * h2n))
        ds = (dh2 * pe * gate * (1.0 - gate)).astype(_MXU)
        dn2 = _dot_nt(ds, wg_ref[...])
        vec_ref[1:2, :] += _sum_rows(dn2 * hn)
        dnn = dn2 * pg_ref[...]
        dh = dh2 + r2 * (dnn - hn * _mean_lanes(dnn * hn))
        dhb = dh.astype(_MXU)
        dyc_ref[...] = _dot_nt(dhb, wo_c)
        dyh_ref[...] = _dot_nt(dhb, wo_h)
        dh_ref[...] = dh
        n2_ref[...] = n2
        ds_ref[...] = ds
        dpe_ref[...] = (dh2 * gate).astype(_MXU)
        dhb_ref[...] = dhb
        pb_ref[...] = pb

    tok = lambda w: pl.BlockSpec((tT, w), lambda i: (i, 0))
    full = lambda r, c: pl.BlockSpec((r, c), lambda i: (0, 0))
    tokshape = lambda w, dt: jax.ShapeDtypeStruct((T, w), dt)
    blk = (5 * _nbytes((tT, D), F32) + 7 * _nbytes((tT, D), _MXU) + _nbytes((4 * D + PLE, D), _MXU)
           + 12 * _nbytes((tT, D), F32))
    return pl.pallas_call(
        body, name="tail_fwd_bwd", grid=(T // tT,),
        in_specs=[tok(D), tok(D), tok(D), tok(PLE), tok(D), full(2 * D, D), full(D, D), full(PLE, D), full(1, D), full(1, D)],
        out_specs=[tok(D), tok(D), tok(D), tok(D), tok(D), tok(D), tok(D), tok(PLE), full(8, D)],
        out_shape=[tokshape(D, F32), tokshape(D, F32), tokshape(D, F32), tokshape(D, _MXU), tokshape(D, _MXU),
                   tokshape(D, _MXU), tokshape(D, _MXU), tokshape(PLE, _MXU),
                   jax.ShapeDtypeStruct((8, D), F32)],
        compiler_params=_cparams(blk, 1),
    )(x, yc, yh, p, target, w_out, w_pg, w_pp, pe_g, fin_g)


def _conv_bwd(z, y1, dyc, conv_w, cn_g, cn_b, w_pw2, b_pw2, tT):
    T = z.shape[0]
    nI = T // tT
    hb = tT // HALO

    def body(cv_ref, cg_ref, ct_ref, hv_ref, hg_ref, y1_ref, dyc_ref, cw_ref, ng_ref, nb_ref, wp_ref, bp_ref,
             dz_ref, a_ref, dy2_ref, vec_ref, gcw_ref, ext, ext2, gpart):
        i = pl.program_id(0)

        @pl.when(i == 0)
        def _():
            ext2[...] = jnp.zeros_like(ext2)
            gpart[...] = jnp.zeros_like(gpart)
            vec_ref[...] = jnp.zeros_like(vec_ref)
        cv, cg, ct = cv_ref[...], cg_ref[...], ct_ref[...]
        sg = _sigmoid(cg)
        has_hist = (i < nI - 1).astype(F32)
        ext[0:HALO, :] = hv_ref[...] * _sigmoid(hg_ref[...]) * has_hist
        ext[HALO:, :] = cv * sg
        yn, rstd = _group_ln(y1_ref[...])
        apre = yn * ng_ref[...] + nb_ref[...]
        sa = _sigmoid(apre)
        a = (apre * sa).astype(_MXU)
        y2 = _dot(a, wp_ref[...]) + bp_ref[...]
        st = _sigmoid(ct)
        dyc_v = dyc_ref[...]
        dy2 = dyc_v * (ct * st)
        dy2b = dy2.astype(_MXU)
        da = _dot_nt(dy2b, wp_ref[...])
        dapre = da * (sa * (1.0 + apre * (1.0 - sa)))
        dy1 = _group_ln_bwd(dapre * ng_ref[...], yn, rstd)
        vec_ref[0:1, :] += _sum_rows(dy1)
        vec_ref[1:2, :] += _sum_rows(dapre * yn)
        vec_ref[2:3, :] += _sum_rows(dapre)
        vec_ref[3:4, :] += _sum_rows(dy2)
        dz_ref[2] = (dyc_v * y2 * (st * (1.0 + ct * (1.0 - st)))).astype(_MXU)
        a_ref[...] = a
        dy2_ref[...] = dy2b
        ext2[tT:tT + HALO, :] = ext2[0:HALO, :]
        ext2[0:tT, :] = dy1
        def grad_tap(m, win):
            p = dy1 * win
            part = p[0:8, :]
            for q in range(1, tT // 8):
                part = part + p[8 * q:8 * q + 8, :]
            gpart[m - 2] += part
        _shifted_windows(ext[...], 2, grad_tap)
        cw = cw_ref[...]
        acc = [None]

        def dv_tap(m, win):
            term = win * cw[CONV_K - 1 - m:CONV_K - m, :]
            acc[0] = term if acc[0] is None else acc[0] + term
        _shifted_windows(ext2[...], 0, dv_tap)
        dv = acc[0]
        dz_ref[0] = (dv * sg).astype(_MXU)
        dz_ref[1] = (dv * cv * sg * (1.0 - sg)).astype(_MXU)

        @pl.when(i == nI - 1)
        def _():
            gcw_ref[...] = jnp.sum(gpart[...], axis=1)

    part = lambda p: pl.BlockSpec((tT, D), lambda i: (nI - 1 - i, p))
    hist = lambda p: pl.BlockSpec((HALO, D), lambda i: (jnp.maximum((nI - 1 - i) * hb - 1, 0), p))
    tok = pl.BlockSpec((tT, D), lambda i: (nI - 1 - i, 0))
    row = pl.BlockSpec((1, D), lambda i: (0, 0))
    blk = (5 * _nbytes((tT, D), F32) + _nbytes((D, D), _MXU) + 5 * _nbytes((tT, D), _MXU)
           + 10 * _nbytes((tT + HALO, D), F32))
    return pl.pallas_call(
        body, name="conv_bwd", grid=(nI,),
        in_specs=[part(0), part(1), part(2), hist(0), hist(1), tok, tok, pl.BlockSpec((HALO, D), lambda i: (0, 0)),
                  row, row, pl.BlockSpec((D, D), lambda i: (0, 0)), row],
        out_specs=[pl.BlockSpec((3, tT, D), lambda i: (0, nI - 1 - i, 0)), tok, tok,
                   pl.BlockSpec((8, D), lambda i: (0, 0)), pl.BlockSpec((HALO, D), lambda i: (0, 0))],
        out_shape=[jax.ShapeDtypeStruct((3, T, D), _MXU), jax.ShapeDtypeStruct((T, D), _MXU),
                   jax.ShapeDtypeStruct((T, D), _MXU), jax.ShapeDtypeStruct((8, D), F32),
                   jax.ShapeDtypeStruct((HALO, D), F32)],
        scratch_shapes=[pltpu.VMEM((tT + HALO, D), F32), pltpu.VMEM((tT + HALO, D), F32), pltpu.VMEM((HALO, 8, D), F32)],
        compiler_params=_cparams(blk, 1),
    )(z, z, z, z, z, y1, dyc, conv_w, cn_g, cn_b, w_pw2, b_pw2)


def _inproj_bwd_x(x, ln_g, dzc, dzh, w_in, dh, partials, tT):
    T = x.shape[0]
    n = len(partials)
    at = _hosted_steps((T // tT) * NPART)

    def body(x_ref, g_ref, dzc_ref, dzh_ref, w_ref, dh_ref, *rest):
        (gx_ref, vec_ref), du = rest[n:n + 2], rest[2 * n + 2]
        exchange = _ChipExchange(n, rest[:n], rest[n + 2:2 * n + 2], rest[2 * n + 3:3 * n + 3], rest[3 * n + 3:])
        i, j = pl.program_id(0), pl.program_id(1)
        step = i * NPART + j
        pl.when(step == at["start"])(exchange.start)
        pl.when(step == at["turn"])(exchange.turn)

        @pl.when(j == 0)
        def _():
            du[...] = jnp.zeros_like(du)

        @pl.when(jnp.logical_and(i == 0, j == 0))
        def _():
            vec_ref[...] = jnp.zeros_like(vec_ref)

        @pl.when(j < 3)
        def _():
            du[...] += _dot_nt(dzc_ref[0], w_ref[...])

        @pl.when(j >= 3)
        def _():
            du[...] += _dot_nt(dzh_ref[0], w_ref[...])

        @pl.when(j == NPART - 1)
        def _():
            xv = x_ref[...]
            r = lax.rsqrt(_mean_lanes(xv * xv) + EPS)
            xn = xv * r
            duv = du[...]
            vec_ref[0:1, :] += _sum_rows(duv * xn)
            dun = duv * g_ref[...]
            gx_ref[...] = dh_ref[...] + r * (dun - xn * _mean_lanes(dun * xn))
        pl.when(step == at["finish"])(exchange.finish)

    tok = pl.BlockSpec((tT, D), lambda i, j: (i, 0))
    blk = (3 * _nbytes((tT, D), F32) + 2 * _nbytes((tT, D), _MXU) + _nbytes((D, D), _MXU) + 4 * _nbytes((tT, D), F32)
           + _ChipExchange.scratch_bytes(partials))
    outs = pl.pallas_call(
        body, name="inproj_bwd_x", grid=(T // tT, NPART),
        in_specs=[tok, pl.BlockSpec((1, D), lambda i, j: (0, 0)),
                  pl.BlockSpec((1, tT, D), lambda i, j: (jnp.minimum(j, 2), i, 0)),
                  pl.BlockSpec((1, tT, D), lambda i, j: (jnp.maximum(j - 3, 0), i, 0)),
                  pl.BlockSpec((D, D), lambda i, j: (0, j)), tok] + [ANY] * n,
        out_specs=[tok, pl.BlockSpec((8, D), lambda i, j: (0, 0))] + [ANY] * n,
        out_shape=[jax.ShapeDtypeStruct((T, D), F32), jax.ShapeDtypeStruct((8, D), F32)]
        + [jax.ShapeDtypeStruct(p.shape, p.dtype) for p in partials],
        scratch_shapes=[pltpu.VMEM((tT, D), F32)] + _ChipExchange.scratch(partials),
        compiler_params=_cparams(blk, 2),
    )(x, ln_g, dzc, dzh, w_in, dh, *partials)
    return outs[0], outs[1], outs[2:]


def _inproj_bwd_w(u, dzc, dzh, tk):
    T = u.shape[0]
    nK = T // tk

    def body(u_ref, dzc_ref, dzh_ref, gw_ref):
        j, k = pl.program_id(0), pl.program_id(1)

        @pl.when(k == 0)
        def _():
            gw_ref[...] = jnp.zeros_like(gw_ref)

        @pl.when(j < 3)
        def _():
            gw_ref[...] += _dot_tn(u_ref[...], dzc_ref[0])

        @pl.when(j >= 3)
        def _():
            gw_ref[...] += _dot_tn(u_ref[...], dzh_ref[0])

    blk = 3 * _nbytes((tk, D), _MXU) + 2 * _nbytes((D, D), F32)
    return pl.pallas_call(
        body, name="inproj_bwd_w", grid=(NPART, nK),
        in_specs=[pl.BlockSpec((tk, D), lambda j, k: (k, 0)),
                  pl.BlockSpec((1, tk, D), lambda j, k: (jnp.minimum(j, 2), jnp.where(j < 3, k, nK - 1), 0)),
                  pl.BlockSpec((1, tk, D), lambda j, k: (jnp.maximum(j - 3, 0), jnp.where(j < 3, 0, k), 0))],
        out_specs=pl.BlockSpec((D, D), lambda j, k: (0, j)),
        out_shape=jax.ShapeDtypeStruct((D, NPART * D), F32),
        compiler_params=_cparams(blk, 2),
    )(u, dzc, dzh)


def _tn_matmul(a, b, tk, name):
    T, M = a.shape
    N = b.shape[1]

    def body(a_ref, b_ref, o_ref):
        @pl.when(pl.program_id(0) == 0)
        def _():
            o_ref[...] = jnp.zeros_like(o_ref)
        o_ref[...] += _dot_tn(a_ref[...], b_ref[...])

    blk = _nbytes((tk, M), _MXU) + _nbytes((tk, N), _MXU) + 2 * _nbytes((M, N), F32)
    return pl.pallas_call(
        body, name=name, grid=(T // tk,),
        in_specs=[pl.BlockSpec((tk, M), lambda k: (k, 0)), pl.BlockSpec((tk, N), lambda k: (k, 0))],
        out_specs=pl.BlockSpec((M, N), lambda k: (0, 0)),
        out_shape=jax.ShapeDtypeStruct((M, N), F32),
        compiler_params=_cparams(blk, 1),
    )(a, b)


def _place():
    return lax.axis_index("x"), lax.axis_index("y"), lax.axis_index("c")


def _flip(v, d):
    return 1 - v if d else v


CHIP_MOVES = [(1, 0), (0, 1), (1, 1)]
DEV_MOVES = [(dx, dy, dc) for dx in (0, 1) for dy in (0, 1) for dc in (0, 1)][1:]


def _shard_slice(ref, axis, size, s):
    start = pl.multiple_of(s * size, size)
    return ref.at[pl.ds(start, size), :] if axis == 0 else ref.at[:, pl.ds(start, size)]


class _Bounce:
    def __init__(self, src, buf, dst, sem_in, sem_out):
        self.load = pltpu.make_async_copy(src, buf, sem_in)
        self.store = pltpu.make_async_copy(buf, dst, sem_out)

    def start(self):
        self.load.start()

    def turn(self):
        self.load.wait()
        self.store.start()

    def wait(self):
        self.store.wait()


def _comm_params(scratch_bytes):
    return pltpu.CompilerParams(vmem_limit_bytes=int(min(V7X_VMEM_LIMIT, scratch_bytes + (8 << 20))))


class _Gather:
    def __init__(self, shapes, axes, ins, outs, bufs, sems):
        self.shapes, self.axes, self.ins, self.outs, self.bufs = shapes, axes, ins, outs, bufs
        self.ici_send, self.ici_recv, self.d2d_send, self.d2d_recv, self.in_sems, self.out_sems = sems
        self.x, self.y, self.c = _place()
        self.me = 2 * self.x + self.y
        self.pairs = [(k, j) for k in range(len(shapes)) for j in range(3)]

    @staticmethod
    def scratch(shards):
        n = len(shards)
        return ([pltpu.VMEM(s.shape, s.dtype) for s in shards]
                + [pltpu.SemaphoreType.DMA((3 * n,))] * 4 + [pltpu.SemaphoreType.DMA((n,))] * 2)

    def _own_half(self, k, hc):
        half = self.shapes[k][0] // 2
        return self.ins[k].at[pl.ds(pl.multiple_of(hc * half, 16), half), :]

    def _region(self, k, who, hc):
        rows, cols = self.shapes[k]
        half = rows // 2
        if self.axes[k] == 0:
            return self.outs[k].at[pl.ds(pl.multiple_of(who * rows + hc * half, 16), half), :]
        return self.outs[k].at[pl.ds(pl.multiple_of(hc * half, 16), half), pl.ds(pl.multiple_of(who * cols, HD), cols)]

    def _peer(self, j):
        return 2 * _flip(self.x, CHIP_MOVES[j][0]) + _flip(self.y, CHIP_MOVES[j][1])

    def _ici(self, k, j, who, hc):
        dx, dy = CHIP_MOVES[j]
        return pltpu.make_async_remote_copy(
            src_ref=self._own_half(k, hc), dst_ref=self._region(k, who, hc),
            send_sem=self.ici_send.at[3 * k + j], recv_sem=self.ici_recv.at[3 * k + j],
            device_id=(_flip(self.x, dx), _flip(self.y, dy), self.c), device_id_type=MESH_ID)

    def _d2d(self, k, j, who, hc):
        return pltpu.make_async_remote_copy(
            src_ref=self._region(k, who, hc), dst_ref=self._region(k, who, hc),
            send_sem=self.d2d_send.at[3 * k + j], recv_sem=self.d2d_recv.at[3 * k + j],
            device_id=(self.x, self.y, 1 - self.c), device_id_type=MESH_ID)

    def _local(self, k):
        size = self.shapes[k][self.axes[k]]
        return _Bounce(self.ins[k], self.bufs[k], _shard_slice(self.outs[k], self.axes[k], size, self.me),
                       self.in_sems.at[k], self.out_sems.at[k])

    def start(self):
        for k in range(len(self.shapes)):
            self._local(k).start()
        for k, j in self.pairs:
            self._ici(k, j, self.me, self.c).start()

    def turn(self):
        for k in range(len(self.shapes)):
            self._local(k).turn()

    def forward(self):
        for k, j in self.pairs:
            self._ici(k, j, self._peer(j), self.c).wait_recv()
            self._d2d(k, j, self._peer(j), self.c).start()

    def finish(self):
        for k, j in self.pairs:
            self._d2d(k, j, self._peer(j), 1 - self.c).wait_recv()
        for k, j in self.pairs:
            self._ici(k, j, self.me, self.c).wait_send()
            self._d2d(k, j, self._peer(j), self.c).wait_send()
        for k in range(len(self.shapes)):
            self._local(k).wait()


def _full_shapes(shards, axes):
    return [tuple(d * (N_CHIPS if a == ax else 1) for a, d in enumerate(s.shape)) for s, ax in zip(shards, axes)]


def _all_gather_shards(shards, axes):
    n = len(shards)

    def body(*refs):
        g = _Gather([s.shape for s in shards], axes, refs[:n], refs[n:2 * n], refs[2 * n:3 * n], refs[3 * n:])
        g.start()
        g.turn()
        g.forward()
        g.finish()

    return pl.pallas_call(
        body, name="gather_weights",
        in_specs=[ANY] * n, out_specs=[ANY] * n,
        out_shape=[jax.ShapeDtypeStruct(fs, s.dtype) for fs, s in zip(_full_shapes(shards, axes), shards)],
        scratch_shapes=_Gather.scratch(shards),
        compiler_params=_comm_params(sum(_nbytes(s.shape, s.dtype) for s in shards)),
    )(*shards)


class _Slab:
    def __init__(self, arrays, pick, shard_shape):
        self.arrays = arrays
        self.pick = pick
        self.rows, self.cols = shard_shape
        self.half = self.rows // 2


def _pair_exchange(slabs, name):
    n = len(slabs)
    n_in = sum(len(sl.arrays) for sl in slabs)

    def body(*refs):
        ins = refs[:n_in]
        mine, got = refs[n_in:n_in + n], refs[n_in + n:n_in + 2 * n]
        bufs = refs[n_in + 2 * n:n_in + 3 * n]
        send_sems, recv_sems, in_sems, out_sems = refs[n_in + 3 * n:]
        x, y, c = _place()
        started = []
        base = 0
        for k, sl in enumerate(slabs):
            for s in range(N_CHIPS):
                ai, r0, c0 = sl.pick(s)
                src = ins[base + ai]

                def half(hc):
                    return src.at[pl.ds(pl.multiple_of(r0 + hc * sl.half, 8), sl.half), pl.ds(c0, sl.cols)]
                q = N_CHIPS * k + s
                loc = _Bounce(half(c), bufs[k].at[s], mine[k].at[s], in_sems.at[q], out_sems.at[q])
                loc.start()
                cp = pltpu.make_async_remote_copy(
                    src_ref=half(1 - c), dst_ref=got[k].at[s], send_sem=send_sems.at[q], recv_sem=recv_sems.at[q],
                    device_id=(x, y, 1 - c), device_id_type=MESH_ID)
                cp.start()
                started.append((loc, cp))
            base += len(sl.arrays)
        for loc, cp in started:
            loc.turn()
        for loc, cp in started:
            cp.wait_recv()
        for loc, cp in started:
            cp.wait_send()
            loc.wait()

    flat_in = [a for sl in slabs for a in sl.arrays]
    compact = [jax.ShapeDtypeStruct((N_CHIPS, sl.half, sl.cols), F32) for sl in slabs]
    outs = pl.pallas_call(
        body, name=name,
        in_specs=[ANY] * n_in, out_specs=[ANY] * (2 * n), out_shape=compact + compact,
        scratch_shapes=[pltpu.VMEM(s.shape, F32) for s in compact] + [pltpu.SemaphoreType.DMA((N_CHIPS * n,))] * 4,
        compiler_params=_comm_params(sum(_nbytes(s.shape, F32) for s in compact)),
    )(*flat_in)
    return outs[:n], outs[n:]


class _ChipExchange:
    def __init__(self, n, ins, outs, bufs, sems):
        self.n, self.ins, self.outs, self.bufs = n, ins, outs, bufs
        self.send_sems, self.recv_sems, self.in_sems, self.out_sems = sems
        self.x, self.y, self.c = _place()
        self.me = 2 * self.x + self.y
        self.pairs = [(k, j) for k in range(n) for j in range(3)]

    @staticmethod
    def scratch(partials):
        n = len(partials)
        return ([pltpu.VMEM(p.shape[1:], p.dtype) for p in partials]
                + [pltpu.SemaphoreType.DMA((3 * n,))] * 2 + [pltpu.SemaphoreType.DMA((n,))] * 2)

    @staticmethod
    def scratch_bytes(partials):
        return sum(_nbytes(p.shape[1:], p.dtype) for p in partials)

    def _copy(self, k, j, src_slot, dst_slot):
        px, py = _flip(self.x, CHIP_MOVES[j][0]), _flip(self.y, CHIP_MOVES[j][1])
        return pltpu.make_async_remote_copy(
            src_ref=self.ins[k].at[src_slot], dst_ref=self.outs[k].at[dst_slot],
            send_sem=self.send_sems.at[3 * k + j], recv_sem=self.recv_sems.at[3 * k + j],
            device_id=(px, py, self.c), device_id_type=MESH_ID)

    def _peer(self, j):
        return 2 * _flip(self.x, CHIP_MOVES[j][0]) + _flip(self.y, CHIP_MOVES[j][1])

    def _local(self, k):
        return _Bounce(self.ins[k].at[self.me], self.bufs[k], self.outs[k].at[self.me],
                       self.in_sems.at[k], self.out_sems.at[k])

    def start(self):
        for k in range(self.n):
            self._local(k).start()
        for k, j in self.pairs:
            self._copy(k, j, self._peer(j), self.me).start()

    def turn(self):
        for k in range(self.n):
            self._local(k).turn()

    def finish(self):
        for k, j in self.pairs:
            self._copy(k, j, self.me, self._peer(j)).wait_recv()
        for k, j in self.pairs:
            self._copy(k, j, self._peer(j), self.me).wait_send()
        for k in range(self.n):
            self._local(k).wait()


def _hosted_steps(steps):
    return dict(start=0, turn=steps // 4, forward=steps // 2, finish=steps - 1)


def _vec_exchange(vec):
    def body(vec_ref, vec_out, buf, send_sems, recv_sems, in_sem, out_sem):
        x, y, c = _place()
        dev = 4 * x + 2 * y + c

        def copy(j, slot):
            dx, dy, dc = DEV_MOVES[j]
            return pltpu.make_async_remote_copy(
                src_ref=vec_ref, dst_ref=vec_out.at[slot], send_sem=send_sems.at[j], recv_sem=recv_sems.at[j],
                device_id=(_flip(x, dx), _flip(y, dy), _flip(c, dc)), device_id_type=MESH_ID)

        loc = _Bounce(vec_ref, buf, vec_out.at[dev], in_sem, out_sem)
        loc.start()
        for j in range(len(DEV_MOVES)):
            copy(j, dev).start()
        loc.turn()
        for j, (dx, dy, dc) in enumerate(DEV_MOVES):
            copy(j, 4 * _flip(x, dx) + 2 * _flip(y, dy) + _flip(c, dc)).wait_recv()
        for j in range(len(DEV_MOVES)):
            copy(j, dev).wait_send()
        loc.wait()

    return pl.pallas_call(
        body, name="grad_vec_exchange", in_specs=[ANY], out_specs=ANY,
        out_shape=jax.ShapeDtypeStruct((N_DEV,) + vec.shape, F32),
        scratch_shapes=[pltpu.VMEM(vec.shape, F32), pltpu.SemaphoreType.DMA((len(DEV_MOVES),)),
                        pltpu.SemaphoreType.DMA((len(DEV_MOVES),)), pltpu.SemaphoreType.DMA(()), pltpu.SemaphoreType.DMA(())],
        compiler_params=_comm_params(_nbytes(vec.shape, F32)),
    )(vec)


def _pair_share(halves):
    n = len(halves)

    def body(*refs):
        ins, outs, bufs = refs[:n], refs[n:2 * n], refs[2 * n:3 * n]
        send_sems, recv_sems, in_sems, out_sems = refs[3 * n:]
        x, y, c = _place()
        started = []
        for k in range(n):
            hr = halves[k].shape[0]
            rows = lambda hc, k=k, hr=hr: outs[k].at[pl.ds(pl.multiple_of(hc * hr, 8), hr), :]
            loc = _Bounce(ins[k], bufs[k], rows(c), in_sems.at[k], out_sems.at[k])
            loc.start()
            cp = pltpu.make_async_remote_copy(
                src_ref=ins[k], dst_ref=rows(c), send_sem=send_sems.at[k], recv_sem=recv_sems.at[k],
                device_id=(x, y, 1 - c), device_id_type=MESH_ID)
            cp.start()
            recv = pltpu.make_async_remote_copy(
                src_ref=ins[k], dst_ref=rows(1 - c), send_sem=send_sems.at[k], recv_sem=recv_sems.at[k],
                device_id=(x, y, 1 - c), device_id_type=MESH_ID)
            started.append((loc, cp, recv))
        for loc, cp, recv in started:
            loc.turn()
        for loc, cp, recv in started:
            recv.wait_recv()
        for loc, cp, recv in started:
            cp.wait_send()
            loc.wait()

    return pl.pallas_call(
        body, name="grad_pair_share",
        in_specs=[ANY] * n, out_specs=[ANY] * n,
        out_shape=[jax.ShapeDtypeStruct((2 * h.shape[0], h.shape[1]), F32) for h in halves],
        scratch_shapes=[pltpu.VMEM(h.shape, F32) for h in halves] + [pltpu.SemaphoreType.DMA((n,))] * 4,
        compiler_params=_comm_params(sum(_nbytes(h.shape, F32) for h in halves)),
    )(*halves)


def _row_block(rows, cols, n_arrays):
    br = rows
    while br % 16 == 0 and 2 * n_arrays * br * cols * 4 > (16 << 20):
        br //= 2
    return br


def _add2(a, b, out_dtype, name):
    rows, cols = a.shape
    br = _row_block(rows, cols, 3)

    def body(a_ref, b_ref, o_ref):
        o_ref[...] = (a_ref[...] + b_ref[...]).astype(out_dtype)

    spec = pl.BlockSpec((br, cols), lambda i: (i, 0))
    return pl.pallas_call(body, name=name, grid=(rows // br,), in_specs=[spec, spec], out_specs=spec,
                          out_shape=jax.ShapeDtypeStruct(a.shape, out_dtype),
                          compiler_params=_cparams(3 * br * cols * 4, 1))(a, b)


def _sum_slots(a, name):
    n, rows, cols = a.shape
    br = _row_block(rows, cols, n + 1)

    def body(a_ref, o_ref):
        acc = a_ref[0].astype(F32)
        for s in range(1, n):
            acc = acc + a_ref[s].astype(F32)
        o_ref[...] = acc

    return pl.pallas_call(body, name=name, grid=(rows // br,),
                          in_specs=[pl.BlockSpec((n, br, cols), lambda i: (0, i, 0))],
                          out_specs=pl.BlockSpec((br, cols), lambda i: (i, 0)),
                          out_shape=jax.ShapeDtypeStruct((rows, cols), F32),
                          compiler_params=_cparams((n + 1) * br * cols * 4, 1))(a)


def _adamw_math(w, g, m, v):
    m = ADAM_B1 * m + (1.0 - ADAM_B1) * g
    v = ADAM_B2 * v + (1.0 - ADAM_B2) * (g * g)
    m_hat = m / (1.0 - ADAM_B1 ** ADAM_STEP)
    v_hat = v / (1.0 - ADAM_B2 ** ADAM_STEP)
    delta = -ADAM_LR * (m_hat / (jnp.sqrt(v_hat) + ADAM_EPS) + ADAM_WD * w)
    return delta, m, v


def _adamw(g, w, m, v, name):
    rows, cols = g.shape
    br = _row_block(rows, cols, 7)

    def body(g_ref, w_ref, m_ref, v_ref, d_ref, nm_ref, nv_ref):
        d_ref[...], nm_ref[...], nv_ref[...] = _adamw_math(w_ref[...], g_ref[...], m_ref[...], v_ref[...])

    spec = pl.BlockSpec((br, cols), lambda i: (i, 0))
    return pl.pallas_call(body, name=name, grid=(rows // br,), in_specs=[spec] * 4, out_specs=[spec] * 3,
                          out_shape=[jax.ShapeDtypeStruct(g.shape, F32)] * 3,
                          compiler_params=_cparams(7 * br * cols * 4, 1))(g, w, m, v)


ROW_FINAL_G, ROW_PE_G, ROW_LOSS = 0, 1, 2
ROW_CONV_B, ROW_CN_G, ROW_CN_B, ROW_B_PW2 = 8, 9, 10, 11
ROW_LN_G = 16
ROW_ONORM_G, ROW_LB = 24, 25
ROW_CONV_W = 32
SMALL = ["ln_g", "conv_b", "cnorm_g", "cnorm_b", "b_pw2", "onorm_g", "pe_norm_g", "final_g"]
SMALL_ROW = dict(ln_g=ROW_LN_G, conv_b=ROW_CONV_B, cnorm_g=ROW_CN_G, cnorm_b=ROW_CN_B, b_pw2=ROW_B_PW2,
                 onorm_g=ROW_ONORM_G, pe_norm_g=ROW_PE_G, final_g=ROW_FINAL_G)


def _adamw_small(vsum, gcw, lb_logits, params):
    names = SMALL + ["lb_logits", "conv_w"]
    flat = [t for nm in names for t in params[nm]]

    def body(*refs):
        vs_ref, gcw_ref, lbl_ref = refs[:3]
        ins = refs[3:3 + 3 * len(names)]
        outs = refs[3 + 3 * len(names):]
        for q, nm in enumerate(names):
            w_ref, m_ref, v_ref = ins[3 * q:3 * q + 3]
            g_ref, d_ref, nm_ref, nv_ref = outs[4 * q:4 * q + 4]
            if nm == "conv_w":
                g = gcw_ref[...]
            elif nm == "lb_logits":
                lb = _softmax_row0(lbl_ref[...])
                g0 = vs_ref[ROW_LB:ROW_LB + 1, :] * lb * (1.0 - lb)
                g = jnp.concatenate([g0, -g0], axis=0)
            else:
                g = vs_ref[SMALL_ROW[nm]:SMALL_ROW[nm] + 1, :]
            g_ref[...] = g
            d_ref[...], nm_ref[...], nv_ref[...] = _adamw_math(w_ref[...], g, m_ref[...], v_ref[...])

    out_shape = [jax.ShapeDtypeStruct(params[nm][0].shape, F32) for nm in names for _ in range(4)]
    outs = pl.pallas_call(body, name="adamw_small", out_shape=out_shape)(vsum, gcw, lb_logits, *flat)
    return {nm: tuple(outs[4 * q:4 * q + 4]) for q, nm in enumerate(names)}


TOKEN_TILE = dict(inproj_fwd=1024, conv=256, hgrn=512, tail=256, inproj_bwd_x=1024, weight_grad=1024)


def _tile(T, family):
    return min(T, TOKEN_TILE[family])


def kernel(x, p, ln_g, w_in, conv_w, conv_b, cnorm_g, cnorm_b, w_pw2, b_pw2, lb_logits, onorm_g, w_out, pe_norm_g, w_pg, w_pp, final_g, loss_target, m_ln_g, m_w_in, m_conv_w, m_conv_b, m_cnorm_g, m_cnorm_b, m_w_pw2, m_b_pw2, m_lb_logits, m_onorm_g, m_w_out, m_pe_norm_g, m_w_pg, m_w_pp, m_final_g, v_ln_g, v_w_in, v_conv_w, v_conv_b, v_cnorm_g, v_cnorm_b, v_w_pw2, v_b_pw2, v_lb_logits, v_onorm_g, v_w_out, v_pe_norm_g, v_w_pg, v_w_pp, v_final_g):
    given = dict(locals())
    x2, p2, tgt = x[0], p[0, 0], loss_target[0]
    T = x2.shape[0]
    fin_g = final_g.reshape(1, D)

    conv_w_pad = jnp.pad(conv_w[0], ((0, HALO - CONV_K), (0, 0)))
    (w_in_f,) = _all_gather_shards([w_in[0].astype(_MXU)], [1])

    z, u, (w_pw2_f, w_out_f, w_pg_f, w_pp_f, conv_w_f) = _inproj_fwd(
        x2, ln_g, w_in_f,
        [w_pw2[0].astype(_MXU), w_out[0].astype(_MXU), w_pg[0].astype(_MXU), w_pp[0].astype(_MXU), conv_w_pad],
        [0, 0, 0, 1, 1], _tile(T, "inproj_fwd"))
    yc, y1 = _conv_fwd(z, conv_w_f, conv_b, cnorm_g, cnorm_b, w_pw2_f, b_pw2, _tile(T, "conv"))
    o_raw, yh, s_chunks = _hgrn_fwd(z, lb_logits, onorm_g, _tile(T, "hgrn"), HB)
    dyc, dyh, dh, n2, ds, dpe, dhb, pb, vec_tail = _tail(
        x2, yc, yh, p2, tgt, w_out_f, w_pg_f, w_pp_f, pe_norm_g, fin_g, _tile(T, "tail"))
    tk = _tile(T, "weight_grad")
    g_w_out_c = _tn_matmul(yc, dhb, tk, "grad_w_out_conv")
    g_w_out_h = _tn_matmul(yh, dhb, tk, "grad_w_out_hgrn")
    g_w_pg = _tn_matmul(n2, ds, tk, "grad_w_pg")
    g_w_pp = _tn_matmul(pb, dpe, tk, "grad_w_pp")
    dzc, a_act, dy2, vec_conv, g_conv_w = _conv_bwd(z, y1, dyc, conv_w_f, cnorm_g, cnorm_b, w_pw2_f, b_pw2, _tile(T, "conv"))
    g_w_pw2 = _tn_matmul(a_act, dy2, tk, "grad_w_pw2")

    def pair_sums(names, slabs, tag):
        mine, got = _pair_exchange(slabs, "grad_pair_exchange_" + tag)
        return [_add2(a.reshape(-1, a.shape[-1]), b.reshape(-1, b.shape[-1]), _WIRE, "pair_sum_" + nm).reshape(a.shape)
                for a, b, nm in zip(mine, got, names)]

    rest = ["w_pw2", "w_out", "w_pg", "w_pp"]
    partial_rest = pair_sums(rest, [
        _Slab([g_w_pw2], lambda s: (0, s * (D // N_CHIPS), 0), (D // N_CHIPS, D)),
        _Slab([g_w_out_c, g_w_out_h], lambda s: (s // 2, (s % 2) * (D // 2), 0), (D // 2, D)),
        _Slab([g_w_pg], lambda s: (0, s * (D // N_CHIPS), 0), (D // N_CHIPS, D)),
        _Slab([g_w_pp], lambda s: (0, 0, s * (D // N_CHIPS)), (PLE, D // N_CHIPS)),
    ], "rest")
    dzh, vec_hgrn, slots_rest = _hgrn_bwd(z, lb_logits, onorm_g, o_raw, dyh, s_chunks, partial_rest, _tile(T, "hgrn"), HB)
    g_w_in = _inproj_bwd_w(u, dzc, dzh, tk)
    partial_in = pair_sums(["w_in"], [
        _Slab([g_w_in], lambda s: (0, 0, s * (NPART * D // N_CHIPS)), (D, NPART * D // N_CHIPS))], "w_in")
    grad_x, vec_in, slots_in = _inproj_bwd_x(x2, ln_g, dzc, dzh, w_in_f, dh, partial_in, _tile(T, "inproj_bwd_x"))
    big = ["w_in"] + rest
    halves = [_sum_slots(s, "chip_sum_" + nm) for s, nm in zip(list(slots_in) + list(slots_rest), big)]
    vec = jnp.concatenate([vec_tail, vec_conv, vec_in, vec_hgrn, g_conv_w], axis=0)
    vsum = _sum_slots(_vec_exchange(vec), "vec_sum")
    grads_big = _pair_share(halves)

    out = {}
    for nm, g in zip(big, grads_big):
        w2, m2, v2 = given[nm][0], given["m_" + nm][0], given["v_" + nm][0]
        d, nm_, nv_ = _adamw(g, w2, m2, v2, "adamw_" + nm)
        out[nm] = tuple(t[None] for t in (g, d, nm_, nv_))
    chip = 2 * lax.axis_index("x") + lax.axis_index("y")
    gcw = lax.dynamic_slice(vsum, (ROW_CONV_W, chip * (D // N_CHIPS)), (CONV_K, D // N_CHIPS))
    params = {nm: (given[nm].reshape(-1, D), given["m_" + nm].reshape(-1, D), given["v_" + nm].reshape(-1, D))
              for nm in SMALL + ["lb_logits"]}
    params["conv_w"] = (conv_w[0], m_conv_w[0], v_conv_w[0])
    small = _adamw_small(vsum, gcw, lb_logits, params)
    for nm, ts in small.items():
        out[nm] = tuple(t.reshape(given[nm].shape) for t in ts)

    loss = vsum[ROW_LOSS, 0]
    order = ["ln_g", "w_in", "conv_w", "conv_b", "cnorm_g", "cnorm_b", "w_pw2", "b_pw2", "lb_logits", "onorm_g",
             "w_out", "pe_norm_g", "w_pg", "w_pp", "final_g"]
    return (loss, grad_x[None], *[out[nm][0] for nm in order], *[out[nm][1] for nm in order],
            *[out[nm][2] for nm in order], *[out[nm][3] for nm in order])
```

```python
import functools

import jax
import jax.numpy as jnp
from jax import lax
from jax.experimental import pallas as pl
from jax.experimental.pallas import tpu as pltpu

F32 = jnp.float32
BF16 = jnp.bfloat16
_MXU = jnp.bfloat16
_WIRE = jnp.bfloat16

D = 1024
NPART = 7
PLE = 256
HEADS = 8
HD = 128
CHUNK = 64
CONV_K = 31
HALO = 32
EPS = 1e-6
N_CHIPS = 4
N_DEV = 8
HB = 8
VEC_ROWS = 64

ADAM_LR = 0.001
ADAM_B1 = 0.9
ADAM_B2 = 0.999
ADAM_EPS = 1e-08
ADAM_WD = 0.01
ADAM_STEP = 10

V7X_VMEM_LIMIT = 60000 * 1024
MESH_ID = pl.DeviceIdType.MESH
ANY = pl.BlockSpec(memory_space=pltpu.HBM)


def _in_hbm(arrays):
    return [pltpu.with_memory_space_constraint(a, pltpu.HBM) for a in arrays]


def _cparams(block_bytes, n_grid_dims):
    limit = min(V7X_VMEM_LIMIT, 2 * block_bytes + (24 << 20))
    return pltpu.CompilerParams(vmem_limit_bytes=int(limit), dimension_semantics=("arbitrary",) * n_grid_dims)


def _nbytes(shape, dtype):
    n = 1
    for s in shape:
        n *= s
    return n * jnp.dtype(dtype).itemsize


def _dot(a, b):
    return jnp.dot(a.astype(_MXU), b.astype(_MXU), preferred_element_type=F32)


def _dot_nt(a, b):
    return lax.dot_general(a.astype(_MXU), b.astype(_MXU), (((1,), (1,)), ((), ())), preferred_element_type=F32)


def _dot_tn(a, b):
    return lax.dot_general(a.astype(_MXU), b.astype(_MXU), (((0,), (0,)), ((), ())), preferred_element_type=F32)


def _tri_dot(tri_bf, x):
    x1 = x.astype(BF16)
    r1 = x - x1.astype(F32)
    x2 = r1.astype(BF16)
    x3 = (r1 - x2.astype(F32)).astype(BF16)
    d = lambda t: jnp.dot(tri_bf, t, preferred_element_type=F32)
    return d(x1) + d(x2) + d(x3)


def _split2(x):
    hi = x.astype(BF16)
    return hi, (x - hi.astype(F32)).astype(BF16)


def _dot3(dims, a, b):
    d = lambda p, q: lax.dot_general(p, q, (dims, ((), ())), preferred_element_type=F32)
    return d(a[0], b[0]) + d(a[0], b[1]) + d(a[1], b[0])


def _sigmoid(x):
    return jax.nn.sigmoid(x)


def _mean_lanes(x):
    return jnp.mean(x, axis=-1, keepdims=True)


def _sum_rows(x):
    return jnp.sum(x, axis=0, keepdims=True)


def _group_ln(y):
    yn, rs = [], []
    for g in range(D // HD):
        blk = y[:, g * HD:(g + 1) * HD]
        xc = blk - _mean_lanes(blk)
        r = lax.rsqrt(_mean_lanes(xc * xc) + EPS)
        yn.append(xc * r)
        rs.append(jnp.broadcast_to(r, blk.shape))
    return jnp.concatenate(yn, axis=1), jnp.concatenate(rs, axis=1)


def _group_ln_bwd(dyn, yn, rstd):
    out = []
    for g in range(D // HD):
        sl = slice(g * HD, (g + 1) * HD)
        d, n = dyn[:, sl], yn[:, sl]
        out.append(rstd[:, sl] * (d - _mean_lanes(d) - n * _mean_lanes(d * n)))
    return jnp.concatenate(out, axis=1)


def _head_means(x, hb, fn=lambda m: m):
    return jnp.concatenate([jnp.broadcast_to(fn(_mean_lanes(x[:, hh * HD:(hh + 1) * HD])), (x.shape[0], HD))
                            for hh in range(hb)], axis=1)


def _head_rsqrt_mean(x, hb):
    return _head_means(x, hb, lambda m: lax.rsqrt(m + EPS))


def _softmax_row0(lbl):
    m = jnp.max(lbl, axis=0, keepdims=True)
    e = jnp.exp(lbl - m)
    return e[0:1, :] / jnp.sum(e, axis=0, keepdims=True)


def _inproj_fwd(x, ln_g, w_in, shards, axes, tT):
    T = x.shape[0]
    n = len(shards)
    steps = (T // tT) * NPART
    at = _hosted_steps(steps)

    def body(x_ref, g_ref, w_ref, *rest):
        ins, (z_ref, u_ref), outs = rest[:n], rest[n:n + 2], rest[n + 2:2 * n + 2]
        u_scr, bufs, sems = rest[2 * n + 2], rest[2 * n + 3:3 * n + 3], rest[3 * n + 3:]
        step = pl.program_id(0) * NPART + pl.program_id(1)
        gather = _Gather([s.shape for s in shards], axes, ins, outs, bufs, sems)
        for phase in ("start", "turn", "forward"):
            pl.when(step == at[phase])(getattr(gather, phase))

        @pl.when(pl.program_id(1) == 0)
        def _():
            xv = x_ref[...]
            r = lax.rsqrt(_mean_lanes(xv * xv) + EPS)
            u = (xv * r * g_ref[...]).astype(_MXU)
            u_scr[...] = u
            u_ref[...] = u
        z_ref[...] = jnp.dot(u_scr[...], w_ref[...], preferred_element_type=F32)
        pl.when(step == at["finish"])(gather.finish)

    blk = (_nbytes((tT, D), F32) * 2 + _nbytes((D, D), _MXU) + _nbytes((tT, D), _MXU) * 2
           + sum(_nbytes(s.shape, s.dtype) for s in shards))
    outs = pl.pallas_call(
        body, name="inproj_fwd", grid=(T // tT, NPART),
        in_specs=[pl.BlockSpec((tT, D), lambda i, j: (i, 0)), pl.BlockSpec((1, D), lambda i, j: (0, 0)),
                  pl.BlockSpec((D, D), lambda i, j: (0, j))] + [ANY] * n,
        out_specs=[pl.BlockSpec((tT, D), lambda i, j: (i, j)), pl.BlockSpec((tT, D), lambda i, j: (i, 0))] + [ANY] * n,
        out_shape=[jax.ShapeDtypeStruct((T, NPART * D), F32), jax.ShapeDtypeStruct((T, D), _MXU)]
        + [pltpu.HBM(fs, s.dtype) for fs, s in zip(_full_shapes(shards, axes), shards)],
        scratch_shapes=[pltpu.VMEM((tT, D), _MXU)] + _Gather.scratch(shards),
        compiler_params=_cparams(blk, 2),
    )(x, ln_g, w_in, *_in_hbm(shards))
    return outs[0], outs[1], outs[2:]


def _shifted_windows(ext, first, visit):
    n = ext.shape[0]
    for m in range(first, first + CONV_K):
        visit(m, (ext if m == 0 else pltpu.roll(ext, n - m, axis=0))[0:n - HALO, :])


def _conv_fwd(z, conv_w, conv_b, cn_g, cn_b, w_pw2, b_pw2, tT):
    T = z.shape[0]

    def body(cv_ref, cg_ref, ct_ref, cw_ref, cb_ref, ng_ref, nb_ref, wp_ref, bp_ref, yc_ref, y1_ref, ext):
        @pl.when(pl.program_id(0) == 0)
        def _():
            ext[...] = jnp.zeros_like(ext)
        ext[0:HALO, :] = ext[tT:tT + HALO, :]
        ext[HALO:, :] = cv_ref[...] * _sigmoid(cg_ref[...])
        cw = cw_ref[...]
        acc = [cb_ref[...]]

        def tap(m, win):
            acc[0] = acc[0] + win * cw[m - 2:m - 1, :]
        _shifted_windows(ext[...], 2, tap)
        y1 = acc[0]
        y1_ref[...] = y1
        yn, _ = _group_ln(y1)
        apre = yn * ng_ref[...] + nb_ref[...]
        a = apre * _sigmoid(apre)
        y2 = _dot(a, wp_ref[...]) + bp_ref[...]
        ct = ct_ref[...]
        yc_ref[...] = (y2 * (ct * _sigmoid(ct))).astype(_MXU)

    part = lambda p: pl.BlockSpec((tT, D), lambda i: (i, p))
    row = pl.BlockSpec((1, D), lambda i: (0, 0))
    tok = pl.BlockSpec((tT, D), lambda i: (i, 0))
    blk = 4 * _nbytes((tT, D), F32) + _nbytes((D, D), _MXU) + _nbytes((tT, D), _MXU) + 8 * _nbytes((tT + HALO, D), F32)
    return pl.pallas_call(
        body, name="conv_fwd", grid=(T // tT,),
        in_specs=[part(0), part(1), part(2), pl.BlockSpec((HALO, D), lambda i: (0, 0)), row, row, row,
                  pl.BlockSpec((D, D), lambda i: (0, 0)), row],
        out_specs=[tok, tok],
        out_shape=[jax.ShapeDtypeStruct((T, D), _MXU), jax.ShapeDtypeStruct((T, D), F32)],
        scratch_shapes=[pltpu.VMEM((tT + HALO, D), F32)],
        compiler_params=_cparams(blk, 1),
    )(z, z, z, conv_w, conv_b, cn_g, cn_b, w_pw2, b_pw2)


def _hgrn_gates(lb, hq, hf):
    sq = _sigmoid(hq)
    sg = _sigmoid(hf)
    f = lb + (1.0 - lb) * sg
    return sq, sg, f, hq * sq, (1.0 - lb) * (1.0 - sg), jnp.log(f)


def _chunk_decays(lf, q, k):
    r = lax.broadcasted_iota(jnp.int32, (CHUNK, CHUNK), 0)
    c = lax.broadcasted_iota(jnp.int32, (CHUNK, CHUNK), 1)
    b = _tri_dot((r >= c).astype(BF16), lf)
    bm = b[CHUNK // 2 - 1:CHUNK // 2, :]
    bl = b[CHUNK - 1:CHUNK, :]
    eb = jnp.exp(b)
    eqm = jnp.exp(b - bm)
    ekm = jnp.exp(bm - b)
    ekd = jnp.exp(bl - b)
    return dict(causal=r >= c, eb=eb, eqm=eqm, ekm=ekm, ekd=ekd, ebl=jnp.exp(bl),
                qd=q * eb, qm=q * eqm, km=k * ekm, kd=k * ekd)


def _hgrn_fwd(z, lb_logits, onorm_g, tT, hb):
    T = z.shape[0]
    nc = tT // CHUNK
    w = hb * HD

    def body(lbl_ref, og_ref, hq_ref, hf_ref, hi_ref, hg_ref, o_ref, yh_ref, sc_ref, st):
        @pl.when(pl.program_id(1) == 0)
        def _():
            st[...] = jnp.zeros_like(st)
        lb_all = _softmax_row0(lbl_ref[...])
        og_all = og_ref[...]

        def chunk(c, carry):
            sl = pl.ds(pl.multiple_of(c * CHUNK, CHUNK), CHUNK)
            lanes = [slice(hh * HD, (hh + 1) * HD) for hh in range(hb)]
            heads = lambda fn: [fn(hh, ln) for hh, ln in enumerate(lanes)]
            hg, v = hg_ref[sl, :], hi_ref[sl, :]
            _, _, _, q, k, lf = _hgrn_gates(lb_all, hq_ref[sl, :], hf_ref[sl, :])
            dc = _chunk_decays(lf, q, k)
            s_t = heads(lambda hh, ln: st[hh])
            a = heads(lambda hh, ln: jnp.where(dc["causal"], _dot_nt(dc["qm"][:, ln], dc["km"][:, ln]), 0.0))
            o_inter = heads(lambda hh, ln: _dot_nt(dc["qd"][:, ln], s_t[hh]))
            kv = heads(lambda hh, ln: _dot_tn(v[:, ln], dc["kd"][:, ln]))
            o_intra = heads(lambda hh, ln: _dot(a[hh], v[:, ln]))
            for hh, ln in enumerate(lanes):
                sc_ref[hh, c] = s_t[hh]
                st[hh] = s_t[hh] * dc["ebl"][:, ln] + kv[hh]
            o = jnp.concatenate([o_inter[hh] + o_intra[hh] for hh in range(hb)], axis=1)
            o_ref[sl, :] = o
            n = o * _head_rsqrt_mean(o * o, hb)
            yh_ref[sl, :] = ((n * og_all) * (hg * _sigmoid(hg))).astype(_MXU)
            return carry

        lax.fori_loop(0, nc, chunk, 0)

    zpart = lambda p: pl.BlockSpec((tT, w), lambda h, i: (i, p * (HEADS // hb) + h))
    blk = 6 * _nbytes((tT, w), F32) + _nbytes((hb, nc, HD, HD), F32)
    return pl.pallas_call(
        body, name="hgrn_fwd", grid=(HEADS // hb, T // tT),
        in_specs=[pl.BlockSpec((2, w), lambda h, i: (0, h)), pl.BlockSpec((1, w), lambda h, i: (0, h)),
                  zpart(3), zpart(4), zpart(5), zpart(6)],
        out_specs=[pl.BlockSpec((tT, w), lambda h, i: (i, h)), pl.BlockSpec((tT, w), lambda h, i: (i, h)),
                   pl.BlockSpec((hb, nc, HD, HD), lambda h, i: (h, i, 0, 0))],
        out_shape=[jax.ShapeDtypeStruct((T, D), F32), jax.ShapeDtypeStruct((T, D), _MXU),
                   jax.ShapeDtypeStruct((HEADS, T // CHUNK, HD, HD), F32)],
        scratch_shapes=[pltpu.VMEM((hb, HD, HD), F32)],
        compiler_params=_cparams(blk, 2),
    )(lb_logits, onorm_g, z, z, z, z)


def _hgrn_bwd(z, lb_logits, onorm_g, o_raw, dyh, s_chunks, partials, tT, hb):
    T = z.shape[0]
    nc = tT // CHUNK
    nI = T // tT
    w = hb * HD
    n = len(partials)
    at = _hosted_steps((HEADS // hb) * nI)

    def body(lbl_ref, og_ref, hq_ref, hf_ref, hi_ref, hg_ref, o_ref, dy_ref, sc_ref, *rest):
        (dz_ref, vec_ref), dst = rest[n:n + 2], rest[2 * n + 2]
        exchange = _ChipExchange(n, rest[:n], rest[n + 2:2 * n + 2], rest[2 * n + 3:3 * n + 3], rest[3 * n + 3:])
        step = pl.program_id(0) * nI + pl.program_id(1)
        pl.when(step == at["start"])(exchange.start)
        pl.when(step == at["turn"])(exchange.turn)

        @pl.when(pl.program_id(1) == 0)
        def _():
            dst[...] = jnp.zeros_like(dst)
            vec_ref[...] = jnp.zeros_like(vec_ref)
        lb_all = _softmax_row0(lbl_ref[...])
        og_all = og_ref[...]
        last_row = lax.broadcasted_iota(jnp.int32, (CHUNK, w), 0) == CHUNK - 1
        r64 = lax.broadcasted_iota(jnp.int32, (CHUNK, CHUNK), 0)
        c64 = lax.broadcasted_iota(jnp.int32, (CHUNK, CHUNK), 1)
        upper = (c64 >= r64).astype(BF16)
        lanes = [slice(hh * HD, (hh + 1) * HD) for hh in range(hb)]
        heads = lambda fn: [fn(hh, ln) for hh, ln in enumerate(lanes)]
        wide = lambda parts: jnp.concatenate(parts, axis=1)

        def chunk(cc, carry):
            c = nc - 1 - cc
            sl = pl.ds(pl.multiple_of(c * CHUNK, CHUNK), CHUNK)
            hq, hg, v = hq_ref[sl, :], hg_ref[sl, :], hi_ref[sl, :]
            sq, sg, f, q, k, lf = _hgrn_gates(lb_all, hq, hf_ref[sl, :])
            dc = _chunk_decays(lf, q, k)
            s_t = heads(lambda hh, ln: sc_ref[hh, c])
            ds_t = heads(lambda hh, ln: dst[hh])
            o, dy = o_ref[sl, :], dy_ref[sl, :]
            r = _head_rsqrt_mean(o * o, hb)
            n = o * r
            sgg = _sigmoid(hg)
            silu_g = hg * sgg
            dhg = dy * (n * og_all) * (sgg * (1.0 + hg * (1.0 - sgg)))
            dn = dy * og_all * silu_g
            g_og = _sum_rows(dy * n * silu_g)
            do = r * (dn - n * _head_means(dn * n, hb))
            a = heads(lambda hh, ln: jnp.where(dc["causal"], _dot_nt(dc["qm"][:, ln], dc["km"][:, ln]), 0.0))
            dam = heads(lambda hh, ln: jnp.where(dc["causal"], _dot_nt(do[:, ln], v[:, ln]), 0.0))
            dqd = wide(heads(lambda hh, ln: _dot(do[:, ln], s_t[hh])))
            dkd = wide(heads(lambda hh, ln: _dot(v[:, ln], ds_t[hh])))
            dv_inter = heads(lambda hh, ln: _dot_nt(dc["kd"][:, ln], ds_t[hh]))
            dqs = heads(lambda hh, ln: _dot_tn(do[:, ln], dc["qd"][:, ln]))
            dv = wide(heads(lambda hh, ln: _dot_tn(a[hh], do[:, ln]) + dv_inter[hh]))
            dam2 = [_split2(t) for t in dam]
            km2, qm2 = _split2(dc["km"]), _split2(dc["qm"])
            dqm = wide(heads(lambda hh, ln: _dot3(((1,), (0,)), dam2[hh], (km2[0][:, ln], km2[1][:, ln]))))
            dkm = wide(heads(lambda hh, ln: _dot3(((0,), (0,)), dam2[hh], (qm2[0][:, ln], qm2[1][:, ln]))))
            debl = wide(heads(lambda hh, ln: _sum_rows(ds_t[hh] * s_t[hh])))
            for hh, ln in enumerate(lanes):
                dst[hh] = ds_t[hh] * dc["ebl"][:, ln] + dqs[hh]
            dq = dqd * dc["eb"] + dqm * dc["eqm"]
            dk = dkm * dc["ekm"] + dkd * dc["ekd"]
            dbl = _sum_rows(dkd * dc["kd"]) + debl * dc["ebl"]
            db = dq * q - dk * k + jnp.where(last_row, dbl, 0.0)
            dlf = _tri_dot(upper, db)
            dfk = dlf / f - dk
            dz_ref[0, sl, :] = (dq * (sq * (1.0 + hq * (1.0 - sq)))).astype(_MXU)
            dz_ref[1, sl, :] = (dfk * ((1.0 - lb_all) * sg * (1.0 - sg))).astype(_MXU)
            dz_ref[2, sl, :] = dv.astype(_MXU)
            dz_ref[3, sl, :] = dhg.astype(_MXU)
            vec_ref[0:1, :] += g_og
            vec_ref[1:2, :] += _sum_rows(dfk * (1.0 - sg))
            return carry

        lax.fori_loop(0, nc, chunk, 0)
        pl.when(step == at["finish"])(exchange.finish)

    zpart = lambda p: pl.BlockSpec((tT, w), lambda h, i: (nI - 1 - i, p * (HEADS // hb) + h))
    act = pl.BlockSpec((tT, w), lambda h, i: (nI - 1 - i, h))
    blk = (6 * _nbytes((tT, w), F32) + _nbytes((hb, nc, HD, HD), F32) + 4 * _nbytes((tT, w), _MXU)
           + _ChipExchange.scratch_bytes(partials))
    outs = pl.pallas_call(
        body, name="hgrn_bwd", grid=(HEADS // hb, nI),
        in_specs=[pl.BlockSpec((2, w), lambda h, i: (0, h)), pl.BlockSpec((1, w), lambda h, i: (0, h)),
                  zpart(3), zpart(4), zpart(5), zpart(6), act, act,
                  pl.BlockSpec((hb, nc, HD, HD), lambda h, i: (h, nI - 1 - i, 0, 0))] + [ANY] * n,
        out_specs=[pl.BlockSpec((4, tT, w), lambda h, i: (0, nI - 1 - i, h)),
                   pl.BlockSpec((8, w), lambda h, i: (0, h))] + [ANY] * n,
        out_shape=[jax.ShapeDtypeStruct((4, T, D), _MXU), jax.ShapeDtypeStruct((8, D), F32)]
        + [pltpu.HBM(p.shape, p.dtype) for p in partials],
        scratch_shapes=[pltpu.VMEM((hb, HD, HD), F32)] + _ChipExchange.scratch(partials),
        compiler_params=_cparams(blk, 2),
    )(lb_logits, onorm_g, z, z, z, z, o_raw, dyh, s_chunks, *_in_hbm(partials))
    return outs[0], outs[1], outs[2:]


def _tail(x, yc, yh, p, target, w_out, w_pg, w_pp, pe_g, fin_g, tT):
    T = x.shape[0]

    def body(x_ref, yc_ref, yh_ref, p_ref, t_ref, wo_ref, wg_ref, wp_ref, pg_ref, fg_ref,
             dyc_ref, dyh_ref, dh_ref, n2_ref, ds_ref, dpe_ref, dhb_ref, pb_ref, vec_ref):
        @pl.when(pl.program_id(0) == 0)
        def _():
            vec_ref[...] = jnp.zeros_like(vec_ref)
        wo_c, wo_h = wo_ref[0:D, :], wo_ref[D:2 * D, :]
        h = x_ref[...] + _dot(yc_ref[...], wo_c) + _dot(yh_ref[...], wo_h)
        pb = p_ref[...].astype(_MXU)
        pe = _dot(pb, wp_ref[...])
        r2 = lax.rsqrt(_mean_lanes(h * h) + EPS)
        hn = h * r2
        n2 = (hn * pg_ref[...]).astype(_MXU)
        gate = _sigmoid(_dot(n2, wg_ref[...]))
        h2 = h + gate * pe
        r3 = lax.rsqrt(_mean_lanes(h2 * h2) + EPS)
        h2n = h2 * r3
        err = h2n * fg_ref[...] - t_ref[...]
        vec_ref[ROW_LOSS:ROW_LOSS + 1, :] += 0.5 * jnp.sum(_mean_lanes(err * err))
        dout = err * (1.0 / D)
        vec_ref[0:1, :] += _sum_rows(dout * h2n)
        dn3 = dout * fg_ref[...]
        dh2 = r3 * (dn3 - h2n * _mean_lanes(dn3 * h2n))
        ds = (dh2 * pe * gate * (1.0 - gate)).astype(_MXU)
        dn2 = _dot_nt(ds, wg_ref[...])
        vec_ref[1:2, :] += _sum_rows(dn2 * hn)
        dnn = dn2 * pg_ref[...]
        dh = dh2 + r2 * (dnn - hn * _mean_lanes(dnn * hn))
        dhb = dh.astype(_MXU)
        dyc_ref[...] = _dot_nt(dhb, wo_c)
        dyh_ref[...] = _dot_nt(dhb, wo_h)
        dh_ref[...] = dh
        n2_ref[...] = n2
        ds_ref[...] = ds
        dpe_ref[...] = (dh2 * gate).astype(_MXU)
        dhb_ref[...] = dhb
        pb_ref[...] = pb

    tok = lambda w: pl.BlockSpec((tT, w), lambda i: (i, 0))
    full = lambda r, c: pl.BlockSpec((r, c), lambda i: (0, 0))
    tokshape = lambda w, dt: jax.ShapeDtypeStruct((T, w), dt)
    blk = (5 * _nbytes((tT, D), F32) + 7 * _nbytes((tT, D), _MXU) + _nbytes((4 * D + PLE, D), _MXU)
           + 12 * _nbytes((tT, D), F32))
    return pl.pallas_call(
        body, name="tail_fwd_bwd", grid=(T // tT,),
        in_specs=[tok(D), tok(D), tok(D), tok(PLE), tok(D), full(2 * D, D), full(D, D), full(PLE, D), full(1, D), full(1, D)],
        out_specs=[tok(D), tok(D), tok(D), tok(D), tok(D), tok(D), tok(D), tok(PLE), full(8, D)],
        out_shape=[tokshape(D, F32), tokshape(D, F32), tokshape(D, F32), tokshape(D, _MXU), tokshape(D, _MXU),
                   tokshape(D, _MXU), tokshape(D, _MXU), tokshape(PLE, _MXU),
                   jax.ShapeDtypeStruct((8, D), F32)],
        compiler_params=_cparams(blk, 1),
    )(x, yc, yh, p, target, w_out, w_pg, w_pp, pe_g, fin_g)


def _conv_bwd(z, y1, dyc, conv_w, cn_g, cn_b, w_pw2, b_pw2, tT):
    T = z.shape[0]
    nI = T // tT
    hb = tT // HALO

    def body(cv_ref, cg_ref, ct_ref, hv_ref, hg_ref, y1_ref, dyc_ref, cw_ref, ng_ref, nb_ref, wp_ref, bp_ref,
             dz_ref, a_ref, dy2_ref, vec_ref, gcw_ref, ext, ext2, gpart):
        i = pl.program_id(0)

        @pl.when(i == 0)
        def _():
            ext2[...] = jnp.zeros_like(ext2)
            gpart[...] = jnp.zeros_like(gpart)
            vec_ref[...] = jnp.zeros_like(vec_ref)
        cv, cg, ct = cv_ref[...], cg_ref[...], ct_ref[...]
        sg = _sigmoid(cg)
        has_hist = (i < nI - 1).astype(F32)
        ext[0:HALO, :] = hv_ref[...] * _sigmoid(hg_ref[...]) * has_hist
        ext[HALO:, :] = cv * sg
        yn, rstd = _group_ln(y1_ref[...])
        apre = yn * ng_ref[...] + nb_ref[...]
        sa = _sigmoid(apre)
        a = (apre * sa).astype(_MXU)
        y2 = _dot(a, wp_ref[...]) + bp_ref[...]
        st = _sigmoid(ct)
        dyc_v = dyc_ref[...]
        dy2 = dyc_v * (ct * st)
        dy2b = dy2.astype(_MXU)
        da = _dot_nt(dy2b, wp_ref[...])
        dapre = da * (sa * (1.0 + apre * (1.0 - sa)))
        dy1 = _group_ln_bwd(dapre * ng_ref[...], yn, rstd)
        vec_ref[0:1, :] += _sum_rows(dy1)
        vec_ref[1:2, :] += _sum_rows(dapre * yn)
        vec_ref[2:3, :] += _sum_rows(dapre)
        vec_ref[3:4, :] += _sum_rows(dy2)
        dz_ref[2] = (dyc_v * y2 * (st * (1.0 + ct * (1.0 - st)))).astype(_MXU)
        a_ref[...] = a
        dy2_ref[...] = dy2b
        ext2[tT:tT + HALO, :] = ext2[0:HALO, :]
        ext2[0:tT, :] = dy1
        def grad_tap(m, win):
            p = dy1 * win
            part = p[0:8, :]
            for q in range(1, tT // 8):
                part = part + p[8 * q:8 * q + 8, :]
            gpart[m - 2] += part
        _shifted_windows(ext[...], 2, grad_tap)
        cw = cw_ref[...]
        acc = [None]

        def dv_tap(m, win):
            term = win * cw[CONV_K - 1 - m:CONV_K - m, :]
            acc[0] = term if acc[0] is None else acc[0] + term
        _shifted_windows(ext2[...], 0, dv_tap)
        dv = acc[0]
        dz_ref[0] = (dv * sg).astype(_MXU)
        dz_ref[1] = (dv * cv * sg * (1.0 - sg)).astype(_MXU)

        @pl.when(i == nI - 1)
        def _():
            gcw_ref[...] = jnp.sum(gpart[...], axis=1)

    part = lambda p: pl.BlockSpec((tT, D), lambda i: (nI - 1 - i, p))
    hist = lambda p: pl.BlockSpec((HALO, D), lambda i: (jnp.maximum((nI - 1 - i) * hb - 1, 0), p))
    tok = pl.BlockSpec((tT, D), lambda i: (nI - 1 - i, 0))
    row = pl.BlockSpec((1, D), lambda i: (0, 0))
    blk = (5 * _nbytes((tT, D), F32) + _nbytes((D, D), _MXU) + 5 * _nbytes((tT, D), _MXU)
           + 10 * _nbytes((tT + HALO, D), F32))
    return pl.pallas_call(
        body, name="conv_bwd", grid=(nI,),
        in_specs=[part(0), part(1), part(2), hist(0), hist(1), tok, tok, pl.BlockSpec((HALO, D), lambda i: (0, 0)),
                  row, row, pl.BlockSpec((D, D), lambda i: (0, 0)), row],
        out_specs=[pl.BlockSpec((3, tT, D), lambda i: (0, nI - 1 - i, 0)), tok, tok,
                   pl.BlockSpec((8, D), lambda i: (0, 0)), pl.BlockSpec((HALO, D), lambda i: (0, 0))],
        out_shape=[jax.ShapeDtypeStruct((3, T, D), _MXU), jax.ShapeDtypeStruct((T, D), _MXU),
                   jax.ShapeDtypeStruct((T, D), _MXU), jax.ShapeDtypeStruct((8, D), F32),
                   jax.ShapeDtypeStruct((HALO, D), F32)],
        scratch_shapes=[pltpu.VMEM((tT + HALO, D), F32), pltpu.VMEM((tT + HALO, D), F32), pltpu.VMEM((HALO, 8, D), F32)],
        compiler_params=_cparams(blk, 1),
    )(z, z, z, z, z, y1, dyc, conv_w, cn_g, cn_b, w_pw2, b_pw2)


def _inproj_bwd_x(x, ln_g, dzc, dzh, w_in, dh, partials, tT):
    T = x.shape[0]
    n = len(partials)
    at = _hosted_steps((T // tT) * NPART)

    def body(x_ref, g_ref, dzc_ref, dzh_ref, w_ref, dh_ref, *rest):
        (gx_ref, vec_ref), du = rest[n:n + 2], rest[2 * n + 2]
        exchange = _ChipExchange(n, rest[:n], rest[n + 2:2 * n + 2], rest[2 * n + 3:3 * n + 3], rest[3 * n + 3:])
        i, j = pl.program_id(0), pl.program_id(1)
        step = i * NPART + j
        pl.when(step == at["start"])(exchange.start)
        pl.when(step == at["turn"])(exchange.turn)

        @pl.when(j == 0)
        def _():
            du[...] = jnp.zeros_like(du)

        @pl.when(jnp.logical_and(i == 0, j == 0))
        def _():
            vec_ref[...] = jnp.zeros_like(vec_ref)

        @pl.when(j < 3)
        def _():
            du[...] += _dot_nt(dzc_ref[0], w_ref[...])

        @pl.when(j >= 3)
        def _():
            du[...] += _dot_nt(dzh_ref[0], w_ref[...])

        @pl.when(j == NPART - 1)
        def _():
            xv = x_ref[...]
            r = lax.rsqrt(_mean_lanes(xv * xv) + EPS)
            xn = xv * r
            duv = du[...]
            vec_ref[0:1, :] += _sum_rows(duv * xn)
            dun = duv * g_ref[...]
            gx_ref[...] = dh_ref[...] + r * (dun - xn * _mean_lanes(dun * xn))
        pl.when(step == at["finish"])(exchange.finish)

    tok = pl.BlockSpec((tT, D), lambda i, j: (i, 0))
    blk = (3 * _nbytes((tT, D), F32) + 2 * _nbytes((tT, D), _MXU) + _nbytes((D, D), _MXU) + 4 * _nbytes((tT, D), F32)
           + _ChipExchange.scratch_bytes(partials))
    outs = pl.pallas_call(
        body, name="inproj_bwd_x", grid=(T // tT, NPART),
        in_specs=[tok, pl.BlockSpec((1, D), lambda i, j: (0, 0)),
                  pl.BlockSpec((1, tT, D), lambda i, j: (jnp.minimum(j, 2), i, 0)),
                  pl.BlockSpec((1, tT, D), lambda i, j: (jnp.maximum(j - 3, 0), i, 0)),
                  pl.BlockSpec((D, D), lambda i, j: (0, j)), tok] + [ANY] * n,
        out_specs=[tok, pl.BlockSpec((8, D), lambda i, j: (0, 0))] + [ANY] * n,
        out_shape=[jax.ShapeDtypeStruct((T, D), F32), jax.ShapeDtypeStruct((8, D), F32)]
        + [pltpu.HBM(p.shape, p.dtype) for p in partials],
        scratch_shapes=[pltpu.VMEM((tT, D), F32)] + _ChipExchange.scratch(partials),
        compiler_params=_cparams(blk, 2),
    )(x, ln_g, dzc, dzh, w_in, dh, *_in_hbm(partials))
    return outs[0], outs[1], outs[2:]


def _inproj_bwd_w(u, dzc, dzh, tk):
    T = u.shape[0]
    nK = T // tk

    def body(u_ref, dzc_ref, dzh_ref, gw_ref):
        j, k = pl.program_id(0), pl.program_id(1)

        @pl.when(k == 0)
        def _():
            gw_ref[...] = jnp.zeros_like(gw_ref)

        @pl.when(j < 3)
        def _():
            gw_ref[...] += _dot_tn(u_ref[...], dzc_ref[0])

        @pl.when(j >= 3)
        def _():
            gw_ref[...] += _dot_tn(u_ref[...], dzh_ref[0])

    blk = 3 * _nbytes((tk, D), _MXU) + 2 * _nbytes((D, D), F32)
    return pl.pallas_call(
        body, name="inproj_bwd_w", grid=(NPART, nK),
        in_specs=[pl.BlockSpec((tk, D), lambda j, k: (k, 0)),
                  pl.BlockSpec((1, tk, D), lambda j, k: (jnp.minimum(j, 2), jnp.where(j < 3, k, nK - 1), 0)),
                  pl.BlockSpec((1, tk, D), lambda j, k: (jnp.maximum(j - 3, 0), jnp.where(j < 3, 0, k), 0))],
        out_specs=pl.BlockSpec((D, D), lambda j, k: (0, j)),
        out_shape=jax.ShapeDtypeStruct((D, NPART * D), F32),
        compiler_params=_cparams(blk, 2),
    )(u, dzc, dzh)


def _tn_matmul(a, b, tk, name):
    T, M = a.shape
    N = b.shape[1]

    def body(a_ref, b_ref, o_ref):
        @pl.when(pl.program_id(0) == 0)
        def _():
            o_ref[...] = jnp.zeros_like(o_ref)
        o_ref[...] += _dot_tn(a_ref[...], b_ref[...])

    blk = _nbytes((tk, M), _MXU) + _nbytes((tk, N), _MXU) + 2 * _nbytes((M, N), F32)
    return pl.pallas_call(
        body, name=name, grid=(T // tk,),
        in_specs=[pl.BlockSpec((tk, M), lambda k: (k, 0)), pl.BlockSpec((tk, N), lambda k: (k, 0))],
        out_specs=pl.BlockSpec((M, N), lambda k: (0, 0)),
        out_shape=pltpu.HBM((M, N), F32),
        compiler_params=_cparams(blk, 1),
    )(a, b)


def _place():
    return lax.axis_index("x"), lax.axis_index("y"), lax.axis_index("c")


def _flip(v, d):
    return 1 - v if d else v


CHIP_MOVES = [(1, 0), (0, 1), (1, 1)]
DEV_MOVES = [(dx, dy, dc) for dx in (0, 1) for dy in (0, 1) for dc in (0, 1)][1:]


def _shard_slice(ref, axis, size, s):
    start = pl.multiple_of(s * size, size)
    return ref.at[pl.ds(start, size), :] if axis == 0 else ref.at[:, pl.ds(start, size)]


class _Bounce:
    def __init__(self, src, buf, dst, sem_in, sem_out):
        self.load = pltpu.make_async_copy(src, buf, sem_in)
        self.store = pltpu.make_async_copy(buf, dst, sem_out)

    def start(self):
        self.load.start()

    def turn(self):
        self.load.wait()
        self.store.start()

    def wait(self):
        self.store.wait()


def _comm_params(scratch_bytes):
    return pltpu.CompilerParams(vmem_limit_bytes=int(min(V7X_VMEM_LIMIT, scratch_bytes + (8 << 20))))


class _Gather:
    def __init__(self, shapes, axes, ins, outs, bufs, sems):
        self.shapes, self.axes, self.ins, self.outs, self.bufs = shapes, axes, ins, outs, bufs
        self.ici_send, self.ici_recv, self.d2d_send, self.d2d_recv, self.in_sems, self.out_sems = sems
        self.x, self.y, self.c = _place()
        self.me = 2 * self.x + self.y
        self.pairs = [(k, j) for k in range(len(shapes)) for j in range(3)]

    @staticmethod
    def scratch(shards):
        n = len(shards)
        return ([pltpu.VMEM(s.shape, s.dtype) for s in shards]
                + [pltpu.SemaphoreType.DMA((3 * n,))] * 4 + [pltpu.SemaphoreType.DMA((n,))] * 2)

    def _own_half(self, k, hc):
        half = self.shapes[k][0] // 2
        return self.ins[k].at[pl.ds(pl.multiple_of(hc * half, 16), half), :]

    def _region(self, k, who, hc):
        rows, cols = self.shapes[k]
        half = rows // 2
        if self.axes[k] == 0:
            return self.outs[k].at[pl.ds(pl.multiple_of(who * rows + hc * half, 16), half), :]
        return self.outs[k].at[pl.ds(pl.multiple_of(hc * half, 16), half), pl.ds(pl.multiple_of(who * cols, HD), cols)]

    def _peer(self, j):
        return 2 * _flip(self.x, CHIP_MOVES[j][0]) + _flip(self.y, CHIP_MOVES[j][1])

    def _ici(self, k, j, who, hc):
        dx, dy = CHIP_MOVES[j]
        return pltpu.make_async_remote_copy(
            src_ref=self._own_half(k, hc), dst_ref=self._region(k, who, hc),
            send_sem=self.ici_send.at[3 * k + j], recv_sem=self.ici_recv.at[3 * k + j],
            device_id=(_flip(self.x, dx), _flip(self.y, dy), self.c), device_id_type=MESH_ID)

    def _d2d(self, k, j, who, hc):
        return pltpu.make_async_remote_copy(
            src_ref=self._region(k, who, hc), dst_ref=self._region(k, who, hc),
            send_sem=self.d2d_send.at[3 * k + j], recv_sem=self.d2d_recv.at[3 * k + j],
            device_id=(self.x, self.y, 1 - self.c), device_id_type=MESH_ID)

    def _local(self, k):
        size = self.shapes[k][self.axes[k]]
        return _Bounce(self.ins[k], self.bufs[k], _shard_slice(self.outs[k], self.axes[k], size, self.me),
                       self.in_sems.at[k], self.out_sems.at[k])

    def start(self):
        for k in range(len(self.shapes)):
            self._local(k).start()
        for k, j in self.pairs:
            self._ici(k, j, self.me, self.c).start()

    def turn(self):
        for k in range(len(self.shapes)):
            self._local(k).turn()

    def forward(self):
        for k, j in self.pairs:
            self._ici(k, j, self._peer(j), self.c).wait_recv()
            self._d2d(k, j, self._peer(j), self.c).start()

    def finish(self):
        for k, j in self.pairs:
            self._d2d(k, j, self._peer(j), 1 - self.c).wait_recv()
        for k, j in self.pairs:
            self._ici(k, j, self.me, self.c).wait_send()
            self._d2d(k, j, self._peer(j), self.c).wait_send()
        for k in range(len(self.shapes)):
            self._local(k).wait()


def _full_shapes(shards, axes):
    return [tuple(d * (N_CHIPS if a == ax else 1) for a, d in enumerate(s.shape)) for s, ax in zip(shards, axes)]


def _all_gather_shards(shards, axes):
    n = len(shards)

    def body(*refs):
        g = _Gather([s.shape for s in shards], axes, refs[:n], refs[n:2 * n], refs[2 * n:3 * n], refs[3 * n:])
        g.start()
        g.turn()
        g.forward()
        g.finish()

    return pl.pallas_call(
        body, name="gather_weights",
        in_specs=[ANY] * n, out_specs=[ANY] * n,
        out_shape=[pltpu.HBM(fs, s.dtype) for fs, s in zip(_full_shapes(shards, axes), shards)],
        scratch_shapes=_Gather.scratch(shards),
        compiler_params=_comm_params(sum(_nbytes(s.shape, s.dtype) for s in shards)),
    )(*_in_hbm(shards))


class _Slab:
    def __init__(self, arrays, pick, shard_shape):
        self.arrays = arrays
        self.pick = pick
        self.rows, self.cols = shard_shape
        self.half = self.rows // 2


def _pair_exchange(slabs, name):
    n = len(slabs)
    n_in = sum(len(sl.arrays) for sl in slabs)

    def body(*refs):
        ins = refs[:n_in]
        mine, got = refs[n_in:n_in + n], refs[n_in + n:n_in + 2 * n]
        bufs = refs[n_in + 2 * n:n_in + 3 * n]
        send_sems, recv_sems, in_sems, out_sems = refs[n_in + 3 * n:]
        x, y, c = _place()
        started = []
        base = 0
        for k, sl in enumerate(slabs):
            for s in range(N_CHIPS):
                ai, r0, c0 = sl.pick(s)
                src = ins[base + ai]

                def half(hc):
                    return src.at[pl.ds(pl.multiple_of(r0 + hc * sl.half, 8), sl.half), pl.ds(c0, sl.cols)]
                q = N_CHIPS * k + s
                loc = _Bounce(half(c), bufs[k].at[s], mine[k].at[s], in_sems.at[q], out_sems.at[q])
                loc.start()
                cp = pltpu.make_async_remote_copy(
                    src_ref=half(1 - c), dst_ref=got[k].at[s], send_sem=send_sems.at[q], recv_sem=recv_sems.at[q],
                    device_id=(x, y, 1 - c), device_id_type=MESH_ID)
                cp.start()
                started.append((loc, cp))
            base += len(sl.arrays)
        for loc, cp in started:
            loc.turn()
        for loc, cp in started:
            cp.wait_recv()
        for loc, cp in started:
            cp.wait_send()
            loc.wait()

    flat_in = [a for sl in slabs for a in sl.arrays]
    compact = [pltpu.HBM((N_CHIPS, sl.half, sl.cols), F32) for sl in slabs]
    outs = pl.pallas_call(
        body, name=name,
        in_specs=[ANY] * n_in, out_specs=[ANY] * (2 * n), out_shape=compact + compact,
        scratch_shapes=[pltpu.VMEM(s.shape, F32) for s in compact] + [pltpu.SemaphoreType.DMA((N_CHIPS * n,))] * 4,
        compiler_params=_comm_params(sum(_nbytes(s.shape, F32) for s in compact)),
    )(*_in_hbm(flat_in))
    return outs[:n], outs[n:]


class _ChipExchange:
    def __init__(self, n, ins, outs, bufs, sems):
        self.n, self.ins, self.outs, self.bufs = n, ins, outs, bufs
        self.send_sems, self.recv_sems, self.in_sems, self.out_sems = sems
        self.x, self.y, self.c = _place()
        self.me = 2 * self.x + self.y
        self.pairs = [(k, j) for k in range(n) for j in range(3)]

    @staticmethod
    def scratch(partials):
        n = len(partials)
        return ([pltpu.VMEM(p.shape[1:], p.dtype) for p in partials]
                + [pltpu.SemaphoreType.DMA((3 * n,))] * 2 + [pltpu.SemaphoreType.DMA((n,))] * 2)

    @staticmethod
    def scratch_bytes(partials):
        return sum(_nbytes(p.shape[1:], p.dtype) for p in partials)

    def _copy(self, k, j, src_slot, dst_slot):
        px, py = _flip(self.x, CHIP_MOVES[j][0]), _flip(self.y, CHIP_MOVES[j][1])
        return pltpu.make_async_remote_copy(
            src_ref=self.ins[k].at[src_slot], dst_ref=self.outs[k].at[dst_slot],
            send_sem=self.send_sems.at[3 * k + j], recv_sem=self.recv_sems.at[3 * k + j],
            device_id=(px, py, self.c), device_id_type=MESH_ID)

    def _peer(self, j):
        return 2 * _flip(self.x, CHIP_MOVES[j][0]) + _flip(self.y, CHIP_MOVES[j][1])

    def _local(self, k):
        return _Bounce(self.ins[k].at[self.me], self.bufs[k], self.outs[k].at[self.me],
                       self.in_sems.at[k], self.out_sems.at[k])

    def start(self):
        for k in range(self.n):
            self._local(k).start()
        for k, j in self.pairs:
            self._copy(k, j, self._peer(j), self.me).start()

    def turn(self):
        for k in range(self.n):
            self._local(k).turn()

    def finish(self):
        for k, j in self.pairs:
            self._copy(k, j, self.me, self._peer(j)).wait_recv()
        for k, j in self.pairs:
            self._copy(k, j, self._peer(j), self.me).wait_send()
        for k in range(self.n):
            self._local(k).wait()


def _hosted_steps(steps):
    return dict(start=0, turn=steps // 4, forward=steps // 2, finish=steps - 1)


def _vec_exchange(vec):
    def body(vec_ref, vec_out, buf, send_sems, recv_sems, in_sem, out_sem):
        x, y, c = _place()
        dev = 4 * x + 2 * y + c

        def copy(j, slot):
            dx, dy, dc = DEV_MOVES[j]
            return pltpu.make_async_remote_copy(
                src_ref=vec_ref, dst_ref=vec_out.at[slot], send_sem=send_sems.at[j], recv_sem=recv_sems.at[j],
                device_id=(_flip(x, dx), _flip(y, dy), _flip(c, dc)), device_id_type=MESH_ID)

        loc = _Bounce(vec_ref, buf, vec_out.at[dev], in_sem, out_sem)
        loc.start()
        for j in range(len(DEV_MOVES)):
            copy(j, dev).start()
        loc.turn()
        for j, (dx, dy, dc) in enumerate(DEV_MOVES):
            copy(j, 4 * _flip(x, dx) + 2 * _flip(y, dy) + _flip(c, dc)).wait_recv()
        for j in range(len(DEV_MOVES)):
            copy(j, dev).wait_send()
        loc.wait()

    return pl.pallas_call(
        body, name="grad_vec_exchange", in_specs=[ANY], out_specs=ANY,
        out_shape=pltpu.HBM((N_DEV,) + vec.shape, F32),
        scratch_shapes=[pltpu.VMEM(vec.shape, F32), pltpu.SemaphoreType.DMA((len(DEV_MOVES),)),
                        pltpu.SemaphoreType.DMA((len(DEV_MOVES),)), pltpu.SemaphoreType.DMA(()), pltpu.SemaphoreType.DMA(())],
        compiler_params=_comm_params(_nbytes(vec.shape, F32)),
    )(*_in_hbm([vec]))


def _pair_share(halves):
    n = len(halves)

    def body(*refs):
        ins, outs, bufs = refs[:n], refs[n:2 * n], refs[2 * n:3 * n]
        send_sems, recv_sems, in_sems, out_sems = refs[3 * n:]
        x, y, c = _place()
        started = []
        for k in range(n):
            hr = halves[k].shape[0]
            rows = lambda hc, k=k, hr=hr: outs[k].at[pl.ds(pl.multiple_of(hc * hr, 8), hr), :]
            loc = _Bounce(ins[k], bufs[k], rows(c), in_sems.at[k], out_sems.at[k])
            loc.start()
            cp = pltpu.make_async_remote_copy(
                src_ref=ins[k], dst_ref=rows(c), send_sem=send_sems.at[k], recv_sem=recv_sems.at[k],
                device_id=(x, y, 1 - c), device_id_type=MESH_ID)
            cp.start()
            recv = pltpu.make_async_remote_copy(
                src_ref=ins[k], dst_ref=rows(1 - c), send_sem=send_sems.at[k], recv_sem=recv_sems.at[k],
                device_id=(x, y, 1 - c), device_id_type=MESH_ID)
            started.append((loc, cp, recv))
        for loc, cp, recv in started:
            loc.turn()
        for loc, cp, recv in started:
            recv.wait_recv()
        for loc, cp, recv in started:
            cp.wait_send()
            loc.wait()

    return pl.pallas_call(
        body, name="grad_pair_share",
        in_specs=[ANY] * n, out_specs=[ANY] * n,
        out_shape=[pltpu.HBM((2 * h.shape[0], h.shape[1]), F32) for h in halves],
        scratch_shapes=[pltpu.VMEM(h.shape, F32) for h in halves] + [pltpu.SemaphoreType.DMA((n,))] * 4,
        compiler_params=_comm_params(sum(_nbytes(h.shape, F32) for h in halves)),
    )(*_in_hbm(halves))


def _row_block(rows, cols, n_arrays):
    br = rows
    while br % 16 == 0 and 2 * n_arrays * br * cols * 4 > (16 << 20):
        br //= 2
    return br


def _add2(a, b, out_dtype, name):
    rows, cols = a.shape
    br = _row_block(rows, cols, 3)

    def body(a_ref, b_ref, o_ref):
        o_ref[...] = (a_ref[...] + b_ref[...]).astype(out_dtype)

    spec = pl.BlockSpec((br, cols), lambda i: (i, 0))
    return pl.pallas_call(body, name=name, grid=(rows // br,), in_specs=[spec, spec], out_specs=spec,
                          out_shape=pltpu.HBM(a.shape, out_dtype),
                          compiler_params=_cparams(3 * br * cols * 4, 1))(*_in_hbm([a, b]))


def _sum_slots(a, name):
    n, rows, cols = a.shape
    br = _row_block(rows, cols, n + 1)

    def body(a_ref, o_ref):
        acc = a_ref[0].astype(F32)
        for s in range(1, n):
            acc = acc + a_ref[s].astype(F32)
        o_ref[...] = acc

    return pl.pallas_call(body, name=name, grid=(rows // br,),
                          in_specs=[pl.BlockSpec((n, br, cols), lambda i: (0, i, 0))],
                          out_specs=pl.BlockSpec((br, cols), lambda i: (i, 0)),
                          out_shape=pltpu.HBM((rows, cols), F32),
                          compiler_params=_cparams((n + 1) * br * cols * 4, 1))(*_in_hbm([a]))


def _adamw_math(w, g, m, v):
    m = ADAM_B1 * m + (1.0 - ADAM_B1) * g
    v = ADAM_B2 * v + (1.0 - ADAM_B2) * (g * g)
    m_hat = m / (1.0 - ADAM_B1 ** ADAM_STEP)
    v_hat = v / (1.0 - ADAM_B2 ** ADAM_STEP)
    delta = -ADAM_LR * (m_hat / (jnp.sqrt(v_hat) + ADAM_EPS) + ADAM_WD * w)
    return delta, m, v


def _adamw(g, w, m, v, name):
    rows, cols = g.shape
    br = _row_block(rows, cols, 7)

    def body(g_ref, w_ref, m_ref, v_ref, d_ref, nm_ref, nv_ref):
        d_ref[...], nm_ref[...], nv_ref[...] = _adamw_math(w_ref[...], g_ref[...], m_ref[...], v_ref[...])

    spec = pl.BlockSpec((br, cols), lambda i: (i, 0))
    return pl.pallas_call(body, name=name, grid=(rows // br,), in_specs=[spec] * 4, out_specs=[spec] * 3,
                          out_shape=[jax.ShapeDtypeStruct(g.shape, F32)] * 3,
                          compiler_params=_cparams(7 * br * cols * 4, 1))(g, w, m, v)


ROW_FINAL_G, ROW_PE_G, ROW_LOSS = 0, 1, 2
ROW_CONV_B, ROW_CN_G, ROW_CN_B, ROW_B_PW2 = 8, 9, 10, 11
ROW_LN_G = 16
ROW_ONORM_G, ROW_LB = 24, 25
ROW_CONV_W = 32
SMALL = ["ln_g", "conv_b", "cnorm_g", "cnorm_b", "b_pw2", "onorm_g", "pe_norm_g", "final_g"]
SMALL_ROW = dict(ln_g=ROW_LN_G, conv_b=ROW_CONV_B, cnorm_g=ROW_CN_G, cnorm_b=ROW_CN_B, b_pw2=ROW_B_PW2,
                 onorm_g=ROW_ONORM_G, pe_norm_g=ROW_PE_G, final_g=ROW_FINAL_G)


def _adamw_small(vsum, gcw, lb_logits, params):
    names = SMALL + ["lb_logits", "conv_w"]
    flat = [t for nm in names for t in params[nm]]

    def body(*refs):
        vs_ref, gcw_ref, lbl_ref = refs[:3]
        ins = refs[3:3 + 3 * len(names)]
        outs = refs[3 + 3 * len(names):]
        for q, nm in enumerate(names):
            w_ref, m_ref, v_ref = ins[3 * q:3 * q + 3]
            g_ref, d_ref, nm_ref, nv_ref = outs[4 * q:4 * q + 4]
            if nm == "conv_w":
                g = gcw_ref[...]
            elif nm == "lb_logits":
                lb = _softmax_row0(lbl_ref[...])
                g0 = vs_ref[ROW_LB:ROW_LB + 1, :] * lb * (1.0 - lb)
                g = jnp.concatenate([g0, -g0], axis=0)
            else:
                g = vs_ref[SMALL_ROW[nm]:SMALL_ROW[nm] + 1, :]
            g_ref[...] = g
            d_ref[...], nm_ref[...], nv_ref[...] = _adamw_math(w_ref[...], g, m_ref[...], v_ref[...])

    out_shape = [jax.ShapeDtypeStruct(params[nm][0].shape, F32) for nm in names for _ in range(4)]
    outs = pl.pallas_call(body, name="adamw_small", out_shape=out_shape)(vsum, gcw, lb_logits, *flat)
    return {nm: tuple(outs[4 * q:4 * q + 4]) for q, nm in enumerate(names)}


TOKEN_TILE = dict(inproj_fwd=1024, conv=256, hgrn=512, tail=256, inproj_bwd_x=1024, weight_grad=1024)


def _tile(T, family):
    return min(T, TOKEN_TILE[family])


def kernel(x, p, ln_g, w_in, conv_w, conv_b, cnorm_g, cnorm_b, w_pw2, b_pw2, lb_logits, onorm_g, w_out, pe_norm_g, w_pg, w_pp, final_g, loss_target, m_ln_g, m_w_in, m_conv_w, m_conv_b, m_cnorm_g, m_cnorm_b, m_w_pw2, m_b_pw2, m_lb_logits, m_onorm_g, m_w_out, m_pe_norm_g, m_w_pg, m_w_pp, m_final_g, v_ln_g, v_w_in, v_conv_w, v_conv_b, v_cnorm_g, v_cnorm_b, v_w_pw2, v_b_pw2, v_lb_logits, v_onorm_g, v_w_out, v_pe_norm_g, v_w_pg, v_w_pp, v_final_g):
    given = dict(locals())
    x2, p2, tgt = x[0], p[0, 0], loss_target[0]
    T = x2.shape[0]
    fin_g = final_g.reshape(1, D)

    conv_w_pad = jnp.pad(conv_w[0], ((0, HALO - CONV_K), (0, 0)))
    (w_in_f,) = _all_gather_shards([w_in[0].astype(_MXU)], [1])

    z, u, (w_pw2_f, w_out_f, w_pg_f, w_pp_f, conv_w_f) = _inproj_fwd(
        x2, ln_g, w_in_f,
        [w_pw2[0].astype(_MXU), w_out[0].astype(_MXU), w_pg[0].astype(_MXU), w_pp[0].astype(_MXU), conv_w_pad],
        [0, 0, 0, 1, 1], _tile(T, "inproj_fwd"))
    yc, y1 = _conv_fwd(z, conv_w_f, conv_b, cnorm_g, cnorm_b, w_pw2_f, b_pw2, _tile(T, "conv"))
    o_raw, yh, s_chunks = _hgrn_fwd(z, lb_logits, onorm_g, _tile(T, "hgrn"), HB)
    dyc, dyh, dh, n2, ds, dpe, dhb, pb, vec_tail = _tail(
        x2, yc, yh, p2, tgt, w_out_f, w_pg_f, w_pp_f, pe_norm_g, fin_g, _tile(T, "tail"))
    tk = _tile(T, "weight_grad")
    g_w_out_c = _tn_matmul(yc, dhb, tk, "grad_w_out_conv")
    g_w_out_h = _tn_matmul(yh, dhb, tk, "grad_w_out_hgrn")
    g_w_pg = _tn_matmul(n2, ds, tk, "grad_w_pg")
    g_w_pp = _tn_matmul(pb, dpe, tk, "grad_w_pp")
    dzc, a_act, dy2, vec_conv, g_conv_w = _conv_bwd(z, y1, dyc, conv_w_f, cnorm_g, cnorm_b, w_pw2_f, b_pw2, _tile(T, "conv"))
    g_w_pw2 = _tn_matmul(a_act, dy2, tk, "grad_w_pw2")

    def pair_sums(names, slabs, tag):
        mine, got = _pair_exchange(slabs, "grad_pair_exchange_" + tag)
        return [_add2(a.reshape(-1, a.shape[-1]), b.reshape(-1, b.shape[-1]), _WIRE, "pair_sum_" + nm).reshape(a.shape)
                for a, b, nm in zip(mine, got, names)]

    rest = ["w_pw2", "w_out", "w_pg", "w_pp"]
    partial_rest = pair_sums(rest, [
        _Slab([g_w_pw2], lambda s: (0, s * (D // N_CHIPS), 0), (D // N_CHIPS, D)),
        _Slab([g_w_out_c, g_w_out_h], lambda s: (s // 2, (s % 2) * (D // 2), 0), (D // 2, D)),
        _Slab([g_w_pg], lambda s: (0, s * (D // N_CHIPS), 0), (D // N_CHIPS, D)),
        _Slab([g_w_pp], lambda s: (0, 0, s * (D // N_CHIPS)), (PLE, D // N_CHIPS)),
    ], "rest")
    dzh, vec_hgrn, slots_rest = _hgrn_bwd(z, lb_logits, onorm_g, o_raw, dyh, s_chunks, partial_rest, _tile(T, "hgrn"), HB)
    g_w_in = _inproj_bwd_w(u, dzc, dzh, tk)
    partial_in = pair_sums(["w_in"], [
        _Slab([g_w_in], lambda s: (0, 0, s * (NPART * D // N_CHIPS)), (D, NPART * D // N_CHIPS))], "w_in")
    grad_x, vec_in, slots_in = _inproj_bwd_x(x2, ln_g, dzc, dzh, w_in_f, dh, partial_in, _tile(T, "inproj_bwd_x"))
    big = ["w_in"] + rest
    halves = [_sum_slots(s, "chip_sum_" + nm) for s, nm in zip(list(slots_in) + list(slots_rest), big)]
    vec = jnp.concatenate([vec_tail, vec_conv, vec_in, vec_hgrn, g_conv_w], axis=0)
    vsum = _sum_slots(_vec_exchange(vec), "vec_sum")
    grads_big = _pair_share(halves)

    out = {}
    for nm, g in zip(big, grads_big):
        w2, m2, v2 = given[nm][0], given["m_" + nm][0], given["v_" + nm][0]
        d, nm_, nv_ = _adamw(g, w2, m2, v2, "adamw_" + nm)
        out[nm] = tuple(t[None] for t in (g, d, nm_, nv_))
    chip = 2 * lax.axis_index("x") + lax.axis_index("y")
    gcw = lax.dynamic_slice(vsum, (ROW_CONV_W, chip * (D // N_CHIPS)), (CONV_K, D // N_CHIPS))
    params = {nm: (given[nm].reshape(-1, D), given["m_" + nm].reshape(-1, D), given["v_" + nm].reshape(-1, D))
              for nm in SMALL + ["lb_logits"]}
    params["conv_w"] = (conv_w[0], m_conv_w[0], v_conv_w[0])
    small = _adamw_small(vsum, gcw, lb_logits, params)
    for nm, ts in small.items():
        out[nm] = tuple(t.reshape(given[nm].shape) for t in ts)

    loss = vsum[ROW_LOSS, 0]
    order = ["ln_g", "w_in", "conv_w", "conv_b", "cnorm_g", "cnorm_b", "w_pw2", "b_pw2", "lb_logits", "onorm_g",
             "w_out", "pe_norm_g", "w_pg", "w_pp", "final_g"]
    return (loss, grad_x[None], *[out[nm][0] for nm in order], *[out[nm][1] for nm in order],
            *[out[nm][2] for nm in order], *[out[nm][3] for nm in order])
```

```python
import functools

import jax
import jax.numpy as jnp
from jax import lax
from jax.experimental import pallas as pl
from jax.experimental.pallas import tpu as pltpu

F32 = jnp.float32
BF16 = jnp.bfloat16
_MXU = jnp.bfloat16
_WIRE = jnp.bfloat16

D = 1024
NPART = 7
PLE = 256
HEADS = 8
HD = 128
CHUNK = 64
CONV_K = 31
HALO = 32
EPS = 1e-6
N_CHIPS = 4
N_DEV = 8
HB = 8
VEC_ROWS = 64

ADAM_LR = 0.001
ADAM_B1 = 0.9
ADAM_B2 = 0.999
ADAM_EPS = 1e-08
ADAM_WD = 0.01
ADAM_STEP = 10

V7X_VMEM_LIMIT = 60000 * 1024
MESH_ID = pl.DeviceIdType.MESH
ANY = pl.BlockSpec(memory_space=pltpu.HBM)


def _in_hbm(arrays):
    return [pltpu.with_memory_space_constraint(a, pltpu.HBM) for a in arrays]


def _cparams(block_bytes, n_grid_dims):
    limit = min(V7X_VMEM_LIMIT, 2 * block_bytes + (24 << 20))
    return pltpu.CompilerParams(vmem_limit_bytes=int(limit), dimension_semantics=("arbitrary",) * n_grid_dims)


def _nbytes(shape, dtype):
    n = 1
    for s in shape:
        n *= s
    return n * jnp.dtype(dtype).itemsize


def _dot(a, b):
    return jnp.dot(a.astype(_MXU), b.astype(_MXU), preferred_element_type=F32)


def _dot_nt(a, b):
    return lax.dot_general(a.astype(_MXU), b.astype(_MXU), (((1,), (1,)), ((), ())), preferred_element_type=F32)


def _dot_tn(a, b):
    return lax.dot_general(a.astype(_MXU), b.astype(_MXU), (((0,), (0,)), ((), ())), preferred_element_type=F32)


def _tri_dot(tri_bf, x):
    x1 = x.astype(BF16)
    r1 = x - x1.astype(F32)
    x2 = r1.astype(BF16)
    x3 = (r1 - x2.astype(F32)).astype(BF16)
    d = lambda t: jnp.dot(tri_bf, t, preferred_element_type=F32)
    return d(x1) + d(x2) + d(x3)


def _split2(x):
    hi = x.astype(BF16)
    return hi, (x - hi.astype(F32)).astype(BF16)


def _dot3(dims, a, b):
    d = lambda p, q: lax.dot_general(p, q, (dims, ((), ())), preferred_element_type=F32)
    return d(a[0], b[0]) + d(a[0], b[1]) + d(a[1], b[0])


def _sigmoid(x):
    return jax.nn.sigmoid(x)


def _mean_lanes(x):
    return jnp.mean(x, axis=-1, keepdims=True)


def _sum_rows(x):
    return jnp.sum(x, axis=0, keepdims=True)


def _group_ln(y):
    yn, rs = [], []
    for g in range(D // HD):
        blk = y[:, g * HD:(g + 1) * HD]
        xc = blk - _mean_lanes(blk)
        r = lax.rsqrt(_mean_lanes(xc * xc) + EPS)
        yn.append(xc * r)
        rs.append(jnp.broadcast_to(r, blk.shape))
    return jnp.concatenate(yn, axis=1), jnp.concatenate(rs, axis=1)


def _group_ln_bwd(dyn, yn, rstd):
    out = []
    for g in range(D // HD):
        sl = slice(g * HD, (g + 1) * HD)
        d, n = dyn[:, sl], yn[:, sl]
        out.append(rstd[:, sl] * (d - _mean_lanes(d) - n * _mean_lanes(d * n)))
    return jnp.concatenate(out, axis=1)


def _head_means(x, hb, fn=lambda m: m):
    return jnp.concatenate([jnp.broadcast_to(fn(_mean_lanes(x[:, hh * HD:(hh + 1) * HD])), (x.shape[0], HD))
                            for hh in range(hb)], axis=1)


def _head_rsqrt_mean(x, hb):
    return _head_means(x, hb, lambda m: lax.rsqrt(m + EPS))


def _softmax_row0(lbl):
    m = jnp.max(lbl, axis=0, keepdims=True)
    e = jnp.exp(lbl - m)
    return e[0:1, :] / jnp.sum(e, axis=0, keepdims=True)


def _hosted_gather(phases, step, at, shards, axes, ins, outs, bufs, sems):
    gather = _Gather([s.shape for s in shards], axes, ins, outs, bufs, sems)
    for phase in phases:
        pl.when(step == at[phase])(getattr(gather, phase))


def _rmsnorm_gather(x, ln_g, shards, axes, tT):
    T = x.shape[0]
    n = len(shards)
    at = _hosted_steps(T // tT)

    def body(x_ref, g_ref, *rest):
        ins, u_ref, outs, bufs, sems = rest[:n], rest[n], rest[n + 1:2 * n + 1], rest[2 * n + 1:3 * n + 1], rest[3 * n + 1:]
        host = functools.partial(_hosted_gather, step=pl.program_id(0), at=at, shards=shards, axes=axes,
                                 ins=ins, outs=outs, bufs=bufs, sems=sems)
        host(("start", "turn", "forward"))
        xv = x_ref[...]
        r = lax.rsqrt(_mean_lanes(xv * xv) + EPS)
        u_ref[...] = (xv * r * g_ref[...]).astype(_MXU)
        host(("finish",))

    blk = _nbytes((tT, D), F32) * 2 + _nbytes((tT, D), _MXU) + sum(_nbytes(s.shape, s.dtype) for s in shards)
    outs = pl.pallas_call(
        body, name="rmsnorm_gather", grid=(T // tT,),
        in_specs=[pl.BlockSpec((tT, D), lambda i: (i, 0)), pl.BlockSpec((1, D), lambda i: (0, 0))] + [ANY] * n,
        out_specs=[pl.BlockSpec((tT, D), lambda i: (i, 0))] + [ANY] * n,
        out_shape=[jax.ShapeDtypeStruct((T, D), _MXU)]
        + [pltpu.HBM(fs, s.dtype) for fs, s in zip(_full_shapes(shards, axes), shards)],
        scratch_shapes=_Gather.scratch(shards),
        compiler_params=_cparams(blk, 1),
    )(x, ln_g, *_in_hbm(shards))
    return outs[0], outs[1:]


def _inproj_fwd(u, w_in, shards, axes, tT):
    T = u.shape[0]
    n = len(shards)
    at = _hosted_steps((T // tT) * NPART)

    def body(u_ref, w_ref, *rest):
        ins, z_ref, outs, bufs, sems = rest[:n], rest[n], rest[n + 1:2 * n + 1], rest[2 * n + 1:3 * n + 1], rest[3 * n + 1:]
        host = functools.partial(_hosted_gather, step=pl.program_id(0) * NPART + pl.program_id(1), at=at, shards=shards,
                                 axes=axes, ins=ins, outs=outs, bufs=bufs, sems=sems)
        host(("start", "turn", "forward"))
        z_ref[...] = jnp.dot(u_ref[...], w_ref[...], preferred_element_type=F32)
        host(("finish",))

    blk = (_nbytes((tT, D), F32) + _nbytes((D, D), _MXU) + _nbytes((tT, D), _MXU)
           + sum(_nbytes(s.shape, s.dtype) for s in shards))
    outs = pl.pallas_call(
        body, name="inproj_fwd", grid=(T // tT, NPART),
        in_specs=[pl.BlockSpec((tT, D), lambda i, j: (i, 0)), pl.BlockSpec((D, D), lambda i, j: (0, j))] + [ANY] * n,
        out_specs=[pl.BlockSpec((tT, D), lambda i, j: (i, j))] + [ANY] * n,
        out_shape=[jax.ShapeDtypeStruct((T, NPART * D), F32)]
        + [pltpu.HBM(fs, s.dtype) for fs, s in zip(_full_shapes(shards, axes), shards)],
        scratch_shapes=_Gather.scratch(shards),
        compiler_params=_cparams(blk, 2),
    )(u, w_in, *_in_hbm(shards))
    return outs[0], outs[1:]


def _shifted_windows(ext, first, visit):
    n = ext.shape[0]
    for m in range(first, first + CONV_K):
        visit(m, (ext if m == 0 else pltpu.roll(ext, n - m, axis=0))[0:n - HALO, :])


def _conv_fwd(z, conv_w, conv_b, cn_g, cn_b, w_pw2, b_pw2, tT):
    T = z.shape[0]

    def body(cv_ref, cg_ref, ct_ref, cw_ref, cb_ref, ng_ref, nb_ref, wp_ref, bp_ref, yc_ref, y1_ref, ext):
        @pl.when(pl.program_id(0) == 0)
        def _():
            ext[...] = jnp.zeros_like(ext)
        ext[0:HALO, :] = ext[tT:tT + HALO, :]
        ext[HALO:, :] = cv_ref[...] * _sigmoid(cg_ref[...])
        cw = cw_ref[...]
        acc = [cb_ref[...]]

        def tap(m, win):
            acc[0] = acc[0] + win * cw[m - 2:m - 1, :]
        _shifted_windows(ext[...], 2, tap)
        y1 = acc[0]
        y1_ref[...] = y1
        yn, _ = _group_ln(y1)
        apre = yn * ng_ref[...] + nb_ref[...]
        a = apre * _sigmoid(apre)
        y2 = _dot(a, wp_ref[...]) + bp_ref[...]
        ct = ct_ref[...]
        yc_ref[...] = (y2 * (ct * _sigmoid(ct))).astype(_MXU)

    part = lambda p: pl.BlockSpec((tT, D), lambda i: (i, p))
    row = pl.BlockSpec((1, D), lambda i: (0, 0))
    tok = pl.BlockSpec((tT, D), lambda i: (i, 0))
    blk = 4 * _nbytes((tT, D), F32) + _nbytes((D, D), _MXU) + _nbytes((tT, D), _MXU) + 8 * _nbytes((tT + HALO, D), F32)
    return pl.pallas_call(
        body, name="conv_fwd", grid=(T // tT,),
        in_specs=[part(0), part(1), part(2), pl.BlockSpec((HALO, D), lambda i: (0, 0)), row, row, row,
                  pl.BlockSpec((D, D), lambda i: (0, 0)), row],
        out_specs=[tok, tok],
        out_shape=[jax.ShapeDtypeStruct((T, D), _MXU), jax.ShapeDtypeStruct((T, D), F32)],
        scratch_shapes=[pltpu.VMEM((tT + HALO, D), F32)],
        compiler_params=_cparams(blk, 1),
    )(z, z, z, conv_w, conv_b, cn_g, cn_b, w_pw2, b_pw2)


def _hgrn_gates(lb, hq, hf):
    sq = _sigmoid(hq)
    sg = _sigmoid(hf)
    f = lb + (1.0 - lb) * sg
    return sq, sg, f, hq * sq, (1.0 - lb) * (1.0 - sg), jnp.log(f)


def _chunk_decays(lf, q, k):
    r = lax.broadcasted_iota(jnp.int32, (CHUNK, CHUNK), 0)
    c = lax.broadcasted_iota(jnp.int32, (CHUNK, CHUNK), 1)
    b = _tri_dot((r >= c).astype(BF16), lf)
    bm = b[CHUNK // 2 - 1:CHUNK // 2, :]
    bl = b[CHUNK - 1:CHUNK, :]
    eb = jnp.exp(b)
    eqm = jnp.exp(b - bm)
    ekm = jnp.exp(bm - b)
    ekd = jnp.exp(bl - b)
    return dict(causal=r >= c, eb=eb, eqm=eqm, ekm=ekm, ekd=ekd, ebl=jnp.exp(bl),
                qd=q * eb, qm=q * eqm, km=k * ekm, kd=k * ekd)


def _hgrn_fwd(z, lb_logits, onorm_g, tT, hb):
    T = z.shape[0]
    nc = tT // CHUNK
    w = hb * HD

    def body(lbl_ref, og_ref, hq_ref, hf_ref, hi_ref, hg_ref, o_ref, yh_ref, sc_ref, st):
        @pl.when(pl.program_id(1) == 0)
        def _():
            st[...] = jnp.zeros_like(st)
        lb_all = _softmax_row0(lbl_ref[...])
        og_all = og_ref[...]

        def chunk(c, carry):
            sl = pl.ds(pl.multiple_of(c * CHUNK, CHUNK), CHUNK)
            lanes = [slice(hh * HD, (hh + 1) * HD) for hh in range(hb)]
            heads = lambda fn: [fn(hh, ln) for hh, ln in enumerate(lanes)]
            hg, v = hg_ref[sl, :], hi_ref[sl, :]
            _, _, _, q, k, lf = _hgrn_gates(lb_all, hq_ref[sl, :], hf_ref[sl, :])
            dc = _chunk_decays(lf, q, k)
            s_t = heads(lambda hh, ln: st[hh])
            a = heads(lambda hh, ln: jnp.where(dc["causal"], _dot_nt(dc["qm"][:, ln], dc["km"][:, ln]), 0.0))
            o_inter = heads(lambda hh, ln: _dot_nt(dc["qd"][:, ln], s_t[hh]))
            kv = heads(lambda hh, ln: _dot_tn(v[:, ln], dc["kd"][:, ln]))
            o_intra = heads(lambda hh, ln: _dot(a[hh], v[:, ln]))
            for hh, ln in enumerate(lanes):
                sc_ref[hh, c] = s_t[hh]
                st[hh] = s_t[hh] * dc["ebl"][:, ln] + kv[hh]
            o = jnp.concatenate([o_inter[hh] + o_intra[hh] for hh in range(hb)], axis=1)
            o_ref[sl, :] = o
            n = o * _head_rsqrt_mean(o * o, hb)
            yh_ref[sl, :] = ((n * og_all) * (hg * _sigmoid(hg))).astype(_MXU)
            return carry

        lax.fori_loop(0, nc, chunk, 0, unroll=4)

    zpart = lambda p: pl.BlockSpec((tT, w), lambda h, i: (i, p * (HEADS // hb) + h))
    blk = 6 * _nbytes((tT, w), F32) + _nbytes((hb, nc, HD, HD), F32)
    return pl.pallas_call(
        body, name="hgrn_fwd", grid=(HEADS // hb, T // tT),
        in_specs=[pl.BlockSpec((2, w), lambda h, i: (0, h)), pl.BlockSpec((1, w), lambda h, i: (0, h)),
                  zpart(3), zpart(4), zpart(5), zpart(6)],
        out_specs=[pl.BlockSpec((tT, w), lambda h, i: (i, h)), pl.BlockSpec((tT, w), lambda h, i: (i, h)),
                   pl.BlockSpec((hb, nc, HD, HD), lambda h, i: (h, i, 0, 0))],
        out_shape=[jax.ShapeDtypeStruct((T, D), F32), jax.ShapeDtypeStruct((T, D), _MXU),
                   jax.ShapeDtypeStruct((HEADS, T // CHUNK, HD, HD), F32)],
        scratch_shapes=[pltpu.VMEM((hb, HD, HD), F32)],
        compiler_params=_cparams(blk, 2),
    )(lb_logits, onorm_g, z, z, z, z)


def _hgrn_bwd(z, lb_logits, onorm_g, o_raw, dyh, s_chunks, partials, tT, hb):
    T = z.shape[0]
    nc = tT // CHUNK
    nI = T // tT
    w = hb * HD
    n = len(partials)
    at = _hosted_steps((HEADS // hb) * nI)

    def body(lbl_ref, og_ref, hq_ref, hf_ref, hi_ref, hg_ref, o_ref, dy_ref, sc_ref, *rest):
        (dz_ref, vec_ref), dst = rest[n:n + 2], rest[2 * n + 2]
        exchange = _ChipExchange(n, rest[:n], rest[n + 2:2 * n + 2], rest[2 * n + 3:3 * n + 3], rest[3 * n + 3:])
        step = pl.program_id(0) * nI + pl.program_id(1)
        pl.when(step == at["start"])(exchange.start)
        pl.when(step == at["turn"])(exchange.turn)

        @pl.when(pl.program_id(1) == 0)
        def _():
            dst[...] = jnp.zeros_like(dst)
            vec_ref[...] = jnp.zeros_like(vec_ref)
        lb_all = _softmax_row0(lbl_ref[...])
        og_all = og_ref[...]
        last_row = lax.broadcasted_iota(jnp.int32, (CHUNK, w), 0) == CHUNK - 1
        r64 = lax.broadcasted_iota(jnp.int32, (CHUNK, CHUNK), 0)
        c64 = lax.broadcasted_iota(jnp.int32, (CHUNK, CHUNK), 1)
        upper = (c64 >= r64).astype(BF16)
        lanes = [slice(hh * HD, (hh + 1) * HD) for hh in range(hb)]
        heads = lambda fn: [fn(hh, ln) for hh, ln in enumerate(lanes)]
        wide = lambda parts: jnp.concatenate(parts, axis=1)

        def chunk(cc, carry):
            c = nc - 1 - cc
            sl = pl.ds(pl.multiple_of(c * CHUNK, CHUNK), CHUNK)
            hq, hg, v = hq_ref[sl, :], hg_ref[sl, :], hi_ref[sl, :]
            sq, sg, f, q, k, lf = _hgrn_gates(lb_all, hq, hf_ref[sl, :])
            dc = _chunk_decays(lf, q, k)
            s_t = heads(lambda hh, ln: sc_ref[hh, c])
            ds_t = heads(lambda hh, ln: dst[hh])
            o, dy = o_ref[sl, :], dy_ref[sl, :]
            r = _head_rsqrt_mean(o * o, hb)
            n = o * r
            sgg = _sigmoid(hg)
            silu_g = hg * sgg
            dhg = dy * (n * og_all) * (sgg * (1.0 + hg * (1.0 - sgg)))
            dn = dy * og_all * silu_g
            g_og = _sum_rows(dy * n * silu_g)
            do = r * (dn - n * _head_means(dn * n, hb))
            a = heads(lambda hh, ln: jnp.where(dc["causal"], _dot_nt(dc["qm"][:, ln], dc["km"][:, ln]), 0.0))
            dam = heads(lambda hh, ln: jnp.where(dc["causal"], _dot_nt(do[:, ln], v[:, ln]), 0.0))
            dqd = wide(heads(lambda hh, ln: _dot(do[:, ln], s_t[hh])))
            dkd = wide(heads(lambda hh, ln: _dot(v[:, ln], ds_t[hh])))
            dv_inter = heads(lambda hh, ln: _dot_nt(dc["kd"][:, ln], ds_t[hh]))
            dqs = heads(lambda hh, ln: _dot_tn(do[:, ln], dc["qd"][:, ln]))
            dv = wide(heads(lambda hh, ln: _dot_tn(a[hh], do[:, ln]) + dv_inter[hh]))
            dam2 = [_split2(t) for t in dam]
            km2, qm2 = _split2(dc["km"]), _split2(dc["qm"])
            dqm = wide(heads(lambda hh, ln: _dot3(((1,), (0,)), dam2[hh], (km2[0][:, ln], km2[1][:, ln]))))
            dkm = wide(heads(lambda hh, ln: _dot3(((0,), (0,)), dam2[hh], (qm2[0][:, ln], qm2[1][:, ln]))))
            debl = wide(heads(lambda hh, ln: _sum_rows(ds_t[hh] * s_t[hh])))
            for hh, ln in enumerate(lanes):
                dst[hh] = ds_t[hh] * dc["ebl"][:, ln] + dqs[hh]
            dq = dqd * dc["eb"] + dqm * dc["eqm"]
            dk = dkm * dc["ekm"] + dkd * dc["ekd"]
            dbl = _sum_rows(dkd * dc["kd"]) + debl * dc["ebl"]
            db = dq * q - dk * k + jnp.where(last_row, dbl, 0.0)
            dlf = _tri_dot(upper, db)
            dfk = dlf / f - dk
            dz_ref[0, sl, :] = (dq * (sq * (1.0 + hq * (1.0 - sq)))).astype(_MXU)
            dz_ref[1, sl, :] = (dfk * ((1.0 - lb_all) * sg * (1.0 - sg))).astype(_MXU)
            dz_ref[2, sl, :] = dv.astype(_MXU)
            dz_ref[3, sl, :] = dhg.astype(_MXU)
            vec_ref[0:1, :] += g_og
            vec_ref[1:2, :] += _sum_rows(dfk * (1.0 - sg))
            return carry

        lax.fori_loop(0, nc, chunk, 0, unroll=4)
        pl.when(step == at["finish"])(exchange.finish)

    zpart = lambda p: pl.BlockSpec((tT, w), lambda h, i: (nI - 1 - i, p * (HEADS // hb) + h))
    act = pl.BlockSpec((tT, w), lambda h, i: (nI - 1 - i, h))
    blk = (6 * _nbytes((tT, w), F32) + _nbytes((hb, nc, HD, HD), F32) + 4 * _nbytes((tT, w), _MXU)
           + _ChipExchange.scratch_bytes(partials))
    outs = pl.pallas_call(
        body, name="hgrn_bwd", grid=(HEADS // hb, nI),
        in_specs=[pl.BlockSpec((2, w), lambda h, i: (0, h)), pl.BlockSpec((1, w), lambda h, i: (0, h)),
                  zpart(3), zpart(4), zpart(5), zpart(6), act, act,
                  pl.BlockSpec((hb, nc, HD, HD), lambda h, i: (h, nI - 1 - i, 0, 0))] + [ANY] * n,
        out_specs=[pl.BlockSpec((4, tT, w), lambda h, i: (0, nI - 1 - i, h)),
                   pl.BlockSpec((8, w), lambda h, i: (0, h))] + [ANY] * n,
        out_shape=[jax.ShapeDtypeStruct((4, T, D), _MXU), jax.ShapeDtypeStruct((8, D), F32)]
        + [pltpu.HBM(p.shape, p.dtype) for p in partials],
        scratch_shapes=[pltpu.VMEM((hb, HD, HD), F32)] + _ChipExchange.scratch(partials),
        compiler_params=_cparams(blk, 2),
    )(lb_logits, onorm_g, z, z, z, z, o_raw, dyh, s_chunks, *_in_hbm(partials))
    return outs[0], outs[1], outs[2:]


def _tail(x, yc, yh, p, target, w_out, w_pg, w_pp, pe_g, fin_g, tT):
    T = x.shape[0]

    def body(x_ref, yc_ref, yh_ref, p_ref, t_ref, wo_ref, wg_ref, wp_ref, pg_ref, fg_ref,
             dyc_ref, dyh_ref, dh_ref, n2_ref, ds_ref, dpe_ref, dhb_ref, pb_ref, vec_ref):
        @pl.when(pl.program_id(0) == 0)
        def _():
            vec_ref[...] = jnp.zeros_like(vec_ref)
        wo_c, wo_h = wo_ref[0:D, :], wo_ref[D:2 * D, :]
        h = x_ref[...] + _dot(yc_ref[...], wo_c) + _dot(yh_ref[...], wo_h)
        pb = p_ref[...].astype(_MXU)
        pe = _dot(pb, wp_ref[...])
        r2 = lax.rsqrt(_mean_lanes(h * h) + EPS)
        hn = h * r2
        n2 = (hn * pg_ref[...]).astype(_MXU)
        gate = _sigmoid(_dot(n2, wg_ref[...]))
        h2 = h + gate * pe
        r3 = lax.rsqrt(_mean_lanes(h2 * h2) + EPS)
        h2n = h2 * r3
        err = h2n * fg_ref[...] - t_ref[...]
        vec_ref[ROW_LOSS:ROW_LOSS + 1, :] += 0.5 * jnp.sum(_mean_lanes(err * err))
        dout = err * (1.0 / D)
        vec_ref[0:1, :] += _sum_rows(dout * h2n)
        dn3 = dout * fg_ref[...]
        dh2 = r3 * (dn3 - h2n * _mean_lanes(dn3 * h2n))
        ds = (dh2 * pe * gate * (1.0 - gate)).astype(_MXU)
        dn2 = _dot_nt(ds, wg_ref[...])
        vec_ref[1:2, :] += _sum_rows(dn2 * hn)
        dnn = dn2 * pg_ref[...]
        dh = dh2 + r2 * (dnn - hn * _mean_lanes(dnn * hn))
        dhb = dh.astype(_MXU)
        dyc_ref[...] = _dot_nt(dhb, wo_c)
        dyh_ref[...] = _dot_nt(dhb, wo_h)
        dh_ref[...] = dh
        n2_ref[...] = n2
        ds_ref[...] = ds
        dpe_ref[...] = (dh2 * gate).astype(_MXU)
        dhb_ref[...] = dhb
        pb_ref[...] = pb

    tok = lambda w: pl.BlockSpec((tT, w), lambda i: (i, 0))
    full = lambda r, c: pl.BlockSpec((r, c), lambda i: (0, 0))
    tokshape = lambda w, dt: jax.ShapeDtypeStruct((T, w), dt)
    blk = (5 * _nbytes((tT, D), F32) + 7 * _nbytes((tT, D), _MXU) + _nbytes((4 * D + PLE, D), _MXU)
           + 12 * _nbytes((tT, D), F32))
    return pl.pallas_call(
        body, name="tail_fwd_bwd", grid=(T // tT,),
        in_specs=[tok(D), tok(D), tok(D), tok(PLE), tok(D), full(2 * D, D), full(D, D), full(PLE, D), full(1, D), full(1, D)],
        out_specs=[tok(D), tok(D), tok(D), tok(D), tok(D), tok(D), tok(D), tok(PLE), full(8, D)],
        out_shape=[tokshape(D, F32), tokshape(D, F32), tokshape(D, F32), tokshape(D, _MXU), tokshape(D, _MXU),
                   tokshape(D, _MXU), tokshape(D, _MXU), tokshape(PLE, _MXU),
                   jax.ShapeDtypeStruct((8, D), F32)],
        compiler_params=_cparams(blk, 1),
    )(x, yc, yh, p, target, w_out, w_pg, w_pp, pe_g, fin_g)


def _conv_bwd(z, y1, dyc, conv_w, cn_g, cn_b, w_pw2, b_pw2, tT):
    T = z.shape[0]
    nI = T // tT
    hb = tT // HALO

    def body(cv_ref, cg_ref, ct_ref, hv_ref, hg_ref, y1_ref, dyc_ref, cw_ref, ng_ref, nb_ref, wp_ref, bp_ref,
             dz_ref, a_ref, dy2_ref, vec_ref, gcw_ref, ext, ext2, gpart):
        i = pl.program_id(0)

        @pl.when(i == 0)
        def _():
            ext2[...] = jnp.zeros_like(ext2)
            gpart[...] = jnp.zeros_like(gpart)
            vec_ref[...] = jnp.zeros_like(vec_ref)
        cv, cg, ct = cv_ref[...], cg_ref[...], ct_ref[...]
        sg = _sigmoid(cg)
        has_hist = (i < nI - 1).astype(F32)
        ext[0:HALO, :] = hv_ref[...] * _sigmoid(hg_ref[...]) * has_hist
        ext[HALO:, :] = cv * sg
        yn, rstd = _group_ln(y1_ref[...])
        apre = yn * ng_ref[...] + nb_ref[...]
        sa = _sigmoid(apre)
        a = (apre * sa).astype(_MXU)
        y2 = _dot(a, wp_ref[...]) + bp_ref[...]
        st = _sigmoid(ct)
        dyc_v = dyc_ref[...]
        dy2 = dyc_v * (ct * st)
        dy2b = dy2.astype(_MXU)
        da = _dot_nt(dy2b, wp_ref[...])
        dapre = da * (sa * (1.0 + apre * (1.0 - sa)))
        dy1 = _group_ln_bwd(dapre * ng_ref[...], yn, rstd)
        vec_ref[0:1, :] += _sum_rows(dy1)
        vec_ref[1:2, :] += _sum_rows(dapre * yn)
        vec_ref[2:3, :] += _sum_rows(dapre)
        vec_ref[3:4, :] += _sum_rows(dy2)
        dz_ref[2] = (dyc_v * y2 * (st * (1.0 + ct * (1.0 - st)))).astype(_MXU)
        a_ref[...] = a
        dy2_ref[...] = dy2b
        ext2[tT:tT + HALO, :] = ext2[0:HALO, :]
        ext2[0:tT, :] = dy1
        def grad_tap(m, win):
            p = dy1 * win
            part = p[0:8, :]
            for q in range(1, tT // 8):
                part = part + p[8 * q:8 * q + 8, :]
            gpart[m - 2] += part
        _shifted_windows(ext[...], 2, grad_tap)
        cw = cw_ref[...]
        acc = [None]

        def dv_tap(m, win):
            term = win * cw[CONV_K - 1 - m:CONV_K - m, :]
            acc[0] = term if acc[0] is None else acc[0] + term
        _shifted_windows(ext2[...], 0, dv_tap)
        dv = acc[0]
        dz_ref[0] = (dv * sg).astype(_MXU)
        dz_ref[1] = (dv * cv * sg * (1.0 - sg)).astype(_MXU)

        @pl.when(i == nI - 1)
        def _():
            gcw_ref[...] = jnp.sum(gpart[...], axis=1)

    part = lambda p: pl.BlockSpec((tT, D), lambda i: (nI - 1 - i, p))
    hist = lambda p: pl.BlockSpec((HALO, D), lambda i: (jnp.maximum((nI - 1 - i) * hb - 1, 0), p))
    tok = pl.BlockSpec((tT, D), lambda i: (nI - 1 - i, 0))
    row = pl.BlockSpec((1, D), lambda i: (0, 0))
    blk = (5 * _nbytes((tT, D), F32) + _nbytes((D, D), _MXU) + 5 * _nbytes((tT, D), _MXU)
           + 10 * _nbytes((tT + HALO, D), F32))
    return pl.pallas_call(
        body, name="conv_bwd", grid=(nI,),
        in_specs=[part(0), part(1), part(2), hist(0), hist(1), tok, tok, pl.BlockSpec((HALO, D), lambda i: (0, 0)),
                  row, row, pl.BlockSpec((D, D), lambda i: (0, 0)), row],
        out_specs=[pl.BlockSpec((3, tT, D), lambda i: (0, nI - 1 - i, 0)), tok, tok,
                   pl.BlockSpec((8, D), lambda i: (0, 0)), pl.BlockSpec((HALO, D), lambda i: (0, 0))],
        out_shape=[jax.ShapeDtypeStruct((3, T, D), _MXU), jax.ShapeDtypeStruct((T, D), _MXU),
                   jax.ShapeDtypeStruct((T, D), _MXU), jax.ShapeDtypeStruct((8, D), F32),
                   jax.ShapeDtypeStruct((HALO, D), F32)],
        scratch_shapes=[pltpu.VMEM((tT + HALO, D), F32), pltpu.VMEM((tT + HALO, D), F32), pltpu.VMEM((HALO, 8, D), F32)],
        compiler_params=_cparams(blk, 1),
    )(z, z, z, z, z, y1, dyc, conv_w, cn_g, cn_b, w_pw2, b_pw2)


def _inproj_bwd_x(x, ln_g, dzc, dzh, w_in, dh, partials, tT):
    T = x.shape[0]
    n = len(partials)
    at = _hosted_steps((T // tT) * NPART)

    def body(x_ref, g_ref, dzc_ref, dzh_ref, w_ref, dh_ref, *rest):
        (gx_ref, vec_ref), du = rest[n:n + 2], rest[2 * n + 2]
        exchange = _ChipExchange(n, rest[:n], rest[n + 2:2 * n + 2], rest[2 * n + 3:3 * n + 3], rest[3 * n + 3:])
        i, j = pl.program_id(0), pl.program_id(1)
        step = i * NPART + j
        pl.when(step == at["start"])(exchange.start)
        pl.when(step == at["turn"])(exchange.turn)

        @pl.when(j == 0)
        def _():
            du[...] = jnp.zeros_like(du)

        @pl.when(jnp.logical_and(i == 0, j == 0))
        def _():
            vec_ref[...] = jnp.zeros_like(vec_ref)

        @pl.when(j < 3)
        def _():
            du[...] += _dot_nt(dzc_ref[0], w_ref[...])

        @pl.when(j >= 3)
        def _():
            du[...] += _dot_nt(dzh_ref[0], w_ref[...])

        @pl.when(j == NPART - 1)
        def _():
            xv = x_ref[...]
            r = lax.rsqrt(_mean_lanes(xv * xv) + EPS)
            xn = xv * r
            duv = du[...]
            vec_ref[0:1, :] += _sum_rows(duv * xn)
            dun = duv * g_ref[...]
            gx_ref[...] = dh_ref[...] + r * (dun - xn * _mean_lanes(dun * xn))
        pl.when(step == at["finish"])(exchange.finish)

    tok = pl.BlockSpec((tT, D), lambda i, j: (i, 0))
    blk = (3 * _nbytes((tT, D), F32) + 2 * _nbytes((tT, D), _MXU) + _nbytes((D, D), _MXU) + 4 * _nbytes((tT, D), F32)
           + _ChipExchange.scratch_bytes(partials))
    outs = pl.pallas_call(
        body, name="inproj_bwd_x", grid=(T // tT, NPART),
        in_specs=[tok, pl.BlockSpec((1, D), lambda i, j: (0, 0)),
                  pl.BlockSpec((1, tT, D), lambda i, j: (jnp.minimum(j, 2), i, 0)),
                  pl.BlockSpec((1, tT, D), lambda i, j: (jnp.maximum(j - 3, 0), i, 0)),
                  pl.BlockSpec((D, D), lambda i, j: (0, j)), tok] + [ANY] * n,
        out_specs=[tok, pl.BlockSpec((8, D), lambda i, j: (0, 0))] + [ANY] * n,
        out_shape=[jax.ShapeDtypeStruct((T, D), F32), jax.ShapeDtypeStruct((8, D), F32)]
        + [pltpu.HBM(p.shape, p.dtype) for p in partials],
        scratch_shapes=[pltpu.VMEM((tT, D), F32)] + _ChipExchange.scratch(partials),
        compiler_params=_cparams(blk, 2),
    )(x, ln_g, dzc, dzh, w_in, dh, *_in_hbm(partials))
    return outs[0], outs[1], outs[2:]


def _inproj_bwd_w(u, dzc, dzh, tk):
    T = u.shape[0]
    nK = T // tk

    def body(u_ref, dzc_ref, dzh_ref, gw_ref):
        j, k = pl.program_id(0), pl.program_id(1)

        @pl.when(k == 0)
        def _():
            gw_ref[...] = jnp.zeros_like(gw_ref)

        @pl.when(j < 3)
        def _():
            gw_ref[...] += _dot_tn(u_ref[...], dzc_ref[0])

        @pl.when(j >= 3)
        def _():
            gw_ref[...] += _dot_tn(u_ref[...], dzh_ref[0])

    blk = 3 * _nbytes((tk, D), _MXU) + 2 * _nbytes((D, D), F32)
    return pl.pallas_call(
        body, name="inproj_bwd_w", grid=(NPART, nK),
        in_specs=[pl.BlockSpec((tk, D), lambda j, k: (k, 0)),
                  pl.BlockSpec((1, tk, D), lambda j, k: (jnp.minimum(j, 2), jnp.where(j < 3, k, nK - 1), 0)),
                  pl.BlockSpec((1, tk, D), lambda j, k: (jnp.maximum(j - 3, 0), jnp.where(j < 3, 0, k), 0))],
        out_specs=pl.BlockSpec((D, D), lambda j, k: (0, j)),
        out_shape=jax.ShapeDtypeStruct((D, NPART * D), F32),
        compiler_params=_cparams(blk, 2),
    )(u, dzc, dzh)


def _tn_matmul(a, b, tk, name):
    T, M = a.shape
    N = b.shape[1]

    def body(a_ref, b_ref, o_ref):
        @pl.when(pl.program_id(0) == 0)
        def _():
            o_ref[...] = jnp.zeros_like(o_ref)
        o_ref[...] += _dot_tn(a_ref[...], b_ref[...])

    blk = _nbytes((tk, M), _MXU) + _nbytes((tk, N), _MXU) + 2 * _nbytes((M, N), F32)
    return pl.pallas_call(
        body, name=name, grid=(T // tk,),
        in_specs=[pl.BlockSpec((tk, M), lambda k: (k, 0)), pl.BlockSpec((tk, N), lambda k: (k, 0))],
        out_specs=pl.BlockSpec((M, N), lambda k: (0, 0)),
        out_shape=pltpu.HBM((M, N), F32),
        compiler_params=_cparams(blk, 1),
    )(a, b)


def _place():
    return lax.axis_index("x"), lax.axis_index("y"), lax.axis_index("c")


def _flip(v, d):
    return 1 - v if d else v


CHIP_MOVES = [(1, 0), (0, 1), (1, 1)]
DEV_MOVES = [(dx, dy, dc) for dx in (0, 1) for dy in (0, 1) for dc in (0, 1)][1:]


def _shard_slice(ref, axis, size, s):
    start = pl.multiple_of(s * size, size)
    return ref.at[pl.ds(start, size), :] if axis == 0 else ref.at[:, pl.ds(start, size)]


class _Bounce:
    def __init__(self, src, buf, dst, sem_in, sem_out):
        self.load = pltpu.make_async_copy(src, buf, sem_in)
        self.store = pltpu.make_async_copy(buf, dst, sem_out)

    def start(self):
        self.load.start()

    def turn(self):
        self.load.wait()
        self.store.start()

    def wait(self):
        self.store.wait()


def _comm_params(scratch_bytes):
    return pltpu.CompilerParams(vmem_limit_bytes=int(min(V7X_VMEM_LIMIT, scratch_bytes + (8 << 20))))


class _Gather:
    def __init__(self, shapes, axes, ins, outs, bufs, sems):
        self.shapes, self.axes, self.ins, self.outs, self.bufs = shapes, axes, ins, outs, bufs
        self.ici_send, self.ici_recv, self.d2d_send, self.d2d_recv, self.in_sems, self.out_sems = sems
        self.x, self.y, self.c = _place()
        self.me = 2 * self.x + self.y
        self.pairs = [(k, j) for k in range(len(shapes)) for j in range(3)]

    @staticmethod
    def scratch(shards):
        n = len(shards)
        return ([pltpu.VMEM(s.shape, s.dtype) for s in shards]
                + [pltpu.SemaphoreType.DMA((3 * n,))] * 4 + [pltpu.SemaphoreType.DMA((n,))] * 2)

    def _own_half(self, k, hc):
        half = self.shapes[k][0] // 2
        return self.ins[k].at[pl.ds(pl.multiple_of(hc * half, 16), half), :]

    def _region(self, k, who, hc):
        rows, cols = self.shapes[k]
        half = rows // 2
        if self.axes[k] == 0:
            return self.outs[k].at[pl.ds(pl.multiple_of(who * rows + hc * half, 16), half), :]
        return self.outs[k].at[pl.ds(pl.multiple_of(hc * half, 16), half), pl.ds(pl.multiple_of(who * cols, HD), cols)]

    def _peer(self, j):
        return 2 * _flip(self.x, CHIP_MOVES[j][0]) + _flip(self.y, CHIP_MOVES[j][1])

    def _ici(self, k, j, who, hc):
        dx, dy = CHIP_MOVES[j]
        return pltpu.make_async_remote_copy(
            src_ref=self._own_half(k, hc), dst_ref=self._region(k, who, hc),
            send_sem=self.ici_send.at[3 * k + j], recv_sem=self.ici_recv.at[3 * k + j],
            device_id=(_flip(self.x, dx), _flip(self.y, dy), self.c), device_id_type=MESH_ID)

    def _d2d(self, k, j, who, hc):
        return pltpu.make_async_remote_copy(
            src_ref=self._region(k, who, hc), dst_ref=self._region(k, who, hc),
            send_sem=self.d2d_send.at[3 * k + j], recv_sem=self.d2d_recv.at[3 * k + j],
            device_id=(self.x, self.y, 1 - self.c), device_id_type=MESH_ID)

    def _local(self, k):
        size = self.shapes[k][self.axes[k]]
        return _Bounce(self.ins[k], self.bufs[k], _shard_slice(self.outs[k], self.axes[k], size, self.me),
                       self.in_sems.at[k], self.out_sems.at[k])

    def start(self):
        for k in range(len(self.shapes)):
            self._local(k).start()
        for k, j in self.pairs:
            self._ici(k, j, self.me, self.c).start()

    def turn(self):
        for k in range(len(self.shapes)):
            self._local(k).turn()

    def forward(self):
        for k, j in self.pairs:
            self._ici(k, j, self._peer(j), self.c).wait_recv()
            self._d2d(k, j, self._peer(j), self.c).start()

    def finish(self):
        for k, j in self.pairs:
            self._d2d(k, j, self._peer(j), 1 - self.c).wait_recv()
        for k, j in self.pairs:
            self._ici(k, j, self.me, self.c).wait_send()
            self._d2d(k, j, self._peer(j), self.c).wait_send()
        for k in range(len(self.shapes)):
            self._local(k).wait()


def _full_shapes(shards, axes):
    return [tuple(d * (N_CHIPS if a == ax else 1) for a, d in enumerate(s.shape)) for s, ax in zip(shards, axes)]


class _Slab:
    def __init__(self, arrays, pick, shard_shape):
        self.arrays = arrays
        self.pick = pick
        self.rows, self.cols = shard_shape
        self.half = self.rows // 2


def _pair_exchange(slabs, name):
    n = len(slabs)
    n_in = sum(len(sl.arrays) for sl in slabs)

    def body(*refs):
        ins = refs[:n_in]
        mine, got = refs[n_in:n_in + n], refs[n_in + n:n_in + 2 * n]
        bufs = refs[n_in + 2 * n:n_in + 3 * n]
        send_sems, recv_sems, in_sems, out_sems = refs[n_in + 3 * n:]
        x, y, c = _place()
        started = []
        base = 0
        for k, sl in enumerate(slabs):
            for s in range(N_CHIPS):
                ai, r0, c0 = sl.pick(s)
                src = ins[base + ai]

                def half(hc):
                    return src.at[pl.ds(pl.multiple_of(r0 + hc * sl.half, 8), sl.half), pl.ds(c0, sl.cols)]
                q = N_CHIPS * k + s
                loc = _Bounce(half(c), bufs[k].at[s], mine[k].at[s], in_sems.at[q], out_sems.at[q])
                loc.start()
                cp = pltpu.make_async_remote_copy(
                    src_ref=half(1 - c), dst_ref=got[k].at[s], send_sem=send_sems.at[q], recv_sem=recv_sems.at[q],
                    device_id=(x, y, 1 - c), device_id_type=MESH_ID)
                cp.start()
                started.append((loc, cp))
            base += len(sl.arrays)
        for loc, cp in started:
            loc.turn()
        for loc, cp in started:
            cp.wait_recv()
        for loc, cp in started:
            cp.wait_send()
            loc.wait()

    flat_in = [a for sl in slabs for a in sl.arrays]
    compact = [pltpu.HBM((N_CHIPS, sl.half, sl.cols), F32) for sl in slabs]
    outs = pl.pallas_call(
        body, name=name,
        in_specs=[ANY] * n_in, out_specs=[ANY] * (2 * n), out_shape=compact + compact,
        scratch_shapes=[pltpu.VMEM(s.shape, F32) for s in compact] + [pltpu.SemaphoreType.DMA((N_CHIPS * n,))] * 4,
        compiler_params=_comm_params(sum(_nbytes(s.shape, F32) for s in compact)),
    )(*_in_hbm(flat_in))
    return outs[:n], outs[n:]


class _ChipExchange:
    def __init__(self, n, ins, outs, bufs, sems):
        self.n, self.ins, self.outs, self.bufs = n, ins, outs, bufs
        self.send_sems, self.recv_sems, self.in_sems, self.out_sems = sems
        self.x, self.y, self.c = _place()
        self.me = 2 * self.x + self.y
        self.pairs = [(k, j) for k in range(n) for j in range(3)]

    @staticmethod
    def scratch(partials):
        n = len(partials)
        return ([pltpu.VMEM(p.shape[1:], p.dtype) for p in partials]
                + [pltpu.SemaphoreType.DMA((3 * n,))] * 2 + [pltpu.SemaphoreType.DMA((n,))] * 2)

    @staticmethod
    def scratch_bytes(partials):
        return sum(_nbytes(p.shape[1:], p.dtype) for p in partials)

    def _copy(self, k, j, src_slot, dst_slot):
        px, py = _flip(self.x, CHIP_MOVES[j][0]), _flip(self.y, CHIP_MOVES[j][1])
        return pltpu.make_async_remote_copy(
            src_ref=self.ins[k].at[src_slot], dst_ref=self.outs[k].at[dst_slot],
            send_sem=self.send_sems.at[3 * k + j], recv_sem=self.recv_sems.at[3 * k + j],
            device_id=(px, py, self.c), device_id_type=MESH_ID)

    def _peer(self, j):
        return 2 * _flip(self.x, CHIP_MOVES[j][0]) + _flip(self.y, CHIP_MOVES[j][1])

    def _local(self, k):
        return _Bounce(self.ins[k].at[self.me], self.bufs[k], self.outs[k].at[self.me],
                       self.in_sems.at[k], self.out_sems.at[k])

    def start(self):
        for k in range(self.n):
            self._local(k).start()
        for k, j in self.pairs:
            self._copy(k, j, self._peer(j), self.me).start()

    def turn(self):
        for k in range(self.n):
            self._local(k).turn()

    def finish(self):
        for k, j in self.pairs:
            self._copy(k, j, self.me, self._peer(j)).wait_recv()
        for k, j in self.pairs:
            self._copy(k, j, self._peer(j), self.me).wait_send()
        for k in range(self.n):
            self._local(k).wait()


def _hosted_steps(steps):
    return dict(start=0, turn=steps // 4, forward=steps // 2, finish=steps - 1)


def _pair_share(halves, vec):
    n = len(halves)
    nv = len(DEV_MOVES)

    def body(*refs):
        ins, vec_ref = refs[:n], refs[n]
        outs, vec_out = refs[n + 1:2 * n + 1], refs[2 * n + 1]
        bufs = refs[2 * n + 2:3 * n + 3]
        send_sems, recv_sems, in_sems, out_sems = refs[3 * n + 3:]
        x, y, c = _place()
        dev = 4 * x + 2 * y + c

        def vec_copy(j, slot):
            dx, dy, dc = DEV_MOVES[j]
            return pltpu.make_async_remote_copy(
                src_ref=vec_ref, dst_ref=vec_out.at[slot], send_sem=send_sems.at[n + j], recv_sem=recv_sems.at[n + j],
                device_id=(_flip(x, dx), _flip(y, dy), _flip(c, dc)), device_id_type=MESH_ID)

        def rows(k, hc):
            hr = halves[k].shape[0]
            return outs[k].at[pl.ds(pl.multiple_of(hc * hr, 8), hr), :]

        def share(k, hc):
            return pltpu.make_async_remote_copy(
                src_ref=ins[k], dst_ref=rows(k, hc), send_sem=send_sems.at[k], recv_sem=recv_sems.at[k],
                device_id=(x, y, 1 - c), device_id_type=MESH_ID)

        locs = [_Bounce(vec_ref, bufs[n], vec_out.at[dev], in_sems.at[n], out_sems.at[n])]
        locs += [_Bounce(ins[k], bufs[k], rows(k, c), in_sems.at[k], out_sems.at[k]) for k in range(n)]
        for loc in locs:
            loc.start()
        for j in range(nv):
            vec_copy(j, dev).start()
        for k in range(n):
            share(k, c).start()
        for loc in locs:
            loc.turn()
        for k in range(n):
            share(k, 1 - c).wait_recv()
        for j, (dx, dy, dc) in enumerate(DEV_MOVES):
            vec_copy(j, 4 * _flip(x, dx) + 2 * _flip(y, dy) + _flip(c, dc)).wait_recv()
        for k in range(n):
            share(k, c).wait_send()
        for j in range(nv):
            vec_copy(j, dev).wait_send()
        for loc in locs:
            loc.wait()

    outs = pl.pallas_call(
        body, name="grad_pair_share",
        in_specs=[ANY] * (n + 1), out_specs=[ANY] * (n + 1),
        out_shape=[pltpu.HBM((2 * h.shape[0], h.shape[1]), F32) for h in halves] + [pltpu.HBM((N_DEV,) + vec.shape, F32)],
        scratch_shapes=[pltpu.VMEM(h.shape, F32) for h in halves] + [pltpu.VMEM(vec.shape, F32)]
        + [pltpu.SemaphoreType.DMA((n + nv,))] * 2 + [pltpu.SemaphoreType.DMA((n + 1,))] * 2,
        compiler_params=_comm_params(sum(_nbytes(h.shape, F32) for h in halves) + _nbytes(vec.shape, F32)),
    )(*_in_hbm(list(halves) + [vec]))
    return outs[:n], outs[n]


def _row_block(rows, cols, n_arrays):
    br = rows
    while br % 16 == 0 and 2 * n_arrays * br * cols * 4 > (16 << 20):
        br //= 2
    return br


def _add2(a, b, out_dtype, name):
    rows, cols = a.shape
    br = _row_block(rows, cols, 3)

    def body(a_ref, b_ref, o_ref):
        o_ref[...] = (a_ref[...] + b_ref[...]).astype(out_dtype)

    spec = pl.BlockSpec((br, cols), lambda i: (i, 0))
    return pl.pallas_call(body, name=name, grid=(rows // br,), in_specs=[spec, spec], out_specs=spec,
                          out_shape=pltpu.HBM(a.shape, out_dtype),
                          compiler_params=_cparams(3 * br * cols * 4, 1))(*_in_hbm([a, b]))


def _sum_slots(a, name):
    n, rows, cols = a.shape
    br = _row_block(rows, cols, n + 1)

    def body(a_ref, o_ref):
        acc = a_ref[0].astype(F32)
        for s in range(1, n):
            acc = acc + a_ref[s].astype(F32)
        o_ref[...] = acc

    return pl.pallas_call(body, name=name, grid=(rows // br,),
                          in_specs=[pl.BlockSpec((n, br, cols), lambda i: (0, i, 0))],
                          out_specs=pl.BlockSpec((br, cols), lambda i: (i, 0)),
                          out_shape=pltpu.HBM((rows, cols), F32),
                          compiler_params=_cparams((n + 1) * br * cols * 4, 1))(*_in_hbm([a]))


def _adamw_math(w, g, m, v):
    m = ADAM_B1 * m + (1.0 - ADAM_B1) * g
    v = ADAM_B2 * v + (1.0 - ADAM_B2) * (g * g)
    m_hat = m / (1.0 - ADAM_B1 ** ADAM_STEP)
    v_hat = v / (1.0 - ADAM_B2 ** ADAM_STEP)
    delta = -ADAM_LR * (m_hat / (jnp.sqrt(v_hat) + ADAM_EPS) + ADAM_WD * w)
    return delta, m, v


def _adamw(g, w, m, v, name):
    rows, cols = g.shape
    br = _row_block(rows, cols, 7)

    def body(g_ref, w_ref, m_ref, v_ref, d_ref, nm_ref, nv_ref):
        d_ref[...], nm_ref[...], nv_ref[...] = _adamw_math(w_ref[...], g_ref[...], m_ref[...], v_ref[...])

    spec = pl.BlockSpec((br, cols), lambda i: (i, 0))
    return pl.pallas_call(body, name=name, grid=(rows // br,), in_specs=[spec] * 4, out_specs=[spec] * 3,
                          out_shape=[jax.ShapeDtypeStruct(g.shape, F32)] * 3,
                          compiler_params=_cparams(7 * br * cols * 4, 1))(g, w, m, v)


ROW_FINAL_G, ROW_PE_G, ROW_LOSS = 0, 1, 2
ROW_CONV_B, ROW_CN_G, ROW_CN_B, ROW_B_PW2 = 8, 9, 10, 11
ROW_LN_G = 16
ROW_ONORM_G, ROW_LB = 24, 25
ROW_CONV_W = 32
SMALL = ["ln_g", "conv_b", "cnorm_g", "cnorm_b", "b_pw2", "onorm_g", "pe_norm_g", "final_g"]
SMALL_ROW = dict(ln_g=ROW_LN_G, conv_b=ROW_CONV_B, cnorm_g=ROW_CN_G, cnorm_b=ROW_CN_B, b_pw2=ROW_B_PW2,
                 onorm_g=ROW_ONORM_G, pe_norm_g=ROW_PE_G, final_g=ROW_FINAL_G)


def _adamw_small(vsum, gcw, lb_logits, params):
    names = SMALL + ["lb_logits", "conv_w"]
    flat = [t for nm in names for t in params[nm]]

    def body(*refs):
        vs_ref, gcw_ref, lbl_ref = refs[:3]
        ins = refs[3:3 + 3 * len(names)]
        outs = refs[3 + 3 * len(names):]
        for q, nm in enumerate(names):
            w_ref, m_ref, v_ref = ins[3 * q:3 * q + 3]
            g_ref, d_ref, nm_ref, nv_ref = outs[4 * q:4 * q + 4]
            if nm == "conv_w":
                g = gcw_ref[...]
            elif nm == "lb_logits":
                lb = _softmax_row0(lbl_ref[...])
                g0 = vs_ref[ROW_LB:ROW_LB + 1, :] * lb * (1.0 - lb)
                g = jnp.concatenate([g0, -g0], axis=0)
            else:
                g = vs_ref[SMALL_ROW[nm]:SMALL_ROW[nm] + 1, :]
            g_ref[...] = g
            d_ref[...], nm_ref[...], nv_ref[...] = _adamw_math(w_ref[...], g, m_ref[...], v_ref[...])

    out_shape = [jax.ShapeDtypeStruct(params[nm][0].shape, F32) for nm in names for _ in range(4)]
    outs = pl.pallas_call(body, name="adamw_small", out_shape=out_shape)(vsum, gcw, lb_logits, *flat)
    return {nm: tuple(outs[4 * q:4 * q + 4]) for q, nm in enumerate(names)}


TOKEN_TILE = dict(rmsnorm=512, inproj_fwd=1024, conv=256, hgrn=512, tail=256, inproj_bwd_x=1024, weight_grad=1024)


def _tile(T, family):
    return min(T, TOKEN_TILE[family])


def kernel(x, p, ln_g, w_in, conv_w, conv_b, cnorm_g, cnorm_b, w_pw2, b_pw2, lb_logits, onorm_g, w_out, pe_norm_g, w_pg, w_pp, final_g, loss_target, m_ln_g, m_w_in, m_conv_w, m_conv_b, m_cnorm_g, m_cnorm_b, m_w_pw2, m_b_pw2, m_lb_logits, m_onorm_g, m_w_out, m_pe_norm_g, m_w_pg, m_w_pp, m_final_g, v_ln_g, v_w_in, v_conv_w, v_conv_b, v_cnorm_g, v_cnorm_b, v_w_pw2, v_b_pw2, v_lb_logits, v_onorm_g, v_w_out, v_pe_norm_g, v_w_pg, v_w_pp, v_final_g):
    given = dict(locals())
    x2, p2, tgt = x[0], p[0, 0], loss_target[0]
    T = x2.shape[0]
    fin_g = final_g.reshape(1, D)

    conv_w_pad = jnp.pad(conv_w[0], ((0, HALO - CONV_K), (0, 0)))
    u, (w_in_f,) = _rmsnorm_gather(x2, ln_g, [w_in[0].astype(_MXU)], [1], _tile(T, "rmsnorm"))

    z, (w_pw2_f, w_out_f, w_pg_f, w_pp_f, conv_w_f) = _inproj_fwd(
        u, w_in_f,
        [w_pw2[0].astype(_MXU), w_out[0].astype(_MXU), w_pg[0].astype(_MXU), w_pp[0].astype(_MXU), conv_w_pad],
        [0, 0, 0, 1, 1], _tile(T, "inproj_fwd"))
    yc, y1 = _conv_fwd(z, conv_w_f, conv_b, cnorm_g, cnorm_b, w_pw2_f, b_pw2, _tile(T, "conv"))
    o_raw, yh, s_chunks = _hgrn_fwd(z, lb_logits, onorm_g, _tile(T, "hgrn"), HB)
    dyc, dyh, dh, n2, ds, dpe, dhb, pb, vec_tail = _tail(
        x2, yc, yh, p2, tgt, w_out_f, w_pg_f, w_pp_f, pe_norm_g, fin_g, _tile(T, "tail"))
    tk = _tile(T, "weight_grad")
    g_w_out_c = _tn_matmul(yc, dhb, tk, "grad_w_out_conv")
    g_w_out_h = _tn_matmul(yh, dhb, tk, "grad_w_out_hgrn")
    g_w_pg = _tn_matmul(n2, ds, tk, "grad_w_pg")
    g_w_pp = _tn_matmul(pb, dpe, tk, "grad_w_pp")
    dzc, a_act, dy2, vec_conv, g_conv_w = _conv_bwd(z, y1, dyc, conv_w_f, cnorm_g, cnorm_b, w_pw2_f, b_pw2, _tile(T, "conv"))
    g_w_pw2 = _tn_matmul(a_act, dy2, tk, "grad_w_pw2")

    def pair_sums(names, slabs, tag):
        mine, got = _pair_exchange(slabs, "grad_pair_exchange_" + tag)
        return [_add2(a.reshape(-1, a.shape[-1]), b.reshape(-1, b.shape[-1]), _WIRE, "pair_sum_" + nm).reshape(a.shape)
                for a, b, nm in zip(mine, got, names)]

    rest = ["w_pw2", "w_out", "w_pg", "w_pp"]
    partial_rest = pair_sums(rest, [
        _Slab([g_w_pw2], lambda s: (0, s * (D // N_CHIPS), 0), (D // N_CHIPS, D)),
        _Slab([g_w_out_c, g_w_out_h], lambda s: (s // 2, (s % 2) * (D // 2), 0), (D // 2, D)),
        _Slab([g_w_pg], lambda s: (0, s * (D // N_CHIPS), 0), (D // N_CHIPS, D)),
        _Slab([g_w_pp], lambda s: (0, 0, s * (D // N_CHIPS)), (PLE, D // N_CHIPS)),
    ], "rest")
    dzh, vec_hgrn, slots_rest = _hgrn_bwd(z, lb_logits, onorm_g, o_raw, dyh, s_chunks, partial_rest, _tile(T, "hgrn"), HB)
    g_w_in = _inproj_bwd_w(u, dzc, dzh, tk)
    partial_in = pair_sums(["w_in"], [
        _Slab([g_w_in], lambda s: (0, 0, s * (NPART * D // N_CHIPS)), (D, NPART * D // N_CHIPS))], "w_in")
    grad_x, vec_in, slots_in = _inproj_bwd_x(x2, ln_g, dzc, dzh, w_in_f, dh, partial_in, _tile(T, "inproj_bwd_x"))
    big = ["w_in"] + rest
    halves = [_sum_slots(s, "chip_sum_" + nm) for s, nm in zip(list(slots_in) + list(slots_rest), big)]
    vec = jnp.concatenate([vec_tail, vec_conv, vec_in, vec_hgrn, g_conv_w], axis=0)
    grads_big, vec_slots = _pair_share(halves, vec)
    vsum = _sum_slots(vec_slots, "vec_sum")

    out = {}
    for nm, g in zip(big, grads_big):
        w2, m2, v2 = given[nm][0], given["m_" + nm][0], given["v_" + nm][0]
        d, nm_, nv_ = _adamw(g, w2, m2, v2, "adamw_" + nm)
        out[nm] = tuple(t[None] for t in (g, d, nm_, nv_))
    chip = 2 * lax.axis_index("x") + lax.axis_index("y")
    gcw = lax.dynamic_slice(vsum, (ROW_CONV_W, chip * (D // N_CHIPS)), (CONV_K, D // N_CHIPS))
    params = {nm: (given[nm].reshape(-1, D), given["m_" + nm].reshape(-1, D), given["v_" + nm].reshape(-1, D))
              for nm in SMALL + ["lb_logits"]}
    params["conv_w"] = (conv_w[0], m_conv_w[0], v_conv_w[0])
    small = _adamw_small(vsum, gcw, lb_logits, params)
    for nm, ts in small.items():
        out[nm] = tuple(t.reshape(given[nm].shape) for t in ts)

    loss = vsum[ROW_LOSS, 0]
    order = ["ln_g", "w_in", "conv_w", "conv_b", "cnorm_g", "cnorm_b", "w_pw2", "b_pw2", "lb_logits", "onorm_g",
             "w_out", "pe_norm_g", "w_pg", "w_pp", "final_g"]
    return (loss, grad_x[None], *[out[nm][0] for nm in order], *[out[nm][1] for nm in order],
            *[out[nm][2] for nm in order], *[out[nm][3] for nm in order])
```

```python
import functools

import jax
import jax.numpy as jnp
from jax import lax
from jax.experimental import pallas as pl
from jax.experimental.pallas import tpu as pltpu

F32 = jnp.float32
BF16 = jnp.bfloat16
_MXU = jnp.bfloat16
_WIRE = jnp.bfloat16

D = 1024
NPART = 7
PLE = 256
HEADS = 8
HD = 128
CHUNK = 64
CONV_K = 31
HALO = 32
EPS = 1e-6
N_CHIPS = 4
N_DEV = 8
HB = 8
VEC_ROWS = 64

ADAM_LR = 0.001
ADAM_B1 = 0.9
ADAM_B2 = 0.999
ADAM_EPS = 1e-08
ADAM_WD = 0.01
ADAM_STEP = 10

V7X_VMEM_LIMIT = 60000 * 1024
MESH_ID = pl.DeviceIdType.MESH
ANY = pl.BlockSpec(memory_space=pltpu.HBM)


def _in_hbm(arrays):
    return [pltpu.with_memory_space_constraint(a, pltpu.HBM) for a in arrays]


def _cparams(block_bytes, n_grid_dims):
    limit = min(V7X_VMEM_LIMIT, 2 * block_bytes + (24 << 20))
    return pltpu.CompilerParams(vmem_limit_bytes=int(limit), dimension_semantics=("arbitrary",) * n_grid_dims)


def _nbytes(shape, dtype):
    n = 1
    for s in shape:
        n *= s
    return n * jnp.dtype(dtype).itemsize


def _dot(a, b):
    return jnp.dot(a.astype(_MXU), b.astype(_MXU), preferred_element_type=F32)


def _dot_nt(a, b):
    return lax.dot_general(a.astype(_MXU), b.astype(_MXU), (((1,), (1,)), ((), ())), preferred_element_type=F32)


def _dot_tn(a, b):
    return lax.dot_general(a.astype(_MXU), b.astype(_MXU), (((0,), (0,)), ((), ())), preferred_element_type=F32)


def _tri_dot(tri_bf, x):
    x1 = x.astype(BF16)
    r1 = x - x1.astype(F32)
    x2 = r1.astype(BF16)
    x3 = (r1 - x2.astype(F32)).astype(BF16)
    d = lambda t: jnp.dot(tri_bf, t, preferred_element_type=F32)
    return d(x1) + d(x2) + d(x3)


def _split2(x):
    hi = x.astype(BF16)
    return hi, (x - hi.astype(F32)).astype(BF16)


def _dot3(dims, a, b):
    d = lambda p, q: lax.dot_general(p, q, (dims, ((), ())), preferred_element_type=F32)
    return d(a[0], b[0]) + d(a[0], b[1]) + d(a[1], b[0])


def _sigmoid(x):
    return jax.nn.sigmoid(x)


def _mean_lanes(x):
    return jnp.mean(x, axis=-1, keepdims=True)


def _sum_rows(x):
    return jnp.sum(x, axis=0, keepdims=True)


def _group_ln(y):
    yn, rs = [], []
    for g in range(D // HD):
        blk = y[:, g * HD:(g + 1) * HD]
        xc = blk - _mean_lanes(blk)
        r = lax.rsqrt(_mean_lanes(xc * xc) + EPS)
        yn.append(xc * r)
        rs.append(jnp.broadcast_to(r, blk.shape))
    return jnp.concatenate(yn, axis=1), jnp.concatenate(rs, axis=1)


def _group_ln_bwd(dyn, yn, rstd):
    out = []
    for g in range(D // HD):
        sl = slice(g * HD, (g + 1) * HD)
        d, n = dyn[:, sl], yn[:, sl]
        out.append(rstd[:, sl] * (d - _mean_lanes(d) - n * _mean_lanes(d * n)))
    return jnp.concatenate(out, axis=1)


def _head_means(x, hb, fn=lambda m: m):
    return jnp.concatenate([jnp.broadcast_to(fn(_mean_lanes(x[:, hh * HD:(hh + 1) * HD])), (x.shape[0], HD))
                            for hh in range(hb)], axis=1)


def _head_rsqrt_mean(x, hb):
    return _head_means(x, hb, lambda m: lax.rsqrt(m + EPS))


def _softmax_row0(lbl):
    m = jnp.max(lbl, axis=0, keepdims=True)
    e = jnp.exp(lbl - m)
    return e[0:1, :] / jnp.sum(e, axis=0, keepdims=True)


def _hosted_gather(phases, step, at, shards, axes, ins, outs, bufs, sems):
    gather = _Gather([s.shape for s in shards], axes, ins, outs, bufs, sems)
    for phase in phases:
        pl.when(step == at[phase])(getattr(gather, phase))


def _rmsnorm_gather(x, ln_g, shards, axes, tT):
    T = x.shape[0]
    n = len(shards)
    at = _hosted_steps(T // tT)

    def body(x_ref, g_ref, *rest):
        ins, u_ref, outs, bufs, sems = rest[:n], rest[n], rest[n + 1:2 * n + 1], rest[2 * n + 1:3 * n + 1], rest[3 * n + 1:]
        host = functools.partial(_hosted_gather, step=pl.program_id(0), at=at, shards=shards, axes=axes,
                                 ins=ins, outs=outs, bufs=bufs, sems=sems)
        host(("start", "turn", "forward"))
        xv = x_ref[...]
        r = lax.rsqrt(_mean_lanes(xv * xv) + EPS)
        u_ref[...] = (xv * r * g_ref[...]).astype(_MXU)
        host(("finish",))

    blk = _nbytes((tT, D), F32) * 2 + _nbytes((tT, D), _MXU) + sum(_nbytes(s.shape, s.dtype) for s in shards)
    outs = pl.pallas_call(
        body, name="rmsnorm_gather", grid=(T // tT,),
        in_specs=[pl.BlockSpec((tT, D), lambda i: (i, 0)), pl.BlockSpec((1, D), lambda i: (0, 0))] + [ANY] * n,
        out_specs=[pl.BlockSpec((tT, D), lambda i: (i, 0))] + [ANY] * n,
        out_shape=[jax.ShapeDtypeStruct((T, D), _MXU)]
        + [pltpu.HBM(fs, s.dtype) for fs, s in zip(_full_shapes(shards, axes), shards)],
        scratch_shapes=_Gather.scratch(shards),
        compiler_params=_cparams(blk, 1),
    )(x, ln_g, *_in_hbm(shards))
    return outs[0], outs[1:]


def _inproj_fwd(u, w_in, shards, axes, tT):
    T = u.shape[0]
    n = len(shards)
    at = _hosted_steps((T // tT) * NPART)

    def body(u_ref, w_ref, *rest):
        ins, z_ref, outs, bufs, sems = rest[:n], rest[n], rest[n + 1:2 * n + 1], rest[2 * n + 1:3 * n + 1], rest[3 * n + 1:]
        host = functools.partial(_hosted_gather, step=pl.program_id(0) * NPART + pl.program_id(1), at=at, shards=shards,
                                 axes=axes, ins=ins, outs=outs, bufs=bufs, sems=sems)
        host(("start", "turn", "forward"))
        z_ref[...] = jnp.dot(u_ref[...], w_ref[...], preferred_element_type=F32)
        host(("finish",))

    blk = (_nbytes((tT, D), F32) + _nbytes((D, D), _MXU) + _nbytes((tT, D), _MXU)
           + sum(_nbytes(s.shape, s.dtype) for s in shards))
    outs = pl.pallas_call(
        body, name="inproj_fwd", grid=(T // tT, NPART),
        in_specs=[pl.BlockSpec((tT, D), lambda i, j: (i, 0)), pl.BlockSpec((D, D), lambda i, j: (0, j))] + [ANY] * n,
        out_specs=[pl.BlockSpec((tT, D), lambda i, j: (i, j))] + [ANY] * n,
        out_shape=[jax.ShapeDtypeStruct((T, NPART * D), F32)]
        + [pltpu.HBM(fs, s.dtype) for fs, s in zip(_full_shapes(shards, axes), shards)],
        scratch_shapes=_Gather.scratch(shards),
        compiler_params=_cparams(blk, 2),
    )(u, w_in, *_in_hbm(shards))
    return outs[0], outs[1:]


def _shifted_windows(ext, first, visit):
    n = ext.shape[0]
    for m in range(first, first + CONV_K):
        visit(m, (ext if m == 0 else pltpu.roll(ext, n - m, axis=0))[0:n - HALO, :])


def _conv_fwd(z, conv_w, conv_b, cn_g, cn_b, w_pw2, b_pw2, tT):
    T = z.shape[0]

    def body(cv_ref, cg_ref, ct_ref, cw_ref, cb_ref, ng_ref, nb_ref, wp_ref, bp_ref, yc_ref, y1_ref, ext):
        @pl.when(pl.program_id(0) == 0)
        def _():
            ext[...] = jnp.zeros_like(ext)
        ext[0:HALO, :] = ext[tT:tT + HALO, :]
        ext[HALO:, :] = cv_ref[...] * _sigmoid(cg_ref[...])
        cw = cw_ref[...]
        acc = [cb_ref[...]]

        def tap(m, win):
            acc[0] = acc[0] + win * cw[m - 2:m - 1, :]
        _shifted_windows(ext[...], 2, tap)
        y1 = acc[0]
        y1_ref[...] = y1
        yn, _ = _group_ln(y1)
        apre = yn * ng_ref[...] + nb_ref[...]
        a = apre * _sigmoid(apre)
        y2 = _dot(a, wp_ref[...]) + bp_ref[...]
        ct = ct_ref[...]
        yc_ref[...] = (y2 * (ct * _sigmoid(ct))).astype(_MXU)

    part = lambda p: pl.BlockSpec((tT, D), lambda i: (i, p))
    row = pl.BlockSpec((1, D), lambda i: (0, 0))
    tok = pl.BlockSpec((tT, D), lambda i: (i, 0))
    blk = 4 * _nbytes((tT, D), F32) + _nbytes((D, D), _MXU) + _nbytes((tT, D), _MXU) + 8 * _nbytes((tT + HALO, D), F32)
    return pl.pallas_call(
        body, name="conv_fwd", grid=(T // tT,),
        in_specs=[part(0), part(1), part(2), pl.BlockSpec((HALO, D), lambda i: (0, 0)), row, row, row,
                  pl.BlockSpec((D, D), lambda i: (0, 0)), row],
        out_specs=[tok, tok],
        out_shape=[jax.ShapeDtypeStruct((T, D), _MXU), jax.ShapeDtypeStruct((T, D), F32)],
        scratch_shapes=[pltpu.VMEM((tT + HALO, D), F32)],
        compiler_params=_cparams(blk, 1),
    )(z, z, z, conv_w, conv_b, cn_g, cn_b, w_pw2, b_pw2)


def _hgrn_gates(lb, hq, hf):
    sq = _sigmoid(hq)
    sg = _sigmoid(hf)
    f = lb + (1.0 - lb) * sg
    return sq, sg, f, hq * sq, (1.0 - lb) * (1.0 - sg), jnp.log(f)


def _chunk_decays(lf, q, k):
    r = lax.broadcasted_iota(jnp.int32, (CHUNK, CHUNK), 0)
    c = lax.broadcasted_iota(jnp.int32, (CHUNK, CHUNK), 1)
    b = _tri_dot((r >= c).astype(BF16), lf)
    bm = b[CHUNK // 2 - 1:CHUNK // 2, :]
    bl = b[CHUNK - 1:CHUNK, :]
    eb = jnp.exp(b)
    eqm = jnp.exp(b - bm)
    ekm = jnp.exp(bm - b)
    ekd = jnp.exp(bl - b)
    return dict(causal=r >= c, eb=eb, eqm=eqm, ekm=ekm, ekd=ekd, ebl=jnp.exp(bl),
                qd=q * eb, qm=q * eqm, km=k * ekm, kd=k * ekd)


def _hgrn_fwd(z, lb_logits, onorm_g, tT, hb):
    T = z.shape[0]
    nc = tT // CHUNK
    w = hb * HD

    def body(lbl_ref, og_ref, hq_ref, hf_ref, hi_ref, hg_ref, o_ref, yh_ref, sc_ref, st):
        @pl.when(pl.program_id(1) == 0)
        def _():
            st[...] = jnp.zeros_like(st)
        lb_all = _softmax_row0(lbl_ref[...])
        og_all = og_ref[...]

        def chunk(c, carry):
            sl = pl.ds(pl.multiple_of(c * CHUNK, CHUNK), CHUNK)
            lanes = [slice(hh * HD, (hh + 1) * HD) for hh in range(hb)]
            heads = lambda fn: [fn(hh, ln) for hh, ln in enumerate(lanes)]
            hg, v = hg_ref[sl, :], hi_ref[sl, :]
            _, _, _, q, k, lf = _hgrn_gates(lb_all, hq_ref[sl, :], hf_ref[sl, :])
            dc = _chunk_decays(lf, q, k)
            s_t = heads(lambda hh, ln: st[hh])
            a = heads(lambda hh, ln: jnp.where(dc["causal"], _dot_nt(dc["qm"][:, ln], dc["km"][:, ln]), 0.0))
            o_inter = heads(lambda hh, ln: _dot_nt(dc["qd"][:, ln], s_t[hh]))
            kv = heads(lambda hh, ln: _dot_tn(v[:, ln], dc["kd"][:, ln]))
            o_intra = heads(lambda hh, ln: _dot(a[hh], v[:, ln]))
            for hh, ln in enumerate(lanes):
                sc_ref[hh, c] = s_t[hh]
                st[hh] = s_t[hh] * dc["ebl"][:, ln] + kv[hh]
            o = jnp.concatenate([o_inter[hh] + o_intra[hh] for hh in range(hb)], axis=1)
            o_ref[sl, :] = o
            n = o * _head_rsqrt_mean(o * o, hb)
            yh_ref[sl, :] = ((n * og_all) * (hg * _sigmoid(hg))).astype(_MXU)
            return carry

        lax.fori_loop(0, nc, chunk, 0, unroll=8)

    zpart = lambda p: pl.BlockSpec((tT, w), lambda h, i: (i, p * (HEADS // hb) + h))
    blk = 6 * _nbytes((tT, w), F32) + _nbytes((hb, nc, HD, HD), F32)
    return pl.pallas_call(
        body, name="hgrn_fwd", grid=(HEADS // hb, T // tT),
        in_specs=[pl.BlockSpec((2, w), lambda h, i: (0, h)), pl.BlockSpec((1, w), lambda h, i: (0, h)),
                  zpart(3), zpart(4), zpart(5), zpart(6)],
        out_specs=[pl.BlockSpec((tT, w), lambda h, i: (i, h)), pl.BlockSpec((tT, w), lambda h, i: (i, h)),
                   pl.BlockSpec((hb, nc, HD, HD), lambda h, i: (h, i, 0, 0))],
        out_shape=[jax.ShapeDtypeStruct((T, D), F32), jax.ShapeDtypeStruct((T, D), _MXU),
                   jax.ShapeDtypeStruct((HEADS, T // CHUNK, HD, HD), F32)],
        scratch_shapes=[pltpu.VMEM((hb, HD, HD), F32)],
        compiler_params=_cparams(blk, 2),
    )(lb_logits, onorm_g, z, z, z, z)


def _hgrn_bwd(z, lb_logits, onorm_g, o_raw, dyh, s_chunks, partials, tT, hb):
    T = z.shape[0]
    nc = tT // CHUNK
    nI = T // tT
    w = hb * HD
    n = len(partials)
    at = _hosted_steps((HEADS // hb) * nI)

    def body(lbl_ref, og_ref, hq_ref, hf_ref, hi_ref, hg_ref, o_ref, dy_ref, sc_ref, *rest):
        (dz_ref, vec_ref), dst = rest[n:n + 2], rest[2 * n + 2]
        exchange = _ChipExchange(n, rest[:n], rest[n + 2:2 * n + 2], rest[2 * n + 3:3 * n + 3], rest[3 * n + 3:])
        step = pl.program_id(0) * nI + pl.program_id(1)
        pl.when(step == at["start"])(exchange.start)
        pl.when(step == at["turn"])(exchange.turn)

        @pl.when(pl.program_id(1) == 0)
        def _():
            dst[...] = jnp.zeros_like(dst)
            vec_ref[...] = jnp.zeros_like(vec_ref)
        lb_all = _softmax_row0(lbl_ref[...])
        og_all = og_ref[...]
        last_row = lax.broadcasted_iota(jnp.int32, (CHUNK, w), 0) == CHUNK - 1
        r64 = lax.broadcasted_iota(jnp.int32, (CHUNK, CHUNK), 0)
        c64 = lax.broadcasted_iota(jnp.int32, (CHUNK, CHUNK), 1)
        upper = (c64 >= r64).astype(BF16)
        lanes = [slice(hh * HD, (hh + 1) * HD) for hh in range(hb)]
        heads = lambda fn: [fn(hh, ln) for hh, ln in enumerate(lanes)]
        wide = lambda parts: jnp.concatenate(parts, axis=1)

        def chunk(cc, carry):
            c = nc - 1 - cc
            sl = pl.ds(pl.multiple_of(c * CHUNK, CHUNK), CHUNK)
            hq, hg, v = hq_ref[sl, :], hg_ref[sl, :], hi_ref[sl, :]
            sq, sg, f, q, k, lf = _hgrn_gates(lb_all, hq, hf_ref[sl, :])
            dc = _chunk_decays(lf, q, k)
            s_t = heads(lambda hh, ln: sc_ref[hh, c])
            ds_t = heads(lambda hh, ln: dst[hh])
            o, dy = o_ref[sl, :], dy_ref[sl, :]
            r = _head_rsqrt_mean(o * o, hb)
            n = o * r
            sgg = _sigmoid(hg)
            silu_g = hg * sgg
            dhg = dy * (n * og_all) * (sgg * (1.0 + hg * (1.0 - sgg)))
            dn = dy * og_all * silu_g
            g_og = _sum_rows(dy * n * silu_g)
            do = r * (dn - n * _head_means(dn * n, hb))
            a = heads(lambda hh, ln: jnp.where(dc["causal"], _dot_nt(dc["qm"][:, ln], dc["km"][:, ln]), 0.0))
            dam = heads(lambda hh, ln: jnp.where(dc["causal"], _dot_nt(do[:, ln], v[:, ln]), 0.0))
            dqd = wide(heads(lambda hh, ln: _dot(do[:, ln], s_t[hh])))
            dkd = wide(heads(lambda hh, ln: _dot(v[:, ln], ds_t[hh])))
            dv_inter = heads(lambda hh, ln: _dot_nt(dc["kd"][:, ln], ds_t[hh]))
            dqs = heads(lambda hh, ln: _dot_tn(do[:, ln], dc["qd"][:, ln]))
            dv = wide(heads(lambda hh, ln: _dot_tn(a[hh], do[:, ln]) + dv_inter[hh]))
            dam2 = [_split2(t) for t in dam]
            km2, qm2 = _split2(dc["km"]), _split2(dc["qm"])
            dqm = wide(heads(lambda hh, ln: _dot3(((1,), (0,)), dam2[hh], (km2[0][:, ln], km2[1][:, ln]))))
            dkm = wide(heads(lambda hh, ln: _dot3(((0,), (0,)), dam2[hh], (qm2[0][:, ln], qm2[1][:, ln]))))
            debl = wide(heads(lambda hh, ln: _sum_rows(ds_t[hh] * s_t[hh])))
            for hh, ln in enumerate(lanes):
                dst[hh] = ds_t[hh] * dc["ebl"][:, ln] + dqs[hh]
            dq = dqd * dc["eb"] + dqm * dc["eqm"]
            dk = dkm * dc["ekm"] + dkd * dc["ekd"]
            dbl = _sum_rows(dkd * dc["kd"]) + debl * dc["ebl"]
            db = dq * q - dk * k + jnp.where(last_row, dbl, 0.0)
            dlf = _tri_dot(upper, db)
            dfk = dlf / f - dk
            dz_ref[0, sl, :] = (dq * (sq * (1.0 + hq * (1.0 - sq)))).astype(_MXU)
            dz_ref[1, sl, :] = (dfk * ((1.0 - lb_all) * sg * (1.0 - sg))).astype(_MXU)
            dz_ref[2, sl, :] = dv.astype(_MXU)
            dz_ref[3, sl, :] = dhg.astype(_MXU)
            vec_ref[0:1, :] += g_og
            vec_ref[1:2, :] += _sum_rows(dfk * (1.0 - sg))
            return carry

        lax.fori_loop(0, nc, chunk, 0, unroll=4)
        pl.when(step == at["finish"])(exchange.finish)

    zpart = lambda p: pl.BlockSpec((tT, w), lambda h, i: (nI - 1 - i, p * (HEADS // hb) + h))
    act = pl.BlockSpec((tT, w), lambda h, i: (nI - 1 - i, h))
    blk = (6 * _nbytes((tT, w), F32) + _nbytes((hb, nc, HD, HD), F32) + 4 * _nbytes((tT, w), _MXU)
           + _ChipExchange.scratch_bytes(partials))
    outs = pl.pallas_call(
        body, name="hgrn_bwd", grid=(HEADS // hb, nI),
        in_specs=[pl.BlockSpec((2, w), lambda h, i: (0, h)), pl.BlockSpec((1, w), lambda h, i: (0, h)),
                  zpart(3), zpart(4), zpart(5), zpart(6), act, act,
                  pl.BlockSpec((hb, nc, HD, HD), lambda h, i: (h, nI - 1 - i, 0, 0))] + [ANY] * n,
        out_specs=[pl.BlockSpec((4, tT, w), lambda h, i: (0, nI - 1 - i, h)),
                   pl.BlockSpec((8, w), lambda h, i: (0, h))] + [ANY] * n,
        out_shape=[jax.ShapeDtypeStruct((4, T, D), _MXU), jax.ShapeDtypeStruct((8, D), F32)]
        + [pltpu.HBM(p.shape, p.dtype) for p in partials],
        scratch_shapes=[pltpu.VMEM((hb, HD, HD), F32)] + _ChipExchange.scratch(partials),
        compiler_params=_cparams(blk, 2),
    )(lb_logits, onorm_g, z, z, z, z, o_raw, dyh, s_chunks, *_in_hbm(partials))
    return outs[0], outs[1], outs[2:]


def _tail(x, yc, yh, p, target, w_out, w_pg, w_pp, pe_g, fin_g, tT):
    T = x.shape[0]

    def body(x_ref, yc_ref, yh_ref, p_ref, t_ref, wo_ref, wg_ref, wp_ref, pg_ref, fg_ref,
             dyc_ref, dyh_ref, dh_ref, n2_ref, ds_ref, dpe_ref, dhb_ref, pb_ref, vec_ref):
        @pl.when(pl.program_id(0) == 0)
        def _():
            vec_ref[...] = jnp.zeros_like(vec_ref)
        wo_c, wo_h = wo_ref[0:D, :], wo_ref[D:2 * D, :]
        h = x_ref[...] + _dot(yc_ref[...], wo_c) + _dot(yh_ref[...], wo_h)
        pb = p_ref[...].astype(_MXU)
        pe = _dot(pb, wp_ref[...])
        r2 = lax.rsqrt(_mean_lanes(h * h) + EPS)
        hn = h * r2
        n2 = (hn * pg_ref[...]).astype(_MXU)
        gate = _sigmoid(_dot(n2, wg_ref[...]))
        h2 = h + gate * pe
        r3 = lax.rsqrt(_mean_lanes(h2 * h2) + EPS)
        h2n = h2 * r3
        err = h2n * fg_ref[...] - t_ref[...]
        vec_ref[ROW_LOSS:ROW_LOSS + 1, :] += 0.5 * jnp.sum(_mean_lanes(err * err))
        dout = err * (1.0 / D)
        vec_ref[0:1, :] += _sum_rows(dout * h2n)
        dn3 = dout * fg_ref[...]
        dh2 = r3 * (dn3 - h2n * _mean_lanes(dn3 * h2n))
        ds = (dh2 * pe * gate * (1.0 - gate)).astype(_MXU)
        dn2 = _dot_nt(ds, wg_ref[...])
        vec_ref[1:2, :] += _sum_rows(dn2 * hn)
        dnn = dn2 * pg_ref[...]
        dh = dh2 + r2 * (dnn - hn * _mean_lanes(dnn * hn))
        dhb = dh.astype(_MXU)
        dyc_ref[...] = _dot_nt(dhb, wo_c)
        dyh_ref[...] = _dot_nt(dhb, wo_h)
        dh_ref[...] = dh
        n2_ref[...] = n2
        ds_ref[...] = ds
        dpe_ref[...] = (dh2 * gate).astype(_MXU)
        dhb_ref[...] = dhb
        pb_ref[...] = pb

    tok = lambda w: pl.BlockSpec((tT, w), lambda i: (i, 0))
    full = lambda r, c: pl.BlockSpec((r, c), lambda i: (0, 0))
    tokshape = lambda w, dt: jax.ShapeDtypeStruct((T, w), dt)
    blk = (5 * _nbytes((tT, D), F32) + 7 * _nbytes((tT, D), _MXU) + _nbytes((4 * D + PLE, D), _MXU)
           + 12 * _nbytes((tT, D), F32))
    return pl.pallas_call(
        body, name="tail_fwd_bwd", grid=(T // tT,),
        in_specs=[tok(D), tok(D), tok(D), tok(PLE), tok(D), full(2 * D, D), full(D, D), full(PLE, D), full(1, D), full(1, D)],
        out_specs=[tok(D), tok(D), tok(D), tok(D), tok(D), tok(D), tok(D), tok(PLE), full(8, D)],
        out_shape=[tokshape(D, F32), tokshape(D, F32), tokshape(D, F32), tokshape(D, _MXU), tokshape(D, _MXU),
                   tokshape(D, _MXU), tokshape(D, _MXU), tokshape(PLE, _MXU),
                   jax.ShapeDtypeStruct((8, D), F32)],
        compiler_params=_cparams(blk, 1),
    )(x, yc, yh, p, target, w_out, w_pg, w_pp, pe_g, fin_g)


def _conv_bwd(z, y1, dyc, conv_w, cn_g, cn_b, w_pw2, b_pw2, tT):
    T = z.shape[0]
    nI = T // tT
    hb = tT // HALO

    def body(cv_ref, cg_ref, ct_ref, hv_ref, hg_ref, y1_ref, dyc_ref, cw_ref, ng_ref, nb_ref, wp_ref, bp_ref,
             dz_ref, a_ref, dy2_ref, vec_ref, gcw_ref, ext, ext2, gpart):
        i = pl.program_id(0)

        @pl.when(i == 0)
        def _():
            ext2[...] = jnp.zeros_like(ext2)
            gpart[...] = jnp.zeros_like(gpart)
            vec_ref[...] = jnp.zeros_like(vec_ref)
        cv, cg, ct = cv_ref[...], cg_ref[...], ct_ref[...]
        sg = _sigmoid(cg)
        has_hist = (i < nI - 1).astype(F32)
        ext[0:HALO, :] = hv_ref[...] * _sigmoid(hg_ref[...]) * has_hist
        ext[HALO:, :] = cv * sg
        yn, rstd = _group_ln(y1_ref[...])
        apre = yn * ng_ref[...] + nb_ref[...]
        sa = _sigmoid(apre)
        a = (apre * sa).astype(_MXU)
        y2 = _dot(a, wp_ref[...]) + bp_ref[...]
        st = _sigmoid(ct)
        dyc_v = dyc_ref[...]
        dy2 = dyc_v * (ct * st)
        dy2b = dy2.astype(_MXU)
        da = _dot_nt(dy2b, wp_ref[...])
        dapre = da * (sa * (1.0 + apre * (1.0 - sa)))
        dy1 = _group_ln_bwd(dapre * ng_ref[...], yn, rstd)
        vec_ref[0:1, :] += _sum_rows(dy1)
        vec_ref[1:2, :] += _sum_rows(dapre * yn)
        vec_ref[2:3, :] += _sum_rows(dapre)
        vec_ref[3:4, :] += _sum_rows(dy2)
        dz_ref[2] = (dyc_v * y2 * (st * (1.0 + ct * (1.0 - st)))).astype(_MXU)
        a_ref[...] = a
        dy2_ref[...] = dy2b
        ext2[tT:tT + HALO, :] = ext2[0:HALO, :]
        ext2[0:tT, :] = dy1
        def grad_tap(m, win):
            p = dy1 * win
            part = p[0:8, :]
            for q in range(1, tT // 8):
                part = part + p[8 * q:8 * q + 8, :]
            gpart[m - 2] += part
        _shifted_windows(ext[...], 2, grad_tap)
        cw = cw_ref[...]
        acc = [None]

        def dv_tap(m, win):
            term = win * cw[CONV_K - 1 - m:CONV_K - m, :]
            acc[0] = term if acc[0] is None else acc[0] + term
        _shifted_windows(ext2[...], 0, dv_tap)
        dv = acc[0]
        dz_ref[0] = (dv * sg).astype(_MXU)
        dz_ref[1] = (dv * cv * sg * (1.0 - sg)).astype(_MXU)

        @pl.when(i == nI - 1)
        def _():
            gcw_ref[...] = jnp.sum(gpart[...], axis=1)

    part = lambda p: pl.BlockSpec((tT, D), lambda i: (nI - 1 - i, p))
    hist = lambda p: pl.BlockSpec((HALO, D), lambda i: (jnp.maximum((nI - 1 - i) * hb - 1, 0), p))
    tok = pl.BlockSpec((tT, D), lambda i: (nI - 1 - i, 0))
    row = pl.BlockSpec((1, D), lambda i: (0, 0))
    blk = (5 * _nbytes((tT, D), F32) + _nbytes((D, D), _MXU) + 5 * _nbytes((tT, D), _MXU)
           + 10 * _nbytes((tT + HALO, D), F32))
    return pl.pallas_call(
        body, name="conv_bwd", grid=(nI,),
        in_specs=[part(0), part(1), part(2), hist(0), hist(1), tok, tok, pl.BlockSpec((HALO, D), lambda i: (0, 0)),
                  row, row, pl.BlockSpec((D, D), lambda i: (0, 0)), row],
        out_specs=[pl.BlockSpec((3, tT, D), lambda i: (0, nI - 1 - i, 0)), tok, tok,
                   pl.BlockSpec((8, D), lambda i: (0, 0)), pl.BlockSpec((HALO, D), lambda i: (0, 0))],
        out_shape=[jax.ShapeDtypeStruct((3, T, D), _MXU), jax.ShapeDtypeStruct((T, D), _MXU),
                   jax.ShapeDtypeStruct((T, D), _MXU), jax.ShapeDtypeStruct((8, D), F32),
                   jax.ShapeDtypeStruct((HALO, D), F32)],
        scratch_shapes=[pltpu.VMEM((tT + HALO, D), F32), pltpu.VMEM((tT + HALO, D), F32), pltpu.VMEM((HALO, 8, D), F32)],
        compiler_params=_cparams(blk, 1),
    )(z, z, z, z, z, y1, dyc, conv_w, cn_g, cn_b, w_pw2, b_pw2)


def _inproj_bwd_x(x, ln_g, dzc, dzh, w_in, dh, partials, tT):
    T = x.shape[0]
    n = len(partials)
    at = _hosted_steps((T // tT) * NPART)

    def body(x_ref, g_ref, dzc_ref, dzh_ref, w_ref, dh_ref, *rest):
        (gx_ref, vec_ref), du = rest[n:n + 2], rest[2 * n + 2]
        exchange = _ChipExchange(n, rest[:n], rest[n + 2:2 * n + 2], rest[2 * n + 3:3 * n + 3], rest[3 * n + 3:])
        i, j = pl.program_id(0), pl.program_id(1)
        step = i * NPART + j
        pl.when(step == at["start"])(exchange.start)
        pl.when(step == at["turn"])(exchange.turn)

        @pl.when(j == 0)
        def _():
            du[...] = jnp.zeros_like(du)

        @pl.when(jnp.logical_and(i == 0, j == 0))
        def _():
            vec_ref[...] = jnp.zeros_like(vec_ref)

        @pl.when(j < 3)
        def _():
            du[...] += _dot_nt(dzc_ref[0], w_ref[...])

        @pl.when(j >= 3)
        def _():
            du[...] += _dot_nt(dzh_ref[0], w_ref[...])

        @pl.when(j == NPART - 1)
        def _():
            xv = x_ref[...]
            r = lax.rsqrt(_mean_lanes(xv * xv) + EPS)
            xn = xv * r
            duv = du[...]
            vec_ref[0:1, :] += _sum_rows(duv * xn)
            dun = duv * g_ref[...]
            gx_ref[...] = dh_ref[...] + r * (dun - xn * _mean_lanes(dun * xn))
        pl.when(step == at["finish"])(exchange.finish)

    tok = pl.BlockSpec((tT, D), lambda i, j: (i, 0))
    blk = (3 * _nbytes((tT, D), F32) + 2 * _nbytes((tT, D), _MXU) + _nbytes((D, D), _MXU) + 4 * _nbytes((tT, D), F32)
           + _ChipExchange.scratch_bytes(partials))
    outs = pl.pallas_call(
        body, name="inproj_bwd_x", grid=(T // tT, NPART),
        in_specs=[tok, pl.BlockSpec((1, D), lambda i, j: (0, 0)),
                  pl.BlockSpec((1, tT, D), lambda i, j: (jnp.minimum(j, 2), i, 0)),
                  pl.BlockSpec((1, tT, D), lambda i, j: (jnp.maximum(j - 3, 0), i, 0)),
                  pl.BlockSpec((D, D), lambda i, j: (0, j)), tok] + [ANY] * n,
        out_specs=[tok, pl.BlockSpec((8, D), lambda i, j: (0, 0))] + [ANY] * n,
        out_shape=[jax.ShapeDtypeStruct((T, D), F32), jax.ShapeDtypeStruct((8, D), F32)]
        + [pltpu.HBM(p.shape, p.dtype) for p in partials],
        scratch_shapes=[pltpu.VMEM((tT, D), F32)] + _ChipExchange.scratch(partials),
        compiler_params=_cparams(blk, 2),
    )(x, ln_g, dzc, dzh, w_in, dh, *_in_hbm(partials))
    return outs[0], outs[1], outs[2:]


def _inproj_bwd_w(u, dzc, dzh, tk):
    T = u.shape[0]
    nK = T // tk

    def body(u_ref, dzc_ref, dzh_ref, gw_ref):
        j, k = pl.program_id(0), pl.program_id(1)

        @pl.when(k == 0)
        def _():
            gw_ref[...] = jnp.zeros_like(gw_ref)

        @pl.when(j < 3)
        def _():
            gw_ref[...] += _dot_tn(u_ref[...], dzc_ref[0])

        @pl.when(j >= 3)
        def _():
            gw_ref[...] += _dot_tn(u_ref[...], dzh_ref[0])

    blk = 3 * _nbytes((tk, D), _MXU) + 2 * _nbytes((D, D), F32)
    return pl.pallas_call(
        body, name="inproj_bwd_w", grid=(NPART, nK),
        in_specs=[pl.BlockSpec((tk, D), lambda j, k: (k, 0)),
                  pl.BlockSpec((1, tk, D), lambda j, k: (jnp.minimum(j, 2), jnp.where(j < 3, k, nK - 1), 0)),
                  pl.BlockSpec((1, tk, D), lambda j, k: (jnp.maximum(j - 3, 0), jnp.where(j < 3, 0, k), 0))],
        out_specs=pl.BlockSpec((D, D), lambda j, k: (0, j)),
        out_shape=jax.ShapeDtypeStruct((D, NPART * D), F32),
        compiler_params=_cparams(blk, 2),
    )(u, dzc, dzh)


def _tn_matmul(a, b, tk, name):
    T, M = a.shape
    N = b.shape[1]

    def body(a_ref, b_ref, o_ref):
        @pl.when(pl.program_id(0) == 0)
        def _():
            o_ref[...] = jnp.zeros_like(o_ref)
        o_ref[...] += _dot_tn(a_ref[...], b_ref[...])

    blk = _nbytes((tk, M), _MXU) + _nbytes((tk, N), _MXU) + 2 * _nbytes((M, N), F32)
    return pl.pallas_call(
        body, name=name, grid=(T // tk,),
        in_specs=[pl.BlockSpec((tk, M), lambda k: (k, 0)), pl.BlockSpec((tk, N), lambda k: (k, 0))],
        out_specs=pl.BlockSpec((M, N), lambda k: (0, 0)),
        out_shape=pltpu.HBM((M, N), F32),
        compiler_params=_cparams(blk, 1),
    )(a, b)


def _place():
    return lax.axis_index("x"), lax.axis_index("y"), lax.axis_index("c")


def _flip(v, d):
    return 1 - v if d else v


CHIP_MOVES = [(1, 0), (0, 1), (1, 1)]
DEV_MOVES = [(dx, dy, dc) for dx in (0, 1) for dy in (0, 1) for dc in (0, 1)][1:]


def _shard_slice(ref, axis, size, s):
    start = pl.multiple_of(s * size, size)
    return ref.at[pl.ds(start, size), :] if axis == 0 else ref.at[:, pl.ds(start, size)]


class _Bounce:
    def __init__(self, src, buf, dst, sem_in, sem_out):
        self.load = pltpu.make_async_copy(src, buf, sem_in)
        self.store = pltpu.make_async_copy(buf, dst, sem_out)

    def start(self):
        self.load.start()

    def turn(self):
        self.load.wait()
        self.store.start()

    def wait(self):
        self.store.wait()


def _comm_params(scratch_bytes):
    return pltpu.CompilerParams(vmem_limit_bytes=int(min(V7X_VMEM_LIMIT, scratch_bytes + (8 << 20))))


class _Gather:
    def __init__(self, shapes, axes, ins, outs, bufs, sems):
        self.shapes, self.axes, self.ins, self.outs, self.bufs = shapes, axes, ins, outs, bufs
        self.ici_send, self.ici_recv, self.d2d_send, self.d2d_recv, self.in_sems, self.out_sems = sems
        self.x, self.y, self.c = _place()
        self.me = 2 * self.x + self.y
        self.pairs = [(k, j) for k in range(len(shapes)) for j in range(3)]

    @staticmethod
    def scratch(shards):
        n = len(shards)
        return ([pltpu.VMEM(s.shape, s.dtype) for s in shards]
                + [pltpu.SemaphoreType.DMA((3 * n,))] * 4 + [pltpu.SemaphoreType.DMA((n,))] * 2)

    def _own_half(self, k, hc):
        half = self.shapes[k][0] // 2
        return self.ins[k].at[pl.ds(pl.multiple_of(hc * half, 16), half), :]

    def _region(self, k, who, hc):
        rows, cols = self.shapes[k]
        half = rows // 2
        if self.axes[k] == 0:
            return self.outs[k].at[pl.ds(pl.multiple_of(who * rows + hc * half, 16), half), :]
        return self.outs[k].at[pl.ds(pl.multiple_of(hc * half, 16), half), pl.ds(pl.multiple_of(who * cols, HD), cols)]

    def _peer(self, j):
        return 2 * _flip(self.x, CHIP_MOVES[j][0]) + _flip(self.y, CHIP_MOVES[j][1])

    def _ici(self, k, j, who, hc):
        dx, dy = CHIP_MOVES[j]
        return pltpu.make_async_remote_copy(
            src_ref=self._own_half(k, hc), dst_ref=self._region(k, who, hc),
            send_sem=self.ici_send.at[3 * k + j], recv_sem=self.ici_recv.at[3 * k + j],
            device_id=(_flip(self.x, dx), _flip(self.y, dy), self.c), device_id_type=MESH_ID)

    def _d2d(self, k, j, who, hc):
        return pltpu.make_async_remote_copy(
            src_ref=self._region(k, who, hc), dst_ref=self._region(k, who, hc),
            send_sem=self.d2d_send.at[3 * k + j], recv_sem=self.d2d_recv.at[3 * k + j],
            device_id=(self.x, self.y, 1 - self.c), device_id_type=MESH_ID)

    def _local(self, k):
        size = self.shapes[k][self.axes[k]]
        return _Bounce(self.ins[k], self.bufs[k], _shard_slice(self.outs[k], self.axes[k], size, self.me),
                       self.in_sems.at[k], self.out_sems.at[k])

    def start(self):
        for k in range(len(self.shapes)):
            self._local(k).start()
        for k, j in self.pairs:
            self._ici(k, j, self.me, self.c).start()

    def turn(self):
        for k in range(len(self.shapes)):
            self._local(k).turn()

    def forward(self):
        for k, j in self.pairs:
            self._ici(k, j, self._peer(j), self.c).wait_recv()
            self._d2d(k, j, self._peer(j), self.c).start()

    def finish(self):
        for k, j in self.pairs:
            self._d2d(k, j, self._peer(j), 1 - self.c).wait_recv()
        for k, j in self.pairs:
            self._ici(k, j, self.me, self.c).wait_send()
            self._d2d(k, j, self._peer(j), self.c).wait_send()
        for k in range(len(self.shapes)):
            self._local(k).wait()


def _full_shapes(shards, axes):
    return [tuple(d * (N_CHIPS if a == ax else 1) for a, d in enumerate(s.shape)) for s, ax in zip(shards, axes)]


class _Slab:
    def __init__(self, arrays, pick, shard_shape):
        self.arrays = arrays
        self.pick = pick
        self.rows, self.cols = shard_shape
        self.half = self.rows // 2


def _pair_exchange(slabs, name):
    n = len(slabs)
    n_in = sum(len(sl.arrays) for sl in slabs)

    def body(*refs):
        ins = refs[:n_in]
        mine, got = refs[n_in:n_in + n], refs[n_in + n:n_in + 2 * n]
        bufs = refs[n_in + 2 * n:n_in + 3 * n]
        send_sems, recv_sems, in_sems, out_sems = refs[n_in + 3 * n:]
        x, y, c = _place()
        started = []
        base = 0
        for k, sl in enumerate(slabs):
            for s in range(N_CHIPS):
                ai, r0, c0 = sl.pick(s)
                src = ins[base + ai]

                def half(hc):
                    return src.at[pl.ds(pl.multiple_of(r0 + hc * sl.half, 8), sl.half), pl.ds(c0, sl.cols)]
                q = N_CHIPS * k + s
                loc = _Bounce(half(c), bufs[k].at[s], mine[k].at[s], in_sems.at[q], out_sems.at[q])
                loc.start()
                cp = pltpu.make_async_remote_copy(
                    src_ref=half(1 - c), dst_ref=got[k].at[s], send_sem=send_sems.at[q], recv_sem=recv_sems.at[q],
                    device_id=(x, y, 1 - c), device_id_type=MESH_ID)
                cp.start()
                started.append((loc, cp))
            base += len(sl.arrays)
        for loc, cp in started:
            loc.turn()
        for loc, cp in started:
            cp.wait_recv()
        for loc, cp in started:
            cp.wait_send()
            loc.wait()

    flat_in = [a for sl in slabs for a in sl.arrays]
    compact = [pltpu.HBM((N_CHIPS, sl.half, sl.cols), F32) for sl in slabs]
    outs = pl.pallas_call(
        body, name=name,
        in_specs=[ANY] * n_in, out_specs=[ANY] * (2 * n), out_shape=compact + compact,
        scratch_shapes=[pltpu.VMEM(s.shape, F32) for s in compact] + [pltpu.SemaphoreType.DMA((N_CHIPS * n,))] * 4,
        compiler_params=_comm_params(sum(_nbytes(s.shape, F32) for s in compact)),
    )(*_in_hbm(flat_in))
    return outs[:n], outs[n:]


class _ChipExchange:
    def __init__(self, n, ins, outs, bufs, sems):
        self.n, self.ins, self.outs, self.bufs = n, ins, outs, bufs
        self.send_sems, self.recv_sems, self.in_sems, self.out_sems = sems
        self.x, self.y, self.c = _place()
        self.me = 2 * self.x + self.y
        self.pairs = [(k, j) for k in range(n) for j in range(3)]

    @staticmethod
    def scratch(partials):
        n = len(partials)
        return ([pltpu.VMEM(p.shape[1:], p.dtype) for p in partials]
                + [pltpu.SemaphoreType.DMA((3 * n,))] * 2 + [pltpu.SemaphoreType.DMA((n,))] * 2)

    @staticmethod
    def scratch_bytes(partials):
        return sum(_nbytes(p.shape[1:], p.dtype) for p in partials)

    def _copy(self, k, j, src_slot, dst_slot):
        px, py = _flip(self.x, CHIP_MOVES[j][0]), _flip(self.y, CHIP_MOVES[j][1])
        return pltpu.make_async_remote_copy(
            src_ref=self.ins[k].at[src_slot], dst_ref=self.outs[k].at[dst_slot],
            send_sem=self.send_sems.at[3 * k + j], recv_sem=self.recv_sems.at[3 * k + j],
            device_id=(px, py, self.c), device_id_type=MESH_ID)

    def _peer(self, j):
        return 2 * _flip(self.x, CHIP_MOVES[j][0]) + _flip(self.y, CHIP_MOVES[j][1])

    def _local(self, k):
        return _Bounce(self.ins[k].at[self.me], self.bufs[k], self.outs[k].at[self.me],
                       self.in_sems.at[k], self.out_sems.at[k])

    def start(self):
        for k in range(self.n):
            self._local(k).start()
        for k, j in self.pairs:
            self._copy(k, j, self._peer(j), self.me).start()

    def turn(self):
        for k in range(self.n):
            self._local(k).turn()

    def finish(self):
        for k, j in self.pairs:
            self._copy(k, j, self.me, self._peer(j)).wait_recv()
        for k, j in self.pairs:
            self._copy(k, j, self._peer(j), self.me).wait_send()
        for k in range(self.n):
            self._local(k).wait()


def _hosted_steps(steps):
    return dict(start=0, turn=steps // 4, forward=steps // 2, finish=steps - 1)


def _pair_share(halves, vec):
    n = len(halves)
    nv = len(DEV_MOVES)

    def body(*refs):
        ins, vec_ref = refs[:n], refs[n]
        outs, vec_out = refs[n + 1:2 * n + 1], refs[2 * n + 1]
        bufs = refs[2 * n + 2:3 * n + 3]
        send_sems, recv_sems, in_sems, out_sems = refs[3 * n + 3:]
        x, y, c = _place()
        dev = 4 * x + 2 * y + c

        def vec_copy(j, slot):
            dx, dy, dc = DEV_MOVES[j]
            return pltpu.make_async_remote_copy(
                src_ref=vec_ref, dst_ref=vec_out.at[slot], send_sem=send_sems.at[n + j], recv_sem=recv_sems.at[n + j],
                device_id=(_flip(x, dx), _flip(y, dy), _flip(c, dc)), device_id_type=MESH_ID)

        def rows(k, hc):
            hr = halves[k].shape[0]
            return outs[k].at[pl.ds(pl.multiple_of(hc * hr, 8), hr), :]

        def share(k, hc):
            return pltpu.make_async_remote_copy(
                src_ref=ins[k], dst_ref=rows(k, hc), send_sem=send_sems.at[k], recv_sem=recv_sems.at[k],
                device_id=(x, y, 1 - c), device_id_type=MESH_ID)

        locs = [_Bounce(vec_ref, bufs[n], vec_out.at[dev], in_sems.at[n], out_sems.at[n])]
        locs += [_Bounce(ins[k], bufs[k], rows(k, c), in_sems.at[k], out_sems.at[k]) for k in range(n)]
        for loc in locs:
            loc.start()
        for j in range(nv):
            vec_copy(j, dev).start()
        for k in range(n):
            share(k, c).start()
        for loc in locs:
            loc.turn()
        for k in range(n):
            share(k, 1 - c).wait_recv()
        for j, (dx, dy, dc) in enumerate(DEV_MOVES):
            vec_copy(j, 4 * _flip(x, dx) + 2 * _flip(y, dy) + _flip(c, dc)).wait_recv()
        for k in range(n):
            share(k, c).wait_send()
        for j in range(nv):
            vec_copy(j, dev).wait_send()
        for loc in locs:
            loc.wait()

    outs = pl.pallas_call(
        body, name="grad_pair_share",
        in_specs=[ANY] * (n + 1), out_specs=[ANY] * (n + 1),
        out_shape=[pltpu.HBM((2 * h.shape[0], h.shape[1]), F32) for h in halves] + [pltpu.HBM((N_DEV,) + vec.shape, F32)],
        scratch_shapes=[pltpu.VMEM(h.shape, F32) for h in halves] + [pltpu.VMEM(vec.shape, F32)]
        + [pltpu.SemaphoreType.DMA((n + nv,))] * 2 + [pltpu.SemaphoreType.DMA((n + 1,))] * 2,
        compiler_params=_comm_params(sum(_nbytes(h.shape, F32) for h in halves) + _nbytes(vec.shape, F32)),
    )(*_in_hbm(list(halves) + [vec]))
    return outs[:n], outs[n]


def _row_block(rows, cols, n_arrays):
    br = rows
    while br % 16 == 0 and 2 * n_arrays * br * cols * 4 > (16 << 20):
        br //= 2
    return br


def _add2(a, b, out_dtype, name):
    rows, cols = a.shape
    br = _row_block(rows, cols, 3)

    def body(a_ref, b_ref, o_ref):
        o_ref[...] = (a_ref[...] + b_ref[...]).astype(out_dtype)

    spec = pl.BlockSpec((br, cols), lambda i: (i, 0))
    return pl.pallas_call(body, name=name, grid=(rows // br,), in_specs=[spec, spec], out_specs=spec,
                          out_shape=pltpu.HBM(a.shape, out_dtype),
                          compiler_params=_cparams(3 * br * cols * 4, 1))(*_in_hbm([a, b]))


def _sum_slots(a, name):
    n, rows, cols = a.shape
    br = _row_block(rows, cols, n + 1)

    def body(a_ref, o_ref):
        acc = a_ref[0].astype(F32)
        for s in range(1, n):
            acc = acc + a_ref[s].astype(F32)
        o_ref[...] = acc

    return pl.pallas_call(body, name=name, grid=(rows // br,),
                          in_specs=[pl.BlockSpec((n, br, cols), lambda i: (0, i, 0))],
                          out_specs=pl.BlockSpec((br, cols), lambda i: (i, 0)),
                          out_shape=pltpu.HBM((rows, cols), F32),
                          compiler_params=_cparams((n + 1) * br * cols * 4, 1))(*_in_hbm([a]))


def _adamw_math(w, g, m, v):
    m = ADAM_B1 * m + (1.0 - ADAM_B1) * g
    v = ADAM_B2 * v + (1.0 - ADAM_B2) * (g * g)
    m_hat = m / (1.0 - ADAM_B1 ** ADAM_STEP)
    v_hat = v / (1.0 - ADAM_B2 ** ADAM_STEP)
    delta = -ADAM_LR * (m_hat / (jnp.sqrt(v_hat) + ADAM_EPS) + ADAM_WD * w)
    return delta, m, v


def _adamw(g, w, m, v, name):
    rows, cols = g.shape
    br = _row_block(rows, cols, 7)

    def body(g_ref, w_ref, m_ref, v_ref, d_ref, nm_ref, nv_ref):
        d_ref[...], nm_ref[...], nv_ref[...] = _adamw_math(w_ref[...], g_ref[...], m_ref[...], v_ref[...])

    spec = pl.BlockSpec((br, cols), lambda i: (i, 0))
    return pl.pallas_call(body, name=name, grid=(rows // br,), in_specs=[spec] * 4, out_specs=[spec] * 3,
                          out_shape=[jax.ShapeDtypeStruct(g.shape, F32)] * 3,
                          compiler_params=_cparams(7 * br * cols * 4, 1))(g, w, m, v)


ROW_FINAL_G, ROW_PE_G, ROW_LOSS = 0, 1, 2
ROW_CONV_B, ROW_CN_G, ROW_CN_B, ROW_B_PW2 = 8, 9, 10, 11
ROW_LN_G = 16
ROW_ONORM_G, ROW_LB = 24, 25
ROW_CONV_W = 32
SMALL = ["ln_g", "conv_b", "cnorm_g", "cnorm_b", "b_pw2", "onorm_g", "pe_norm_g", "final_g"]
SMALL_ROW = dict(ln_g=ROW_LN_G, conv_b=ROW_CONV_B, cnorm_g=ROW_CN_G, cnorm_b=ROW_CN_B, b_pw2=ROW_B_PW2,
                 onorm_g=ROW_ONORM_G, pe_norm_g=ROW_PE_G, final_g=ROW_FINAL_G)


def _adamw_small(vsum, gcw, lb_logits, params):
    names = SMALL + ["lb_logits", "conv_w"]
    flat = [t for nm in names for t in params[nm]]

    def body(*refs):
        vs_ref, gcw_ref, lbl_ref = refs[:3]
        ins = refs[3:3 + 3 * len(names)]
        outs = refs[3 + 3 * len(names):]
        for q, nm in enumerate(names):
            w_ref, m_ref, v_ref = ins[3 * q:3 * q + 3]
            g_ref, d_ref, nm_ref, nv_ref = outs[4 * q:4 * q + 4]
            if nm == "conv_w":
                g = gcw_ref[...]
            elif nm == "lb_logits":
                lb = _softmax_row0(lbl_ref[...])
                g0 = vs_ref[ROW_LB:ROW_LB + 1, :] * lb * (1.0 - lb)
                g = jnp.concatenate([g0, -g0], axis=0)
            else:
                g = vs_ref[SMALL_ROW[nm]:SMALL_ROW[nm] + 1, :]
            g_ref[...] = g
            d_ref[...], nm_ref[...], nv_ref[...] = _adamw_math(w_ref[...], g, m_ref[...], v_ref[...])

    out_shape = [jax.ShapeDtypeStruct(params[nm][0].shape, F32) for nm in names for _ in range(4)]
    outs = pl.pallas_call(body, name="adamw_small", out_shape=out_shape)(vsum, gcw, lb_logits, *flat)
    return {nm: tuple(outs[4 * q:4 * q + 4]) for q, nm in enumerate(names)}


TOKEN_TILE = dict(rmsnorm=512, inproj_fwd=2048, conv=256, hgrn=512, tail=256, inproj_bwd_x=1024, weight_grad=2048)


def _tile(T, family):
    return min(T, TOKEN_TILE[family])


def kernel(x, p, ln_g, w_in, conv_w, conv_b, cnorm_g, cnorm_b, w_pw2, b_pw2, lb_logits, onorm_g, w_out, pe_norm_g, w_pg, w_pp, final_g, loss_target, m_ln_g, m_w_in, m_conv_w, m_conv_b, m_cnorm_g, m_cnorm_b, m_w_pw2, m_b_pw2, m_lb_logits, m_onorm_g, m_w_out, m_pe_norm_g, m_w_pg, m_w_pp, m_final_g, v_ln_g, v_w_in, v_conv_w, v_conv_b, v_cnorm_g, v_cnorm_b, v_w_pw2, v_b_pw2, v_lb_logits, v_onorm_g, v_w_out, v_pe_norm_g, v_w_pg, v_w_pp, v_final_g):
    given = dict(locals())
    x2, p2, tgt = x[0], p[0, 0], loss_target[0]
    T = x2.shape[0]
    fin_g = final_g.reshape(1, D)

    conv_w_pad = jnp.pad(conv_w[0], ((0, HALO - CONV_K), (0, 0)))
    u, (w_in_f,) = _rmsnorm_gather(x2, ln_g, [w_in[0].astype(_MXU)], [1], _tile(T, "rmsnorm"))

    z, (w_pw2_f, w_out_f, w_pg_f, w_pp_f, conv_w_f) = _inproj_fwd(
        u, w_in_f,
        [w_pw2[0].astype(_MXU), w_out[0].astype(_MXU), w_pg[0].astype(_MXU), w_pp[0].astype(_MXU), conv_w_pad],
        [0, 0, 0, 1, 1], _tile(T, "inproj_fwd"))
    yc, y1 = _conv_fwd(z, conv_w_f, conv_b, cnorm_g, cnorm_b, w_pw2_f, b_pw2, _tile(T, "conv"))
    o_raw, yh, s_chunks = _hgrn_fwd(z, lb_logits, onorm_g, _tile(T, "hgrn"), HB)
    dyc, dyh, dh, n2, ds, dpe, dhb, pb, vec_tail = _tail(
        x2, yc, yh, p2, tgt, w_out_f, w_pg_f, w_pp_f, pe_norm_g, fin_g, _tile(T, "tail"))
    tk = _tile(T, "weight_grad")
    g_w_out_c = _tn_matmul(yc, dhb, tk, "grad_w_out_conv")
    g_w_out_h = _tn_matmul(yh, dhb, tk, "grad_w_out_hgrn")
    g_w_pg = _tn_matmul(n2, ds, tk, "grad_w_pg")
    g_w_pp = _tn_matmul(pb, dpe, tk, "grad_w_pp")
    dzc, a_act, dy2, vec_conv, g_conv_w = _conv_bwd(z, y1, dyc, conv_w_f, cnorm_g, cnorm_b, w_pw2_f, b_pw2, _tile(T, "conv"))
    g_w_pw2 = _tn_matmul(a_act, dy2, tk, "grad_w_pw2")

    def pair_sums(names, slabs, tag):
        mine, got = _pair_exchange(slabs, "grad_pair_exchange_" + tag)
        return [_add2(a.reshape(-1, a.shape[-1]), b.reshape(-1, b.shape[-1]), _WIRE, "pair_sum_" + nm).reshape(a.shape)
                for a, b, nm in zip(mine, got, names)]

    rest = ["w_pw2", "w_out", "w_pg", "w_pp"]
    partial_rest = pair_sums(rest, [
        _Slab([g_w_pw2], lambda s: (0, s * (D // N_CHIPS), 0), (D // N_CHIPS, D)),
        _Slab([g_w_out_c, g_w_out_h], lambda s: (s // 2, (s % 2) * (D // 2), 0), (D // 2, D)),
        _Slab([g_w_pg], lambda s: (0, s * (D // N_CHIPS), 0), (D // N_CHIPS, D)),
        _Slab([g_w_pp], lambda s: (0, 0, s * (D // N_CHIPS)), (PLE, D // N_CHIPS)),
    ], "rest")
    dzh, vec_hgrn, slots_rest = _hgrn_bwd(z, lb_logits, onorm_g, o_raw, dyh, s_chunks, partial_rest, _tile(T, "hgrn"), HB)
    g_w_in = _inproj_bwd_w(u, dzc, dzh, tk)
    partial_in = pair_sums(["w_in"], [
        _Slab([g_w_in], lambda s: (0, 0, s * (NPART * D // N_CHIPS)), (D, NPART * D // N_CHIPS))], "w_in")
    grad_x, vec_in, slots_in = _inproj_bwd_x(x2, ln_g, dzc, dzh, w_in_f, dh, partial_in, _tile(T, "inproj_bwd_x"))
    big = ["w_in"] + rest
    halves = [_sum_slots(s, "chip_sum_" + nm) for s, nm in zip(list(slots_in) + list(slots_rest), big)]
    vec = jnp.concatenate([vec_tail, vec_conv, vec_in, vec_hgrn, g_conv_w], axis=0)
    grads_big, vec_slots = _pair_share(halves, vec)
    vsum = _sum_slots(vec_slots, "vec_sum")

    out = {}
    for nm, g in zip(big, grads_big):
        w2, m2, v2 = given[nm][0], given["m_" + nm][0], given["v_" + nm][0]
        d, nm_, nv_ = _adamw(g, w2, m2, v2, "adamw_" + nm)
        out[nm] = tuple(t[None] for t in (g, d, nm_, nv_))
    chip = 2 * lax.axis_index("x") + lax.axis_index("y")
    gcw = lax.dynamic_slice(vsum, (ROW_CONV_W, chip * (D // N_CHIPS)), (CONV_K, D // N_CHIPS))
    params = {nm: (given[nm].reshape(-1, D), given["m_" + nm].reshape(-1, D), given["v_" + nm].reshape(-1, D))
              for nm in SMALL + ["lb_logits"]}
    params["conv_w"] = (conv_w[0], m_conv_w[0], v_conv_w[0])
    small = _adamw_small(vsum, gcw, lb_logits, params)
    for nm, ts in small.items():
        out[nm] = tuple(t.reshape(given[nm].shape) for t in ts)

    loss = vsum[ROW_LOSS, 0]
    order = ["ln_g", "w_in", "conv_w", "conv_b", "cnorm_g", "cnorm_b", "w_pw2", "b_pw2", "lb_logits", "onorm_g",
             "w_out", "pe_norm_g", "w_pg", "w_pp", "final_g"]
    return (loss, grad_x[None], *[out[nm][0] for nm in order], *[out[nm][1] for nm in order],
            *[out[nm][2] for nm in order], *[out[nm][3] for nm in order])
```

```python
import functools

import jax
import jax.numpy as jnp
from jax import lax
from jax.experimental import pallas as pl
from jax.experimental.pallas import tpu as pltpu

F32 = jnp.float32
BF16 = jnp.bfloat16
_MXU = jnp.bfloat16
_WIRE = jnp.bfloat16

D = 1024
NPART = 7
PLE = 256
HEADS = 8
HD = 128
CHUNK = 64
CONV_K = 31
HALO = 32
EPS = 1e-6
N_CHIPS = 4
N_DEV = 8
HB = 8
VEC_ROWS = 64

ADAM_LR = 0.001
ADAM_B1 = 0.9
ADAM_B2 = 0.999
ADAM_EPS = 1e-08
ADAM_WD = 0.01
ADAM_STEP = 10

V7X_VMEM_LIMIT = 60000 * 1024
MESH_ID = pl.DeviceIdType.MESH
ANY = pl.BlockSpec(memory_space=pltpu.HBM)


def _in_hbm(arrays):
    return [pltpu.with_memory_space_constraint(a, pltpu.HBM) for a in arrays]


def _cparams(block_bytes, n_grid_dims):
    limit = min(V7X_VMEM_LIMIT, 2 * block_bytes + (24 << 20))
    return pltpu.CompilerParams(vmem_limit_bytes=int(limit), dimension_semantics=("arbitrary",) * n_grid_dims)


def _nbytes(shape, dtype):
    n = 1
    for s in shape:
        n *= s
    return n * jnp.dtype(dtype).itemsize


def _dot(a, b):
    return jnp.dot(a.astype(_MXU), b.astype(_MXU), preferred_element_type=F32)


def _dot_nt(a, b):
    return lax.dot_general(a.astype(_MXU), b.astype(_MXU), (((1,), (1,)), ((), ())), preferred_element_type=F32)


def _dot_tn(a, b):
    return lax.dot_general(a.astype(_MXU), b.astype(_MXU), (((0,), (0,)), ((), ())), preferred_element_type=F32)


def _tri_dot(tri_bf, x):
    x1 = x.astype(BF16)
    r1 = x - x1.astype(F32)
    x2 = r1.astype(BF16)
    x3 = (r1 - x2.astype(F32)).astype(BF16)
    d = lambda t: jnp.dot(tri_bf, t, preferred_element_type=F32)
    return d(x1) + d(x2) + d(x3)


def _split2(x):
    hi = x.astype(BF16)
    return hi, (x - hi.astype(F32)).astype(BF16)


def _dot3(dims, a, b):
    d = lambda p, q: lax.dot_general(p, q, (dims, ((), ())), preferred_element_type=F32)
    return d(a[0], b[0]) + d(a[0], b[1]) + d(a[1], b[0])


def _sigmoid(x):
    return jax.nn.sigmoid(x)


def _mean_lanes(x):
    return jnp.mean(x, axis=-1, keepdims=True)


def _sum_rows(x):
    return jnp.sum(x, axis=0, keepdims=True)


def _group_ln(y):
    yn, rs = [], []
    for g in range(D // HD):
        blk = y[:, g * HD:(g + 1) * HD]
        xc = blk - _mean_lanes(blk)
        r = lax.rsqrt(_mean_lanes(xc * xc) + EPS)
        yn.append(xc * r)
        rs.append(jnp.broadcast_to(r, blk.shape))
    return jnp.concatenate(yn, axis=1), jnp.concatenate(rs, axis=1)


def _group_ln_bwd(dyn, yn, rstd):
    out = []
    for g in range(D // HD):
        sl = slice(g * HD, (g + 1) * HD)
        d, n = dyn[:, sl], yn[:, sl]
        out.append(rstd[:, sl] * (d - _mean_lanes(d) - n * _mean_lanes(d * n)))
    return jnp.concatenate(out, axis=1)


def _head_means(x, hb, fn=lambda m: m):
    return jnp.concatenate([jnp.broadcast_to(fn(_mean_lanes(x[:, hh * HD:(hh + 1) * HD])), (x.shape[0], HD))
                            for hh in range(hb)], axis=1)


def _head_rsqrt_mean(x, hb):
    return _head_means(x, hb, lambda m: lax.rsqrt(m + EPS))


def _softmax_row0(lbl):
    m = jnp.max(lbl, axis=0, keepdims=True)
    e = jnp.exp(lbl - m)
    return e[0:1, :] / jnp.sum(e, axis=0, keepdims=True)


def _hosted_gather(phases, step, at, shards, axes, ins, outs, bufs, sems):
    gather = _Gather([s.shape for s in shards], axes, ins, outs, bufs, sems)
    for phase in phases:
        pl.when(step == at[phase])(getattr(gather, phase))


def _rmsnorm(x, ln_g, tT):
    T = x.shape[0]

    def body(x_ref, g_ref, u_ref):
        xv = x_ref[...]
        r = lax.rsqrt(_mean_lanes(xv * xv) + EPS)
        u_ref[...] = (xv * r * g_ref[...]).astype(_MXU)

    tok = pl.BlockSpec((tT, D), lambda i: (i, 0))
    return pl.pallas_call(
        body, name="rmsnorm", grid=(T // tT,), in_specs=[tok, pl.BlockSpec((1, D), lambda i: (0, 0))], out_specs=tok,
        out_shape=jax.ShapeDtypeStruct((T, D), _MXU),
        compiler_params=_cparams(_nbytes((tT, D), F32) * 2 + _nbytes((tT, D), _MXU), 1),
    )(x, ln_g)


W_BLK = 256
SHARD_COLS = NPART * D // N_CHIPS
BLK_PER_SHARD = SHARD_COLS // W_BLK
BLK_PER_PART = D // W_BLK


class _PartGather:
    def __init__(self, shard, full, buf, sems):
        self.shard, self.full, self.buf = shard, full, buf
        self.ici_send, self.ici_recv, self.d2d_send, self.d2d_recv, self.loc = sems
        self.x, self.y, self.c = _place()
        self.me = 2 * self.x + self.y

    @staticmethod
    def scratch(dtype):
        k = 3 * BLK_PER_SHARD
        return [pltpu.VMEM((D, SHARD_COLS), dtype)] + [pltpu.SemaphoreType.DMA((k,))] * 4 + [pltpu.SemaphoreType.DMA((2,))]

    def _rows(self, hc):
        return pl.ds(pl.multiple_of(hc * (D // 2), 16), D // 2)

    def _region(self, b, hc):
        return self.full.at[self._rows(hc), pl.ds(pl.multiple_of(b * W_BLK, W_BLK), W_BLK)]

    def _rel(self, o):
        return _flip(self.x, o // 2) + 2 * _flip(self.y, o % 2) - 1

    def _ici(self, q, jrel, k, b, hc):
        dx, dy = CHIP_MOVES[jrel] if isinstance(jrel, int) else (0, 0)
        return pltpu.make_async_remote_copy(
            src_ref=self.shard.at[self._rows(hc), pl.ds(q * W_BLK, W_BLK)], dst_ref=self._region(b, hc),
            send_sem=self.ici_send.at[k], recv_sem=self.ici_recv.at[k],
            device_id=(_flip(self.x, dx), _flip(self.y, dy), self.c), device_id_type=MESH_ID)

    def _d2d(self, k, b, hc):
        return pltpu.make_async_remote_copy(
            src_ref=self._region(b, hc), dst_ref=self._region(b, hc),
            send_sem=self.d2d_send.at[k], recv_sem=self.d2d_recv.at[k],
            device_id=(self.x, self.y, 1 - self.c), device_id_type=MESH_ID)

    def start(self):
        own = self.full.at[:, pl.ds(pl.multiple_of(self.me * SHARD_COLS, HD), SHARD_COLS)]
        loc = _Bounce(self.shard, self.buf, own, self.loc.at[0], self.loc.at[1])
        loc.start()
        for q in range(BLK_PER_SHARD):
            for jrel in range(3):
                self._ici(q, jrel, BLK_PER_SHARD * jrel + q, BLK_PER_SHARD * self.me + q, self.c).start()
        loc.turn()
        loc.wait()

    def _foreign_blocks(self, j, visit):
        for b in range(BLK_PER_PART * j, BLK_PER_PART * (j + 1)):
            o, q = divmod(b, BLK_PER_SHARD)
            pl.when(self.me != o)(functools.partial(visit, BLK_PER_SHARD * self._rel(o) + q, b, q))

    def landed(self, j):
        def visit(k, b, q):
            self._ici(q, None, k, b, self.c).wait_recv()
            self._d2d(k, b, self.c).start()
        self._foreign_blocks(j, visit)

    def ready(self, j):
        self._foreign_blocks(j, lambda k, b, q: self._d2d(k, b, 1 - self.c).wait_recv())

    def finish(self):
        for q in range(BLK_PER_SHARD):
            for jrel in range(3):
                self._ici(q, jrel, BLK_PER_SHARD * jrel + q, BLK_PER_SHARD * self.me + q, self.c).wait_send()
        for j in range(NPART):
            self._foreign_blocks(j, lambda k, b, q: self._d2d(k, b, self.c).wait_send())


def _inproj_fwd(u, w_shard, shards, axes, tT):
    T = u.shape[0]
    n = len(shards)
    nI = T // tT
    at = _hosted_steps(NPART * nI)

    def body(u_ref, wshard_ref, *rest):
        ins, (z_ref, wfull_ref), outs = rest[:n], rest[n:n + 2], rest[n + 2:2 * n + 2]
        wbuf, part_sems, w_scr, wload = rest[2 * n + 2], rest[2 * n + 3:2 * n + 8], rest[2 * n + 8], rest[2 * n + 9]
        bufs, sems = rest[2 * n + 10:3 * n + 10], rest[3 * n + 10:]
        j, i = pl.program_id(0), pl.program_id(1)
        step = j * nI + i
        host = functools.partial(_hosted_gather, step=step, at=at, shards=shards, axes=axes,
                                 ins=ins, outs=outs, bufs=bufs, sems=sems)
        parts = _PartGather(wshard_ref, wfull_ref, wbuf, part_sems)

        def load(jn):
            slot = jn % 2
            cols = pl.ds(pl.multiple_of(jn * D, D), D)
            return pltpu.make_async_copy(wfull_ref.at[:, cols], w_scr.at[slot], wload.at[slot])

        @pl.when(step == 0)
        def _():
            parts.start()
            parts.landed(0)
            parts.landed(1)
            parts.ready(0)
            load(0).start()
        host(("start", "turn", "forward"))
        pl.when(i == 0)(lambda: load(j).wait())
        z_ref[...] = jnp.dot(u_ref[...], w_scr[j % 2], preferred_element_type=F32)
        for js in range(NPART - 1):
            @pl.when(jnp.logical_and(i == nI - 1, j == js))
            def _(js=js):
                if js + 2 < NPART:
                    parts.landed(js + 2)
                parts.ready(js + 1)
                load(js + 1).start()
        host(("finish",))
        pl.when(step == at["finish"])(parts.finish)

    blk = (_nbytes((tT, D), F32) + _nbytes((tT, D), _MXU) + _nbytes((D, D), _MXU) + _nbytes((D, SHARD_COLS), _MXU) // 2
           + sum(_nbytes(s.shape, s.dtype) for s in shards))
    outs = pl.pallas_call(
        body, name="inproj_fwd", grid=(NPART, nI),
        in_specs=[pl.BlockSpec((tT, D), lambda j, i: (i, 0))] + [ANY] * (n + 1),
        out_specs=[pl.BlockSpec((tT, D), lambda j, i: (i, j))] + [ANY] * (n + 1),
        out_shape=[jax.ShapeDtypeStruct((T, NPART * D), F32), pltpu.HBM((D, NPART * D), w_shard.dtype)]
        + [pltpu.HBM(fs, s.dtype) for fs, s in zip(_full_shapes(shards, axes), shards)],
        scratch_shapes=_PartGather.scratch(w_shard.dtype)
        + [pltpu.VMEM((2, D, D), w_shard.dtype), pltpu.SemaphoreType.DMA((2,))] + _Gather.scratch(shards),
        compiler_params=_cparams(blk, 2),
    )(u, *_in_hbm([w_shard] + list(shards)))
    return outs[0], outs[1], outs[2:]


def _shifted_windows(ext, first, visit):
    n = ext.shape[0]
    for m in range(first, first + CONV_K):
        visit(m, (ext if m == 0 else pltpu.roll(ext, n - m, axis=0))[0:n - HALO, :])


def _conv_fwd(z, conv_w, conv_b, cn_g, cn_b, w_pw2, b_pw2, tT):
    T = z.shape[0]

    def body(cv_ref, cg_ref, ct_ref, cw_ref, cb_ref, ng_ref, nb_ref, wp_ref, bp_ref, yc_ref, y1_ref, ext):
        @pl.when(pl.program_id(0) == 0)
        def _():
            ext[...] = jnp.zeros_like(ext)
        ext[0:HALO, :] = ext[tT:tT + HALO, :]
        ext[HALO:, :] = cv_ref[...] * _sigmoid(cg_ref[...])
        cw = cw_ref[...]
        acc = [cb_ref[...]]

        def tap(m, win):
            acc[0] = acc[0] + win * cw[m - 2:m - 1, :]
        _shifted_windows(ext[...], 2, tap)
        y1 = acc[0]
        y1_ref[...] = y1
        yn, _ = _group_ln(y1)
        apre = yn * ng_ref[...] + nb_ref[...]
        a = apre * _sigmoid(apre)
        y2 = _dot(a, wp_ref[...]) + bp_ref[...]
        ct = ct_ref[...]
        yc_ref[...] = (y2 * (ct * _sigmoid(ct))).astype(_MXU)

    part = lambda p: pl.BlockSpec((tT, D), lambda i: (i, p))
    row = pl.BlockSpec((1, D), lambda i: (0, 0))
    tok = pl.BlockSpec((tT, D), lambda i: (i, 0))
    blk = 4 * _nbytes((tT, D), F32) + _nbytes((D, D), _MXU) + _nbytes((tT, D), _MXU) + 8 * _nbytes((tT + HALO, D), F32)
    return pl.pallas_call(
        body, name="conv_fwd", grid=(T // tT,),
        in_specs=[part(0), part(1), part(2), pl.BlockSpec((HALO, D), lambda i: (0, 0)), row, row, row,
                  pl.BlockSpec((D, D), lambda i: (0, 0)), row],
        out_specs=[tok, tok],
        out_shape=[jax.ShapeDtypeStruct((T, D), _MXU), jax.ShapeDtypeStruct((T, D), F32)],
        scratch_shapes=[pltpu.VMEM((tT + HALO, D), F32)],
        compiler_params=_cparams(blk, 1),
    )(z, z, z, conv_w, conv_b, cn_g, cn_b, w_pw2, b_pw2)


def _hgrn_gates(lb, hq, hf):
    sq = _sigmoid(hq)
    sg = _sigmoid(hf)
    f = lb + (1.0 - lb) * sg
    return sq, sg, f, hq * sq, (1.0 - lb) * (1.0 - sg), jnp.log(f)


def _chunk_decays(lf, q, k):
    r = lax.broadcasted_iota(jnp.int32, (CHUNK, CHUNK), 0)
    c = lax.broadcasted_iota(jnp.int32, (CHUNK, CHUNK), 1)
    b = _tri_dot((r >= c).astype(BF16), lf)
    bm = b[CHUNK // 2 - 1:CHUNK // 2, :]
    bl = b[CHUNK - 1:CHUNK, :]
    eb = jnp.exp(b)
    eqm = jnp.exp(b - bm)
    ekm = jnp.exp(bm - b)
    ekd = jnp.exp(bl - b)
    return dict(causal=r >= c, eb=eb, eqm=eqm, ekm=ekm, ekd=ekd, ebl=jnp.exp(bl),
                qd=q * eb, qm=q * eqm, km=k * ekm, kd=k * ekd)


def _hgrn_fwd(z, lb_logits, onorm_g, tT, hb):
    T = z.shape[0]
    nc = tT // CHUNK
    w = hb * HD

    def body(lbl_ref, og_ref, hq_ref, hf_ref, hi_ref, hg_ref, o_ref, yh_ref, sc_ref, st):
        @pl.when(pl.program_id(1) == 0)
        def _():
            st[...] = jnp.zeros_like(st)
        lb_all = _softmax_row0(lbl_ref[...])
        og_all = og_ref[...]

        def chunk(c, carry):
            sl = pl.ds(pl.multiple_of(c * CHUNK, CHUNK), CHUNK)
            lanes = [slice(hh * HD, (hh + 1) * HD) for hh in range(hb)]
            heads = lambda fn: [fn(hh, ln) for hh, ln in enumerate(lanes)]
            hg, v = hg_ref[sl, :], hi_ref[sl, :]
            _, _, _, q, k, lf = _hgrn_gates(lb_all, hq_ref[sl, :], hf_ref[sl, :])
            dc = _chunk_decays(lf, q, k)
            s_t = heads(lambda hh, ln: st[hh])
            a = heads(lambda hh, ln: jnp.where(dc["causal"], _dot_nt(dc["qm"][:, ln], dc["km"][:, ln]), 0.0))
            o_inter = heads(lambda hh, ln: _dot_nt(dc["qd"][:, ln], s_t[hh]))
            kv = heads(lambda hh, ln: _dot_tn(v[:, ln], dc["kd"][:, ln]))
            o_intra = heads(lambda hh, ln: _dot(a[hh], v[:, ln]))
            for hh, ln in enumerate(lanes):
                sc_ref[hh, c] = s_t[hh]
                st[hh] = s_t[hh] * dc["ebl"][:, ln] + kv[hh]
            o = jnp.concatenate([o_inter[hh] + o_intra[hh] for hh in range(hb)], axis=1)
            o_ref[sl, :] = o
            n = o * _head_rsqrt_mean(o * o, hb)
            yh_ref[sl, :] = ((n * og_all) * (hg * _sigmoid(hg))).astype(_MXU)
            return carry

        lax.fori_loop(0, nc, chunk, 0, unroll=8)

    zpart = lambda p: pl.BlockSpec((tT, w), lambda h, i: (i, p * (HEADS // hb) + h))
    blk = 6 * _nbytes((tT, w), F32) + _nbytes((hb, nc, HD, HD), F32)
    return pl.pallas_call(
        body, name="hgrn_fwd", grid=(HEADS // hb, T // tT),
        in_specs=[pl.BlockSpec((2, w), lambda h, i: (0, h)), pl.BlockSpec((1, w), lambda h, i: (0, h)),
                  zpart(3), zpart(4), zpart(5), zpart(6)],
        out_specs=[pl.BlockSpec((tT, w), lambda h, i: (i, h)), pl.BlockSpec((tT, w), lambda h, i: (i, h)),
                   pl.BlockSpec((hb, nc, HD, HD), lambda h, i: (h, i, 0, 0))],
        out_shape=[jax.ShapeDtypeStruct((T, D), F32), jax.ShapeDtypeStruct((T, D), _MXU),
                   jax.ShapeDtypeStruct((HEADS, T // CHUNK, HD, HD), F32)],
        scratch_shapes=[pltpu.VMEM((hb, HD, HD), F32)],
        compiler_params=_cparams(blk, 2),
    )(lb_logits, onorm_g, z, z, z, z)


def _hgrn_bwd(z, lb_logits, onorm_g, o_raw, dyh, s_chunks, partials, tT, hb):
    T = z.shape[0]
    nc = tT // CHUNK
    nI = T // tT
    w = hb * HD
    n = len(partials)
    at = _hosted_steps((HEADS // hb) * nI)

    def body(lbl_ref, og_ref, hq_ref, hf_ref, hi_ref, hg_ref, o_ref, dy_ref, sc_ref, *rest):
        (dz_ref, vec_ref), dst = rest[n:n + 2], rest[2 * n + 2]
        exchange = _ChipExchange(n, rest[:n], rest[n + 2:2 * n + 2], rest[2 * n + 3:3 * n + 3], rest[3 * n + 3:])
        step = pl.program_id(0) * nI + pl.program_id(1)
        pl.when(step == at["start"])(exchange.start)
        pl.when(step == at["turn"])(exchange.turn)

        @pl.when(pl.program_id(1) == 0)
        def _():
            dst[...] = jnp.zeros_like(dst)
            vec_ref[...] = jnp.zeros_like(vec_ref)
        lb_all = _softmax_row0(lbl_ref[...])
        og_all = og_ref[...]
        last_row = lax.broadcasted_iota(jnp.int32, (CHUNK, w), 0) == CHUNK - 1
        r64 = lax.broadcasted_iota(jnp.int32, (CHUNK, CHUNK), 0)
        c64 = lax.broadcasted_iota(jnp.int32, (CHUNK, CHUNK), 1)
        upper = (c64 >= r64).astype(BF16)
        lanes = [slice(hh * HD, (hh + 1) * HD) for hh in range(hb)]
        heads = lambda fn: [fn(hh, ln) for hh, ln in enumerate(lanes)]
        wide = lambda parts: jnp.concatenate(parts, axis=1)

        def chunk(cc, carry):
            c = nc - 1 - cc
            sl = pl.ds(pl.multiple_of(c * CHUNK, CHUNK), CHUNK)
            hq, hg, v = hq_ref[sl, :], hg_ref[sl, :], hi_ref[sl, :]
            sq, sg, f, q, k, lf = _hgrn_gates(lb_all, hq, hf_ref[sl, :])
            dc = _chunk_decays(lf, q, k)
            s_t = heads(lambda hh, ln: sc_ref[hh, c])
            ds_t = heads(lambda hh, ln: dst[hh])
            o, dy = o_ref[sl, :], dy_ref[sl, :]
            r = _head_rsqrt_mean(o * o, hb)
            n = o * r
            sgg = _sigmoid(hg)
            silu_g = hg * sgg
            dhg = dy * (n * og_all) * (sgg * (1.0 + hg * (1.0 - sgg)))
            dn = dy * og_all * silu_g
            g_og = _sum_rows(dy * n * silu_g)
            do = r * (dn - n * _head_means(dn * n, hb))
            a = heads(lambda hh, ln: jnp.where(dc["causal"], _dot_nt(dc["qm"][:, ln], dc["km"][:, ln]), 0.0))
            dam = heads(lambda hh, ln: jnp.where(dc["causal"], _dot_nt(do[:, ln], v[:, ln]), 0.0))
            dqd = wide(heads(lambda hh, ln: _dot(do[:, ln], s_t[hh])))
            dkd = wide(heads(lambda hh, ln: _dot(v[:, ln], ds_t[hh])))
            dv_inter = heads(lambda hh, ln: _dot_nt(dc["kd"][:, ln], ds_t[hh]))
            dqs = heads(lambda hh, ln: _dot_tn(do[:, ln], dc["qd"][:, ln]))
            dv = wide(heads(lambda hh, ln: _dot_tn(a[hh], do[:, ln]) + dv_inter[hh]))
            dam2 = [_split2(t) for t in dam]
            km2, qm2 = _split2(dc["km"]), _split2(dc["qm"])
            dqm = wide(heads(lambda hh, ln: _dot3(((1,), (0,)), dam2[hh], (km2[0][:, ln], km2[1][:, ln]))))
            dkm = wide(heads(lambda hh, ln: _dot3(((0,), (0,)), dam2[hh], (qm2[0][:, ln], qm2[1][:, ln]))))
            debl = wide(heads(lambda hh, ln: _sum_rows(ds_t[hh] * s_t[hh])))
            for hh, ln in enumerate(lanes):
                dst[hh] = ds_t[hh] * dc["ebl"][:, ln] + dqs[hh]
            dq = dqd * dc["eb"] + dqm * dc["eqm"]
            dk = dkm * dc["ekm"] + dkd * dc["ekd"]
            dbl = _sum_rows(dkd * dc["kd"]) + debl * dc["ebl"]
            db = dq * q - dk * k + jnp.where(last_row, dbl, 0.0)
            dlf = _tri_dot(upper, db)
            dfk = dlf / f - dk
            dz_ref[0, sl, :] = (dq * (sq * (1.0 + hq * (1.0 - sq)))).astype(_MXU)
            dz_ref[1, sl, :] = (dfk * ((1.0 - lb_all) * sg * (1.0 - sg))).astype(_MXU)
            dz_ref[2, sl, :] = dv.astype(_MXU)
            dz_ref[3, sl, :] = dhg.astype(_MXU)
            vec_ref[0:1, :] += g_og
            vec_ref[1:2, :] += _sum_rows(dfk * (1.0 - sg))
            return carry

        lax.fori_loop(0, nc, chunk, 0, unroll=4)
        pl.when(step == at["finish"])(exchange.finish)

    zpart = lambda p: pl.BlockSpec((tT, w), lambda h, i: (nI - 1 - i, p * (HEADS // hb) + h))
    act = pl.BlockSpec((tT, w), lambda h, i: (nI - 1 - i, h))
    blk = (6 * _nbytes((tT, w), F32) + _nbytes((hb, nc, HD, HD), F32) + 4 * _nbytes((tT, w), _MXU)
           + _ChipExchange.scratch_bytes(partials))
    outs = pl.pallas_call(
        body, name="hgrn_bwd", grid=(HEADS // hb, nI),
        in_specs=[pl.BlockSpec((2, w), lambda h, i: (0, h)), pl.BlockSpec((1, w), lambda h, i: (0, h)),
                  zpart(3), zpart(4), zpart(5), zpart(6), act, act,
                  pl.BlockSpec((hb, nc, HD, HD), lambda h, i: (h, nI - 1 - i, 0, 0))] + [ANY] * n,
        out_specs=[pl.BlockSpec((4, tT, w), lambda h, i: (0, nI - 1 - i, h)),
                   pl.BlockSpec((8, w), lambda h, i: (0, h))] + [ANY] * n,
        out_shape=[jax.ShapeDtypeStruct((4, T, D), _MXU), jax.ShapeDtypeStruct((8, D), F32)]
        + [pltpu.HBM(p.shape, p.dtype) for p in partials],
        scratch_shapes=[pltpu.VMEM((hb, HD, HD), F32)] + _ChipExchange.scratch(partials),
        compiler_params=_cparams(blk, 2),
    )(lb_logits, onorm_g, z, z, z, z, o_raw, dyh, s_chunks, *_in_hbm(partials))
    return outs[0], outs[1], outs[2:]


def _tail(x, yc, yh, p, target, w_out, w_pg, w_pp, pe_g, fin_g, tT):
    T = x.shape[0]

    def body(x_ref, yc_ref, yh_ref, p_ref, t_ref, wo_ref, wg_ref, wp_ref, pg_ref, fg_ref,
             dyc_ref, dyh_ref, dh_ref, n2_ref, ds_ref, dpe_ref, dhb_ref, pb_ref, vec_ref):
        @pl.when(pl.program_id(0) == 0)
        def _():
            vec_ref[...] = jnp.zeros_like(vec_ref)
        wo_c, wo_h = wo_ref[0:D, :], wo_ref[D:2 * D, :]
        h = x_ref[...] + _dot(yc_ref[...], wo_c) + _dot(yh_ref[...], wo_h)
        pb = p_ref[...].astype(_MXU)
        pe = _dot(pb, wp_ref[...])
        r2 = lax.rsqrt(_mean_lanes(h * h) + EPS)
        hn = h * r2
        n2 = (hn * pg_ref[...]).astype(_MXU)
        gate = _sigmoid(_dot(n2, wg_ref[...]))
        h2 = h + gate * pe
        r3 = lax.rsqrt(_mean_lanes(h2 * h2) + EPS)
        h2n = h2 * r3
        err = h2n * fg_ref[...] - t_ref[...]
        vec_ref[ROW_LOSS:ROW_LOSS + 1, :] += 0.5 * jnp.sum(_mean_lanes(err * err))
        dout = err * (1.0 / D)
        vec_ref[0:1, :] += _sum_rows(dout * h2n)
        dn3 = dout * fg_ref[...]
        dh2 = r3 * (dn3 - h2n * _mean_lanes(dn3 * h2n))
        ds = (dh2 * pe * gate * (1.0 - gate)).astype(_MXU)
        dn2 = _dot_nt(ds, wg_ref[...])
        vec_ref[1:2, :] += _sum_rows(dn2 * hn)
        dnn = dn2 * pg_ref[...]
        dh = dh2 + r2 * (dnn - hn * _mean_lanes(dnn * hn))
        dhb = dh.astype(_MXU)
        dyc_ref[...] = _dot_nt(dhb, wo_c)
        dyh_ref[...] = _dot_nt(dhb, wo_h)
        dh_ref[...] = dh
        n2_ref[...] = n2
        ds_ref[...] = ds
        dpe_ref[...] = (dh2 * gate).astype(_MXU)
        dhb_ref[...] = dhb
        pb_ref[...] = pb

    tok = lambda w: pl.BlockSpec((tT, w), lambda i: (i, 0))
    full = lambda r, c: pl.BlockSpec((r, c), lambda i: (0, 0))
    tokshape = lambda w, dt: jax.ShapeDtypeStruct((T, w), dt)
    blk = (5 * _nbytes((tT, D), F32) + 7 * _nbytes((tT, D), _MXU) + _nbytes((4 * D + PLE, D), _MXU)
           + 12 * _nbytes((tT, D), F32))
    return pl.pallas_call(
        body, name="tail_fwd_bwd", grid=(T // tT,),
        in_specs=[tok(D), tok(D), tok(D), tok(PLE), tok(D), full(2 * D, D), full(D, D), full(PLE, D), full(1, D), full(1, D)],
        out_specs=[tok(D), tok(D), tok(D), tok(D), tok(D), tok(D), tok(D), tok(PLE), full(8, D)],
        out_shape=[tokshape(D, F32), tokshape(D, F32), tokshape(D, F32), tokshape(D, _MXU), tokshape(D, _MXU),
                   tokshape(D, _MXU), tokshape(D, _MXU), tokshape(PLE, _MXU),
                   jax.ShapeDtypeStruct((8, D), F32)],
        compiler_params=_cparams(blk, 1),
    )(x, yc, yh, p, target, w_out, w_pg, w_pp, pe_g, fin_g)


def _conv_bwd(z, y1, dyc, conv_w, cn_g, cn_b, w_pw2, b_pw2, tT):
    T = z.shape[0]
    nI = T // tT
    hb = tT // HALO

    def body(cv_ref, cg_ref, ct_ref, hv_ref, hg_ref, y1_ref, dyc_ref, cw_ref, ng_ref, nb_ref, wp_ref, bp_ref,
             dz_ref, a_ref, dy2_ref, vec_ref, gcw_ref, ext, ext2, gpart):
        i = pl.program_id(0)

        @pl.when(i == 0)
        def _():
            ext2[...] = jnp.zeros_like(ext2)
            gpart[...] = jnp.zeros_like(gpart)
            vec_ref[...] = jnp.zeros_like(vec_ref)
        cv, cg, ct = cv_ref[...], cg_ref[...], ct_ref[...]
        sg = _sigmoid(cg)
        has_hist = (i < nI - 1).astype(F32)
        ext[0:HALO, :] = hv_ref[...] * _sigmoid(hg_ref[...]) * has_hist
        ext[HALO:, :] = cv * sg
        yn, rstd = _group_ln(y1_ref[...])
        apre = yn * ng_ref[...] + nb_ref[...]
        sa = _sigmoid(apre)
        a = (apre * sa).astype(_MXU)
        y2 = _dot(a, wp_ref[...]) + bp_ref[...]
        st = _sigmoid(ct)
        dyc_v = dyc_ref[...]
        dy2 = dyc_v * (ct * st)
        dy2b = dy2.astype(_MXU)
        da = _dot_nt(dy2b, wp_ref[...])
        dapre = da * (sa * (1.0 + apre * (1.0 - sa)))
        dy1 = _group_ln_bwd(dapre * ng_ref[...], yn, rstd)
        vec_ref[0:1, :] += _sum_rows(dy1)
        vec_ref[1:2, :] += _sum_rows(dapre * yn)
        vec_ref[2:3, :] += _sum_rows(dapre)
        vec_ref[3:4, :] += _sum_rows(dy2)
        dz_ref[2] = (dyc_v * y2 * (st * (1.0 + ct * (1.0 - st)))).astype(_MXU)
        a_ref[...] = a
        dy2_ref[...] = dy2b
        ext2[tT:tT + HALO, :] = ext2[0:HALO, :]
        ext2[0:tT, :] = dy1
        def grad_tap(m, win):
            p = dy1 * win
            part = p[0:8, :]
            for q in range(1, tT // 8):
                part = part + p[8 * q:8 * q + 8, :]
            gpart[m - 2] += part
        _shifted_windows(ext[...], 2, grad_tap)
        cw = cw_ref[...]
        acc = [None]

        def dv_tap(m, win):
            term = win * cw[CONV_K - 1 - m:CONV_K - m, :]
            acc[0] = term if acc[0] is None else acc[0] + term
        _shifted_windows(ext2[...], 0, dv_tap)
        dv = acc[0]
        dz_ref[0] = (dv * sg).astype(_MXU)
        dz_ref[1] = (dv * cv * sg * (1.0 - sg)).astype(_MXU)

        @pl.when(i == nI - 1)
        def _():
            gcw_ref[...] = jnp.sum(gpart[...], axis=1)

    part = lambda p: pl.BlockSpec((tT, D), lambda i: (nI - 1 - i, p))
    hist = lambda p: pl.BlockSpec((HALO, D), lambda i: (jnp.maximum((nI - 1 - i) * hb - 1, 0), p))
    tok = pl.BlockSpec((tT, D), lambda i: (nI - 1 - i, 0))
    row = pl.BlockSpec((1, D), lambda i: (0, 0))
    blk = (5 * _nbytes((tT, D), F32) + _nbytes((D, D), _MXU) + 5 * _nbytes((tT, D), _MXU)
           + 10 * _nbytes((tT + HALO, D), F32))
    return pl.pallas_call(
        body, name="conv_bwd", grid=(nI,),
        in_specs=[part(0), part(1), part(2), hist(0), hist(1), tok, tok, pl.BlockSpec((HALO, D), lambda i: (0, 0)),
                  row, row, pl.BlockSpec((D, D), lambda i: (0, 0)), row],
        out_specs=[pl.BlockSpec((3, tT, D), lambda i: (0, nI - 1 - i, 0)), tok, tok,
                   pl.BlockSpec((8, D), lambda i: (0, 0)), pl.BlockSpec((HALO, D), lambda i: (0, 0))],
        out_shape=[jax.ShapeDtypeStruct((3, T, D), _MXU), jax.ShapeDtypeStruct((T, D), _MXU),
                   jax.ShapeDtypeStruct((T, D), _MXU), jax.ShapeDtypeStruct((8, D), F32),
                   jax.ShapeDtypeStruct((HALO, D), F32)],
        scratch_shapes=[pltpu.VMEM((tT + HALO, D), F32), pltpu.VMEM((tT + HALO, D), F32), pltpu.VMEM((HALO, 8, D), F32)],
        compiler_params=_cparams(blk, 1),
    )(z, z, z, z, z, y1, dyc, conv_w, cn_g, cn_b, w_pw2, b_pw2)


def _inproj_bwd_x(x, ln_g, dzc, dzh, w_in, dh, partials, tT):
    T = x.shape[0]
    n = len(partials)
    at = _hosted_steps((T // tT) * NPART)

    def body(x_ref, g_ref, dzc_ref, dzh_ref, w_ref, dh_ref, *rest):
        (gx_ref, vec_ref), du = rest[n:n + 2], rest[2 * n + 2]
        exchange = _ChipExchange(n, rest[:n], rest[n + 2:2 * n + 2], rest[2 * n + 3:3 * n + 3], rest[3 * n + 3:])
        i, j = pl.program_id(0), pl.program_id(1)
        step = i * NPART + j
        pl.when(step == at["start"])(exchange.start)
        pl.when(step == at["turn"])(exchange.turn)

        @pl.when(j == 0)
        def _():
            du[...] = jnp.zeros_like(du)

        @pl.when(jnp.logical_and(i == 0, j == 0))
        def _():
            vec_ref[...] = jnp.zeros_like(vec_ref)

        @pl.when(j < 3)
        def _():
            du[...] += _dot_nt(dzc_ref[0], w_ref[...])

        @pl.when(j >= 3)
        def _():
            du[...] += _dot_nt(dzh_ref[0], w_ref[...])

        @pl.when(j == NPART - 1)
        def _():
            xv = x_ref[...]
            r = lax.rsqrt(_mean_lanes(xv * xv) + EPS)
            xn = xv * r
            duv = du[...]
            vec_ref[0:1, :] += _sum_rows(duv * xn)
            dun = duv * g_ref[...]
            gx_ref[...] = dh_ref[...] + r * (dun - xn * _mean_lanes(dun * xn))
        pl.when(step == at["finish"])(exchange.finish)

    tok = pl.BlockSpec((tT, D), lambda i, j: (i, 0))
    blk = (3 * _nbytes((tT, D), F32) + 2 * _nbytes((tT, D), _MXU) + _nbytes((D, D), _MXU) + 4 * _nbytes((tT, D), F32)
           + _ChipExchange.scratch_bytes(partials))
    outs = pl.pallas_call(
        body, name="inproj_bwd_x", grid=(T // tT, NPART),
        in_specs=[tok, pl.BlockSpec((1, D), lambda i, j: (0, 0)),
                  pl.BlockSpec((1, tT, D), lambda i, j: (jnp.minimum(j, 2), i, 0)),
                  pl.BlockSpec((1, tT, D), lambda i, j: (jnp.maximum(j - 3, 0), i, 0)),
                  pl.BlockSpec((D, D), lambda i, j: (0, j)), tok] + [ANY] * n,
        out_specs=[tok, pl.BlockSpec((8, D), lambda i, j: (0, 0))] + [ANY] * n,
        out_shape=[jax.ShapeDtypeStruct((T, D), F32), jax.ShapeDtypeStruct((8, D), F32)]
        + [pltpu.HBM(p.shape, p.dtype) for p in partials],
        scratch_shapes=[pltpu.VMEM((tT, D), F32)] + _ChipExchange.scratch(partials),
        compiler_params=_cparams(blk, 2),
    )(x, ln_g, dzc, dzh, w_in, dh, *_in_hbm(partials))
    return outs[0], outs[1], outs[2:]


def _inproj_bwd_w(u, dzc, dzh, tk):
    T = u.shape[0]
    nK = T // tk

    def body(u_ref, dzc_ref, dzh_ref, gw_ref):
        j, k = pl.program_id(0), pl.program_id(1)

        @pl.when(k == 0)
        def _():
            gw_ref[...] = jnp.zeros_like(gw_ref)

        @pl.when(j < 3)
        def _():
            gw_ref[...] += _dot_tn(u_ref[...], dzc_ref[0])

        @pl.when(j >= 3)
        def _():
            gw_ref[...] += _dot_tn(u_ref[...], dzh_ref[0])

    blk = 3 * _nbytes((tk, D), _MXU) + 2 * _nbytes((D, D), F32)
    return pl.pallas_call(
        body, name="inproj_bwd_w", grid=(NPART, nK),
        in_specs=[pl.BlockSpec((tk, D), lambda j, k: (k, 0)),
                  pl.BlockSpec((1, tk, D), lambda j, k: (jnp.minimum(j, 2), jnp.where(j < 3, k, nK - 1), 0)),
                  pl.BlockSpec((1, tk, D), lambda j, k: (jnp.maximum(j - 3, 0), jnp.where(j < 3, 0, k), 0))],
        out_specs=pl.BlockSpec((D, D), lambda j, k: (0, j)),
        out_shape=jax.ShapeDtypeStruct((D, NPART * D), F32),
        compiler_params=_cparams(blk, 2),
    )(u, dzc, dzh)


def _tn_matmul(a, b, tk, name):
    T, M = a.shape
    N = b.shape[1]

    def body(a_ref, b_ref, o_ref):
        @pl.when(pl.program_id(0) == 0)
        def _():
            o_ref[...] = jnp.zeros_like(o_ref)
        o_ref[...] += _dot_tn(a_ref[...], b_ref[...])

    blk = _nbytes((tk, M), _MXU) + _nbytes((tk, N), _MXU) + 2 * _nbytes((M, N), F32)
    return pl.pallas_call(
        body, name=name, grid=(T // tk,),
        in_specs=[pl.BlockSpec((tk, M), lambda k: (k, 0)), pl.BlockSpec((tk, N), lambda k: (k, 0))],
        out_specs=pl.BlockSpec((M, N), lambda k: (0, 0)),
        out_shape=pltpu.HBM((M, N), F32),
        compiler_params=_cparams(blk, 1),
    )(a, b)


def _place():
    return lax.axis_index("x"), lax.axis_index("y"), lax.axis_index("c")


def _flip(v, d):
    return 1 - v if d else v


CHIP_MOVES = [(1, 0), (0, 1), (1, 1)]
DEV_MOVES = [(dx, dy, dc) for dx in (0, 1) for dy in (0, 1) for dc in (0, 1)][1:]


def _shard_slice(ref, axis, size, s):
    start = pl.multiple_of(s * size, size)
    return ref.at[pl.ds(start, size), :] if axis == 0 else ref.at[:, pl.ds(start, size)]


class _Bounce:
    def __init__(self, src, buf, dst, sem_in, sem_out):
        self.load = pltpu.make_async_copy(src, buf, sem_in)
        self.store = pltpu.make_async_copy(buf, dst, sem_out)

    def start(self):
        self.load.start()

    def turn(self):
        self.load.wait()
        self.store.start()

    def wait(self):
        self.store.wait()


def _comm_params(scratch_bytes):
    return pltpu.CompilerParams(vmem_limit_bytes=int(min(V7X_VMEM_LIMIT, scratch_bytes + (8 << 20))))


class _Gather:
    def __init__(self, shapes, axes, ins, outs, bufs, sems):
        self.shapes, self.axes, self.ins, self.outs, self.bufs = shapes, axes, ins, outs, bufs
        self.ici_send, self.ici_recv, self.d2d_send, self.d2d_recv, self.in_sems, self.out_sems = sems
        self.x, self.y, self.c = _place()
        self.me = 2 * self.x + self.y
        self.pairs = [(k, j) for k in range(len(shapes)) for j in range(3)]

    @staticmethod
    def scratch(shards):
        n = len(shards)
        return ([pltpu.VMEM(s.shape, s.dtype) for s in shards]
                + [pltpu.SemaphoreType.DMA((3 * n,))] * 4 + [pltpu.SemaphoreType.DMA((n,))] * 2)

    def _own_half(self, k, hc):
        half = self.shapes[k][0] // 2
        return self.ins[k].at[pl.ds(pl.multiple_of(hc * half, 16), half), :]

    def _region(self, k, who, hc):
        rows, cols = self.shapes[k]
        half = rows // 2
        if self.axes[k] == 0:
            return self.outs[k].at[pl.ds(pl.multiple_of(who * rows + hc * half, 16), half), :]
        return self.outs[k].at[pl.ds(pl.multiple_of(hc * half, 16), half), pl.ds(pl.multiple_of(who * cols, HD), cols)]

    def _peer(self, j):
        return 2 * _flip(self.x, CHIP_MOVES[j][0]) + _flip(self.y, CHIP_MOVES[j][1])

    def _ici(self, k, j, who, hc):
        dx, dy = CHIP_MOVES[j]
        return pltpu.make_async_remote_copy(
            src_ref=self._own_half(k, hc), dst_ref=self._region(k, who, hc),
            send_sem=self.ici_send.at[3 * k + j], recv_sem=self.ici_recv.at[3 * k + j],
            device_id=(_flip(self.x, dx), _flip(self.y, dy), self.c), device_id_type=MESH_ID)

    def _d2d(self, k, j, who, hc):
        return pltpu.make_async_remote_copy(
            src_ref=self._region(k, who, hc), dst_ref=self._region(k, who, hc),
            send_sem=self.d2d_send.at[3 * k + j], recv_sem=self.d2d_recv.at[3 * k + j],
            device_id=(self.x, self.y, 1 - self.c), device_id_type=MESH_ID)

    def _local(self, k):
        size = self.shapes[k][self.axes[k]]
        return _Bounce(self.ins[k], self.bufs[k], _shard_slice(self.outs[k], self.axes[k], size, self.me),
                       self.in_sems.at[k], self.out_sems.at[k])

    def start(self):
        for k in range(len(self.shapes)):
            self._local(k).start()
        for k, j in self.pairs:
            self._ici(k, j, self.me, self.c).start()

    def turn(self):
        for k in range(len(self.shapes)):
            self._local(k).turn()

    def forward(self):
        for k, j in self.pairs:
            self._ici(k, j, self._peer(j), self.c).wait_recv()
            self._d2d(k, j, self._peer(j), self.c).start()

    def finish(self):
        for k, j in self.pairs:
            self._d2d(k, j, self._peer(j), 1 - self.c).wait_recv()
        for k, j in self.pairs:
            self._ici(k, j, self.me, self.c).wait_send()
            self._d2d(k, j, self._peer(j), self.c).wait_send()
        for k in range(len(self.shapes)):
            self._local(k).wait()


def _full_shapes(shards, axes):
    return [tuple(d * (N_CHIPS if a == ax else 1) for a, d in enumerate(s.shape)) for s, ax in zip(shards, axes)]


class _Slab:
    def __init__(self, arrays, pick, shard_shape):
        self.arrays = arrays
        self.pick = pick
        self.rows, self.cols = shard_shape
        self.half = self.rows // 2


def _pair_exchange(slabs, name):
    n = len(slabs)
    n_in = sum(len(sl.arrays) for sl in slabs)

    def body(*refs):
        ins = refs[:n_in]
        mine, got = refs[n_in:n_in + n], refs[n_in + n:n_in + 2 * n]
        bufs = refs[n_in + 2 * n:n_in + 3 * n]
        send_sems, recv_sems, in_sems, out_sems = refs[n_in + 3 * n:]
        x, y, c = _place()
        started = []
        base = 0
        for k, sl in enumerate(slabs):
            for s in range(N_CHIPS):
                ai, r0, c0 = sl.pick(s)
                src = ins[base + ai]

                def half(hc):
                    return src.at[pl.ds(pl.multiple_of(r0 + hc * sl.half, 8), sl.half), pl.ds(c0, sl.cols)]
                q = N_CHIPS * k + s
                loc = _Bounce(half(c), bufs[k].at[s], mine[k].at[s], in_sems.at[q], out_sems.at[q])
                loc.start()
                cp = pltpu.make_async_remote_copy(
                    src_ref=half(1 - c), dst_ref=got[k].at[s], send_sem=send_sems.at[q], recv_sem=recv_sems.at[q],
                    device_id=(x, y, 1 - c), device_id_type=MESH_ID)
                cp.start()
                started.append((loc, cp))
            base += len(sl.arrays)
        for loc, cp in started:
            loc.turn()
        for loc, cp in started:
            cp.wait_recv()
        for loc, cp in started:
            cp.wait_send()
            loc.wait()

    flat_in = [a for sl in slabs for a in sl.arrays]
    compact = [pltpu.HBM((N_CHIPS, sl.half, sl.cols), F32) for sl in slabs]
    outs = pl.pallas_call(
        body, name=name,
        in_specs=[ANY] * n_in, out_specs=[ANY] * (2 * n), out_shape=compact + compact,
        scratch_shapes=[pltpu.VMEM(s.shape, F32) for s in compact] + [pltpu.SemaphoreType.DMA((N_CHIPS * n,))] * 4,
        compiler_params=_comm_params(sum(_nbytes(s.shape, F32) for s in compact)),
    )(*_in_hbm(flat_in))
    return outs[:n], outs[n:]


class _ChipExchange:
    def __init__(self, n, ins, outs, bufs, sems):
        self.n, self.ins, self.outs, self.bufs = n, ins, outs, bufs
        self.send_sems, self.recv_sems, self.in_sems, self.out_sems = sems
        self.x, self.y, self.c = _place()
        self.me = 2 * self.x + self.y
        self.pairs = [(k, j) for k in range(n) for j in range(3)]

    @staticmethod
    def scratch(partials):
        n = len(partials)
        return ([pltpu.VMEM(p.shape[1:], p.dtype) for p in partials]
                + [pltpu.SemaphoreType.DMA((3 * n,))] * 2 + [pltpu.SemaphoreType.DMA((n,))] * 2)

    @staticmethod
    def scratch_bytes(partials):
        return sum(_nbytes(p.shape[1:], p.dtype) for p in partials)

    def _copy(self, k, j, src_slot, dst_slot):
        px, py = _flip(self.x, CHIP_MOVES[j][0]), _flip(self.y, CHIP_MOVES[j][1])
        return pltpu.make_async_remote_copy(
            src_ref=self.ins[k].at[src_slot], dst_ref=self.outs[k].at[dst_slot],
            send_sem=self.send_sems.at[3 * k + j], recv_sem=self.recv_sems.at[3 * k + j],
            device_id=(px, py, self.c), device_id_type=MESH_ID)

    def _peer(self, j):
        return 2 * _flip(self.x, CHIP_MOVES[j][0]) + _flip(self.y, CHIP_MOVES[j][1])

    def _local(self, k):
        return _Bounce(self.ins[k].at[self.me], self.bufs[k], self.outs[k].at[self.me],
                       self.in_sems.at[k], self.out_sems.at[k])

    def start(self):
        for k in range(self.n):
            self._local(k).start()
        for k, j in self.pairs:
            self._copy(k, j, self._peer(j), self.me).start()

    def turn(self):
        for k in range(self.n):
            self._local(k).turn()

    def finish(self):
        for k, j in self.pairs:
            self._copy(k, j, self.me, self._peer(j)).wait_recv()
        for k, j in self.pairs:
            self._copy(k, j, self._peer(j), self.me).wait_send()
        for k in range(self.n):
            self._local(k).wait()


def _hosted_steps(steps):
    return dict(start=0, turn=steps // 4, forward=steps // 2, finish=steps - 1)


def _pair_share(halves, vec):
    n = len(halves)
    nv = len(DEV_MOVES)

    def body(*refs):
        ins, vec_ref = refs[:n], refs[n]
        outs, vec_out = refs[n + 1:2 * n + 1], refs[2 * n + 1]
        bufs = refs[2 * n + 2:3 * n + 3]
        send_sems, recv_sems, in_sems, out_sems = refs[3 * n + 3:]
        x, y, c = _place()
        dev = 4 * x + 2 * y + c

        def vec_copy(j, slot):
            dx, dy, dc = DEV_MOVES[j]
            return pltpu.make_async_remote_copy(
                src_ref=vec_ref, dst_ref=vec_out.at[slot], send_sem=send_sems.at[n + j], recv_sem=recv_sems.at[n + j],
                device_id=(_flip(x, dx), _flip(y, dy), _flip(c, dc)), device_id_type=MESH_ID)

        def rows(k, hc):
            hr = halves[k].shape[0]
            return outs[k].at[pl.ds(pl.multiple_of(hc * hr, 8), hr), :]

        def share(k, hc):
            return pltpu.make_async_remote_copy(
                src_ref=ins[k], dst_ref=rows(k, hc), send_sem=send_sems.at[k], recv_sem=recv_sems.at[k],
                device_id=(x, y, 1 - c), device_id_type=MESH_ID)

        locs = [_Bounce(vec_ref, bufs[n], vec_out.at[dev], in_sems.at[n], out_sems.at[n])]
        locs += [_Bounce(ins[k], bufs[k], rows(k, c), in_sems.at[k], out_sems.at[k]) for k in range(n)]
        for loc in locs:
            loc.start()
        for j in range(nv):
            vec_copy(j, dev).start()
        for k in range(n):
            share(k, c).start()
        for loc in locs:
            loc.turn()
        for k in range(n):
            share(k, 1 - c).wait_recv()
        for j, (dx, dy, dc) in enumerate(DEV_MOVES):
            vec_copy(j, 4 * _flip(x, dx) + 2 * _flip(y, dy) + _flip(c, dc)).wait_recv()
        for k in range(n):
            share(k, c).wait_send()
        for j in range(nv):
            vec_copy(j, dev).wait_send()
        for loc in locs:
            loc.wait()

    outs = pl.pallas_call(
        body, name="grad_pair_share",
        in_specs=[ANY] * (n + 1), out_specs=[ANY] * (n + 1),
        out_shape=[pltpu.HBM((2 * h.shape[0], h.shape[1]), F32) for h in halves] + [pltpu.HBM((N_DEV,) + vec.shape, F32)],
        scratch_shapes=[pltpu.VMEM(h.shape, F32) for h in halves] + [pltpu.VMEM(vec.shape, F32)]
        + [pltpu.SemaphoreType.DMA((n + nv,))] * 2 + [pltpu.SemaphoreType.DMA((n + 1,))] * 2,
        compiler_params=_comm_params(sum(_nbytes(h.shape, F32) for h in halves) + _nbytes(vec.shape, F32)),
    )(*_in_hbm(list(halves) + [vec]))
    return outs[:n], outs[n]


def _row_block(rows, cols, n_arrays):
    br = rows
    while br % 16 == 0 and 2 * n_arrays * br * cols * 4 > (16 << 20):
        br //= 2
    return br


def _add2(a, b, out_dtype, name):
    rows, cols = a.shape
    br = _row_block(rows, cols, 3)

    def body(a_ref, b_ref, o_ref):
        o_ref[...] = (a_ref[...] + b_ref[...]).astype(out_dtype)

    spec = pl.BlockSpec((br, cols), lambda i: (i, 0))
    return pl.pallas_call(body, name=name, grid=(rows // br,), in_specs=[spec, spec], out_specs=spec,
                          out_shape=pltpu.HBM(a.shape, out_dtype),
                          compiler_params=_cparams(3 * br * cols * 4, 1))(*_in_hbm([a, b]))


def _sum_slots(a, name):
    n, rows, cols = a.shape
    br = _row_block(rows, cols, n + 1)

    def body(a_ref, o_ref):
        acc = a_ref[0].astype(F32)
        for s in range(1, n):
            acc = acc + a_ref[s].astype(F32)
        o_ref[...] = acc

    return pl.pallas_call(body, name=name, grid=(rows // br,),
                          in_specs=[pl.BlockSpec((n, br, cols), lambda i: (0, i, 0))],
                          out_specs=pl.BlockSpec((br, cols), lambda i: (i, 0)),
                          out_shape=pltpu.HBM((rows, cols), F32),
                          compiler_params=_cparams((n + 1) * br * cols * 4, 1))(*_in_hbm([a]))


def _adamw_math(w, g, m, v):
    m = ADAM_B1 * m + (1.0 - ADAM_B1) * g
    v = ADAM_B2 * v + (1.0 - ADAM_B2) * (g * g)
    m_hat = m / (1.0 - ADAM_B1 ** ADAM_STEP)
    v_hat = v / (1.0 - ADAM_B2 ** ADAM_STEP)
    delta = -ADAM_LR * (m_hat / (jnp.sqrt(v_hat) + ADAM_EPS) + ADAM_WD * w)
    return delta, m, v


def _adamw(g, w, m, v, name):
    rows, cols = g.shape
    br = _row_block(rows, cols, 7)

    def body(g_ref, w_ref, m_ref, v_ref, d_ref, nm_ref, nv_ref):
        d_ref[...], nm_ref[...], nv_ref[...] = _adamw_math(w_ref[...], g_ref[...], m_ref[...], v_ref[...])

    spec = pl.BlockSpec((br, cols), lambda i: (i, 0))
    return pl.pallas_call(body, name=name, grid=(rows // br,), in_specs=[spec] * 4, out_specs=[spec] * 3,
                          out_shape=[jax.ShapeDtypeStruct(g.shape, F32)] * 3,
                          compiler_params=_cparams(7 * br * cols * 4, 1))(g, w, m, v)


ROW_FINAL_G, ROW_PE_G, ROW_LOSS = 0, 1, 2
ROW_CONV_B, ROW_CN_G, ROW_CN_B, ROW_B_PW2 = 8, 9, 10, 11
ROW_LN_G = 16
ROW_ONORM_G, ROW_LB = 24, 25
ROW_CONV_W = 32
SMALL = ["ln_g", "conv_b", "cnorm_g", "cnorm_b", "b_pw2", "onorm_g", "pe_norm_g", "final_g"]
SMALL_ROW = dict(ln_g=ROW_LN_G, conv_b=ROW_CONV_B, cnorm_g=ROW_CN_G, cnorm_b=ROW_CN_B, b_pw2=ROW_B_PW2,
                 onorm_g=ROW_ONORM_G, pe_norm_g=ROW_PE_G, final_g=ROW_FINAL_G)


def _adamw_small(vsum, gcw, lb_logits, params):
    names = SMALL + ["lb_logits", "conv_w"]
    flat = [t for nm in names for t in params[nm]]

    def body(*refs):
        vs_ref, gcw_ref, lbl_ref = refs[:3]
        ins = refs[3:3 + 3 * len(names)]
        outs = refs[3 + 3 * len(names):]
        for q, nm in enumerate(names):
            w_ref, m_ref, v_ref = ins[3 * q:3 * q + 3]
            g_ref, d_ref, nm_ref, nv_ref = outs[4 * q:4 * q + 4]
            if nm == "conv_w":
                g = gcw_ref[...]
            elif nm == "lb_logits":
                lb = _softmax_row0(lbl_ref[...])
                g0 = vs_ref[ROW_LB:ROW_LB + 1, :] * lb * (1.0 - lb)
                g = jnp.concatenate([g0, -g0], axis=0)
            else:
                g = vs_ref[SMALL_ROW[nm]:SMALL_ROW[nm] + 1, :]
            g_ref[...] = g
            d_ref[...], nm_ref[...], nv_ref[...] = _adamw_math(w_ref[...], g, m_ref[...], v_ref[...])

    out_shape = [jax.ShapeDtypeStruct(params[nm][0].shape, F32) for nm in names for _ in range(4)]
    outs = pl.pallas_call(body, name="adamw_small", out_shape=out_shape)(vsum, gcw, lb_logits, *flat)
    return {nm: tuple(outs[4 * q:4 * q + 4]) for q, nm in enumerate(names)}


TOKEN_TILE = dict(rmsnorm=512, inproj_fwd=2048, conv=256, hgrn=512, tail=256, inproj_bwd_x=1024, weight_grad=2048)


def _tile(T, family):
    return min(T, TOKEN_TILE[family])


def kernel(x, p, ln_g, w_in, conv_w, conv_b, cnorm_g, cnorm_b, w_pw2, b_pw2, lb_logits, onorm_g, w_out, pe_norm_g, w_pg, w_pp, final_g, loss_target, m_ln_g, m_w_in, m_conv_w, m_conv_b, m_cnorm_g, m_cnorm_b, m_w_pw2, m_b_pw2, m_lb_logits, m_onorm_g, m_w_out, m_pe_norm_g, m_w_pg, m_w_pp, m_final_g, v_ln_g, v_w_in, v_conv_w, v_conv_b, v_cnorm_g, v_cnorm_b, v_w_pw2, v_b_pw2, v_lb_logits, v_onorm_g, v_w_out, v_pe_norm_g, v_w_pg, v_w_pp, v_final_g):
    given = dict(locals())
    x2, p2, tgt = x[0], p[0, 0], loss_target[0]
    T = x2.shape[0]
    fin_g = final_g.reshape(1, D)

    conv_w_pad = jnp.pad(conv_w[0], ((0, HALO - CONV_K), (0, 0)))
    u = _rmsnorm(x2, ln_g, _tile(T, "rmsnorm"))
    z, w_in_f, (w_pw2_f, w_out_f, w_pg_f, w_pp_f, conv_w_f) = _inproj_fwd(
        u, w_in[0].astype(_MXU),
        [w_pw2[0].astype(_MXU), w_out[0].astype(_MXU), w_pg[0].astype(_MXU), w_pp[0].astype(_MXU), conv_w_pad],
        [0, 0, 0, 1, 1], _tile(T, "inproj_fwd"))
    yc, y1 = _conv_fwd(z, conv_w_f, conv_b, cnorm_g, cnorm_b, w_pw2_f, b_pw2, _tile(T, "conv"))
    o_raw, yh, s_chunks = _hgrn_fwd(z, lb_logits, onorm_g, _tile(T, "hgrn"), HB)
    dyc, dyh, dh, n2, ds, dpe, dhb, pb, vec_tail = _tail(
        x2, yc, yh, p2, tgt, w_out_f, w_pg_f, w_pp_f, pe_norm_g, fin_g, _tile(T, "tail"))
    tk = _tile(T, "weight_grad")
    g_w_out_c = _tn_matmul(yc, dhb, tk, "grad_w_out_conv")
    g_w_out_h = _tn_matmul(yh, dhb, tk, "grad_w_out_hgrn")
    g_w_pg = _tn_matmul(n2, ds, tk, "grad_w_pg")
    g_w_pp = _tn_matmul(pb, dpe, tk, "grad_w_pp")
    dzc, a_act, dy2, vec_conv, g_conv_w = _conv_bwd(z, y1, dyc, conv_w_f, cnorm_g, cnorm_b, w_pw2_f, b_pw2, _tile(T, "conv"))
    g_w_pw2 = _tn_matmul(a_act, dy2, tk, "grad_w_pw2")

    def pair_sums(names, slabs, tag):
        mine, got = _pair_exchange(slabs, "grad_pair_exchange_" + tag)
        return [_add2(a.reshape(-1, a.shape[-1]), b.reshape(-1, b.shape[-1]), _WIRE, "pair_sum_" + nm).reshape(a.shape)
                for a, b, nm in zip(mine, got, names)]

    rest = ["w_pw2", "w_out", "w_pg", "w_pp"]
    partial_rest = pair_sums(rest, [
        _Slab([g_w_pw2], lambda s: (0, s * (D // N_CHIPS), 0), (D // N_CHIPS, D)),
        _Slab([g_w_out_c, g_w_out_h], lambda s: (s // 2, (s % 2) * (D // 2), 0), (D // 2, D)),
        _Slab([g_w_pg], lambda s: (0, s * (D // N_CHIPS), 0), (D // N_CHIPS, D)),
        _Slab([g_w_pp], lambda s: (0, 0, s * (D // N_CHIPS)), (PLE, D // N_CHIPS)),
    ], "rest")
    dzh, vec_hgrn, slots_rest = _hgrn_bwd(z, lb_logits, onorm_g, o_raw, dyh, s_chunks, partial_rest, _tile(T, "hgrn"), HB)
    g_w_in = _inproj_bwd_w(u, dzc, dzh, tk)
    partial_in = pair_sums(["w_in"], [
        _Slab([g_w_in], lambda s: (0, 0, s * (NPART * D // N_CHIPS)), (D, NPART * D // N_CHIPS))], "w_in")
    grad_x, vec_in, slots_in = _inproj_bwd_x(x2, ln_g, dzc, dzh, w_in_f, dh, partial_in, _tile(T, "inproj_bwd_x"))
    big = ["w_in"] + rest
    halves = [_sum_slots(s, "chip_sum_" + nm) for s, nm in zip(list(slots_in) + list(slots_rest), big)]
    vec = jnp.concatenate([vec_tail, vec_conv, vec_in, vec_hgrn, g_conv_w], axis=0)
    grads_big, vec_slots = _pair_share(halves, vec)
    vsum = _sum_slots(vec_slots, "vec_sum")

    out = {}
    for nm, g in zip(big, grads_big):
        w2, m2, v2 = given[nm][0], given["m_" + nm][0], given["v_" + nm][0]
        d, nm_, nv_ = _adamw(g, w2, m2, v2, "adamw_" + nm)
        out[nm] = tuple(t[None] for t in (g, d, nm_, nv_))
    chip = 2 * lax.axis_index("x") + lax.axis_index("y")
    gcw = lax.dynamic_slice(vsum, (ROW_CONV_W, chip * (D // N_CHIPS)), (CONV_K, D // N_CHIPS))
    params = {nm: (given[nm].reshape(-1, D), given["m_" + nm].reshape(-1, D), given["v_" + nm].reshape(-1, D))
              for nm in SMALL + ["lb_logits"]}
    params["conv_w"] = (conv_w[0], m_conv_w[0], v_conv_w[0])
    small = _adamw_small(vsum, gcw, lb_logits, params)
    for nm, ts in small.items():
        out[nm] = tuple(t.reshape(given[nm].shape) for t in ts)

    loss = vsum[ROW_LOSS, 0]
    order = ["ln_g", "w_in", "conv_w", "conv_b", "cnorm_g", "cnorm_b", "w_pw2", "b_pw2", "lb_logits", "onorm_g",
             "w_out", "pe_norm_g", "w_pg", "w_pp", "final_g"]
    return (loss, grad_x[None], *[out[nm][0] for nm in order], *[out[nm][1] for nm in order],
            *[out[nm][2] for nm in order], *[out[nm][3] for nm in order])
```

```python
import functools

import jax
import jax.numpy as jnp
from jax import lax
from jax.experimental import pallas as pl
from jax.experimental.pallas import tpu as pltpu

F32 = jnp.float32
BF16 = jnp.bfloat16
_MXU = jnp.bfloat16
_WIRE = jnp.bfloat16

D = 1024
NPART = 7
PLE = 256
HEADS = 8
HD = 128
CHUNK = 64
CONV_K = 31
HALO = 32
EPS = 1e-6
N_CHIPS = 4
N_DEV = 8
HB = 8
VEC_ROWS = 64

ADAM_LR = 0.001
ADAM_B1 = 0.9
ADAM_B2 = 0.999
ADAM_EPS = 1e-08
ADAM_WD = 0.01
ADAM_STEP = 10

V7X_VMEM_LIMIT = 60000 * 1024
MESH_ID = pl.DeviceIdType.MESH
ANY = pl.BlockSpec(memory_space=pltpu.HBM)


def _in_hbm(arrays):
    return [pltpu.with_memory_space_constraint(a, pltpu.HBM) for a in arrays]


def _cparams(block_bytes, n_grid_dims):
    limit = min(V7X_VMEM_LIMIT, 2 * block_bytes + (24 << 20))
    return pltpu.CompilerParams(vmem_limit_bytes=int(limit), dimension_semantics=("arbitrary",) * n_grid_dims)


def _nbytes(shape, dtype):
    n = 1
    for s in shape:
        n *= s
    return n * jnp.dtype(dtype).itemsize


def _dot(a, b):
    return jnp.dot(a.astype(_MXU), b.astype(_MXU), preferred_element_type=F32)


def _dot_nt(a, b):
    return lax.dot_general(a.astype(_MXU), b.astype(_MXU), (((1,), (1,)), ((), ())), preferred_element_type=F32)


def _dot_tn(a, b):
    return lax.dot_general(a.astype(_MXU), b.astype(_MXU), (((0,), (0,)), ((), ())), preferred_element_type=F32)


def _tri_dot(tri_bf, x):
    x1 = x.astype(BF16)
    r1 = x - x1.astype(F32)
    x2 = r1.astype(BF16)
    x3 = (r1 - x2.astype(F32)).astype(BF16)
    d = lambda t: jnp.dot(tri_bf, t, preferred_element_type=F32)
    return d(x1) + d(x2) + d(x3)


def _split2(x):
    hi = x.astype(BF16)
    return hi, (x - hi.astype(F32)).astype(BF16)


def _dot3(dims, a, b):
    d = lambda p, q: lax.dot_general(p, q, (dims, ((), ())), preferred_element_type=F32)
    return d(a[0], b[0]) + d(a[0], b[1]) + d(a[1], b[0])


def _sigmoid(x):
    return jax.nn.sigmoid(x)


def _mean_lanes(x):
    return jnp.mean(x, axis=-1, keepdims=True)


def _sum_rows(x):
    return jnp.sum(x, axis=0, keepdims=True)


def _group_ln(y):
    yn, rs = [], []
    for g in range(D // HD):
        blk = y[:, g * HD:(g + 1) * HD]
        xc = blk - _mean_lanes(blk)
        r = lax.rsqrt(_mean_lanes(xc * xc) + EPS)
        yn.append(xc * r)
        rs.append(jnp.broadcast_to(r, blk.shape))
    return jnp.concatenate(yn, axis=1), jnp.concatenate(rs, axis=1)


def _group_ln_bwd(dyn, yn, rstd):
    out = []
    for g in range(D // HD):
        sl = slice(g * HD, (g + 1) * HD)
        d, n = dyn[:, sl], yn[:, sl]
        out.append(rstd[:, sl] * (d - _mean_lanes(d) - n * _mean_lanes(d * n)))
    return jnp.concatenate(out, axis=1)


def _head_means(x, hb, fn=lambda m: m):
    return jnp.concatenate([jnp.broadcast_to(fn(_mean_lanes(x[:, hh * HD:(hh + 1) * HD])), (x.shape[0], HD))
                            for hh in range(hb)], axis=1)


def _head_rsqrt_mean(x, hb):
    return _head_means(x, hb, lambda m: lax.rsqrt(m + EPS))


def _softmax_row0(lbl):
    m = jnp.max(lbl, axis=0, keepdims=True)
    e = jnp.exp(lbl - m)
    return e[0:1, :] / jnp.sum(e, axis=0, keepdims=True)


def _hosted_gather(phases, step, at, shards, axes, ins, outs, bufs, sems):
    gather = _Gather([s.shape for s in shards], axes, ins, outs, bufs, sems)
    for phase in phases:
        pl.when(step == at[phase])(getattr(gather, phase))


def _rmsnorm_gather(x, ln_g, shards, axes, tT):
    T = x.shape[0]
    n = len(shards)
    at = _hosted_steps(T // tT)

    def body(x_ref, g_ref, *rest):
        ins, u_ref, outs, bufs, sems = rest[:n], rest[n], rest[n + 1:2 * n + 1], rest[2 * n + 1:3 * n + 1], rest[3 * n + 1:]
        host = functools.partial(_hosted_gather, step=pl.program_id(0), at=at, shards=shards, axes=axes,
                                 ins=ins, outs=outs, bufs=bufs, sems=sems)
        host(("start", "turn", "forward"))
        xv = x_ref[...]
        r = lax.rsqrt(_mean_lanes(xv * xv) + EPS)
        u_ref[...] = (xv * r * g_ref[...]).astype(_MXU)
        host(("finish",))

    blk = _nbytes((tT, D), F32) * 2 + _nbytes((tT, D), _MXU) + sum(_nbytes(s.shape, s.dtype) for s in shards)
    outs = pl.pallas_call(
        body, name="rmsnorm_gather", grid=(T // tT,),
        in_specs=[pl.BlockSpec((tT, D), lambda i: (i, 0)), pl.BlockSpec((1, D), lambda i: (0, 0))] + [ANY] * n,
        out_specs=[pl.BlockSpec((tT, D), lambda i: (i, 0))] + [ANY] * n,
        out_shape=[jax.ShapeDtypeStruct((T, D), _MXU)]
        + [pltpu.HBM(fs, s.dtype) for fs, s in zip(_full_shapes(shards, axes), shards)],
        scratch_shapes=_Gather.scratch(shards),
        compiler_params=_cparams(blk, 1),
    )(x, ln_g, *_in_hbm(shards))
    return outs[0], outs[1:]


def _inproj_fwd(u, w_in, shards, axes, tT):
    T = u.shape[0]
    n = len(shards)
    at = _hosted_steps((T // tT) * NPART)

    def body(u_ref, w_ref, *rest):
        ins, z_ref, outs, bufs, sems = rest[:n], rest[n], rest[n + 1:2 * n + 1], rest[2 * n + 1:3 * n + 1], rest[3 * n + 1:]
        host = functools.partial(_hosted_gather, step=pl.program_id(0) * NPART + pl.program_id(1), at=at, shards=shards,
                                 axes=axes, ins=ins, outs=outs, bufs=bufs, sems=sems)
        host(("start", "turn", "forward"))
        z_ref[...] = jnp.dot(u_ref[...], w_ref[...], preferred_element_type=F32)
        host(("finish",))

    blk = (_nbytes((tT, D), F32) + _nbytes((D, D), _MXU) + _nbytes((tT, D), _MXU)
           + sum(_nbytes(s.shape, s.dtype) for s in shards))
    outs = pl.pallas_call(
        body, name="inproj_fwd", grid=(T // tT, NPART),
        in_specs=[pl.BlockSpec((tT, D), lambda i, j: (i, 0)), pl.BlockSpec((D, D), lambda i, j: (0, j))] + [ANY] * n,
        out_specs=[pl.BlockSpec((tT, D), lambda i, j: (i, j))] + [ANY] * n,
        out_shape=[jax.ShapeDtypeStruct((T, NPART * D), F32)]
        + [pltpu.HBM(fs, s.dtype) for fs, s in zip(_full_shapes(shards, axes), shards)],
        scratch_shapes=_Gather.scratch(shards),
        compiler_params=_cparams(blk, 2),
    )(u, w_in, *_in_hbm(shards))
    return outs[0], outs[1:]


def _shifted_windows(ext, first, visit):
    n = ext.shape[0]
    for m in range(first, first + CONV_K):
        visit(m, (ext if m == 0 else pltpu.roll(ext, n - m, axis=0))[0:n - HALO, :])


def _conv_fwd(z, conv_w, conv_b, cn_g, cn_b, w_pw2, b_pw2, tT):
    T = z.shape[0]

    def body(cv_ref, cg_ref, ct_ref, cw_ref, cb_ref, ng_ref, nb_ref, wp_ref, bp_ref, yc_ref, y1_ref, ext):
        @pl.when(pl.program_id(0) == 0)
        def _():
            ext[...] = jnp.zeros_like(ext)
        ext[0:HALO, :] = ext[tT:tT + HALO, :]
        ext[HALO:, :] = cv_ref[...] * _sigmoid(cg_ref[...])
        cw = cw_ref[...]
        acc = [cb_ref[...]]

        def tap(m, win):
            acc[0] = acc[0] + win * cw[m - 2:m - 1, :]
        _shifted_windows(ext[...], 2, tap)
        y1 = acc[0]
        y1_ref[...] = y1
        yn, _ = _group_ln(y1)
        apre = yn * ng_ref[...] + nb_ref[...]
        a = apre * _sigmoid(apre)
        y2 = _dot(a, wp_ref[...]) + bp_ref[...]
        ct = ct_ref[...]
        yc_ref[...] = (y2 * (ct * _sigmoid(ct))).astype(_MXU)

    part = lambda p: pl.BlockSpec((tT, D), lambda i: (i, p))
    row = pl.BlockSpec((1, D), lambda i: (0, 0))
    tok = pl.BlockSpec((tT, D), lambda i: (i, 0))
    blk = 4 * _nbytes((tT, D), F32) + _nbytes((D, D), _MXU) + _nbytes((tT, D), _MXU) + 8 * _nbytes((tT + HALO, D), F32)
    return pl.pallas_call(
        body, name="conv_fwd", grid=(T // tT,),
        in_specs=[part(0), part(1), part(2), pl.BlockSpec((HALO, D), lambda i: (0, 0)), row, row, row,
                  pl.BlockSpec((D, D), lambda i: (0, 0)), row],
        out_specs=[tok, tok],
        out_shape=[jax.ShapeDtypeStruct((T, D), _MXU), jax.ShapeDtypeStruct((T, D), F32)],
        scratch_shapes=[pltpu.VMEM((tT + HALO, D), F32)],
        compiler_params=_cparams(blk, 1),
    )(z, z, z, conv_w, conv_b, cn_g, cn_b, w_pw2, b_pw2)


def _hgrn_gates(lb, hq, hf):
    sq = _sigmoid(hq)
    sg = _sigmoid(hf)
    f = lb + (1.0 - lb) * sg
    return sq, sg, f, hq * sq, (1.0 - lb) * (1.0 - sg), jnp.log(f)


def _chunk_decays(lf, q, k):
    r = lax.broadcasted_iota(jnp.int32, (CHUNK, CHUNK), 0)
    c = lax.broadcasted_iota(jnp.int32, (CHUNK, CHUNK), 1)
    b = _tri_dot((r >= c).astype(BF16), lf)
    bm = b[CHUNK // 2 - 1:CHUNK // 2, :]
    bl = b[CHUNK - 1:CHUNK, :]
    eb = jnp.exp(b)
    eqm = jnp.exp(b - bm)
    ekm = jnp.exp(bm - b)
    ekd = jnp.exp(bl - b)
    return dict(causal=r >= c, eb=eb, eqm=eqm, ekm=ekm, ekd=ekd, ebl=jnp.exp(bl),
                qd=q * eb, qm=q * eqm, km=k * ekm, kd=k * ekd)


def _hgrn_fwd(z, lb_logits, onorm_g, tT, hb):
    T = z.shape[0]
    nc = tT // CHUNK
    w = hb * HD

    def body(lbl_ref, og_ref, hq_ref, hf_ref, hi_ref, hg_ref, o_ref, yh_ref, sc_ref, st):
        @pl.when(pl.program_id(1) == 0)
        def _():
            st[...] = jnp.zeros_like(st)
        lb_all = _softmax_row0(lbl_ref[...])
        og_all = og_ref[...]

        def chunk(c, carry):
            sl = pl.ds(pl.multiple_of(c * CHUNK, CHUNK), CHUNK)
            lanes = [slice(hh * HD, (hh + 1) * HD) for hh in range(hb)]
            heads = lambda fn: [fn(hh, ln) for hh, ln in enumerate(lanes)]
            hg, v = hg_ref[sl, :], hi_ref[sl, :]
            _, _, _, q, k, lf = _hgrn_gates(lb_all, hq_ref[sl, :], hf_ref[sl, :])
            dc = _chunk_decays(lf, q, k)
            s_t = heads(lambda hh, ln: st[hh])
            a = heads(lambda hh, ln: jnp.where(dc["causal"], _dot_nt(dc["qm"][:, ln], dc["km"][:, ln]), 0.0))
            o_inter = heads(lambda hh, ln: _dot_nt(dc["qd"][:, ln], s_t[hh]))
            kv = heads(lambda hh, ln: _dot_tn(v[:, ln], dc["kd"][:, ln]))
            o_intra = heads(lambda hh, ln: _dot(a[hh], v[:, ln]))
            for hh, ln in enumerate(lanes):
                sc_ref[hh, c] = s_t[hh]
                st[hh] = s_t[hh] * dc["ebl"][:, ln] + kv[hh]
            o = jnp.concatenate([o_inter[hh] + o_intra[hh] for hh in range(hb)], axis=1)
            o_ref[sl, :] = o
            n = o * _head_rsqrt_mean(o * o, hb)
            yh_ref[sl, :] = ((n * og_all) * (hg * _sigmoid(hg))).astype(_MXU)
            return carry

        lax.fori_loop(0, nc, chunk, 0, unroll=8)

    zpart = lambda p: pl.BlockSpec((tT, w), lambda h, i: (i, p * (HEADS // hb) + h))
    blk = 6 * _nbytes((tT, w), F32) + _nbytes((hb, nc, HD, HD), F32)
    return pl.pallas_call(
        body, name="hgrn_fwd", grid=(HEADS // hb, T // tT),
        in_specs=[pl.BlockSpec((2, w), lambda h, i: (0, h)), pl.BlockSpec((1, w), lambda h, i: (0, h)),
                  zpart(3), zpart(4), zpart(5), zpart(6)],
        out_specs=[pl.BlockSpec((tT, w), lambda h, i: (i, h)), pl.BlockSpec((tT, w), lambda h, i: (i, h)),
                   pl.BlockSpec((hb, nc, HD, HD), lambda h, i: (h, i, 0, 0))],
        out_shape=[jax.ShapeDtypeStruct((T, D), F32), jax.ShapeDtypeStruct((T, D), _MXU),
                   jax.ShapeDtypeStruct((HEADS, T // CHUNK, HD, HD), F32)],
        scratch_shapes=[pltpu.VMEM((hb, HD, HD), F32)],
        compiler_params=_cparams(blk, 2),
    )(lb_logits, onorm_g, z, z, z, z)


def _hgrn_bwd(z, lb_logits, onorm_g, o_raw, dyh, s_chunks, partials, tT, hb):
    T = z.shape[0]
    nc = tT // CHUNK
    nI = T // tT
    w = hb * HD
    n = len(partials)
    at = _hosted_steps((HEADS // hb) * nI)

    def body(lbl_ref, og_ref, hq_ref, hf_ref, hi_ref, hg_ref, o_ref, dy_ref, sc_ref, *rest):
        (dz_ref, vec_ref), dst = rest[n:n + 2], rest[2 * n + 2]
        exchange = _ChipExchange(n, rest[:n], rest[n + 2:2 * n + 2], rest[2 * n + 3:3 * n + 3], rest[3 * n + 3:])
        step = pl.program_id(0) * nI + pl.program_id(1)
        pl.when(step == at["start"])(exchange.start)
        pl.when(step == at["turn"])(exchange.turn)

        @pl.when(pl.program_id(1) == 0)
        def _():
            dst[...] = jnp.zeros_like(dst)
            vec_ref[...] = jnp.zeros_like(vec_ref)
        lb_all = _softmax_row0(lbl_ref[...])
        og_all = og_ref[...]
        last_row = lax.broadcasted_iota(jnp.int32, (CHUNK, w), 0) == CHUNK - 1
        r64 = lax.broadcasted_iota(jnp.int32, (CHUNK, CHUNK), 0)
        c64 = lax.broadcasted_iota(jnp.int32, (CHUNK, CHUNK), 1)
        upper = (c64 >= r64).astype(BF16)
        lanes = [slice(hh * HD, (hh + 1) * HD) for hh in range(hb)]
        heads = lambda fn: [fn(hh, ln) for hh, ln in enumerate(lanes)]
        wide = lambda parts: jnp.concatenate(parts, axis=1)

        def chunk(cc, carry):
            c = nc - 1 - cc
            sl = pl.ds(pl.multiple_of(c * CHUNK, CHUNK), CHUNK)
            hq, hg, v = hq_ref[sl, :], hg_ref[sl, :], hi_ref[sl, :]
            sq, sg, f, q, k, lf = _hgrn_gates(lb_all, hq, hf_ref[sl, :])
            dc = _chunk_decays(lf, q, k)
            s_t = heads(lambda hh, ln: sc_ref[hh, c])
            ds_t = heads(lambda hh, ln: dst[hh])
            o, dy = o_ref[sl, :], dy_ref[sl, :]
            r = _head_rsqrt_mean(o * o, hb)
            n = o * r
            sgg = _sigmoid(hg)
            silu_g = hg * sgg
            dhg = dy * (n * og_all) * (sgg * (1.0 + hg * (1.0 - sgg)))
            dn = dy * og_all * silu_g
            g_og = _sum_rows(dy * n * silu_g)
            do = r * (dn - n * _head_means(dn * n, hb))
            a = heads(lambda hh, ln: jnp.where(dc["causal"], _dot_nt(dc["qm"][:, ln], dc["km"][:, ln]), 0.0))
            dam = heads(lambda hh, ln: jnp.where(dc["causal"], _dot_nt(do[:, ln], v[:, ln]), 0.0))
            dqd = wide(heads(lambda hh, ln: _dot(do[:, ln], s_t[hh])))
            dkd = wide(heads(lambda hh, ln: _dot(v[:, ln], ds_t[hh])))
            dv_inter = heads(lambda hh, ln: _dot_nt(dc["kd"][:, ln], ds_t[hh]))
            dqs = heads(lambda hh, ln: _dot_tn(do[:, ln], dc["qd"][:, ln]))
            dv = wide(heads(lambda hh, ln: _dot_tn(a[hh], do[:, ln]) + dv_inter[hh]))
            dam2 = [_split2(t) for t in dam]
            km2, qm2 = _split2(dc["km"]), _split2(dc["qm"])
            dqm = wide(heads(lambda hh, ln: _dot3(((1,), (0,)), dam2[hh], (km2[0][:, ln], km2[1][:, ln]))))
            dkm = wide(heads(lambda hh, ln: _dot3(((0,), (0,)), dam2[hh], (qm2[0][:, ln], qm2[1][:, ln]))))
            debl = wide(heads(lambda hh, ln: _sum_rows(ds_t[hh] * s_t[hh])))
            for hh, ln in enumerate(lanes):
                dst[hh] = ds_t[hh] * dc["ebl"][:, ln] + dqs[hh]
            dq = dqd * dc["eb"] + dqm * dc["eqm"]
            dk = dkm * dc["ekm"] + dkd * dc["ekd"]
            dbl = _sum_rows(dkd * dc["kd"]) + debl * dc["ebl"]
            db = dq * q - dk * k + jnp.where(last_row, dbl, 0.0)
            dlf = _tri_dot(upper, db)
            dfk = dlf / f - dk
            dz_ref[0, sl, :] = (dq * (sq * (1.0 + hq * (1.0 - sq)))).astype(_MXU)
            dz_ref[1, sl, :] = (dfk * ((1.0 - lb_all) * sg * (1.0 - sg))).astype(_MXU)
            dz_ref[2, sl, :] = dv.astype(_MXU)
            dz_ref[3, sl, :] = dhg.astype(_MXU)
            vec_ref[0:1, :] += g_og
            vec_ref[1:2, :] += _sum_rows(dfk * (1.0 - sg))
            return carry

        lax.fori_loop(0, nc, chunk, 0, unroll=8)
        pl.when(step == at["finish"])(exchange.finish)

    zpart = lambda p: pl.BlockSpec((tT, w), lambda h, i: (nI - 1 - i, p * (HEADS // hb) + h))
    act = pl.BlockSpec((tT, w), lambda h, i: (nI - 1 - i, h))
    blk = (6 * _nbytes((tT, w), F32) + _nbytes((hb, nc, HD, HD), F32) + 4 * _nbytes((tT, w), _MXU)
           + _ChipExchange.scratch_bytes(partials))
    outs = pl.pallas_call(
        body, name="hgrn_bwd", grid=(HEADS // hb, nI),
        in_specs=[pl.BlockSpec((2, w), lambda h, i: (0, h)), pl.BlockSpec((1, w), lambda h, i: (0, h)),
                  zpart(3), zpart(4), zpart(5), zpart(6), act, act,
                  pl.BlockSpec((hb, nc, HD, HD), lambda h, i: (h, nI - 1 - i, 0, 0))] + [ANY] * n,
        out_specs=[pl.BlockSpec((4, tT, w), lambda h, i: (0, nI - 1 - i, h)),
                   pl.BlockSpec((8, w), lambda h, i: (0, h))] + [ANY] * n,
        out_shape=[jax.ShapeDtypeStruct((4, T, D), _MXU), jax.ShapeDtypeStruct((8, D), F32)]
        + [pltpu.HBM(p.shape, p.dtype) for p in partials],
        scratch_shapes=[pltpu.VMEM((hb, HD, HD), F32)] + _ChipExchange.scratch(partials),
        compiler_params=_cparams(blk, 2),
    )(lb_logits, onorm_g, z, z, z, z, o_raw, dyh, s_chunks, *_in_hbm(partials))
    return outs[0], outs[1], outs[2:]


def _tail(x, yc, yh, p, target, w_out, w_pg, w_pp, pe_g, fin_g, tT):
    T = x.shape[0]

    def body(x_ref, yc_ref, yh_ref, p_ref, t_ref, wo_ref, wg_ref, wp_ref, pg_ref, fg_ref,
             dyc_ref, dyh_ref, dh_ref, n2_ref, ds_ref, dpe_ref, dhb_ref, pb_ref, vec_ref):
        @pl.when(pl.program_id(0) == 0)
        def _():
            vec_ref[...] = jnp.zeros_like(vec_ref)
        wo_c, wo_h = wo_ref[0:D, :], wo_ref[D:2 * D, :]
        h = x_ref[...] + _dot(yc_ref[...], wo_c) + _dot(yh_ref[...], wo_h)
        pb = p_ref[...].astype(_MXU)
        pe = _dot(pb, wp_ref[...])
        r2 = lax.rsqrt(_mean_lanes(h * h) + EPS)
        hn = h * r2
        n2 = (hn * pg_ref[...]).astype(_MXU)
        gate = _sigmoid(_dot(n2, wg_ref[...]))
        h2 = h + gate * pe
        r3 = lax.rsqrt(_mean_lanes(h2 * h2) + EPS)
        h2n = h2 * r3
        err = h2n * fg_ref[...] - t_ref[...]
        vec_ref[ROW_LOSS:ROW_LOSS + 1, :] += 0.5 * jnp.sum(_mean_lanes(err * err))
        dout = err * (1.0 / D)
        vec_ref[0:1, :] += _sum_rows(dout * h2n)
        dn3 = dout * fg_ref[...]
        dh2 = r3 * (dn3 - h2n * _mean_lanes(dn3 * h2n))
        ds = (dh2 * pe * gate * (1.0 - gate)).astype(_MXU)
        dn2 = _dot_nt(ds, wg_ref[...])
        vec_ref[1:2, :] += _sum_rows(dn2 * hn)
        dnn = dn2 * pg_ref[...]
        dh = dh2 + r2 * (dnn - hn * _mean_lanes(dnn * hn))
        dhb = dh.astype(_MXU)
        dyc_ref[...] = _dot_nt(dhb, wo_c)
        dyh_ref[...] = _dot_nt(dhb, wo_h)
        dh_ref[...] = dh
        n2_ref[...] = n2
        ds_ref[...] = ds
        dpe_ref[...] = (dh2 * gate).astype(_MXU)
        dhb_ref[...] = dhb
        pb_ref[...] = pb

    tok = lambda w: pl.BlockSpec((tT, w), lambda i: (i, 0))
    full = lambda r, c: pl.BlockSpec((r, c), lambda i: (0, 0))
    tokshape = lambda w, dt: jax.ShapeDtypeStruct((T, w), dt)
    blk = (5 * _nbytes((tT, D), F32) + 7 * _nbytes((tT, D), _MXU) + _nbytes((4 * D + PLE, D), _MXU)
           + 12 * _nbytes((tT, D), F32))
    return pl.pallas_call(
        body, name="tail_fwd_bwd", grid=(T // tT,),
        in_specs=[tok(D), tok(D), tok(D), tok(PLE), tok(D), full(2 * D, D), full(D, D), full(PLE, D), full(1, D), full(1, D)],
        out_specs=[tok(D), tok(D), tok(D), tok(D), tok(D), tok(D), tok(D), tok(PLE), full(8, D)],
        out_shape=[tokshape(D, F32), tokshape(D, F32), tokshape(D, F32), tokshape(D, _MXU), tokshape(D, _MXU),
                   tokshape(D, _MXU), tokshape(D, _MXU), tokshape(PLE, _MXU),
                   jax.ShapeDtypeStruct((8, D), F32)],
        compiler_params=_cparams(blk, 1),
    )(x, yc, yh, p, target, w_out, w_pg, w_pp, pe_g, fin_g)


def _conv_bwd(z, y1, dyc, conv_w, cn_g, cn_b, w_pw2, b_pw2, tT):
    T = z.shape[0]
    nI = T // tT
    hb = tT // HALO

    def body(cv_ref, cg_ref, ct_ref, hv_ref, hg_ref, y1_ref, dyc_ref, cw_ref, ng_ref, nb_ref, wp_ref, bp_ref,
             dz_ref, a_ref, dy2_ref, vec_ref, gcw_ref, ext, ext2, gpart):
        i = pl.program_id(0)

        @pl.when(i == 0)
        def _():
            ext2[...] = jnp.zeros_like(ext2)
            gpart[...] = jnp.zeros_like(gpart)
            vec_ref[...] = jnp.zeros_like(vec_ref)
        cv, cg, ct = cv_ref[...], cg_ref[...], ct_ref[...]
        sg = _sigmoid(cg)
        has_hist = (i < nI - 1).astype(F32)
        ext[0:HALO, :] = hv_ref[...] * _sigmoid(hg_ref[...]) * has_hist
        ext[HALO:, :] = cv * sg
        yn, rstd = _group_ln(y1_ref[...])
        apre = yn * ng_ref[...] + nb_ref[...]
        sa = _sigmoid(apre)
        a = (apre * sa).astype(_MXU)
        y2 = _dot(a, wp_ref[...]) + bp_ref[...]
        st = _sigmoid(ct)
        dyc_v = dyc_ref[...]
        dy2 = dyc_v * (ct * st)
        dy2b = dy2.astype(_MXU)
        da = _dot_nt(dy2b, wp_ref[...])
        dapre = da * (sa * (1.0 + apre * (1.0 - sa)))
        dy1 = _group_ln_bwd(dapre * ng_ref[...], yn, rstd)
        vec_ref[0:1, :] += _sum_rows(dy1)
        vec_ref[1:2, :] += _sum_rows(dapre * yn)
        vec_ref[2:3, :] += _sum_rows(dapre)
        vec_ref[3:4, :] += _sum_rows(dy2)
        dz_ref[2] = (dyc_v * y2 * (st * (1.0 + ct * (1.0 - st)))).astype(_MXU)
        a_ref[...] = a
        dy2_ref[...] = dy2b
        ext2[tT:tT + HALO, :] = ext2[0:HALO, :]
        ext2[0:tT, :] = dy1
        def grad_tap(m, win):
            p = dy1 * win
            part = p[0:8, :]
            for q in range(1, tT // 8):
                part = part + p[8 * q:8 * q + 8, :]
            gpart[m - 2] += part
        _shifted_windows(ext[...], 2, grad_tap)
        cw = cw_ref[...]
        acc = [None]

        def dv_tap(m, win):
            term = win * cw[CONV_K - 1 - m:CONV_K - m, :]
            acc[0] = term if acc[0] is None else acc[0] + term
        _shifted_windows(ext2[...], 0, dv_tap)
        dv = acc[0]
        dz_ref[0] = (dv * sg).astype(_MXU)
        dz_ref[1] = (dv * cv * sg * (1.0 - sg)).astype(_MXU)

        @pl.when(i == nI - 1)
        def _():
            gcw_ref[...] = jnp.sum(gpart[...], axis=1)

    part = lambda p: pl.BlockSpec((tT, D), lambda i: (nI - 1 - i, p))
    hist = lambda p: pl.BlockSpec((HALO, D), lambda i: (jnp.maximum((nI - 1 - i) * hb - 1, 0), p))
    tok = pl.BlockSpec((tT, D), lambda i: (nI - 1 - i, 0))
    row = pl.BlockSpec((1, D), lambda i: (0, 0))
    blk = (5 * _nbytes((tT, D), F32) + _nbytes((D, D), _MXU) + 5 * _nbytes((tT, D), _MXU)
           + 10 * _nbytes((tT + HALO, D), F32))
    return pl.pallas_call(
        body, name="conv_bwd", grid=(nI,),
        in_specs=[part(0), part(1), part(2), hist(0), hist(1), tok, tok, pl.BlockSpec((HALO, D), lambda i: (0, 0)),
                  row, row, pl.BlockSpec((D, D), lambda i: (0, 0)), row],
        out_specs=[pl.BlockSpec((3, tT, D), lambda i: (0, nI - 1 - i, 0)), tok, tok,
                   pl.BlockSpec((8, D), lambda i: (0, 0)), pl.BlockSpec((HALO, D), lambda i: (0, 0))],
        out_shape=[jax.ShapeDtypeStruct((3, T, D), _MXU), jax.ShapeDtypeStruct((T, D), _MXU),
                   jax.ShapeDtypeStruct((T, D), _MXU), jax.ShapeDtypeStruct((8, D), F32),
                   jax.ShapeDtypeStruct((HALO, D), F32)],
        scratch_shapes=[pltpu.VMEM((tT + HALO, D), F32), pltpu.VMEM((tT + HALO, D), F32), pltpu.VMEM((HALO, 8, D), F32)],
        compiler_params=_cparams(blk, 1),
    )(z, z, z, z, z, y1, dyc, conv_w, cn_g, cn_b, w_pw2, b_pw2)


def _inproj_bwd_x(x, ln_g, dzc, dzh, w_in, dh, partials, tT):
    T = x.shape[0]
    n = len(partials)
    at = _hosted_steps((T // tT) * NPART)

    def body(x_ref, g_ref, dzc_ref, dzh_ref, w_ref, dh_ref, *rest):
        (gx_ref, vec_ref), du = rest[n:n + 2], rest[2 * n + 2]
        exchange = _ChipExchange(n, rest[:n], rest[n + 2:2 * n + 2], rest[2 * n + 3:3 * n + 3], rest[3 * n + 3:])
        i, j = pl.program_id(0), pl.program_id(1)
        step = i * NPART + j
        pl.when(step == at["start"])(exchange.start)
        pl.when(step == at["turn"])(exchange.turn)

        @pl.when(j == 0)
        def _():
            du[...] = jnp.zeros_like(du)

        @pl.when(jnp.logical_and(i == 0, j == 0))
        def _():
            vec_ref[...] = jnp.zeros_like(vec_ref)

        @pl.when(j < 3)
        def _():
            du[...] += _dot_nt(dzc_ref[0], w_ref[...])

        @pl.when(j >= 3)
        def _():
            du[...] += _dot_nt(dzh_ref[0], w_ref[...])

        @pl.when(j == NPART - 1)
        def _():
            xv = x_ref[...]
            r = lax.rsqrt(_mean_lanes(xv * xv) + EPS)
            xn = xv * r
            duv = du[...]
            vec_ref[0:1, :] += _sum_rows(duv * xn)
            dun = duv * g_ref[...]
            gx_ref[...] = dh_ref[...] + r * (dun - xn * _mean_lanes(dun * xn))
        pl.when(step == at["finish"])(exchange.finish)

    tok = pl.BlockSpec((tT, D), lambda i, j: (i, 0))
    blk = (3 * _nbytes((tT, D), F32) + 2 * _nbytes((tT, D), _MXU) + _nbytes((D, D), _MXU) + 4 * _nbytes((tT, D), F32)
           + _ChipExchange.scratch_bytes(partials))
    outs = pl.pallas_call(
        body, name="inproj_bwd_x", grid=(T // tT, NPART),
        in_specs=[tok, pl.BlockSpec((1, D), lambda i, j: (0, 0)),
                  pl.BlockSpec((1, tT, D), lambda i, j: (jnp.minimum(j, 2), i, 0)),
                  pl.BlockSpec((1, tT, D), lambda i, j: (jnp.maximum(j - 3, 0), i, 0)),
                  pl.BlockSpec((D, D), lambda i, j: (0, j)), tok] + [ANY] * n,
        out_specs=[tok, pl.BlockSpec((8, D), lambda i, j: (0, 0))] + [ANY] * n,
        out_shape=[jax.ShapeDtypeStruct((T, D), F32), jax.ShapeDtypeStruct((8, D), F32)]
        + [pltpu.HBM(p.shape, p.dtype) for p in partials],
        scratch_shapes=[pltpu.VMEM((tT, D), F32)] + _ChipExchange.scratch(partials),
        compiler_params=_cparams(blk, 2),
    )(x, ln_g, dzc, dzh, w_in, dh, *_in_hbm(partials))
    return outs[0], outs[1], outs[2:]


def _inproj_bwd_w(u, dzc, dzh, tk):
    T = u.shape[0]
    nK = T // tk

    def body(u_ref, dzc_ref, dzh_ref, gw_ref):
        j, k = pl.program_id(0), pl.program_id(1)

        @pl.when(k == 0)
        def _():
            gw_ref[...] = jnp.zeros_like(gw_ref)

        @pl.when(j < 3)
        def _():
            gw_ref[...] += _dot_tn(u_ref[...], dzc_ref[0])

        @pl.when(j >= 3)
        def _():
            gw_ref[...] += _dot_tn(u_ref[...], dzh_ref[0])

    blk = 3 * _nbytes((tk, D), _MXU) + 2 * _nbytes((D, D), F32)
    return pl.pallas_call(
        body, name="inproj_bwd_w", grid=(NPART, nK),
        in_specs=[pl.BlockSpec((tk, D), lambda j, k: (k, 0)),
                  pl.BlockSpec((1, tk, D), lambda j, k: (jnp.minimum(j, 2), jnp.where(j < 3, k, nK - 1), 0)),
                  pl.BlockSpec((1, tk, D), lambda j, k: (jnp.maximum(j - 3, 0), jnp.where(j < 3, 0, k), 0))],
        out_specs=pl.BlockSpec((D, D), lambda j, k: (0, j)),
        out_shape=jax.ShapeDtypeStruct((D, NPART * D), F32),
        compiler_params=_cparams(blk, 2),
    )(u, dzc, dzh)


def _tn_matmul(a, b, tk, name):
    T, M = a.shape
    N = b.shape[1]

    def body(a_ref, b_ref, o_ref):
        @pl.when(pl.program_id(0) == 0)
        def _():
            o_ref[...] = jnp.zeros_like(o_ref)
        o_ref[...] += _dot_tn(a_ref[...], b_ref[...])

    blk = _nbytes((tk, M), _MXU) + _nbytes((tk, N), _MXU) + 2 * _nbytes((M, N), F32)
    return pl.pallas_call(
        body, name=name, grid=(T // tk,),
        in_specs=[pl.BlockSpec((tk, M), lambda k: (k, 0)), pl.BlockSpec((tk, N), lambda k: (k, 0))],
        out_specs=pl.BlockSpec((M, N), lambda k: (0, 0)),
        out_shape=pltpu.HBM((M, N), F32),
        compiler_params=_cparams(blk, 1),
    )(a, b)


def _place():
    return lax.axis_index("x"), lax.axis_index("y"), lax.axis_index("c")


def _flip(v, d):
    return 1 - v if d else v


CHIP_MOVES = [(1, 0), (0, 1), (1, 1)]
DEV_MOVES = [(dx, dy, dc) for dx in (0, 1) for dy in (0, 1) for dc in (0, 1)][1:]


def _shard_slice(ref, axis, size, s):
    start = pl.multiple_of(s * size, size)
    return ref.at[pl.ds(start, size), :] if axis == 0 else ref.at[:, pl.ds(start, size)]


class _Bounce:
    def __init__(self, src, buf, dst, sem_in, sem_out):
        self.load = pltpu.make_async_copy(src, buf, sem_in)
        self.store = pltpu.make_async_copy(buf, dst, sem_out)

    def start(self):
        self.load.start()

    def turn(self):
        self.load.wait()
        self.store.start()

    def wait(self):
        self.store.wait()


def _comm_params(scratch_bytes):
    return pltpu.CompilerParams(vmem_limit_bytes=int(min(V7X_VMEM_LIMIT, scratch_bytes + (8 << 20))))


class _Gather:
    def __init__(self, shapes, axes, ins, outs, bufs, sems):
        self.shapes, self.axes, self.ins, self.outs, self.bufs = shapes, axes, ins, outs, bufs
        self.ici_send, self.ici_recv, self.d2d_send, self.d2d_recv, self.in_sems, self.out_sems = sems
        self.x, self.y, self.c = _place()
        self.me = 2 * self.x + self.y
        self.pairs = [(k, j) for k in range(len(shapes)) for j in range(3)]

    @staticmethod
    def scratch(shards):
        n = len(shards)
        return ([pltpu.VMEM(s.shape, s.dtype) for s in shards]
                + [pltpu.SemaphoreType.DMA((3 * n,))] * 4 + [pltpu.SemaphoreType.DMA((n,))] * 2)

    def _own_half(self, k, hc):
        half = self.shapes[k][0] // 2
        return self.ins[k].at[pl.ds(pl.multiple_of(hc * half, 16), half), :]

    def _region(self, k, who, hc):
        rows, cols = self.shapes[k]
        half = rows // 2
        if self.axes[k] == 0:
            return self.outs[k].at[pl.ds(pl.multiple_of(who * rows + hc * half, 16), half), :]
        return self.outs[k].at[pl.ds(pl.multiple_of(hc * half, 16), half), pl.ds(pl.multiple_of(who * cols, HD), cols)]

    def _peer(self, j):
        return 2 * _flip(self.x, CHIP_MOVES[j][0]) + _flip(self.y, CHIP_MOVES[j][1])

    def _ici(self, k, j, who, hc):
        dx, dy = CHIP_MOVES[j]
        return pltpu.make_async_remote_copy(
            src_ref=self._own_half(k, hc), dst_ref=self._region(k, who, hc),
            send_sem=self.ici_send.at[3 * k + j], recv_sem=self.ici_recv.at[3 * k + j],
            device_id=(_flip(self.x, dx), _flip(self.y, dy), self.c), device_id_type=MESH_ID)

    def _d2d(self, k, j, who, hc):
        return pltpu.make_async_remote_copy(
            src_ref=self._region(k, who, hc), dst_ref=self._region(k, who, hc),
            send_sem=self.d2d_send.at[3 * k + j], recv_sem=self.d2d_recv.at[3 * k + j],
            device_id=(self.x, self.y, 1 - self.c), device_id_type=MESH_ID)

    def _local(self, k):
        size = self.shapes[k][self.axes[k]]
        return _Bounce(self.ins[k], self.bufs[k], _shard_slice(self.outs[k], self.axes[k], size, self.me),
                       self.in_sems.at[k], self.out_sems.at[k])

    def start(self):
        for k in range(len(self.shapes)):
            self._local(k).start()
        for k, j in self.pairs:
            self._ici(k, j, self.me, self.c).start()

    def turn(self):
        for k in range(len(self.shapes)):
            self._local(k).turn()

    def forward(self):
        for k, j in self.pairs:
            self._ici(k, j, self._peer(j), self.c).wait_recv()
            self._d2d(k, j, self._peer(j), self.c).start()

    def finish(self):
        for k, j in self.pairs:
            self._d2d(k, j, self._peer(j), 1 - self.c).wait_recv()
        for k, j in self.pairs:
            self._ici(k, j, self.me, self.c).wait_send()
            self._d2d(k, j, self._peer(j), self.c).wait_send()
        for k in range(len(self.shapes)):
            self._local(k).wait()


def _full_shapes(shards, axes):
    return [tuple(d * (N_CHIPS if a == ax else 1) for a, d in enumerate(s.shape)) for s, ax in zip(shards, axes)]


class _Slab:
    def __init__(self, arrays, pick, shard_shape):
        self.arrays = arrays
        self.pick = pick
        self.rows, self.cols = shard_shape
        self.half = self.rows // 2


def _pair_exchange(slabs, name):
    n = len(slabs)
    n_in = sum(len(sl.arrays) for sl in slabs)

    def body(*refs):
        ins = refs[:n_in]
        mine, got = refs[n_in:n_in + n], refs[n_in + n:n_in + 2 * n]
        bufs = refs[n_in + 2 * n:n_in + 3 * n]
        send_sems, recv_sems, in_sems, out_sems = refs[n_in + 3 * n:]
        x, y, c = _place()
        started = []
        base = 0
        for k, sl in enumerate(slabs):
            for s in range(N_CHIPS):
                ai, r0, c0 = sl.pick(s)
                src = ins[base + ai]

                def half(hc):
                    return src.at[pl.ds(pl.multiple_of(r0 + hc * sl.half, 8), sl.half), pl.ds(c0, sl.cols)]
                q = N_CHIPS * k + s
                loc = _Bounce(half(c), bufs[k].at[s], mine[k].at[s], in_sems.at[q], out_sems.at[q])
                loc.start()
                cp = pltpu.make_async_remote_copy(
                    src_ref=half(1 - c), dst_ref=got[k].at[s], send_sem=send_sems.at[q], recv_sem=recv_sems.at[q],
                    device_id=(x, y, 1 - c), device_id_type=MESH_ID)
                cp.start()
                started.append((loc, cp))
            base += len(sl.arrays)
        for loc, cp in started:
            loc.turn()
        for loc, cp in started:
            cp.wait_recv()
        for loc, cp in started:
            cp.wait_send()
            loc.wait()

    flat_in = [a for sl in slabs for a in sl.arrays]
    compact = [pltpu.HBM((N_CHIPS, sl.half, sl.cols), F32) for sl in slabs]
    outs = pl.pallas_call(
        body, name=name,
        in_specs=[ANY] * n_in, out_specs=[ANY] * (2 * n), out_shape=compact + compact,
        scratch_shapes=[pltpu.VMEM(s.shape, F32) for s in compact] + [pltpu.SemaphoreType.DMA((N_CHIPS * n,))] * 4,
        compiler_params=_comm_params(sum(_nbytes(s.shape, F32) for s in compact)),
    )(*_in_hbm(flat_in))
    return outs[:n], outs[n:]


class _ChipExchange:
    def __init__(self, n, ins, outs, bufs, sems):
        self.n, self.ins, self.outs, self.bufs = n, ins, outs, bufs
        self.send_sems, self.recv_sems, self.in_sems, self.out_sems = sems
        self.x, self.y, self.c = _place()
        self.me = 2 * self.x + self.y
        self.pairs = [(k, j) for k in range(n) for j in range(3)]

    @staticmethod
    def scratch(partials):
        n = len(partials)
        return ([pltpu.VMEM(p.shape[1:], p.dtype) for p in partials]
                + [pltpu.SemaphoreType.DMA((3 * n,))] * 2 + [pltpu.SemaphoreType.DMA((n,))] * 2)

    @staticmethod
    def scratch_bytes(partials):
        return sum(_nbytes(p.shape[1:], p.dtype) for p in partials)

    def _copy(self, k, j, src_slot, dst_slot):
        px, py = _flip(self.x, CHIP_MOVES[j][0]), _flip(self.y, CHIP_MOVES[j][1])
        return pltpu.make_async_remote_copy(
            src_ref=self.ins[k].at[src_slot], dst_ref=self.outs[k].at[dst_slot],
            send_sem=self.send_sems.at[3 * k + j], recv_sem=self.recv_sems.at[3 * k + j],
            device_id=(px, py, self.c), device_id_type=MESH_ID)

    def _peer(self, j):
        return 2 * _flip(self.x, CHIP_MOVES[j][0]) + _flip(self.y, CHIP_MOVES[j][1])

    def _local(self, k):
        return _Bounce(self.ins[k].at[self.me], self.bufs[k], self.outs[k].at[self.me],
                       self.in_sems.at[k], self.out_sems.at[k])

    def start(self):
        for k in range(self.n):
            self._local(k).start()
        for k, j in self.pairs:
            self._copy(k, j, self._peer(j), self.me).start()

    def turn(self):
        for k in range(self.n):
            self._local(k).turn()

    def finish(self):
        for k, j in self.pairs:
            self._copy(k, j, self.me, self._peer(j)).wait_recv()
        for k, j in self.pairs:
            self._copy(k, j, self._peer(j), self.me).wait_send()
        for k in range(self.n):
            self._local(k).wait()


def _hosted_steps(steps):
    return dict(start=0, turn=steps // 4, forward=steps // 2, finish=steps - 1)


def _pair_share(halves, vec):
    n = len(halves)
    nv = len(DEV_MOVES)

    def body(*refs):
        ins, vec_ref = refs[:n], refs[n]
        outs, vec_out = refs[n + 1:2 * n + 1], refs[2 * n + 1]
        bufs = refs[2 * n + 2:3 * n + 3]
        send_sems, recv_sems, in_sems, out_sems = refs[3 * n + 3:]
        x, y, c = _place()
        dev = 4 * x + 2 * y + c

        def vec_copy(j, slot):
            dx, dy, dc = DEV_MOVES[j]
            return pltpu.make_async_remote_copy(
                src_ref=vec_ref, dst_ref=vec_out.at[slot], send_sem=send_sems.at[n + j], recv_sem=recv_sems.at[n + j],
                device_id=(_flip(x, dx), _flip(y, dy), _flip(c, dc)), device_id_type=MESH_ID)

        def rows(k, hc):
            hr = halves[k].shape[0]
            return outs[k].at[pl.ds(pl.multiple_of(hc * hr, 8), hr), :]

        def share(k, hc):
            return pltpu.make_async_remote_copy(
                src_ref=ins[k], dst_ref=rows(k, hc), send_sem=send_sems.at[k], recv_sem=recv_sems.at[k],
                device_id=(x, y, 1 - c), device_id_type=MESH_ID)

        locs = [_Bounce(vec_ref, bufs[n], vec_out.at[dev], in_sems.at[n], out_sems.at[n])]
        locs += [_Bounce(ins[k], bufs[k], rows(k, c), in_sems.at[k], out_sems.at[k]) for k in range(n)]
        for loc in locs:
            loc.start()
        for j in range(nv):
            vec_copy(j, dev).start()
        for k in range(n):
            share(k, c).start()
        for loc in locs:
            loc.turn()
        for k in range(n):
            share(k, 1 - c).wait_recv()
        for j, (dx, dy, dc) in enumerate(DEV_MOVES):
            vec_copy(j, 4 * _flip(x, dx) + 2 * _flip(y, dy) + _flip(c, dc)).wait_recv()
        for k in range(n):
            share(k, c).wait_send()
        for j in range(nv):
            vec_copy(j, dev).wait_send()
        for loc in locs:
            loc.wait()

    outs = pl.pallas_call(
        body, name="grad_pair_share",
        in_specs=[ANY] * (n + 1), out_specs=[ANY] * (n + 1),
        out_shape=[pltpu.HBM((2 * h.shape[0], h.shape[1]), F32) for h in halves] + [pltpu.HBM((N_DEV,) + vec.shape, F32)],
        scratch_shapes=[pltpu.VMEM(h.shape, F32) for h in halves] + [pltpu.VMEM(vec.shape, F32)]
        + [pltpu.SemaphoreType.DMA((n + nv,))] * 2 + [pltpu.SemaphoreType.DMA((n + 1,))] * 2,
        compiler_params=_comm_params(sum(_nbytes(h.shape, F32) for h in halves) + _nbytes(vec.shape, F32)),
    )(*_in_hbm(list(halves) + [vec]))
    return outs[:n], outs[n]


def _row_block(rows, cols, n_arrays):
    br = rows
    while br % 16 == 0 and 2 * n_arrays * br * cols * 4 > (16 << 20):
        br //= 2
    return br


def _add2(a, b, out_dtype, name):
    rows, cols = a.shape
    br = _row_block(rows, cols, 3)

    def body(a_ref, b_ref, o_ref):
        o_ref[...] = (a_ref[...] + b_ref[...]).astype(out_dtype)

    spec = pl.BlockSpec((br, cols), lambda i: (i, 0))
    return pl.pallas_call(body, name=name, grid=(rows // br,), in_specs=[spec, spec], out_specs=spec,
                          out_shape=pltpu.HBM(a.shape, out_dtype),
                          compiler_params=_cparams(3 * br * cols * 4, 1))(*_in_hbm([a, b]))


def _sum_slots(a, name):
    n, rows, cols = a.shape
    br = _row_block(rows, cols, n + 1)

    def body(a_ref, o_ref):
        acc = a_ref[0].astype(F32)
        for s in range(1, n):
            acc = acc + a_ref[s].astype(F32)
        o_ref[...] = acc

    return pl.pallas_call(body, name=name, grid=(rows // br,),
                          in_specs=[pl.BlockSpec((n, br, cols), lambda i: (0, i, 0))],
                          out_specs=pl.BlockSpec((br, cols), lambda i: (i, 0)),
                          out_shape=pltpu.HBM((rows, cols), F32),
                          compiler_params=_cparams((n + 1) * br * cols * 4, 1))(*_in_hbm([a]))


def _adamw_math(w, g, m, v):
    m = ADAM_B1 * m + (1.0 - ADAM_B1) * g
    v = ADAM_B2 * v + (1.0 - ADAM_B2) * (g * g)
    m_hat = m / (1.0 - ADAM_B1 ** ADAM_STEP)
    v_hat = v / (1.0 - ADAM_B2 ** ADAM_STEP)
    delta = -ADAM_LR * (m_hat / (jnp.sqrt(v_hat) + ADAM_EPS) + ADAM_WD * w)
    return delta, m, v


def _adamw(g, w, m, v, name):
    rows, cols = g.shape
    br = _row_block(rows, cols, 7)

    def body(g_ref, w_ref, m_ref, v_ref, d_ref, nm_ref, nv_ref):
        d_ref[...], nm_ref[...], nv_ref[...] = _adamw_math(w_ref[...], g_ref[...], m_ref[...], v_ref[...])

    spec = pl.BlockSpec((br, cols), lambda i: (i, 0))
    return pl.pallas_call(body, name=name, grid=(rows // br,), in_specs=[spec] * 4, out_specs=[spec] * 3,
                          out_shape=[jax.ShapeDtypeStruct(g.shape, F32)] * 3,
                          compiler_params=_cparams(7 * br * cols * 4, 1))(g, w, m, v)


ROW_FINAL_G, ROW_PE_G, ROW_LOSS = 0, 1, 2
ROW_CONV_B, ROW_CN_G, ROW_CN_B, ROW_B_PW2 = 8, 9, 10, 11
ROW_LN_G = 16
ROW_ONORM_G, ROW_LB = 24, 25
ROW_CONV_W = 32
SMALL = ["ln_g", "conv_b", "cnorm_g", "cnorm_b", "b_pw2", "onorm_g", "pe_norm_g", "final_g"]
SMALL_ROW = dict(ln_g=ROW_LN_G, conv_b=ROW_CONV_B, cnorm_g=ROW_CN_G, cnorm_b=ROW_CN_B, b_pw2=ROW_B_PW2,
                 onorm_g=ROW_ONORM_G, pe_norm_g=ROW_PE_G, final_g=ROW_FINAL_G)


def _adamw_small(vsum, gcw, lb_logits, params):
    names = SMALL + ["lb_logits", "conv_w"]
    flat = [t for nm in names for t in params[nm]]

    def body(*refs):
        vs_ref, gcw_ref, lbl_ref = refs[:3]
        ins = refs[3:3 + 3 * len(names)]
        outs = refs[3 + 3 * len(names):]
        for q, nm in enumerate(names):
            w_ref, m_ref, v_ref = ins[3 * q:3 * q + 3]
            g_ref, d_ref, nm_ref, nv_ref = outs[4 * q:4 * q + 4]
            if nm == "conv_w":
                g = gcw_ref[...]
            elif nm == "lb_logits":
                lb = _softmax_row0(lbl_ref[...])
                g0 = vs_ref[ROW_LB:ROW_LB + 1, :] * lb * (1.0 - lb)
                g = jnp.concatenate([g0, -g0], axis=0)
            else:
                g = vs_ref[SMALL_ROW[nm]:SMALL_ROW[nm] + 1, :]
            g_ref[...] = g
            d_ref[...], nm_ref[...], nv_ref[...] = _adamw_math(w_ref[...], g, m_ref[...], v_ref[...])

    out_shape = [jax.ShapeDtypeStruct(params[nm][0].shape, F32) for nm in names for _ in range(4)]
    outs = pl.pallas_call(body, name="adamw_small", out_shape=out_shape)(vsum, gcw, lb_logits, *flat)
    return {nm: tuple(outs[4 * q:4 * q + 4]) for q, nm in enumerate(names)}


TOKEN_TILE = dict(rmsnorm=512, inproj_fwd=2048, conv=256, hgrn=512, tail=512, inproj_bwd_x=1024, weight_grad=2048)


def _tile(T, family):
    return min(T, TOKEN_TILE[family])


def kernel(x, p, ln_g, w_in, conv_w, conv_b, cnorm_g, cnorm_b, w_pw2, b_pw2, lb_logits, onorm_g, w_out, pe_norm_g, w_pg, w_pp, final_g, loss_target, m_ln_g, m_w_in, m_conv_w, m_conv_b, m_cnorm_g, m_cnorm_b, m_w_pw2, m_b_pw2, m_lb_logits, m_onorm_g, m_w_out, m_pe_norm_g, m_w_pg, m_w_pp, m_final_g, v_ln_g, v_w_in, v_conv_w, v_conv_b, v_cnorm_g, v_cnorm_b, v_w_pw2, v_b_pw2, v_lb_logits, v_onorm_g, v_w_out, v_pe_norm_g, v_w_pg, v_w_pp, v_final_g):
    given = dict(locals())
    x2, p2, tgt = x[0], p[0, 0], loss_target[0]
    T = x2.shape[0]
    fin_g = final_g.reshape(1, D)

    conv_w_pad = jnp.pad(conv_w[0], ((0, HALO - CONV_K), (0, 0)))
    u, (w_in_f,) = _rmsnorm_gather(x2, ln_g, [w_in[0].astype(_MXU)], [1], _tile(T, "rmsnorm"))

    z, (w_pw2_f, w_out_f, w_pg_f, w_pp_f, conv_w_f) = _inproj_fwd(
        u, w_in_f,
        [w_pw2[0].astype(_MXU), w_out[0].astype(_MXU), w_pg[0].astype(_MXU), w_pp[0].astype(_MXU), conv_w_pad],
        [0, 0, 0, 1, 1], _tile(T, "inproj_fwd"))
    yc, y1 = _conv_fwd(z, conv_w_f, conv_b, cnorm_g, cnorm_b, w_pw2_f, b_pw2, _tile(T, "conv"))
    o_raw, yh, s_chunks = _hgrn_fwd(z, lb_logits, onorm_g, _tile(T, "hgrn"), HB)
    dyc, dyh, dh, n2, ds, dpe, dhb, pb, vec_tail = _tail(
        x2, yc, yh, p2, tgt, w_out_f, w_pg_f, w_pp_f, pe_norm_g, fin_g, _tile(T, "tail"))
    tk = _tile(T, "weight_grad")
    g_w_out_c = _tn_matmul(yc, dhb, tk, "grad_w_out_conv")
    g_w_out_h = _tn_matmul(yh, dhb, tk, "grad_w_out_hgrn")
    g_w_pg = _tn_matmul(n2, ds, tk, "grad_w_pg")
    g_w_pp = _tn_matmul(pb, dpe, tk, "grad_w_pp")
    dzc, a_act, dy2, vec_conv, g_conv_w = _conv_bwd(z, y1, dyc, conv_w_f, cnorm_g, cnorm_b, w_pw2_f, b_pw2, _tile(T, "conv"))
    g_w_pw2 = _tn_matmul(a_act, dy2, tk, "grad_w_pw2")

    def pair_sums(names, slabs, tag):
        mine, got = _pair_exchange(slabs, "grad_pair_exchange_" + tag)
        return [_add2(a.reshape(-1, a.shape[-1]), b.reshape(-1, b.shape[-1]), _WIRE, "pair_sum_" + nm).reshape(a.shape)
                for a, b, nm in zip(mine, got, names)]

    rest = ["w_pw2", "w_out", "w_pg", "w_pp"]
    partial_rest = pair_sums(rest, [
        _Slab([g_w_pw2], lambda s: (0, s * (D // N_CHIPS), 0), (D // N_CHIPS, D)),
        _Slab([g_w_out_c, g_w_out_h], lambda s: (s // 2, (s % 2) * (D // 2), 0), (D // 2, D)),
        _Slab([g_w_pg], lambda s: (0, s * (D // N_CHIPS), 0), (D // N_CHIPS, D)),
        _Slab([g_w_pp], lambda s: (0, 0, s * (D // N_CHIPS)), (PLE, D // N_CHIPS)),
    ], "rest")
    dzh, vec_hgrn, slots_rest = _hgrn_bwd(z, lb_logits, onorm_g, o_raw, dyh, s_chunks, partial_rest, _tile(T, "hgrn"), HB)
    g_w_in = _inproj_bwd_w(u, dzc, dzh, tk)
    partial_in = pair_sums(["w_in"], [
        _Slab([g_w_in], lambda s: (0, 0, s * (NPART * D // N_CHIPS)), (D, NPART * D // N_CHIPS))], "w_in")
    grad_x, vec_in, slots_in = _inproj_bwd_x(x2, ln_g, dzc, dzh, w_in_f, dh, partial_in, _tile(T, "inproj_bwd_x"))
    big = ["w_in"] + rest
    halves = [_sum_slots(s, "chip_sum_" + nm) for s, nm in zip(list(slots_in) + list(slots_rest), big)]
    vec = jnp.concatenate([vec_tail, vec_conv, vec_in, vec_hgrn, g_conv_w], axis=0)
    grads_big, vec_slots = _pair_share(halves, vec)
    vsum = _sum_slots(vec_slots, "vec_sum")

    out = {}
    for nm, g in zip(big, grads_big):
        w2, m2, v2 = given[nm][0], given["m_" + nm][0], given["v_" + nm][0]
        d, nm_, nv_ = _adamw(g, w2, m2, v2, "adamw_" + nm)
        out[nm] = tuple(t[None] for t in (g, d, nm_, nv_))
    chip = 2 * lax.axis_index("x") + lax.axis_index("y")
    gcw = lax.dynamic_slice(vsum, (ROW_CONV_W, chip * (D // N_CHIPS)), (CONV_K, D // N_CHIPS))
    params = {nm: (given[nm].reshape(-1, D), given["m_" + nm].reshape(-1, D), given["v_" + nm].reshape(-1, D))
              for nm in SMALL + ["lb_logits"]}
    params["conv_w"] = (conv_w[0], m_conv_w[0], v_conv_w[0])
    small = _adamw_small(vsum, gcw, lb_logits, params)
    for nm, ts in small.items():
        out[nm] = tuple(t.reshape(given[nm].shape) for t in ts)

    loss = vsum[ROW_LOSS, 0]
    order = ["ln_g", "w_in", "conv_w", "conv_b", "cnorm_g", "cnorm_b", "w_pw2", "b_pw2", "lb_logits", "onorm_g",
             "w_out", "pe_norm_g", "w_pg", "w_pp", "final_g"]
    return (loss, grad_x[None], *[out[nm][0] for nm in order], *[out[nm][1] for nm in order],
            *[out[nm][2] for nm in order], *[out[nm][3] for nm in order])
```

```python
import functools

import jax
import jax.numpy as jnp
from jax import lax
from jax.experimental import pallas as pl
from jax.experimental.pallas import tpu as pltpu

F32 = jnp.float32
BF16 = jnp.bfloat16
_MXU = jnp.bfloat16
_WIRE = jnp.bfloat16

D = 1024
NPART = 7
PLE = 256
HEADS = 8
HD = 128
CHUNK = 64
CONV_K = 31
HALO = 32
EPS = 1e-6
N_CHIPS = 4
N_DEV = 8
HB = 8
VEC_ROWS = 64

ADAM_LR = 0.001
ADAM_B1 = 0.9
ADAM_B2 = 0.999
ADAM_EPS = 1e-08
ADAM_WD = 0.01
ADAM_STEP = 10

V7X_VMEM_LIMIT = 60000 * 1024
MESH_ID = pl.DeviceIdType.MESH
ANY = pl.BlockSpec(memory_space=pltpu.HBM)


def _in_hbm(arrays):
    return [pltpu.with_memory_space_constraint(a, pltpu.HBM) for a in arrays]


def _cparams(block_bytes, n_grid_dims):
    limit = min(V7X_VMEM_LIMIT, 2 * block_bytes + (24 << 20))
    return pltpu.CompilerParams(vmem_limit_bytes=int(limit), dimension_semantics=("arbitrary",) * n_grid_dims)


def _nbytes(shape, dtype):
    n = 1
    for s in shape:
        n *= s
    return n * jnp.dtype(dtype).itemsize


def _dot(a, b):
    return jnp.dot(a.astype(_MXU), b.astype(_MXU), preferred_element_type=F32)


def _dot_nt(a, b):
    return lax.dot_general(a.astype(_MXU), b.astype(_MXU), (((1,), (1,)), ((), ())), preferred_element_type=F32)


def _dot_tn(a, b):
    return lax.dot_general(a.astype(_MXU), b.astype(_MXU), (((0,), (0,)), ((), ())), preferred_element_type=F32)


def _tri_dot(tri_bf, x):
    x1 = x.astype(BF16)
    r1 = x - x1.astype(F32)
    x2 = r1.astype(BF16)
    x3 = (r1 - x2.astype(F32)).astype(BF16)
    d = lambda t: jnp.dot(tri_bf, t, preferred_element_type=F32)
    return d(x1) + d(x2) + d(x3)


def _split2(x):
    hi = x.astype(BF16)
    return hi, (x - hi.astype(F32)).astype(BF16)


def _dot3(dims, a, b):
    d = lambda p, q: lax.dot_general(p, q, (dims, ((), ())), preferred_element_type=F32)
    return d(a[0], b[0]) + d(a[0], b[1]) + d(a[1], b[0])


def _sigmoid(x):
    return jax.nn.sigmoid(x)


def _mean_lanes(x):
    return jnp.mean(x, axis=-1, keepdims=True)


def _sum_rows(x):
    return jnp.sum(x, axis=0, keepdims=True)


def _group_ln(y):
    yn, rs = [], []
    for g in range(D // HD):
        blk = y[:, g * HD:(g + 1) * HD]
        xc = blk - _mean_lanes(blk)
        r = lax.rsqrt(_mean_lanes(xc * xc) + EPS)
        yn.append(xc * r)
        rs.append(jnp.broadcast_to(r, blk.shape))
    return jnp.concatenate(yn, axis=1), jnp.concatenate(rs, axis=1)


def _group_ln_bwd(dyn, yn, rstd):
    out = []
    for g in range(D // HD):
        sl = slice(g * HD, (g + 1) * HD)
        d, n = dyn[:, sl], yn[:, sl]
        out.append(rstd[:, sl] * (d - _mean_lanes(d) - n * _mean_lanes(d * n)))
    return jnp.concatenate(out, axis=1)


def _head_means(x, hb, fn=lambda m: m):
    return jnp.concatenate([jnp.broadcast_to(fn(_mean_lanes(x[:, hh * HD:(hh + 1) * HD])), (x.shape[0], HD))
                            for hh in range(hb)], axis=1)


def _head_rsqrt_mean(x, hb):
    return _head_means(x, hb, lambda m: lax.rsqrt(m + EPS))


def _softmax_row0(lbl):
    m = jnp.max(lbl, axis=0, keepdims=True)
    e = jnp.exp(lbl - m)
    return e[0:1, :] / jnp.sum(e, axis=0, keepdims=True)


def _hosted_gather(phases, step, at, shards, axes, ins, outs, bufs, sems):
    gather = _Gather([s.shape for s in shards], axes, ins, outs, bufs, sems)
    for phase in phases:
        pl.when(step == at[phase])(getattr(gather, phase))


def _rmsnorm_gather(x, ln_g, shards, axes, tT):
    T = x.shape[0]
    n = len(shards)
    at = _hosted_steps(T // tT)

    def body(x_ref, g_ref, *rest):
        ins, u_ref, outs, bufs, sems = rest[:n], rest[n], rest[n + 1:2 * n + 1], rest[2 * n + 1:3 * n + 1], rest[3 * n + 1:]
        host = functools.partial(_hosted_gather, step=pl.program_id(0), at=at, shards=shards, axes=axes,
                                 ins=ins, outs=outs, bufs=bufs, sems=sems)
        host(("start", "turn", "forward"))
        xv = x_ref[...]
        r = lax.rsqrt(_mean_lanes(xv * xv) + EPS)
        u_ref[...] = (xv * r * g_ref[...]).astype(_MXU)
        host(("finish",))

    blk = _nbytes((tT, D), F32) * 2 + _nbytes((tT, D), _MXU) + sum(_nbytes(s.shape, s.dtype) for s in shards)
    outs = pl.pallas_call(
        body, name="rmsnorm_gather", grid=(T // tT,),
        in_specs=[pl.BlockSpec((tT, D), lambda i: (i, 0)), pl.BlockSpec((1, D), lambda i: (0, 0))] + [ANY] * n,
        out_specs=[pl.BlockSpec((tT, D), lambda i: (i, 0))] + [ANY] * n,
        out_shape=[jax.ShapeDtypeStruct((T, D), _MXU)]
        + [pltpu.HBM(fs, s.dtype) for fs, s in zip(_full_shapes(shards, axes), shards)],
        scratch_shapes=_Gather.scratch(shards),
        compiler_params=_cparams(blk, 1),
    )(x, ln_g, *_in_hbm(shards))
    return outs[0], outs[1:]


def _inproj_fwd(u, w_in, shards, axes, tT):
    T = u.shape[0]
    n = len(shards)
    at = _hosted_steps((T // tT) * NPART)

    def body(u_ref, w_ref, *rest):
        ins, z_ref, outs, bufs, sems = rest[:n], rest[n], rest[n + 1:2 * n + 1], rest[2 * n + 1:3 * n + 1], rest[3 * n + 1:]
        host = functools.partial(_hosted_gather, step=pl.program_id(0) * NPART + pl.program_id(1), at=at, shards=shards,
                                 axes=axes, ins=ins, outs=outs, bufs=bufs, sems=sems)
        host(("start", "turn", "forward"))
        z_ref[...] = jnp.dot(u_ref[...], w_ref[...], preferred_element_type=F32)
        host(("finish",))

    blk = (_nbytes((tT, D), F32) + _nbytes((D, D), _MXU) + _nbytes((tT, D), _MXU)
           + sum(_nbytes(s.shape, s.dtype) for s in shards))
    outs = pl.pallas_call(
        body, name="inproj_fwd", grid=(T // tT, NPART),
        in_specs=[pl.BlockSpec((tT, D), lambda i, j: (i, 0)), pl.BlockSpec((D, D), lambda i, j: (0, j))] + [ANY] * n,
        out_specs=[pl.BlockSpec((tT, D), lambda i, j: (i, j))] + [ANY] * n,
        out_shape=[jax.ShapeDtypeStruct((T, NPART * D), F32)]
        + [pltpu.HBM(fs, s.dtype) for fs, s in zip(_full_shapes(shards, axes), shards)],
        scratch_shapes=_Gather.scratch(shards),
        compiler_params=_cparams(blk, 2),
    )(u, w_in, *_in_hbm(shards))
    return outs[0], outs[1:]


def _shifted_windows(ext, first, visit):
    n = ext.shape[0]
    for m in range(first, first + CONV_K):
        visit(m, (ext if m == 0 else pltpu.roll(ext, n - m, axis=0))[0:n - HALO, :])


def _conv_fwd(z, conv_w, conv_b, cn_g, cn_b, w_pw2, b_pw2, tT):
    T = z.shape[0]

    def body(cv_ref, cg_ref, ct_ref, cw_ref, cb_ref, ng_ref, nb_ref, wp_ref, bp_ref, yc_ref, y1_ref, ext):
        @pl.when(pl.program_id(0) == 0)
        def _():
            ext[...] = jnp.zeros_like(ext)
        ext[0:HALO, :] = ext[tT:tT + HALO, :]
        ext[HALO:, :] = cv_ref[...] * _sigmoid(cg_ref[...])
        cw = cw_ref[...]
        acc = [cb_ref[...]]

        def tap(m, win):
            acc[0] = acc[0] + win * cw[m - 2:m - 1, :]
        _shifted_windows(ext[...], 2, tap)
        y1 = acc[0]
        y1_ref[...] = y1
        yn, _ = _group_ln(y1)
        apre = yn * ng_ref[...] + nb_ref[...]
        a = apre * _sigmoid(apre)
        y2 = _dot(a, wp_ref[...]) + bp_ref[...]
        ct = ct_ref[...]
        yc_ref[...] = (y2 * (ct * _sigmoid(ct))).astype(_MXU)

    part = lambda p: pl.BlockSpec((tT, D), lambda i: (i, p))
    row = pl.BlockSpec((1, D), lambda i: (0, 0))
    tok = pl.BlockSpec((tT, D), lambda i: (i, 0))
    blk = 4 * _nbytes((tT, D), F32) + _nbytes((D, D), _MXU) + _nbytes((tT, D), _MXU) + 8 * _nbytes((tT + HALO, D), F32)
    return pl.pallas_call(
        body, name="conv_fwd", grid=(T // tT,),
        in_specs=[part(0), part(1), part(2), pl.BlockSpec((HALO, D), lambda i: (0, 0)), row, row, row,
                  pl.BlockSpec((D, D), lambda i: (0, 0)), row],
        out_specs=[tok, tok],
        out_shape=[jax.ShapeDtypeStruct((T, D), _MXU), jax.ShapeDtypeStruct((T, D), F32)],
        scratch_shapes=[pltpu.VMEM((tT + HALO, D), F32)],
        compiler_params=_cparams(blk, 1),
    )(z, z, z, conv_w, conv_b, cn_g, cn_b, w_pw2, b_pw2)


def _hgrn_gates(lb, hq, hf):
    sq = _sigmoid(hq)
    sg = _sigmoid(hf)
    f = lb + (1.0 - lb) * sg
    return sq, sg, f, hq * sq, (1.0 - lb) * (1.0 - sg), jnp.log(f)


def _chunk_decays(lf, q, k):
    r = lax.broadcasted_iota(jnp.int32, (CHUNK, CHUNK), 0)
    c = lax.broadcasted_iota(jnp.int32, (CHUNK, CHUNK), 1)
    b = _tri_dot((r >= c).astype(BF16), lf)
    bm = b[CHUNK // 2 - 1:CHUNK // 2, :]
    bl = b[CHUNK - 1:CHUNK, :]
    eb = jnp.exp(b)
    eqm = jnp.exp(b - bm)
    ekm = jnp.exp(bm - b)
    ekd = jnp.exp(bl - b)
    return dict(causal=r >= c, eb=eb, eqm=eqm, ekm=ekm, ekd=ekd, ebl=jnp.exp(bl),
                qd=q * eb, qm=q * eqm, km=k * ekm, kd=k * ekd)


def _hgrn_fwd(z, lb_logits, onorm_g, tT, hb):
    T = z.shape[0]
    nc = tT // CHUNK
    w = hb * HD

    def body(lbl_ref, og_ref, hq_ref, hf_ref, hi_ref, hg_ref, o_ref, yh_ref, sc_ref, st):
        @pl.when(pl.program_id(1) == 0)
        def _():
            st[...] = jnp.zeros_like(st)
        lb_all = _softmax_row0(lbl_ref[...])
        og_all = og_ref[...]

        def chunk(c, carry):
            sl = pl.ds(pl.multiple_of(c * CHUNK, CHUNK), CHUNK)
            lanes = [slice(hh * HD, (hh + 1) * HD) for hh in range(hb)]
            heads = lambda fn: [fn(hh, ln) for hh, ln in enumerate(lanes)]
            hg, v = hg_ref[sl, :], hi_ref[sl, :]
            _, _, _, q, k, lf = _hgrn_gates(lb_all, hq_ref[sl, :], hf_ref[sl, :])
            dc = _chunk_decays(lf, q, k)
            s_t = heads(lambda hh, ln: st[hh])
            a = heads(lambda hh, ln: jnp.where(dc["causal"], _dot_nt(dc["qm"][:, ln], dc["km"][:, ln]), 0.0))
            o_inter = heads(lambda hh, ln: _dot_nt(dc["qd"][:, ln], s_t[hh]))
            kv = heads(lambda hh, ln: _dot_tn(v[:, ln], dc["kd"][:, ln]))
            o_intra = heads(lambda hh, ln: _dot(a[hh], v[:, ln]))
            for hh, ln in enumerate(lanes):
                sc_ref[hh, c] = s_t[hh]
                st[hh] = s_t[hh] * dc["ebl"][:, ln] + kv[hh]
            o = jnp.concatenate([o_inter[hh] + o_intra[hh] for hh in range(hb)], axis=1)
            o_ref[sl, :] = o
            n = o * _head_rsqrt_mean(o * o, hb)
            yh_ref[sl, :] = ((n * og_all) * (hg * _sigmoid(hg))).astype(_MXU)
            return carry

        lax.fori_loop(0, nc, chunk, 0, unroll=8)

    zpart = lambda p: pl.BlockSpec((tT, w), lambda h, i: (i, p * (HEADS // hb) + h))
    blk = 6 * _nbytes((tT, w), F32) + _nbytes((hb, nc, HD, HD), F32)
    return pl.pallas_call(
        body, name="hgrn_fwd", grid=(HEADS // hb, T // tT),
        in_specs=[pl.BlockSpec((2, w), lambda h, i: (0, h)), pl.BlockSpec((1, w), lambda h, i: (0, h)),
                  zpart(3), zpart(4), zpart(5), zpart(6)],
        out_specs=[pl.BlockSpec((tT, w), lambda h, i: (i, h)), pl.BlockSpec((tT, w), lambda h, i: (i, h)),
                   pl.BlockSpec((hb, nc, HD, HD), lambda h, i: (h, i, 0, 0))],
        out_shape=[jax.ShapeDtypeStruct((T, D), F32), jax.ShapeDtypeStruct((T, D), _MXU),
                   jax.ShapeDtypeStruct((HEADS, T // CHUNK, HD, HD), F32)],
        scratch_shapes=[pltpu.VMEM((hb, HD, HD), F32)],
        compiler_params=_cparams(blk, 2),
    )(lb_logits, onorm_g, z, z, z, z)


def _hgrn_bwd(z, lb_logits, onorm_g, o_raw, dyh, s_chunks, partials, tT, hb):
    T = z.shape[0]
    nc = tT // CHUNK
    nI = T // tT
    w = hb * HD
    n = len(partials)
    at = _hosted_steps((HEADS // hb) * nI)

    def body(lbl_ref, og_ref, hq_ref, hf_ref, hi_ref, hg_ref, o_ref, dy_ref, sc_ref, *rest):
        (dz_ref, vec_ref), dst = rest[n:n + 2], rest[2 * n + 2]
        exchange = _ChipExchange(n, rest[:n], rest[n + 2:2 * n + 2], rest[2 * n + 3:3 * n + 3], rest[3 * n + 3:])
        step = pl.program_id(0) * nI + pl.program_id(1)
        pl.when(step == at["start"])(exchange.start)
        pl.when(step == at["turn"])(exchange.turn)

        @pl.when(pl.program_id(1) == 0)
        def _():
            dst[...] = jnp.zeros_like(dst)
            vec_ref[...] = jnp.zeros_like(vec_ref)
        lb_all = _softmax_row0(lbl_ref[...])
        og_all = og_ref[...]
        last_row = lax.broadcasted_iota(jnp.int32, (CHUNK, w), 0) == CHUNK - 1
        r64 = lax.broadcasted_iota(jnp.int32, (CHUNK, CHUNK), 0)
        c64 = lax.broadcasted_iota(jnp.int32, (CHUNK, CHUNK), 1)
        upper = (c64 >= r64).astype(BF16)
        lanes = [slice(hh * HD, (hh + 1) * HD) for hh in range(hb)]
        heads = lambda fn: [fn(hh, ln) for hh, ln in enumerate(lanes)]
        wide = lambda parts: jnp.concatenate(parts, axis=1)

        def chunk(cc, carry):
            c = nc - 1 - cc
            sl = pl.ds(pl.multiple_of(c * CHUNK, CHUNK), CHUNK)
            hq, hg, v = hq_ref[sl, :], hg_ref[sl, :], hi_ref[sl, :]
            sq, sg, f, q, k, lf = _hgrn_gates(lb_all, hq, hf_ref[sl, :])
            dc = _chunk_decays(lf, q, k)
            s_t = heads(lambda hh, ln: sc_ref[hh, c])
            ds_t = heads(lambda hh, ln: dst[hh])
            o, dy = o_ref[sl, :], dy_ref[sl, :]
            r = _head_rsqrt_mean(o * o, hb)
            n = o * r
            sgg = _sigmoid(hg)
            silu_g = hg * sgg
            dhg = dy * (n * og_all) * (sgg * (1.0 + hg * (1.0 - sgg)))
            dn = dy * og_all * silu_g
            g_og = _sum_rows(dy * n * silu_g)
            do = r * (dn - n * _head_means(dn * n, hb))
            a = heads(lambda hh, ln: jnp.where(dc["causal"], _dot_nt(dc["qm"][:, ln], dc["km"][:, ln]), 0.0))
            dam = heads(lambda hh, ln: jnp.where(dc["causal"], _dot_nt(do[:, ln], v[:, ln]), 0.0))
            dqd = wide(heads(lambda hh, ln: _dot(do[:, ln], s_t[hh])))
            dkd = wide(heads(lambda hh, ln: _dot(v[:, ln], ds_t[hh])))
            dv_inter = heads(lambda hh, ln: _dot_nt(dc["kd"][:, ln], ds_t[hh]))
            dqs = heads(lambda hh, ln: _dot_tn(do[:, ln], dc["qd"][:, ln]))
            dv = wide(heads(lambda hh, ln: _dot_tn(a[hh], do[:, ln]) + dv_inter[hh]))
            dam2 = [_split2(t) for t in dam]
            km2, qm2 = _split2(dc["km"]), _split2(dc["qm"])
            dqm = wide(heads(lambda hh, ln: _dot3(((1,), (0,)), dam2[hh], (km2[0][:, ln], km2[1][:, ln]))))
            dkm = wide(heads(lambda hh, ln: _dot3(((0,), (0,)), dam2[hh], (qm2[0][:, ln], qm2[1][:, ln]))))
            debl = wide(heads(lambda hh, ln: _sum_rows(ds_t[hh] * s_t[hh])))
            for hh, ln in enumerate(lanes):
                dst[hh] = ds_t[hh] * dc["ebl"][:, ln] + dqs[hh]
            dq = dqd * dc["eb"] + dqm * dc["eqm"]
            dk = dkm * dc["ekm"] + dkd * dc["ekd"]
            dbl = _sum_rows(dkd * dc["kd"]) + debl * dc["ebl"]
            db = dq * q - dk * k + jnp.where(last_row, dbl, 0.0)
            dlf = _tri_dot(upper, db)
            dfk = dlf / f - dk
            dz_ref[0, sl, :] = (dq * (sq * (1.0 + hq * (1.0 - sq)))).astype(_MXU)
            dz_ref[1, sl, :] = (dfk * ((1.0 - lb_all) * sg * (1.0 - sg))).astype(_MXU)
            dz_ref[2, sl, :] = dv.astype(_MXU)
            dz_ref[3, sl, :] = dhg.astype(_MXU)
            vec_ref[0:1, :] += g_og
            vec_ref[1:2, :] += _sum_rows(dfk * (1.0 - sg))
            return carry

        lax.fori_loop(0, nc, chunk, 0, unroll=8)
        pl.when(step == at["finish"])(exchange.finish)

    zpart = lambda p: pl.BlockSpec((tT, w), lambda h, i: (nI - 1 - i, p * (HEADS // hb) + h))
    act = pl.BlockSpec((tT, w), lambda h, i: (nI - 1 - i, h))
    blk = (6 * _nbytes((tT, w), F32) + _nbytes((hb, nc, HD, HD), F32) + 4 * _nbytes((tT, w), _MXU)
           + _ChipExchange.scratch_bytes(partials))
    outs = pl.pallas_call(
        body, name="hgrn_bwd", grid=(HEADS // hb, nI),
        in_specs=[pl.BlockSpec((2, w), lambda h, i: (0, h)), pl.BlockSpec((1, w), lambda h, i: (0, h)),
                  zpart(3), zpart(4), zpart(5), zpart(6), act, act,
                  pl.BlockSpec((hb, nc, HD, HD), lambda h, i: (h, nI - 1 - i, 0, 0))] + [ANY] * n,
        out_specs=[pl.BlockSpec((4, tT, w), lambda h, i: (0, nI - 1 - i, h)),
                   pl.BlockSpec((8, w), lambda h, i: (0, h))] + [ANY] * n,
        out_shape=[jax.ShapeDtypeStruct((4, T, D), _MXU), jax.ShapeDtypeStruct((8, D), F32)]
        + [pltpu.HBM(p.shape, p.dtype) for p in partials],
        scratch_shapes=[pltpu.VMEM((hb, HD, HD), F32)] + _ChipExchange.scratch(partials),
        compiler_params=_cparams(blk, 2),
    )(lb_logits, onorm_g, z, z, z, z, o_raw, dyh, s_chunks, *_in_hbm(partials))
    return outs[0], outs[1], outs[2:]


def _tail(x, yc, yh, p, target, w_out, w_pg, w_pp, pe_g, fin_g, tT):
    T = x.shape[0]

    def body(x_ref, yc_ref, yh_ref, p_ref, t_ref, wo_ref, wg_ref, wp_ref, pg_ref, fg_ref,
             dyc_ref, dyh_ref, dh_ref, n2_ref, ds_ref, dpe_ref, dhb_ref, pb_ref, vec_ref):
        @pl.when(pl.program_id(0) == 0)
        def _():
            vec_ref[...] = jnp.zeros_like(vec_ref)
        wo_c, wo_h = wo_ref[0:D, :], wo_ref[D:2 * D, :]
        h = x_ref[...] + _dot(yc_ref[...], wo_c) + _dot(yh_ref[...], wo_h)
        pb = p_ref[...].astype(_MXU)
        pe = _dot(pb, wp_ref[...])
        r2 = lax.rsqrt(_mean_lanes(h * h) + EPS)
        hn = h * r2
        n2 = (hn * pg_ref[...]).astype(_MXU)
        gate = _sigmoid(_dot(n2, wg_ref[...]))
        h2 = h + gate * pe
        r3 = lax.rsqrt(_mean_lanes(h2 * h2) + EPS)
        h2n = h2 * r3
        err = h2n * fg_ref[...] - t_ref[...]
        vec_ref[ROW_LOSS:ROW_LOSS + 1, :] += 0.5 * jnp.sum(_mean_lanes(err * err))
        dout = err * (1.0 / D)
        vec_ref[0:1, :] += _sum_rows(dout * h2n)
        dn3 = dout * fg_ref[...]
        dh2 = r3 * (dn3 - h2n * _mean_lanes(dn3 * h2n))
        ds = (dh2 * pe * gate * (1.0 - gate)).astype(_MXU)
        dn2 = _dot_nt(ds, wg_ref[...])
        vec_ref[1:2, :] += _sum_rows(dn2 * hn)
        dnn = dn2 * pg_ref[...]
        dh = dh2 + r2 * (dnn - hn * _mean_lanes(dnn * hn))
        dhb = dh.astype(_MXU)
        dyc_ref[...] = _dot_nt(dhb, wo_c)
        dyh_ref[...] = _dot_nt(dhb, wo_h)
        dh_ref[...] = dh
        n2_ref[...] = n2
        ds_ref[...] = ds
        dpe_ref[...] = (dh2 * gate).astype(_MXU)
        dhb_ref[...] = dhb
        pb_ref[...] = pb

    tok = lambda w: pl.BlockSpec((tT, w), lambda i: (i, 0))
    full = lambda r, c: pl.BlockSpec((r, c), lambda i: (0, 0))
    tokshape = lambda w, dt: jax.ShapeDtypeStruct((T, w), dt)
    blk = (5 * _nbytes((tT, D), F32) + 7 * _nbytes((tT, D), _MXU) + _nbytes((4 * D + PLE, D), _MXU)
           + 12 * _nbytes((tT, D), F32))
    return pl.pallas_call(
        body, name="tail_fwd_bwd", grid=(T // tT,),
        in_specs=[tok(D), tok(D), tok(D), tok(PLE), tok(D), full(2 * D, D), full(D, D), full(PLE, D), full(1, D), full(1, D)],
        out_specs=[tok(D), tok(D), tok(D), tok(D), tok(D), tok(D), tok(D), tok(PLE), full(8, D)],
        out_shape=[tokshape(D, F32), tokshape(D, F32), tokshape(D, F32), tokshape(D, _MXU), tokshape(D, _MXU),
                   tokshape(D, _MXU), tokshape(D, _MXU), tokshape(PLE, _MXU),
                   jax.ShapeDtypeStruct((8, D), F32)],
        compiler_params=_cparams(blk, 1),
    )(x, yc, yh, p, target, w_out, w_pg, w_pp, pe_g, fin_g)


def _conv_bwd(z, y1, dyc, conv_w, cn_g, cn_b, w_pw2, b_pw2, tT):
    T = z.shape[0]
    nI = T // tT
    hb = tT // HALO

    def body(cv_ref, cg_ref, ct_ref, hv_ref, hg_ref, y1_ref, dyc_ref, cw_ref, ng_ref, nb_ref, wp_ref, bp_ref,
             dz_ref, a_ref, dy2_ref, vec_ref, gcw_ref, ext, ext2, gpart):
        i = pl.program_id(0)

        @pl.when(i == 0)
        def _():
            ext2[...] = jnp.zeros_like(ext2)
            gpart[...] = jnp.zeros_like(gpart)
            vec_ref[...] = jnp.zeros_like(vec_ref)
        cv, cg, ct = cv_ref[...], cg_ref[...], ct_ref[...]
        sg = _sigmoid(cg)
        has_hist = (i < nI - 1).astype(F32)
        ext[0:HALO, :] = hv_ref[...] * _sigmoid(hg_ref[...]) * has_hist
        ext[HALO:, :] = cv * sg
        yn, rstd = _group_ln(y1_ref[...])
        apre = yn * ng_ref[...] + nb_ref[...]
        sa = _sigmoid(apre)
        a = (apre * sa).astype(_MXU)
        y2 = _dot(a, wp_ref[...]) + bp_ref[...]
        st = _sigmoid(ct)
        dyc_v = dyc_ref[...]
        dy2 = dyc_v * (ct * st)
        dy2b = dy2.astype(_MXU)
        da = _dot_nt(dy2b, wp_ref[...])
        dapre = da * (sa * (1.0 + apre * (1.0 - sa)))
        dy1 = _group_ln_bwd(dapre * ng_ref[...], yn, rstd)
        vec_ref[0:1, :] += _sum_rows(dy1)
        vec_ref[1:2, :] += _sum_rows(dapre * yn)
        vec_ref[2:3, :] += _sum_rows(dapre)
        vec_ref[3:4, :] += _sum_rows(dy2)
        dz_ref[2] = (dyc_v * y2 * (st * (1.0 + ct * (1.0 - st)))).astype(_MXU)
        a_ref[...] = a
        dy2_ref[...] = dy2b
        ext2[tT:tT + HALO, :] = ext2[0:HALO, :]
        ext2[0:tT, :] = dy1
        def grad_tap(m, win):
            p = dy1 * win
            part = p[0:8, :]
            for q in range(1, tT // 8):
                part = part + p[8 * q:8 * q + 8, :]
            gpart[m - 2] += part
        _shifted_windows(ext[...], 2, grad_tap)
        cw = cw_ref[...]
        acc = [None]

        def dv_tap(m, win):
            term = win * cw[CONV_K - 1 - m:CONV_K - m, :]
            acc[0] = term if acc[0] is None else acc[0] + term
        _shifted_windows(ext2[...], 0, dv_tap)
        dv = acc[0]
        dz_ref[0] = (dv * sg).astype(_MXU)
        dz_ref[1] = (dv * cv * sg * (1.0 - sg)).astype(_MXU)

        @pl.when(i == nI - 1)
        def _():
            gcw_ref[...] = jnp.sum(gpart[...], axis=1)

    part = lambda p: pl.BlockSpec((tT, D), lambda i: (nI - 1 - i, p))
    hist = lambda p: pl.BlockSpec((HALO, D), lambda i: (jnp.maximum((nI - 1 - i) * hb - 1, 0), p))
    tok = pl.BlockSpec((tT, D), lambda i: (nI - 1 - i, 0))
    row = pl.BlockSpec((1, D), lambda i: (0, 0))
    blk = (5 * _nbytes((tT, D), F32) + _nbytes((D, D), _MXU) + 5 * _nbytes((tT, D), _MXU)
           + 10 * _nbytes((tT + HALO, D), F32))
    return pl.pallas_call(
        body, name="conv_bwd", grid=(nI,),
        in_specs=[part(0), part(1), part(2), hist(0), hist(1), tok, tok, pl.BlockSpec((HALO, D), lambda i: (0, 0)),
                  row, row, pl.BlockSpec((D, D), lambda i: (0, 0)), row],
        out_specs=[pl.BlockSpec((3, tT, D), lambda i: (0, nI - 1 - i, 0)), tok, tok,
                   pl.BlockSpec((8, D), lambda i: (0, 0)), pl.BlockSpec((HALO, D), lambda i: (0, 0))],
        out_shape=[jax.ShapeDtypeStruct((3, T, D), _MXU), jax.ShapeDtypeStruct((T, D), _MXU),
                   jax.ShapeDtypeStruct((T, D), _MXU), jax.ShapeDtypeStruct((8, D), F32),
                   jax.ShapeDtypeStruct((HALO, D), F32)],
        scratch_shapes=[pltpu.VMEM((tT + HALO, D), F32), pltpu.VMEM((tT + HALO, D), F32), pltpu.VMEM((HALO, 8, D), F32)],
        compiler_params=_cparams(blk, 1),
    )(z, z, z, z, z, y1, dyc, conv_w, cn_g, cn_b, w_pw2, b_pw2)


def _inproj_bwd_u(dzc, dzh, w_in, partials, tT):
    T = dzc.shape[1]
    n = len(partials)
    at = _hosted_steps((T // tT) * NPART)

    def body(dzc_ref, dzh_ref, w_ref, *rest):
        du_ref = rest[n]
        exchange = _ChipExchange(n, rest[:n], rest[n + 1:2 * n + 1], rest[2 * n + 1:3 * n + 1], rest[3 * n + 1:])
        j = pl.program_id(1)
        step = pl.program_id(0) * NPART + j
        pl.when(step == at["start"])(exchange.start)
        pl.when(step == at["turn"])(exchange.turn)

        @pl.when(j == 0)
        def _():
            du_ref[...] = jnp.zeros_like(du_ref)

        @pl.when(j < 3)
        def _():
            du_ref[...] += _dot_nt(dzc_ref[0], w_ref[...])

        @pl.when(j >= 3)
        def _():
            du_ref[...] += _dot_nt(dzh_ref[0], w_ref[...])
        pl.when(step == at["finish"])(exchange.finish)

    blk = (2 * _nbytes((tT, D), _MXU) + _nbytes((D, D), _MXU) + 2 * _nbytes((tT, D), F32)
           + _ChipExchange.scratch_bytes(partials))
    outs = pl.pallas_call(
        body, name="inproj_bwd_u", grid=(T // tT, NPART),
        in_specs=[pl.BlockSpec((1, tT, D), lambda i, j: (jnp.minimum(j, 2), i, 0)),
                  pl.BlockSpec((1, tT, D), lambda i, j: (jnp.maximum(j - 3, 0), i, 0)),
                  pl.BlockSpec((D, D), lambda i, j: (0, j))] + [ANY] * n,
        out_specs=[pl.BlockSpec((tT, D), lambda i, j: (i, 0))] + [ANY] * n,
        out_shape=[jax.ShapeDtypeStruct((T, D), F32)] + [pltpu.HBM(p.shape, p.dtype) for p in partials],
        scratch_shapes=_ChipExchange.scratch(partials),
        compiler_params=_cparams(blk, 2),
    )(dzc, dzh, w_in, *_in_hbm(partials))
    return outs[0], outs[1:]


def _inproj_bwd_x(x, ln_g, du, dh, tT):
    T = x.shape[0]

    def body(x_ref, g_ref, du_ref, dh_ref, gx_ref, vec_ref):
        @pl.when(pl.program_id(0) == 0)
        def _():
            vec_ref[...] = jnp.zeros_like(vec_ref)
        xv = x_ref[...]
        r = lax.rsqrt(_mean_lanes(xv * xv) + EPS)
        xn = xv * r
        duv = du_ref[...]
        vec_ref[0:1, :] += _sum_rows(duv * xn)
        dun = duv * g_ref[...]
        gx_ref[...] = dh_ref[...] + r * (dun - xn * _mean_lanes(dun * xn))

    tok = pl.BlockSpec((tT, D), lambda i: (i, 0))
    return pl.pallas_call(
        body, name="inproj_bwd_x", grid=(T // tT,),
        in_specs=[tok, pl.BlockSpec((1, D), lambda i: (0, 0)), tok, tok],
        out_specs=[tok, pl.BlockSpec((8, D), lambda i: (0, 0))],
        out_shape=[jax.ShapeDtypeStruct((T, D), F32), jax.ShapeDtypeStruct((8, D), F32)],
        compiler_params=_cparams(6 * _nbytes((tT, D), F32), 1),
    )(x, ln_g, du, dh)


def _inproj_bwd_w(u, dzc, dzh, tk):
    T = u.shape[0]
    nK = T // tk

    def body(u_ref, dzc_ref, dzh_ref, gw_ref):
        j, k = pl.program_id(0), pl.program_id(1)

        @pl.when(k == 0)
        def _():
            gw_ref[...] = jnp.zeros_like(gw_ref)

        @pl.when(j < 3)
        def _():
            gw_ref[...] += _dot_tn(u_ref[...], dzc_ref[0])

        @pl.when(j >= 3)
        def _():
            gw_ref[...] += _dot_tn(u_ref[...], dzh_ref[0])

    blk = 3 * _nbytes((tk, D), _MXU) + 2 * _nbytes((D, D), F32)
    return pl.pallas_call(
        body, name="inproj_bwd_w", grid=(NPART, nK),
        in_specs=[pl.BlockSpec((tk, D), lambda j, k: (k, 0)),
                  pl.BlockSpec((1, tk, D), lambda j, k: (jnp.minimum(j, 2), jnp.where(j < 3, k, nK - 1), 0)),
                  pl.BlockSpec((1, tk, D), lambda j, k: (jnp.maximum(j - 3, 0), jnp.where(j < 3, 0, k), 0))],
        out_specs=pl.BlockSpec((D, D), lambda j, k: (0, j)),
        out_shape=jax.ShapeDtypeStruct((D, NPART * D), F32),
        compiler_params=_cparams(blk, 2),
    )(u, dzc, dzh)


def _tn_matmul(a, b, tk, name):
    T, M = a.shape
    N = b.shape[1]

    def body(a_ref, b_ref, o_ref):
        @pl.when(pl.program_id(0) == 0)
        def _():
            o_ref[...] = jnp.zeros_like(o_ref)
        o_ref[...] += _dot_tn(a_ref[...], b_ref[...])

    blk = _nbytes((tk, M), _MXU) + _nbytes((tk, N), _MXU) + 2 * _nbytes((M, N), F32)
    return pl.pallas_call(
        body, name=name, grid=(T // tk,),
        in_specs=[pl.BlockSpec((tk, M), lambda k: (k, 0)), pl.BlockSpec((tk, N), lambda k: (k, 0))],
        out_specs=pl.BlockSpec((M, N), lambda k: (0, 0)),
        out_shape=pltpu.HBM((M, N), F32),
        compiler_params=_cparams(blk, 1),
    )(a, b)


def _place():
    return lax.axis_index("x"), lax.axis_index("y"), lax.axis_index("c")


def _flip(v, d):
    return 1 - v if d else v


CHIP_MOVES = [(1, 0), (0, 1), (1, 1)]
DEV_MOVES = [(dx, dy, dc) for dx in (0, 1) for dy in (0, 1) for dc in (0, 1)][1:]


def _shard_slice(ref, axis, size, s):
    start = pl.multiple_of(s * size, size)
    return ref.at[pl.ds(start, size), :] if axis == 0 else ref.at[:, pl.ds(start, size)]


class _Bounce:
    def __init__(self, src, buf, dst, sem_in, sem_out):
        self.load = pltpu.make_async_copy(src, buf, sem_in)
        self.store = pltpu.make_async_copy(buf, dst, sem_out)

    def start(self):
        self.load.start()

    def turn(self):
        self.load.wait()
        self.store.start()

    def wait(self):
        self.store.wait()


def _comm_params(scratch_bytes):
    return pltpu.CompilerParams(vmem_limit_bytes=int(min(V7X_VMEM_LIMIT, scratch_bytes + (8 << 20))))


class _Gather:
    def __init__(self, shapes, axes, ins, outs, bufs, sems):
        self.shapes, self.axes, self.ins, self.outs, self.bufs = shapes, axes, ins, outs, bufs
        self.ici_send, self.ici_recv, self.d2d_send, self.d2d_recv, self.in_sems, self.out_sems = sems
        self.x, self.y, self.c = _place()
        self.me = 2 * self.x + self.y
        self.pairs = [(k, j) for k in range(len(shapes)) for j in range(3)]

    @staticmethod
    def scratch(shards):
        n = len(shards)
        return ([pltpu.VMEM(s.shape, s.dtype) for s in shards]
                + [pltpu.SemaphoreType.DMA((3 * n,))] * 4 + [pltpu.SemaphoreType.DMA((n,))] * 2)

    def _own_half(self, k, hc):
        half = self.shapes[k][0] // 2
        return self.ins[k].at[pl.ds(pl.multiple_of(hc * half, 16), half), :]

    def _region(self, k, who, hc):
        rows, cols = self.shapes[k]
        half = rows // 2
        if self.axes[k] == 0:
            return self.outs[k].at[pl.ds(pl.multiple_of(who * rows + hc * half, 16), half), :]
        return self.outs[k].at[pl.ds(pl.multiple_of(hc * half, 16), half), pl.ds(pl.multiple_of(who * cols, HD), cols)]

    def _peer(self, j):
        return 2 * _flip(self.x, CHIP_MOVES[j][0]) + _flip(self.y, CHIP_MOVES[j][1])

    def _ici(self, k, j, who, hc):
        dx, dy = CHIP_MOVES[j]
        return pltpu.make_async_remote_copy(
            src_ref=self._own_half(k, hc), dst_ref=self._region(k, who, hc),
            send_sem=self.ici_send.at[3 * k + j], recv_sem=self.ici_recv.at[3 * k + j],
            device_id=(_flip(self.x, dx), _flip(self.y, dy), self.c), device_id_type=MESH_ID)

    def _d2d(self, k, j, who, hc):
        return pltpu.make_async_remote_copy(
            src_ref=self._region(k, who, hc), dst_ref=self._region(k, who, hc),
            send_sem=self.d2d_send.at[3 * k + j], recv_sem=self.d2d_recv.at[3 * k + j],
            device_id=(self.x, self.y, 1 - self.c), device_id_type=MESH_ID)

    def _local(self, k):
        size = self.shapes[k][self.axes[k]]
        return _Bounce(self.ins[k], self.bufs[k], _shard_slice(self.outs[k], self.axes[k], size, self.me),
                       self.in_sems.at[k], self.out_sems.at[k])

    def start(self):
        for k in range(len(self.shapes)):
            self._local(k).start()
        for k, j in self.pairs:
            self._ici(k, j, self.me, self.c).start()

    def turn(self):
        for k in range(len(self.shapes)):
            self._local(k).turn()

    def forward(self):
        for k, j in self.pairs:
            self._ici(k, j, self._peer(j), self.c).wait_recv()
            self._d2d(k, j, self._peer(j), self.c).start()

    def finish(self):
        for k, j in self.pairs:
            self._d2d(k, j, self._peer(j), 1 - self.c).wait_recv()
        for k, j in self.pairs:
            self._ici(k, j, self.me, self.c).wait_send()
            self._d2d(k, j, self._peer(j), self.c).wait_send()
        for k in range(len(self.shapes)):
            self._local(k).wait()


def _full_shapes(shards, axes):
    return [tuple(d * (N_CHIPS if a == ax else 1) for a, d in enumerate(s.shape)) for s, ax in zip(shards, axes)]


class _Slab:
    def __init__(self, arrays, pick, shard_shape):
        self.arrays = arrays
        self.pick = pick
        self.rows, self.cols = shard_shape
        self.half = self.rows // 2


def _pair_exchange(slabs, name):
    n = len(slabs)
    n_in = sum(len(sl.arrays) for sl in slabs)

    def body(*refs):
        ins = refs[:n_in]
        mine, got = refs[n_in:n_in + n], refs[n_in + n:n_in + 2 * n]
        bufs = refs[n_in + 2 * n:n_in + 3 * n]
        send_sems, recv_sems, in_sems, out_sems = refs[n_in + 3 * n:]
        x, y, c = _place()
        started = []
        base = 0
        for k, sl in enumerate(slabs):
            for s in range(N_CHIPS):
                ai, r0, c0 = sl.pick(s)
                src = ins[base + ai]

                def half(hc):
                    return src.at[pl.ds(pl.multiple_of(r0 + hc * sl.half, 8), sl.half), pl.ds(c0, sl.cols)]
                q = N_CHIPS * k + s
                loc = _Bounce(half(c), bufs[k].at[s], mine[k].at[s], in_sems.at[q], out_sems.at[q])
                loc.start()
                cp = pltpu.make_async_remote_copy(
                    src_ref=half(1 - c), dst_ref=got[k].at[s], send_sem=send_sems.at[q], recv_sem=recv_sems.at[q],
                    device_id=(x, y, 1 - c), device_id_type=MESH_ID)
                cp.start()
                started.append((loc, cp))
            base += len(sl.arrays)
        for loc, cp in started:
            loc.turn()
        for loc, cp in started:
            cp.wait_recv()
        for loc, cp in started:
            cp.wait_send()
            loc.wait()

    flat_in = [a for sl in slabs for a in sl.arrays]
    compact = [pltpu.HBM((N_CHIPS, sl.half, sl.cols), F32) for sl in slabs]
    outs = pl.pallas_call(
        body, name=name,
        in_specs=[ANY] * n_in, out_specs=[ANY] * (2 * n), out_shape=compact + compact,
        scratch_shapes=[pltpu.VMEM(s.shape, F32) for s in compact] + [pltpu.SemaphoreType.DMA((N_CHIPS * n,))] * 4,
        compiler_params=_comm_params(sum(_nbytes(s.shape, F32) for s in compact)),
    )(*_in_hbm(flat_in))
    return outs[:n], outs[n:]


class _ChipExchange:
    def __init__(self, n, ins, outs, bufs, sems):
        self.n, self.ins, self.outs, self.bufs = n, ins, outs, bufs
        self.send_sems, self.recv_sems, self.in_sems, self.out_sems = sems
        self.x, self.y, self.c = _place()
        self.me = 2 * self.x + self.y
        self.pairs = [(k, j) for k in range(n) for j in range(3)]

    @staticmethod
    def scratch(partials):
        n = len(partials)
        return ([pltpu.VMEM(p.shape[1:], p.dtype) for p in partials]
                + [pltpu.SemaphoreType.DMA((3 * n,))] * 2 + [pltpu.SemaphoreType.DMA((n,))] * 2)

    @staticmethod
    def scratch_bytes(partials):
        return sum(_nbytes(p.shape[1:], p.dtype) for p in partials)

    def _copy(self, k, j, src_slot, dst_slot):
        px, py = _flip(self.x, CHIP_MOVES[j][0]), _flip(self.y, CHIP_MOVES[j][1])
        return pltpu.make_async_remote_copy(
            src_ref=self.ins[k].at[src_slot], dst_ref=self.outs[k].at[dst_slot],
            send_sem=self.send_sems.at[3 * k + j], recv_sem=self.recv_sems.at[3 * k + j],
            device_id=(px, py, self.c), device_id_type=MESH_ID)

    def _peer(self, j):
        return 2 * _flip(self.x, CHIP_MOVES[j][0]) + _flip(self.y, CHIP_MOVES[j][1])

    def _local(self, k):
        return _Bounce(self.ins[k].at[self.me], self.bufs[k], self.outs[k].at[self.me],
                       self.in_sems.at[k], self.out_sems.at[k])

    def start(self):
        for k in range(self.n):
            self._local(k).start()
        for k, j in self.pairs:
            self._copy(k, j, self._peer(j), self.me).start()

    def turn(self):
        for k in range(self.n):
            self._local(k).turn()

    def finish(self):
        for k, j in self.pairs:
            self._copy(k, j, self.me, self._peer(j)).wait_recv()
        for k, j in self.pairs:
            self._copy(k, j, self._peer(j), self.me).wait_send()
        for k in range(self.n):
            self._local(k).wait()


def _hosted_steps(steps):
    return dict(start=0, turn=steps // 4, forward=steps // 2, finish=steps - 1)


def _pair_share(halves, vec):
    n = len(halves)
    nv = len(DEV_MOVES)

    def body(*refs):
        ins, vec_ref = refs[:n], refs[n]
        outs, vec_out = refs[n + 1:2 * n + 1], refs[2 * n + 1]
        bufs = refs[2 * n + 2:3 * n + 3]
        send_sems, recv_sems, in_sems, out_sems = refs[3 * n + 3:]
        x, y, c = _place()
        dev = 4 * x + 2 * y + c

        def vec_copy(j, slot):
            dx, dy, dc = DEV_MOVES[j]
            return pltpu.make_async_remote_copy(
                src_ref=vec_ref, dst_ref=vec_out.at[slot], send_sem=send_sems.at[n + j], recv_sem=recv_sems.at[n + j],
                device_id=(_flip(x, dx), _flip(y, dy), _flip(c, dc)), device_id_type=MESH_ID)

        def rows(k, hc):
            hr = halves[k].shape[0]
            return outs[k].at[pl.ds(pl.multiple_of(hc * hr, 8), hr), :]

        def share(k, hc):
            return pltpu.make_async_remote_copy(
                src_ref=ins[k], dst_ref=rows(k, hc), send_sem=send_sems.at[k], recv_sem=recv_sems.at[k],
                device_id=(x, y, 1 - c), device_id_type=MESH_ID)

        locs = [_Bounce(vec_ref, bufs[n], vec_out.at[dev], in_sems.at[n], out_sems.at[n])]
        locs += [_Bounce(ins[k], bufs[k], rows(k, c), in_sems.at[k], out_sems.at[k]) for k in range(n)]
        for loc in locs:
            loc.start()
        for j in range(nv):
            vec_copy(j, dev).start()
        for k in range(n):
            share(k, c).start()
        for loc in locs:
            loc.turn()
        for k in range(n):
            share(k, 1 - c).wait_recv()
        for j, (dx, dy, dc) in enumerate(DEV_MOVES):
            vec_copy(j, 4 * _flip(x, dx) + 2 * _flip(y, dy) + _flip(c, dc)).wait_recv()
        for k in range(n):
            share(k, c).wait_send()
        for j in range(nv):
            vec_copy(j, dev).wait_send()
        for loc in locs:
            loc.wait()

    outs = pl.pallas_call(
        body, name="grad_pair_share",
        in_specs=[ANY] * (n + 1), out_specs=[ANY] * (n + 1),
        out_shape=[pltpu.HBM((2 * h.shape[0], h.shape[1]), F32) for h in halves] + [pltpu.HBM((N_DEV,) + vec.shape, F32)],
        scratch_shapes=[pltpu.VMEM(h.shape, F32) for h in halves] + [pltpu.VMEM(vec.shape, F32)]
        + [pltpu.SemaphoreType.DMA((n + nv,))] * 2 + [pltpu.SemaphoreType.DMA((n + 1,))] * 2,
        compiler_params=_comm_params(sum(_nbytes(h.shape, F32) for h in halves) + _nbytes(vec.shape, F32)),
    )(*_in_hbm(list(halves) + [vec]))
    return outs[:n], outs[n]


def _row_block(rows, cols, n_arrays):
    br = rows
    while br % 16 == 0 and 2 * n_arrays * br * cols * 4 > (16 << 20):
        br //= 2
    return br


def _add2(a, b, out_dtype, name):
    rows, cols = a.shape
    br = _row_block(rows, cols, 3)

    def body(a_ref, b_ref, o_ref):
        o_ref[...] = (a_ref[...] + b_ref[...]).astype(out_dtype)

    spec = pl.BlockSpec((br, cols), lambda i: (i, 0))
    return pl.pallas_call(body, name=name, grid=(rows // br,), in_specs=[spec, spec], out_specs=spec,
                          out_shape=pltpu.HBM(a.shape, out_dtype),
                          compiler_params=_cparams(3 * br * cols * 4, 1))(*_in_hbm([a, b]))


def _sum_slots(a, name):
    n, rows, cols = a.shape
    br = _row_block(rows, cols, n + 1)

    def body(a_ref, o_ref):
        acc = a_ref[0].astype(F32)
        for s in range(1, n):
            acc = acc + a_ref[s].astype(F32)
        o_ref[...] = acc

    return pl.pallas_call(body, name=name, grid=(rows // br,),
                          in_specs=[pl.BlockSpec((n, br, cols), lambda i: (0, i, 0))],
                          out_specs=pl.BlockSpec((br, cols), lambda i: (i, 0)),
                          out_shape=pltpu.HBM((rows, cols), F32),
                          compiler_params=_cparams((n + 1) * br * cols * 4, 1))(*_in_hbm([a]))


def _adamw_math(w, g, m, v):
    m = ADAM_B1 * m + (1.0 - ADAM_B1) * g
    v = ADAM_B2 * v + (1.0 - ADAM_B2) * (g * g)
    m_hat = m / (1.0 - ADAM_B1 ** ADAM_STEP)
    v_hat = v / (1.0 - ADAM_B2 ** ADAM_STEP)
    delta = -ADAM_LR * (m_hat / (jnp.sqrt(v_hat) + ADAM_EPS) + ADAM_WD * w)
    return delta, m, v


def _adamw(g, w, m, v, name):
    rows, cols = g.shape
    br = _row_block(rows, cols, 7)

    def body(g_ref, w_ref, m_ref, v_ref, d_ref, nm_ref, nv_ref):
        d_ref[...], nm_ref[...], nv_ref[...] = _adamw_math(w_ref[...], g_ref[...], m_ref[...], v_ref[...])

    spec = pl.BlockSpec((br, cols), lambda i: (i, 0))
    return pl.pallas_call(body, name=name, grid=(rows // br,), in_specs=[spec] * 4, out_specs=[spec] * 3,
                          out_shape=[jax.ShapeDtypeStruct(g.shape, F32)] * 3,
                          compiler_params=_cparams(7 * br * cols * 4, 1))(g, w, m, v)


ROW_FINAL_G, ROW_PE_G, ROW_LOSS = 0, 1, 2
ROW_CONV_B, ROW_CN_G, ROW_CN_B, ROW_B_PW2 = 8, 9, 10, 11
ROW_LN_G = 16
ROW_ONORM_G, ROW_LB = 24, 25
ROW_CONV_W = 32
SMALL = ["ln_g", "conv_b", "cnorm_g", "cnorm_b", "b_pw2", "onorm_g", "pe_norm_g", "final_g"]
SMALL_ROW = dict(ln_g=ROW_LN_G, conv_b=ROW_CONV_B, cnorm_g=ROW_CN_G, cnorm_b=ROW_CN_B, b_pw2=ROW_B_PW2,
                 onorm_g=ROW_ONORM_G, pe_norm_g=ROW_PE_G, final_g=ROW_FINAL_G)


def _adamw_small(vsum, gcw, lb_logits, params):
    names = SMALL + ["lb_logits", "conv_w"]
    flat = [t for nm in names for t in params[nm]]

    def body(*refs):
        vs_ref, gcw_ref, lbl_ref = refs[:3]
        ins = refs[3:3 + 3 * len(names)]
        outs = refs[3 + 3 * len(names):]
        for q, nm in enumerate(names):
            w_ref, m_ref, v_ref = ins[3 * q:3 * q + 3]
            g_ref, d_ref, nm_ref, nv_ref = outs[4 * q:4 * q + 4]
            if nm == "conv_w":
                g = gcw_ref[...]
            elif nm == "lb_logits":
                lb = _softmax_row0(lbl_ref[...])
                g0 = vs_ref[ROW_LB:ROW_LB + 1, :] * lb * (1.0 - lb)
                g = jnp.concatenate([g0, -g0], axis=0)
            else:
                g = vs_ref[SMALL_ROW[nm]:SMALL_ROW[nm] + 1, :]
            g_ref[...] = g
            d_ref[...], nm_ref[...], nv_ref[...] = _adamw_math(w_ref[...], g, m_ref[...], v_ref[...])

    out_shape = [jax.ShapeDtypeStruct(params[nm][0].shape, F32) for nm in names for _ in range(4)]
    outs = pl.pallas_call(body, name="adamw_small", out_shape=out_shape)(vsum, gcw, lb_logits, *flat)
    return {nm: tuple(outs[4 * q:4 * q + 4]) for q, nm in enumerate(names)}


TOKEN_TILE = dict(rmsnorm=512, inproj_fwd=2048, conv=256, hgrn=512, tail=512, inproj_bwd_u=2048, inproj_bwd_x=512,
                  weight_grad=2048)


def _tile(T, family):
    return min(T, TOKEN_TILE[family])


def kernel(x, p, ln_g, w_in, conv_w, conv_b, cnorm_g, cnorm_b, w_pw2, b_pw2, lb_logits, onorm_g, w_out, pe_norm_g, w_pg, w_pp, final_g, loss_target, m_ln_g, m_w_in, m_conv_w, m_conv_b, m_cnorm_g, m_cnorm_b, m_w_pw2, m_b_pw2, m_lb_logits, m_onorm_g, m_w_out, m_pe_norm_g, m_w_pg, m_w_pp, m_final_g, v_ln_g, v_w_in, v_conv_w, v_conv_b, v_cnorm_g, v_cnorm_b, v_w_pw2, v_b_pw2, v_lb_logits, v_onorm_g, v_w_out, v_pe_norm_g, v_w_pg, v_w_pp, v_final_g):
    given = dict(locals())
    x2, p2, tgt = x[0], p[0, 0], loss_target[0]
    T = x2.shape[0]
    fin_g = final_g.reshape(1, D)

    conv_w_pad = jnp.pad(conv_w[0], ((0, HALO - CONV_K), (0, 0)))
    u, (w_in_f,) = _rmsnorm_gather(x2, ln_g, [w_in[0].astype(_MXU)], [1], _tile(T, "rmsnorm"))

    z, (w_pw2_f, w_out_f, w_pg_f, w_pp_f, conv_w_f) = _inproj_fwd(
        u, w_in_f,
        [w_pw2[0].astype(_MXU), w_out[0].astype(_MXU), w_pg[0].astype(_MXU), w_pp[0].astype(_MXU), conv_w_pad],
        [0, 0, 0, 1, 1], _tile(T, "inproj_fwd"))
    yc, y1 = _conv_fwd(z, conv_w_f, conv_b, cnorm_g, cnorm_b, w_pw2_f, b_pw2, _tile(T, "conv"))
    o_raw, yh, s_chunks = _hgrn_fwd(z, lb_logits, onorm_g, _tile(T, "hgrn"), HB)
    dyc, dyh, dh, n2, ds, dpe, dhb, pb, vec_tail = _tail(
        x2, yc, yh, p2, tgt, w_out_f, w_pg_f, w_pp_f, pe_norm_g, fin_g, _tile(T, "tail"))
    tk = _tile(T, "weight_grad")
    g_w_out_c = _tn_matmul(yc, dhb, tk, "grad_w_out_conv")
    g_w_out_h = _tn_matmul(yh, dhb, tk, "grad_w_out_hgrn")
    g_w_pg = _tn_matmul(n2, ds, tk, "grad_w_pg")
    g_w_pp = _tn_matmul(pb, dpe, tk, "grad_w_pp")
    dzc, a_act, dy2, vec_conv, g_conv_w = _conv_bwd(z, y1, dyc, conv_w_f, cnorm_g, cnorm_b, w_pw2_f, b_pw2, _tile(T, "conv"))
    g_w_pw2 = _tn_matmul(a_act, dy2, tk, "grad_w_pw2")

    def pair_sums(names, slabs, tag):
        mine, got = _pair_exchange(slabs, "grad_pair_exchange_" + tag)
        return [_add2(a.reshape(-1, a.shape[-1]), b.reshape(-1, b.shape[-1]), _WIRE, "pair_sum_" + nm).reshape(a.shape)
                for a, b, nm in zip(mine, got, names)]

    rest = ["w_pw2", "w_out", "w_pg", "w_pp"]
    partial_rest = pair_sums(rest, [
        _Slab([g_w_pw2], lambda s: (0, s * (D // N_CHIPS), 0), (D // N_CHIPS, D)),
        _Slab([g_w_out_c, g_w_out_h], lambda s: (s // 2, (s % 2) * (D // 2), 0), (D // 2, D)),
        _Slab([g_w_pg], lambda s: (0, s * (D // N_CHIPS), 0), (D // N_CHIPS, D)),
        _Slab([g_w_pp], lambda s: (0, 0, s * (D // N_CHIPS)), (PLE, D // N_CHIPS)),
    ], "rest")
    dzh, vec_hgrn, slots_rest = _hgrn_bwd(z, lb_logits, onorm_g, o_raw, dyh, s_chunks, partial_rest, _tile(T, "hgrn"), HB)
    g_w_in = _inproj_bwd_w(u, dzc, dzh, tk)
    partial_in = pair_sums(["w_in"], [
        _Slab([g_w_in], lambda s: (0, 0, s * (NPART * D // N_CHIPS)), (D, NPART * D // N_CHIPS))], "w_in")
    du, slots_in = _inproj_bwd_u(dzc, dzh, w_in_f, partial_in, _tile(T, "inproj_bwd_u"))
    grad_x, vec_in = _inproj_bwd_x(x2, ln_g, du, dh, _tile(T, "inproj_bwd_x"))
    big = ["w_in"] + rest
    halves = [_sum_slots(s, "chip_sum_" + nm) for s, nm in zip(list(slots_in) + list(slots_rest), big)]
    vec = jnp.concatenate([vec_tail, vec_conv, vec_in, vec_hgrn, g_conv_w], axis=0)
    grads_big, vec_slots = _pair_share(halves, vec)
    vsum = _sum_slots(vec_slots, "vec_sum")

    out = {}
    for nm, g in zip(big, grads_big):
        w2, m2, v2 = given[nm][0], given["m_" + nm][0], given["v_" + nm][0]
        d, nm_, nv_ = _adamw(g, w2, m2, v2, "adamw_" + nm)
        out[nm] = tuple(t[None] for t in (g, d, nm_, nv_))
    chip = 2 * lax.axis_index("x") + lax.axis_index("y")
    gcw = lax.dynamic_slice(vsum, (ROW_CONV_W, chip * (D // N_CHIPS)), (CONV_K, D // N_CHIPS))
    params = {nm: (given[nm].reshape(-1, D), given["m_" + nm].reshape(-1, D), given["v_" + nm].reshape(-1, D))
              for nm in SMALL + ["lb_logits"]}
    params["conv_w"] = (conv_w[0], m_conv_w[0], v_conv_w[0])
    small = _adamw_small(vsum, gcw, lb_logits, params)
    for nm, ts in small.items():
        out[nm] = tuple(t.reshape(given[nm].shape) for t in ts)

    loss = vsum[ROW_LOSS, 0]
    order = ["ln_g", "w_in", "conv_w", "conv_b", "cnorm_g", "cnorm_b", "w_pw2", "b_pw2", "lb_logits", "onorm_g",
             "w_out", "pe_norm_g", "w_pg", "w_pp", "final_g"]
    return (loss, grad_x[None], *[out[nm][0] for nm in order], *[out[nm][1] for nm in order],
            *[out[nm][2] for nm in order], *[out[nm][3] for nm in order])
```

```python
import functools

import jax
import jax.numpy as jnp
from jax import lax
from jax.experimental import pallas as pl
from jax.experimental.pallas import tpu as pltpu

F32 = jnp.float32
BF16 = jnp.bfloat16
_MXU = jnp.bfloat16
_WIRE = jnp.bfloat16

D = 1024
NPART = 7
PLE = 256
HEADS = 8
HD = 128
CHUNK = 64
CONV_K = 31
HALO = 32
EPS = 1e-6
N_CHIPS = 4
N_DEV = 8
HB = 8
VEC_ROWS = 64

ADAM_LR = 0.001
ADAM_B1 = 0.9
ADAM_B2 = 0.999
ADAM_EPS = 1e-08
ADAM_WD = 0.01
ADAM_STEP = 10

V7X_VMEM_LIMIT = 60000 * 1024
MESH_ID = pl.DeviceIdType.MESH
ANY = pl.BlockSpec(memory_space=pltpu.HBM)


def _in_hbm(arrays):
    return [pltpu.with_memory_space_constraint(a, pltpu.HBM) for a in arrays]


def _cparams(block_bytes, n_grid_dims):
    limit = min(V7X_VMEM_LIMIT, 2 * block_bytes + (24 << 20))
    return pltpu.CompilerParams(vmem_limit_bytes=int(limit), dimension_semantics=("arbitrary",) * n_grid_dims)


def _nbytes(shape, dtype):
    n = 1
    for s in shape:
        n *= s
    return n * jnp.dtype(dtype).itemsize


def _dot(a, b):
    return jnp.dot(a.astype(_MXU), b.astype(_MXU), preferred_element_type=F32)


def _dot_nt(a, b):
    return lax.dot_general(a.astype(_MXU), b.astype(_MXU), (((1,), (1,)), ((), ())), preferred_element_type=F32)


def _dot_tn(a, b):
    return lax.dot_general(a.astype(_MXU), b.astype(_MXU), (((0,), (0,)), ((), ())), preferred_element_type=F32)


def _tri_dot(tri_bf, x):
    x1 = x.astype(BF16)
    r1 = x - x1.astype(F32)
    x2 = r1.astype(BF16)
    x3 = (r1 - x2.astype(F32)).astype(BF16)
    d = lambda t: jnp.dot(tri_bf, t, preferred_element_type=F32)
    return d(x1) + d(x2) + d(x3)


def _split2(x):
    hi = x.astype(BF16)
    return hi, (x - hi.astype(F32)).astype(BF16)


def _dot3(dims, a, b):
    d = lambda p, q: lax.dot_general(p, q, (dims, ((), ())), preferred_element_type=F32)
    return d(a[0], b[0]) + d(a[0], b[1]) + d(a[1], b[0])


def _sigmoid(x):
    return jax.nn.sigmoid(x)


def _mean_lanes(x):
    return jnp.mean(x, axis=-1, keepdims=True)


def _sum_rows(x):
    return jnp.sum(x, axis=0, keepdims=True)


def _group_ln(y):
    yn, rs = [], []
    for g in range(D // HD):
        blk = y[:, g * HD:(g + 1) * HD]
        xc = blk - _mean_lanes(blk)
        r = lax.rsqrt(_mean_lanes(xc * xc) + EPS)
        yn.append(xc * r)
        rs.append(jnp.broadcast_to(r, blk.shape))
    return jnp.concatenate(yn, axis=1), jnp.concatenate(rs, axis=1)


def _group_ln_bwd(dyn, yn, rstd):
    out = []
    for g in range(D // HD):
        sl = slice(g * HD, (g + 1) * HD)
        d, n = dyn[:, sl], yn[:, sl]
        out.append(rstd[:, sl] * (d - _mean_lanes(d) - n * _mean_lanes(d * n)))
    return jnp.concatenate(out, axis=1)


def _head_means(x, hb, fn=lambda m: m):
    return jnp.concatenate([jnp.broadcast_to(fn(_mean_lanes(x[:, hh * HD:(hh + 1) * HD])), (x.shape[0], HD))
                            for hh in range(hb)], axis=1)


def _head_rsqrt_mean(x, hb):
    return _head_means(x, hb, lambda m: lax.rsqrt(m + EPS))


def _softmax_row0(lbl):
    m = jnp.max(lbl, axis=0, keepdims=True)
    e = jnp.exp(lbl - m)
    return e[0:1, :] / jnp.sum(e, axis=0, keepdims=True)


def _hosted_gather(phases, step, at, shards, axes, ins, outs, bufs, sems):
    gather = _Gather([s.shape for s in shards], axes, ins, outs, bufs, sems)
    for phase in phases:
        pl.when(step == at[phase])(getattr(gather, phase))


def _rmsnorm_gather(x, ln_g, shards, axes, tT):
    T = x.shape[0]
    n = len(shards)
    at = _hosted_steps(T // tT)

    def body(x_ref, g_ref, *rest):
        ins, u_ref, outs, bufs, sems = rest[:n], rest[n], rest[n + 1:2 * n + 1], rest[2 * n + 1:3 * n + 1], rest[3 * n + 1:]
        host = functools.partial(_hosted_gather, step=pl.program_id(0), at=at, shards=shards, axes=axes,
                                 ins=ins, outs=outs, bufs=bufs, sems=sems)
        host(("start", "turn", "forward"))
        xv = x_ref[...]
        r = lax.rsqrt(_mean_lanes(xv * xv) + EPS)
        u_ref[...] = (xv * r * g_ref[...]).astype(_MXU)
        host(("finish",))

    blk = _nbytes((tT, D), F32) * 2 + _nbytes((tT, D), _MXU) + sum(_nbytes(s.shape, s.dtype) for s in shards)
    outs = pl.pallas_call(
        body, name="rmsnorm_gather", grid=(T // tT,),
        in_specs=[pl.BlockSpec((tT, D), lambda i: (i, 0)), pl.BlockSpec((1, D), lambda i: (0, 0))] + [ANY] * n,
        out_specs=[pl.BlockSpec((tT, D), lambda i: (i, 0))] + [ANY] * n,
        out_shape=[jax.ShapeDtypeStruct((T, D), _MXU)]
        + [pltpu.HBM(fs, s.dtype) for fs, s in zip(_full_shapes(shards, axes), shards)],
        scratch_shapes=_Gather.scratch(shards),
        compiler_params=_cparams(blk, 1),
    )(x, ln_g, *_in_hbm(shards))
    return outs[0], outs[1:]


def _inproj_fwd(u, w_in, shards, axes, tT):
    T = u.shape[0]
    n = len(shards)
    at = _hosted_steps((T // tT) * NPART)

    def body(u_ref, w_ref, *rest):
        ins, z_ref, outs, bufs, sems = rest[:n], rest[n], rest[n + 1:2 * n + 1], rest[2 * n + 1:3 * n + 1], rest[3 * n + 1:]
        host = functools.partial(_hosted_gather, step=pl.program_id(0) * NPART + pl.program_id(1), at=at, shards=shards,
                                 axes=axes, ins=ins, outs=outs, bufs=bufs, sems=sems)
        host(("start", "turn", "forward"))
        z_ref[...] = jnp.dot(u_ref[...], w_ref[...], preferred_element_type=F32)
        host(("finish",))

    blk = (_nbytes((tT, D), F32) + _nbytes((D, D), _MXU) + _nbytes((tT, D), _MXU)
           + sum(_nbytes(s.shape, s.dtype) for s in shards))
    outs = pl.pallas_call(
        body, name="inproj_fwd", grid=(T // tT, NPART),
        in_specs=[pl.BlockSpec((tT, D), lambda i, j: (i, 0)), pl.BlockSpec((D, D), lambda i, j: (0, j))] + [ANY] * n,
        out_specs=[pl.BlockSpec((tT, D), lambda i, j: (i, j))] + [ANY] * n,
        out_shape=[jax.ShapeDtypeStruct((T, NPART * D), F32)]
        + [pltpu.HBM(fs, s.dtype) for fs, s in zip(_full_shapes(shards, axes), shards)],
        scratch_shapes=_Gather.scratch(shards),
        compiler_params=_cparams(blk, 2),
    )(u, w_in, *_in_hbm(shards))
    return outs[0], outs[1:]


def _shifted_windows(ext, first, visit):
    n = ext.shape[0]
    for m in range(first, first + CONV_K):
        visit(m, (ext if m == 0 else pltpu.roll(ext, n - m, axis=0))[0:n - HALO, :])


def _conv_fwd(z, conv_w, conv_b, cn_g, cn_b, w_pw2, b_pw2, tT):
    T = z.shape[0]

    def body(cv_ref, cg_ref, ct_ref, cw_ref, cb_ref, ng_ref, nb_ref, wp_ref, bp_ref, yc_ref, y1_ref, ext):
        @pl.when(pl.program_id(0) == 0)
        def _():
            ext[...] = jnp.zeros_like(ext)
        ext[0:HALO, :] = ext[tT:tT + HALO, :]
        ext[HALO:, :] = cv_ref[...] * _sigmoid(cg_ref[...])
        cw = cw_ref[...]
        acc = [cb_ref[...]]

        def tap(m, win):
            acc[0] = acc[0] + win * cw[m - 2:m - 1, :]
        _shifted_windows(ext[...], 2, tap)
        y1 = acc[0]
        y1_ref[...] = y1
        yn, _ = _group_ln(y1)
        apre = yn * ng_ref[...] + nb_ref[...]
        a = apre * _sigmoid(apre)
        y2 = _dot(a, wp_ref[...]) + bp_ref[...]
        ct = ct_ref[...]
        yc_ref[...] = (y2 * (ct * _sigmoid(ct))).astype(_MXU)

    part = lambda p: pl.BlockSpec((tT, D), lambda i: (i, p))
    row = pl.BlockSpec((1, D), lambda i: (0, 0))
    tok = pl.BlockSpec((tT, D), lambda i: (i, 0))
    blk = 4 * _nbytes((tT, D), F32) + _nbytes((D, D), _MXU) + _nbytes((tT, D), _MXU) + 8 * _nbytes((tT + HALO, D), F32)
    return pl.pallas_call(
        body, name="conv_fwd", grid=(T // tT,),
        in_specs=[part(0), part(1), part(2), pl.BlockSpec((HALO, D), lambda i: (0, 0)), row, row, row,
                  pl.BlockSpec((D, D), lambda i: (0, 0)), row],
        out_specs=[tok, tok],
        out_shape=[jax.ShapeDtypeStruct((T, D), _MXU), jax.ShapeDtypeStruct((T, D), F32)],
        scratch_shapes=[pltpu.VMEM((tT + HALO, D), F32)],
        compiler_params=_cparams(blk, 1),
    )(z, z, z, conv_w, conv_b, cn_g, cn_b, w_pw2, b_pw2)


def _hgrn_gates(lb, hq, hf):
    sq = _sigmoid(hq)
    sg = _sigmoid(hf)
    f = lb + (1.0 - lb) * sg
    return sq, sg, f, hq * sq, (1.0 - lb) * (1.0 - sg), jnp.log(f)


def _chunk_decays(lf, q, k):
    r = lax.broadcasted_iota(jnp.int32, (CHUNK, CHUNK), 0)
    c = lax.broadcasted_iota(jnp.int32, (CHUNK, CHUNK), 1)
    b = _tri_dot((r >= c).astype(BF16), lf)
    bm = b[CHUNK // 2 - 1:CHUNK // 2, :]
    bl = b[CHUNK - 1:CHUNK, :]
    eb = jnp.exp(b)
    eqm = jnp.exp(b - bm)
    ekm = jnp.exp(bm - b)
    ekd = jnp.exp(bl - b)
    return dict(causal=r >= c, eb=eb, eqm=eqm, ekm=ekm, ekd=ekd, ebl=jnp.exp(bl),
                qd=q * eb, qm=q * eqm, km=k * ekm, kd=k * ekd)


def _hgrn_fwd(z, lb_logits, onorm_g, tT, hb):
    T = z.shape[0]
    nc = tT // CHUNK
    w = hb * HD

    def body(lbl_ref, og_ref, hq_ref, hf_ref, hi_ref, hg_ref, o_ref, yh_ref, sc_ref, st):
        @pl.when(pl.program_id(1) == 0)
        def _():
            st[...] = jnp.zeros_like(st)
        lb_all = _softmax_row0(lbl_ref[...])
        og_all = og_ref[...]

        def chunk(c, carry):
            sl = pl.ds(pl.multiple_of(c * CHUNK, CHUNK), CHUNK)
            lanes = [slice(hh * HD, (hh + 1) * HD) for hh in range(hb)]
            heads = lambda fn: [fn(hh, ln) for hh, ln in enumerate(lanes)]
            hg, v = hg_ref[sl, :], hi_ref[sl, :]
            _, _, _, q, k, lf = _hgrn_gates(lb_all, hq_ref[sl, :], hf_ref[sl, :])
            dc = _chunk_decays(lf, q, k)
            s_t = heads(lambda hh, ln: st[hh])
            a = heads(lambda hh, ln: jnp.where(dc["causal"], _dot_nt(dc["qm"][:, ln], dc["km"][:, ln]), 0.0))
            o_inter = heads(lambda hh, ln: _dot_nt(dc["qd"][:, ln], s_t[hh]))
            kv = heads(lambda hh, ln: _dot_tn(v[:, ln], dc["kd"][:, ln]))
            o_intra = heads(lambda hh, ln: _dot(a[hh], v[:, ln]))
            for hh, ln in enumerate(lanes):
                sc_ref[hh, c] = s_t[hh]
                st[hh] = s_t[hh] * dc["ebl"][:, ln] + kv[hh]
            o = jnp.concatenate([o_inter[hh] + o_intra[hh] for hh in range(hb)], axis=1)
            o_ref[sl, :] = o
            n = o * _head_rsqrt_mean(o * o, hb)
            yh_ref[sl, :] = ((n * og_all) * (hg * _sigmoid(hg))).astype(_MXU)
            return carry

        lax.fori_loop(0, nc, chunk, 0, unroll=8)

    zpart = lambda p: pl.BlockSpec((tT, w), lambda h, i: (i, p * (HEADS // hb) + h))
    blk = 6 * _nbytes((tT, w), F32) + _nbytes((hb, nc, HD, HD), F32)
    return pl.pallas_call(
        body, name="hgrn_fwd", grid=(HEADS // hb, T // tT),
        in_specs=[pl.BlockSpec((2, w), lambda h, i: (0, h)), pl.BlockSpec((1, w), lambda h, i: (0, h)),
                  zpart(3), zpart(4), zpart(5), zpart(6)],
        out_specs=[pl.BlockSpec((tT, w), lambda h, i: (i, h)), pl.BlockSpec((tT, w), lambda h, i: (i, h)),
                   pl.BlockSpec((hb, nc, HD, HD), lambda h, i: (h, i, 0, 0))],
        out_shape=[jax.ShapeDtypeStruct((T, D), F32), jax.ShapeDtypeStruct((T, D), _MXU),
                   jax.ShapeDtypeStruct((HEADS, T // CHUNK, HD, HD), F32)],
        scratch_shapes=[pltpu.VMEM((hb, HD, HD), F32)],
        compiler_params=_cparams(blk, 2),
    )(lb_logits, onorm_g, z, z, z, z)


def _hgrn_bwd(z, lb_logits, onorm_g, o_raw, dyh, s_chunks, partials, tT, hb):
    T = z.shape[0]
    nc = tT // CHUNK
    nI = T // tT
    w = hb * HD
    n = len(partials)
    at = _hosted_steps((HEADS // hb) * nI)

    def body(lbl_ref, og_ref, hq_ref, hf_ref, hi_ref, hg_ref, o_ref, dy_ref, sc_ref, *rest):
        (dz_ref, vec_ref), dst = rest[n:n + 2], rest[2 * n + 2]
        exchange = _ChipExchange(n, rest[:n], rest[n + 2:2 * n + 2], rest[2 * n + 3:3 * n + 3], rest[3 * n + 3:])
        step = pl.program_id(0) * nI + pl.program_id(1)
        pl.when(step == at["start"])(exchange.start)
        pl.when(step == at["turn"])(exchange.turn)

        @pl.when(pl.program_id(1) == 0)
        def _():
            dst[...] = jnp.zeros_like(dst)
            vec_ref[...] = jnp.zeros_like(vec_ref)
        lb_all = _softmax_row0(lbl_ref[...])
        og_all = og_ref[...]
        last_row = lax.broadcasted_iota(jnp.int32, (CHUNK, w), 0) == CHUNK - 1
        r64 = lax.broadcasted_iota(jnp.int32, (CHUNK, CHUNK), 0)
        c64 = lax.broadcasted_iota(jnp.int32, (CHUNK, CHUNK), 1)
        upper = (c64 >= r64).astype(BF16)
        lanes = [slice(hh * HD, (hh + 1) * HD) for hh in range(hb)]
        heads = lambda fn: [fn(hh, ln) for hh, ln in enumerate(lanes)]
        wide = lambda parts: jnp.concatenate(parts, axis=1)

        def chunk(cc, carry):
            c = nc - 1 - cc
            sl = pl.ds(pl.multiple_of(c * CHUNK, CHUNK), CHUNK)
            hq, hg, v = hq_ref[sl, :], hg_ref[sl, :], hi_ref[sl, :]
            sq, sg, f, q, k, lf = _hgrn_gates(lb_all, hq, hf_ref[sl, :])
            dc = _chunk_decays(lf, q, k)
            s_t = heads(lambda hh, ln: sc_ref[hh, c])
            ds_t = heads(lambda hh, ln: dst[hh])
            o, dy = o_ref[sl, :], dy_ref[sl, :]
            r = _head_rsqrt_mean(o * o, hb)
            n = o * r
            sgg = _sigmoid(hg)
            silu_g = hg * sgg
            dhg = dy * (n * og_all) * (sgg * (1.0 + hg * (1.0 - sgg)))
            dn = dy * og_all * silu_g
            g_og = _sum_rows(dy * n * silu_g)
            do = r * (dn - n * _head_means(dn * n, hb))
            a = heads(lambda hh, ln: jnp.where(dc["causal"], _dot_nt(dc["qm"][:, ln], dc["km"][:, ln]), 0.0))
            dam = heads(lambda hh, ln: jnp.where(dc["causal"], _dot_nt(do[:, ln], v[:, ln]), 0.0))
            dqd = wide(heads(lambda hh, ln: _dot(do[:, ln], s_t[hh])))
            dkd = wide(heads(lambda hh, ln: _dot(v[:, ln], ds_t[hh])))
            dv_inter = heads(lambda hh, ln: _dot_nt(dc["kd"][:, ln], ds_t[hh]))
            dqs = heads(lambda hh, ln: _dot_tn(do[:, ln], dc["qd"][:, ln]))
            dv = wide(heads(lambda hh, ln: _dot_tn(a[hh], do[:, ln]) + dv_inter[hh]))
            dam2 = [_split2(t) for t in dam]
            km2, qm2 = _split2(dc["km"]), _split2(dc["qm"])
            dqm = wide(heads(lambda hh, ln: _dot3(((1,), (0,)), dam2[hh], (km2[0][:, ln], km2[1][:, ln]))))
            dkm = wide(heads(lambda hh, ln: _dot3(((0,), (0,)), dam2[hh], (qm2[0][:, ln], qm2[1][:, ln]))))
            debl = wide(heads(lambda hh, ln: _sum_rows(ds_t[hh] * s_t[hh])))
            for hh, ln in enumerate(lanes):
                dst[hh] = ds_t[hh] * dc["ebl"][:, ln] + dqs[hh]
            dq = dqd * dc["eb"] + dqm * dc["eqm"]
            dk = dkm * dc["ekm"] + dkd * dc["ekd"]
            dbl = _sum_rows(dkd * dc["kd"]) + debl * dc["ebl"]
            db = dq * q - dk * k + jnp.where(last_row, dbl, 0.0)
            dlf = _tri_dot(upper, db)
            dfk = dlf / f - dk
            dz_ref[0, sl, :] = (dq * (sq * (1.0 + hq * (1.0 - sq)))).astype(_MXU)
            dz_ref[1, sl, :] = (dfk * ((1.0 - lb_all) * sg * (1.0 - sg))).astype(_MXU)
            dz_ref[2, sl, :] = dv.astype(_MXU)
            dz_ref[3, sl, :] = dhg.astype(_MXU)
            vec_ref[0:1, :] += g_og
            vec_ref[1:2, :] += _sum_rows(dfk * (1.0 - sg))
            return carry

        lax.fori_loop(0, nc, chunk, 0, unroll=8)
        pl.when(step == at["finish"])(exchange.finish)

    zpart = lambda p: pl.BlockSpec((tT, w), lambda h, i: (nI - 1 - i, p * (HEADS // hb) + h))
    act = pl.BlockSpec((tT, w), lambda h, i: (nI - 1 - i, h))
    blk = (6 * _nbytes((tT, w), F32) + _nbytes((hb, nc, HD, HD), F32) + 4 * _nbytes((tT, w), _MXU)
           + _ChipExchange.scratch_bytes(partials))
    outs = pl.pallas_call(
        body, name="hgrn_bwd", grid=(HEADS // hb, nI),
        in_specs=[pl.BlockSpec((2, w), lambda h, i: (0, h)), pl.BlockSpec((1, w), lambda h, i: (0, h)),
                  zpart(3), zpart(4), zpart(5), zpart(6), act, act,
                  pl.BlockSpec((hb, nc, HD, HD), lambda h, i: (h, nI - 1 - i, 0, 0))] + [ANY] * n,
        out_specs=[pl.BlockSpec((4, tT, w), lambda h, i: (0, nI - 1 - i, h)),
                   pl.BlockSpec((8, w), lambda h, i: (0, h))] + [ANY] * n,
        out_shape=[jax.ShapeDtypeStruct((4, T, D), _MXU), jax.ShapeDtypeStruct((8, D), F32)]
        + [pltpu.HBM(p.shape, p.dtype) for p in partials],
        scratch_shapes=[pltpu.VMEM((hb, HD, HD), F32)] + _ChipExchange.scratch(partials),
        compiler_params=_cparams(blk, 2),
    )(lb_logits, onorm_g, z, z, z, z, o_raw, dyh, s_chunks, *_in_hbm(partials))
    return outs[0], outs[1], outs[2:]


def _tail(x, yc, yh, p, target, w_out, w_pg, w_pp, pe_g, fin_g, tT):
    T = x.shape[0]

    def body(x_ref, yc_ref, yh_ref, p_ref, t_ref, wo_ref, wg_ref, wp_ref, pg_ref, fg_ref,
             dyc_ref, dyh_ref, dh_ref, n2_ref, ds_ref, dpe_ref, dhb_ref, pb_ref, vec_ref):
        @pl.when(pl.program_id(0) == 0)
        def _():
            vec_ref[...] = jnp.zeros_like(vec_ref)
        wo_c, wo_h = wo_ref[0:D, :], wo_ref[D:2 * D, :]
        h = x_ref[...] + _dot(yc_ref[...], wo_c) + _dot(yh_ref[...], wo_h)
        pb = p_ref[...].astype(_MXU)
        pe = _dot(pb, wp_ref[...])
        r2 = lax.rsqrt(_mean_lanes(h * h) + EPS)
        hn = h * r2
        n2 = (hn * pg_ref[...]).astype(_MXU)
        gate = _sigmoid(_dot(n2, wg_ref[...]))
        h2 = h + gate * pe
        r3 = lax.rsqrt(_mean_lanes(h2 * h2) + EPS)
        h2n = h2 * r3
        err = h2n * fg_ref[...] - t_ref[...]
        vec_ref[ROW_LOSS:ROW_LOSS + 1, :] += 0.5 * jnp.sum(_mean_lanes(err * err))
        dout = err * (1.0 / D)
        vec_ref[0:1, :] += _sum_rows(dout * h2n)
        dn3 = dout * fg_ref[...]
        dh2 = r3 * (dn3 - h2n * _mean_lanes(dn3 * h2n))
        ds = (dh2 * pe * gate * (1.0 - gate)).astype(_MXU)
        dn2 = _dot_nt(ds, wg_ref[...])
        vec_ref[1:2, :] += _sum_rows(dn2 * hn)
        dnn = dn2 * pg_ref[...]
        dh = dh2 + r2 * (dnn - hn * _mean_lanes(dnn * hn))
        dhb = dh.astype(_MXU)
        dyc_ref[...] = _dot_nt(dhb, wo_c)
        dyh_ref[...] = _dot_nt(dhb, wo_h)
        dh_ref[...] = dh
        n2_ref[...] = n2
        ds_ref[...] = ds
        dpe_ref[...] = (dh2 * gate).astype(_MXU)
        dhb_ref[...] = dhb
        pb_ref[...] = pb

    tok = lambda w: pl.BlockSpec((tT, w), lambda i: (i, 0))
    full = lambda r, c: pl.BlockSpec((r, c), lambda i: (0, 0))
    tokshape = lambda w, dt: jax.ShapeDtypeStruct((T, w), dt)
    blk = (5 * _nbytes((tT, D), F32) + 7 * _nbytes((tT, D), _MXU) + _nbytes((4 * D + PLE, D), _MXU)
           + 12 * _nbytes((tT, D), F32))
    return pl.pallas_call(
        body, name="tail_fwd_bwd", grid=(T // tT,),
        in_specs=[tok(D), tok(D), tok(D), tok(PLE), tok(D), full(2 * D, D), full(D, D), full(PLE, D), full(1, D), full(1, D)],
        out_specs=[tok(D), tok(D), tok(D), tok(D), tok(D), tok(D), tok(D), tok(PLE), full(8, D)],
        out_shape=[tokshape(D, F32), tokshape(D, F32), tokshape(D, F32), tokshape(D, _MXU), tokshape(D, _MXU),
                   tokshape(D, _MXU), tokshape(D, _MXU), tokshape(PLE, _MXU),
                   jax.ShapeDtypeStruct((8, D), F32)],
        compiler_params=_cparams(blk, 1),
    )(x, yc, yh, p, target, w_out, w_pg, w_pp, pe_g, fin_g)


def _conv_bwd(z, y1, dyc, conv_w, cn_g, cn_b, w_pw2, b_pw2, tT):
    T = z.shape[0]
    nI = T // tT
    hb = tT // HALO

    def body(cv_ref, cg_ref, ct_ref, hv_ref, hg_ref, y1_ref, dyc_ref, cw_ref, ng_ref, nb_ref, wp_ref, bp_ref,
             dz_ref, a_ref, dy2_ref, vec_ref, gcw_ref, ext, ext2, gpart):
        i = pl.program_id(0)

        @pl.when(i == 0)
        def _():
            ext2[...] = jnp.zeros_like(ext2)
            gpart[...] = jnp.zeros_like(gpart)
            vec_ref[...] = jnp.zeros_like(vec_ref)
        cv, cg, ct = cv_ref[...], cg_ref[...], ct_ref[...]
        sg = _sigmoid(cg)
        has_hist = (i < nI - 1).astype(F32)
        ext[0:HALO, :] = hv_ref[...] * _sigmoid(hg_ref[...]) * has_hist
        ext[HALO:, :] = cv * sg
        yn, rstd = _group_ln(y1_ref[...])
        apre = yn * ng_ref[...] + nb_ref[...]
        sa = _sigmoid(apre)
        a = (apre * sa).astype(_MXU)
        y2 = _dot(a, wp_ref[...]) + bp_ref[...]
        st = _sigmoid(ct)
        dyc_v = dyc_ref[...]
        dy2 = dyc_v * (ct * st)
        dy2b = dy2.astype(_MXU)
        da = _dot_nt(dy2b, wp_ref[...])
        dapre = da * (sa * (1.0 + apre * (1.0 - sa)))
        dy1 = _group_ln_bwd(dapre * ng_ref[...], yn, rstd)
        vec_ref[0:1, :] += _sum_rows(dy1)
        vec_ref[1:2, :] += _sum_rows(dapre * yn)
        vec_ref[2:3, :] += _sum_rows(dapre)
        vec_ref[3:4, :] += _sum_rows(dy2)
        dz_ref[2] = (dyc_v * y2 * (st * (1.0 + ct * (1.0 - st)))).astype(_MXU)
        a_ref[...] = a
        dy2_ref[...] = dy2b
        ext2[tT:tT + HALO, :] = ext2[0:HALO, :]
        ext2[0:tT, :] = dy1
        def grad_tap(m, win):
            p = dy1 * win
            part = p[0:8, :]
            for q in range(1, tT // 8):
                part = part + p[8 * q:8 * q + 8, :]
            gpart[m - 2] += part
        _shifted_windows(ext[...], 2, grad_tap)
        cw = cw_ref[...]
        acc = [None]

        def dv_tap(m, win):
            term = win * cw[CONV_K - 1 - m:CONV_K - m, :]
            acc[0] = term if acc[0] is None else acc[0] + term
        _shifted_windows(ext2[...], 0, dv_tap)
        dv = acc[0]
        dz_ref[0] = (dv * sg).astype(_MXU)
        dz_ref[1] = (dv * cv * sg * (1.0 - sg)).astype(_MXU)

        @pl.when(i == nI - 1)
        def _():
            gcw_ref[...] = jnp.sum(gpart[...], axis=1)

    part = lambda p: pl.BlockSpec((tT, D), lambda i: (nI - 1 - i, p))
    hist = lambda p: pl.BlockSpec((HALO, D), lambda i: (jnp.maximum((nI - 1 - i) * hb - 1, 0), p))
    tok = pl.BlockSpec((tT, D), lambda i: (nI - 1 - i, 0))
    row = pl.BlockSpec((1, D), lambda i: (0, 0))
    blk = (5 * _nbytes((tT, D), F32) + _nbytes((D, D), _MXU) + 5 * _nbytes((tT, D), _MXU)
           + 10 * _nbytes((tT + HALO, D), F32))
    return pl.pallas_call(
        body, name="conv_bwd", grid=(nI,),
        in_specs=[part(0), part(1), part(2), hist(0), hist(1), tok, tok, pl.BlockSpec((HALO, D), lambda i: (0, 0)),
                  row, row, pl.BlockSpec((D, D), lambda i: (0, 0)), row],
        out_specs=[pl.BlockSpec((3, tT, D), lambda i: (0, nI - 1 - i, 0)), tok, tok,
                   pl.BlockSpec((8, D), lambda i: (0, 0)), pl.BlockSpec((HALO, D), lambda i: (0, 0))],
        out_shape=[jax.ShapeDtypeStruct((3, T, D), _MXU), jax.ShapeDtypeStruct((T, D), _MXU),
                   jax.ShapeDtypeStruct((T, D), _MXU), jax.ShapeDtypeStruct((8, D), F32),
                   jax.ShapeDtypeStruct((HALO, D), F32)],
        scratch_shapes=[pltpu.VMEM((tT + HALO, D), F32), pltpu.VMEM((tT + HALO, D), F32), pltpu.VMEM((HALO, 8, D), F32)],
        compiler_params=_cparams(blk, 1),
    )(z, z, z, z, z, y1, dyc, conv_w, cn_g, cn_b, w_pw2, b_pw2)


def _inproj_bwd_u(dzc, dzh, w_in, partials, tT):
    T = dzc.shape[1]
    n = len(partials)
    at = _hosted_steps((T // tT) * NPART)

    def body(dzc_ref, dzh_ref, w_ref, *rest):
        du_ref = rest[n]
        exchange = _ChipExchange(n, rest[:n], rest[n + 1:2 * n + 1], rest[2 * n + 1:3 * n + 1], rest[3 * n + 1:])
        j = pl.program_id(1)
        step = pl.program_id(0) * NPART + j
        pl.when(step == at["start"])(exchange.start)
        pl.when(step == at["turn"])(exchange.turn)

        @pl.when(j == 0)
        def _():
            du_ref[...] = jnp.zeros_like(du_ref)

        @pl.when(j < 3)
        def _():
            du_ref[...] += _dot_nt(dzc_ref[0], w_ref[...])

        @pl.when(j >= 3)
        def _():
            du_ref[...] += _dot_nt(dzh_ref[0], w_ref[...])
        pl.when(step == at["finish"])(exchange.finish)

    blk = (2 * _nbytes((tT, D), _MXU) + _nbytes((D, D), _MXU) + 2 * _nbytes((tT, D), F32)
           + _ChipExchange.scratch_bytes(partials))
    outs = pl.pallas_call(
        body, name="inproj_bwd_u", grid=(T // tT, NPART),
        in_specs=[pl.BlockSpec((1, tT, D), lambda i, j: (jnp.minimum(j, 2), i, 0)),
                  pl.BlockSpec((1, tT, D), lambda i, j: (jnp.maximum(j - 3, 0), i, 0)),
                  pl.BlockSpec((D, D), lambda i, j: (0, j))] + [ANY] * n,
        out_specs=[pl.BlockSpec((tT, D), lambda i, j: (i, 0))] + [ANY] * n,
        out_shape=[jax.ShapeDtypeStruct((T, D), F32)] + [pltpu.HBM(p.shape, p.dtype) for p in partials],
        scratch_shapes=_ChipExchange.scratch(partials),
        compiler_params=_cparams(blk, 2),
    )(dzc, dzh, w_in, *_in_hbm(partials))
    return outs[0], outs[1:]


def _inproj_bwd_x(x, ln_g, du, dh, tT):
    T = x.shape[0]

    def body(x_ref, g_ref, du_ref, dh_ref, gx_ref, vec_ref):
        @pl.when(pl.program_id(0) == 0)
        def _():
            vec_ref[...] = jnp.zeros_like(vec_ref)
        xv = x_ref[...]
        r = lax.rsqrt(_mean_lanes(xv * xv) + EPS)
        xn = xv * r
        duv = du_ref[...]
        vec_ref[0:1, :] += _sum_rows(duv * xn)
        dun = duv * g_ref[...]
        gx_ref[...] = dh_ref[...] + r * (dun - xn * _mean_lanes(dun * xn))

    tok = pl.BlockSpec((tT, D), lambda i: (i, 0))
    return pl.pallas_call(
        body, name="inproj_bwd_x", grid=(T // tT,),
        in_specs=[tok, pl.BlockSpec((1, D), lambda i: (0, 0)), tok, tok],
        out_specs=[tok, pl.BlockSpec((8, D), lambda i: (0, 0))],
        out_shape=[jax.ShapeDtypeStruct((T, D), F32), jax.ShapeDtypeStruct((8, D), F32)],
        compiler_params=_cparams(6 * _nbytes((tT, D), F32), 1),
    )(x, ln_g, du, dh)


def _inproj_bwd_w(u, dzc, dzh, tk):
    T = u.shape[0]
    nK = T // tk

    def body(u_ref, dzc_ref, dzh_ref, gw_ref):
        j, k = pl.program_id(0), pl.program_id(1)

        @pl.when(k == 0)
        def _():
            gw_ref[...] = jnp.zeros_like(gw_ref)

        @pl.when(j < 3)
        def _():
            gw_ref[...] += _dot_tn(u_ref[...], dzc_ref[0])

        @pl.when(j >= 3)
        def _():
            gw_ref[...] += _dot_tn(u_ref[...], dzh_ref[0])

    blk = 3 * _nbytes((tk, D), _MXU) + 2 * _nbytes((D, D), F32)
    return pl.pallas_call(
        body, name="inproj_bwd_w", grid=(NPART, nK),
        in_specs=[pl.BlockSpec((tk, D), lambda j, k: (k, 0)),
                  pl.BlockSpec((1, tk, D), lambda j, k: (jnp.minimum(j, 2), jnp.where(j < 3, k, nK - 1), 0)),
                  pl.BlockSpec((1, tk, D), lambda j, k: (jnp.maximum(j - 3, 0), jnp.where(j < 3, 0, k), 0))],
        out_specs=pl.BlockSpec((D, D), lambda j, k: (0, j)),
        out_shape=jax.ShapeDtypeStruct((D, NPART * D), F32),
        compiler_params=_cparams(blk, 2),
    )(u, dzc, dzh)


def _tn_matmul(a, b, tk, name):
    T, M = a.shape
    N = b.shape[1]

    def body(a_ref, b_ref, o_ref):
        @pl.when(pl.program_id(0) == 0)
        def _():
            o_ref[...] = jnp.zeros_like(o_ref)
        o_ref[...] += _dot_tn(a_ref[...], b_ref[...])

    blk = _nbytes((tk, M), _MXU) + _nbytes((tk, N), _MXU) + 2 * _nbytes((M, N), F32)
    return pl.pallas_call(
        body, name=name, grid=(T // tk,),
        in_specs=[pl.BlockSpec((tk, M), lambda k: (k, 0)), pl.BlockSpec((tk, N), lambda k: (k, 0))],
        out_specs=pl.BlockSpec((M, N), lambda k: (0, 0)),
        out_shape=pltpu.HBM((M, N), F32),
        compiler_params=_cparams(blk, 1),
    )(a, b)


def _place():
    return lax.axis_index("x"), lax.axis_index("y"), lax.axis_index("c")


def _flip(v, d):
    return 1 - v if d else v


CHIP_MOVES = [(1, 0), (0, 1), (1, 1)]
DEV_MOVES = [(dx, dy, dc) for dx in (0, 1) for dy in (0, 1) for dc in (0, 1)][1:]


def _shard_slice(ref, axis, size, s):
    start = pl.multiple_of(s * size, size)
    return ref.at[pl.ds(start, size), :] if axis == 0 else ref.at[:, pl.ds(start, size)]


class _Bounce:
    def __init__(self, src, buf, dst, sem_in, sem_out):
        self.load = pltpu.make_async_copy(src, buf, sem_in)
        self.store = pltpu.make_async_copy(buf, dst, sem_out)

    def start(self):
        self.load.start()

    def turn(self):
        self.load.wait()
        self.store.start()

    def wait(self):
        self.store.wait()


def _comm_params(scratch_bytes):
    return pltpu.CompilerParams(vmem_limit_bytes=int(min(V7X_VMEM_LIMIT, scratch_bytes + (8 << 20))))


class _Gather:
    def __init__(self, shapes, axes, ins, outs, bufs, sems):
        self.shapes, self.axes, self.ins, self.outs, self.bufs = shapes, axes, ins, outs, bufs
        self.ici_send, self.ici_recv, self.d2d_send, self.d2d_recv, self.in_sems, self.out_sems = sems
        self.x, self.y, self.c = _place()
        self.me = 2 * self.x + self.y
        self.pairs = [(k, j) for k in range(len(shapes)) for j in range(3)]

    @staticmethod
    def scratch(shards):
        n = len(shards)
        return ([pltpu.VMEM(s.shape, s.dtype) for s in shards]
                + [pltpu.SemaphoreType.DMA((3 * n,))] * 4 + [pltpu.SemaphoreType.DMA((n,))] * 2)

    def _own_half(self, k, hc):
        half = self.shapes[k][0] // 2
        return self.ins[k].at[pl.ds(pl.multiple_of(hc * half, 16), half), :]

    def _region(self, k, who, hc):
        rows, cols = self.shapes[k]
        half = rows // 2
        if self.axes[k] == 0:
            return self.outs[k].at[pl.ds(pl.multiple_of(who * rows + hc * half, 16), half), :]
        return self.outs[k].at[pl.ds(pl.multiple_of(hc * half, 16), half), pl.ds(pl.multiple_of(who * cols, HD), cols)]

    def _peer(self, j):
        return 2 * _flip(self.x, CHIP_MOVES[j][0]) + _flip(self.y, CHIP_MOVES[j][1])

    def _ici(self, k, j, who, hc):
        dx, dy = CHIP_MOVES[j]
        return pltpu.make_async_remote_copy(
            src_ref=self._own_half(k, hc), dst_ref=self._region(k, who, hc),
            send_sem=self.ici_send.at[3 * k + j], recv_sem=self.ici_recv.at[3 * k + j],
            device_id=(_flip(self.x, dx), _flip(self.y, dy), self.c), device_id_type=MESH_ID)

    def _d2d(self, k, j, who, hc):
        return pltpu.make_async_remote_copy(
            src_ref=self._region(k, who, hc), dst_ref=self._region(k, who, hc),
            send_sem=self.d2d_send.at[3 * k + j], recv_sem=self.d2d_recv.at[3 * k + j],
            device_id=(self.x, self.y, 1 - self.c), device_id_type=MESH_ID)

    def _local(self, k):
        size = self.shapes[k][self.axes[k]]
        return _Bounce(self.ins[k], self.bufs[k], _shard_slice(self.outs[k], self.axes[k], size, self.me),
                       self.in_sems.at[k], self.out_sems.at[k])

    def start(self):
        for k in range(len(self.shapes)):
            self._local(k).start()
        for k, j in self.pairs:
            self._ici(k, j, self.me, self.c).start()

    def turn(self):
        for k in range(len(self.shapes)):
            self._local(k).turn()

    def forward(self):
        for k, j in self.pairs:
            self._ici(k, j, self._peer(j), self.c).wait_recv()
            self._d2d(k, j, self._peer(j), self.c).start()

    def finish(self):
        for k, j in self.pairs:
            self._d2d(k, j, self._peer(j), 1 - self.c).wait_recv()
        for k, j in self.pairs:
            self._ici(k, j, self.me, self.c).wait_send()
            self._d2d(k, j, self._peer(j), self.c).wait_send()
        for k in range(len(self.shapes)):
            self._local(k).wait()


def _full_shapes(shards, axes):
    return [tuple(d * (N_CHIPS if a == ax else 1) for a, d in enumerate(s.shape)) for s, ax in zip(shards, axes)]


class _Slab:
    def __init__(self, arrays, pick, shard_shape):
        self.arrays = arrays
        self.pick = pick
        self.rows, self.cols = shard_shape
        self.half = self.rows // 2


PAIR_SUM_ROWS = 64


def _pair_exchange_sum(slabs, name):
    n = len(slabs)
    n_in = sum(len(sl.arrays) for sl in slabs)

    def body(*refs):
        ins, outs = refs[:n_in], refs[n_in:n_in + n]
        mine, got, total = (refs[n_in + (1 + t) * n:n_in + (2 + t) * n] for t in range(3))
        send_sems, recv_sems, in_sems, out_sems = refs[n_in + 4 * n:]
        x, y, c = _place()
        started = []
        base = 0
        for k, sl in enumerate(slabs):
            for s in range(N_CHIPS):
                ai, r0, c0 = sl.pick(s)
                src = ins[base + ai]

                def half(hc):
                    return src.at[pl.ds(pl.multiple_of(r0 + hc * sl.half, 8), sl.half), pl.ds(c0, sl.cols)]
                q = N_CHIPS * k + s
                load = pltpu.make_async_copy(half(c), mine[k].at[s], in_sems.at[q])
                load.start()
                cp = pltpu.make_async_remote_copy(
                    src_ref=half(1 - c), dst_ref=got[k].at[s], send_sem=send_sems.at[q], recv_sem=recv_sems.at[q],
                    device_id=(x, y, 1 - c), device_id_type=MESH_ID)
                cp.start()
                store = pltpu.make_async_copy(total[k].at[s], outs[k].at[s], out_sems.at[q])
                started.append((k, s, sl.half, load, cp, store))
            base += len(sl.arrays)
        for k, s, half_rows, load, cp, store in started:
            load.wait()
            cp.wait_recv()
            rows = min(half_rows, PAIR_SUM_ROWS)

            def add(t, carry, k=k, s=s, rows=rows):
                sl_ = pl.ds(pl.multiple_of(t * rows, rows), rows)
                total[k][s, sl_, :] = (mine[k][s, sl_, :] + got[k][s, sl_, :]).astype(_WIRE)
                return carry
            lax.fori_loop(0, half_rows // rows, add, 0)
            store.start()
        for k, s, half_rows, load, cp, store in started:
            cp.wait_send()
            store.wait()

    flat_in = [a for sl in slabs for a in sl.arrays]
    shapes = [(N_CHIPS, sl.half, sl.cols) for sl in slabs]
    vmem = [pltpu.VMEM(sh, dt) for dt in (F32, F32, _WIRE) for sh in shapes]
    return pl.pallas_call(
        body, name=name,
        in_specs=[ANY] * n_in, out_specs=[ANY] * n, out_shape=[pltpu.HBM(sh, _WIRE) for sh in shapes],
        scratch_shapes=vmem + [pltpu.SemaphoreType.DMA((N_CHIPS * n,))] * 4,
        compiler_params=_comm_params(sum(_nbytes(sh, F32) * 2 + _nbytes(sh, _WIRE) for sh in shapes)),
    )(*_in_hbm(flat_in))


class _ChipExchange:
    def __init__(self, n, ins, outs, bufs, sems):
        self.n, self.ins, self.outs, self.bufs = n, ins, outs, bufs
        self.send_sems, self.recv_sems, self.in_sems, self.out_sems = sems
        self.x, self.y, self.c = _place()
        self.me = 2 * self.x + self.y
        self.pairs = [(k, j) for k in range(n) for j in range(3)]

    @staticmethod
    def scratch(partials):
        n = len(partials)
        return ([pltpu.VMEM(p.shape[1:], p.dtype) for p in partials]
                + [pltpu.SemaphoreType.DMA((3 * n,))] * 2 + [pltpu.SemaphoreType.DMA((n,))] * 2)

    @staticmethod
    def scratch_bytes(partials):
        return sum(_nbytes(p.shape[1:], p.dtype) for p in partials)

    def _copy(self, k, j, src_slot, dst_slot):
        px, py = _flip(self.x, CHIP_MOVES[j][0]), _flip(self.y, CHIP_MOVES[j][1])
        return pltpu.make_async_remote_copy(
            src_ref=self.ins[k].at[src_slot], dst_ref=self.outs[k].at[dst_slot],
            send_sem=self.send_sems.at[3 * k + j], recv_sem=self.recv_sems.at[3 * k + j],
            device_id=(px, py, self.c), device_id_type=MESH_ID)

    def _peer(self, j):
        return 2 * _flip(self.x, CHIP_MOVES[j][0]) + _flip(self.y, CHIP_MOVES[j][1])

    def _local(self, k):
        return _Bounce(self.ins[k].at[self.me], self.bufs[k], self.outs[k].at[self.me],
                       self.in_sems.at[k], self.out_sems.at[k])

    def start(self):
        for k in range(self.n):
            self._local(k).start()
        for k, j in self.pairs:
            self._copy(k, j, self._peer(j), self.me).start()

    def turn(self):
        for k in range(self.n):
            self._local(k).turn()

    def finish(self):
        for k, j in self.pairs:
            self._copy(k, j, self.me, self._peer(j)).wait_recv()
        for k, j in self.pairs:
            self._copy(k, j, self._peer(j), self.me).wait_send()
        for k in range(self.n):
            self._local(k).wait()


def _hosted_steps(steps):
    return dict(start=0, turn=steps // 4, forward=steps // 2, finish=steps - 1)


def _pair_share(halves, vec):
    n = len(halves)
    nv = len(DEV_MOVES)

    def body(*refs):
        ins, vec_ref = refs[:n], refs[n]
        outs, vec_out = refs[n + 1:2 * n + 1], refs[2 * n + 1]
        bufs = refs[2 * n + 2:3 * n + 3]
        send_sems, recv_sems, in_sems, out_sems = refs[3 * n + 3:]
        x, y, c = _place()
        dev = 4 * x + 2 * y + c

        def vec_copy(j, slot):
            dx, dy, dc = DEV_MOVES[j]
            return pltpu.make_async_remote_copy(
                src_ref=vec_ref, dst_ref=vec_out.at[slot], send_sem=send_sems.at[n + j], recv_sem=recv_sems.at[n + j],
                device_id=(_flip(x, dx), _flip(y, dy), _flip(c, dc)), device_id_type=MESH_ID)

        def rows(k, hc):
            hr = halves[k].shape[0]
            return outs[k].at[pl.ds(pl.multiple_of(hc * hr, 8), hr), :]

        def share(k, hc):
            return pltpu.make_async_remote_copy(
                src_ref=ins[k], dst_ref=rows(k, hc), send_sem=send_sems.at[k], recv_sem=recv_sems.at[k],
                device_id=(x, y, 1 - c), device_id_type=MESH_ID)

        locs = [_Bounce(vec_ref, bufs[n], vec_out.at[dev], in_sems.at[n], out_sems.at[n])]
        locs += [_Bounce(ins[k], bufs[k], rows(k, c), in_sems.at[k], out_sems.at[k]) for k in range(n)]
        for loc in locs:
            loc.start()
        for j in range(nv):
            vec_copy(j, dev).start()
        for k in range(n):
            share(k, c).start()
        for loc in locs:
            loc.turn()
        for k in range(n):
            share(k, 1 - c).wait_recv()
        for j, (dx, dy, dc) in enumerate(DEV_MOVES):
            vec_copy(j, 4 * _flip(x, dx) + 2 * _flip(y, dy) + _flip(c, dc)).wait_recv()
        for k in range(n):
            share(k, c).wait_send()
        for j in range(nv):
            vec_copy(j, dev).wait_send()
        for loc in locs:
            loc.wait()

    outs = pl.pallas_call(
        body, name="grad_pair_share",
        in_specs=[ANY] * (n + 1), out_specs=[ANY] * (n + 1),
        out_shape=[pltpu.HBM((2 * h.shape[0], h.shape[1]), F32) for h in halves] + [pltpu.HBM((N_DEV,) + vec.shape, F32)],
        scratch_shapes=[pltpu.VMEM(h.shape, F32) for h in halves] + [pltpu.VMEM(vec.shape, F32)]
        + [pltpu.SemaphoreType.DMA((n + nv,))] * 2 + [pltpu.SemaphoreType.DMA((n + 1,))] * 2,
        compiler_params=_comm_params(sum(_nbytes(h.shape, F32) for h in halves) + _nbytes(vec.shape, F32)),
    )(*_in_hbm(list(halves) + [vec]))
    return outs[:n], outs[n]


def _row_block(rows, cols, n_arrays):
    br = rows
    while br % 16 == 0 and 2 * n_arrays * br * cols * 4 > (16 << 20):
        br //= 2
    return br


def _sum_slots(a, name):
    n, rows, cols = a.shape
    br = _row_block(rows, cols, n + 1)

    def body(a_ref, o_ref):
        acc = a_ref[0].astype(F32)
        for s in range(1, n):
            acc = acc + a_ref[s].astype(F32)
        o_ref[...] = acc

    return pl.pallas_call(body, name=name, grid=(rows // br,),
                          in_specs=[pl.BlockSpec((n, br, cols), lambda i: (0, i, 0))],
                          out_specs=pl.BlockSpec((br, cols), lambda i: (i, 0)),
                          out_shape=pltpu.HBM((rows, cols), F32),
                          compiler_params=_cparams((n + 1) * br * cols * 4, 1))(*_in_hbm([a]))


def _adamw_math(w, g, m, v):
    m = ADAM_B1 * m + (1.0 - ADAM_B1) * g
    v = ADAM_B2 * v + (1.0 - ADAM_B2) * (g * g)
    m_hat = m / (1.0 - ADAM_B1 ** ADAM_STEP)
    v_hat = v / (1.0 - ADAM_B2 ** ADAM_STEP)
    delta = -ADAM_LR * (m_hat / (jnp.sqrt(v_hat) + ADAM_EPS) + ADAM_WD * w)
    return delta, m, v


def _adamw(g, w, m, v, name):
    rows, cols = g.shape
    br = _row_block(rows, cols, 7)

    def body(g_ref, w_ref, m_ref, v_ref, d_ref, nm_ref, nv_ref):
        d_ref[...], nm_ref[...], nv_ref[...] = _adamw_math(w_ref[...], g_ref[...], m_ref[...], v_ref[...])

    spec = pl.BlockSpec((br, cols), lambda i: (i, 0))
    return pl.pallas_call(body, name=name, grid=(rows // br,), in_specs=[spec] * 4, out_specs=[spec] * 3,
                          out_shape=[jax.ShapeDtypeStruct(g.shape, F32)] * 3,
                          compiler_params=_cparams(7 * br * cols * 4, 1))(g, w, m, v)


ROW_FINAL_G, ROW_PE_G, ROW_LOSS = 0, 1, 2
ROW_CONV_B, ROW_CN_G, ROW_CN_B, ROW_B_PW2 = 8, 9, 10, 11
ROW_LN_G = 16
ROW_ONORM_G, ROW_LB = 24, 25
ROW_CONV_W = 32
SMALL = ["ln_g", "conv_b", "cnorm_g", "cnorm_b", "b_pw2", "onorm_g", "pe_norm_g", "final_g"]
SMALL_ROW = dict(ln_g=ROW_LN_G, conv_b=ROW_CONV_B, cnorm_g=ROW_CN_G, cnorm_b=ROW_CN_B, b_pw2=ROW_B_PW2,
                 onorm_g=ROW_ONORM_G, pe_norm_g=ROW_PE_G, final_g=ROW_FINAL_G)


def _adamw_small(vsum, gcw, lb_logits, params):
    names = SMALL + ["lb_logits", "conv_w"]
    flat = [t for nm in names for t in params[nm]]

    def body(*refs):
        vs_ref, gcw_ref, lbl_ref = refs[:3]
        ins = refs[3:3 + 3 * len(names)]
        outs = refs[3 + 3 * len(names):]
        for q, nm in enumerate(names):
            w_ref, m_ref, v_ref = ins[3 * q:3 * q + 3]
            g_ref, d_ref, nm_ref, nv_ref = outs[4 * q:4 * q + 4]
            if nm == "conv_w":
                g = gcw_ref[...]
            elif nm == "lb_logits":
                lb = _softmax_row0(lbl_ref[...])
                g0 = vs_ref[ROW_LB:ROW_LB + 1, :] * lb * (1.0 - lb)
                g = jnp.concatenate([g0, -g0], axis=0)
            else:
                g = vs_ref[SMALL_ROW[nm]:SMALL_ROW[nm] + 1, :]
            g_ref[...] = g
            d_ref[...], nm_ref[...], nv_ref[...] = _adamw_math(w_ref[...], g, m_ref[...], v_ref[...])

    out_shape = [jax.ShapeDtypeStruct(params[nm][0].shape, F32) for nm in names for _ in range(4)]
    outs = pl.pallas_call(body, name="adamw_small", out_shape=out_shape)(vsum, gcw, lb_logits, *flat)
    return {nm: tuple(outs[4 * q:4 * q + 4]) for q, nm in enumerate(names)}


TOKEN_TILE = dict(rmsnorm=512, inproj_fwd=2048, conv=256, hgrn=512, tail=512, inproj_bwd_u=2048, inproj_bwd_x=512,
                  weight_grad=2048)


def _tile(T, family):
    return min(T, TOKEN_TILE[family])


def kernel(x, p, ln_g, w_in, conv_w, conv_b, cnorm_g, cnorm_b, w_pw2, b_pw2, lb_logits, onorm_g, w_out, pe_norm_g, w_pg, w_pp, final_g, loss_target, m_ln_g, m_w_in, m_conv_w, m_conv_b, m_cnorm_g, m_cnorm_b, m_w_pw2, m_b_pw2, m_lb_logits, m_onorm_g, m_w_out, m_pe_norm_g, m_w_pg, m_w_pp, m_final_g, v_ln_g, v_w_in, v_conv_w, v_conv_b, v_cnorm_g, v_cnorm_b, v_w_pw2, v_b_pw2, v_lb_logits, v_onorm_g, v_w_out, v_pe_norm_g, v_w_pg, v_w_pp, v_final_g):
    given = dict(locals())
    x2, p2, tgt = x[0], p[0, 0], loss_target[0]
    T = x2.shape[0]
    fin_g = final_g.reshape(1, D)

    conv_w_pad = jnp.pad(conv_w[0], ((0, HALO - CONV_K), (0, 0)))
    u, (w_in_f,) = _rmsnorm_gather(x2, ln_g, [w_in[0].astype(_MXU)], [1], _tile(T, "rmsnorm"))

    z, (w_pw2_f, w_out_f, w_pg_f, w_pp_f, conv_w_f) = _inproj_fwd(
        u, w_in_f,
        [w_pw2[0].astype(_MXU), w_out[0].astype(_MXU), w_pg[0].astype(_MXU), w_pp[0].astype(_MXU), conv_w_pad],
        [0, 0, 0, 1, 1], _tile(T, "inproj_fwd"))
    yc, y1 = _conv_fwd(z, conv_w_f, conv_b, cnorm_g, cnorm_b, w_pw2_f, b_pw2, _tile(T, "conv"))
    o_raw, yh, s_chunks = _hgrn_fwd(z, lb_logits, onorm_g, _tile(T, "hgrn"), HB)
    dyc, dyh, dh, n2, ds, dpe, dhb, pb, vec_tail = _tail(
        x2, yc, yh, p2, tgt, w_out_f, w_pg_f, w_pp_f, pe_norm_g, fin_g, _tile(T, "tail"))
    tk = _tile(T, "weight_grad")
    g_w_out_c = _tn_matmul(yc, dhb, tk, "grad_w_out_conv")
    g_w_out_h = _tn_matmul(yh, dhb, tk, "grad_w_out_hgrn")
    g_w_pg = _tn_matmul(n2, ds, tk, "grad_w_pg")
    g_w_pp = _tn_matmul(pb, dpe, tk, "grad_w_pp")
    dzc, a_act, dy2, vec_conv, g_conv_w = _conv_bwd(z, y1, dyc, conv_w_f, cnorm_g, cnorm_b, w_pw2_f, b_pw2, _tile(T, "conv"))
    g_w_pw2 = _tn_matmul(a_act, dy2, tk, "grad_w_pw2")

    rest = ["w_pw2", "w_out", "w_pg", "w_pp"]
    partial_rest = _pair_exchange_sum([
        _Slab([g_w_pw2], lambda s: (0, s * (D // N_CHIPS), 0), (D // N_CHIPS, D)),
        _Slab([g_w_out_c, g_w_out_h], lambda s: (s // 2, (s % 2) * (D // 2), 0), (D // 2, D)),
        _Slab([g_w_pg], lambda s: (0, s * (D // N_CHIPS), 0), (D // N_CHIPS, D)),
        _Slab([g_w_pp], lambda s: (0, 0, s * (D // N_CHIPS)), (PLE, D // N_CHIPS)),
    ], "grad_pair_exchange_rest")
    dzh, vec_hgrn, slots_rest = _hgrn_bwd(z, lb_logits, onorm_g, o_raw, dyh, s_chunks, partial_rest, _tile(T, "hgrn"), HB)
    g_w_in = _inproj_bwd_w(u, dzc, dzh, tk)
    partial_in = _pair_exchange_sum([
        _Slab([g_w_in], lambda s: (0, 0, s * (NPART * D // N_CHIPS)), (D, NPART * D // N_CHIPS))], "grad_pair_exchange_w_in")
    du, slots_in = _inproj_bwd_u(dzc, dzh, w_in_f, partial_in, _tile(T, "inproj_bwd_u"))
    grad_x, vec_in = _inproj_bwd_x(x2, ln_g, du, dh, _tile(T, "inproj_bwd_x"))
    big = ["w_in"] + rest
    halves = [_sum_slots(s, "chip_sum_" + nm) for s, nm in zip(list(slots_in) + list(slots_rest), big)]
    vec = jnp.concatenate([vec_tail, vec_conv, vec_in, vec_hgrn, g_conv_w], axis=0)
    grads_big, vec_slots = _pair_share(halves, vec)
    vsum = _sum_slots(vec_slots, "vec_sum")

    out = {}
    for nm, g in zip(big, grads_big):
        w2, m2, v2 = given[nm][0], given["m_" + nm][0], given["v_" + nm][0]
        d, nm_, nv_ = _adamw(g, w2, m2, v2, "adamw_" + nm)
        out[nm] = tuple(t[None] for t in (g, d, nm_, nv_))
    chip = 2 * lax.axis_index("x") + lax.axis_index("y")
    gcw = lax.dynamic_slice(vsum, (ROW_CONV_W, chip * (D // N_CHIPS)), (CONV_K, D // N_CHIPS))
    params = {nm: (given[nm].reshape(-1, D), given["m_" + nm].reshape(-1, D), given["v_" + nm].reshape(-1, D))
              for nm in SMALL + ["lb_logits"]}
    params["conv_w"] = (conv_w[0], m_conv_w[0], v_conv_w[0])
    small = _adamw_small(vsum, gcw, lb_logits, params)
    for nm, ts in small.items():
        out[nm] = tuple(t.reshape(given[nm].shape) for t in ts)

    loss = vsum[ROW_LOSS, 0]
    order = ["ln_g", "w_in", "conv_w", "conv_b", "cnorm_g", "cnorm_b", "w_pw2", "b_pw2", "lb_logits", "onorm_g",
             "w_out", "pe_norm_g", "w_pg", "w_pp", "final_g"]
    return (loss, grad_x[None], *[out[nm][0] for nm in order], *[out[nm][1] for nm in order],
            *[out[nm][2] for nm in order], *[out[nm][3] for nm in order])
```

```python
import functools

import jax
import jax.numpy as jnp
from jax import lax
from jax.experimental import pallas as pl
from jax.experimental.pallas import tpu as pltpu

F32 = jnp.float32
BF16 = jnp.bfloat16
_MXU = jnp.bfloat16
_WIRE = jnp.bfloat16

D = 1024
NPART = 7
PLE = 256
HEADS = 8
HD = 128
CHUNK = 64
CONV_K = 31
HALO = 32
EPS = 1e-6
N_CHIPS = 4
N_DEV = 8
HB = 8
VEC_ROWS = 64

ADAM_LR = 0.001
ADAM_B1 = 0.9
ADAM_B2 = 0.999
ADAM_EPS = 1e-08
ADAM_WD = 0.01
ADAM_STEP = 10

V7X_VMEM_LIMIT = 60000 * 1024
MESH_ID = pl.DeviceIdType.MESH
ANY = pl.BlockSpec(memory_space=pltpu.HBM)


def _in_hbm(arrays):
    return [pltpu.with_memory_space_constraint(a, pltpu.HBM) for a in arrays]


def _cparams(block_bytes, n_grid_dims):
    limit = min(V7X_VMEM_LIMIT, 2 * block_bytes + (24 << 20))
    return pltpu.CompilerParams(vmem_limit_bytes=int(limit), dimension_semantics=("arbitrary",) * n_grid_dims)


def _nbytes(shape, dtype):
    n = 1
    for s in shape:
        n *= s
    return n * jnp.dtype(dtype).itemsize


def _dot(a, b):
    return jnp.dot(a.astype(_MXU), b.astype(_MXU), preferred_element_type=F32)


def _dot_nt(a, b):
    return lax.dot_general(a.astype(_MXU), b.astype(_MXU), (((1,), (1,)), ((), ())), preferred_element_type=F32)


def _dot_tn(a, b):
    return lax.dot_general(a.astype(_MXU), b.astype(_MXU), (((0,), (0,)), ((), ())), preferred_element_type=F32)


def _tri_dot(tri_bf, x):
    x1 = x.astype(BF16)
    r1 = x - x1.astype(F32)
    x2 = r1.astype(BF16)
    x3 = (r1 - x2.astype(F32)).astype(BF16)
    d = lambda t: jnp.dot(tri_bf, t, preferred_element_type=F32)
    return d(x1) + d(x2) + d(x3)


def _split2(x):
    hi = x.astype(BF16)
    return hi, (x - hi.astype(F32)).astype(BF16)


def _dot3(dims, a, b):
    d = lambda p, q: lax.dot_general(p, q, (dims, ((), ())), preferred_element_type=F32)
    return d(a[0], b[0]) + d(a[0], b[1]) + d(a[1], b[0])


def _sigmoid(x):
    return jax.nn.sigmoid(x)


def _mean_lanes(x):
    return jnp.mean(x, axis=-1, keepdims=True)


def _sum_rows(x):
    return jnp.sum(x, axis=0, keepdims=True)


def _group_ln(y):
    yn, rs = [], []
    for g in range(D // HD):
        blk = y[:, g * HD:(g + 1) * HD]
        xc = blk - _mean_lanes(blk)
        r = lax.rsqrt(_mean_lanes(xc * xc) + EPS)
        yn.append(xc * r)
        rs.append(jnp.broadcast_to(r, blk.shape))
    return jnp.concatenate(yn, axis=1), jnp.concatenate(rs, axis=1)


def _group_ln_bwd(dyn, yn, rstd):
    out = []
    for g in range(D // HD):
        sl = slice(g * HD, (g + 1) * HD)
        d, n = dyn[:, sl], yn[:, sl]
        out.append(rstd[:, sl] * (d - _mean_lanes(d) - n * _mean_lanes(d * n)))
    return jnp.concatenate(out, axis=1)


def _head_means(x, hb, fn=lambda m: m):
    return jnp.concatenate([jnp.broadcast_to(fn(_mean_lanes(x[:, hh * HD:(hh + 1) * HD])), (x.shape[0], HD))
                            for hh in range(hb)], axis=1)


def _head_rsqrt_mean(x, hb):
    return _head_means(x, hb, lambda m: lax.rsqrt(m + EPS))


def _softmax_row0(lbl):
    m = jnp.max(lbl, axis=0, keepdims=True)
    e = jnp.exp(lbl - m)
    return e[0:1, :] / jnp.sum(e, axis=0, keepdims=True)


def _hosted_gather(phases, step, at, shards, axes, ins, outs, bufs, sems):
    gather = _Gather([s.shape for s in shards], axes, ins, outs, bufs, sems)
    for phase in phases:
        pl.when(step == at[phase])(getattr(gather, phase))


def _rmsnorm_gather(x, ln_g, shards, axes, tT):
    T = x.shape[0]
    n = len(shards)
    at = _hosted_steps(T // tT)

    def body(x_ref, g_ref, *rest):
        ins, u_ref, outs, bufs, sems = rest[:n], rest[n], rest[n + 1:2 * n + 1], rest[2 * n + 1:3 * n + 1], rest[3 * n + 1:]
        host = functools.partial(_hosted_gather, step=pl.program_id(0), at=at, shards=shards, axes=axes,
                                 ins=ins, outs=outs, bufs=bufs, sems=sems)
        host(("start", "turn", "forward"))
        xv = x_ref[...]
        r = lax.rsqrt(_mean_lanes(xv * xv) + EPS)
        u_ref[...] = (xv * r * g_ref[...]).astype(_MXU)
        host(("finish",))

    blk = _nbytes((tT, D), F32) * 2 + _nbytes((tT, D), _MXU) + sum(_nbytes(s.shape, s.dtype) for s in shards)
    outs = pl.pallas_call(
        body, name="rmsnorm_gather", grid=(T // tT,),
        in_specs=[pl.BlockSpec((tT, D), lambda i: (i, 0)), pl.BlockSpec((1, D), lambda i: (0, 0))] + [ANY] * n,
        out_specs=[pl.BlockSpec((tT, D), lambda i: (i, 0))] + [ANY] * n,
        out_shape=[jax.ShapeDtypeStruct((T, D), _MXU)]
        + [pltpu.HBM(fs, s.dtype) for fs, s in zip(_full_shapes(shards, axes), shards)],
        scratch_shapes=_Gather.scratch(shards),
        compiler_params=_cparams(blk, 1),
    )(x, ln_g, *_in_hbm(shards))
    return outs[0], outs[1:]


def _inproj_fwd(u, w_in, shards, axes, tT):
    T = u.shape[0]
    n = len(shards)
    at = _hosted_steps((T // tT) * NPART)

    def body(u_ref, w_ref, *rest):
        ins, z_ref, outs, bufs, sems = rest[:n], rest[n], rest[n + 1:2 * n + 1], rest[2 * n + 1:3 * n + 1], rest[3 * n + 1:]
        host = functools.partial(_hosted_gather, step=pl.program_id(0) * NPART + pl.program_id(1), at=at, shards=shards,
                                 axes=axes, ins=ins, outs=outs, bufs=bufs, sems=sems)
        host(("start", "turn", "forward"))
        z_ref[...] = jnp.dot(u_ref[...], w_ref[...], preferred_element_type=F32)
        host(("finish",))

    blk = (_nbytes((tT, D), F32) + _nbytes((D, D), _MXU) + _nbytes((tT, D), _MXU)
           + sum(_nbytes(s.shape, s.dtype) for s in shards))
    outs = pl.pallas_call(
        body, name="inproj_fwd", grid=(T // tT, NPART),
        in_specs=[pl.BlockSpec((tT, D), lambda i, j: (i, 0)), pl.BlockSpec((D, D), lambda i, j: (0, j))] + [ANY] * n,
        out_specs=[pl.BlockSpec((tT, D), lambda i, j: (i, j))] + [ANY] * n,
        out_shape=[jax.ShapeDtypeStruct((T, NPART * D), F32)]
        + [pltpu.HBM(fs, s.dtype) for fs, s in zip(_full_shapes(shards, axes), shards)],
        scratch_shapes=_Gather.scratch(shards),
        compiler_params=_cparams(blk, 2),
    )(u, w_in, *_in_hbm(shards))
    return outs[0], outs[1:]


def _shifted_windows(ext, first, visit):
    n = ext.shape[0]
    for m in range(first, first + CONV_K):
        visit(m, (ext if m == 0 else pltpu.roll(ext, n - m, axis=0))[0:n - HALO, :])


def _conv_fwd(z, conv_w, conv_b, cn_g, cn_b, w_pw2, b_pw2, tT):
    T = z.shape[0]

    def body(cv_ref, cg_ref, ct_ref, cw_ref, cb_ref, ng_ref, nb_ref, wp_ref, bp_ref, yc_ref, y1_ref, ext):
        @pl.when(pl.program_id(0) == 0)
        def _():
            ext[...] = jnp.zeros_like(ext)
        ext[0:HALO, :] = ext[tT:tT + HALO, :]
        ext[HALO:, :] = cv_ref[...] * _sigmoid(cg_ref[...])
        cw = cw_ref[...]
        acc = [cb_ref[...]]

        def tap(m, win):
            acc[0] = acc[0] + win * cw[m - 2:m - 1, :]
        _shifted_windows(ext[...], 2, tap)
        y1 = acc[0]
        y1_ref[...] = y1
        yn, _ = _group_ln(y1)
        apre = yn * ng_ref[...] + nb_ref[...]
        a = apre * _sigmoid(apre)
        y2 = _dot(a, wp_ref[...]) + bp_ref[...]
        ct = ct_ref[...]
        yc_ref[...] = (y2 * (ct * _sigmoid(ct))).astype(_MXU)

    part = lambda p: pl.BlockSpec((tT, D), lambda i: (i, p))
    row = pl.BlockSpec((1, D), lambda i: (0, 0))
    tok = pl.BlockSpec((tT, D), lambda i: (i, 0))
    blk = 4 * _nbytes((tT, D), F32) + _nbytes((D, D), _MXU) + _nbytes((tT, D), _MXU) + 8 * _nbytes((tT + HALO, D), F32)
    return pl.pallas_call(
        body, name="conv_fwd", grid=(T // tT,),
        in_specs=[part(0), part(1), part(2), pl.BlockSpec((HALO, D), lambda i: (0, 0)), row, row, row,
                  pl.BlockSpec((D, D), lambda i: (0, 0)), row],
        out_specs=[tok, tok],
        out_shape=[jax.ShapeDtypeStruct((T, D), _MXU), jax.ShapeDtypeStruct((T, D), F32)],
        scratch_shapes=[pltpu.VMEM((tT + HALO, D), F32)],
        compiler_params=_cparams(blk, 1),
    )(z, z, z, conv_w, conv_b, cn_g, cn_b, w_pw2, b_pw2)


def _hgrn_gates(lb, hq, hf):
    sq = _sigmoid(hq)
    sg = _sigmoid(hf)
    f = lb + (1.0 - lb) * sg
    return sq, sg, f, hq * sq, (1.0 - lb) * (1.0 - sg), jnp.log(f)


def _chunk_decays(lf, q, k):
    r = lax.broadcasted_iota(jnp.int32, (CHUNK, CHUNK), 0)
    c = lax.broadcasted_iota(jnp.int32, (CHUNK, CHUNK), 1)
    b = _tri_dot((r >= c).astype(BF16), lf)
    bm = b[CHUNK // 2 - 1:CHUNK // 2, :]
    bl = b[CHUNK - 1:CHUNK, :]
    eb = jnp.exp(b)
    eqm = jnp.exp(b - bm)
    ekm = jnp.exp(bm - b)
    ekd = jnp.exp(bl - b)
    return dict(causal=r >= c, eb=eb, eqm=eqm, ekm=ekm, ekd=ekd, ebl=jnp.exp(bl),
                qd=q * eb, qm=q * eqm, km=k * ekm, kd=k * ekd)


def _hgrn_fwd(z, lb_logits, onorm_g, tT, hb):
    T = z.shape[0]
    nc = tT // CHUNK
    w = hb * HD

    def body(lbl_ref, og_ref, hq_ref, hf_ref, hi_ref, hg_ref, o_ref, yh_ref, sc_ref, st):
        @pl.when(pl.program_id(1) == 0)
        def _():
            st[...] = jnp.zeros_like(st)
        lb_all = _softmax_row0(lbl_ref[...])
        og_all = og_ref[...]

        def chunk(c, carry):
            sl = pl.ds(pl.multiple_of(c * CHUNK, CHUNK), CHUNK)
            lanes = [slice(hh * HD, (hh + 1) * HD) for hh in range(hb)]
            heads = lambda fn: [fn(hh, ln) for hh, ln in enumerate(lanes)]
            hg, v = hg_ref[sl, :], hi_ref[sl, :]
            _, _, _, q, k, lf = _hgrn_gates(lb_all, hq_ref[sl, :], hf_ref[sl, :])
            dc = _chunk_decays(lf, q, k)
            s_t = heads(lambda hh, ln: st[hh])
            a = heads(lambda hh, ln: jnp.where(dc["causal"], _dot_nt(dc["qm"][:, ln], dc["km"][:, ln]), 0.0))
            o_inter = heads(lambda hh, ln: _dot_nt(dc["qd"][:, ln], s_t[hh]))
            kv = heads(lambda hh, ln: _dot_tn(v[:, ln], dc["kd"][:, ln]))
            o_intra = heads(lambda hh, ln: _dot(a[hh], v[:, ln]))
            for hh, ln in enumerate(lanes):
                sc_ref[hh, c] = s_t[hh]
                st[hh] = s_t[hh] * dc["ebl"][:, ln] + kv[hh]
            o = jnp.concatenate([o_inter[hh] + o_intra[hh] for hh in range(hb)], axis=1)
            o_ref[sl, :] = o
            n = o * _head_rsqrt_mean(o * o, hb)
            yh_ref[sl, :] = ((n * og_all) * (hg * _sigmoid(hg))).astype(_MXU)
            return carry

        lax.fori_loop(0, nc, chunk, 0, unroll=8)

    zpart = lambda p: pl.BlockSpec((tT, w), lambda h, i: (i, p * (HEADS // hb) + h))
    blk = 6 * _nbytes((tT, w), F32) + _nbytes((hb, nc, HD, HD), F32)
    return pl.pallas_call(
        body, name="hgrn_fwd", grid=(HEADS // hb, T // tT),
        in_specs=[pl.BlockSpec((2, w), lambda h, i: (0, h)), pl.BlockSpec((1, w), lambda h, i: (0, h)),
                  zpart(3), zpart(4), zpart(5), zpart(6)],
        out_specs=[pl.BlockSpec((tT, w), lambda h, i: (i, h)), pl.BlockSpec((tT, w), lambda h, i: (i, h)),
                   pl.BlockSpec((hb, nc, HD, HD), lambda h, i: (h, i, 0, 0))],
        out_shape=[jax.ShapeDtypeStruct((T, D), F32), jax.ShapeDtypeStruct((T, D), _MXU),
                   jax.ShapeDtypeStruct((HEADS, T // CHUNK, HD, HD), F32)],
        scratch_shapes=[pltpu.VMEM((hb, HD, HD), F32)],
        compiler_params=_cparams(blk, 2),
    )(lb_logits, onorm_g, z, z, z, z)


def _hgrn_bwd(z, lb_logits, onorm_g, o_raw, dyh, s_chunks, partials, tT, hb):
    T = z.shape[0]
    nc = tT // CHUNK
    nI = T // tT
    w = hb * HD
    n = len(partials)
    at = _hosted_steps((HEADS // hb) * nI)

    def body(lbl_ref, og_ref, hq_ref, hf_ref, hi_ref, hg_ref, o_ref, dy_ref, sc_ref, *rest):
        (dz_ref, vec_ref), dst = rest[n:n + 2], rest[2 * n + 2]
        exchange = _ChipExchange(n, rest[:n], rest[n + 2:2 * n + 2], rest[2 * n + 3:3 * n + 3], rest[3 * n + 3:])
        step = pl.program_id(0) * nI + pl.program_id(1)
        pl.when(step == at["start"])(exchange.start)
        pl.when(step == at["turn"])(exchange.turn)

        @pl.when(pl.program_id(1) == 0)
        def _():
            dst[...] = jnp.zeros_like(dst)
            vec_ref[...] = jnp.zeros_like(vec_ref)
        lb_all = _softmax_row0(lbl_ref[...])
        og_all = og_ref[...]
        last_row = lax.broadcasted_iota(jnp.int32, (CHUNK, w), 0) == CHUNK - 1
        r64 = lax.broadcasted_iota(jnp.int32, (CHUNK, CHUNK), 0)
        c64 = lax.broadcasted_iota(jnp.int32, (CHUNK, CHUNK), 1)
        upper = (c64 >= r64).astype(BF16)
        lanes = [slice(hh * HD, (hh + 1) * HD) for hh in range(hb)]
        heads = lambda fn: [fn(hh, ln) for hh, ln in enumerate(lanes)]
        wide = lambda parts: jnp.concatenate(parts, axis=1)

        def chunk(cc, carry):
            c = nc - 1 - cc
            sl = pl.ds(pl.multiple_of(c * CHUNK, CHUNK), CHUNK)
            hq, hg, v = hq_ref[sl, :], hg_ref[sl, :], hi_ref[sl, :]
            sq, sg, f, q, k, lf = _hgrn_gates(lb_all, hq, hf_ref[sl, :])
            dc = _chunk_decays(lf, q, k)
            s_t = heads(lambda hh, ln: sc_ref[hh, c])
            ds_t = heads(lambda hh, ln: dst[hh])
            o, dy = o_ref[sl, :], dy_ref[sl, :]
            r = _head_rsqrt_mean(o * o, hb)
            n = o * r
            sgg = _sigmoid(hg)
            silu_g = hg * sgg
            dhg = dy * (n * og_all) * (sgg * (1.0 + hg * (1.0 - sgg)))
            dn = dy * og_all * silu_g
            g_og = _sum_rows(dy * n * silu_g)
            do = r * (dn - n * _head_means(dn * n, hb))
            a = heads(lambda hh, ln: jnp.where(dc["causal"], _dot_nt(dc["qm"][:, ln], dc["km"][:, ln]), 0.0))
            dam = heads(lambda hh, ln: jnp.where(dc["causal"], _dot_nt(do[:, ln], v[:, ln]), 0.0))
            dqd = wide(heads(lambda hh, ln: _dot(do[:, ln], s_t[hh])))
            dkd = wide(heads(lambda hh, ln: _dot(v[:, ln], ds_t[hh])))
            dv_inter = heads(lambda hh, ln: _dot_nt(dc["kd"][:, ln], ds_t[hh]))
            dqs = heads(lambda hh, ln: _dot_tn(do[:, ln], dc["qd"][:, ln]))
            dv = wide(heads(lambda hh, ln: _dot_tn(a[hh], do[:, ln]) + dv_inter[hh]))
            dam2 = [_split2(t) for t in dam]
            km2, qm2 = _split2(dc["km"]), _split2(dc["qm"])
            dqm = wide(heads(lambda hh, ln: _dot3(((1,), (0,)), dam2[hh], (km2[0][:, ln], km2[1][:, ln]))))
            dkm = wide(heads(lambda hh, ln: _dot3(((0,), (0,)), dam2[hh], (qm2[0][:, ln], qm2[1][:, ln]))))
            debl = wide(heads(lambda hh, ln: _sum_rows(ds_t[hh] * s_t[hh])))
            for hh, ln in enumerate(lanes):
                dst[hh] = ds_t[hh] * dc["ebl"][:, ln] + dqs[hh]
            dq = dqd * dc["eb"] + dqm * dc["eqm"]
            dk = dkm * dc["ekm"] + dkd * dc["ekd"]
            dbl = _sum_rows(dkd * dc["kd"]) + debl * dc["ebl"]
            db = dq * q - dk * k + jnp.where(last_row, dbl, 0.0)
            dlf = _tri_dot(upper, db)
            dfk = dlf / f - dk
            dz_ref[0, sl, :] = (dq * (sq * (1.0 + hq * (1.0 - sq)))).astype(_MXU)
            dz_ref[1, sl, :] = (dfk * ((1.0 - lb_all) * sg * (1.0 - sg))).astype(_MXU)
            dz_ref[2, sl, :] = dv.astype(_MXU)
            dz_ref[3, sl, :] = dhg.astype(_MXU)
            vec_ref[0:1, :] += g_og
            vec_ref[1:2, :] += _sum_rows(dfk * (1.0 - sg))
            return carry

        lax.fori_loop(0, nc, chunk, 0, unroll=8)
        pl.when(step == at["finish"])(exchange.finish)

    zpart = lambda p: pl.BlockSpec((tT, w), lambda h, i: (nI - 1 - i, p * (HEADS // hb) + h))
    act = pl.BlockSpec((tT, w), lambda h, i: (nI - 1 - i, h))
    blk = (6 * _nbytes((tT, w), F32) + _nbytes((hb, nc, HD, HD), F32) + 4 * _nbytes((tT, w), _MXU)
           + _ChipExchange.scratch_bytes(partials))
    outs = pl.pallas_call(
        body, name="hgrn_bwd", grid=(HEADS // hb, nI),
        in_specs=[pl.BlockSpec((2, w), lambda h, i: (0, h)), pl.BlockSpec((1, w), lambda h, i: (0, h)),
                  zpart(3), zpart(4), zpart(5), zpart(6), act, act,
                  pl.BlockSpec((hb, nc, HD, HD), lambda h, i: (h, nI - 1 - i, 0, 0))] + [ANY] * n,
        out_specs=[pl.BlockSpec((4, tT, w), lambda h, i: (0, nI - 1 - i, h)),
                   pl.BlockSpec((8, w), lambda h, i: (0, h))] + [ANY] * n,
        out_shape=[jax.ShapeDtypeStruct((4, T, D), _MXU), jax.ShapeDtypeStruct((8, D), F32)]
        + [pltpu.HBM(p.shape, p.dtype) for p in partials],
        scratch_shapes=[pltpu.VMEM((hb, HD, HD), F32)] + _ChipExchange.scratch(partials),
        compiler_params=_cparams(blk, 2),
    )(lb_logits, onorm_g, z, z, z, z, o_raw, dyh, s_chunks, *_in_hbm(partials))
    return outs[0], outs[1], outs[2:]


def _tail(x, yc, yh, p, target, w_out, w_pg, w_pp, pe_g, fin_g, tT):
    T = x.shape[0]

    def body(x_ref, yc_ref, yh_ref, p_ref, t_ref, wo_ref, wg_ref, wp_ref, pg_ref, fg_ref,
             dyc_ref, dyh_ref, dh_ref, n2_ref, ds_ref, dpe_ref, dhb_ref, pb_ref, vec_ref):
        @pl.when(pl.program_id(0) == 0)
        def _():
            vec_ref[...] = jnp.zeros_like(vec_ref)
        wo_c, wo_h = wo_ref[0:D, :], wo_ref[D:2 * D, :]
        h = x_ref[...] + _dot(yc_ref[...], wo_c) + _dot(yh_ref[...], wo_h)
        pb = p_ref[...].astype(_MXU)
        pe = _dot(pb, wp_ref[...])
        r2 = lax.rsqrt(_mean_lanes(h * h) + EPS)
        hn = h * r2
        n2 = (hn * pg_ref[...]).astype(_MXU)
        gate = _sigmoid(_dot(n2, wg_ref[...]))
        h2 = h + gate * pe
        r3 = lax.rsqrt(_mean_lanes(h2 * h2) + EPS)
        h2n = h2 * r3
        err = h2n * fg_ref[...] - t_ref[...]
        vec_ref[ROW_LOSS:ROW_LOSS + 1, :] += 0.5 * jnp.sum(_mean_lanes(err * err))
        dout = err * (1.0 / D)
        vec_ref[0:1, :] += _sum_rows(dout * h2n)
        dn3 = dout * fg_ref[...]
        dh2 = r3 * (dn3 - h2n * _mean_lanes(dn3 * h2n))
        ds = (dh2 * pe * gate * (1.0 - gate)).astype(_MXU)
        dn2 = _dot_nt(ds, wg_ref[...])
        vec_ref[1:2, :] += _sum_rows(dn2 * hn)
        dnn = dn2 * pg_ref[...]
        dh = dh2 + r2 * (dnn - hn * _mean_lanes(dnn * hn))
        dhb = dh.astype(_MXU)
        dyc_ref[...] = _dot_nt(dhb, wo_c)
        dyh_ref[...] = _dot_nt(dhb, wo_h)
        dh_ref[...] = dh
        n2_ref[...] = n2
        ds_ref[...] = ds
        dpe_ref[...] = (dh2 * gate).astype(_MXU)
        dhb_ref[...] = dhb
        pb_ref[...] = pb

    tok = lambda w: pl.BlockSpec((tT, w), lambda i: (i, 0))
    full = lambda r, c: pl.BlockSpec((r, c), lambda i: (0, 0))
    tokshape = lambda w, dt: jax.ShapeDtypeStruct((T, w), dt)
    blk = (5 * _nbytes((tT, D), F32) + 7 * _nbytes((tT, D), _MXU) + _nbytes((4 * D + PLE, D), _MXU)
           + 12 * _nbytes((tT, D), F32))
    return pl.pallas_call(
        body, name="tail_fwd_bwd", grid=(T // tT,),
        in_specs=[tok(D), tok(D), tok(D), tok(PLE), tok(D), full(2 * D, D), full(D, D), full(PLE, D), full(1, D), full(1, D)],
        out_specs=[tok(D), tok(D), tok(D), tok(D), tok(D), tok(D), tok(D), tok(PLE), full(8, D)],
        out_shape=[tokshape(D, F32), tokshape(D, F32), tokshape(D, F32), tokshape(D, _MXU), tokshape(D, _MXU),
                   tokshape(D, _MXU), tokshape(D, _MXU), tokshape(PLE, _MXU),
                   jax.ShapeDtypeStruct((8, D), F32)],
        compiler_params=_cparams(blk, 1),
    )(x, yc, yh, p, target, w_out, w_pg, w_pp, pe_g, fin_g)


def _conv_bwd(z, y1, dyc, conv_w, cn_g, cn_b, w_pw2, b_pw2, tT):
    T = z.shape[0]
    nI = T // tT
    hb = tT // HALO

    def body(cv_ref, cg_ref, ct_ref, hv_ref, hg_ref, y1_ref, dyc_ref, cw_ref, ng_ref, nb_ref, wp_ref, bp_ref,
             dz_ref, a_ref, dy2_ref, vec_ref, gcw_ref, ext, ext2, gpart):
        i = pl.program_id(0)

        @pl.when(i == 0)
        def _():
            ext2[...] = jnp.zeros_like(ext2)
            gpart[...] = jnp.zeros_like(gpart)
            vec_ref[...] = jnp.zeros_like(vec_ref)
        cv, cg, ct = cv_ref[...], cg_ref[...], ct_ref[...]
        sg = _sigmoid(cg)
        has_hist = (i < nI - 1).astype(F32)
        ext[0:HALO, :] = hv_ref[...] * _sigmoid(hg_ref[...]) * has_hist
        ext[HALO:, :] = cv * sg
        yn, rstd = _group_ln(y1_ref[...])
        apre = yn * ng_ref[...] + nb_ref[...]
        sa = _sigmoid(apre)
        a = (apre * sa).astype(_MXU)
        y2 = _dot(a, wp_ref[...]) + bp_ref[...]
        st = _sigmoid(ct)
        dyc_v = dyc_ref[...]
        dy2 = dyc_v * (ct * st)
        dy2b = dy2.astype(_MXU)
        da = _dot_nt(dy2b, wp_ref[...])
        dapre = da * (sa * (1.0 + apre * (1.0 - sa)))
        dy1 = _group_ln_bwd(dapre * ng_ref[...], yn, rstd)
        vec_ref[0:1, :] += _sum_rows(dy1)
        vec_ref[1:2, :] += _sum_rows(dapre * yn)
        vec_ref[2:3, :] += _sum_rows(dapre)
        vec_ref[3:4, :] += _sum_rows(dy2)
        dz_ref[2] = (dyc_v * y2 * (st * (1.0 + ct * (1.0 - st)))).astype(_MXU)
        a_ref[...] = a
        dy2_ref[...] = dy2b
        ext2[tT:tT + HALO, :] = ext2[0:HALO, :]
        ext2[0:tT, :] = dy1
        def grad_tap(m, win):
            p = dy1 * win
            part = p[0:8, :]
            for q in range(1, tT // 8):
                part = part + p[8 * q:8 * q + 8, :]
            gpart[m - 2] += part
        _shifted_windows(ext[...], 2, grad_tap)
        cw = cw_ref[...]
        acc = [None]

        def dv_tap(m, win):
            term = win * cw[CONV_K - 1 - m:CONV_K - m, :]
            acc[0] = term if acc[0] is None else acc[0] + term
        _shifted_windows(ext2[...], 0, dv_tap)
        dv = acc[0]
        dz_ref[0] = (dv * sg).astype(_MXU)
        dz_ref[1] = (dv * cv * sg * (1.0 - sg)).astype(_MXU)

        @pl.when(i == nI - 1)
        def _():
            gcw_ref[...] = jnp.sum(gpart[...], axis=1)

    part = lambda p: pl.BlockSpec((tT, D), lambda i: (nI - 1 - i, p))
    hist = lambda p: pl.BlockSpec((HALO, D), lambda i: (jnp.maximum((nI - 1 - i) * hb - 1, 0), p))
    tok = pl.BlockSpec((tT, D), lambda i: (nI - 1 - i, 0))
    row = pl.BlockSpec((1, D), lambda i: (0, 0))
    blk = (5 * _nbytes((tT, D), F32) + _nbytes((D, D), _MXU) + 5 * _nbytes((tT, D), _MXU)
           + 10 * _nbytes((tT + HALO, D), F32))
    return pl.pallas_call(
        body, name="conv_bwd", grid=(nI,),
        in_specs=[part(0), part(1), part(2), hist(0), hist(1), tok, tok, pl.BlockSpec((HALO, D), lambda i: (0, 0)),
                  row, row, pl.BlockSpec((D, D), lambda i: (0, 0)), row],
        out_specs=[pl.BlockSpec((3, tT, D), lambda i: (0, nI - 1 - i, 0)), tok, tok,
                   pl.BlockSpec((8, D), lambda i: (0, 0)), pl.BlockSpec((HALO, D), lambda i: (0, 0))],
        out_shape=[jax.ShapeDtypeStruct((3, T, D), _MXU), jax.ShapeDtypeStruct((T, D), _MXU),
                   jax.ShapeDtypeStruct((T, D), _MXU), jax.ShapeDtypeStruct((8, D), F32),
                   jax.ShapeDtypeStruct((HALO, D), F32)],
        scratch_shapes=[pltpu.VMEM((tT + HALO, D), F32), pltpu.VMEM((tT + HALO, D), F32), pltpu.VMEM((HALO, 8, D), F32)],
        compiler_params=_cparams(blk, 1),
    )(z, z, z, z, z, y1, dyc, conv_w, cn_g, cn_b, w_pw2, b_pw2)


def _inproj_bwd_u(dzc, dzh, w_in, partials, tT):
    T = dzc.shape[1]
    n = len(partials)
    at = _hosted_steps((T // tT) * NPART)

    def body(dzc_ref, dzh_ref, w_ref, *rest):
        du_ref = rest[n]
        exchange = _ChipExchange(n, rest[:n], rest[n + 1:2 * n + 1], rest[2 * n + 1:3 * n + 1], rest[3 * n + 1:])
        j = pl.program_id(1)
        step = pl.program_id(0) * NPART + j
        pl.when(step == at["start"])(exchange.start)
        pl.when(step == at["turn"])(exchange.turn)

        @pl.when(j == 0)
        def _():
            du_ref[...] = jnp.zeros_like(du_ref)

        @pl.when(j < 3)
        def _():
            du_ref[...] += _dot_nt(dzc_ref[0], w_ref[...])

        @pl.when(j >= 3)
        def _():
            du_ref[...] += _dot_nt(dzh_ref[0], w_ref[...])
        pl.when(step == at["finish"])(exchange.finish)

    blk = (2 * _nbytes((tT, D), _MXU) + _nbytes((D, D), _MXU) + 2 * _nbytes((tT, D), F32)
           + _ChipExchange.scratch_bytes(partials))
    outs = pl.pallas_call(
        body, name="inproj_bwd_u", grid=(T // tT, NPART),
        in_specs=[pl.BlockSpec((1, tT, D), lambda i, j: (jnp.minimum(j, 2), i, 0)),
                  pl.BlockSpec((1, tT, D), lambda i, j: (jnp.maximum(j - 3, 0), i, 0)),
                  pl.BlockSpec((D, D), lambda i, j: (0, j))] + [ANY] * n,
        out_specs=[pl.BlockSpec((tT, D), lambda i, j: (i, 0))] + [ANY] * n,
        out_shape=[jax.ShapeDtypeStruct((T, D), F32)] + [pltpu.HBM(p.shape, p.dtype) for p in partials],
        scratch_shapes=_ChipExchange.scratch(partials),
        compiler_params=_cparams(blk, 2),
    )(dzc, dzh, w_in, *_in_hbm(partials))
    return outs[0], outs[1:]


def _inproj_bwd_x(x, ln_g, du, dh, tT):
    T = x.shape[0]

    def body(x_ref, g_ref, du_ref, dh_ref, gx_ref, vec_ref):
        @pl.when(pl.program_id(0) == 0)
        def _():
            vec_ref[...] = jnp.zeros_like(vec_ref)
        xv = x_ref[...]
        r = lax.rsqrt(_mean_lanes(xv * xv) + EPS)
        xn = xv * r
        duv = du_ref[...]
        vec_ref[0:1, :] += _sum_rows(duv * xn)
        dun = duv * g_ref[...]
        gx_ref[...] = dh_ref[...] + r * (dun - xn * _mean_lanes(dun * xn))

    tok = pl.BlockSpec((tT, D), lambda i: (i, 0))
    return pl.pallas_call(
        body, name="inproj_bwd_x", grid=(T // tT,),
        in_specs=[tok, pl.BlockSpec((1, D), lambda i: (0, 0)), tok, tok],
        out_specs=[tok, pl.BlockSpec((8, D), lambda i: (0, 0))],
        out_shape=[jax.ShapeDtypeStruct((T, D), F32), jax.ShapeDtypeStruct((8, D), F32)],
        compiler_params=_cparams(6 * _nbytes((tT, D), F32), 1),
    )(x, ln_g, du, dh)


def _inproj_bwd_w(u, dzc, dzh, tk):
    T = u.shape[0]
    nK = T // tk

    def body(u_ref, dzc_ref, dzh_ref, gw_ref):
        j, k = pl.program_id(0), pl.program_id(1)

        @pl.when(k == 0)
        def _():
            gw_ref[...] = jnp.zeros_like(gw_ref)

        @pl.when(j < 3)
        def _():
            gw_ref[...] += _dot_tn(u_ref[...], dzc_ref[0])

        @pl.when(j >= 3)
        def _():
            gw_ref[...] += _dot_tn(u_ref[...], dzh_ref[0])

    blk = 3 * _nbytes((tk, D), _MXU) + 2 * _nbytes((D, D), F32)
    return pl.pallas_call(
        body, name="inproj_bwd_w", grid=(NPART, nK),
        in_specs=[pl.BlockSpec((tk, D), lambda j, k: (k, 0)),
                  pl.BlockSpec((1, tk, D), lambda j, k: (jnp.minimum(j, 2), jnp.where(j < 3, k, nK - 1), 0)),
                  pl.BlockSpec((1, tk, D), lambda j, k: (jnp.maximum(j - 3, 0), jnp.where(j < 3, 0, k), 0))],
        out_specs=pl.BlockSpec((D, D), lambda j, k: (0, j)),
        out_shape=jax.ShapeDtypeStruct((D, NPART * D), F32),
        compiler_params=_cparams(blk, 2),
    )(u, dzc, dzh)


def _tn_matmul(a, b, tk, name):
    T, M = a.shape
    N = b.shape[1]

    def body(a_ref, b_ref, o_ref):
        @pl.when(pl.program_id(0) == 0)
        def _():
            o_ref[...] = jnp.zeros_like(o_ref)
        o_ref[...] += _dot_tn(a_ref[...], b_ref[...])

    blk = _nbytes((tk, M), _MXU) + _nbytes((tk, N), _MXU) + 2 * _nbytes((M, N), F32)
    return pl.pallas_call(
        body, name=name, grid=(T // tk,),
        in_specs=[pl.BlockSpec((tk, M), lambda k: (k, 0)), pl.BlockSpec((tk, N), lambda k: (k, 0))],
        out_specs=pl.BlockSpec((M, N), lambda k: (0, 0)),
        out_shape=pltpu.HBM((M, N), F32),
        compiler_params=_cparams(blk, 1),
    )(a, b)


def _place():
    return lax.axis_index("x"), lax.axis_index("y"), lax.axis_index("c")


def _flip(v, d):
    return 1 - v if d else v


CHIP_MOVES = [(1, 0), (0, 1), (1, 1)]
DEV_MOVES = [(dx, dy, dc) for dx in (0, 1) for dy in (0, 1) for dc in (0, 1)][1:]


def _shard_slice(ref, axis, size, s):
    start = pl.multiple_of(s * size, size)
    return ref.at[pl.ds(start, size), :] if axis == 0 else ref.at[:, pl.ds(start, size)]


class _Bounce:
    def __init__(self, src, buf, dst, sem_in, sem_out):
        self.load = pltpu.make_async_copy(src, buf, sem_in)
        self.store = pltpu.make_async_copy(buf, dst, sem_out)

    def start(self):
        self.load.start()

    def turn(self):
        self.load.wait()
        self.store.start()

    def wait(self):
        self.store.wait()


def _comm_params(scratch_bytes):
    return pltpu.CompilerParams(vmem_limit_bytes=int(min(V7X_VMEM_LIMIT, scratch_bytes + (8 << 20))))


class _Gather:
    def __init__(self, shapes, axes, ins, outs, bufs, sems):
        self.shapes, self.axes, self.ins, self.outs, self.bufs = shapes, axes, ins, outs, bufs
        self.ici_send, self.ici_recv, self.d2d_send, self.d2d_recv, self.in_sems, self.out_sems = sems
        self.x, self.y, self.c = _place()
        self.me = 2 * self.x + self.y
        self.pairs = [(k, j) for k in range(len(shapes)) for j in range(3)]

    @staticmethod
    def scratch(shards):
        n = len(shards)
        return ([pltpu.VMEM(s.shape, s.dtype) for s in shards]
                + [pltpu.SemaphoreType.DMA((3 * n,))] * 4 + [pltpu.SemaphoreType.DMA((n,))] * 2)

    def _own_half(self, k, hc):
        half = self.shapes[k][0] // 2
        return self.ins[k].at[pl.ds(pl.multiple_of(hc * half, 16), half), :]

    def _region(self, k, who, hc):
        rows, cols = self.shapes[k]
        half = rows // 2
        if self.axes[k] == 0:
            return self.outs[k].at[pl.ds(pl.multiple_of(who * rows + hc * half, 16), half), :]
        return self.outs[k].at[pl.ds(pl.multiple_of(hc * half, 16), half), pl.ds(pl.multiple_of(who * cols, HD), cols)]

    def _peer(self, j):
        return 2 * _flip(self.x, CHIP_MOVES[j][0]) + _flip(self.y, CHIP_MOVES[j][1])

    def _ici(self, k, j, who, hc):
        dx, dy = CHIP_MOVES[j]
        return pltpu.make_async_remote_copy(
            src_ref=self._own_half(k, hc), dst_ref=self._region(k, who, hc),
            send_sem=self.ici_send.at[3 * k + j], recv_sem=self.ici_recv.at[3 * k + j],
            device_id=(_flip(self.x, dx), _flip(self.y, dy), self.c), device_id_type=MESH_ID)

    def _d2d(self, k, j, who, hc):
        return pltpu.make_async_remote_copy(
            src_ref=self._region(k, who, hc), dst_ref=self._region(k, who, hc),
            send_sem=self.d2d_send.at[3 * k + j], recv_sem=self.d2d_recv.at[3 * k + j],
            device_id=(self.x, self.y, 1 - self.c), device_id_type=MESH_ID)

    def _local(self, k):
        size = self.shapes[k][self.axes[k]]
        return _Bounce(self.ins[k], self.bufs[k], _shard_slice(self.outs[k], self.axes[k], size, self.me),
                       self.in_sems.at[k], self.out_sems.at[k])

    def start(self):
        for k in range(len(self.shapes)):
            self._local(k).start()
        for k, j in self.pairs:
            self._ici(k, j, self.me, self.c).start()

    def turn(self):
        for k in range(len(self.shapes)):
            self._local(k).turn()

    def forward(self):
        for k, j in self.pairs:
            self._ici(k, j, self._peer(j), self.c).wait_recv()
            self._d2d(k, j, self._peer(j), self.c).start()

    def finish(self):
        for k, j in self.pairs:
            self._d2d(k, j, self._peer(j), 1 - self.c).wait_recv()
        for k, j in self.pairs:
            self._ici(k, j, self.me, self.c).wait_send()
            self._d2d(k, j, self._peer(j), self.c).wait_send()
        for k in range(len(self.shapes)):
            self._local(k).wait()


def _full_shapes(shards, axes):
    return [tuple(d * (N_CHIPS if a == ax else 1) for a, d in enumerate(s.shape)) for s, ax in zip(shards, axes)]


class _Slab:
    def __init__(self, arrays, pick, shard_shape):
        self.arrays = arrays
        self.pick = pick
        self.rows, self.cols = shard_shape
        self.half = self.rows // 2


PAIR_SUM_ROWS = 64


def _pair_exchange_sum(slabs, name):
    n = len(slabs)
    n_in = sum(len(sl.arrays) for sl in slabs)

    def body(*refs):
        ins, outs = refs[:n_in], refs[n_in:n_in + n]
        mine, got, total = (refs[n_in + (1 + t) * n:n_in + (2 + t) * n] for t in range(3))
        send_sems, recv_sems, in_sems, out_sems = refs[n_in + 4 * n:]
        x, y, c = _place()
        started = []
        base = 0
        for k, sl in enumerate(slabs):
            for s in range(N_CHIPS):
                ai, r0, c0 = sl.pick(s)
                src = ins[base + ai]

                def half(hc):
                    return src.at[pl.ds(pl.multiple_of(r0 + hc * sl.half, 8), sl.half), pl.ds(c0, sl.cols)]
                q = N_CHIPS * k + s
                load = pltpu.make_async_copy(half(c), mine[k].at[s], in_sems.at[q])
                load.start()
                cp = pltpu.make_async_remote_copy(
                    src_ref=half(1 - c), dst_ref=got[k].at[s], send_sem=send_sems.at[q], recv_sem=recv_sems.at[q],
                    device_id=(x, y, 1 - c), device_id_type=MESH_ID)
                cp.start()
                store = pltpu.make_async_copy(total[k].at[s], outs[k].at[s], out_sems.at[q])
                started.append((k, s, sl.half, load, cp, store))
            base += len(sl.arrays)
        for k, s, half_rows, load, cp, store in started:
            load.wait()
            cp.wait_recv()
            rows = min(half_rows, PAIR_SUM_ROWS)

            def add(t, carry, k=k, s=s, rows=rows):
                sl_ = pl.ds(pl.multiple_of(t * rows, rows), rows)
                total[k][s, sl_, :] = (mine[k][s, sl_, :] + got[k][s, sl_, :]).astype(_WIRE)
                return carry
            lax.fori_loop(0, half_rows // rows, add, 0)
            store.start()
        for k, s, half_rows, load, cp, store in started:
            cp.wait_send()
            store.wait()

    flat_in = [a for sl in slabs for a in sl.arrays]
    shapes = [(N_CHIPS, sl.half, sl.cols) for sl in slabs]
    vmem = [pltpu.VMEM(sh, dt) for dt in (F32, F32, _WIRE) for sh in shapes]
    return pl.pallas_call(
        body, name=name,
        in_specs=[ANY] * n_in, out_specs=[ANY] * n, out_shape=[pltpu.HBM(sh, _WIRE) for sh in shapes],
        scratch_shapes=vmem + [pltpu.SemaphoreType.DMA((N_CHIPS * n,))] * 4,
        compiler_params=_comm_params(sum(_nbytes(sh, F32) * 2 + _nbytes(sh, _WIRE) for sh in shapes)),
    )(*_in_hbm(flat_in))


class _ChipExchange:
    def __init__(self, n, ins, outs, bufs, sems):
        self.n, self.ins, self.outs, self.bufs = n, ins, outs, bufs
        self.send_sems, self.recv_sems, self.in_sems, self.out_sems = sems
        self.x, self.y, self.c = _place()
        self.me = 2 * self.x + self.y
        self.pairs = [(k, j) for k in range(n) for j in range(3)]

    @staticmethod
    def scratch(partials):
        n = len(partials)
        return ([pltpu.VMEM(p.shape[1:], p.dtype) for p in partials]
                + [pltpu.SemaphoreType.DMA((3 * n,))] * 2 + [pltpu.SemaphoreType.DMA((n,))] * 2)

    @staticmethod
    def scratch_bytes(partials):
        return sum(_nbytes(p.shape[1:], p.dtype) for p in partials)

    def _copy(self, k, j, src_slot, dst_slot):
        px, py = _flip(self.x, CHIP_MOVES[j][0]), _flip(self.y, CHIP_MOVES[j][1])
        return pltpu.make_async_remote_copy(
            src_ref=self.ins[k].at[src_slot], dst_ref=self.outs[k].at[dst_slot],
            send_sem=self.send_sems.at[3 * k + j], recv_sem=self.recv_sems.at[3 * k + j],
            device_id=(px, py, self.c), device_id_type=MESH_ID)

    def _peer(self, j):
        return 2 * _flip(self.x, CHIP_MOVES[j][0]) + _flip(self.y, CHIP_MOVES[j][1])

    def _local(self, k):
        return _Bounce(self.ins[k].at[self.me], self.bufs[k], self.outs[k].at[self.me],
                       self.in_sems.at[k], self.out_sems.at[k])

    def start(self):
        for k in range(self.n):
            self._local(k).start()
        for k, j in self.pairs:
            self._copy(k, j, self._peer(j), self.me).start()

    def turn(self):
        for k in range(self.n):
            self._local(k).turn()

    def finish(self):
        for k, j in self.pairs:
            self._copy(k, j, self.me, self._peer(j)).wait_recv()
        for k, j in self.pairs:
            self._copy(k, j, self._peer(j), self.me).wait_send()
        for k in range(self.n):
            self._local(k).wait()


def _hosted_steps(steps):
    return dict(start=0, turn=steps // 4, forward=steps // 2, finish=steps - 1)


def _pair_share(slots, vec):
    n = len(slots)
    nv = len(DEV_MOVES)

    def body(*refs):
        ins, vec_ref = refs[:n], refs[n]
        outs, vec_out = refs[n + 1:2 * n + 1], refs[2 * n + 1]
        slot_b, half_b, vec_b = refs[2 * n + 2:3 * n + 2], refs[3 * n + 2:4 * n + 2], refs[4 * n + 2]
        send_sems, recv_sems, in_sems, out_sems = refs[4 * n + 3:]
        x, y, c = _place()
        dev = 4 * x + 2 * y + c

        def vec_copy(j, slot):
            dx, dy, dc = DEV_MOVES[j]
            return pltpu.make_async_remote_copy(
                src_ref=vec_ref, dst_ref=vec_out.at[slot], send_sem=send_sems.at[n + j], recv_sem=recv_sems.at[n + j],
                device_id=(_flip(x, dx), _flip(y, dy), _flip(c, dc)), device_id_type=MESH_ID)

        def rows(k, hc):
            hr = slots[k].shape[1]
            return outs[k].at[pl.ds(pl.multiple_of(hc * hr, 8), hr), :]

        def share(k, hc):
            return pltpu.make_async_remote_copy(
                src_ref=half_b[k], dst_ref=rows(k, hc), send_sem=send_sems.at[k], recv_sem=recv_sems.at[k],
                device_id=(x, y, 1 - c), device_id_type=MESH_ID)

        vec_loc = _Bounce(vec_ref, vec_b, vec_out.at[dev], in_sems.at[n], out_sems.at[n])
        vec_loc.start()
        for j in range(nv):
            vec_copy(j, dev).start()
        loads = [pltpu.make_async_copy(ins[k], slot_b[k], in_sems.at[k]) for k in range(n)]
        stores = [pltpu.make_async_copy(half_b[k], rows(k, c), out_sems.at[k]) for k in range(n)]
        for load in loads:
            load.start()
        vec_loc.turn()
        for k in range(n):
            loads[k].wait()
            hr = slots[k].shape[1]
            step_rows = min(hr, PAIR_SUM_ROWS)

            def add(t, carry, k=k, step_rows=step_rows):
                sl_ = pl.ds(pl.multiple_of(t * step_rows, step_rows), step_rows)
                acc = slot_b[k][0, sl_, :].astype(F32)
                for s in range(1, N_CHIPS):
                    acc = acc + slot_b[k][s, sl_, :].astype(F32)
                half_b[k][sl_, :] = acc
                return carry
            lax.fori_loop(0, hr // step_rows, add, 0)
            stores[k].start()
            share(k, c).start()
        for k in range(n):
            share(k, 1 - c).wait_recv()
        for j, (dx, dy, dc) in enumerate(DEV_MOVES):
            vec_copy(j, 4 * _flip(x, dx) + 2 * _flip(y, dy) + _flip(c, dc)).wait_recv()
        for k in range(n):
            share(k, c).wait_send()
            stores[k].wait()
        for j in range(nv):
            vec_copy(j, dev).wait_send()
        vec_loc.wait()

    halves = [s.shape[1:] for s in slots]
    vmem = ([pltpu.VMEM(s.shape, s.dtype) for s in slots] + [pltpu.VMEM(h, F32) for h in halves]
            + [pltpu.VMEM(vec.shape, F32)])
    outs = pl.pallas_call(
        body, name="grad_pair_share",
        in_specs=[ANY] * (n + 1), out_specs=[ANY] * (n + 1),
        out_shape=[pltpu.HBM((2 * h[0], h[1]), F32) for h in halves] + [pltpu.HBM((N_DEV,) + vec.shape, F32)],
        scratch_shapes=vmem + [pltpu.SemaphoreType.DMA((n + nv,))] * 2 + [pltpu.SemaphoreType.DMA((n + 1,))] * 2,
        compiler_params=_comm_params(sum(_nbytes(s.shape, s.dtype) for s in slots) + sum(_nbytes(h, F32) for h in halves)
                                     + _nbytes(vec.shape, F32)),
    )(*_in_hbm(list(slots) + [vec]))
    return outs[:n], outs[n]


def _row_block(rows, cols, n_arrays):
    br = rows
    while br % 16 == 0 and 2 * n_arrays * br * cols * 4 > (16 << 20):
        br //= 2
    return br


def _sum_slots(a, name):
    n, rows, cols = a.shape
    br = _row_block(rows, cols, n + 1)

    def body(a_ref, o_ref):
        acc = a_ref[0].astype(F32)
        for s in range(1, n):
            acc = acc + a_ref[s].astype(F32)
        o_ref[...] = acc

    return pl.pallas_call(body, name=name, grid=(rows // br,),
                          in_specs=[pl.BlockSpec((n, br, cols), lambda i: (0, i, 0))],
                          out_specs=pl.BlockSpec((br, cols), lambda i: (i, 0)),
                          out_shape=pltpu.HBM((rows, cols), F32),
                          compiler_params=_cparams((n + 1) * br * cols * 4, 1))(*_in_hbm([a]))


def _adamw_math(w, g, m, v):
    m = ADAM_B1 * m + (1.0 - ADAM_B1) * g
    v = ADAM_B2 * v + (1.0 - ADAM_B2) * (g * g)
    m_hat = m / (1.0 - ADAM_B1 ** ADAM_STEP)
    v_hat = v / (1.0 - ADAM_B2 ** ADAM_STEP)
    delta = -ADAM_LR * (m_hat / (jnp.sqrt(v_hat) + ADAM_EPS) + ADAM_WD * w)
    return delta, m, v


def _adamw(g, w, m, v, name):
    rows, cols = g.shape
    br = _row_block(rows, cols, 7)

    def body(g_ref, w_ref, m_ref, v_ref, d_ref, nm_ref, nv_ref):
        d_ref[...], nm_ref[...], nv_ref[...] = _adamw_math(w_ref[...], g_ref[...], m_ref[...], v_ref[...])

    spec = pl.BlockSpec((br, cols), lambda i: (i, 0))
    return pl.pallas_call(body, name=name, grid=(rows // br,), in_specs=[spec] * 4, out_specs=[spec] * 3,
                          out_shape=[jax.ShapeDtypeStruct(g.shape, F32)] * 3,
                          compiler_params=_cparams(7 * br * cols * 4, 1))(g, w, m, v)


ROW_FINAL_G, ROW_PE_G, ROW_LOSS = 0, 1, 2
ROW_CONV_B, ROW_CN_G, ROW_CN_B, ROW_B_PW2 = 8, 9, 10, 11
ROW_LN_G = 16
ROW_ONORM_G, ROW_LB = 24, 25
ROW_CONV_W = 32
SMALL = ["ln_g", "conv_b", "cnorm_g", "cnorm_b", "b_pw2", "onorm_g", "pe_norm_g", "final_g"]
SMALL_ROW = dict(ln_g=ROW_LN_G, conv_b=ROW_CONV_B, cnorm_g=ROW_CN_G, cnorm_b=ROW_CN_B, b_pw2=ROW_B_PW2,
                 onorm_g=ROW_ONORM_G, pe_norm_g=ROW_PE_G, final_g=ROW_FINAL_G)


def _adamw_small(vsum, gcw, lb_logits, params):
    names = SMALL + ["lb_logits", "conv_w"]
    flat = [t for nm in names for t in params[nm]]

    def body(*refs):
        vs_ref, gcw_ref, lbl_ref = refs[:3]
        ins = refs[3:3 + 3 * len(names)]
        outs = refs[3 + 3 * len(names):]
        for q, nm in enumerate(names):
            w_ref, m_ref, v_ref = ins[3 * q:3 * q + 3]
            g_ref, d_ref, nm_ref, nv_ref = outs[4 * q:4 * q + 4]
            if nm == "conv_w":
                g = gcw_ref[...]
            elif nm == "lb_logits":
                lb = _softmax_row0(lbl_ref[...])
                g0 = vs_ref[ROW_LB:ROW_LB + 1, :] * lb * (1.0 - lb)
                g = jnp.concatenate([g0, -g0], axis=0)
            else:
                g = vs_ref[SMALL_ROW[nm]:SMALL_ROW[nm] + 1, :]
            g_ref[...] = g
            d_ref[...], nm_ref[...], nv_ref[...] = _adamw_math(w_ref[...], g, m_ref[...], v_ref[...])

    out_shape = [jax.ShapeDtypeStruct(params[nm][0].shape, F32) for nm in names for _ in range(4)]
    outs = pl.pallas_call(body, name="adamw_small", out_shape=out_shape)(vsum, gcw, lb_logits, *flat)
    return {nm: tuple(outs[4 * q:4 * q + 4]) for q, nm in enumerate(names)}


TOKEN_TILE = dict(rmsnorm=512, inproj_fwd=2048, conv=256, hgrn=512, tail=512, inproj_bwd_u=2048, inproj_bwd_x=512,
                  weight_grad=2048)


def _tile(T, family):
    return min(T, TOKEN_TILE[family])


def kernel(x, p, ln_g, w_in, conv_w, conv_b, cnorm_g, cnorm_b, w_pw2, b_pw2, lb_logits, onorm_g, w_out, pe_norm_g, w_pg, w_pp, final_g, loss_target, m_ln_g, m_w_in, m_conv_w, m_conv_b, m_cnorm_g, m_cnorm_b, m_w_pw2, m_b_pw2, m_lb_logits, m_onorm_g, m_w_out, m_pe_norm_g, m_w_pg, m_w_pp, m_final_g, v_ln_g, v_w_in, v_conv_w, v_conv_b, v_cnorm_g, v_cnorm_b, v_w_pw2, v_b_pw2, v_lb_logits, v_onorm_g, v_w_out, v_pe_norm_g, v_w_pg, v_w_pp, v_final_g):
    given = dict(locals())
    x2, p2, tgt = x[0], p[0, 0], loss_target[0]
    T = x2.shape[0]
    fin_g = final_g.reshape(1, D)

    conv_w_pad = jnp.pad(conv_w[0], ((0, HALO - CONV_K), (0, 0)))
    u, (w_in_f,) = _rmsnorm_gather(x2, ln_g, [w_in[0].astype(_MXU)], [1], _tile(T, "rmsnorm"))

    z, (w_pw2_f, w_out_f, w_pg_f, w_pp_f, conv_w_f) = _inproj_fwd(
        u, w_in_f,
        [w_pw2[0].astype(_MXU), w_out[0].astype(_MXU), w_pg[0].astype(_MXU), w_pp[0].astype(_MXU), conv_w_pad],
        [0, 0, 0, 1, 1], _tile(T, "inproj_fwd"))
    yc, y1 = _conv_fwd(z, conv_w_f, conv_b, cnorm_g, cnorm_b, w_pw2_f, b_pw2, _tile(T, "conv"))
    o_raw, yh, s_chunks = _hgrn_fwd(z, lb_logits, onorm_g, _tile(T, "hgrn"), HB)
    dyc, dyh, dh, n2, ds, dpe, dhb, pb, vec_tail = _tail(
        x2, yc, yh, p2, tgt, w_out_f, w_pg_f, w_pp_f, pe_norm_g, fin_g, _tile(T, "tail"))
    tk = _tile(T, "weight_grad")
    g_w_out_c = _tn_matmul(yc, dhb, tk, "grad_w_out_conv")
    g_w_out_h = _tn_matmul(yh, dhb, tk, "grad_w_out_hgrn")
    g_w_pg = _tn_matmul(n2, ds, tk, "grad_w_pg")
    g_w_pp = _tn_matmul(pb, dpe, tk, "grad_w_pp")
    dzc, a_act, dy2, vec_conv, g_conv_w = _conv_bwd(z, y1, dyc, conv_w_f, cnorm_g, cnorm_b, w_pw2_f, b_pw2, _tile(T, "conv"))
    g_w_pw2 = _tn_matmul(a_act, dy2, tk, "grad_w_pw2")

    rest = ["w_pw2", "w_out", "w_pg", "w_pp"]
    partial_rest = _pair_exchange_sum([
        _Slab([g_w_pw2], lambda s: (0, s * (D // N_CHIPS), 0), (D // N_CHIPS, D)),
        _Slab([g_w_out_c, g_w_out_h], lambda s: (s // 2, (s % 2) * (D // 2), 0), (D // 2, D)),
        _Slab([g_w_pg], lambda s: (0, s * (D // N_CHIPS), 0), (D // N_CHIPS, D)),
        _Slab([g_w_pp], lambda s: (0, 0, s * (D // N_CHIPS)), (PLE, D // N_CHIPS)),
    ], "grad_pair_exchange_rest")
    dzh, vec_hgrn, slots_rest = _hgrn_bwd(z, lb_logits, onorm_g, o_raw, dyh, s_chunks, partial_rest, _tile(T, "hgrn"), HB)
    g_w_in = _inproj_bwd_w(u, dzc, dzh, tk)
    partial_in = _pair_exchange_sum([
        _Slab([g_w_in], lambda s: (0, 0, s * (NPART * D // N_CHIPS)), (D, NPART * D // N_CHIPS))], "grad_pair_exchange_w_in")
    du, slots_in = _inproj_bwd_u(dzc, dzh, w_in_f, partial_in, _tile(T, "inproj_bwd_u"))
    grad_x, vec_in = _inproj_bwd_x(x2, ln_g, du, dh, _tile(T, "inproj_bwd_x"))
    big = ["w_in"] + rest
    vec = jnp.concatenate([vec_tail, vec_conv, vec_in, vec_hgrn, g_conv_w], axis=0)
    grads_big, vec_slots = _pair_share(list(slots_in) + list(slots_rest), vec)
    vsum = _sum_slots(vec_slots, "vec_sum")

    out = {}
    for nm, g in zip(big, grads_big):
        w2, m2, v2 = given[nm][0], given["m_" + nm][0], given["v_" + nm][0]
        d, nm_, nv_ = _adamw(g, w2, m2, v2, "adamw_" + nm)
        out[nm] = tuple(t[None] for t in (g, d, nm_, nv_))
    chip = 2 * lax.axis_index("x") + lax.axis_index("y")
    gcw = lax.dynamic_slice(vsum, (ROW_CONV_W, chip * (D // N_CHIPS)), (CONV_K, D // N_CHIPS))
    params = {nm: (given[nm].reshape(-1, D), given["m_" + nm].reshape(-1, D), given["v_" + nm].reshape(-1, D))
              for nm in SMALL + ["lb_logits"]}
    params["conv_w"] = (conv_w[0], m_conv_w[0], v_conv_w[0])
    small = _adamw_small(vsum, gcw, lb_logits, params)
    for nm, ts in small.items():
        out[nm] = tuple(t.reshape(given[nm].shape) for t in ts)

    loss = vsum[ROW_LOSS, 0]
    order = ["ln_g", "w_in", "conv_w", "conv_b", "cnorm_g", "cnorm_b", "w_pw2", "b_pw2", "lb_logits", "onorm_g",
             "w_out", "pe_norm_g", "w_pg", "w_pp", "final_g"]
    return (loss, grad_x[None], *[out[nm][0] for nm in order], *[out[nm][1] for nm in order],
            *[out[nm][2] for nm in order], *[out[nm][3] for nm in order])
```

```python
import functools

import jax
import jax.numpy as jnp
from jax import lax
from jax.experimental import pallas as pl
from jax.experimental.pallas import tpu as pltpu

F32 = jnp.float32
BF16 = jnp.bfloat16
_MXU = jnp.bfloat16
_WIRE = jnp.bfloat16

D = 1024
NPART = 7
PLE = 256
HEADS = 8
HD = 128
CHUNK = 64
CONV_K = 31
HALO = 32
EPS = 1e-6
N_CHIPS = 4
N_DEV = 8
HB = 8
VEC_ROWS = 64

ADAM_LR = 0.001
ADAM_B1 = 0.9
ADAM_B2 = 0.999
ADAM_EPS = 1e-08
ADAM_WD = 0.01
ADAM_STEP = 10

V7X_VMEM_LIMIT = 60000 * 1024
MESH_ID = pl.DeviceIdType.MESH
ANY = pl.BlockSpec(memory_space=pltpu.HBM)


def _in_hbm(arrays):
    return [pltpu.with_memory_space_constraint(a, pltpu.HBM) for a in arrays]


def _cparams(block_bytes, n_grid_dims):
    limit = min(V7X_VMEM_LIMIT, 2 * block_bytes + (24 << 20))
    return pltpu.CompilerParams(vmem_limit_bytes=int(limit), dimension_semantics=("arbitrary",) * n_grid_dims)


def _nbytes(shape, dtype):
    n = 1
    for s in shape:
        n *= s
    return n * jnp.dtype(dtype).itemsize


def _dot(a, b):
    return jnp.dot(a.astype(_MXU), b.astype(_MXU), preferred_element_type=F32)


def _dot_nt(a, b):
    return lax.dot_general(a.astype(_MXU), b.astype(_MXU), (((1,), (1,)), ((), ())), preferred_element_type=F32)


def _dot_tn(a, b):
    return lax.dot_general(a.astype(_MXU), b.astype(_MXU), (((0,), (0,)), ((), ())), preferred_element_type=F32)


def _tri_dot(tri_bf, x):
    x1 = x.astype(BF16)
    r1 = x - x1.astype(F32)
    x2 = r1.astype(BF16)
    x3 = (r1 - x2.astype(F32)).astype(BF16)
    d = lambda t: jnp.dot(tri_bf, t, preferred_element_type=F32)
    return d(x1) + d(x2) + d(x3)


def _split2(x):
    hi = x.astype(BF16)
    return hi, (x - hi.astype(F32)).astype(BF16)


def _dot3(dims, a, b):
    d = lambda p, q: lax.dot_general(p, q, (dims, ((), ())), preferred_element_type=F32)
    return d(a[0], b[0]) + d(a[0], b[1]) + d(a[1], b[0])


def _sigmoid(x):
    return jax.nn.sigmoid(x)


def _mean_lanes(x):
    return jnp.mean(x, axis=-1, keepdims=True)


def _sum_rows(x):
    return jnp.sum(x, axis=0, keepdims=True)


def _group_ln(y):
    yn, rs = [], []
    for g in range(D // HD):
        blk = y[:, g * HD:(g + 1) * HD]
        xc = blk - _mean_lanes(blk)
        r = lax.rsqrt(_mean_lanes(xc * xc) + EPS)
        yn.append(xc * r)
        rs.append(jnp.broadcast_to(r, blk.shape))
    return jnp.concatenate(yn, axis=1), jnp.concatenate(rs, axis=1)


def _group_ln_bwd(dyn, yn, rstd):
    out = []
    for g in range(D // HD):
        sl = slice(g * HD, (g + 1) * HD)
        d, n = dyn[:, sl], yn[:, sl]
        out.append(rstd[:, sl] * (d - _mean_lanes(d) - n * _mean_lanes(d * n)))
    return jnp.concatenate(out, axis=1)


def _head_means(x, hb, fn=lambda m: m):
    return jnp.concatenate([jnp.broadcast_to(fn(_mean_lanes(x[:, hh * HD:(hh + 1) * HD])), (x.shape[0], HD))
                            for hh in range(hb)], axis=1)


def _head_rsqrt_mean(x, hb):
    return _head_means(x, hb, lambda m: lax.rsqrt(m + EPS))


def _softmax_row0(lbl):
    m = jnp.max(lbl, axis=0, keepdims=True)
    e = jnp.exp(lbl - m)
    return e[0:1, :] / jnp.sum(e, axis=0, keepdims=True)


def _hosted_gather(phases, step, at, shards, axes, ins, outs, bufs, sems):
    gather = _Gather([s.shape for s in shards], axes, ins, outs, bufs, sems)
    for phase in phases:
        pl.when(step == at[phase])(getattr(gather, phase))


def _rmsnorm_gather(x, ln_g, shards, axes, tT):
    T = x.shape[0]
    n = len(shards)
    at = _hosted_steps(T // tT)

    def body(x_ref, g_ref, *rest):
        ins, u_ref, outs, bufs, sems = rest[:n], rest[n], rest[n + 1:2 * n + 1], rest[2 * n + 1:3 * n + 1], rest[3 * n + 1:]
        host = functools.partial(_hosted_gather, step=pl.program_id(0), at=at, shards=shards, axes=axes,
                                 ins=ins, outs=outs, bufs=bufs, sems=sems)
        host(("start", "turn", "forward"))
        xv = x_ref[...]
        r = lax.rsqrt(_mean_lanes(xv * xv) + EPS)
        u_ref[...] = (xv * r * g_ref[...]).astype(_MXU)
        host(("finish",))

    blk = _nbytes((tT, D), F32) * 2 + _nbytes((tT, D), _MXU) + sum(_nbytes(s.shape, s.dtype) for s in shards)
    outs = pl.pallas_call(
        body, name="rmsnorm_gather", grid=(T // tT,),
        in_specs=[pl.BlockSpec((tT, D), lambda i: (i, 0)), pl.BlockSpec((1, D), lambda i: (0, 0))] + [ANY] * n,
        out_specs=[pl.BlockSpec((tT, D), lambda i: (i, 0))] + [ANY] * n,
        out_shape=[jax.ShapeDtypeStruct((T, D), _MXU)]
        + [pltpu.HBM(fs, s.dtype) for fs, s in zip(_full_shapes(shards, axes), shards)],
        scratch_shapes=_Gather.scratch(shards),
        compiler_params=_cparams(blk, 1),
    )(x, ln_g, *_in_hbm(shards))
    return outs[0], outs[1:]


def _inproj_fwd(u, w_in, shards, axes, tT):
    T = u.shape[0]
    n = len(shards)
    at = _hosted_steps((T // tT) * NPART)

    def body(u_ref, w_ref, *rest):
        ins, z_ref, outs, bufs, sems = rest[:n], rest[n], rest[n + 1:2 * n + 1], rest[2 * n + 1:3 * n + 1], rest[3 * n + 1:]
        host = functools.partial(_hosted_gather, step=pl.program_id(0) * NPART + pl.program_id(1), at=at, shards=shards,
                                 axes=axes, ins=ins, outs=outs, bufs=bufs, sems=sems)
        host(("start", "turn", "forward"))
        z_ref[...] = jnp.dot(u_ref[...], w_ref[...], preferred_element_type=F32)
        host(("finish",))

    blk = (_nbytes((tT, D), F32) + _nbytes((D, D), _MXU) + _nbytes((tT, D), _MXU)
           + sum(_nbytes(s.shape, s.dtype) for s in shards))
    outs = pl.pallas_call(
        body, name="inproj_fwd", grid=(T // tT, NPART),
        in_specs=[pl.BlockSpec((tT, D), lambda i, j: (i, 0)), pl.BlockSpec((D, D), lambda i, j: (0, j))] + [ANY] * n,
        out_specs=[pl.BlockSpec((tT, D), lambda i, j: (i, j))] + [ANY] * n,
        out_shape=[jax.ShapeDtypeStruct((T, NPART * D), F32)]
        + [pltpu.HBM(fs, s.dtype) for fs, s in zip(_full_shapes(shards, axes), shards)],
        scratch_shapes=_Gather.scratch(shards),
        compiler_params=_cparams(blk, 2),
    )(u, w_in, *_in_hbm(shards))
    return outs[0], outs[1:]


def _shifted_windows(ext, first, visit):
    n = ext.shape[0]
    for m in range(first, first + CONV_K):
        visit(m, (ext if m == 0 else pltpu.roll(ext, n - m, axis=0))[0:n - HALO, :])


def _conv_fwd(z, conv_w, conv_b, cn_g, cn_b, w_pw2, b_pw2, tT):
    T = z.shape[0]

    def body(cv_ref, cg_ref, ct_ref, cw_ref, cb_ref, ng_ref, nb_ref, wp_ref, bp_ref, yc_ref, y1_ref, ext):
        @pl.when(pl.program_id(0) == 0)
        def _():
            ext[...] = jnp.zeros_like(ext)
        ext[0:HALO, :] = ext[tT:tT + HALO, :]
        ext[HALO:, :] = cv_ref[...] * _sigmoid(cg_ref[...])
        cw = cw_ref[...]
        acc = [cb_ref[...]]

        def tap(m, win):
            acc[0] = acc[0] + win * cw[m - 2:m - 1, :]
        _shifted_windows(ext[...], 2, tap)
        y1 = acc[0]
        y1_ref[...] = y1
        yn, _ = _group_ln(y1)
        apre = yn * ng_ref[...] + nb_ref[...]
        a = apre * _sigmoid(apre)
        y2 = _dot(a, wp_ref[...]) + bp_ref[...]
        ct = ct_ref[...]
        yc_ref[...] = (y2 * (ct * _sigmoid(ct))).astype(_MXU)

    part = lambda p: pl.BlockSpec((tT, D), lambda i: (i, p))
    row = pl.BlockSpec((1, D), lambda i: (0, 0))
    tok = pl.BlockSpec((tT, D), lambda i: (i, 0))
    blk = 4 * _nbytes((tT, D), F32) + _nbytes((D, D), _MXU) + _nbytes((tT, D), _MXU) + 8 * _nbytes((tT + HALO, D), F32)
    return pl.pallas_call(
        body, name="conv_fwd", grid=(T // tT,),
        in_specs=[part(0), part(1), part(2), pl.BlockSpec((HALO, D), lambda i: (0, 0)), row, row, row,
                  pl.BlockSpec((D, D), lambda i: (0, 0)), row],
        out_specs=[tok, tok],
        out_shape=[jax.ShapeDtypeStruct((T, D), _MXU), jax.ShapeDtypeStruct((T, D), F32)],
        scratch_shapes=[pltpu.VMEM((tT + HALO, D), F32)],
        compiler_params=_cparams(blk, 1),
    )(z, z, z, conv_w, conv_b, cn_g, cn_b, w_pw2, b_pw2)


def _hgrn_gates(lb, hq, hf):
    sq = _sigmoid(hq)
    sg = _sigmoid(hf)
    f = lb + (1.0 - lb) * sg
    return sq, sg, f, hq * sq, (1.0 - lb) * (1.0 - sg), jnp.log(f)


def _chunk_decays(lf, q, k):
    r = lax.broadcasted_iota(jnp.int32, (CHUNK, CHUNK), 0)
    c = lax.broadcasted_iota(jnp.int32, (CHUNK, CHUNK), 1)
    b = _tri_dot((r >= c).astype(BF16), lf)
    bm = b[CHUNK // 2 - 1:CHUNK // 2, :]
    bl = b[CHUNK - 1:CHUNK, :]
    eb = jnp.exp(b)
    eqm = jnp.exp(b - bm)
    ekm = jnp.exp(bm - b)
    ekd = jnp.exp(bl - b)
    return dict(causal=r >= c, eb=eb, eqm=eqm, ekm=ekm, ekd=ekd, ebl=jnp.exp(bl),
                qd=q * eb, qm=q * eqm, km=k * ekm, kd=k * ekd)


def _hgrn_fwd(z, lb_logits, onorm_g, tT, hb):
    T = z.shape[0]
    nc = tT // CHUNK
    w = hb * HD

    def body(lbl_ref, og_ref, hq_ref, hf_ref, hi_ref, hg_ref, o_ref, yh_ref, sc_ref, st):
        @pl.when(pl.program_id(1) == 0)
        def _():
            st[...] = jnp.zeros_like(st)
        lb_all = _softmax_row0(lbl_ref[...])
        og_all = og_ref[...]

        def chunk(c, carry):
            sl = pl.ds(pl.multiple_of(c * CHUNK, CHUNK), CHUNK)
            lanes = [slice(hh * HD, (hh + 1) * HD) for hh in range(hb)]
            heads = lambda fn: [fn(hh, ln) for hh, ln in enumerate(lanes)]
            hg, v = hg_ref[sl, :], hi_ref[sl, :]
            _, _, _, q, k, lf = _hgrn_gates(lb_all, hq_ref[sl, :], hf_ref[sl, :])
            dc = _chunk_decays(lf, q, k)
            s_t = heads(lambda hh, ln: st[hh])
            a = heads(lambda hh, ln: jnp.where(dc["causal"], _dot_nt(dc["qm"][:, ln], dc["km"][:, ln]), 0.0))
            o_inter = heads(lambda hh, ln: _dot_nt(dc["qd"][:, ln], s_t[hh]))
            kv = heads(lambda hh, ln: _dot_tn(v[:, ln], dc["kd"][:, ln]))
            o_intra = heads(lambda hh, ln: _dot(a[hh], v[:, ln]))
            for hh, ln in enumerate(lanes):
                sc_ref[hh, c] = s_t[hh]
                st[hh] = s_t[hh] * dc["ebl"][:, ln] + kv[hh]
            o = jnp.concatenate([o_inter[hh] + o_intra[hh] for hh in range(hb)], axis=1)
            o_ref[sl, :] = o
            n = o * _head_rsqrt_mean(o * o, hb)
            yh_ref[sl, :] = ((n * og_all) * (hg * _sigmoid(hg))).astype(_MXU)
            return carry

        lax.fori_loop(0, nc, chunk, 0, unroll=8)

    zpart = lambda p: pl.BlockSpec((tT, w), lambda h, i: (i, p * (HEADS // hb) + h))
    blk = 6 * _nbytes((tT, w), F32) + _nbytes((hb, nc, HD, HD), F32)
    return pl.pallas_call(
        body, name="hgrn_fwd", grid=(HEADS // hb, T // tT),
        in_specs=[pl.BlockSpec((2, w), lambda h, i: (0, h)), pl.BlockSpec((1, w), lambda h, i: (0, h)),
                  zpart(3), zpart(4), zpart(5), zpart(6)],
        out_specs=[pl.BlockSpec((tT, w), lambda h, i: (i, h)), pl.BlockSpec((tT, w), lambda h, i: (i, h)),
                   pl.BlockSpec((hb, nc, HD, HD), lambda h, i: (h, i, 0, 0))],
        out_shape=[jax.ShapeDtypeStruct((T, D), F32), jax.ShapeDtypeStruct((T, D), _MXU),
                   jax.ShapeDtypeStruct((HEADS, T // CHUNK, HD, HD), F32)],
        scratch_shapes=[pltpu.VMEM((hb, HD, HD), F32)],
        compiler_params=_cparams(blk, 2),
    )(lb_logits, onorm_g, z, z, z, z)


def _hgrn_bwd(z, lb_logits, onorm_g, o_raw, dyh, s_chunks, partials, tT, hb):
    T = z.shape[0]
    nc = tT // CHUNK
    nI = T // tT
    w = hb * HD
    n = len(partials)
    at = _hosted_steps((HEADS // hb) * nI)

    def body(lbl_ref, og_ref, hq_ref, hf_ref, hi_ref, hg_ref, o_ref, dy_ref, sc_ref, *rest):
        (dz_ref, vec_ref), dst = rest[n:n + 2], rest[2 * n + 2]
        exchange = _ChipExchange(n, rest[:n], rest[n + 2:2 * n + 2], rest[2 * n + 3:3 * n + 3], rest[3 * n + 3:])
        step = pl.program_id(0) * nI + pl.program_id(1)
        pl.when(step == at["start"])(exchange.start)
        pl.when(step == at["turn"])(exchange.turn)

        @pl.when(pl.program_id(1) == 0)
        def _():
            dst[...] = jnp.zeros_like(dst)
            vec_ref[...] = jnp.zeros_like(vec_ref)
        lb_all = _softmax_row0(lbl_ref[...])
        og_all = og_ref[...]
        last_row = lax.broadcasted_iota(jnp.int32, (CHUNK, w), 0) == CHUNK - 1
        r64 = lax.broadcasted_iota(jnp.int32, (CHUNK, CHUNK), 0)
        c64 = lax.broadcasted_iota(jnp.int32, (CHUNK, CHUNK), 1)
        upper = (c64 >= r64).astype(BF16)
        lanes = [slice(hh * HD, (hh + 1) * HD) for hh in range(hb)]
        heads = lambda fn: [fn(hh, ln) for hh, ln in enumerate(lanes)]
        wide = lambda parts: jnp.concatenate(parts, axis=1)

        def chunk(cc, carry):
            c = nc - 1 - cc
            sl = pl.ds(pl.multiple_of(c * CHUNK, CHUNK), CHUNK)
            hq, hg, v = hq_ref[sl, :], hg_ref[sl, :], hi_ref[sl, :]
            sq, sg, f, q, k, lf = _hgrn_gates(lb_all, hq, hf_ref[sl, :])
            dc = _chunk_decays(lf, q, k)
            s_t = heads(lambda hh, ln: sc_ref[hh, c])
            ds_t = heads(lambda hh, ln: dst[hh])
            o, dy = o_ref[sl, :], dy_ref[sl, :]
            r = _head_rsqrt_mean(o * o, hb)
            n = o * r
            sgg = _sigmoid(hg)
            silu_g = hg * sgg
            dhg = dy * (n * og_all) * (sgg * (1.0 + hg * (1.0 - sgg)))
            dn = dy * og_all * silu_g
            g_og = _sum_rows(dy * n * silu_g)
            do = r * (dn - n * _head_means(dn * n, hb))
            a = heads(lambda hh, ln: jnp.where(dc["causal"], _dot_nt(dc["qm"][:, ln], dc["km"][:, ln]), 0.0))
            dam = heads(lambda hh, ln: jnp.where(dc["causal"], _dot_nt(do[:, ln], v[:, ln]), 0.0))
            dqd = wide(heads(lambda hh, ln: _dot(do[:, ln], s_t[hh])))
            dkd = wide(heads(lambda hh, ln: _dot(v[:, ln], ds_t[hh])))
            dv_inter = heads(lambda hh, ln: _dot_nt(dc["kd"][:, ln], ds_t[hh]))
            dqs = heads(lambda hh, ln: _dot_tn(do[:, ln], dc["qd"][:, ln]))
            dv = wide(heads(lambda hh, ln: _dot_tn(a[hh], do[:, ln]) + dv_inter[hh]))
            dam2 = [_split2(t) for t in dam]
            km2, qm2 = _split2(dc["km"]), _split2(dc["qm"])
            dqm = wide(heads(lambda hh, ln: _dot3(((1,), (0,)), dam2[hh], (km2[0][:, ln], km2[1][:, ln]))))
            dkm = wide(heads(lambda hh, ln: _dot3(((0,), (0,)), dam2[hh], (qm2[0][:, ln], qm2[1][:, ln]))))
            debl = wide(heads(lambda hh, ln: _sum_rows(ds_t[hh] * s_t[hh])))
            for hh, ln in enumerate(lanes):
                dst[hh] = ds_t[hh] * dc["ebl"][:, ln] + dqs[hh]
            dq = dqd * dc["eb"] + dqm * dc["eqm"]
            dk = dkm * dc["ekm"] + dkd * dc["ekd"]
            dbl = _sum_rows(dkd * dc["kd"]) + debl * dc["ebl"]
            db = dq * q - dk * k + jnp.where(last_row, dbl, 0.0)
            dlf = _tri_dot(upper, db)
            dfk = dlf / f - dk
            dz_ref[0, sl, :] = (dq * (sq * (1.0 + hq * (1.0 - sq)))).astype(_MXU)
            dz_ref[1, sl, :] = (dfk * ((1.0 - lb_all) * sg * (1.0 - sg))).astype(_MXU)
            dz_ref[2, sl, :] = dv.astype(_MXU)
            dz_ref[3, sl, :] = dhg.astype(_MXU)
            vec_ref[0:1, :] += g_og
            vec_ref[1:2, :] += _sum_rows(dfk * (1.0 - sg))
            return carry

        lax.fori_loop(0, nc, chunk, 0, unroll=8)
        pl.when(step == at["finish"])(exchange.finish)

    zpart = lambda p: pl.BlockSpec((tT, w), lambda h, i: (nI - 1 - i, p * (HEADS // hb) + h))
    act = pl.BlockSpec((tT, w), lambda h, i: (nI - 1 - i, h))
    blk = (6 * _nbytes((tT, w), F32) + _nbytes((hb, nc, HD, HD), F32) + 4 * _nbytes((tT, w), _MXU)
           + _ChipExchange.scratch_bytes(partials))
    outs = pl.pallas_call(
        body, name="hgrn_bwd", grid=(HEADS // hb, nI),
        in_specs=[pl.BlockSpec((2, w), lambda h, i: (0, h)), pl.BlockSpec((1, w), lambda h, i: (0, h)),
                  zpart(3), zpart(4), zpart(5), zpart(6), act, act,
                  pl.BlockSpec((hb, nc, HD, HD), lambda h, i: (h, nI - 1 - i, 0, 0))] + [ANY] * n,
        out_specs=[pl.BlockSpec((4, tT, w), lambda h, i: (0, nI - 1 - i, h)),
                   pl.BlockSpec((8, w), lambda h, i: (0, h))] + [ANY] * n,
        out_shape=[jax.ShapeDtypeStruct((4, T, D), _MXU), jax.ShapeDtypeStruct((8, D), F32)]
        + [pltpu.HBM(p.shape, p.dtype) for p in partials],
        scratch_shapes=[pltpu.VMEM((hb, HD, HD), F32)] + _ChipExchange.scratch(partials),
        compiler_params=_cparams(blk, 2),
    )(lb_logits, onorm_g, z, z, z, z, o_raw, dyh, s_chunks, *_in_hbm(partials))
    return outs[0], outs[1], outs[2:]


def _tail(x, yc, yh, p, target, w_out, w_pg, w_pp, pe_g, fin_g, tT):
    T = x.shape[0]

    def body(x_ref, yc_ref, yh_ref, p_ref, t_ref, wo_ref, wg_ref, wp_ref, pg_ref, fg_ref,
             dyc_ref, dyh_ref, n2_ref, ds_ref, dpe_ref, dhb_ref, pb_ref, vec_ref):
        @pl.when(pl.program_id(0) == 0)
        def _():
            vec_ref[...] = jnp.zeros_like(vec_ref)
        wo_c, wo_h = wo_ref[0:D, :], wo_ref[D:2 * D, :]
        h = x_ref[...] + _dot(yc_ref[...], wo_c) + _dot(yh_ref[...], wo_h)
        pb = p_ref[...].astype(_MXU)
        pe = _dot(pb, wp_ref[...])
        r2 = lax.rsqrt(_mean_lanes(h * h) + EPS)
        hn = h * r2
        n2 = (hn * pg_ref[...]).astype(_MXU)
        gate = _sigmoid(_dot(n2, wg_ref[...]))
        h2 = h + gate * pe
        r3 = lax.rsqrt(_mean_lanes(h2 * h2) + EPS)
        h2n = h2 * r3
        err = h2n * fg_ref[...] - t_ref[...]
        vec_ref[ROW_LOSS:ROW_LOSS + 1, :] += 0.5 * jnp.sum(_mean_lanes(err * err))
        dout = err * (1.0 / D)
        vec_ref[0:1, :] += _sum_rows(dout * h2n)
        dn3 = dout * fg_ref[...]
        dh2 = r3 * (dn3 - h2n * _mean_lanes(dn3 * h2n))
        ds = (dh2 * pe * gate * (1.0 - gate)).astype(_MXU)
        dn2 = _dot_nt(ds, wg_ref[...])
        vec_ref[1:2, :] += _sum_rows(dn2 * hn)
        dnn = dn2 * pg_ref[...]
        dh = dh2 + r2 * (dnn - hn * _mean_lanes(dnn * hn))
        dhb = dh.astype(_MXU)
        dyc_ref[...] = _dot_nt(dhb, wo_c)
        dyh_ref[...] = _dot_nt(dhb, wo_h)
        n2_ref[...] = n2
        ds_ref[...] = ds
        dpe_ref[...] = (dh2 * gate).astype(_MXU)
        dhb_ref[...] = dhb
        pb_ref[...] = pb

    tok = lambda w: pl.BlockSpec((tT, w), lambda i: (i, 0))
    full = lambda r, c: pl.BlockSpec((r, c), lambda i: (0, 0))
    tokshape = lambda w, dt: jax.ShapeDtypeStruct((T, w), dt)
    blk = (5 * _nbytes((tT, D), F32) + 7 * _nbytes((tT, D), _MXU) + _nbytes((4 * D + PLE, D), _MXU)
           + 12 * _nbytes((tT, D), F32))
    return pl.pallas_call(
        body, name="tail_fwd_bwd", grid=(T // tT,),
        in_specs=[tok(D), tok(D), tok(D), tok(PLE), tok(D), full(2 * D, D), full(D, D), full(PLE, D), full(1, D), full(1, D)],
        out_specs=[tok(D), tok(D), tok(D), tok(D), tok(D), tok(D), tok(PLE), full(8, D)],
        out_shape=[tokshape(D, F32), tokshape(D, F32), tokshape(D, _MXU), tokshape(D, _MXU),
                   tokshape(D, _MXU), tokshape(D, _MXU), tokshape(PLE, _MXU),
                   jax.ShapeDtypeStruct((8, D), F32)],
        compiler_params=_cparams(blk, 1),
    )(x, yc, yh, p, target, w_out, w_pg, w_pp, pe_g, fin_g)


def _conv_bwd(z, y1, dyc, conv_w, cn_g, cn_b, w_pw2, b_pw2, tT):
    T = z.shape[0]
    nI = T // tT
    hb = tT // HALO

    def body(cv_ref, cg_ref, ct_ref, hv_ref, hg_ref, y1_ref, dyc_ref, cw_ref, ng_ref, nb_ref, wp_ref, bp_ref,
             dz_ref, a_ref, dy2_ref, vec_ref, gcw_ref, ext, ext2, gpart):
        i = pl.program_id(0)

        @pl.when(i == 0)
        def _():
            ext2[...] = jnp.zeros_like(ext2)
            gpart[...] = jnp.zeros_like(gpart)
            vec_ref[...] = jnp.zeros_like(vec_ref)
        cv, cg, ct = cv_ref[...], cg_ref[...], ct_ref[...]
        sg = _sigmoid(cg)
        has_hist = (i < nI - 1).astype(F32)
        ext[0:HALO, :] = hv_ref[...] * _sigmoid(hg_ref[...]) * has_hist
        ext[HALO:, :] = cv * sg
        yn, rstd = _group_ln(y1_ref[...])
        apre = yn * ng_ref[...] + nb_ref[...]
        sa = _sigmoid(apre)
        a = (apre * sa).astype(_MXU)
        y2 = _dot(a, wp_ref[...]) + bp_ref[...]
        st = _sigmoid(ct)
        dyc_v = dyc_ref[...]
        dy2 = dyc_v * (ct * st)
        dy2b = dy2.astype(_MXU)
        da = _dot_nt(dy2b, wp_ref[...])
        dapre = da * (sa * (1.0 + apre * (1.0 - sa)))
        dy1 = _group_ln_bwd(dapre * ng_ref[...], yn, rstd)
        vec_ref[0:1, :] += _sum_rows(dy1)
        vec_ref[1:2, :] += _sum_rows(dapre * yn)
        vec_ref[2:3, :] += _sum_rows(dapre)
        vec_ref[3:4, :] += _sum_rows(dy2)
        dz_ref[2] = (dyc_v * y2 * (st * (1.0 + ct * (1.0 - st)))).astype(_MXU)
        a_ref[...] = a
        dy2_ref[...] = dy2b
        ext2[tT:tT + HALO, :] = ext2[0:HALO, :]
        ext2[0:tT, :] = dy1
        def grad_tap(m, win):
            p = dy1 * win
            part = p[0:8, :]
            for q in range(1, tT // 8):
                part = part + p[8 * q:8 * q + 8, :]
            gpart[m - 2] += part
        _shifted_windows(ext[...], 2, grad_tap)
        cw = cw_ref[...]
        acc = [None]

        def dv_tap(m, win):
            term = win * cw[CONV_K - 1 - m:CONV_K - m, :]
            acc[0] = term if acc[0] is None else acc[0] + term
        _shifted_windows(ext2[...], 0, dv_tap)
        dv = acc[0]
        dz_ref[0] = (dv * sg).astype(_MXU)
        dz_ref[1] = (dv * cv * sg * (1.0 - sg)).astype(_MXU)

        @pl.when(i == nI - 1)
        def _():
            gcw_ref[...] = jnp.sum(gpart[...], axis=1)

    part = lambda p: pl.BlockSpec((tT, D), lambda i: (nI - 1 - i, p))
    hist = lambda p: pl.BlockSpec((HALO, D), lambda i: (jnp.maximum((nI - 1 - i) * hb - 1, 0), p))
    tok = pl.BlockSpec((tT, D), lambda i: (nI - 1 - i, 0))
    row = pl.BlockSpec((1, D), lambda i: (0, 0))
    blk = (5 * _nbytes((tT, D), F32) + _nbytes((D, D), _MXU) + 5 * _nbytes((tT, D), _MXU)
           + 10 * _nbytes((tT + HALO, D), F32))
    return pl.pallas_call(
        body, name="conv_bwd", grid=(nI,),
        in_specs=[part(0), part(1), part(2), hist(0), hist(1), tok, tok, pl.BlockSpec((HALO, D), lambda i: (0, 0)),
                  row, row, pl.BlockSpec((D, D), lambda i: (0, 0)), row],
        out_specs=[pl.BlockSpec((3, tT, D), lambda i: (0, nI - 1 - i, 0)), tok, tok,
                   pl.BlockSpec((8, D), lambda i: (0, 0)), pl.BlockSpec((HALO, D), lambda i: (0, 0))],
        out_shape=[jax.ShapeDtypeStruct((3, T, D), _MXU), jax.ShapeDtypeStruct((T, D), _MXU),
                   jax.ShapeDtypeStruct((T, D), _MXU), jax.ShapeDtypeStruct((8, D), F32),
                   jax.ShapeDtypeStruct((HALO, D), F32)],
        scratch_shapes=[pltpu.VMEM((tT + HALO, D), F32), pltpu.VMEM((tT + HALO, D), F32), pltpu.VMEM((HALO, 8, D), F32)],
        compiler_params=_cparams(blk, 1),
    )(z, z, z, z, z, y1, dyc, conv_w, cn_g, cn_b, w_pw2, b_pw2)


def _inproj_bwd_u(dzc, dzh, w_in, partials, tT):
    T = dzc.shape[1]
    n = len(partials)
    at = _hosted_steps((T // tT) * NPART)

    def body(dzc_ref, dzh_ref, w_ref, *rest):
        du_ref, acc = rest[n], rest[2 * n + 1]
        exchange = _ChipExchange(n, rest[:n], rest[n + 1:2 * n + 1], rest[2 * n + 2:3 * n + 2], rest[3 * n + 2:])
        j = pl.program_id(1)
        step = pl.program_id(0) * NPART + j
        pl.when(step == at["start"])(exchange.start)
        pl.when(step == at["turn"])(exchange.turn)

        @pl.when(j == 0)
        def _():
            acc[...] = jnp.zeros_like(acc)

        @pl.when(j < 3)
        def _():
            acc[...] += _dot_nt(dzc_ref[0], w_ref[...])

        @pl.when(j >= 3)
        def _():
            acc[...] += _dot_nt(dzh_ref[0], w_ref[...])

        @pl.when(j == NPART - 1)
        def _():
            du_ref[...] = acc[...].astype(du_ref.dtype)
        pl.when(step == at["finish"])(exchange.finish)

    blk = (3 * _nbytes((tT, D), _MXU) + _nbytes((D, D), _MXU) + _nbytes((tT, D), F32)
           + _ChipExchange.scratch_bytes(partials))
    outs = pl.pallas_call(
        body, name="inproj_bwd_u", grid=(T // tT, NPART),
        in_specs=[pl.BlockSpec((1, tT, D), lambda i, j: (jnp.minimum(j, 2), i, 0)),
                  pl.BlockSpec((1, tT, D), lambda i, j: (jnp.maximum(j - 3, 0), i, 0)),
                  pl.BlockSpec((D, D), lambda i, j: (0, j))] + [ANY] * n,
        out_specs=[pl.BlockSpec((tT, D), lambda i, j: (i, 0))] + [ANY] * n,
        out_shape=[jax.ShapeDtypeStruct((T, D), _MXU)] + [pltpu.HBM(p.shape, p.dtype) for p in partials],
        scratch_shapes=[pltpu.VMEM((tT, D), F32)] + _ChipExchange.scratch(partials),
        compiler_params=_cparams(blk, 2),
    )(dzc, dzh, w_in, *_in_hbm(partials))
    return outs[0], outs[1:]


def _inproj_bwd_x(x, ln_g, du, dh, tT):
    T = x.shape[0]

    def body(x_ref, g_ref, du_ref, dh_ref, gx_ref, vec_ref):
        @pl.when(pl.program_id(0) == 0)
        def _():
            vec_ref[...] = jnp.zeros_like(vec_ref)
        xv = x_ref[...]
        r = lax.rsqrt(_mean_lanes(xv * xv) + EPS)
        xn = xv * r
        duv = du_ref[...].astype(F32)
        vec_ref[0:1, :] += _sum_rows(duv * xn)
        dun = duv * g_ref[...]
        gx_ref[...] = dh_ref[...].astype(F32) + r * (dun - xn * _mean_lanes(dun * xn))

    tok = pl.BlockSpec((tT, D), lambda i: (i, 0))
    return pl.pallas_call(
        body, name="inproj_bwd_x", grid=(T // tT,),
        in_specs=[tok, pl.BlockSpec((1, D), lambda i: (0, 0)), tok, tok],
        out_specs=[tok, pl.BlockSpec((8, D), lambda i: (0, 0))],
        out_shape=[jax.ShapeDtypeStruct((T, D), F32), jax.ShapeDtypeStruct((8, D), F32)],
        compiler_params=_cparams(6 * _nbytes((tT, D), F32), 1),
    )(x, ln_g, du, dh)


def _inproj_bwd_w(u, dzc, dzh, tk):
    T = u.shape[0]
    nK = T // tk

    def body(u_ref, dzc_ref, dzh_ref, gw_ref):
        j, k = pl.program_id(0), pl.program_id(1)

        @pl.when(k == 0)
        def _():
            gw_ref[...] = jnp.zeros_like(gw_ref)

        @pl.when(j < 3)
        def _():
            gw_ref[...] += _dot_tn(u_ref[...], dzc_ref[0])

        @pl.when(j >= 3)
        def _():
            gw_ref[...] += _dot_tn(u_ref[...], dzh_ref[0])

    blk = 3 * _nbytes((tk, D), _MXU) + 2 * _nbytes((D, D), F32)
    return pl.pallas_call(
        body, name="inproj_bwd_w", grid=(NPART, nK),
        in_specs=[pl.BlockSpec((tk, D), lambda j, k: (k, 0)),
                  pl.BlockSpec((1, tk, D), lambda j, k: (jnp.minimum(j, 2), jnp.where(j < 3, k, nK - 1), 0)),
                  pl.BlockSpec((1, tk, D), lambda j, k: (jnp.maximum(j - 3, 0), jnp.where(j < 3, 0, k), 0))],
        out_specs=pl.BlockSpec((D, D), lambda j, k: (0, j)),
        out_shape=jax.ShapeDtypeStruct((D, NPART * D), F32),
        compiler_params=_cparams(blk, 2),
    )(u, dzc, dzh)


def _tn_matmul(a, b, tk, name):
    T, M = a.shape
    N = b.shape[1]

    def body(a_ref, b_ref, o_ref):
        @pl.when(pl.program_id(0) == 0)
        def _():
            o_ref[...] = jnp.zeros_like(o_ref)
        o_ref[...] += _dot_tn(a_ref[...], b_ref[...])

    blk = _nbytes((tk, M), _MXU) + _nbytes((tk, N), _MXU) + 2 * _nbytes((M, N), F32)
    return pl.pallas_call(
        body, name=name, grid=(T // tk,),
        in_specs=[pl.BlockSpec((tk, M), lambda k: (k, 0)), pl.BlockSpec((tk, N), lambda k: (k, 0))],
        out_specs=pl.BlockSpec((M, N), lambda k: (0, 0)),
        out_shape=pltpu.HBM((M, N), F32),
        compiler_params=_cparams(blk, 1),
    )(a, b)


def _place():
    return lax.axis_index("x"), lax.axis_index("y"), lax.axis_index("c")


def _flip(v, d):
    return 1 - v if d else v


CHIP_MOVES = [(1, 0), (0, 1), (1, 1)]
DEV_MOVES = [(dx, dy, dc) for dx in (0, 1) for dy in (0, 1) for dc in (0, 1)][1:]


def _shard_slice(ref, axis, size, s):
    start = pl.multiple_of(s * size, size)
    return ref.at[pl.ds(start, size), :] if axis == 0 else ref.at[:, pl.ds(start, size)]


class _Bounce:
    def __init__(self, src, buf, dst, sem_in, sem_out):
        self.load = pltpu.make_async_copy(src, buf, sem_in)
        self.store = pltpu.make_async_copy(buf, dst, sem_out)

    def start(self):
        self.load.start()

    def turn(self):
        self.load.wait()
        self.store.start()

    def wait(self):
        self.store.wait()


def _comm_params(scratch_bytes):
    return pltpu.CompilerParams(vmem_limit_bytes=int(min(V7X_VMEM_LIMIT, scratch_bytes + (8 << 20))))


class _Gather:
    def __init__(self, shapes, axes, ins, outs, bufs, sems):
        self.shapes, self.axes, self.ins, self.outs, self.bufs = shapes, axes, ins, outs, bufs
        self.ici_send, self.ici_recv, self.d2d_send, self.d2d_recv, self.in_sems, self.out_sems = sems
        self.x, self.y, self.c = _place()
        self.me = 2 * self.x + self.y
        self.pairs = [(k, j) for k in range(len(shapes)) for j in range(3)]

    @staticmethod
    def scratch(shards):
        n = len(shards)
        return ([pltpu.VMEM(s.shape, s.dtype) for s in shards]
                + [pltpu.SemaphoreType.DMA((3 * n,))] * 4 + [pltpu.SemaphoreType.DMA((n,))] * 2)

    def _own_half(self, k, hc):
        half = self.shapes[k][0] // 2
        return self.ins[k].at[pl.ds(pl.multiple_of(hc * half, 16), half), :]

    def _region(self, k, who, hc):
        rows, cols = self.shapes[k]
        half = rows // 2
        if self.axes[k] == 0:
            return self.outs[k].at[pl.ds(pl.multiple_of(who * rows + hc * half, 16), half), :]
        return self.outs[k].at[pl.ds(pl.multiple_of(hc * half, 16), half), pl.ds(pl.multiple_of(who * cols, HD), cols)]

    def _peer(self, j):
        return 2 * _flip(self.x, CHIP_MOVES[j][0]) + _flip(self.y, CHIP_MOVES[j][1])

    def _ici(self, k, j, who, hc):
        dx, dy = CHIP_MOVES[j]
        return pltpu.make_async_remote_copy(
            src_ref=self._own_half(k, hc), dst_ref=self._region(k, who, hc),
            send_sem=self.ici_send.at[3 * k + j], recv_sem=self.ici_recv.at[3 * k + j],
            device_id=(_flip(self.x, dx), _flip(self.y, dy), self.c), device_id_type=MESH_ID)

    def _d2d(self, k, j, who, hc):
        return pltpu.make_async_remote_copy(
            src_ref=self._region(k, who, hc), dst_ref=self._region(k, who, hc),
            send_sem=self.d2d_send.at[3 * k + j], recv_sem=self.d2d_recv.at[3 * k + j],
            device_id=(self.x, self.y, 1 - self.c), device_id_type=MESH_ID)

    def _local(self, k):
        size = self.shapes[k][self.axes[k]]
        return _Bounce(self.ins[k], self.bufs[k], _shard_slice(self.outs[k], self.axes[k], size, self.me),
                       self.in_sems.at[k], self.out_sems.at[k])

    def start(self):
        for k in range(len(self.shapes)):
            self._local(k).start()
        for k, j in self.pairs:
            self._ici(k, j, self.me, self.c).start()

    def turn(self):
        for k in range(len(self.shapes)):
            self._local(k).turn()

    def forward(self):
        for k, j in self.pairs:
            self._ici(k, j, self._peer(j), self.c).wait_recv()
            self._d2d(k, j, self._peer(j), self.c).start()

    def finish(self):
        for k, j in self.pairs:
            self._d2d(k, j, self._peer(j), 1 - self.c).wait_recv()
        for k, j in self.pairs:
            self._ici(k, j, self.me, self.c).wait_send()
            self._d2d(k, j, self._peer(j), self.c).wait_send()
        for k in range(len(self.shapes)):
            self._local(k).wait()


def _full_shapes(shards, axes):
    return [tuple(d * (N_CHIPS if a == ax else 1) for a, d in enumerate(s.shape)) for s, ax in zip(shards, axes)]


class _Slab:
    def __init__(self, arrays, pick, shard_shape):
        self.arrays = arrays
        self.pick = pick
        self.rows, self.cols = shard_shape
        self.half = self.rows // 2


PAIR_SUM_ROWS = 64


def _pair_exchange_sum(slabs, name):
    n = len(slabs)
    n_in = sum(len(sl.arrays) for sl in slabs)

    def body(*refs):
        ins, outs = refs[:n_in], refs[n_in:n_in + n]
        mine, got, total = (refs[n_in + (1 + t) * n:n_in + (2 + t) * n] for t in range(3))
        send_sems, recv_sems, in_sems, out_sems = refs[n_in + 4 * n:]
        x, y, c = _place()
        started = []
        base = 0
        for k, sl in enumerate(slabs):
            for s in range(N_CHIPS):
                ai, r0, c0 = sl.pick(s)
                src = ins[base + ai]

                def half(hc):
                    return src.at[pl.ds(pl.multiple_of(r0 + hc * sl.half, 8), sl.half), pl.ds(c0, sl.cols)]
                q = N_CHIPS * k + s
                load = pltpu.make_async_copy(half(c), mine[k].at[s], in_sems.at[q])
                load.start()
                cp = pltpu.make_async_remote_copy(
                    src_ref=half(1 - c), dst_ref=got[k].at[s], send_sem=send_sems.at[q], recv_sem=recv_sems.at[q],
                    device_id=(x, y, 1 - c), device_id_type=MESH_ID)
                cp.start()
                store = pltpu.make_async_copy(total[k].at[s], outs[k].at[s], out_sems.at[q])
                started.append((k, s, sl.half, load, cp, store))
            base += len(sl.arrays)
        for k, s, half_rows, load, cp, store in started:
            load.wait()
            cp.wait_recv()
            rows = min(half_rows, PAIR_SUM_ROWS)

            def add(t, carry, k=k, s=s, rows=rows):
                sl_ = pl.ds(pl.multiple_of(t * rows, rows), rows)
                total[k][s, sl_, :] = (mine[k][s, sl_, :] + got[k][s, sl_, :]).astype(_WIRE)
                return carry
            lax.fori_loop(0, half_rows // rows, add, 0)
            store.start()
        for k, s, half_rows, load, cp, store in started:
            cp.wait_send()
            store.wait()

    flat_in = [a for sl in slabs for a in sl.arrays]
    shapes = [(N_CHIPS, sl.half, sl.cols) for sl in slabs]
    vmem = [pltpu.VMEM(sh, dt) for dt in (F32, F32, _WIRE) for sh in shapes]
    return pl.pallas_call(
        body, name=name,
        in_specs=[ANY] * n_in, out_specs=[ANY] * n, out_shape=[pltpu.HBM(sh, _WIRE) for sh in shapes],
        scratch_shapes=vmem + [pltpu.SemaphoreType.DMA((N_CHIPS * n,))] * 4,
        compiler_params=_comm_params(sum(_nbytes(sh, F32) * 2 + _nbytes(sh, _WIRE) for sh in shapes)),
    )(*_in_hbm(flat_in))


class _ChipExchange:
    def __init__(self, n, ins, outs, bufs, sems):
        self.n, self.ins, self.outs, self.bufs = n, ins, outs, bufs
        self.send_sems, self.recv_sems, self.in_sems, self.out_sems = sems
        self.x, self.y, self.c = _place()
        self.me = 2 * self.x + self.y
        self.pairs = [(k, j) for k in range(n) for j in range(3)]

    @staticmethod
    def scratch(partials):
        n = len(partials)
        return ([pltpu.VMEM(p.shape[1:], p.dtype) for p in partials]
                + [pltpu.SemaphoreType.DMA((3 * n,))] * 2 + [pltpu.SemaphoreType.DMA((n,))] * 2)

    @staticmethod
    def scratch_bytes(partials):
        return sum(_nbytes(p.shape[1:], p.dtype) for p in partials)

    def _copy(self, k, j, src_slot, dst_slot):
        px, py = _flip(self.x, CHIP_MOVES[j][0]), _flip(self.y, CHIP_MOVES[j][1])
        return pltpu.make_async_remote_copy(
            src_ref=self.ins[k].at[src_slot], dst_ref=self.outs[k].at[dst_slot],
            send_sem=self.send_sems.at[3 * k + j], recv_sem=self.recv_sems.at[3 * k + j],
            device_id=(px, py, self.c), device_id_type=MESH_ID)

    def _peer(self, j):
        return 2 * _flip(self.x, CHIP_MOVES[j][0]) + _flip(self.y, CHIP_MOVES[j][1])

    def _local(self, k):
        return _Bounce(self.ins[k].at[self.me], self.bufs[k], self.outs[k].at[self.me],
                       self.in_sems.at[k], self.out_sems.at[k])

    def start(self):
        for k in range(self.n):
            self._local(k).start()
        for k, j in self.pairs:
            self._copy(k, j, self._peer(j), self.me).start()

    def turn(self):
        for k in range(self.n):
            self._local(k).turn()

    def finish(self):
        for k, j in self.pairs:
            self._copy(k, j, self.me, self._peer(j)).wait_recv()
        for k, j in self.pairs:
            self._copy(k, j, self._peer(j), self.me).wait_send()
        for k in range(self.n):
            self._local(k).wait()


def _hosted_steps(steps):
    return dict(start=0, turn=steps // 4, forward=steps // 2, finish=steps - 1)


def _pair_share(slots, vec):
    n = len(slots)
    nv = len(DEV_MOVES)

    def body(*refs):
        ins, vec_ref = refs[:n], refs[n]
        outs, vec_out = refs[n + 1:2 * n + 1], refs[2 * n + 1]
        slot_b, half_b, vec_b = refs[2 * n + 2:3 * n + 2], refs[3 * n + 2:4 * n + 2], refs[4 * n + 2]
        send_sems, recv_sems, in_sems, out_sems = refs[4 * n + 3:]
        x, y, c = _place()
        dev = 4 * x + 2 * y + c

        def vec_copy(j, slot):
            dx, dy, dc = DEV_MOVES[j]
            return pltpu.make_async_remote_copy(
                src_ref=vec_ref, dst_ref=vec_out.at[slot], send_sem=send_sems.at[n + j], recv_sem=recv_sems.at[n + j],
                device_id=(_flip(x, dx), _flip(y, dy), _flip(c, dc)), device_id_type=MESH_ID)

        def rows(k, hc):
            hr = slots[k].shape[1]
            return outs[k].at[pl.ds(pl.multiple_of(hc * hr, 8), hr), :]

        def share(k, hc):
            return pltpu.make_async_remote_copy(
                src_ref=half_b[k], dst_ref=rows(k, hc), send_sem=send_sems.at[k], recv_sem=recv_sems.at[k],
                device_id=(x, y, 1 - c), device_id_type=MESH_ID)

        vec_loc = _Bounce(vec_ref, vec_b, vec_out.at[dev], in_sems.at[n], out_sems.at[n])
        vec_loc.start()
        for j in range(nv):
            vec_copy(j, dev).start()
        loads = [pltpu.make_async_copy(ins[k], slot_b[k], in_sems.at[k]) for k in range(n)]
        stores = [pltpu.make_async_copy(half_b[k], rows(k, c), out_sems.at[k]) for k in range(n)]
        for load in loads:
            load.start()
        vec_loc.turn()
        for k in range(n):
            loads[k].wait()
            hr = slots[k].shape[1]
            step_rows = min(hr, PAIR_SUM_ROWS)

            def add(t, carry, k=k, step_rows=step_rows):
                sl_ = pl.ds(pl.multiple_of(t * step_rows, step_rows), step_rows)
                acc = slot_b[k][0, sl_, :].astype(F32)
                for s in range(1, N_CHIPS):
                    acc = acc + slot_b[k][s, sl_, :].astype(F32)
                half_b[k][sl_, :] = acc
                return carry
            lax.fori_loop(0, hr // step_rows, add, 0)
            stores[k].start()
            share(k, c).start()
        for k in range(n):
            share(k, 1 - c).wait_recv()
        for j, (dx, dy, dc) in enumerate(DEV_MOVES):
            vec_copy(j, 4 * _flip(x, dx) + 2 * _flip(y, dy) + _flip(c, dc)).wait_recv()
        for k in range(n):
            share(k, c).wait_send()
            stores[k].wait()
        for j in range(nv):
            vec_copy(j, dev).wait_send()
        vec_loc.wait()

    halves = [s.shape[1:] for s in slots]
    vmem = ([pltpu.VMEM(s.shape, s.dtype) for s in slots] + [pltpu.VMEM(h, F32) for h in halves]
            + [pltpu.VMEM(vec.shape, F32)])
    outs = pl.pallas_call(
        body, name="grad_pair_share",
        in_specs=[ANY] * (n + 1), out_specs=[ANY] * (n + 1),
        out_shape=[pltpu.HBM((2 * h[0], h[1]), F32) for h in halves] + [pltpu.HBM((N_DEV,) + vec.shape, F32)],
        scratch_shapes=vmem + [pltpu.SemaphoreType.DMA((n + nv,))] * 2 + [pltpu.SemaphoreType.DMA((n + 1,))] * 2,
        compiler_params=_comm_params(sum(_nbytes(s.shape, s.dtype) for s in slots) + sum(_nbytes(h, F32) for h in halves)
                                     + _nbytes(vec.shape, F32)),
    )(*_in_hbm(list(slots) + [vec]))
    return outs[:n], outs[n]


def _row_block(rows, cols, n_arrays):
    br = rows
    while br % 16 == 0 and 2 * n_arrays * br * cols * 4 > (16 << 20):
        br //= 2
    return br


def _sum_slots(a, name):
    n, rows, cols = a.shape
    br = _row_block(rows, cols, n + 1)

    def body(a_ref, o_ref):
        acc = a_ref[0].astype(F32)
        for s in range(1, n):
            acc = acc + a_ref[s].astype(F32)
        o_ref[...] = acc

    return pl.pallas_call(body, name=name, grid=(rows // br,),
                          in_specs=[pl.BlockSpec((n, br, cols), lambda i: (0, i, 0))],
                          out_specs=pl.BlockSpec((br, cols), lambda i: (i, 0)),
                          out_shape=pltpu.HBM((rows, cols), F32),
                          compiler_params=_cparams((n + 1) * br * cols * 4, 1))(*_in_hbm([a]))


def _adamw_math(w, g, m, v):
    m = ADAM_B1 * m + (1.0 - ADAM_B1) * g
    v = ADAM_B2 * v + (1.0 - ADAM_B2) * (g * g)
    m_hat = m / (1.0 - ADAM_B1 ** ADAM_STEP)
    v_hat = v / (1.0 - ADAM_B2 ** ADAM_STEP)
    delta = -ADAM_LR * (m_hat / (jnp.sqrt(v_hat) + ADAM_EPS) + ADAM_WD * w)
    return delta, m, v


def _adamw(g, w, m, v, name):
    rows, cols = g.shape
    br = _row_block(rows, cols, 7)

    def body(g_ref, w_ref, m_ref, v_ref, d_ref, nm_ref, nv_ref):
        d_ref[...], nm_ref[...], nv_ref[...] = _adamw_math(w_ref[...], g_ref[...], m_ref[...], v_ref[...])

    spec = pl.BlockSpec((br, cols), lambda i: (i, 0))
    return pl.pallas_call(body, name=name, grid=(rows // br,), in_specs=[spec] * 4, out_specs=[spec] * 3,
                          out_shape=[jax.ShapeDtypeStruct(g.shape, F32)] * 3,
                          compiler_params=_cparams(7 * br * cols * 4, 1))(g, w, m, v)


ROW_FINAL_G, ROW_PE_G, ROW_LOSS = 0, 1, 2
ROW_CONV_B, ROW_CN_G, ROW_CN_B, ROW_B_PW2 = 8, 9, 10, 11
ROW_LN_G = 16
ROW_ONORM_G, ROW_LB = 24, 25
ROW_CONV_W = 32
SMALL = ["ln_g", "conv_b", "cnorm_g", "cnorm_b", "b_pw2", "onorm_g", "pe_norm_g", "final_g"]
SMALL_ROW = dict(ln_g=ROW_LN_G, conv_b=ROW_CONV_B, cnorm_g=ROW_CN_G, cnorm_b=ROW_CN_B, b_pw2=ROW_B_PW2,
                 onorm_g=ROW_ONORM_G, pe_norm_g=ROW_PE_G, final_g=ROW_FINAL_G)


def _adamw_small(vsum, gcw, lb_logits, params):
    names = SMALL + ["lb_logits", "conv_w"]
    flat = [t for nm in names for t in params[nm]]

    def body(*refs):
        vs_ref, gcw_ref, lbl_ref = refs[:3]
        ins = refs[3:3 + 3 * len(names)]
        outs = refs[3 + 3 * len(names):]
        for q, nm in enumerate(names):
            w_ref, m_ref, v_ref = ins[3 * q:3 * q + 3]
            g_ref, d_ref, nm_ref, nv_ref = outs[4 * q:4 * q + 4]
            if nm == "conv_w":
                g = gcw_ref[...]
            elif nm == "lb_logits":
                lb = _softmax_row0(lbl_ref[...])
                g0 = vs_ref[ROW_LB:ROW_LB + 1, :] * lb * (1.0 - lb)
                g = jnp.concatenate([g0, -g0], axis=0)
            else:
                g = vs_ref[SMALL_ROW[nm]:SMALL_ROW[nm] + 1, :]
            g_ref[...] = g
            d_ref[...], nm_ref[...], nv_ref[...] = _adamw_math(w_ref[...], g, m_ref[...], v_ref[...])

    out_shape = [jax.ShapeDtypeStruct(params[nm][0].shape, F32) for nm in names for _ in range(4)]
    outs = pl.pallas_call(body, name="adamw_small", out_shape=out_shape)(vsum, gcw, lb_logits, *flat)
    return {nm: tuple(outs[4 * q:4 * q + 4]) for q, nm in enumerate(names)}


TOKEN_TILE = dict(rmsnorm=512, inproj_fwd=2048, conv=256, hgrn=512, tail=512, inproj_bwd_u=2048, inproj_bwd_x=512,
                  weight_grad=2048)


def _tile(T, family):
    return min(T, TOKEN_TILE[family])


def kernel(x, p, ln_g, w_in, conv_w, conv_b, cnorm_g, cnorm_b, w_pw2, b_pw2, lb_logits, onorm_g, w_out, pe_norm_g, w_pg, w_pp, final_g, loss_target, m_ln_g, m_w_in, m_conv_w, m_conv_b, m_cnorm_g, m_cnorm_b, m_w_pw2, m_b_pw2, m_lb_logits, m_onorm_g, m_w_out, m_pe_norm_g, m_w_pg, m_w_pp, m_final_g, v_ln_g, v_w_in, v_conv_w, v_conv_b, v_cnorm_g, v_cnorm_b, v_w_pw2, v_b_pw2, v_lb_logits, v_onorm_g, v_w_out, v_pe_norm_g, v_w_pg, v_w_pp, v_final_g):
    given = dict(locals())
    x2, p2, tgt = x[0], p[0, 0], loss_target[0]
    T = x2.shape[0]
    fin_g = final_g.reshape(1, D)

    conv_w_pad = jnp.pad(conv_w[0], ((0, HALO - CONV_K), (0, 0)))
    u, (w_in_f,) = _rmsnorm_gather(x2, ln_g, [w_in[0].astype(_MXU)], [1], _tile(T, "rmsnorm"))

    z, (w_pw2_f, w_out_f, w_pg_f, w_pp_f, conv_w_f) = _inproj_fwd(
        u, w_in_f,
        [w_pw2[0].astype(_MXU), w_out[0].astype(_MXU), w_pg[0].astype(_MXU), w_pp[0].astype(_MXU), conv_w_pad],
        [0, 0, 0, 1, 1], _tile(T, "inproj_fwd"))
    yc, y1 = _conv_fwd(z, conv_w_f, conv_b, cnorm_g, cnorm_b, w_pw2_f, b_pw2, _tile(T, "conv"))
    o_raw, yh, s_chunks = _hgrn_fwd(z, lb_logits, onorm_g, _tile(T, "hgrn"), HB)
    dyc, dyh, n2, ds, dpe, dhb, pb, vec_tail = _tail(
        x2, yc, yh, p2, tgt, w_out_f, w_pg_f, w_pp_f, pe_norm_g, fin_g, _tile(T, "tail"))
    tk = _tile(T, "weight_grad")
    g_w_out_c = _tn_matmul(yc, dhb, tk, "grad_w_out_conv")
    g_w_out_h = _tn_matmul(yh, dhb, tk, "grad_w_out_hgrn")
    g_w_pg = _tn_matmul(n2, ds, tk, "grad_w_pg")
    g_w_pp = _tn_matmul(pb, dpe, tk, "grad_w_pp")
    dzc, a_act, dy2, vec_conv, g_conv_w = _conv_bwd(z, y1, dyc, conv_w_f, cnorm_g, cnorm_b, w_pw2_f, b_pw2, _tile(T, "conv"))
    g_w_pw2 = _tn_matmul(a_act, dy2, tk, "grad_w_pw2")

    rest = ["w_pw2", "w_out", "w_pg", "w_pp"]
    partial_rest = _pair_exchange_sum([
        _Slab([g_w_pw2], lambda s: (0, s * (D // N_CHIPS), 0), (D // N_CHIPS, D)),
        _Slab([g_w_out_c, g_w_out_h], lambda s: (s // 2, (s % 2) * (D // 2), 0), (D // 2, D)),
        _Slab([g_w_pg], lambda s: (0, s * (D // N_CHIPS), 0), (D // N_CHIPS, D)),
        _Slab([g_w_pp], lambda s: (0, 0, s * (D // N_CHIPS)), (PLE, D // N_CHIPS)),
    ], "grad_pair_exchange_rest")
    dzh, vec_hgrn, slots_rest = _hgrn_bwd(z, lb_logits, onorm_g, o_raw, dyh, s_chunks, partial_rest, _tile(T, "hgrn"), HB)
    g_w_in = _inproj_bwd_w(u, dzc, dzh, tk)
    partial_in = _pair_exchange_sum([
        _Slab([g_w_in], lambda s: (0, 0, s * (NPART * D // N_CHIPS)), (D, NPART * D // N_CHIPS))], "grad_pair_exchange_w_in")
    du, slots_in = _inproj_bwd_u(dzc, dzh, w_in_f, partial_in, _tile(T, "inproj_bwd_u"))
    grad_x, vec_in = _inproj_bwd_x(x2, ln_g, du, dhb, _tile(T, "inproj_bwd_x"))
    big = ["w_in"] + rest
    vec = jnp.concatenate([vec_tail, vec_conv, vec_in, vec_hgrn, g_conv_w], axis=0)
    grads_big, vec_slots = _pair_share(list(slots_in) + list(slots_rest), vec)
    vsum = _sum_slots(vec_slots, "vec_sum")

    out = {}
    for nm, g in zip(big, grads_big):
        w2, m2, v2 = given[nm][0], given["m_" + nm][0], given["v_" + nm][0]
        d, nm_, nv_ = _adamw(g, w2, m2, v2, "adamw_" + nm)
        out[nm] = tuple(t[None] for t in (g, d, nm_, nv_))
    chip = 2 * lax.axis_index("x") + lax.axis_index("y")
    gcw = lax.dynamic_slice(vsum, (ROW_CONV_W, chip * (D // N_CHIPS)), (CONV_K, D // N_CHIPS))
    params = {nm: (given[nm].reshape(-1, D), given["m_" + nm].reshape(-1, D), given["v_" + nm].reshape(-1, D))
              for nm in SMALL + ["lb_logits"]}
    params["conv_w"] = (conv_w[0], m_conv_w[0], v_conv_w[0])
    small = _adamw_small(vsum, gcw, lb_logits, params)
    for nm, ts in small.items():
        out[nm] = tuple(t.reshape(given[nm].shape) for t in ts)

    loss = vsum[ROW_LOSS, 0]
    order = ["ln_g", "w_in", "conv_w", "conv_b", "cnorm_g", "cnorm_b", "w_pw2", "b_pw2", "lb_logits", "onorm_g",
             "w_out", "pe_norm_g", "w_pg", "w_pp", "final_g"]
    return (loss, grad_x[None], *[out[nm][0] for nm in order], *[out[nm][1] for nm in order],
            *[out[nm][2] for nm in order], *[out[nm][3] for nm in order])
```

```python
import functools

import jax
import jax.numpy as jnp
from jax import lax
from jax.experimental import pallas as pl
from jax.experimental.pallas import tpu as pltpu

F32 = jnp.float32
BF16 = jnp.bfloat16
_MXU = jnp.bfloat16
_WIRE = jnp.bfloat16

D = 1024
NPART = 7
PLE = 256
HEADS = 8
HD = 128
CHUNK = 64
CONV_K = 31
HALO = 32
EPS = 1e-6
N_CHIPS = 4
N_DEV = 8
HB = 8

ADAM_LR = 0.001
ADAM_B1 = 0.9
ADAM_B2 = 0.999
ADAM_EPS = 1e-08
ADAM_WD = 0.01
ADAM_STEP = 10

V7X_VMEM_LIMIT = 60000 * 1024
MESH_ID = pl.DeviceIdType.MESH
ANY = pl.BlockSpec(memory_space=pltpu.HBM)


def _in_hbm(arrays):
    return [pltpu.with_memory_space_constraint(a, pltpu.HBM) for a in arrays]


def _cparams(block_bytes, n_grid_dims):
    limit = min(V7X_VMEM_LIMIT, 2 * block_bytes + (24 << 20))
    return pltpu.CompilerParams(vmem_limit_bytes=int(limit), dimension_semantics=("arbitrary",) * n_grid_dims)


def _nbytes(shape, dtype):
    n = 1
    for s in shape:
        n *= s
    return n * jnp.dtype(dtype).itemsize


def _dot(a, b):
    return jnp.dot(a.astype(_MXU), b.astype(_MXU), preferred_element_type=F32)


def _dot_nt(a, b):
    return lax.dot_general(a.astype(_MXU), b.astype(_MXU), (((1,), (1,)), ((), ())), preferred_element_type=F32)


def _dot_tn(a, b):
    return lax.dot_general(a.astype(_MXU), b.astype(_MXU), (((0,), (0,)), ((), ())), preferred_element_type=F32)


def _tri_dot(tri_bf, x):
    x1 = x.astype(BF16)
    r1 = x - x1.astype(F32)
    x2 = r1.astype(BF16)
    x3 = (r1 - x2.astype(F32)).astype(BF16)
    d = lambda t: jnp.dot(tri_bf, t, preferred_element_type=F32)
    return d(x1) + d(x2) + d(x3)


def _split2(x):
    hi = x.astype(BF16)
    return hi, (x - hi.astype(F32)).astype(BF16)


def _dot3(dims, a, b):
    d = lambda p, q: lax.dot_general(p, q, (dims, ((), ())), preferred_element_type=F32)
    return d(a[0], b[0]) + d(a[0], b[1]) + d(a[1], b[0])


def _sigmoid(x):
    return jax.nn.sigmoid(x)


def _mean_lanes(x):
    return jnp.mean(x, axis=-1, keepdims=True)


def _sum_rows(x):
    return jnp.sum(x, axis=0, keepdims=True)


def _group_ln(y):
    yn, rs = [], []
    for g in range(D // HD):
        blk = y[:, g * HD:(g + 1) * HD]
        xc = blk - _mean_lanes(blk)
        r = lax.rsqrt(_mean_lanes(xc * xc) + EPS)
        yn.append(xc * r)
        rs.append(jnp.broadcast_to(r, blk.shape))
    return jnp.concatenate(yn, axis=1), jnp.concatenate(rs, axis=1)


def _group_ln_bwd(dyn, yn, rstd):
    out = []
    for g in range(D // HD):
        sl = slice(g * HD, (g + 1) * HD)
        d, n = dyn[:, sl], yn[:, sl]
        out.append(rstd[:, sl] * (d - _mean_lanes(d) - n * _mean_lanes(d * n)))
    return jnp.concatenate(out, axis=1)


def _head_means(x, hb, fn=lambda m: m):
    return jnp.concatenate([jnp.broadcast_to(fn(_mean_lanes(x[:, hh * HD:(hh + 1) * HD])), (x.shape[0], HD))
                            for hh in range(hb)], axis=1)


def _head_rsqrt_mean(x, hb):
    return _head_means(x, hb, lambda m: lax.rsqrt(m + EPS))


def _softmax_row0(lbl):
    m = jnp.max(lbl, axis=0, keepdims=True)
    e = jnp.exp(lbl - m)
    return e[0:1, :] / jnp.sum(e, axis=0, keepdims=True)


def _hosted_gather(phases, step, at, shards, axes, ins, outs, bufs, sems):
    gather = _Gather([s.shape for s in shards], axes, ins, outs, bufs, sems)
    for phase in phases:
        pl.when(step == at[phase])(getattr(gather, phase))


def _rmsnorm_gather(x, ln_g, shards, axes, tT):
    T = x.shape[0]
    n = len(shards)
    at = _hosted_steps(T // tT)

    def body(x_ref, g_ref, *rest):
        ins, u_ref, outs, bufs, sems = rest[:n], rest[n], rest[n + 1:2 * n + 1], rest[2 * n + 1:3 * n + 1], rest[3 * n + 1:]
        host = functools.partial(_hosted_gather, step=pl.program_id(0), at=at, shards=shards, axes=axes,
                                 ins=ins, outs=outs, bufs=bufs, sems=sems)
        host(("start", "turn", "forward"))
        xv = x_ref[...]
        r = lax.rsqrt(_mean_lanes(xv * xv) + EPS)
        u_ref[...] = (xv * r * g_ref[...]).astype(_MXU)
        host(("finish",))

    blk = _nbytes((tT, D), F32) * 2 + _nbytes((tT, D), _MXU) + sum(_nbytes(s.shape, s.dtype) for s in shards)
    outs = pl.pallas_call(
        body, name="rmsnorm_gather", grid=(T // tT,),
        in_specs=[pl.BlockSpec((tT, D), lambda i: (i, 0)), pl.BlockSpec((1, D), lambda i: (0, 0))] + [ANY] * n,
        out_specs=[pl.BlockSpec((tT, D), lambda i: (i, 0))] + [ANY] * n,
        out_shape=[jax.ShapeDtypeStruct((T, D), _MXU)]
        + [pltpu.HBM(fs, s.dtype) for fs, s in zip(_full_shapes(shards, axes), shards)],
        scratch_shapes=_Gather.scratch(shards),
        compiler_params=_cparams(blk, 1),
    )(x, ln_g, *_in_hbm(shards))
    return outs[0], outs[1:]


def _inproj_fwd(u, w_in, shards, axes, tT):
    T = u.shape[0]
    n = len(shards)
    at = _hosted_steps((T // tT) * NPART)

    def body(u_ref, w_ref, *rest):
        ins, z_ref, outs, bufs, sems = rest[:n], rest[n], rest[n + 1:2 * n + 1], rest[2 * n + 1:3 * n + 1], rest[3 * n + 1:]
        host = functools.partial(_hosted_gather, step=pl.program_id(0) * NPART + pl.program_id(1), at=at, shards=shards,
                                 axes=axes, ins=ins, outs=outs, bufs=bufs, sems=sems)
        host(("start", "turn", "forward"))
        z_ref[...] = jnp.dot(u_ref[...], w_ref[...], preferred_element_type=F32)
        host(("finish",))

    blk = (_nbytes((tT, D), F32) + _nbytes((D, D), _MXU) + _nbytes((tT, D), _MXU)
           + sum(_nbytes(s.shape, s.dtype) for s in shards))
    outs = pl.pallas_call(
        body, name="inproj_fwd", grid=(T // tT, NPART),
        in_specs=[pl.BlockSpec((tT, D), lambda i, j: (i, 0)), pl.BlockSpec((D, D), lambda i, j: (0, j))] + [ANY] * n,
        out_specs=[pl.BlockSpec((tT, D), lambda i, j: (i, j))] + [ANY] * n,
        out_shape=[jax.ShapeDtypeStruct((T, NPART * D), F32)]
        + [pltpu.HBM(fs, s.dtype) for fs, s in zip(_full_shapes(shards, axes), shards)],
        scratch_shapes=_Gather.scratch(shards),
        compiler_params=_cparams(blk, 2),
    )(u, w_in, *_in_hbm(shards))
    return outs[0], outs[1:]


def _shifted_windows(ext, first, visit):
    n = ext.shape[0]
    for m in range(first, first + CONV_K):
        visit(m, (ext if m == 0 else pltpu.roll(ext, n - m, axis=0))[0:n - HALO, :])


def _conv_fwd(z, conv_w, conv_b, cn_g, cn_b, w_pw2, b_pw2, tT):
    T = z.shape[0]

    def body(cv_ref, cg_ref, ct_ref, cw_ref, cb_ref, ng_ref, nb_ref, wp_ref, bp_ref, yc_ref, y1_ref, ext):
        @pl.when(pl.program_id(0) == 0)
        def _():
            ext[...] = jnp.zeros_like(ext)
        ext[0:HALO, :] = ext[tT:tT + HALO, :]
        ext[HALO:, :] = cv_ref[...] * _sigmoid(cg_ref[...])
        cw = cw_ref[...]
        acc = [cb_ref[...]]

        def tap(m, win):
            acc[0] = acc[0] + win * cw[m - 2:m - 1, :]
        _shifted_windows(ext[...], 2, tap)
        y1 = acc[0]
        y1_ref[...] = y1
        yn, _ = _group_ln(y1)
        apre = yn * ng_ref[...] + nb_ref[...]
        a = apre * _sigmoid(apre)
        y2 = _dot(a, wp_ref[...]) + bp_ref[...]
        ct = ct_ref[...]
        yc_ref[...] = (y2 * (ct * _sigmoid(ct))).astype(_MXU)

    part = lambda p: pl.BlockSpec((tT, D), lambda i: (i, p))
    row = pl.BlockSpec((1, D), lambda i: (0, 0))
    tok = pl.BlockSpec((tT, D), lambda i: (i, 0))
    blk = 4 * _nbytes((tT, D), F32) + _nbytes((D, D), _MXU) + _nbytes((tT, D), _MXU) + 8 * _nbytes((tT + HALO, D), F32)
    return pl.pallas_call(
        body, name="conv_fwd", grid=(T // tT,),
        in_specs=[part(0), part(1), part(2), pl.BlockSpec((HALO, D), lambda i: (0, 0)), row, row, row,
                  pl.BlockSpec((D, D), lambda i: (0, 0)), row],
        out_specs=[tok, tok],
        out_shape=[jax.ShapeDtypeStruct((T, D), _MXU), jax.ShapeDtypeStruct((T, D), F32)],
        scratch_shapes=[pltpu.VMEM((tT + HALO, D), F32)],
        compiler_params=_cparams(blk, 1),
    )(z, z, z, conv_w, conv_b, cn_g, cn_b, w_pw2, b_pw2)


def _hgrn_gates(lb, hq, hf):
    sq = _sigmoid(hq)
    sg = _sigmoid(hf)
    f = lb + (1.0 - lb) * sg
    return sq, sg, f, hq * sq, (1.0 - lb) * (1.0 - sg), jnp.log(f)


def _chunk_decays(lf, q, k):
    r = lax.broadcasted_iota(jnp.int32, (CHUNK, CHUNK), 0)
    c = lax.broadcasted_iota(jnp.int32, (CHUNK, CHUNK), 1)
    b = _tri_dot((r >= c).astype(BF16), lf)
    bm = b[CHUNK // 2 - 1:CHUNK // 2, :]
    bl = b[CHUNK - 1:CHUNK, :]
    eb = jnp.exp(b)
    eqm = jnp.exp(b - bm)
    ekm = jnp.exp(bm - b)
    ekd = jnp.exp(bl - b)
    return dict(causal=r >= c, eb=eb, eqm=eqm, ekm=ekm, ekd=ekd, ebl=jnp.exp(bl),
                qd=q * eb, qm=q * eqm, km=k * ekm, kd=k * ekd)


def _hgrn_fwd(z, lb_logits, onorm_g, tT, hb):
    T = z.shape[0]
    nc = tT // CHUNK
    w = hb * HD

    def body(lbl_ref, og_ref, hq_ref, hf_ref, hi_ref, hg_ref, o_ref, yh_ref, sc_ref, st):
        @pl.when(pl.program_id(1) == 0)
        def _():
            st[...] = jnp.zeros_like(st)
        lb_all = _softmax_row0(lbl_ref[...])
        og_all = og_ref[...]

        def chunk(c, carry):
            sl = pl.ds(pl.multiple_of(c * CHUNK, CHUNK), CHUNK)
            lanes = [slice(hh * HD, (hh + 1) * HD) for hh in range(hb)]
            heads = lambda fn: [fn(hh, ln) for hh, ln in enumerate(lanes)]
            hg, v = hg_ref[sl, :], hi_ref[sl, :]
            _, _, _, q, k, lf = _hgrn_gates(lb_all, hq_ref[sl, :], hf_ref[sl, :])
            dc = _chunk_decays(lf, q, k)
            s_t = heads(lambda hh, ln: st[hh])
            a = heads(lambda hh, ln: jnp.where(dc["causal"], _dot_nt(dc["qm"][:, ln], dc["km"][:, ln]), 0.0))
            o_inter = heads(lambda hh, ln: _dot_nt(dc["qd"][:, ln], s_t[hh]))
            kv = heads(lambda hh, ln: _dot_tn(v[:, ln], dc["kd"][:, ln]))
            o_intra = heads(lambda hh, ln: _dot(a[hh], v[:, ln]))
            for hh, ln in enumerate(lanes):
                sc_ref[hh, c] = s_t[hh]
                st[hh] = s_t[hh] * dc["ebl"][:, ln] + kv[hh]
            o = jnp.concatenate([o_inter[hh] + o_intra[hh] for hh in range(hb)], axis=1)
            o_ref[sl, :] = o
            n = o * _head_rsqrt_mean(o * o, hb)
            yh_ref[sl, :] = ((n * og_all) * (hg * _sigmoid(hg))).astype(_MXU)
            return carry

        lax.fori_loop(0, nc, chunk, 0, unroll=8)

    zpart = lambda p: pl.BlockSpec((tT, w), lambda h, i: (i, p * (HEADS // hb) + h))
    blk = 6 * _nbytes((tT, w), F32) + _nbytes((hb, nc, HD, HD), F32)
    return pl.pallas_call(
        body, name="hgrn_fwd", grid=(HEADS // hb, T // tT),
        in_specs=[pl.BlockSpec((2, w), lambda h, i: (0, h)), pl.BlockSpec((1, w), lambda h, i: (0, h)),
                  zpart(3), zpart(4), zpart(5), zpart(6)],
        out_specs=[pl.BlockSpec((tT, w), lambda h, i: (i, h)), pl.BlockSpec((tT, w), lambda h, i: (i, h)),
                   pl.BlockSpec((hb, nc, HD, HD), lambda h, i: (h, i, 0, 0))],
        out_shape=[jax.ShapeDtypeStruct((T, D), F32), jax.ShapeDtypeStruct((T, D), _MXU),
                   jax.ShapeDtypeStruct((HEADS, T // CHUNK, HD, HD), F32)],
        scratch_shapes=[pltpu.VMEM((hb, HD, HD), F32)],
        compiler_params=_cparams(blk, 2),
    )(lb_logits, onorm_g, z, z, z, z)


def _hgrn_bwd(z, lb_logits, onorm_g, o_raw, dyh, s_chunks, partials, tT, hb):
    T = z.shape[0]
    nc = tT // CHUNK
    nI = T // tT
    w = hb * HD
    n = len(partials)
    at = _hosted_steps((HEADS // hb) * nI)

    def body(lbl_ref, og_ref, hq_ref, hf_ref, hi_ref, hg_ref, o_ref, dy_ref, sc_ref, *rest):
        (dz_ref, vec_ref), dst = rest[n:n + 2], rest[2 * n + 2]
        exchange = _ChipExchange(n, rest[:n], rest[n + 2:2 * n + 2], rest[2 * n + 3:3 * n + 3], rest[3 * n + 3:])
        step = pl.program_id(0) * nI + pl.program_id(1)
        pl.when(step == at["start"])(exchange.start)
        pl.when(step == at["turn"])(exchange.turn)

        @pl.when(pl.program_id(1) == 0)
        def _():
            dst[...] = jnp.zeros_like(dst)
            vec_ref[...] = jnp.zeros_like(vec_ref)
        lb_all = _softmax_row0(lbl_ref[...])
        og_all = og_ref[...]
        last_row = lax.broadcasted_iota(jnp.int32, (CHUNK, w), 0) == CHUNK - 1
        r64 = lax.broadcasted_iota(jnp.int32, (CHUNK, CHUNK), 0)
        c64 = lax.broadcasted_iota(jnp.int32, (CHUNK, CHUNK), 1)
        upper = (c64 >= r64).astype(BF16)
        lanes = [slice(hh * HD, (hh + 1) * HD) for hh in range(hb)]
        heads = lambda fn: [fn(hh, ln) for hh, ln in enumerate(lanes)]
        wide = lambda parts: jnp.concatenate(parts, axis=1)

        def chunk(cc, carry):
            c = nc - 1 - cc
            sl = pl.ds(pl.multiple_of(c * CHUNK, CHUNK), CHUNK)
            hq, hg, v = hq_ref[sl, :], hg_ref[sl, :], hi_ref[sl, :]
            sq, sg, f, q, k, lf = _hgrn_gates(lb_all, hq, hf_ref[sl, :])
            dc = _chunk_decays(lf, q, k)
            s_t = heads(lambda hh, ln: sc_ref[hh, c])
            ds_t = heads(lambda hh, ln: dst[hh])
            o, dy = o_ref[sl, :], dy_ref[sl, :]
            r = _head_rsqrt_mean(o * o, hb)
            n = o * r
            sgg = _sigmoid(hg)
            silu_g = hg * sgg
            dhg = dy * (n * og_all) * (sgg * (1.0 + hg * (1.0 - sgg)))
            dn = dy * og_all * silu_g
            g_og = _sum_rows(dy * n * silu_g)
            do = r * (dn - n * _head_means(dn * n, hb))
            a = heads(lambda hh, ln: jnp.where(dc["causal"], _dot_nt(dc["qm"][:, ln], dc["km"][:, ln]), 0.0))
            dam = heads(lambda hh, ln: jnp.where(dc["causal"], _dot_nt(do[:, ln], v[:, ln]), 0.0))
            dqd = wide(heads(lambda hh, ln: _dot(do[:, ln], s_t[hh])))
            dkd = wide(heads(lambda hh, ln: _dot(v[:, ln], ds_t[hh])))
            dv_inter = heads(lambda hh, ln: _dot_nt(dc["kd"][:, ln], ds_t[hh]))
            dqs = heads(lambda hh, ln: _dot_tn(do[:, ln], dc["qd"][:, ln]))
            dv = wide(heads(lambda hh, ln: _dot_tn(a[hh], do[:, ln]) + dv_inter[hh]))
            dam2 = [_split2(t) for t in dam]
            km2, qm2 = _split2(dc["km"]), _split2(dc["qm"])
            dqm = wide(heads(lambda hh, ln: _dot3(((1,), (0,)), dam2[hh], (km2[0][:, ln], km2[1][:, ln]))))
            dkm = wide(heads(lambda hh, ln: _dot3(((0,), (0,)), dam2[hh], (qm2[0][:, ln], qm2[1][:, ln]))))
            debl = wide(heads(lambda hh, ln: _sum_rows(ds_t[hh] * s_t[hh])))
            for hh, ln in enumerate(lanes):
                dst[hh] = ds_t[hh] * dc["ebl"][:, ln] + dqs[hh]
            dq = dqd * dc["eb"] + dqm * dc["eqm"]
            dk = dkm * dc["ekm"] + dkd * dc["ekd"]
            dbl = _sum_rows(dkd * dc["kd"]) + debl * dc["ebl"]
            db = dq * q - dk * k + jnp.where(last_row, dbl, 0.0)
            dlf = _tri_dot(upper, db)
            dfk = dlf / f - dk
            dz_ref[0, sl, :] = (dq * (sq * (1.0 + hq * (1.0 - sq)))).astype(_MXU)
            dz_ref[1, sl, :] = (dfk * ((1.0 - lb_all) * sg * (1.0 - sg))).astype(_MXU)
            dz_ref[2, sl, :] = dv.astype(_MXU)
            dz_ref[3, sl, :] = dhg.astype(_MXU)
            vec_ref[0:1, :] += g_og
            vec_ref[1:2, :] += _sum_rows(dfk * (1.0 - sg))
            return carry

        lax.fori_loop(0, nc, chunk, 0, unroll=8)
        pl.when(step == at["finish"])(exchange.finish)

    zpart = lambda p: pl.BlockSpec((tT, w), lambda h, i: (nI - 1 - i, p * (HEADS // hb) + h))
    act = pl.BlockSpec((tT, w), lambda h, i: (nI - 1 - i, h))
    blk = (6 * _nbytes((tT, w), F32) + _nbytes((hb, nc, HD, HD), F32) + 4 * _nbytes((tT, w), _MXU)
           + _ChipExchange.scratch_bytes(partials))
    outs = pl.pallas_call(
        body, name="hgrn_bwd", grid=(HEADS // hb, nI),
        in_specs=[pl.BlockSpec((2, w), lambda h, i: (0, h)), pl.BlockSpec((1, w), lambda h, i: (0, h)),
                  zpart(3), zpart(4), zpart(5), zpart(6), act, act,
                  pl.BlockSpec((hb, nc, HD, HD), lambda h, i: (h, nI - 1 - i, 0, 0))] + [ANY] * n,
        out_specs=[pl.BlockSpec((4, tT, w), lambda h, i: (0, nI - 1 - i, h)),
                   pl.BlockSpec((8, w), lambda h, i: (0, h))] + [ANY] * n,
        out_shape=[jax.ShapeDtypeStruct((4, T, D), _MXU), jax.ShapeDtypeStruct((8, D), F32)]
        + [pltpu.HBM(p.shape, p.dtype) for p in partials],
        scratch_shapes=[pltpu.VMEM((hb, HD, HD), F32)] + _ChipExchange.scratch(partials),
        compiler_params=_cparams(blk, 2),
    )(lb_logits, onorm_g, z, z, z, z, o_raw, dyh, s_chunks, *_in_hbm(partials))
    return outs[0], outs[1], outs[2:]


def _tail(x, yc, yh, p, target, w_out, w_pg, w_pp, pe_g, fin_g, tT):
    T = x.shape[0]

    def body(x_ref, yc_ref, yh_ref, p_ref, t_ref, wo_ref, wg_ref, wp_ref, pg_ref, fg_ref,
             dyc_ref, dyh_ref, n2_ref, ds_ref, dpe_ref, dhb_ref, pb_ref, vec_ref):
        @pl.when(pl.program_id(0) == 0)
        def _():
            vec_ref[...] = jnp.zeros_like(vec_ref)
        wo_c, wo_h = wo_ref[0:D, :], wo_ref[D:2 * D, :]
        h = x_ref[...] + _dot(yc_ref[...], wo_c) + _dot(yh_ref[...], wo_h)
        pb = p_ref[...].astype(_MXU)
        pe = _dot(pb, wp_ref[...])
        r2 = lax.rsqrt(_mean_lanes(h * h) + EPS)
        hn = h * r2
        n2 = (hn * pg_ref[...]).astype(_MXU)
        gate = _sigmoid(_dot(n2, wg_ref[...]))
        h2 = h + gate * pe
        r3 = lax.rsqrt(_mean_lanes(h2 * h2) + EPS)
        h2n = h2 * r3
        err = h2n * fg_ref[...] - t_ref[...]
        vec_ref[ROW_LOSS:ROW_LOSS + 1, :] += 0.5 * jnp.sum(_mean_lanes(err * err))
        dout = err * (1.0 / D)
        vec_ref[0:1, :] += _sum_rows(dout * h2n)
        dn3 = dout * fg_ref[...]
        dh2 = r3 * (dn3 - h2n * _mean_lanes(dn3 * h2n))
        ds = (dh2 * pe * gate * (1.0 - gate)).astype(_MXU)
        dn2 = _dot_nt(ds, wg_ref[...])
        vec_ref[1:2, :] += _sum_rows(dn2 * hn)
        dnn = dn2 * pg_ref[...]
        dh = dh2 + r2 * (dnn - hn * _mean_lanes(dnn * hn))
        dhb = dh.astype(_MXU)
        dyc_ref[...] = _dot_nt(dhb, wo_c)
        dyh_ref[...] = _dot_nt(dhb, wo_h)
        n2_ref[...] = n2
        ds_ref[...] = ds
        dpe_ref[...] = (dh2 * gate).astype(_MXU)
        dhb_ref[...] = dhb
        pb_ref[...] = pb

    tok = lambda w: pl.BlockSpec((tT, w), lambda i: (i, 0))
    full = lambda r, c: pl.BlockSpec((r, c), lambda i: (0, 0))
    tokshape = lambda w, dt: jax.ShapeDtypeStruct((T, w), dt)
    blk = (5 * _nbytes((tT, D), F32) + 7 * _nbytes((tT, D), _MXU) + _nbytes((4 * D + PLE, D), _MXU)
           + 12 * _nbytes((tT, D), F32))
    return pl.pallas_call(
        body, name="tail_fwd_bwd", grid=(T // tT,),
        in_specs=[tok(D), tok(D), tok(D), tok(PLE), tok(D), full(2 * D, D), full(D, D), full(PLE, D), full(1, D), full(1, D)],
        out_specs=[tok(D), tok(D), tok(D), tok(D), tok(D), tok(D), tok(PLE), full(8, D)],
        out_shape=[tokshape(D, F32), tokshape(D, F32), tokshape(D, _MXU), tokshape(D, _MXU),
                   tokshape(D, _MXU), tokshape(D, _MXU), tokshape(PLE, _MXU),
                   jax.ShapeDtypeStruct((8, D), F32)],
        compiler_params=_cparams(blk, 1),
    )(x, yc, yh, p, target, w_out, w_pg, w_pp, pe_g, fin_g)


def _conv_bwd(z, y1, dyc, conv_w, cn_g, cn_b, w_pw2, b_pw2, tT):
    T = z.shape[0]
    nI = T // tT
    hb = tT // HALO

    def body(cv_ref, cg_ref, ct_ref, hv_ref, hg_ref, y1_ref, dyc_ref, cw_ref, ng_ref, nb_ref, wp_ref, bp_ref,
             dz_ref, a_ref, dy2_ref, vec_ref, gcw_ref, ext, ext2, gpart):
        i = pl.program_id(0)

        @pl.when(i == 0)
        def _():
            ext2[...] = jnp.zeros_like(ext2)
            gpart[...] = jnp.zeros_like(gpart)
            vec_ref[...] = jnp.zeros_like(vec_ref)
        cv, cg, ct = cv_ref[...], cg_ref[...], ct_ref[...]
        sg = _sigmoid(cg)
        has_hist = (i < nI - 1).astype(F32)
        ext[0:HALO, :] = hv_ref[...] * _sigmoid(hg_ref[...]) * has_hist
        ext[HALO:, :] = cv * sg
        yn, rstd = _group_ln(y1_ref[...])
        apre = yn * ng_ref[...] + nb_ref[...]
        sa = _sigmoid(apre)
        a = (apre * sa).astype(_MXU)
        y2 = _dot(a, wp_ref[...]) + bp_ref[...]
        st = _sigmoid(ct)
        dyc_v = dyc_ref[...]
        dy2 = dyc_v * (ct * st)
        dy2b = dy2.astype(_MXU)
        da = _dot_nt(dy2b, wp_ref[...])
        dapre = da * (sa * (1.0 + apre * (1.0 - sa)))
        dy1 = _group_ln_bwd(dapre * ng_ref[...], yn, rstd)
        vec_ref[0:1, :] += _sum_rows(dy1)
        vec_ref[1:2, :] += _sum_rows(dapre * yn)
        vec_ref[2:3, :] += _sum_rows(dapre)
        vec_ref[3:4, :] += _sum_rows(dy2)
        dz_ref[2] = (dyc_v * y2 * (st * (1.0 + ct * (1.0 - st)))).astype(_MXU)
        a_ref[...] = a
        dy2_ref[...] = dy2b
        ext2[tT:tT + HALO, :] = ext2[0:HALO, :]
        ext2[0:tT, :] = dy1
        def grad_tap(m, win):
            p = dy1 * win
            part = p[0:8, :]
            for q in range(1, tT // 8):
                part = part + p[8 * q:8 * q + 8, :]
            gpart[m - 2] += part
        _shifted_windows(ext[...], 2, grad_tap)
        cw = cw_ref[...]
        acc = [None]

        def dv_tap(m, win):
            term = win * cw[CONV_K - 1 - m:CONV_K - m, :]
            acc[0] = term if acc[0] is None else acc[0] + term
        _shifted_windows(ext2[...], 0, dv_tap)
        dv = acc[0]
        dz_ref[0] = (dv * sg).astype(_MXU)
        dz_ref[1] = (dv * cv * sg * (1.0 - sg)).astype(_MXU)

        @pl.when(i == nI - 1)
        def _():
            gcw_ref[...] = jnp.sum(gpart[...], axis=1)

    part = lambda p: pl.BlockSpec((tT, D), lambda i: (nI - 1 - i, p))
    hist = lambda p: pl.BlockSpec((HALO, D), lambda i: (jnp.maximum((nI - 1 - i) * hb - 1, 0), p))
    tok = pl.BlockSpec((tT, D), lambda i: (nI - 1 - i, 0))
    row = pl.BlockSpec((1, D), lambda i: (0, 0))
    blk = (5 * _nbytes((tT, D), F32) + _nbytes((D, D), _MXU) + 5 * _nbytes((tT, D), _MXU)
           + 10 * _nbytes((tT + HALO, D), F32))
    return pl.pallas_call(
        body, name="conv_bwd", grid=(nI,),
        in_specs=[part(0), part(1), part(2), hist(0), hist(1), tok, tok, pl.BlockSpec((HALO, D), lambda i: (0, 0)),
                  row, row, pl.BlockSpec((D, D), lambda i: (0, 0)), row],
        out_specs=[pl.BlockSpec((3, tT, D), lambda i: (0, nI - 1 - i, 0)), tok, tok,
                   pl.BlockSpec((8, D), lambda i: (0, 0)), pl.BlockSpec((HALO, D), lambda i: (0, 0))],
        out_shape=[jax.ShapeDtypeStruct((3, T, D), _MXU), jax.ShapeDtypeStruct((T, D), _MXU),
                   jax.ShapeDtypeStruct((T, D), _MXU), jax.ShapeDtypeStruct((8, D), F32),
                   jax.ShapeDtypeStruct((HALO, D), F32)],
        scratch_shapes=[pltpu.VMEM((tT + HALO, D), F32), pltpu.VMEM((tT + HALO, D), F32), pltpu.VMEM((HALO, 8, D), F32)],
        compiler_params=_cparams(blk, 1),
    )(z, z, z, z, z, y1, dyc, conv_w, cn_g, cn_b, w_pw2, b_pw2)


def _inproj_bwd_u(dzc, dzh, w_in, partials, tT):
    T = dzc.shape[1]
    n = len(partials)
    at = _hosted_steps((T // tT) * NPART)

    def body(dzc_ref, dzh_ref, w_ref, *rest):
        du_ref, acc = rest[n], rest[2 * n + 1]
        exchange = _ChipExchange(n, rest[:n], rest[n + 1:2 * n + 1], rest[2 * n + 2:3 * n + 2], rest[3 * n + 2:])
        j = pl.program_id(1)
        step = pl.program_id(0) * NPART + j
        pl.when(step == at["start"])(exchange.start)
        pl.when(step == at["turn"])(exchange.turn)

        @pl.when(j == 0)
        def _():
            acc[...] = jnp.zeros_like(acc)

        @pl.when(j < 3)
        def _():
            acc[...] += _dot_nt(dzc_ref[0], w_ref[...])

        @pl.when(j >= 3)
        def _():
            acc[...] += _dot_nt(dzh_ref[0], w_ref[...])

        @pl.when(j == NPART - 1)
        def _():
            du_ref[...] = acc[...].astype(du_ref.dtype)
        pl.when(step == at["finish"])(exchange.finish)

    blk = (3 * _nbytes((tT, D), _MXU) + _nbytes((D, D), _MXU) + _nbytes((tT, D), F32)
           + _ChipExchange.scratch_bytes(partials))
    outs = pl.pallas_call(
        body, name="inproj_bwd_u", grid=(T // tT, NPART),
        in_specs=[pl.BlockSpec((1, tT, D), lambda i, j: (jnp.minimum(j, 2), i, 0)),
                  pl.BlockSpec((1, tT, D), lambda i, j: (jnp.maximum(j - 3, 0), i, 0)),
                  pl.BlockSpec((D, D), lambda i, j: (0, j))] + [ANY] * n,
        out_specs=[pl.BlockSpec((tT, D), lambda i, j: (i, 0))] + [ANY] * n,
        out_shape=[jax.ShapeDtypeStruct((T, D), _MXU)] + [pltpu.HBM(p.shape, p.dtype) for p in partials],
        scratch_shapes=[pltpu.VMEM((tT, D), F32)] + _ChipExchange.scratch(partials),
        compiler_params=_cparams(blk, 2),
    )(dzc, dzh, w_in, *_in_hbm(partials))
    return outs[0], outs[1:]


def _inproj_bwd_x(x, ln_g, du, dh, tT):
    T = x.shape[0]

    def body(x_ref, g_ref, du_ref, dh_ref, gx_ref, vec_ref):
        @pl.when(pl.program_id(0) == 0)
        def _():
            vec_ref[...] = jnp.zeros_like(vec_ref)
        xv = x_ref[...]
        r = lax.rsqrt(_mean_lanes(xv * xv) + EPS)
        xn = xv * r
        duv = du_ref[...].astype(F32)
        vec_ref[0:1, :] += _sum_rows(duv * xn)
        dun = duv * g_ref[...]
        gx_ref[...] = dh_ref[...].astype(F32) + r * (dun - xn * _mean_lanes(dun * xn))

    tok = pl.BlockSpec((tT, D), lambda i: (i, 0))
    return pl.pallas_call(
        body, name="inproj_bwd_x", grid=(T // tT,),
        in_specs=[tok, pl.BlockSpec((1, D), lambda i: (0, 0)), tok, tok],
        out_specs=[tok, pl.BlockSpec((8, D), lambda i: (0, 0))],
        out_shape=[jax.ShapeDtypeStruct((T, D), F32), jax.ShapeDtypeStruct((8, D), F32)],
        compiler_params=_cparams(6 * _nbytes((tT, D), F32), 1),
    )(x, ln_g, du, dh)


def _inproj_bwd_w(u, dzc, dzh, tk):
    T = u.shape[0]
    nK = T // tk

    def body(u_ref, dzc_ref, dzh_ref, gw_ref):
        j, k = pl.program_id(0), pl.program_id(1)

        @pl.when(k == 0)
        def _():
            gw_ref[...] = jnp.zeros_like(gw_ref)

        @pl.when(j < 3)
        def _():
            gw_ref[...] += _dot_tn(u_ref[...], dzc_ref[0])

        @pl.when(j >= 3)
        def _():
            gw_ref[...] += _dot_tn(u_ref[...], dzh_ref[0])

    blk = 3 * _nbytes((tk, D), _MXU) + 2 * _nbytes((D, D), F32)
    return pl.pallas_call(
        body, name="inproj_bwd_w", grid=(NPART, nK),
        in_specs=[pl.BlockSpec((tk, D), lambda j, k: (k, 0)),
                  pl.BlockSpec((1, tk, D), lambda j, k: (jnp.minimum(j, 2), jnp.where(j < 3, k, nK - 1), 0)),
                  pl.BlockSpec((1, tk, D), lambda j, k: (jnp.maximum(j - 3, 0), jnp.where(j < 3, 0, k), 0))],
        out_specs=pl.BlockSpec((D, D), lambda j, k: (0, j)),
        out_shape=jax.ShapeDtypeStruct((D, NPART * D), F32),
        compiler_params=_cparams(blk, 2),
    )(u, dzc, dzh)


def _tn_matmul(a, b, tk, name):
    T, M = a.shape
    N = b.shape[1]

    def body(a_ref, b_ref, o_ref):
        @pl.when(pl.program_id(0) == 0)
        def _():
            o_ref[...] = jnp.zeros_like(o_ref)
        o_ref[...] += _dot_tn(a_ref[...], b_ref[...])

    blk = _nbytes((tk, M), _MXU) + _nbytes((tk, N), _MXU) + 2 * _nbytes((M, N), F32)
    return pl.pallas_call(
        body, name=name, grid=(T // tk,),
        in_specs=[pl.BlockSpec((tk, M), lambda k: (k, 0)), pl.BlockSpec((tk, N), lambda k: (k, 0))],
        out_specs=pl.BlockSpec((M, N), lambda k: (0, 0)),
        out_shape=pltpu.HBM((M, N), F32),
        compiler_params=_cparams(blk, 1),
    )(a, b)


def _place():
    return lax.axis_index("x"), lax.axis_index("y"), lax.axis_index("c")


def _flip(v, d):
    return 1 - v if d else v


CHIP_MOVES = [(1, 0), (0, 1), (1, 1)]
DEV_MOVES = [(dx, dy, dc) for dx in (0, 1) for dy in (0, 1) for dc in (0, 1)][1:]


def _shard_slice(ref, axis, size, s):
    start = pl.multiple_of(s * size, size)
    return ref.at[pl.ds(start, size), :] if axis == 0 else ref.at[:, pl.ds(start, size)]


class _Bounce:
    def __init__(self, src, buf, dst, sem_in, sem_out):
        self.load = pltpu.make_async_copy(src, buf, sem_in)
        self.store = pltpu.make_async_copy(buf, dst, sem_out)

    def start(self):
        self.load.start()

    def turn(self):
        self.load.wait()
        self.store.start()

    def wait(self):
        self.store.wait()


def _comm_params(scratch_bytes):
    return pltpu.CompilerParams(vmem_limit_bytes=int(min(V7X_VMEM_LIMIT, scratch_bytes + (8 << 20))))


class _Gather:
    def __init__(self, shapes, axes, ins, outs, bufs, sems):
        self.shapes, self.axes, self.ins, self.outs, self.bufs = shapes, axes, ins, outs, bufs
        self.ici_send, self.ici_recv, self.d2d_send, self.d2d_recv, self.in_sems, self.out_sems = sems
        self.x, self.y, self.c = _place()
        self.me = 2 * self.x + self.y
        self.pairs = [(k, j) for k in range(len(shapes)) for j in range(3)]

    @staticmethod
    def scratch(shards):
        n = len(shards)
        return ([pltpu.VMEM(s.shape, s.dtype) for s in shards]
                + [pltpu.SemaphoreType.DMA((3 * n,))] * 4 + [pltpu.SemaphoreType.DMA((n,))] * 2)

    def _own_half(self, k, hc):
        half = self.shapes[k][0] // 2
        return self.ins[k].at[pl.ds(pl.multiple_of(hc * half, 16), half), :]

    def _region(self, k, who, hc):
        rows, cols = self.shapes[k]
        half = rows // 2
        if self.axes[k] == 0:
            return self.outs[k].at[pl.ds(pl.multiple_of(who * rows + hc * half, 16), half), :]
        return self.outs[k].at[pl.ds(pl.multiple_of(hc * half, 16), half), pl.ds(pl.multiple_of(who * cols, HD), cols)]

    def _peer(self, j):
        return 2 * _flip(self.x, CHIP_MOVES[j][0]) + _flip(self.y, CHIP_MOVES[j][1])

    def _ici(self, k, j, who, hc):
        dx, dy = CHIP_MOVES[j]
        return pltpu.make_async_remote_copy(
            src_ref=self._own_half(k, hc), dst_ref=self._region(k, who, hc),
            send_sem=self.ici_send.at[3 * k + j], recv_sem=self.ici_recv.at[3 * k + j],
            device_id=(_flip(self.x, dx), _flip(self.y, dy), self.c), device_id_type=MESH_ID)

    def _d2d(self, k, j, who, hc):
        return pltpu.make_async_remote_copy(
            src_ref=self._region(k, who, hc), dst_ref=self._region(k, who, hc),
            send_sem=self.d2d_send.at[3 * k + j], recv_sem=self.d2d_recv.at[3 * k + j],
            device_id=(self.x, self.y, 1 - self.c), device_id_type=MESH_ID)

    def _local(self, k):
        size = self.shapes[k][self.axes[k]]
        return _Bounce(self.ins[k], self.bufs[k], _shard_slice(self.outs[k], self.axes[k], size, self.me),
                       self.in_sems.at[k], self.out_sems.at[k])

    def start(self):
        for k in range(len(self.shapes)):
            self._local(k).start()
        for k, j in self.pairs:
            self._ici(k, j, self.me, self.c).start()

    def turn(self):
        for k in range(len(self.shapes)):
            self._local(k).turn()

    def forward(self):
        for k, j in self.pairs:
            self._ici(k, j, self._peer(j), self.c).wait_recv()
            self._d2d(k, j, self._peer(j), self.c).start()

    def finish(self):
        for k, j in self.pairs:
            self._d2d(k, j, self._peer(j), 1 - self.c).wait_recv()
        for k, j in self.pairs:
            self._ici(k, j, self.me, self.c).wait_send()
            self._d2d(k, j, self._peer(j), self.c).wait_send()
        for k in range(len(self.shapes)):
            self._local(k).wait()


def _full_shapes(shards, axes):
    return [tuple(d * (N_CHIPS if a == ax else 1) for a, d in enumerate(s.shape)) for s, ax in zip(shards, axes)]


class _Slab:
    def __init__(self, arrays, pick, shard_shape):
        self.arrays = arrays
        self.pick = pick
        self.rows, self.cols = shard_shape
        self.half = self.rows // 2


PAIR_SUM_ROWS = 64


def _pair_exchange_sum(slabs, name):
    n = len(slabs)
    n_in = sum(len(sl.arrays) for sl in slabs)

    def body(*refs):
        ins, outs = refs[:n_in], refs[n_in:n_in + n]
        mine, got, total = (refs[n_in + (1 + t) * n:n_in + (2 + t) * n] for t in range(3))
        send_sems, recv_sems, in_sems, out_sems = refs[n_in + 4 * n:]
        x, y, c = _place()
        started = []
        base = 0
        for k, sl in enumerate(slabs):
            for s in range(N_CHIPS):
                ai, r0, c0 = sl.pick(s)
                src = ins[base + ai]

                def half(hc):
                    return src.at[pl.ds(pl.multiple_of(r0 + hc * sl.half, 8), sl.half), pl.ds(c0, sl.cols)]
                q = N_CHIPS * k + s
                load = pltpu.make_async_copy(half(c), mine[k].at[s], in_sems.at[q])
                load.start()
                cp = pltpu.make_async_remote_copy(
                    src_ref=half(1 - c), dst_ref=got[k].at[s], send_sem=send_sems.at[q], recv_sem=recv_sems.at[q],
                    device_id=(x, y, 1 - c), device_id_type=MESH_ID)
                cp.start()
                store = pltpu.make_async_copy(total[k].at[s], outs[k].at[s], out_sems.at[q])
                started.append((k, s, sl.half, load, cp, store))
            base += len(sl.arrays)
        for k, s, half_rows, load, cp, store in started:
            load.wait()
            cp.wait_recv()
            rows = min(half_rows, PAIR_SUM_ROWS)

            def add(t, carry, k=k, s=s, rows=rows):
                sl_ = pl.ds(pl.multiple_of(t * rows, rows), rows)
                total[k][s, sl_, :] = (mine[k][s, sl_, :] + got[k][s, sl_, :]).astype(_WIRE)
                return carry
            lax.fori_loop(0, half_rows // rows, add, 0)
            store.start()
        for k, s, half_rows, load, cp, store in started:
            cp.wait_send()
            store.wait()

    flat_in = [a for sl in slabs for a in sl.arrays]
    shapes = [(N_CHIPS, sl.half, sl.cols) for sl in slabs]
    vmem = [pltpu.VMEM(sh, dt) for dt in (F32, F32, _WIRE) for sh in shapes]
    return pl.pallas_call(
        body, name=name,
        in_specs=[ANY] * n_in, out_specs=[ANY] * n, out_shape=[pltpu.HBM(sh, _WIRE) for sh in shapes],
        scratch_shapes=vmem + [pltpu.SemaphoreType.DMA((N_CHIPS * n,))] * 4,
        compiler_params=_comm_params(sum(_nbytes(sh, F32) * 2 + _nbytes(sh, _WIRE) for sh in shapes)),
    )(*_in_hbm(flat_in))


class _ChipExchange:
    def __init__(self, n, ins, outs, bufs, sems):
        self.n, self.ins, self.outs, self.bufs = n, ins, outs, bufs
        self.send_sems, self.recv_sems, self.in_sems, self.out_sems = sems
        self.x, self.y, self.c = _place()
        self.me = 2 * self.x + self.y
        self.pairs = [(k, j) for k in range(n) for j in range(3)]

    @staticmethod
    def scratch(partials):
        n = len(partials)
        return ([pltpu.VMEM(p.shape[1:], p.dtype) for p in partials]
                + [pltpu.SemaphoreType.DMA((3 * n,))] * 2 + [pltpu.SemaphoreType.DMA((n,))] * 2)

    @staticmethod
    def scratch_bytes(partials):
        return sum(_nbytes(p.shape[1:], p.dtype) for p in partials)

    def _copy(self, k, j, src_slot, dst_slot):
        px, py = _flip(self.x, CHIP_MOVES[j][0]), _flip(self.y, CHIP_MOVES[j][1])
        return pltpu.make_async_remote_copy(
            src_ref=self.ins[k].at[src_slot], dst_ref=self.outs[k].at[dst_slot],
            send_sem=self.send_sems.at[3 * k + j], recv_sem=self.recv_sems.at[3 * k + j],
            device_id=(px, py, self.c), device_id_type=MESH_ID)

    def _peer(self, j):
        return 2 * _flip(self.x, CHIP_MOVES[j][0]) + _flip(self.y, CHIP_MOVES[j][1])

    def _local(self, k):
        return _Bounce(self.ins[k].at[self.me], self.bufs[k], self.outs[k].at[self.me],
                       self.in_sems.at[k], self.out_sems.at[k])

    def start(self):
        for k in range(self.n):
            self._local(k).start()
        for k, j in self.pairs:
            self._copy(k, j, self._peer(j), self.me).start()

    def turn(self):
        for k in range(self.n):
            self._local(k).turn()

    def finish(self):
        for k, j in self.pairs:
            self._copy(k, j, self.me, self._peer(j)).wait_recv()
        for k, j in self.pairs:
            self._copy(k, j, self._peer(j), self.me).wait_send()
        for k in range(self.n):
            self._local(k).wait()


def _hosted_steps(steps):
    return dict(start=0, turn=steps // 4, forward=steps // 2, finish=steps - 1)


def _pair_share(slots, vec):
    n = len(slots)
    nv = len(DEV_MOVES)

    def body(*refs):
        ins, vec_ref = refs[:n], refs[n]
        outs, vec_out = refs[n + 1:2 * n + 1], refs[2 * n + 1]
        slot_b, half_b, vec_b = refs[2 * n + 2:3 * n + 2], refs[3 * n + 2:4 * n + 2], refs[4 * n + 2]
        send_sems, recv_sems, in_sems, out_sems = refs[4 * n + 3:]
        x, y, c = _place()
        dev = 4 * x + 2 * y + c

        def vec_copy(j, slot):
            dx, dy, dc = DEV_MOVES[j]
            return pltpu.make_async_remote_copy(
                src_ref=vec_ref, dst_ref=vec_out.at[slot], send_sem=send_sems.at[n + j], recv_sem=recv_sems.at[n + j],
                device_id=(_flip(x, dx), _flip(y, dy), _flip(c, dc)), device_id_type=MESH_ID)

        def rows(k, hc):
            hr = slots[k].shape[1]
            return outs[k].at[pl.ds(pl.multiple_of(hc * hr, 8), hr), :]

        def share(k, hc):
            return pltpu.make_async_remote_copy(
                src_ref=half_b[k], dst_ref=rows(k, hc), send_sem=send_sems.at[k], recv_sem=recv_sems.at[k],
                device_id=(x, y, 1 - c), device_id_type=MESH_ID)

        vec_loc = _Bounce(vec_ref, vec_b, vec_out.at[dev], in_sems.at[n], out_sems.at[n])
        vec_loc.start()
        for j in range(nv):
            vec_copy(j, dev).start()
        loads = [pltpu.make_async_copy(ins[k], slot_b[k], in_sems.at[k]) for k in range(n)]
        stores = [pltpu.make_async_copy(half_b[k], rows(k, c), out_sems.at[k]) for k in range(n)]
        for load in loads:
            load.start()
        vec_loc.turn()
        for k in range(n):
            loads[k].wait()
            hr = slots[k].shape[1]
            step_rows = min(hr, PAIR_SUM_ROWS)

            def add(t, carry, k=k, step_rows=step_rows):
                sl_ = pl.ds(pl.multiple_of(t * step_rows, step_rows), step_rows)
                acc = slot_b[k][0, sl_, :].astype(F32)
                for s in range(1, N_CHIPS):
                    acc = acc + slot_b[k][s, sl_, :].astype(F32)
                half_b[k][sl_, :] = acc
                return carry
            lax.fori_loop(0, hr // step_rows, add, 0)
            stores[k].start()
            share(k, c).start()
        for k in range(n):
            share(k, 1 - c).wait_recv()
        for j, (dx, dy, dc) in enumerate(DEV_MOVES):
            vec_copy(j, 4 * _flip(x, dx) + 2 * _flip(y, dy) + _flip(c, dc)).wait_recv()
        for k in range(n):
            share(k, c).wait_send()
            stores[k].wait()
        for j in range(nv):
            vec_copy(j, dev).wait_send()
        vec_loc.wait()

    halves = [s.shape[1:] for s in slots]
    vmem = ([pltpu.VMEM(s.shape, s.dtype) for s in slots] + [pltpu.VMEM(h, F32) for h in halves]
            + [pltpu.VMEM(vec.shape, F32)])
    outs = pl.pallas_call(
        body, name="grad_pair_share",
        in_specs=[ANY] * (n + 1), out_specs=[ANY] * (n + 1),
        out_shape=[pltpu.HBM((2 * h[0], h[1]), F32) for h in halves] + [pltpu.HBM((N_DEV,) + vec.shape, F32)],
        scratch_shapes=vmem + [pltpu.SemaphoreType.DMA((n + nv,))] * 2 + [pltpu.SemaphoreType.DMA((n + 1,))] * 2,
        compiler_params=_comm_params(sum(_nbytes(s.shape, s.dtype) for s in slots) + sum(_nbytes(h, F32) for h in halves)
                                     + _nbytes(vec.shape, F32)),
    )(*_in_hbm(list(slots) + [vec]))
    return outs[:n], outs[n]


def _row_block(rows, cols, n_arrays):
    br = rows
    while br % 16 == 0 and 2 * n_arrays * br * cols * 4 > (16 << 20):
        br //= 2
    return br


def _sum_slots(a, name):
    n, rows, cols = a.shape
    br = _row_block(rows, cols, n + 1)

    def body(a_ref, o_ref):
        acc = a_ref[0].astype(F32)
        for s in range(1, n):
            acc = acc + a_ref[s].astype(F32)
        o_ref[...] = acc

    return pl.pallas_call(body, name=name, grid=(rows // br,),
                          in_specs=[pl.BlockSpec((n, br, cols), lambda i: (0, i, 0))],
                          out_specs=pl.BlockSpec((br, cols), lambda i: (i, 0)),
                          out_shape=pltpu.HBM((rows, cols), F32),
                          compiler_params=_cparams((n + 1) * br * cols * 4, 1))(*_in_hbm([a]))


def _adamw_math(w, g, m, v):
    m = ADAM_B1 * m + (1.0 - ADAM_B1) * g
    v = ADAM_B2 * v + (1.0 - ADAM_B2) * (g * g)
    m_hat = m / (1.0 - ADAM_B1 ** ADAM_STEP)
    v_hat = v / (1.0 - ADAM_B2 ** ADAM_STEP)
    delta = -ADAM_LR * (m_hat / (jnp.sqrt(v_hat) + ADAM_EPS) + ADAM_WD * w)
    return delta, m, v


def _adamw(g, w, m, v, name):
    rows, cols = g.shape
    br = _row_block(rows, cols, 7)

    def body(g_ref, w_ref, m_ref, v_ref, d_ref, nm_ref, nv_ref):
        d_ref[...], nm_ref[...], nv_ref[...] = _adamw_math(w_ref[...], g_ref[...], m_ref[...], v_ref[...])

    spec = pl.BlockSpec((br, cols), lambda i: (i, 0))
    return pl.pallas_call(body, name=name, grid=(rows // br,), in_specs=[spec] * 4, out_specs=[spec] * 3,
                          out_shape=[jax.ShapeDtypeStruct(g.shape, F32)] * 3,
                          compiler_params=_cparams(7 * br * cols * 4, 1))(g, w, m, v)


ROW_FINAL_G, ROW_PE_G, ROW_LOSS = 0, 1, 2
ROW_CONV_B, ROW_CN_G, ROW_CN_B, ROW_B_PW2 = 8, 9, 10, 11
ROW_LN_G = 16
ROW_ONORM_G, ROW_LB = 24, 25
ROW_CONV_W = 32
SMALL = ["ln_g", "conv_b", "cnorm_g", "cnorm_b", "b_pw2", "onorm_g", "pe_norm_g", "final_g"]
SMALL_ROW = dict(ln_g=ROW_LN_G, conv_b=ROW_CONV_B, cnorm_g=ROW_CN_G, cnorm_b=ROW_CN_B, b_pw2=ROW_B_PW2,
                 onorm_g=ROW_ONORM_G, pe_norm_g=ROW_PE_G, final_g=ROW_FINAL_G)


def _adamw_small(vsum, gcw, lb_logits, params):
    names = SMALL + ["lb_logits", "conv_w"]
    flat = [t for nm in names for t in params[nm]]

    def body(*refs):
        vs_ref, gcw_ref, lbl_ref = refs[:3]
        ins = refs[3:3 + 3 * len(names)]
        outs = refs[3 + 3 * len(names):]
        for q, nm in enumerate(names):
            w_ref, m_ref, v_ref = ins[3 * q:3 * q + 3]
            g_ref, d_ref, nm_ref, nv_ref = outs[4 * q:4 * q + 4]
            if nm == "conv_w":
                g = gcw_ref[...]
            elif nm == "lb_logits":
                lb = _softmax_row0(lbl_ref[...])
                g0 = vs_ref[ROW_LB:ROW_LB + 1, :] * lb * (1.0 - lb)
                g = jnp.concatenate([g0, -g0], axis=0)
            else:
                g = vs_ref[SMALL_ROW[nm]:SMALL_ROW[nm] + 1, :]
            g_ref[...] = g
            d_ref[...], nm_ref[...], nv_ref[...] = _adamw_math(w_ref[...], g, m_ref[...], v_ref[...])

    out_shape = [jax.ShapeDtypeStruct(params[nm][0].shape, F32) for nm in names for _ in range(4)]
    outs = pl.pallas_call(body, name="adamw_small", out_shape=out_shape)(vsum, gcw, lb_logits, *flat)
    return {nm: tuple(outs[4 * q:4 * q + 4]) for q, nm in enumerate(names)}


TOKEN_TILE = dict(rmsnorm=512, inproj_fwd=2048, conv=256, hgrn=512, tail=512, inproj_bwd_u=2048, inproj_bwd_x=512,
                  weight_grad=2048, small_weight_grad=1024)


def _tile(T, family):
    return min(T, TOKEN_TILE[family])


def kernel(x, p, ln_g, w_in, conv_w, conv_b, cnorm_g, cnorm_b, w_pw2, b_pw2, lb_logits, onorm_g, w_out, pe_norm_g, w_pg, w_pp, final_g, loss_target, m_ln_g, m_w_in, m_conv_w, m_conv_b, m_cnorm_g, m_cnorm_b, m_w_pw2, m_b_pw2, m_lb_logits, m_onorm_g, m_w_out, m_pe_norm_g, m_w_pg, m_w_pp, m_final_g, v_ln_g, v_w_in, v_conv_w, v_conv_b, v_cnorm_g, v_cnorm_b, v_w_pw2, v_b_pw2, v_lb_logits, v_onorm_g, v_w_out, v_pe_norm_g, v_w_pg, v_w_pp, v_final_g):
    given = dict(locals())
    x2, p2, tgt = x[0], p[0, 0], loss_target[0]
    T = x2.shape[0]
    fin_g = final_g.reshape(1, D)

    conv_w_pad = jnp.pad(conv_w[0], ((0, HALO - CONV_K), (0, 0)))
    u, (w_in_f,) = _rmsnorm_gather(x2, ln_g, [w_in[0].astype(_MXU)], [1], _tile(T, "rmsnorm"))

    z, (w_pw2_f, w_out_f, w_pg_f, w_pp_f, conv_w_f) = _inproj_fwd(
        u, w_in_f,
        [w_pw2[0].astype(_MXU), w_out[0].astype(_MXU), w_pg[0].astype(_MXU), w_pp[0].astype(_MXU), conv_w_pad],
        [0, 0, 0, 1, 1], _tile(T, "inproj_fwd"))
    yc, y1 = _conv_fwd(z, conv_w_f, conv_b, cnorm_g, cnorm_b, w_pw2_f, b_pw2, _tile(T, "conv"))
    o_raw, yh, s_chunks = _hgrn_fwd(z, lb_logits, onorm_g, _tile(T, "hgrn"), HB)
    dyc, dyh, n2, ds, dpe, dhb, pb, vec_tail = _tail(
        x2, yc, yh, p2, tgt, w_out_f, w_pg_f, w_pp_f, pe_norm_g, fin_g, _tile(T, "tail"))
    tk = _tile(T, "small_weight_grad")
    g_w_out_c = _tn_matmul(yc, dhb, tk, "grad_w_out_conv")
    g_w_out_h = _tn_matmul(yh, dhb, tk, "grad_w_out_hgrn")
    g_w_pg = _tn_matmul(n2, ds, tk, "grad_w_pg")
    g_w_pp = _tn_matmul(pb, dpe, tk, "grad_w_pp")
    dzc, a_act, dy2, vec_conv, g_conv_w = _conv_bwd(z, y1, dyc, conv_w_f, cnorm_g, cnorm_b, w_pw2_f, b_pw2, _tile(T, "conv"))
    g_w_pw2 = _tn_matmul(a_act, dy2, tk, "grad_w_pw2")

    rest = ["w_pw2", "w_out", "w_pg", "w_pp"]
    partial_rest = _pair_exchange_sum([
        _Slab([g_w_pw2], lambda s: (0, s * (D // N_CHIPS), 0), (D // N_CHIPS, D)),
        _Slab([g_w_out_c, g_w_out_h], lambda s: (s // 2, (s % 2) * (D // 2), 0), (D // 2, D)),
        _Slab([g_w_pg], lambda s: (0, s * (D // N_CHIPS), 0), (D // N_CHIPS, D)),
        _Slab([g_w_pp], lambda s: (0, 0, s * (D // N_CHIPS)), (PLE, D // N_CHIPS)),
    ], "grad_pair_exchange_rest")
    dzh, vec_hgrn, slots_rest = _hgrn_bwd(z, lb_logits, onorm_g, o_raw, dyh, s_chunks, partial_rest, _tile(T, "hgrn"), HB)
    g_w_in = _inproj_bwd_w(u, dzc, dzh, _tile(T, "weight_grad"))
    partial_in = _pair_exchange_sum([
        _Slab([g_w_in], lambda s: (0, 0, s * (NPART * D // N_CHIPS)), (D, NPART * D // N_CHIPS))], "grad_pair_exchange_w_in")
    du, slots_in = _inproj_bwd_u(dzc, dzh, w_in_f, partial_in, _tile(T, "inproj_bwd_u"))
    grad_x, vec_in = _inproj_bwd_x(x2, ln_g, du, dhb, _tile(T, "inproj_bwd_x"))
    big = ["w_in"] + rest
    vec = jnp.concatenate([vec_tail, vec_conv, vec_in, vec_hgrn, g_conv_w], axis=0)
    grads_big, vec_slots = _pair_share(list(slots_in) + list(slots_rest), vec)
    vsum = _sum_slots(vec_slots, "vec_sum")

    out = {}
    for nm, g in zip(big, grads_big):
        w2, m2, v2 = given[nm][0], given["m_" + nm][0], given["v_" + nm][0]
        d, nm_, nv_ = _adamw(g, w2, m2, v2, "adamw_" + nm)
        out[nm] = tuple(t[None] for t in (g, d, nm_, nv_))
    chip = 2 * lax.axis_index("x") + lax.axis_index("y")
    gcw = lax.dynamic_slice(vsum, (ROW_CONV_W, chip * (D // N_CHIPS)), (CONV_K, D // N_CHIPS))
    params = {nm: (given[nm].reshape(-1, D), given["m_" + nm].reshape(-1, D), given["v_" + nm].reshape(-1, D))
              for nm in SMALL + ["lb_logits"]}
    params["conv_w"] = (conv_w[0], m_conv_w[0], v_conv_w[0])
    small = _adamw_small(vsum, gcw, lb_logits, params)
    for nm, ts in small.items():
        out[nm] = tuple(t.reshape(given[nm].shape) for t in ts)

    loss = vsum[ROW_LOSS, 0]
    order = ["ln_g", "w_in", "conv_w", "conv_b", "cnorm_g", "cnorm_b", "w_pw2", "b_pw2", "lb_logits", "onorm_g",
             "w_out", "pe_norm_g", "w_pg", "w_pp", "final_g"]
    return (loss, grad_x[None], *[out[nm][0] for nm in order], *[out[nm][1] for nm in order],
            *[out[nm][2] for nm in order], *[out[nm][3] for nm in order])
```

```python
import functools

import jax
import jax.numpy as jnp
from jax import lax
from jax.experimental import pallas as pl
from jax.experimental.pallas import tpu as pltpu

F32 = jnp.float32
BF16 = jnp.bfloat16
_MXU = jnp.bfloat16
_WIRE = jnp.bfloat16

D = 1024
NPART = 7
PLE = 256
HEADS = 8
HD = 128
CHUNK = 64
CONV_K = 31
HALO = 32
EPS = 1e-6
N_CHIPS = 4
N_DEV = 8
HB = 8
VEC_ROWS = 64

ADAM_LR = 0.001
ADAM_B1 = 0.9
ADAM_B2 = 0.999
ADAM_EPS = 1e-08
ADAM_WD = 0.01
ADAM_STEP = 10

V7X_VMEM_LIMIT = 60000 * 1024
MESH_ID = pl.DeviceIdType.MESH
ANY = pl.BlockSpec(memory_space=pltpu.HBM)


def _in_hbm(arrays):
    return [pltpu.with_memory_space_constraint(a, pltpu.HBM) for a in arrays]


def _cparams(block_bytes, n_grid_dims):
    limit = min(V7X_VMEM_LIMIT, 2 * block_bytes + (24 << 20))
    return pltpu.CompilerParams(vmem_limit_bytes=int(limit), dimension_semantics=("arbitrary",) * n_grid_dims)


def _nbytes(shape, dtype):
    n = 1
    for s in shape:
        n *= s
    return n * jnp.dtype(dtype).itemsize


def _dot(a, b):
    return jnp.dot(a.astype(_MXU), b.astype(_MXU), preferred_element_type=F32)


def _dot_nt(a, b):
    return lax.dot_general(a.astype(_MXU), b.astype(_MXU), (((1,), (1,)), ((), ())), preferred_element_type=F32)


def _dot_tn(a, b):
    return lax.dot_general(a.astype(_MXU), b.astype(_MXU), (((0,), (0,)), ((), ())), preferred_element_type=F32)


def _tri_dot(tri_bf, x):
    x1 = x.astype(BF16)
    r1 = x - x1.astype(F32)
    x2 = r1.astype(BF16)
    x3 = (r1 - x2.astype(F32)).astype(BF16)
    d = lambda t: jnp.dot(tri_bf, t, preferred_element_type=F32)
    return d(x1) + d(x2) + d(x3)


def _split2(x):
    hi = x.astype(BF16)
    return hi, (x - hi.astype(F32)).astype(BF16)


def _dot3(dims, a, b):
    d = lambda p, q: lax.dot_general(p, q, (dims, ((), ())), preferred_element_type=F32)
    return d(a[0], b[0]) + d(a[0], b[1]) + d(a[1], b[0])


def _sigmoid(x):
    return jax.nn.sigmoid(x)


def _mean_lanes(x):
    return jnp.mean(x, axis=-1, keepdims=True)


def _sum_rows(x):
    return jnp.sum(x, axis=0, keepdims=True)


def _group_ln(y):
    yn, rs = [], []
    for g in range(D // HD):
        blk = y[:, g * HD:(g + 1) * HD]
        xc = blk - _mean_lanes(blk)
        r = lax.rsqrt(_mean_lanes(xc * xc) + EPS)
        yn.append(xc * r)
        rs.append(jnp.broadcast_to(r, blk.shape))
    return jnp.concatenate(yn, axis=1), jnp.concatenate(rs, axis=1)


def _group_ln_bwd(dyn, yn, rstd):
    out = []
    for g in range(D // HD):
        sl = slice(g * HD, (g + 1) * HD)
        d, n = dyn[:, sl], yn[:, sl]
        out.append(rstd[:, sl] * (d - _mean_lanes(d) - n * _mean_lanes(d * n)))
    return jnp.concatenate(out, axis=1)


def _head_means(x, hb, fn=lambda m: m):
    return jnp.concatenate([jnp.broadcast_to(fn(_mean_lanes(x[:, hh * HD:(hh + 1) * HD])), (x.shape[0], HD))
                            for hh in range(hb)], axis=1)


def _head_rsqrt_mean(x, hb):
    return _head_means(x, hb, lambda m: lax.rsqrt(m + EPS))


def _softmax_row0(lbl):
    m = jnp.max(lbl, axis=0, keepdims=True)
    e = jnp.exp(lbl - m)
    return e[0:1, :] / jnp.sum(e, axis=0, keepdims=True)


def _hosted_gather(phases, step, at, shards, axes, ins, outs, bufs, sems):
    gather = _Gather([s.shape for s in shards], axes, ins, outs, bufs, sems)
    for phase in phases:
        pl.when(step == at[phase])(getattr(gather, phase))


def _rmsnorm_gather(x, ln_g, shards, axes, tT):
    T = x.shape[0]
    n = len(shards)
    at = _hosted_steps(T // tT)

    def body(x_ref, g_ref, *rest):
        ins, u_ref, outs, bufs, sems = rest[:n], rest[n], rest[n + 1:2 * n + 1], rest[2 * n + 1:3 * n + 1], rest[3 * n + 1:]
        host = functools.partial(_hosted_gather, step=pl.program_id(0), at=at, shards=shards, axes=axes,
                                 ins=ins, outs=outs, bufs=bufs, sems=sems)
        host(("start", "turn", "forward"))
        xv = x_ref[...]
        r = lax.rsqrt(_mean_lanes(xv * xv) + EPS)
        u_ref[...] = (xv * r * g_ref[...]).astype(_MXU)
        host(("finish",))

    blk = _nbytes((tT, D), F32) * 2 + _nbytes((tT, D), _MXU) + sum(_nbytes(s.shape, s.dtype) for s in shards)
    outs = pl.pallas_call(
        body, name="rmsnorm_gather", grid=(T // tT,),
        in_specs=[pl.BlockSpec((tT, D), lambda i: (i, 0)), pl.BlockSpec((1, D), lambda i: (0, 0))] + [ANY] * n,
        out_specs=[pl.BlockSpec((tT, D), lambda i: (i, 0))] + [ANY] * n,
        out_shape=[jax.ShapeDtypeStruct((T, D), _MXU)]
        + [pltpu.HBM(fs, s.dtype) for fs, s in zip(_full_shapes(shards, axes), shards)],
        scratch_shapes=_Gather.scratch(shards),
        compiler_params=_cparams(blk, 1),
    )(x, ln_g, *_in_hbm(shards))
    return outs[0], outs[1:]


def _inproj_fwd(u, w_in, shards, axes, tT):
    T = u.shape[0]
    n = len(shards)
    at = _hosted_steps((T // tT) * NPART)

    def body(u_ref, w_ref, *rest):
        ins, z_ref, outs, bufs, sems = rest[:n], rest[n], rest[n + 1:2 * n + 1], rest[2 * n + 1:3 * n + 1], rest[3 * n + 1:]
        host = functools.partial(_hosted_gather, step=pl.program_id(0) * NPART + pl.program_id(1), at=at, shards=shards,
                                 axes=axes, ins=ins, outs=outs, bufs=bufs, sems=sems)
        host(("start", "turn", "forward"))
        z_ref[...] = jnp.dot(u_ref[...], w_ref[...], preferred_element_type=F32)
        host(("finish",))

    blk = (_nbytes((tT, D), F32) + _nbytes((D, D), _MXU) + _nbytes((tT, D), _MXU)
           + sum(_nbytes(s.shape, s.dtype) for s in shards))
    outs = pl.pallas_call(
        body, name="inproj_fwd", grid=(T // tT, NPART),
        in_specs=[pl.BlockSpec((tT, D), lambda i, j: (i, 0)), pl.BlockSpec((D, D), lambda i, j: (0, j))] + [ANY] * n,
        out_specs=[pl.BlockSpec((tT, D), lambda i, j: (i, j))] + [ANY] * n,
        out_shape=[jax.ShapeDtypeStruct((T, NPART * D), F32)]
        + [pltpu.HBM(fs, s.dtype) for fs, s in zip(_full_shapes(shards, axes), shards)],
        scratch_shapes=_Gather.scratch(shards),
        compiler_params=_cparams(blk, 2),
    )(u, w_in, *_in_hbm(shards))
    return outs[0], outs[1:]


def _shifted_windows(ext, first, visit):
    n = ext.shape[0]
    for m in range(first, first + CONV_K):
        visit(m, (ext if m == 0 else pltpu.roll(ext, n - m, axis=0))[0:n - HALO, :])


def _conv_fwd(z, conv_w, conv_b, cn_g, cn_b, w_pw2, b_pw2, tT):
    T = z.shape[0]

    def body(cv_ref, cg_ref, ct_ref, cw_ref, cb_ref, ng_ref, nb_ref, wp_ref, bp_ref, yc_ref, y1_ref, ext):
        @pl.when(pl.program_id(0) == 0)
        def _():
            ext[...] = jnp.zeros_like(ext)
        ext[0:HALO, :] = ext[tT:tT + HALO, :]
        ext[HALO:, :] = cv_ref[...] * _sigmoid(cg_ref[...])
        cw = cw_ref[...]
        acc = [cb_ref[...]]

        def tap(m, win):
            acc[0] = acc[0] + win * cw[m - 2:m - 1, :]
        _shifted_windows(ext[...], 2, tap)
        y1 = acc[0]
        y1_ref[...] = y1
        yn, _ = _group_ln(y1)
        apre = yn * ng_ref[...] + nb_ref[...]
        a = apre * _sigmoid(apre)
        y2 = _dot(a, wp_ref[...]) + bp_ref[...]
        ct = ct_ref[...]
        yc_ref[...] = (y2 * (ct * _sigmoid(ct))).astype(_MXU)

    part = lambda p: pl.BlockSpec((tT, D), lambda i: (i, p))
    row = pl.BlockSpec((1, D), lambda i: (0, 0))
    tok = pl.BlockSpec((tT, D), lambda i: (i, 0))
    blk = 4 * _nbytes((tT, D), F32) + _nbytes((D, D), _MXU) + _nbytes((tT, D), _MXU) + 8 * _nbytes((tT + HALO, D), F32)
    return pl.pallas_call(
        body, name="conv_fwd", grid=(T // tT,),
        in_specs=[part(0), part(1), part(2), pl.BlockSpec((HALO, D), lambda i: (0, 0)), row, row, row,
                  pl.BlockSpec((D, D), lambda i: (0, 0)), row],
        out_specs=[tok, tok],
        out_shape=[jax.ShapeDtypeStruct((T, D), _MXU), jax.ShapeDtypeStruct((T, D), F32)],
        scratch_shapes=[pltpu.VMEM((tT + HALO, D), F32)],
        compiler_params=_cparams(blk, 1),
    )(z, z, z, conv_w, conv_b, cn_g, cn_b, w_pw2, b_pw2)


def _hgrn_gates(lb, hq, hf):
    sq = _sigmoid(hq)
    sg = _sigmoid(hf)
    f = lb + (1.0 - lb) * sg
    return sq, sg, f, hq * sq, (1.0 - lb) * (1.0 - sg), jnp.log(f)


def _chunk_decays(lf, q, k):
    r = lax.broadcasted_iota(jnp.int32, (CHUNK, CHUNK), 0)
    c = lax.broadcasted_iota(jnp.int32, (CHUNK, CHUNK), 1)
    b = _tri_dot((r >= c).astype(BF16), lf)
    bm = b[CHUNK // 2 - 1:CHUNK // 2, :]
    bl = b[CHUNK - 1:CHUNK, :]
    eb = jnp.exp(b)
    eqm = jnp.exp(b - bm)
    ekm = jnp.exp(bm - b)
    ekd = jnp.exp(bl - b)
    return dict(causal=r >= c, eb=eb, eqm=eqm, ekm=ekm, ekd=ekd, ebl=jnp.exp(bl),
                qd=q * eb, qm=q * eqm, km=k * ekm, kd=k * ekd)


def _hgrn_fwd(z, lb_logits, onorm_g, tT, hb):
    T = z.shape[0]
    nc = tT // CHUNK
    w = hb * HD

    def body(lbl_ref, og_ref, hq_ref, hf_ref, hi_ref, hg_ref, o_ref, yh_ref, sc_ref, st):
        @pl.when(pl.program_id(1) == 0)
        def _():
            st[...] = jnp.zeros_like(st)
        lb_all = _softmax_row0(lbl_ref[...])
        og_all = og_ref[...]

        def chunk(c, carry):
            sl = pl.ds(pl.multiple_of(c * CHUNK, CHUNK), CHUNK)
            lanes = [slice(hh * HD, (hh + 1) * HD) for hh in range(hb)]
            heads = lambda fn: [fn(hh, ln) for hh, ln in enumerate(lanes)]
            hg, v = hg_ref[sl, :], hi_ref[sl, :]
            _, _, _, q, k, lf = _hgrn_gates(lb_all, hq_ref[sl, :], hf_ref[sl, :])
            dc = _chunk_decays(lf, q, k)
            s_t = heads(lambda hh, ln: st[hh])
            a = heads(lambda hh, ln: jnp.where(dc["causal"], _dot_nt(dc["qm"][:, ln], dc["km"][:, ln]), 0.0))
            o_inter = heads(lambda hh, ln: _dot_nt(dc["qd"][:, ln], s_t[hh]))
            kv = heads(lambda hh, ln: _dot_tn(v[:, ln], dc["kd"][:, ln]))
            o_intra = heads(lambda hh, ln: _dot(a[hh], v[:, ln]))
            for hh, ln in enumerate(lanes):
                sc_ref[hh, c] = s_t[hh]
                st[hh] = s_t[hh] * dc["ebl"][:, ln] + kv[hh]
            o = jnp.concatenate([o_inter[hh] + o_intra[hh] for hh in range(hb)], axis=1)
            o_ref[sl, :] = o
            n = o * _head_rsqrt_mean(o * o, hb)
            yh_ref[sl, :] = ((n * og_all) * (hg * _sigmoid(hg))).astype(_MXU)
            return carry

        lax.fori_loop(0, nc, chunk, 0, unroll=8)

    zpart = lambda p: pl.BlockSpec((tT, w), lambda h, i: (i, p * (HEADS // hb) + h))
    blk = 6 * _nbytes((tT, w), F32) + _nbytes((hb, nc, HD, HD), F32)
    return pl.pallas_call(
        body, name="hgrn_fwd", grid=(HEADS // hb, T // tT),
        in_specs=[pl.BlockSpec((2, w), lambda h, i: (0, h)), pl.BlockSpec((1, w), lambda h, i: (0, h)),
                  zpart(3), zpart(4), zpart(5), zpart(6)],
        out_specs=[pl.BlockSpec((tT, w), lambda h, i: (i, h)), pl.BlockSpec((tT, w), lambda h, i: (i, h)),
                   pl.BlockSpec((hb, nc, HD, HD), lambda h, i: (h, i, 0, 0))],
        out_shape=[jax.ShapeDtypeStruct((T, D), F32), jax.ShapeDtypeStruct((T, D), _MXU),
                   jax.ShapeDtypeStruct((HEADS, T // CHUNK, HD, HD), F32)],
        scratch_shapes=[pltpu.VMEM((hb, HD, HD), F32)],
        compiler_params=_cparams(blk, 2),
    )(lb_logits, onorm_g, z, z, z, z)


def _hgrn_bwd(z, lb_logits, onorm_g, o_raw, dyh, s_chunks, partials, tT, hb):
    T = z.shape[0]
    nc = tT // CHUNK
    nI = T // tT
    w = hb * HD
    n = len(partials)
    at = _hosted_steps((HEADS // hb) * nI)

    def body(lbl_ref, og_ref, hq_ref, hf_ref, hi_ref, hg_ref, o_ref, dy_ref, sc_ref, *rest):
        (dz_ref, vec_ref), dst = rest[n:n + 2], rest[2 * n + 2]
        exchange = _ChipExchange(n, rest[:n], rest[n + 2:2 * n + 2], rest[2 * n + 3:3 * n + 3], rest[3 * n + 3:])
        step = pl.program_id(0) * nI + pl.program_id(1)
        pl.when(step == at["start"])(exchange.start)
        pl.when(step == at["turn"])(exchange.turn)

        @pl.when(pl.program_id(1) == 0)
        def _():
            dst[...] = jnp.zeros_like(dst)
            vec_ref[...] = jnp.zeros_like(vec_ref)
        lb_all = _softmax_row0(lbl_ref[...])
        og_all = og_ref[...]
        last_row = lax.broadcasted_iota(jnp.int32, (CHUNK, w), 0) == CHUNK - 1
        r64 = lax.broadcasted_iota(jnp.int32, (CHUNK, CHUNK), 0)
        c64 = lax.broadcasted_iota(jnp.int32, (CHUNK, CHUNK), 1)
        upper = (c64 >= r64).astype(BF16)
        lanes = [slice(hh * HD, (hh + 1) * HD) for hh in range(hb)]
        heads = lambda fn: [fn(hh, ln) for hh, ln in enumerate(lanes)]
        wide = lambda parts: jnp.concatenate(parts, axis=1)

        def chunk(cc, carry):
            c = nc - 1 - cc
            sl = pl.ds(pl.multiple_of(c * CHUNK, CHUNK), CHUNK)
            hq, hg, v = hq_ref[sl, :], hg_ref[sl, :], hi_ref[sl, :]
            sq, sg, f, q, k, lf = _hgrn_gates(lb_all, hq, hf_ref[sl, :])
            dc = _chunk_decays(lf, q, k)
            s_t = heads(lambda hh, ln: sc_ref[hh, c])
            ds_t = heads(lambda hh, ln: dst[hh])
            o, dy = o_ref[sl, :], dy_ref[sl, :]
            r = _head_rsqrt_mean(o * o, hb)
            n = o * r
            sgg = _sigmoid(hg)
            silu_g = hg * sgg
            dhg = dy * (n * og_all) * (sgg * (1.0 + hg * (1.0 - sgg)))
            dn = dy * og_all * silu_g
            g_og = _sum_rows(dy * n * silu_g)
            do = r * (dn - n * _head_means(dn * n, hb))
            a = heads(lambda hh, ln: jnp.where(dc["causal"], _dot_nt(dc["qm"][:, ln], dc["km"][:, ln]), 0.0))
            dam = heads(lambda hh, ln: jnp.where(dc["causal"], _dot_nt(do[:, ln], v[:, ln]), 0.0))
            dqd = wide(heads(lambda hh, ln: _dot(do[:, ln], s_t[hh])))
            dkd = wide(heads(lambda hh, ln: _dot(v[:, ln], ds_t[hh])))
            dv_inter = heads(lambda hh, ln: _dot_nt(dc["kd"][:, ln], ds_t[hh]))
            dqs = heads(lambda hh, ln: _dot_tn(do[:, ln], dc["qd"][:, ln]))
            dv = wide(heads(lambda hh, ln: _dot_tn(a[hh], do[:, ln]) + dv_inter[hh]))
            dam2 = [_split2(t) for t in dam]
            km2, qm2 = _split2(dc["km"]), _split2(dc["qm"])
            dqm = wide(heads(lambda hh, ln: _dot3(((1,), (0,)), dam2[hh], (km2[0][:, ln], km2[1][:, ln]))))
            dkm = wide(heads(lambda hh, ln: _dot3(((0,), (0,)), dam2[hh], (qm2[0][:, ln], qm2[1][:, ln]))))
            debl = wide(heads(lambda hh, ln: _sum_rows(ds_t[hh] * s_t[hh])))
            for hh, ln in enumerate(lanes):
                dst[hh] = ds_t[hh] * dc["ebl"][:, ln] + dqs[hh]
            dq = dqd * dc["eb"] + dqm * dc["eqm"]
            dk = dkm * dc["ekm"] + dkd * dc["ekd"]
            dbl = _sum_rows(dkd * dc["kd"]) + debl * dc["ebl"]
            db = dq * q - dk * k + jnp.where(last_row, dbl, 0.0)
            dlf = _tri_dot(upper, db)
            dfk = dlf / f - dk
            dz_ref[0, sl, :] = (dq * (sq * (1.0 + hq * (1.0 - sq)))).astype(_MXU)
            dz_ref[1, sl, :] = (dfk * ((1.0 - lb_all) * sg * (1.0 - sg))).astype(_MXU)
            dz_ref[2, sl, :] = dv.astype(_MXU)
            dz_ref[3, sl, :] = dhg.astype(_MXU)
            vec_ref[0:1, :] += g_og
            vec_ref[1:2, :] += _sum_rows(dfk * (1.0 - sg))
            return carry

        lax.fori_loop(0, nc, chunk, 0, unroll=8)
        pl.when(step == at["finish"])(exchange.finish)

    zpart = lambda p: pl.BlockSpec((tT, w), lambda h, i: (nI - 1 - i, p * (HEADS // hb) + h))
    act = pl.BlockSpec((tT, w), lambda h, i: (nI - 1 - i, h))
    blk = (6 * _nbytes((tT, w), F32) + _nbytes((hb, nc, HD, HD), F32) + 4 * _nbytes((tT, w), _MXU)
           + _ChipExchange.scratch_bytes(partials))
    outs = pl.pallas_call(
        body, name="hgrn_bwd", grid=(HEADS // hb, nI),
        in_specs=[pl.BlockSpec((2, w), lambda h, i: (0, h)), pl.BlockSpec((1, w), lambda h, i: (0, h)),
                  zpart(3), zpart(4), zpart(5), zpart(6), act, act,
                  pl.BlockSpec((hb, nc, HD, HD), lambda h, i: (h, nI - 1 - i, 0, 0))] + [ANY] * n,
        out_specs=[pl.BlockSpec((4, tT, w), lambda h, i: (0, nI - 1 - i, h)),
                   pl.BlockSpec((8, w), lambda h, i: (0, h))] + [ANY] * n,
        out_shape=[jax.ShapeDtypeStruct((4, T, D), _MXU), jax.ShapeDtypeStruct((8, D), F32)]
        + [pltpu.HBM(p.shape, p.dtype) for p in partials],
        scratch_shapes=[pltpu.VMEM((hb, HD, HD), F32)] + _ChipExchange.scratch(partials),
        compiler_params=_cparams(blk, 2),
    )(lb_logits, onorm_g, z, z, z, z, o_raw, dyh, s_chunks, *_in_hbm(partials))
    return outs[0], outs[1], outs[2:]


def _tail(x, yc, yh, p, target, w_out, w_pg, w_pp, pe_g, fin_g, tT):
    T = x.shape[0]

    def body(x_ref, yc_ref, yh_ref, p_ref, t_ref, wo_ref, wg_ref, wp_ref, pg_ref, fg_ref,
             dyc_ref, dyh_ref, n2_ref, ds_ref, dpe_ref, dhb_ref, pb_ref, vec_ref):
        @pl.when(pl.program_id(0) == 0)
        def _():
            vec_ref[...] = jnp.zeros_like(vec_ref)
        wo_c, wo_h = wo_ref[0:D, :], wo_ref[D:2 * D, :]
        h = x_ref[...] + _dot(yc_ref[...], wo_c) + _dot(yh_ref[...], wo_h)
        pb = p_ref[...].astype(_MXU)
        pe = _dot(pb, wp_ref[...])
        r2 = lax.rsqrt(_mean_lanes(h * h) + EPS)
        hn = h * r2
        n2 = (hn * pg_ref[...]).astype(_MXU)
        gate = _sigmoid(_dot(n2, wg_ref[...]))
        h2 = h + gate * pe
        r3 = lax.rsqrt(_mean_lanes(h2 * h2) + EPS)
        h2n = h2 * r3
        err = h2n * fg_ref[...] - t_ref[...]
        vec_ref[ROW_LOSS:ROW_LOSS + 1, :] += 0.5 * jnp.sum(_mean_lanes(err * err))
        dout = err * (1.0 / D)
        vec_ref[0:1, :] += _sum_rows(dout * h2n)
        dn3 = dout * fg_ref[...]
        dh2 = r3 * (dn3 - h2n * _mean_lanes(dn3 * h2n))
        ds = (dh2 * pe * gate * (1.0 - gate)).astype(_MXU)
        dn2 = _dot_nt(ds, wg_ref[...])
        vec_ref[1:2, :] += _sum_rows(dn2 * hn)
        dnn = dn2 * pg_ref[...]
        dh = dh2 + r2 * (dnn - hn * _mean_lanes(dnn * hn))
        dhb = dh.astype(_MXU)
        dyc_ref[...] = _dot_nt(dhb, wo_c)
        dyh_ref[...] = _dot_nt(dhb, wo_h)
        n2_ref[...] = n2
        ds_ref[...] = ds
        dpe_ref[...] = (dh2 * gate).astype(_MXU)
        dhb_ref[...] = dhb
        pb_ref[...] = pb

    tok = lambda w: pl.BlockSpec((tT, w), lambda i: (i, 0))
    full = lambda r, c: pl.BlockSpec((r, c), lambda i: (0, 0))
    tokshape = lambda w, dt: jax.ShapeDtypeStruct((T, w), dt)
    blk = (5 * _nbytes((tT, D), F32) + 7 * _nbytes((tT, D), _MXU) + _nbytes((4 * D + PLE, D), _MXU)
           + 12 * _nbytes((tT, D), F32))
    return pl.pallas_call(
        body, name="tail_fwd_bwd", grid=(T // tT,),
        in_specs=[tok(D), tok(D), tok(D), tok(PLE), tok(D), full(2 * D, D), full(D, D), full(PLE, D), full(1, D), full(1, D)],
        out_specs=[tok(D), tok(D), tok(D), tok(D), tok(D), tok(D), tok(PLE), full(8, D)],
        out_shape=[tokshape(D, F32), tokshape(D, F32), tokshape(D, _MXU), tokshape(D, _MXU),
                   tokshape(D, _MXU), tokshape(D, _MXU), tokshape(PLE, _MXU),
                   jax.ShapeDtypeStruct((8, D), F32)],
        compiler_params=_cparams(blk, 1),
    )(x, yc, yh, p, target, w_out, w_pg, w_pp, pe_g, fin_g)


def _conv_bwd(z, y1, dyc, conv_w, cn_g, cn_b, w_pw2, b_pw2, tT):
    T = z.shape[0]
    nI = T // tT
    hb = tT // HALO

    def body(cv_ref, cg_ref, ct_ref, hv_ref, hg_ref, y1_ref, dyc_ref, cw_ref, ng_ref, nb_ref, wp_ref, bp_ref,
             dz_ref, a_ref, dy2_ref, vec_ref, gcw_ref, ext, ext2, gpart):
        i = pl.program_id(0)

        @pl.when(i == 0)
        def _():
            ext2[...] = jnp.zeros_like(ext2)
            gpart[...] = jnp.zeros_like(gpart)
            vec_ref[...] = jnp.zeros_like(vec_ref)
        cv, cg, ct = cv_ref[...], cg_ref[...], ct_ref[...]
        sg = _sigmoid(cg)
        has_hist = (i < nI - 1).astype(F32)
        ext[0:HALO, :] = hv_ref[...] * _sigmoid(hg_ref[...]) * has_hist
        ext[HALO:, :] = cv * sg
        yn, rstd = _group_ln(y1_ref[...])
        apre = yn * ng_ref[...] + nb_ref[...]
        sa = _sigmoid(apre)
        a = (apre * sa).astype(_MXU)
        y2 = _dot(a, wp_ref[...]) + bp_ref[...]
        st = _sigmoid(ct)
        dyc_v = dyc_ref[...]
        dy2 = dyc_v * (ct * st)
        dy2b = dy2.astype(_MXU)
        da = _dot_nt(dy2b, wp_ref[...])
        dapre = da * (sa * (1.0 + apre * (1.0 - sa)))
        dy1 = _group_ln_bwd(dapre * ng_ref[...], yn, rstd)
        vec_ref[0:1, :] += _sum_rows(dy1)
        vec_ref[1:2, :] += _sum_rows(dapre * yn)
        vec_ref[2:3, :] += _sum_rows(dapre)
        vec_ref[3:4, :] += _sum_rows(dy2)
        dz_ref[2] = (dyc_v * y2 * (st * (1.0 + ct * (1.0 - st)))).astype(_MXU)
        a_ref[...] = a
        dy2_ref[...] = dy2b
        ext2[tT:tT + HALO, :] = ext2[0:HALO, :]
        ext2[0:tT, :] = dy1
        def grad_tap(m, win):
            p = dy1 * win
            part = p[0:8, :]
            for q in range(1, tT // 8):
                part = part + p[8 * q:8 * q + 8, :]
            gpart[m - 2] += part
        _shifted_windows(ext[...], 2, grad_tap)
        cw = cw_ref[...]
        acc = [None]

        def dv_tap(m, win):
            term = win * cw[CONV_K - 1 - m:CONV_K - m, :]
            acc[0] = term if acc[0] is None else acc[0] + term
        _shifted_windows(ext2[...], 0, dv_tap)
        dv = acc[0]
        dz_ref[0] = (dv * sg).astype(_MXU)
        dz_ref[1] = (dv * cv * sg * (1.0 - sg)).astype(_MXU)

        @pl.when(i == nI - 1)
        def _():
            gcw_ref[...] = jnp.sum(gpart[...], axis=1)

    part = lambda p: pl.BlockSpec((tT, D), lambda i: (nI - 1 - i, p))
    hist = lambda p: pl.BlockSpec((HALO, D), lambda i: (jnp.maximum((nI - 1 - i) * hb - 1, 0), p))
    tok = pl.BlockSpec((tT, D), lambda i: (nI - 1 - i, 0))
    row = pl.BlockSpec((1, D), lambda i: (0, 0))
    blk = (5 * _nbytes((tT, D), F32) + _nbytes((D, D), _MXU) + 5 * _nbytes((tT, D), _MXU)
           + 10 * _nbytes((tT + HALO, D), F32))
    return pl.pallas_call(
        body, name="conv_bwd", grid=(nI,),
        in_specs=[part(0), part(1), part(2), hist(0), hist(1), tok, tok, pl.BlockSpec((HALO, D), lambda i: (0, 0)),
                  row, row, pl.BlockSpec((D, D), lambda i: (0, 0)), row],
        out_specs=[pl.BlockSpec((3, tT, D), lambda i: (0, nI - 1 - i, 0)), tok, tok,
                   pl.BlockSpec((8, D), lambda i: (0, 0)), pl.BlockSpec((HALO, D), lambda i: (0, 0))],
        out_shape=[jax.ShapeDtypeStruct((3, T, D), _MXU), jax.ShapeDtypeStruct((T, D), _MXU),
                   jax.ShapeDtypeStruct((T, D), _MXU), jax.ShapeDtypeStruct((8, D), F32),
                   jax.ShapeDtypeStruct((HALO, D), F32)],
        scratch_shapes=[pltpu.VMEM((tT + HALO, D), F32), pltpu.VMEM((tT + HALO, D), F32), pltpu.VMEM((HALO, 8, D), F32)],
        compiler_params=_cparams(blk, 1),
    )(z, z, z, z, z, y1, dyc, conv_w, cn_g, cn_b, w_pw2, b_pw2)


def _inproj_bwd_u(dzc, dzh, w_in, partials, tT):
    T = dzc.shape[1]
    n = len(partials)
    at = _hosted_steps((T // tT) * NPART)

    def body(dzc_ref, dzh_ref, w_ref, *rest):
        du_ref, acc = rest[n], rest[2 * n + 1]
        exchange = _ChipExchange(n, rest[:n], rest[n + 1:2 * n + 1], rest[2 * n + 2:3 * n + 2], rest[3 * n + 2:])
        j = pl.program_id(1)
        step = pl.program_id(0) * NPART + j
        pl.when(step == at["start"])(exchange.start)
        pl.when(step == at["turn"])(exchange.turn)

        @pl.when(j == 0)
        def _():
            acc[...] = jnp.zeros_like(acc)

        @pl.when(j < 3)
        def _():
            acc[...] += _dot_nt(dzc_ref[0], w_ref[...])

        @pl.when(j >= 3)
        def _():
            acc[...] += _dot_nt(dzh_ref[0], w_ref[...])

        @pl.when(j == NPART - 1)
        def _():
            du_ref[...] = acc[...].astype(du_ref.dtype)
        pl.when(step == at["finish"])(exchange.finish)

    blk = (3 * _nbytes((tT, D), _MXU) + _nbytes((D, D), _MXU) + _nbytes((tT, D), F32)
           + _ChipExchange.scratch_bytes(partials))
    outs = pl.pallas_call(
        body, name="inproj_bwd_u", grid=(T // tT, NPART),
        in_specs=[pl.BlockSpec((1, tT, D), lambda i, j: (jnp.minimum(j, 2), i, 0)),
                  pl.BlockSpec((1, tT, D), lambda i, j: (jnp.maximum(j - 3, 0), i, 0)),
                  pl.BlockSpec((D, D), lambda i, j: (0, j))] + [ANY] * n,
        out_specs=[pl.BlockSpec((tT, D), lambda i, j: (i, 0))] + [ANY] * n,
        out_shape=[jax.ShapeDtypeStruct((T, D), _MXU)] + [pltpu.HBM(p.shape, p.dtype) for p in partials],
        scratch_shapes=[pltpu.VMEM((tT, D), F32)] + _ChipExchange.scratch(partials),
        compiler_params=_cparams(blk, 2),
    )(dzc, dzh, w_in, *_in_hbm(partials))
    return outs[0], outs[1:]


def _inproj_bwd_x(x, ln_g, du, dh, tT):
    T = x.shape[0]

    def body(x_ref, g_ref, du_ref, dh_ref, gx_ref, vec_ref):
        @pl.when(pl.program_id(0) == 0)
        def _():
            vec_ref[...] = jnp.zeros_like(vec_ref)
        xv = x_ref[...]
        r = lax.rsqrt(_mean_lanes(xv * xv) + EPS)
        xn = xv * r
        duv = du_ref[...].astype(F32)
        vec_ref[0:1, :] += _sum_rows(duv * xn)
        dun = duv * g_ref[...]
        gx_ref[...] = dh_ref[...].astype(F32) + r * (dun - xn * _mean_lanes(dun * xn))

    tok = pl.BlockSpec((tT, D), lambda i: (i, 0))
    return pl.pallas_call(
        body, name="inproj_bwd_x", grid=(T // tT,),
        in_specs=[tok, pl.BlockSpec((1, D), lambda i: (0, 0)), tok, tok],
        out_specs=[tok, pl.BlockSpec((8, D), lambda i: (0, 0))],
        out_shape=[jax.ShapeDtypeStruct((T, D), F32), jax.ShapeDtypeStruct((8, D), F32)],
        compiler_params=_cparams(6 * _nbytes((tT, D), F32), 1),
    )(x, ln_g, du, dh)


def _inproj_bwd_w(u, dzc, dzh, tk):
    T = u.shape[0]
    nK = T // tk

    def body(u_ref, dzc_ref, dzh_ref, gw_ref):
        j, k = pl.program_id(0), pl.program_id(1)

        @pl.when(k == 0)
        def _():
            gw_ref[...] = jnp.zeros_like(gw_ref)

        @pl.when(j < 3)
        def _():
            gw_ref[...] += _dot_tn(u_ref[...], dzc_ref[0])

        @pl.when(j >= 3)
        def _():
            gw_ref[...] += _dot_tn(u_ref[...], dzh_ref[0])

    blk = 3 * _nbytes((tk, D), _MXU) + 2 * _nbytes((D, D), F32)
    return pl.pallas_call(
        body, name="inproj_bwd_w", grid=(NPART, nK),
        in_specs=[pl.BlockSpec((tk, D), lambda j, k: (k, 0)),
                  pl.BlockSpec((1, tk, D), lambda j, k: (jnp.minimum(j, 2), jnp.where(j < 3, k, nK - 1), 0)),
                  pl.BlockSpec((1, tk, D), lambda j, k: (jnp.maximum(j - 3, 0), jnp.where(j < 3, 0, k), 0))],
        out_specs=pl.BlockSpec((D, D), lambda j, k: (0, j)),
        out_shape=jax.ShapeDtypeStruct((D, NPART * D), F32),
        compiler_params=_cparams(blk, 2),
    )(u, dzc, dzh)


def _tn_matmul(a_list, b, tk, name):
    T, M = a_list[0].shape
    N = b.shape[1]
    n = len(a_list)

    def body(*refs):
        a_refs, b_ref, o_refs = refs[:n], refs[n], refs[n + 1:]
        for a_ref, o_ref in zip(a_refs, o_refs):
            @pl.when(pl.program_id(0) == 0)
            def _(o_ref=o_ref):
                o_ref[...] = jnp.zeros_like(o_ref)
            o_ref[...] += _dot_tn(a_ref[...], b_ref[...])

    blk = n * _nbytes((tk, M), _MXU) + _nbytes((tk, N), _MXU) + 2 * n * _nbytes((M, N), F32)
    return pl.pallas_call(
        body, name=name, grid=(T // tk,),
        in_specs=[pl.BlockSpec((tk, M), lambda k: (k, 0))] * n + [pl.BlockSpec((tk, N), lambda k: (k, 0))],
        out_specs=[pl.BlockSpec((M, N), lambda k: (0, 0))] * n,
        out_shape=[pltpu.HBM((M, N), F32)] * n,
        compiler_params=_cparams(blk, 1),
    )(*a_list, b)


def _place():
    return lax.axis_index("x"), lax.axis_index("y"), lax.axis_index("c")


def _flip(v, d):
    return 1 - v if d else v


CHIP_MOVES = [(1, 0), (0, 1), (1, 1)]
DEV_MOVES = [(dx, dy, dc) for dx in (0, 1) for dy in (0, 1) for dc in (0, 1)][1:]


def _shard_slice(ref, axis, size, s):
    start = pl.multiple_of(s * size, size)
    return ref.at[pl.ds(start, size), :] if axis == 0 else ref.at[:, pl.ds(start, size)]


class _Bounce:
    def __init__(self, src, buf, dst, sem_in, sem_out):
        self.load = pltpu.make_async_copy(src, buf, sem_in)
        self.store = pltpu.make_async_copy(buf, dst, sem_out)

    def start(self):
        self.load.start()

    def turn(self):
        self.load.wait()
        self.store.start()

    def wait(self):
        self.store.wait()


def _comm_params(scratch_bytes):
    return pltpu.CompilerParams(vmem_limit_bytes=int(min(V7X_VMEM_LIMIT, scratch_bytes + (8 << 20))))


class _Gather:
    def __init__(self, shapes, axes, ins, outs, bufs, sems):
        self.shapes, self.axes, self.ins, self.outs, self.bufs = shapes, axes, ins, outs, bufs
        self.ici_send, self.ici_recv, self.d2d_send, self.d2d_recv, self.in_sems, self.out_sems = sems
        self.x, self.y, self.c = _place()
        self.me = 2 * self.x + self.y
        self.pairs = [(k, j) for k in range(len(shapes)) for j in range(3)]

    @staticmethod
    def scratch(shards):
        n = len(shards)
        return ([pltpu.VMEM(s.shape, s.dtype) for s in shards]
                + [pltpu.SemaphoreType.DMA((3 * n,))] * 4 + [pltpu.SemaphoreType.DMA((n,))] * 2)

    def _own_half(self, k, hc):
        half = self.shapes[k][0] // 2
        return self.ins[k].at[pl.ds(pl.multiple_of(hc * half, 16), half), :]

    def _region(self, k, who, hc):
        rows, cols = self.shapes[k]
        half = rows // 2
        if self.axes[k] == 0:
            return self.outs[k].at[pl.ds(pl.multiple_of(who * rows + hc * half, 16), half), :]
        return self.outs[k].at[pl.ds(pl.multiple_of(hc * half, 16), half), pl.ds(pl.multiple_of(who * cols, HD), cols)]

    def _peer(self, j):
        return 2 * _flip(self.x, CHIP_MOVES[j][0]) + _flip(self.y, CHIP_MOVES[j][1])

    def _ici(self, k, j, who, hc):
        dx, dy = CHIP_MOVES[j]
        return pltpu.make_async_remote_copy(
            src_ref=self._own_half(k, hc), dst_ref=self._region(k, who, hc),
            send_sem=self.ici_send.at[3 * k + j], recv_sem=self.ici_recv.at[3 * k + j],
            device_id=(_flip(self.x, dx), _flip(self.y, dy), self.c), device_id_type=MESH_ID)

    def _d2d(self, k, j, who, hc):
        return pltpu.make_async_remote_copy(
            src_ref=self._region(k, who, hc), dst_ref=self._region(k, who, hc),
            send_sem=self.d2d_send.at[3 * k + j], recv_sem=self.d2d_recv.at[3 * k + j],
            device_id=(self.x, self.y, 1 - self.c), device_id_type=MESH_ID)

    def _local(self, k):
        size = self.shapes[k][self.axes[k]]
        return _Bounce(self.ins[k], self.bufs[k], _shard_slice(self.outs[k], self.axes[k], size, self.me),
                       self.in_sems.at[k], self.out_sems.at[k])

    def start(self):
        for k in range(len(self.shapes)):
            self._local(k).start()
        for k, j in self.pairs:
            self._ici(k, j, self.me, self.c).start()

    def turn(self):
        for k in range(len(self.shapes)):
            self._local(k).turn()

    def forward(self):
        for k, j in self.pairs:
            self._ici(k, j, self._peer(j), self.c).wait_recv()
            self._d2d(k, j, self._peer(j), self.c).start()

    def finish(self):
        for k, j in self.pairs:
            self._d2d(k, j, self._peer(j), 1 - self.c).wait_recv()
        for k, j in self.pairs:
            self._ici(k, j, self.me, self.c).wait_send()
            self._d2d(k, j, self._peer(j), self.c).wait_send()
        for k in range(len(self.shapes)):
            self._local(k).wait()


def _full_shapes(shards, axes):
    return [tuple(d * (N_CHIPS if a == ax else 1) for a, d in enumerate(s.shape)) for s, ax in zip(shards, axes)]


class _Slab:
    def __init__(self, arrays, pick, shard_shape):
        self.arrays = arrays
        self.pick = pick
        self.rows, self.cols = shard_shape
        self.half = self.rows // 2


PAIR_SUM_ROWS = 64


def _pair_exchange_sum(slabs, name):
    n = len(slabs)
    n_in = sum(len(sl.arrays) for sl in slabs)

    def body(*refs):
        ins, outs = refs[:n_in], refs[n_in:n_in + n]
        mine, got, total = (refs[n_in + (1 + t) * n:n_in + (2 + t) * n] for t in range(3))
        send_sems, recv_sems, in_sems, out_sems = refs[n_in + 4 * n:]
        x, y, c = _place()
        started = []
        base = 0
        for k, sl in enumerate(slabs):
            for s in range(N_CHIPS):
                ai, r0, c0 = sl.pick(s)
                src = ins[base + ai]

                def half(hc):
                    return src.at[pl.ds(pl.multiple_of(r0 + hc * sl.half, 8), sl.half), pl.ds(c0, sl.cols)]
                q = N_CHIPS * k + s
                load = pltpu.make_async_copy(half(c), mine[k].at[s], in_sems.at[q])
                load.start()
                cp = pltpu.make_async_remote_copy(
                    src_ref=half(1 - c), dst_ref=got[k].at[s], send_sem=send_sems.at[q], recv_sem=recv_sems.at[q],
                    device_id=(x, y, 1 - c), device_id_type=MESH_ID)
                cp.start()
                store = pltpu.make_async_copy(total[k].at[s], outs[k].at[s], out_sems.at[q])
                started.append((k, s, sl.half, load, cp, store))
            base += len(sl.arrays)
        for k, s, half_rows, load, cp, store in started:
            load.wait()
            cp.wait_recv()
            rows = min(half_rows, PAIR_SUM_ROWS)

            def add(t, carry, k=k, s=s, rows=rows):
                sl_ = pl.ds(pl.multiple_of(t * rows, rows), rows)
                total[k][s, sl_, :] = (mine[k][s, sl_, :] + got[k][s, sl_, :]).astype(_WIRE)
                return carry
            lax.fori_loop(0, half_rows // rows, add, 0)
            store.start()
        for k, s, half_rows, load, cp, store in started:
            cp.wait_send()
            store.wait()

    flat_in = [a for sl in slabs for a in sl.arrays]
    shapes = [(N_CHIPS, sl.half, sl.cols) for sl in slabs]
    vmem = [pltpu.VMEM(sh, dt) for dt in (F32, F32, _WIRE) for sh in shapes]
    return pl.pallas_call(
        body, name=name,
        in_specs=[ANY] * n_in, out_specs=[ANY] * n, out_shape=[pltpu.HBM(sh, _WIRE) for sh in shapes],
        scratch_shapes=vmem + [pltpu.SemaphoreType.DMA((N_CHIPS * n,))] * 4,
        compiler_params=_comm_params(sum(_nbytes(sh, F32) * 2 + _nbytes(sh, _WIRE) for sh in shapes)),
    )(*_in_hbm(flat_in))


class _ChipExchange:
    def __init__(self, n, ins, outs, bufs, sems):
        self.n, self.ins, self.outs, self.bufs = n, ins, outs, bufs
        self.send_sems, self.recv_sems, self.in_sems, self.out_sems = sems
        self.x, self.y, self.c = _place()
        self.me = 2 * self.x + self.y
        self.pairs = [(k, j) for k in range(n) for j in range(3)]

    @staticmethod
    def scratch(partials):
        n = len(partials)
        return ([pltpu.VMEM(p.shape[1:], p.dtype) for p in partials]
                + [pltpu.SemaphoreType.DMA((3 * n,))] * 2 + [pltpu.SemaphoreType.DMA((n,))] * 2)

    @staticmethod
    def scratch_bytes(partials):
        return sum(_nbytes(p.shape[1:], p.dtype) for p in partials)

    def _copy(self, k, j, src_slot, dst_slot):
        px, py = _flip(self.x, CHIP_MOVES[j][0]), _flip(self.y, CHIP_MOVES[j][1])
        return pltpu.make_async_remote_copy(
            src_ref=self.ins[k].at[src_slot], dst_ref=self.outs[k].at[dst_slot],
            send_sem=self.send_sems.at[3 * k + j], recv_sem=self.recv_sems.at[3 * k + j],
            device_id=(px, py, self.c), device_id_type=MESH_ID)

    def _peer(self, j):
        return 2 * _flip(self.x, CHIP_MOVES[j][0]) + _flip(self.y, CHIP_MOVES[j][1])

    def _local(self, k):
        return _Bounce(self.ins[k].at[self.me], self.bufs[k], self.outs[k].at[self.me],
                       self.in_sems.at[k], self.out_sems.at[k])

    def start(self):
        for k in range(self.n):
            self._local(k).start()
        for k, j in self.pairs:
            self._copy(k, j, self._peer(j), self.me).start()

    def turn(self):
        for k in range(self.n):
            self._local(k).turn()

    def finish(self):
        for k, j in self.pairs:
            self._copy(k, j, self.me, self._peer(j)).wait_recv()
        for k, j in self.pairs:
            self._copy(k, j, self._peer(j), self.me).wait_send()
        for k in range(self.n):
            self._local(k).wait()


def _hosted_steps(steps):
    return dict(start=0, turn=steps // 4, forward=steps // 2, finish=steps - 1)


def _pair_share(slots, vec):
    n = len(slots)
    nv = len(DEV_MOVES)

    def body(*refs):
        ins, vec_ref = refs[:n], refs[n]
        outs, vec_out = refs[n + 1:2 * n + 1], refs[2 * n + 1]
        slot_b, half_b, vec_b = refs[2 * n + 2:3 * n + 2], refs[3 * n + 2:4 * n + 2], refs[4 * n + 2]
        send_sems, recv_sems, in_sems, out_sems = refs[4 * n + 3:]
        x, y, c = _place()
        dev = 4 * x + 2 * y + c

        def vec_copy(j, slot):
            dx, dy, dc = DEV_MOVES[j]
            return pltpu.make_async_remote_copy(
                src_ref=vec_ref, dst_ref=vec_out.at[slot], send_sem=send_sems.at[n + j], recv_sem=recv_sems.at[n + j],
                device_id=(_flip(x, dx), _flip(y, dy), _flip(c, dc)), device_id_type=MESH_ID)

        def rows(k, hc):
            hr = slots[k].shape[1]
            return outs[k].at[pl.ds(pl.multiple_of(hc * hr, 8), hr), :]

        def share(k, hc):
            return pltpu.make_async_remote_copy(
                src_ref=half_b[k], dst_ref=rows(k, hc), send_sem=send_sems.at[k], recv_sem=recv_sems.at[k],
                device_id=(x, y, 1 - c), device_id_type=MESH_ID)

        vec_loc = _Bounce(vec_ref, vec_b, vec_out.at[dev], in_sems.at[n], out_sems.at[n])
        vec_loc.start()
        for j in range(nv):
            vec_copy(j, dev).start()
        loads = [pltpu.make_async_copy(ins[k], slot_b[k], in_sems.at[k]) for k in range(n)]
        stores = [pltpu.make_async_copy(half_b[k], rows(k, c), out_sems.at[k]) for k in range(n)]
        for load in loads:
            load.start()
        vec_loc.turn()
        for k in range(n):
            loads[k].wait()
            hr = slots[k].shape[1]
            step_rows = min(hr, PAIR_SUM_ROWS)

            def add(t, carry, k=k, step_rows=step_rows):
                sl_ = pl.ds(pl.multiple_of(t * step_rows, step_rows), step_rows)
                acc = slot_b[k][0, sl_, :].astype(F32)
                for s in range(1, N_CHIPS):
                    acc = acc + slot_b[k][s, sl_, :].astype(F32)
                half_b[k][sl_, :] = acc
                return carry
            lax.fori_loop(0, hr // step_rows, add, 0)
            stores[k].start()
            share(k, c).start()
        for k in range(n):
            share(k, 1 - c).wait_recv()
        for j, (dx, dy, dc) in enumerate(DEV_MOVES):
            vec_copy(j, 4 * _flip(x, dx) + 2 * _flip(y, dy) + _flip(c, dc)).wait_recv()
        for k in range(n):
            share(k, c).wait_send()
            stores[k].wait()
        for j in range(nv):
            vec_copy(j, dev).wait_send()
        vec_loc.wait()

    halves = [s.shape[1:] for s in slots]
    vmem = ([pltpu.VMEM(s.shape, s.dtype) for s in slots] + [pltpu.VMEM(h, F32) for h in halves]
            + [pltpu.VMEM(vec.shape, F32)])
    outs = pl.pallas_call(
        body, name="grad_pair_share",
        in_specs=[ANY] * (n + 1), out_specs=[ANY] * (n + 1),
        out_shape=[pltpu.HBM((2 * h[0], h[1]), F32) for h in halves] + [pltpu.HBM((N_DEV,) + vec.shape, F32)],
        scratch_shapes=vmem + [pltpu.SemaphoreType.DMA((n + nv,))] * 2 + [pltpu.SemaphoreType.DMA((n + 1,))] * 2,
        compiler_params=_comm_params(sum(_nbytes(s.shape, s.dtype) for s in slots) + sum(_nbytes(h, F32) for h in halves)
                                     + _nbytes(vec.shape, F32)),
    )(*_in_hbm(list(slots) + [vec]))
    return outs[:n], outs[n]


def _row_block(rows, cols, n_arrays):
    br = rows
    while br % 16 == 0 and 2 * n_arrays * br * cols * 4 > (16 << 20):
        br //= 2
    return br


def _sum_slots(a, name):
    n, rows, cols = a.shape
    br = _row_block(rows, cols, n + 1)

    def body(a_ref, o_ref):
        acc = a_ref[0].astype(F32)
        for s in range(1, n):
            acc = acc + a_ref[s].astype(F32)
        o_ref[...] = acc

    return pl.pallas_call(body, name=name, grid=(rows // br,),
                          in_specs=[pl.BlockSpec((n, br, cols), lambda i: (0, i, 0))],
                          out_specs=pl.BlockSpec((br, cols), lambda i: (i, 0)),
                          out_shape=pltpu.HBM((rows, cols), F32),
                          compiler_params=_cparams((n + 1) * br * cols * 4, 1))(*_in_hbm([a]))


def _adamw_math(w, g, m, v):
    m = ADAM_B1 * m + (1.0 - ADAM_B1) * g
    v = ADAM_B2 * v + (1.0 - ADAM_B2) * (g * g)
    m_hat = m / (1.0 - ADAM_B1 ** ADAM_STEP)
    v_hat = v / (1.0 - ADAM_B2 ** ADAM_STEP)
    delta = -ADAM_LR * (m_hat / (jnp.sqrt(v_hat) + ADAM_EPS) + ADAM_WD * w)
    return delta, m, v


def _adamw(g, w, m, v, name):
    rows, cols = g.shape
    br = _row_block(rows, cols, 7)

    def body(g_ref, w_ref, m_ref, v_ref, d_ref, nm_ref, nv_ref):
        d_ref[...], nm_ref[...], nv_ref[...] = _adamw_math(w_ref[...], g_ref[...], m_ref[...], v_ref[...])

    spec = pl.BlockSpec((br, cols), lambda i: (i, 0))
    return pl.pallas_call(body, name=name, grid=(rows // br,), in_specs=[spec] * 4, out_specs=[spec] * 3,
                          out_shape=[jax.ShapeDtypeStruct(g.shape, F32)] * 3,
                          compiler_params=_cparams(7 * br * cols * 4, 1))(g, w, m, v)


ROW_FINAL_G, ROW_PE_G, ROW_LOSS = 0, 1, 2
ROW_CONV_B, ROW_CN_G, ROW_CN_B, ROW_B_PW2 = 8, 9, 10, 11
ROW_LN_G = 16
ROW_ONORM_G, ROW_LB = 24, 25
ROW_CONV_W = 32
SMALL = ["ln_g", "conv_b", "cnorm_g", "cnorm_b", "b_pw2", "onorm_g", "pe_norm_g", "final_g"]
SMALL_ROW = dict(ln_g=ROW_LN_G, conv_b=ROW_CONV_B, cnorm_g=ROW_CN_G, cnorm_b=ROW_CN_B, b_pw2=ROW_B_PW2,
                 onorm_g=ROW_ONORM_G, pe_norm_g=ROW_PE_G, final_g=ROW_FINAL_G)


def _adamw_small(vsum, gcw, lb_logits, params):
    names = SMALL + ["lb_logits", "conv_w"]
    flat = [t for nm in names for t in params[nm]]

    def body(*refs):
        vs_ref, gcw_ref, lbl_ref = refs[:3]
        ins = refs[3:3 + 3 * len(names)]
        outs = refs[3 + 3 * len(names):]
        for q, nm in enumerate(names):
            w_ref, m_ref, v_ref = ins[3 * q:3 * q + 3]
            g_ref, d_ref, nm_ref, nv_ref = outs[4 * q:4 * q + 4]
            if nm == "conv_w":
                g = gcw_ref[...]
            elif nm == "lb_logits":
                lb = _softmax_row0(lbl_ref[...])
                g0 = vs_ref[ROW_LB:ROW_LB + 1, :] * lb * (1.0 - lb)
                g = jnp.concatenate([g0, -g0], axis=0)
            else:
                g = vs_ref[SMALL_ROW[nm]:SMALL_ROW[nm] + 1, :]
            g_ref[...] = g
            d_ref[...], nm_ref[...], nv_ref[...] = _adamw_math(w_ref[...], g, m_ref[...], v_ref[...])

    out_shape = [jax.ShapeDtypeStruct(params[nm][0].shape, F32) for nm in names for _ in range(4)]
    outs = pl.pallas_call(body, name="adamw_small", out_shape=out_shape)(vsum, gcw, lb_logits, *flat)
    return {nm: tuple(outs[4 * q:4 * q + 4]) for q, nm in enumerate(names)}


TOKEN_TILE = dict(rmsnorm=512, inproj_fwd=2048, conv=256, hgrn=512, tail=512, inproj_bwd_u=2048, inproj_bwd_x=512,
                  weight_grad=2048)


def _tile(T, family):
    return min(T, TOKEN_TILE[family])


def kernel(x, p, ln_g, w_in, conv_w, conv_b, cnorm_g, cnorm_b, w_pw2, b_pw2, lb_logits, onorm_g, w_out, pe_norm_g, w_pg, w_pp, final_g, loss_target, m_ln_g, m_w_in, m_conv_w, m_conv_b, m_cnorm_g, m_cnorm_b, m_w_pw2, m_b_pw2, m_lb_logits, m_onorm_g, m_w_out, m_pe_norm_g, m_w_pg, m_w_pp, m_final_g, v_ln_g, v_w_in, v_conv_w, v_conv_b, v_cnorm_g, v_cnorm_b, v_w_pw2, v_b_pw2, v_lb_logits, v_onorm_g, v_w_out, v_pe_norm_g, v_w_pg, v_w_pp, v_final_g):
    given = dict(locals())
    x2, p2, tgt = x[0], p[0, 0], loss_target[0]
    T = x2.shape[0]
    fin_g = final_g.reshape(1, D)

    conv_w_pad = jnp.pad(conv_w[0], ((0, HALO - CONV_K), (0, 0)))
    u, (w_in_f,) = _rmsnorm_gather(x2, ln_g, [w_in[0].astype(_MXU)], [1], _tile(T, "rmsnorm"))

    z, (w_pw2_f, w_out_f, w_pg_f, w_pp_f, conv_w_f) = _inproj_fwd(
        u, w_in_f,
        [w_pw2[0].astype(_MXU), w_out[0].astype(_MXU), w_pg[0].astype(_MXU), w_pp[0].astype(_MXU), conv_w_pad],
        [0, 0, 0, 1, 1], _tile(T, "inproj_fwd"))
    yc, y1 = _conv_fwd(z, conv_w_f, conv_b, cnorm_g, cnorm_b, w_pw2_f, b_pw2, _tile(T, "conv"))
    o_raw, yh, s_chunks = _hgrn_fwd(z, lb_logits, onorm_g, _tile(T, "hgrn"), HB)
    dyc, dyh, n2, ds, dpe, dhb, pb, vec_tail = _tail(
        x2, yc, yh, p2, tgt, w_out_f, w_pg_f, w_pp_f, pe_norm_g, fin_g, _tile(T, "tail"))
    tk = _tile(T, "weight_grad")
    g_w_out_c, g_w_out_h = _tn_matmul([yc, yh], dhb, tk, "grad_w_out")
    (g_w_pg,) = _tn_matmul([n2], ds, tk, "grad_w_pg")
    (g_w_pp,) = _tn_matmul([pb], dpe, tk, "grad_w_pp")
    dzc, a_act, dy2, vec_conv, g_conv_w = _conv_bwd(z, y1, dyc, conv_w_f, cnorm_g, cnorm_b, w_pw2_f, b_pw2, _tile(T, "conv"))
    (g_w_pw2,) = _tn_matmul([a_act], dy2, tk, "grad_w_pw2")

    rest = ["w_pw2", "w_out", "w_pg", "w_pp"]
    partial_rest = _pair_exchange_sum([
        _Slab([g_w_pw2], lambda s: (0, s * (D // N_CHIPS), 0), (D // N_CHIPS, D)),
        _Slab([g_w_out_c, g_w_out_h], lambda s: (s // 2, (s % 2) * (D // 2), 0), (D // 2, D)),
        _Slab([g_w_pg], lambda s: (0, s * (D // N_CHIPS), 0), (D // N_CHIPS, D)),
        _Slab([g_w_pp], lambda s: (0, 0, s * (D // N_CHIPS)), (PLE, D // N_CHIPS)),
    ], "grad_pair_exchange_rest")
    dzh, vec_hgrn, slots_rest = _hgrn_bwd(z, lb_logits, onorm_g, o_raw, dyh, s_chunks, partial_rest, _tile(T, "hgrn"), HB)
    g_w_in = _inproj_bwd_w(u, dzc, dzh, tk)
    partial_in = _pair_exchange_sum([
        _Slab([g_w_in], lambda s: (0, 0, s * (NPART * D // N_CHIPS)), (D, NPART * D // N_CHIPS))], "grad_pair_exchange_w_in")
    du, slots_in = _inproj_bwd_u(dzc, dzh, w_in_f, partial_in, _tile(T, "inproj_bwd_u"))
    grad_x, vec_in = _inproj_bwd_x(x2, ln_g, du, dhb, _tile(T, "inproj_bwd_x"))
    big = ["w_in"] + rest
    vec = jnp.concatenate([vec_tail, vec_conv, vec_in, vec_hgrn, g_conv_w], axis=0)
    grads_big, vec_slots = _pair_share(list(slots_in) + list(slots_rest), vec)
    vsum = _sum_slots(vec_slots, "vec_sum")

    out = {}
    for nm, g in zip(big, grads_big):
        w2, m2, v2 = given[nm][0], given["m_" + nm][0], given["v_" + nm][0]
        d, nm_, nv_ = _adamw(g, w2, m2, v2, "adamw_" + nm)
        out[nm] = tuple(t[None] for t in (g, d, nm_, nv_))
    chip = 2 * lax.axis_index("x") + lax.axis_index("y")
    gcw = lax.dynamic_slice(vsum, (ROW_CONV_W, chip * (D // N_CHIPS)), (CONV_K, D // N_CHIPS))
    params = {nm: (given[nm].reshape(-1, D), given["m_" + nm].reshape(-1, D), given["v_" + nm].reshape(-1, D))
              for nm in SMALL + ["lb_logits"]}
    params["conv_w"] = (conv_w[0], m_conv_w[0], v_conv_w[0])
    small = _adamw_small(vsum, gcw, lb_logits, params)
    for nm, ts in small.items():
        out[nm] = tuple(t.reshape(given[nm].shape) for t in ts)

    loss = vsum[ROW_LOSS, 0]
    order = ["ln_g", "w_in", "conv_w", "conv_b", "cnorm_g", "cnorm_b", "w_pw2", "b_pw2", "lb_logits", "onorm_g",
             "w_out", "pe_norm_g", "w_pg", "w_pp", "final_g"]
    return (loss, grad_x[None], *[out[nm][0] for nm in order], *[out[nm][1] for nm in order],
            *[out[nm][2] for nm in order], *[out[nm][3] for nm in order])
```

```python
import functools

import jax
import jax.numpy as jnp
from jax import lax
from jax.experimental import pallas as pl
from jax.experimental.pallas import tpu as pltpu

F32 = jnp.float32
BF16 = jnp.bfloat16
_MXU = jnp.bfloat16
_WIRE = jnp.bfloat16

D = 1024
NPART = 7
PLE = 256
HEADS = 8
HD = 128
CHUNK = 64
CONV_K = 31
HALO = 32
EPS = 1e-6
N_CHIPS = 4
N_DEV = 8
HB = 8
VEC_ROWS = 64

ADAM_LR = 0.001
ADAM_B1 = 0.9
ADAM_B2 = 0.999
ADAM_EPS = 1e-08
ADAM_WD = 0.01
ADAM_STEP = 10

V7X_VMEM_LIMIT = 60000 * 1024
MESH_ID = pl.DeviceIdType.MESH
ANY = pl.BlockSpec(memory_space=pltpu.HBM)


def _in_hbm(arrays):
    return [pltpu.with_memory_space_constraint(a, pltpu.HBM) for a in arrays]


def _cparams(block_bytes, n_grid_dims):
    limit = min(V7X_VMEM_LIMIT, 2 * block_bytes + (24 << 20))
    return pltpu.CompilerParams(vmem_limit_bytes=int(limit), dimension_semantics=("arbitrary",) * n_grid_dims)


def _nbytes(shape, dtype):
    n = 1
    for s in shape:
        n *= s
    return n * jnp.dtype(dtype).itemsize


def _dot(a, b):
    return jnp.dot(a.astype(_MXU), b.astype(_MXU), preferred_element_type=F32)


def _dot_nt(a, b):
    return lax.dot_general(a.astype(_MXU), b.astype(_MXU), (((1,), (1,)), ((), ())), preferred_element_type=F32)


def _dot_tn(a, b):
    return lax.dot_general(a.astype(_MXU), b.astype(_MXU), (((0,), (0,)), ((), ())), preferred_element_type=F32)


def _tri_dot(tri_bf, x):
    x1 = x.astype(BF16)
    r1 = x - x1.astype(F32)
    x2 = r1.astype(BF16)
    x3 = (r1 - x2.astype(F32)).astype(BF16)
    d = lambda t: jnp.dot(tri_bf, t, preferred_element_type=F32)
    return d(x1) + d(x2) + d(x3)


def _split2(x):
    hi = x.astype(BF16)
    return hi, (x - hi.astype(F32)).astype(BF16)


def _dot3(dims, a, b):
    d = lambda p, q: lax.dot_general(p, q, (dims, ((), ())), preferred_element_type=F32)
    return d(a[0], b[0]) + d(a[0], b[1]) + d(a[1], b[0])


def _sigmoid(x):
    return jax.nn.sigmoid(x)


def _mean_lanes(x):
    return jnp.mean(x, axis=-1, keepdims=True)


def _sum_rows(x):
    return jnp.sum(x, axis=0, keepdims=True)


def _group_ln(y):
    yn, rs = [], []
    for g in range(D // HD):
        blk = y[:, g * HD:(g + 1) * HD]
        xc = blk - _mean_lanes(blk)
        r = lax.rsqrt(_mean_lanes(xc * xc) + EPS)
        yn.append(xc * r)
        rs.append(jnp.broadcast_to(r, blk.shape))
    return jnp.concatenate(yn, axis=1), jnp.concatenate(rs, axis=1)


def _group_ln_bwd(dyn, yn, rstd):
    out = []
    for g in range(D // HD):
        sl = slice(g * HD, (g + 1) * HD)
        d, n = dyn[:, sl], yn[:, sl]
        out.append(rstd[:, sl] * (d - _mean_lanes(d) - n * _mean_lanes(d * n)))
    return jnp.concatenate(out, axis=1)


def _head_means(x, hb, fn=lambda m: m):
    return jnp.concatenate([jnp.broadcast_to(fn(_mean_lanes(x[:, hh * HD:(hh + 1) * HD])), (x.shape[0], HD))
                            for hh in range(hb)], axis=1)


def _head_rsqrt_mean(x, hb):
    return _head_means(x, hb, lambda m: lax.rsqrt(m + EPS))


def _softmax_row0(lbl):
    m = jnp.max(lbl, axis=0, keepdims=True)
    e = jnp.exp(lbl - m)
    return e[0:1, :] / jnp.sum(e, axis=0, keepdims=True)


def _hosted_gather(phases, step, at, shards, axes, ins, outs, bufs, sems):
    gather = _Gather([s.shape for s in shards], axes, ins, outs, bufs, sems)
    for phase in phases:
        pl.when(step == at[phase])(getattr(gather, phase))


def _rmsnorm_gather(x, ln_g, shards, axes, tT):
    T = x.shape[0]
    n = len(shards)
    at = _hosted_steps(T // tT)

    def body(x_ref, g_ref, *rest):
        ins, u_ref, outs, bufs, sems = rest[:n], rest[n], rest[n + 1:2 * n + 1], rest[2 * n + 1:3 * n + 1], rest[3 * n + 1:]
        host = functools.partial(_hosted_gather, step=pl.program_id(0), at=at, shards=shards, axes=axes,
                                 ins=ins, outs=outs, bufs=bufs, sems=sems)
        host(("start", "turn", "forward"))
        xv = x_ref[...]
        r = lax.rsqrt(_mean_lanes(xv * xv) + EPS)
        u_ref[...] = (xv * r * g_ref[...]).astype(_MXU)
        host(("finish",))

    blk = _nbytes((tT, D), F32) * 2 + _nbytes((tT, D), _MXU) + sum(_nbytes(s.shape, s.dtype) for s in shards)
    outs = pl.pallas_call(
        body, name="rmsnorm_gather", grid=(T // tT,),
        in_specs=[pl.BlockSpec((tT, D), lambda i: (i, 0)), pl.BlockSpec((1, D), lambda i: (0, 0))] + [ANY] * n,
        out_specs=[pl.BlockSpec((tT, D), lambda i: (i, 0))] + [ANY] * n,
        out_shape=[jax.ShapeDtypeStruct((T, D), _MXU)]
        + [pltpu.HBM(fs, s.dtype) for fs, s in zip(_full_shapes(shards, axes), shards)],
        scratch_shapes=_Gather.scratch(shards),
        compiler_params=_cparams(blk, 1),
    )(x, ln_g, *_in_hbm(shards))
    return outs[0], outs[1:]


def _inproj_fwd(u, w_in, shards, axes, tT):
    T = u.shape[0]
    n = len(shards)
    at = _hosted_steps((T // tT) * NPART)

    def body(u_ref, w_ref, *rest):
        ins, z_ref, outs, bufs, sems = rest[:n], rest[n], rest[n + 1:2 * n + 1], rest[2 * n + 1:3 * n + 1], rest[3 * n + 1:]
        host = functools.partial(_hosted_gather, step=pl.program_id(0) * NPART + pl.program_id(1), at=at, shards=shards,
                                 axes=axes, ins=ins, outs=outs, bufs=bufs, sems=sems)
        host(("start", "turn", "forward"))
        z_ref[...] = jnp.dot(u_ref[...], w_ref[...], preferred_element_type=F32)
        host(("finish",))

    blk = (_nbytes((tT, D), F32) + _nbytes((D, D), _MXU) + _nbytes((tT, D), _MXU)
           + sum(_nbytes(s.shape, s.dtype) for s in shards))
    outs = pl.pallas_call(
        body, name="inproj_fwd", grid=(T // tT, NPART),
        in_specs=[pl.BlockSpec((tT, D), lambda i, j: (i, 0)), pl.BlockSpec((D, D), lambda i, j: (0, j))] + [ANY] * n,
        out_specs=[pl.BlockSpec((tT, D), lambda i, j: (i, j))] + [ANY] * n,
        out_shape=[jax.ShapeDtypeStruct((T, NPART * D), F32)]
        + [pltpu.HBM(fs, s.dtype) for fs, s in zip(_full_shapes(shards, axes), shards)],
        scratch_shapes=_Gather.scratch(shards),
        compiler_params=_cparams(blk, 2),
    )(u, w_in, *_in_hbm(shards))
    return outs[0], outs[1:]


TAP_LANES = 128


def _shifted_windows(ext, first, visit):
    n = ext.shape[0]
    for m in range(first, first + CONV_K):
        visit(m, (ext if m == 0 else pltpu.roll(ext, n - m, axis=0))[0:n - HALO, :])


def _conv_fwd(z, conv_w, conv_b, cn_g, cn_b, w_pw2, b_pw2, tT):
    T = z.shape[0]

    def body(cv_ref, cg_ref, ct_ref, cw_ref, cb_ref, ng_ref, nb_ref, wp_ref, bp_ref, yc_ref, y1_ref, ext):
        @pl.when(pl.program_id(0) == 0)
        def _():
            ext[...] = jnp.zeros_like(ext)
        ext[0:HALO, :] = ext[tT:tT + HALO, :]
        ext[HALO:, :] = cv_ref[...] * _sigmoid(cg_ref[...])
        cw = cw_ref[...]
        acc = [cb_ref[...]]

        def tap(m, win):
            acc[0] = acc[0] + win * cw[m - 2:m - 1, :]
        _shifted_windows(ext[...], 2, tap)
        y1 = acc[0]
        y1_ref[...] = y1
        yn, _ = _group_ln(y1)
        apre = yn * ng_ref[...] + nb_ref[...]
        a = apre * _sigmoid(apre)
        y2 = _dot(a, wp_ref[...]) + bp_ref[...]
        ct = ct_ref[...]
        yc_ref[...] = (y2 * (ct * _sigmoid(ct))).astype(_MXU)

    part = lambda p: pl.BlockSpec((tT, D), lambda i: (i, p))
    row = pl.BlockSpec((1, D), lambda i: (0, 0))
    tok = pl.BlockSpec((tT, D), lambda i: (i, 0))
    blk = 4 * _nbytes((tT, D), F32) + _nbytes((D, D), _MXU) + _nbytes((tT, D), _MXU) + 8 * _nbytes((tT + HALO, D), F32)
    return pl.pallas_call(
        body, name="conv_fwd", grid=(T // tT,),
        in_specs=[part(0), part(1), part(2), pl.BlockSpec((HALO, D), lambda i: (0, 0)), row, row, row,
                  pl.BlockSpec((D, D), lambda i: (0, 0)), row],
        out_specs=[tok, tok],
        out_shape=[jax.ShapeDtypeStruct((T, D), _MXU), jax.ShapeDtypeStruct((T, D), F32)],
        scratch_shapes=[pltpu.VMEM((tT + HALO, D), F32)],
        compiler_params=_cparams(blk, 1),
    )(z, z, z, conv_w, conv_b, cn_g, cn_b, w_pw2, b_pw2)


def _hgrn_gates(lb, hq, hf):
    sq = _sigmoid(hq)
    sg = _sigmoid(hf)
    f = lb + (1.0 - lb) * sg
    return sq, sg, f, hq * sq, (1.0 - lb) * (1.0 - sg), jnp.log(f)


def _chunk_decays(lf, q, k):
    r = lax.broadcasted_iota(jnp.int32, (CHUNK, CHUNK), 0)
    c = lax.broadcasted_iota(jnp.int32, (CHUNK, CHUNK), 1)
    b = _tri_dot((r >= c).astype(BF16), lf)
    bm = b[CHUNK // 2 - 1:CHUNK // 2, :]
    bl = b[CHUNK - 1:CHUNK, :]
    eb = jnp.exp(b)
    eqm = jnp.exp(b - bm)
    ekm = jnp.exp(bm - b)
    ekd = jnp.exp(bl - b)
    return dict(causal=r >= c, eb=eb, eqm=eqm, ekm=ekm, ekd=ekd, ebl=jnp.exp(bl),
                qd=q * eb, qm=q * eqm, km=k * ekm, kd=k * ekd)


def _hgrn_fwd(z, lb_logits, onorm_g, tT, hb):
    T = z.shape[0]
    nc = tT // CHUNK
    w = hb * HD

    def body(lbl_ref, og_ref, hq_ref, hf_ref, hi_ref, hg_ref, o_ref, yh_ref, sc_ref, st):
        @pl.when(pl.program_id(1) == 0)
        def _():
            st[...] = jnp.zeros_like(st)
        lb_all = _softmax_row0(lbl_ref[...])
        og_all = og_ref[...]

        def chunk(c, carry):
            sl = pl.ds(pl.multiple_of(c * CHUNK, CHUNK), CHUNK)
            lanes = [slice(hh * HD, (hh + 1) * HD) for hh in range(hb)]
            heads = lambda fn: [fn(hh, ln) for hh, ln in enumerate(lanes)]
            hg, v = hg_ref[sl, :], hi_ref[sl, :]
            _, _, _, q, k, lf = _hgrn_gates(lb_all, hq_ref[sl, :], hf_ref[sl, :])
            dc = _chunk_decays(lf, q, k)
            s_t = heads(lambda hh, ln: st[hh])
            a = heads(lambda hh, ln: jnp.where(dc["causal"], _dot_nt(dc["qm"][:, ln], dc["km"][:, ln]), 0.0))
            o_inter = heads(lambda hh, ln: _dot_nt(dc["qd"][:, ln], s_t[hh]))
            kv = heads(lambda hh, ln: _dot_tn(v[:, ln], dc["kd"][:, ln]))
            o_intra = heads(lambda hh, ln: _dot(a[hh], v[:, ln]))
            for hh, ln in enumerate(lanes):
                sc_ref[hh, c] = s_t[hh]
                st[hh] = s_t[hh] * dc["ebl"][:, ln] + kv[hh]
            o = jnp.concatenate([o_inter[hh] + o_intra[hh] for hh in range(hb)], axis=1)
            o_ref[sl, :] = o
            n = o * _head_rsqrt_mean(o * o, hb)
            yh_ref[sl, :] = ((n * og_all) * (hg * _sigmoid(hg))).astype(_MXU)
            return carry

        lax.fori_loop(0, nc, chunk, 0, unroll=8)

    zpart = lambda p: pl.BlockSpec((tT, w), lambda h, i: (i, p * (HEADS // hb) + h))
    blk = 6 * _nbytes((tT, w), F32) + _nbytes((hb, nc, HD, HD), F32)
    return pl.pallas_call(
        body, name="hgrn_fwd", grid=(HEADS // hb, T // tT),
        in_specs=[pl.BlockSpec((2, w), lambda h, i: (0, h)), pl.BlockSpec((1, w), lambda h, i: (0, h)),
                  zpart(3), zpart(4), zpart(5), zpart(6)],
        out_specs=[pl.BlockSpec((tT, w), lambda h, i: (i, h)), pl.BlockSpec((tT, w), lambda h, i: (i, h)),
                   pl.BlockSpec((hb, nc, HD, HD), lambda h, i: (h, i, 0, 0))],
        out_shape=[jax.ShapeDtypeStruct((T, D), F32), jax.ShapeDtypeStruct((T, D), _MXU),
                   jax.ShapeDtypeStruct((HEADS, T // CHUNK, HD, HD), F32)],
        scratch_shapes=[pltpu.VMEM((hb, HD, HD), F32)],
        compiler_params=_cparams(blk, 2),
    )(lb_logits, onorm_g, z, z, z, z)


def _hgrn_bwd(z, lb_logits, onorm_g, o_raw, dyh, s_chunks, partials, tT, hb):
    T = z.shape[0]
    nc = tT // CHUNK
    nI = T // tT
    w = hb * HD
    n = len(partials)
    at = _hosted_steps((HEADS // hb) * nI)

    def body(lbl_ref, og_ref, hq_ref, hf_ref, hi_ref, hg_ref, o_ref, dy_ref, sc_ref, *rest):
        (dz_ref, vec_ref), dst = rest[n:n + 2], rest[2 * n + 2]
        exchange = _ChipExchange(n, rest[:n], rest[n + 2:2 * n + 2], rest[2 * n + 3:3 * n + 3], rest[3 * n + 3:])
        step = pl.program_id(0) * nI + pl.program_id(1)
        pl.when(step == at["start"])(exchange.start)
        pl.when(step == at["turn"])(exchange.turn)

        @pl.when(pl.program_id(1) == 0)
        def _():
            dst[...] = jnp.zeros_like(dst)
            vec_ref[...] = jnp.zeros_like(vec_ref)
        lb_all = _softmax_row0(lbl_ref[...])
        og_all = og_ref[...]
        last_row = lax.broadcasted_iota(jnp.int32, (CHUNK, w), 0) == CHUNK - 1
        r64 = lax.broadcasted_iota(jnp.int32, (CHUNK, CHUNK), 0)
        c64 = lax.broadcasted_iota(jnp.int32, (CHUNK, CHUNK), 1)
        upper = (c64 >= r64).astype(BF16)
        lanes = [slice(hh * HD, (hh + 1) * HD) for hh in range(hb)]
        heads = lambda fn: [fn(hh, ln) for hh, ln in enumerate(lanes)]
        wide = lambda parts: jnp.concatenate(parts, axis=1)

        def chunk(cc, carry):
            c = nc - 1 - cc
            sl = pl.ds(pl.multiple_of(c * CHUNK, CHUNK), CHUNK)
            hq, hg, v = hq_ref[sl, :], hg_ref[sl, :], hi_ref[sl, :]
            sq, sg, f, q, k, lf = _hgrn_gates(lb_all, hq, hf_ref[sl, :])
            dc = _chunk_decays(lf, q, k)
            s_t = heads(lambda hh, ln: sc_ref[hh, c])
            ds_t = heads(lambda hh, ln: dst[hh])
            o, dy = o_ref[sl, :], dy_ref[sl, :]
            r = _head_rsqrt_mean(o * o, hb)
            n = o * r
            sgg = _sigmoid(hg)
            silu_g = hg * sgg
            dhg = dy * (n * og_all) * (sgg * (1.0 + hg * (1.0 - sgg)))
            dn = dy * og_all * silu_g
            g_og = _sum_rows(dy * n * silu_g)
            do = r * (dn - n * _head_means(dn * n, hb))
            a = heads(lambda hh, ln: jnp.where(dc["causal"], _dot_nt(dc["qm"][:, ln], dc["km"][:, ln]), 0.0))
            dam = heads(lambda hh, ln: jnp.where(dc["causal"], _dot_nt(do[:, ln], v[:, ln]), 0.0))
            dqd = wide(heads(lambda hh, ln: _dot(do[:, ln], s_t[hh])))
            dkd = wide(heads(lambda hh, ln: _dot(v[:, ln], ds_t[hh])))
            dv_inter = heads(lambda hh, ln: _dot_nt(dc["kd"][:, ln], ds_t[hh]))
            dqs = heads(lambda hh, ln: _dot_tn(do[:, ln], dc["qd"][:, ln]))
            dv = wide(heads(lambda hh, ln: _dot_tn(a[hh], do[:, ln]) + dv_inter[hh]))
            dam2 = [_split2(t) for t in dam]
            km2, qm2 = _split2(dc["km"]), _split2(dc["qm"])
            dqm = wide(heads(lambda hh, ln: _dot3(((1,), (0,)), dam2[hh], (km2[0][:, ln], km2[1][:, ln]))))
            dkm = wide(heads(lambda hh, ln: _dot3(((0,), (0,)), dam2[hh], (qm2[0][:, ln], qm2[1][:, ln]))))
            debl = wide(heads(lambda hh, ln: _sum_rows(ds_t[hh] * s_t[hh])))
            for hh, ln in enumerate(lanes):
                dst[hh] = ds_t[hh] * dc["ebl"][:, ln] + dqs[hh]
            dq = dqd * dc["eb"] + dqm * dc["eqm"]
            dk = dkm * dc["ekm"] + dkd * dc["ekd"]
            dbl = _sum_rows(dkd * dc["kd"]) + debl * dc["ebl"]
            db = dq * q - dk * k + jnp.where(last_row, dbl, 0.0)
            dlf = _tri_dot(upper, db)
            dfk = dlf / f - dk
            dz_ref[0, sl, :] = (dq * (sq * (1.0 + hq * (1.0 - sq)))).astype(_MXU)
            dz_ref[1, sl, :] = (dfk * ((1.0 - lb_all) * sg * (1.0 - sg))).astype(_MXU)
            dz_ref[2, sl, :] = dv.astype(_MXU)
            dz_ref[3, sl, :] = dhg.astype(_MXU)
            vec_ref[0:1, :] += g_og
            vec_ref[1:2, :] += _sum_rows(dfk * (1.0 - sg))
            return carry

        lax.fori_loop(0, nc, chunk, 0, unroll=8)
        pl.when(step == at["finish"])(exchange.finish)

    zpart = lambda p: pl.BlockSpec((tT, w), lambda h, i: (nI - 1 - i, p * (HEADS // hb) + h))
    act = pl.BlockSpec((tT, w), lambda h, i: (nI - 1 - i, h))
    blk = (6 * _nbytes((tT, w), F32) + _nbytes((hb, nc, HD, HD), F32) + 4 * _nbytes((tT, w), _MXU)
           + _ChipExchange.scratch_bytes(partials))
    outs = pl.pallas_call(
        body, name="hgrn_bwd", grid=(HEADS // hb, nI),
        in_specs=[pl.BlockSpec((2, w), lambda h, i: (0, h)), pl.BlockSpec((1, w), lambda h, i: (0, h)),
                  zpart(3), zpart(4), zpart(5), zpart(6), act, act,
                  pl.BlockSpec((hb, nc, HD, HD), lambda h, i: (h, nI - 1 - i, 0, 0))] + [ANY] * n,
        out_specs=[pl.BlockSpec((4, tT, w), lambda h, i: (0, nI - 1 - i, h)),
                   pl.BlockSpec((8, w), lambda h, i: (0, h))] + [ANY] * n,
        out_shape=[jax.ShapeDtypeStruct((4, T, D), _MXU), jax.ShapeDtypeStruct((8, D), F32)]
        + [pltpu.HBM(p.shape, p.dtype) for p in partials],
        scratch_shapes=[pltpu.VMEM((hb, HD, HD), F32)] + _ChipExchange.scratch(partials),
        compiler_params=_cparams(blk, 2),
    )(lb_logits, onorm_g, z, z, z, z, o_raw, dyh, s_chunks, *_in_hbm(partials))
    return outs[0], outs[1], outs[2:]


def _tail(x, yc, yh, p, target, w_out, w_pg, w_pp, pe_g, fin_g, tT):
    T = x.shape[0]

    def body(x_ref, yc_ref, yh_ref, p_ref, t_ref, wo_ref, wg_ref, wp_ref, pg_ref, fg_ref,
             dyc_ref, dyh_ref, n2_ref, ds_ref, dpe_ref, dhb_ref, pb_ref, vec_ref):
        @pl.when(pl.program_id(0) == 0)
        def _():
            vec_ref[...] = jnp.zeros_like(vec_ref)
        wo_c, wo_h = wo_ref[0:D, :], wo_ref[D:2 * D, :]
        h = x_ref[...] + _dot(yc_ref[...], wo_c) + _dot(yh_ref[...], wo_h)
        pb = p_ref[...].astype(_MXU)
        pe = _dot(pb, wp_ref[...])
        r2 = lax.rsqrt(_mean_lanes(h * h) + EPS)
        hn = h * r2
        n2 = (hn * pg_ref[...]).astype(_MXU)
        gate = _sigmoid(_dot(n2, wg_ref[...]))
        h2 = h + gate * pe
        r3 = lax.rsqrt(_mean_lanes(h2 * h2) + EPS)
        h2n = h2 * r3
        err = h2n * fg_ref[...] - t_ref[...]
        vec_ref[ROW_LOSS:ROW_LOSS + 1, :] += 0.5 * jnp.sum(_mean_lanes(err * err))
        dout = err * (1.0 / D)
        vec_ref[0:1, :] += _sum_rows(dout * h2n)
        dn3 = dout * fg_ref[...]
        dh2 = r3 * (dn3 - h2n * _mean_lanes(dn3 * h2n))
        ds = (dh2 * pe * gate * (1.0 - gate)).astype(_MXU)
        dn2 = _dot_nt(ds, wg_ref[...])
        vec_ref[1:2, :] += _sum_rows(dn2 * hn)
        dnn = dn2 * pg_ref[...]
        dh = dh2 + r2 * (dnn - hn * _mean_lanes(dnn * hn))
        dhb = dh.astype(_MXU)
        dyc_ref[...] = _dot_nt(dhb, wo_c)
        dyh_ref[...] = _dot_nt(dhb, wo_h)
        n2_ref[...] = n2
        ds_ref[...] = ds
        dpe_ref[...] = (dh2 * gate).astype(_MXU)
        dhb_ref[...] = dhb
        pb_ref[...] = pb

    tok = lambda w: pl.BlockSpec((tT, w), lambda i: (i, 0))
    full = lambda r, c: pl.BlockSpec((r, c), lambda i: (0, 0))
    tokshape = lambda w, dt: jax.ShapeDtypeStruct((T, w), dt)
    blk = (5 * _nbytes((tT, D), F32) + 7 * _nbytes((tT, D), _MXU) + _nbytes((4 * D + PLE, D), _MXU)
           + 12 * _nbytes((tT, D), F32))
    return pl.pallas_call(
        body, name="tail_fwd_bwd", grid=(T // tT,),
        in_specs=[tok(D), tok(D), tok(D), tok(PLE), tok(D), full(2 * D, D), full(D, D), full(PLE, D), full(1, D), full(1, D)],
        out_specs=[tok(D), tok(D), tok(D), tok(D), tok(D), tok(D), tok(PLE), full(8, D)],
        out_shape=[tokshape(D, F32), tokshape(D, F32), tokshape(D, _MXU), tokshape(D, _MXU),
                   tokshape(D, _MXU), tokshape(D, _MXU), tokshape(PLE, _MXU),
                   jax.ShapeDtypeStruct((8, D), F32)],
        compiler_params=_cparams(blk, 1),
    )(x, yc, yh, p, target, w_out, w_pg, w_pp, pe_g, fin_g)


def _conv_bwd(z, y1, dyc, conv_w, cn_g, cn_b, w_pw2, b_pw2, tT):
    T = z.shape[0]
    nI = T // tT
    hb = tT // HALO

    def body(cv_ref, cg_ref, ct_ref, hv_ref, hg_ref, y1_ref, dyc_ref, cw_ref, ng_ref, nb_ref, wp_ref, bp_ref,
             dz_ref, a_ref, dy2_ref, vec_ref, gcw_ref, ext, ext2, gpart):
        i = pl.program_id(0)

        @pl.when(i == 0)
        def _():
            ext2[...] = jnp.zeros_like(ext2)
            gpart[...] = jnp.zeros_like(gpart)
            vec_ref[...] = jnp.zeros_like(vec_ref)
        cv, cg, ct = cv_ref[...], cg_ref[...], ct_ref[...]
        sg = _sigmoid(cg)
        has_hist = (i < nI - 1).astype(F32)
        ext[0:HALO, :] = hv_ref[...] * _sigmoid(hg_ref[...]) * has_hist
        ext[HALO:, :] = cv * sg
        yn, rstd = _group_ln(y1_ref[...])
        apre = yn * ng_ref[...] + nb_ref[...]
        sa = _sigmoid(apre)
        a = (apre * sa).astype(_MXU)
        y2 = _dot(a, wp_ref[...]) + bp_ref[...]
        st = _sigmoid(ct)
        dyc_v = dyc_ref[...]
        dy2 = dyc_v * (ct * st)
        dy2b = dy2.astype(_MXU)
        da = _dot_nt(dy2b, wp_ref[...])
        dapre = da * (sa * (1.0 + apre * (1.0 - sa)))
        dy1 = _group_ln_bwd(dapre * ng_ref[...], yn, rstd)
        vec_ref[0:1, :] += _sum_rows(dy1)
        vec_ref[1:2, :] += _sum_rows(dapre * yn)
        vec_ref[2:3, :] += _sum_rows(dapre)
        vec_ref[3:4, :] += _sum_rows(dy2)
        dz_ref[2] = (dyc_v * y2 * (st * (1.0 + ct * (1.0 - st)))).astype(_MXU)
        a_ref[...] = a
        dy2_ref[...] = dy2b
        ext2[tT:tT + HALO, :] = ext2[0:HALO, :]
        ext2[0:tT, :] = dy1
        cw = cw_ref[...]
        strips = []
        for g in range(D // TAP_LANES):
            ln = slice(g * TAP_LANES, (g + 1) * TAP_LANES)
            d = ext2[0:tT, ln]

            def grad_tap(m, win, d=d, ln=ln):
                p = d * win
                part = p[0:8, :]
                for q in range(1, tT // 8):
                    part = part + p[8 * q:8 * q + 8, :]
                gpart[m - 2, :, ln] += part
            _shifted_windows(ext[:, ln], 2, grad_tap)
            acc = [None]

            def dv_tap(m, win, acc=acc, ln=ln):
                term = win * cw[CONV_K - 1 - m:CONV_K - m, ln]
                acc[0] = term if acc[0] is None else acc[0] + term
            _shifted_windows(ext2[:, ln], 0, dv_tap)
            strips.append(acc[0])
        dv = jnp.concatenate(strips, axis=1)
        dz_ref[0] = (dv * sg).astype(_MXU)
        dz_ref[1] = (dv * cv * sg * (1.0 - sg)).astype(_MXU)

        @pl.when(i == nI - 1)
        def _():
            gcw_ref[...] = jnp.sum(gpart[...], axis=1)

    part = lambda p: pl.BlockSpec((tT, D), lambda i: (nI - 1 - i, p))
    hist = lambda p: pl.BlockSpec((HALO, D), lambda i: (jnp.maximum((nI - 1 - i) * hb - 1, 0), p))
    tok = pl.BlockSpec((tT, D), lambda i: (nI - 1 - i, 0))
    row = pl.BlockSpec((1, D), lambda i: (0, 0))
    blk = (5 * _nbytes((tT, D), F32) + _nbytes((D, D), _MXU) + 5 * _nbytes((tT, D), _MXU)
           + 10 * _nbytes((tT + HALO, D), F32))
    return pl.pallas_call(
        body, name="conv_bwd", grid=(nI,),
        in_specs=[part(0), part(1), part(2), hist(0), hist(1), tok, tok, pl.BlockSpec((HALO, D), lambda i: (0, 0)),
                  row, row, pl.BlockSpec((D, D), lambda i: (0, 0)), row],
        out_specs=[pl.BlockSpec((3, tT, D), lambda i: (0, nI - 1 - i, 0)), tok, tok,
                   pl.BlockSpec((8, D), lambda i: (0, 0)), pl.BlockSpec((HALO, D), lambda i: (0, 0))],
        out_shape=[jax.ShapeDtypeStruct((3, T, D), _MXU), jax.ShapeDtypeStruct((T, D), _MXU),
                   jax.ShapeDtypeStruct((T, D), _MXU), jax.ShapeDtypeStruct((8, D), F32),
                   jax.ShapeDtypeStruct((HALO, D), F32)],
        scratch_shapes=[pltpu.VMEM((tT + HALO, D), F32), pltpu.VMEM((tT + HALO, D), F32), pltpu.VMEM((HALO, 8, D), F32)],
        compiler_params=_cparams(blk, 1),
    )(z, z, z, z, z, y1, dyc, conv_w, cn_g, cn_b, w_pw2, b_pw2)


def _inproj_bwd_u(dzc, dzh, w_in, partials, tT):
    T = dzc.shape[1]
    n = len(partials)
    at = _hosted_steps((T // tT) * NPART)

    def body(dzc_ref, dzh_ref, w_ref, *rest):
        du_ref, acc = rest[n], rest[2 * n + 1]
        exchange = _ChipExchange(n, rest[:n], rest[n + 1:2 * n + 1], rest[2 * n + 2:3 * n + 2], rest[3 * n + 2:])
        j = pl.program_id(1)
        step = pl.program_id(0) * NPART + j
        pl.when(step == at["start"])(exchange.start)
        pl.when(step == at["turn"])(exchange.turn)

        @pl.when(j == 0)
        def _():
            acc[...] = jnp.zeros_like(acc)

        @pl.when(j < 3)
        def _():
            acc[...] += _dot_nt(dzc_ref[0], w_ref[...])

        @pl.when(j >= 3)
        def _():
            acc[...] += _dot_nt(dzh_ref[0], w_ref[...])

        @pl.when(j == NPART - 1)
        def _():
            du_ref[...] = acc[...].astype(du_ref.dtype)
        pl.when(step == at["finish"])(exchange.finish)

    blk = (3 * _nbytes((tT, D), _MXU) + _nbytes((D, D), _MXU) + _nbytes((tT, D), F32)
           + _ChipExchange.scratch_bytes(partials))
    outs = pl.pallas_call(
        body, name="inproj_bwd_u", grid=(T // tT, NPART),
        in_specs=[pl.BlockSpec((1, tT, D), lambda i, j: (jnp.minimum(j, 2), i, 0)),
                  pl.BlockSpec((1, tT, D), lambda i, j: (jnp.maximum(j - 3, 0), i, 0)),
                  pl.BlockSpec((D, D), lambda i, j: (0, j))] + [ANY] * n,
        out_specs=[pl.BlockSpec((tT, D), lambda i, j: (i, 0))] + [ANY] * n,
        out_shape=[jax.ShapeDtypeStruct((T, D), _MXU)] + [pltpu.HBM(p.shape, p.dtype) for p in partials],
        scratch_shapes=[pltpu.VMEM((tT, D), F32)] + _ChipExchange.scratch(partials),
        compiler_params=_cparams(blk, 2),
    )(dzc, dzh, w_in, *_in_hbm(partials))
    return outs[0], outs[1:]


def _inproj_bwd_x(x, ln_g, du, dh, tT):
    T = x.shape[0]

    def body(x_ref, g_ref, du_ref, dh_ref, gx_ref, vec_ref):
        @pl.when(pl.program_id(0) == 0)
        def _():
            vec_ref[...] = jnp.zeros_like(vec_ref)
        xv = x_ref[...]
        r = lax.rsqrt(_mean_lanes(xv * xv) + EPS)
        xn = xv * r
        duv = du_ref[...].astype(F32)
        vec_ref[0:1, :] += _sum_rows(duv * xn)
        dun = duv * g_ref[...]
        gx_ref[...] = dh_ref[...].astype(F32) + r * (dun - xn * _mean_lanes(dun * xn))

    tok = pl.BlockSpec((tT, D), lambda i: (i, 0))
    return pl.pallas_call(
        body, name="inproj_bwd_x", grid=(T // tT,),
        in_specs=[tok, pl.BlockSpec((1, D), lambda i: (0, 0)), tok, tok],
        out_specs=[tok, pl.BlockSpec((8, D), lambda i: (0, 0))],
        out_shape=[jax.ShapeDtypeStruct((T, D), F32), jax.ShapeDtypeStruct((8, D), F32)],
        compiler_params=_cparams(6 * _nbytes((tT, D), F32), 1),
    )(x, ln_g, du, dh)


def _inproj_bwd_w(u, dzc, dzh, tk):
    T = u.shape[0]
    nK = T // tk

    def body(u_ref, dzc_ref, dzh_ref, gw_ref):
        j, k = pl.program_id(0), pl.program_id(1)

        @pl.when(k == 0)
        def _():
            gw_ref[...] = jnp.zeros_like(gw_ref)

        @pl.when(j < 3)
        def _():
            gw_ref[...] += _dot_tn(u_ref[...], dzc_ref[0])

        @pl.when(j >= 3)
        def _():
            gw_ref[...] += _dot_tn(u_ref[...], dzh_ref[0])

    blk = 3 * _nbytes((tk, D), _MXU) + 2 * _nbytes((D, D), F32)
    return pl.pallas_call(
        body, name="inproj_bwd_w", grid=(NPART, nK),
        in_specs=[pl.BlockSpec((tk, D), lambda j, k: (k, 0)),
                  pl.BlockSpec((1, tk, D), lambda j, k: (jnp.minimum(j, 2), jnp.where(j < 3, k, nK - 1), 0)),
                  pl.BlockSpec((1, tk, D), lambda j, k: (jnp.maximum(j - 3, 0), jnp.where(j < 3, 0, k), 0))],
        out_specs=pl.BlockSpec((D, D), lambda j, k: (0, j)),
        out_shape=jax.ShapeDtypeStruct((D, NPART * D), F32),
        compiler_params=_cparams(blk, 2),
    )(u, dzc, dzh)


def _tn_matmul(a, b, tk, name):
    T, M = a.shape
    N = b.shape[1]

    def body(a_ref, b_ref, o_ref):
        @pl.when(pl.program_id(0) == 0)
        def _():
            o_ref[...] = jnp.zeros_like(o_ref)
        o_ref[...] += _dot_tn(a_ref[...], b_ref[...])

    blk = _nbytes((tk, M), _MXU) + _nbytes((tk, N), _MXU) + 2 * _nbytes((M, N), F32)
    return pl.pallas_call(
        body, name=name, grid=(T // tk,),
        in_specs=[pl.BlockSpec((tk, M), lambda k: (k, 0)), pl.BlockSpec((tk, N), lambda k: (k, 0))],
        out_specs=pl.BlockSpec((M, N), lambda k: (0, 0)),
        out_shape=pltpu.HBM((M, N), F32),
        compiler_params=_cparams(blk, 1),
    )(a, b)


def _place():
    return lax.axis_index("x"), lax.axis_index("y"), lax.axis_index("c")


def _flip(v, d):
    return 1 - v if d else v


CHIP_MOVES = [(1, 0), (0, 1), (1, 1)]
DEV_MOVES = [(dx, dy, dc) for dx in (0, 1) for dy in (0, 1) for dc in (0, 1)][1:]


def _shard_slice(ref, axis, size, s):
    start = pl.multiple_of(s * size, size)
    return ref.at[pl.ds(start, size), :] if axis == 0 else ref.at[:, pl.ds(start, size)]


class _Bounce:
    def __init__(self, src, buf, dst, sem_in, sem_out):
        self.load = pltpu.make_async_copy(src, buf, sem_in)
        self.store = pltpu.make_async_copy(buf, dst, sem_out)

    def start(self):
        self.load.start()

    def turn(self):
        self.load.wait()
        self.store.start()

    def wait(self):
        self.store.wait()


def _comm_params(scratch_bytes):
    return pltpu.CompilerParams(vmem_limit_bytes=int(min(V7X_VMEM_LIMIT, scratch_bytes + (8 << 20))))


class _Gather:
    def __init__(self, shapes, axes, ins, outs, bufs, sems):
        self.shapes, self.axes, self.ins, self.outs, self.bufs = shapes, axes, ins, outs, bufs
        self.ici_send, self.ici_recv, self.d2d_send, self.d2d_recv, self.in_sems, self.out_sems = sems
        self.x, self.y, self.c = _place()
        self.me = 2 * self.x + self.y
        self.pairs = [(k, j) for k in range(len(shapes)) for j in range(3)]

    @staticmethod
    def scratch(shards):
        n = len(shards)
        return ([pltpu.VMEM(s.shape, s.dtype) for s in shards]
                + [pltpu.SemaphoreType.DMA((3 * n,))] * 4 + [pltpu.SemaphoreType.DMA((n,))] * 2)

    def _own_half(self, k, hc):
        half = self.shapes[k][0] // 2
        return self.ins[k].at[pl.ds(pl.multiple_of(hc * half, 16), half), :]

    def _region(self, k, who, hc):
        rows, cols = self.shapes[k]
        half = rows // 2
        if self.axes[k] == 0:
            return self.outs[k].at[pl.ds(pl.multiple_of(who * rows + hc * half, 16), half), :]
        return self.outs[k].at[pl.ds(pl.multiple_of(hc * half, 16), half), pl.ds(pl.multiple_of(who * cols, HD), cols)]

    def _peer(self, j):
        return 2 * _flip(self.x, CHIP_MOVES[j][0]) + _flip(self.y, CHIP_MOVES[j][1])

    def _ici(self, k, j, who, hc):
        dx, dy = CHIP_MOVES[j]
        return pltpu.make_async_remote_copy(
            src_ref=self._own_half(k, hc), dst_ref=self._region(k, who, hc),
            send_sem=self.ici_send.at[3 * k + j], recv_sem=self.ici_recv.at[3 * k + j],
            device_id=(_flip(self.x, dx), _flip(self.y, dy), self.c), device_id_type=MESH_ID)

    def _d2d(self, k, j, who, hc):
        return pltpu.make_async_remote_copy(
            src_ref=self._region(k, who, hc), dst_ref=self._region(k, who, hc),
            send_sem=self.d2d_send.at[3 * k + j], recv_sem=self.d2d_recv.at[3 * k + j],
            device_id=(self.x, self.y, 1 - self.c), device_id_type=MESH_ID)

    def _local(self, k):
        size = self.shapes[k][self.axes[k]]
        return _Bounce(self.ins[k], self.bufs[k], _shard_slice(self.outs[k], self.axes[k], size, self.me),
                       self.in_sems.at[k], self.out_sems.at[k])

    def start(self):
        for k in range(len(self.shapes)):
            self._local(k).start()
        for k, j in self.pairs:
            self._ici(k, j, self.me, self.c).start()

    def turn(self):
        for k in range(len(self.shapes)):
            self._local(k).turn()

    def forward(self):
        for k, j in self.pairs:
            self._ici(k, j, self._peer(j), self.c).wait_recv()
            self._d2d(k, j, self._peer(j), self.c).start()

    def finish(self):
        for k, j in self.pairs:
            self._d2d(k, j, self._peer(j), 1 - self.c).wait_recv()
        for k, j in self.pairs:
            self._ici(k, j, self.me, self.c).wait_send()
            self._d2d(k, j, self._peer(j), self.c).wait_send()
        for k in range(len(self.shapes)):
            self._local(k).wait()


def _full_shapes(shards, axes):
    return [tuple(d * (N_CHIPS if a == ax else 1) for a, d in enumerate(s.shape)) for s, ax in zip(shards, axes)]


class _Slab:
    def __init__(self, arrays, pick, shard_shape):
        self.arrays = arrays
        self.pick = pick
        self.rows, self.cols = shard_shape
        self.half = self.rows // 2


PAIR_SUM_ROWS = 64


def _pair_exchange_sum(slabs, name):
    n = len(slabs)
    n_in = sum(len(sl.arrays) for sl in slabs)

    def body(*refs):
        ins, outs = refs[:n_in], refs[n_in:n_in + n]
        mine, got, total = (refs[n_in + (1 + t) * n:n_in + (2 + t) * n] for t in range(3))
        send_sems, recv_sems, in_sems, out_sems = refs[n_in + 4 * n:]
        x, y, c = _place()
        started = []
        base = 0
        for k, sl in enumerate(slabs):
            for s in range(N_CHIPS):
                ai, r0, c0 = sl.pick(s)
                src = ins[base + ai]

                def half(hc):
                    return src.at[pl.ds(pl.multiple_of(r0 + hc * sl.half, 8), sl.half), pl.ds(c0, sl.cols)]
                q = N_CHIPS * k + s
                load = pltpu.make_async_copy(half(c), mine[k].at[s], in_sems.at[q])
                load.start()
                cp = pltpu.make_async_remote_copy(
                    src_ref=half(1 - c), dst_ref=got[k].at[s], send_sem=send_sems.at[q], recv_sem=recv_sems.at[q],
                    device_id=(x, y, 1 - c), device_id_type=MESH_ID)
                cp.start()
                store = pltpu.make_async_copy(total[k].at[s], outs[k].at[s], out_sems.at[q])
                started.append((k, s, sl.half, load, cp, store))
            base += len(sl.arrays)
        for k, s, half_rows, load, cp, store in started:
            load.wait()
            cp.wait_recv()
            rows = min(half_rows, PAIR_SUM_ROWS)

            def add(t, carry, k=k, s=s, rows=rows):
                sl_ = pl.ds(pl.multiple_of(t * rows, rows), rows)
                total[k][s, sl_, :] = (mine[k][s, sl_, :] + got[k][s, sl_, :]).astype(_WIRE)
                return carry
            lax.fori_loop(0, half_rows // rows, add, 0)
            store.start()
        for k, s, half_rows, load, cp, store in started:
            cp.wait_send()
            store.wait()

    flat_in = [a for sl in slabs for a in sl.arrays]
    shapes = [(N_CHIPS, sl.half, sl.cols) for sl in slabs]
    vmem = [pltpu.VMEM(sh, dt) for dt in (F32, F32, _WIRE) for sh in shapes]
    return pl.pallas_call(
        body, name=name,
        in_specs=[ANY] * n_in, out_specs=[ANY] * n, out_shape=[pltpu.HBM(sh, _WIRE) for sh in shapes],
        scratch_shapes=vmem + [pltpu.SemaphoreType.DMA((N_CHIPS * n,))] * 4,
        compiler_params=_comm_params(sum(_nbytes(sh, F32) * 2 + _nbytes(sh, _WIRE) for sh in shapes)),
    )(*_in_hbm(flat_in))


class _ChipExchange:
    def __init__(self, n, ins, outs, bufs, sems):
        self.n, self.ins, self.outs, self.bufs = n, ins, outs, bufs
        self.send_sems, self.recv_sems, self.in_sems, self.out_sems = sems
        self.x, self.y, self.c = _place()
        self.me = 2 * self.x + self.y
        self.pairs = [(k, j) for k in range(n) for j in range(3)]

    @staticmethod
    def scratch(partials):
        n = len(partials)
        return ([pltpu.VMEM(p.shape[1:], p.dtype) for p in partials]
                + [pltpu.SemaphoreType.DMA((3 * n,))] * 2 + [pltpu.SemaphoreType.DMA((n,))] * 2)

    @staticmethod
    def scratch_bytes(partials):
        return sum(_nbytes(p.shape[1:], p.dtype) for p in partials)

    def _copy(self, k, j, src_slot, dst_slot):
        px, py = _flip(self.x, CHIP_MOVES[j][0]), _flip(self.y, CHIP_MOVES[j][1])
        return pltpu.make_async_remote_copy(
            src_ref=self.ins[k].at[src_slot], dst_ref=self.outs[k].at[dst_slot],
            send_sem=self.send_sems.at[3 * k + j], recv_sem=self.recv_sems.at[3 * k + j],
            device_id=(px, py, self.c), device_id_type=MESH_ID)

    def _peer(self, j):
        return 2 * _flip(self.x, CHIP_MOVES[j][0]) + _flip(self.y, CHIP_MOVES[j][1])

    def _local(self, k):
        return _Bounce(self.ins[k].at[self.me], self.bufs[k], self.outs[k].at[self.me],
                       self.in_sems.at[k], self.out_sems.at[k])

    def start(self):
        for k in range(self.n):
            self._local(k).start()
        for k, j in self.pairs:
            self._copy(k, j, self._peer(j), self.me).start()

    def turn(self):
        for k in range(self.n):
            self._local(k).turn()

    def finish(self):
        for k, j in self.pairs:
            self._copy(k, j, self.me, self._peer(j)).wait_recv()
        for k, j in self.pairs:
            self._copy(k, j, self._peer(j), self.me).wait_send()
        for k in range(self.n):
            self._local(k).wait()


def _hosted_steps(steps):
    return dict(start=0, turn=steps // 4, forward=steps // 2, finish=steps - 1)


def _pair_share(slots, vec):
    n = len(slots)
    nv = len(DEV_MOVES)

    def body(*refs):
        ins, vec_ref = refs[:n], refs[n]
        outs, vec_out = refs[n + 1:2 * n + 1], refs[2 * n + 1]
        slot_b, half_b, vec_b = refs[2 * n + 2:3 * n + 2], refs[3 * n + 2:4 * n + 2], refs[4 * n + 2]
        send_sems, recv_sems, in_sems, out_sems = refs[4 * n + 3:]
        x, y, c = _place()
        dev = 4 * x + 2 * y + c

        def vec_copy(j, slot):
            dx, dy, dc = DEV_MOVES[j]
            return pltpu.make_async_remote_copy(
                src_ref=vec_ref, dst_ref=vec_out.at[slot], send_sem=send_sems.at[n + j], recv_sem=recv_sems.at[n + j],
                device_id=(_flip(x, dx), _flip(y, dy), _flip(c, dc)), device_id_type=MESH_ID)

        def rows(k, hc):
            hr = slots[k].shape[1]
            return outs[k].at[pl.ds(pl.multiple_of(hc * hr, 8), hr), :]

        def share(k, hc):
            return pltpu.make_async_remote_copy(
                src_ref=half_b[k], dst_ref=rows(k, hc), send_sem=send_sems.at[k], recv_sem=recv_sems.at[k],
                device_id=(x, y, 1 - c), device_id_type=MESH_ID)

        vec_loc = _Bounce(vec_ref, vec_b, vec_out.at[dev], in_sems.at[n], out_sems.at[n])
        vec_loc.start()
        for j in range(nv):
            vec_copy(j, dev).start()
        loads = [pltpu.make_async_copy(ins[k], slot_b[k], in_sems.at[k]) for k in range(n)]
        stores = [pltpu.make_async_copy(half_b[k], rows(k, c), out_sems.at[k]) for k in range(n)]
        for load in loads:
            load.start()
        vec_loc.turn()
        for k in range(n):
            loads[k].wait()
            hr = slots[k].shape[1]
            step_rows = min(hr, PAIR_SUM_ROWS)

            def add(t, carry, k=k, step_rows=step_rows):
                sl_ = pl.ds(pl.multiple_of(t * step_rows, step_rows), step_rows)
                acc = slot_b[k][0, sl_, :].astype(F32)
                for s in range(1, N_CHIPS):
                    acc = acc + slot_b[k][s, sl_, :].astype(F32)
                half_b[k][sl_, :] = acc
                return carry
            lax.fori_loop(0, hr // step_rows, add, 0)
            stores[k].start()
            share(k, c).start()
        for k in range(n):
            share(k, 1 - c).wait_recv()
        for j, (dx, dy, dc) in enumerate(DEV_MOVES):
            vec_copy(j, 4 * _flip(x, dx) + 2 * _flip(y, dy) + _flip(c, dc)).wait_recv()
        for k in range(n):
            share(k, c).wait_send()
            stores[k].wait()
        for j in range(nv):
            vec_copy(j, dev).wait_send()
        vec_loc.wait()

    halves = [s.shape[1:] for s in slots]
    vmem = ([pltpu.VMEM(s.shape, s.dtype) for s in slots] + [pltpu.VMEM(h, F32) for h in halves]
            + [pltpu.VMEM(vec.shape, F32)])
    outs = pl.pallas_call(
        body, name="grad_pair_share",
        in_specs=[ANY] * (n + 1), out_specs=[ANY] * (n + 1),
        out_shape=[pltpu.HBM((2 * h[0], h[1]), F32) for h in halves] + [pltpu.HBM((N_DEV,) + vec.shape, F32)],
        scratch_shapes=vmem + [pltpu.SemaphoreType.DMA((n + nv,))] * 2 + [pltpu.SemaphoreType.DMA((n + 1,))] * 2,
        compiler_params=_comm_params(sum(_nbytes(s.shape, s.dtype) for s in slots) + sum(_nbytes(h, F32) for h in halves)
                                     + _nbytes(vec.shape, F32)),
    )(*_in_hbm(list(slots) + [vec]))
    return outs[:n], outs[n]


def _row_block(rows, cols, n_arrays):
    br = rows
    while br % 16 == 0 and 2 * n_arrays * br * cols * 4 > (16 << 20):
        br //= 2
    return br


def _sum_slots(a, name):
    n, rows, cols = a.shape
    br = _row_block(rows, cols, n + 1)

    def body(a_ref, o_ref):
        acc = a_ref[0].astype(F32)
        for s in range(1, n):
            acc = acc + a_ref[s].astype(F32)
        o_ref[...] = acc

    return pl.pallas_call(body, name=name, grid=(rows // br,),
                          in_specs=[pl.BlockSpec((n, br, cols), lambda i: (0, i, 0))],
                          out_specs=pl.BlockSpec((br, cols), lambda i: (i, 0)),
                          out_shape=pltpu.HBM((rows, cols), F32),
                          compiler_params=_cparams((n + 1) * br * cols * 4, 1))(*_in_hbm([a]))


def _adamw_math(w, g, m, v):
    m = ADAM_B1 * m + (1.0 - ADAM_B1) * g
    v = ADAM_B2 * v + (1.0 - ADAM_B2) * (g * g)
    m_hat = m / (1.0 - ADAM_B1 ** ADAM_STEP)
    v_hat = v / (1.0 - ADAM_B2 ** ADAM_STEP)
    delta = -ADAM_LR * (m_hat / (jnp.sqrt(v_hat) + ADAM_EPS) + ADAM_WD * w)
    return delta, m, v


def _adamw(g, w, m, v, name):
    rows, cols = g.shape
    br = _row_block(rows, cols, 7)

    def body(g_ref, w_ref, m_ref, v_ref, d_ref, nm_ref, nv_ref):
        d_ref[...], nm_ref[...], nv_ref[...] = _adamw_math(w_ref[...], g_ref[...], m_ref[...], v_ref[...])

    spec = pl.BlockSpec((br, cols), lambda i: (i, 0))
    return pl.pallas_call(body, name=name, grid=(rows // br,), in_specs=[spec] * 4, out_specs=[spec] * 3,
                          out_shape=[jax.ShapeDtypeStruct(g.shape, F32)] * 3,
                          compiler_params=_cparams(7 * br * cols * 4, 1))(g, w, m, v)


ROW_FINAL_G, ROW_PE_G, ROW_LOSS = 0, 1, 2
ROW_CONV_B, ROW_CN_G, ROW_CN_B, ROW_B_PW2 = 8, 9, 10, 11
ROW_LN_G = 16
ROW_ONORM_G, ROW_LB = 24, 25
ROW_CONV_W = 32
SMALL = ["ln_g", "conv_b", "cnorm_g", "cnorm_b", "b_pw2", "onorm_g", "pe_norm_g", "final_g"]
SMALL_ROW = dict(ln_g=ROW_LN_G, conv_b=ROW_CONV_B, cnorm_g=ROW_CN_G, cnorm_b=ROW_CN_B, b_pw2=ROW_B_PW2,
                 onorm_g=ROW_ONORM_G, pe_norm_g=ROW_PE_G, final_g=ROW_FINAL_G)


def _adamw_small(vsum, gcw, lb_logits, params):
    names = SMALL + ["lb_logits", "conv_w"]
    flat = [t for nm in names for t in params[nm]]

    def body(*refs):
        vs_ref, gcw_ref, lbl_ref = refs[:3]
        ins = refs[3:3 + 3 * len(names)]
        outs = refs[3 + 3 * len(names):]
        for q, nm in enumerate(names):
            w_ref, m_ref, v_ref = ins[3 * q:3 * q + 3]
            g_ref, d_ref, nm_ref, nv_ref = outs[4 * q:4 * q + 4]
            if nm == "conv_w":
                g = gcw_ref[...]
            elif nm == "lb_logits":
                lb = _softmax_row0(lbl_ref[...])
                g0 = vs_ref[ROW_LB:ROW_LB + 1, :] * lb * (1.0 - lb)
                g = jnp.concatenate([g0, -g0], axis=0)
            else:
                g = vs_ref[SMALL_ROW[nm]:SMALL_ROW[nm] + 1, :]
            g_ref[...] = g
            d_ref[...], nm_ref[...], nv_ref[...] = _adamw_math(w_ref[...], g, m_ref[...], v_ref[...])

    out_shape = [jax.ShapeDtypeStruct(params[nm][0].shape, F32) for nm in names for _ in range(4)]
    outs = pl.pallas_call(body, name="adamw_small", out_shape=out_shape)(vsum, gcw, lb_logits, *flat)
    return {nm: tuple(outs[4 * q:4 * q + 4]) for q, nm in enumerate(names)}


TOKEN_TILE = dict(rmsnorm=512, inproj_fwd=2048, conv=256, hgrn=512, tail=512, inproj_bwd_u=2048, inproj_bwd_x=512,
                  weight_grad=2048)


def _tile(T, family):
    return min(T, TOKEN_TILE[family])


def kernel(x, p, ln_g, w_in, conv_w, conv_b, cnorm_g, cnorm_b, w_pw2, b_pw2, lb_logits, onorm_g, w_out, pe_norm_g, w_pg, w_pp, final_g, loss_target, m_ln_g, m_w_in, m_conv_w, m_conv_b, m_cnorm_g, m_cnorm_b, m_w_pw2, m_b_pw2, m_lb_logits, m_onorm_g, m_w_out, m_pe_norm_g, m_w_pg, m_w_pp, m_final_g, v_ln_g, v_w_in, v_conv_w, v_conv_b, v_cnorm_g, v_cnorm_b, v_w_pw2, v_b_pw2, v_lb_logits, v_onorm_g, v_w_out, v_pe_norm_g, v_w_pg, v_w_pp, v_final_g):
    given = dict(locals())
    x2, p2, tgt = x[0], p[0, 0], loss_target[0]
    T = x2.shape[0]
    fin_g = final_g.reshape(1, D)

    conv_w_pad = jnp.pad(conv_w[0], ((0, HALO - CONV_K), (0, 0)))
    u, (w_in_f,) = _rmsnorm_gather(x2, ln_g, [w_in[0].astype(_MXU)], [1], _tile(T, "rmsnorm"))

    z, (w_pw2_f, w_out_f, w_pg_f, w_pp_f, conv_w_f) = _inproj_fwd(
        u, w_in_f,
        [w_pw2[0].astype(_MXU), w_out[0].astype(_MXU), w_pg[0].astype(_MXU), w_pp[0].astype(_MXU), conv_w_pad],
        [0, 0, 0, 1, 1], _tile(T, "inproj_fwd"))
    yc, y1 = _conv_fwd(z, conv_w_f, conv_b, cnorm_g, cnorm_b, w_pw2_f, b_pw2, _tile(T, "conv"))
    o_raw, yh, s_chunks = _hgrn_fwd(z, lb_logits, onorm_g, _tile(T, "hgrn"), HB)
    dyc, dyh, n2, ds, dpe, dhb, pb, vec_tail = _tail(
        x2, yc, yh, p2, tgt, w_out_f, w_pg_f, w_pp_f, pe_norm_g, fin_g, _tile(T, "tail"))
    tk = _tile(T, "weight_grad")
    g_w_out_c = _tn_matmul(yc, dhb, tk, "grad_w_out_conv")
    g_w_out_h = _tn_matmul(yh, dhb, tk, "grad_w_out_hgrn")
    g_w_pg = _tn_matmul(n2, ds, tk, "grad_w_pg")
    g_w_pp = _tn_matmul(pb, dpe, tk, "grad_w_pp")
    dzc, a_act, dy2, vec_conv, g_conv_w = _conv_bwd(z, y1, dyc, conv_w_f, cnorm_g, cnorm_b, w_pw2_f, b_pw2, _tile(T, "conv"))
    g_w_pw2 = _tn_matmul(a_act, dy2, tk, "grad_w_pw2")

    rest = ["w_pw2", "w_out", "w_pg", "w_pp"]
    partial_rest = _pair_exchange_sum([
        _Slab([g_w_pw2], lambda s: (0, s * (D // N_CHIPS), 0), (D // N_CHIPS, D)),
        _Slab([g_w_out_c, g_w_out_h], lambda s: (s // 2, (s % 2) * (D // 2), 0), (D // 2, D)),
        _Slab([g_w_pg], lambda s: (0, s * (D // N_CHIPS), 0), (D // N_CHIPS, D)),
        _Slab([g_w_pp], lambda s: (0, 0, s * (D // N_CHIPS)), (PLE, D // N_CHIPS)),
    ], "grad_pair_exchange_rest")
    dzh, vec_hgrn, slots_rest = _hgrn_bwd(z, lb_logits, onorm_g, o_raw, dyh, s_chunks, partial_rest, _tile(T, "hgrn"), HB)
    g_w_in = _inproj_bwd_w(u, dzc, dzh, tk)
    partial_in = _pair_exchange_sum([
        _Slab([g_w_in], lambda s: (0, 0, s * (NPART * D // N_CHIPS)), (D, NPART * D // N_CHIPS))], "grad_pair_exchange_w_in")
    du, slots_in = _inproj_bwd_u(dzc, dzh, w_in_f, partial_in, _tile(T, "inproj_bwd_u"))
    grad_x, vec_in = _inproj_bwd_x(x2, ln_g, du, dhb, _tile(T, "inproj_bwd_x"))
    big = ["w_in"] + rest
    vec = jnp.concatenate([vec_tail, vec_conv, vec_in, vec_hgrn, g_conv_w], axis=0)
    grads_big, vec_slots = _pair_share(list(slots_in) + list(slots_rest), vec)
    vsum = _sum_slots(vec_slots, "vec_sum")

    out = {}
    for nm, g in zip(big, grads_big):
        w2, m2, v2 = given[nm][0], given["m_" + nm][0], given["v_" + nm][0]
        d, nm_, nv_ = _adamw(g, w2, m2, v2, "adamw_" + nm)
        out[nm] = tuple(t[None] for t in (g, d, nm_, nv_))
    chip = 2 * lax.axis_index("x") + lax.axis_index("y")
    gcw = lax.dynamic_slice(vsum, (ROW_CONV_W, chip * (D // N_CHIPS)), (CONV_K, D // N_CHIPS))
    params = {nm: (given[nm].reshape(-1, D), given["m_" + nm].reshape(-1, D), given["v_" + nm].reshape(-1, D))
              for nm in SMALL + ["lb_logits"]}
    params["conv_w"] = (conv_w[0], m_conv_w[0], v_conv_w[0])
    small = _adamw_small(vsum, gcw, lb_logits, params)
    for nm, ts in small.items():
        out[nm] = tuple(t.reshape(given[nm].shape) for t in ts)

    loss = vsum[ROW_LOSS, 0]
    order = ["ln_g", "w_in", "conv_w", "conv_b", "cnorm_g", "cnorm_b", "w_pw2", "b_pw2", "lb_logits", "onorm_g",
             "w_out", "pe_norm_g", "w_pg", "w_pp", "final_g"]
    return (loss, grad_x[None], *[out[nm][0] for nm in order], *[out[nm][1] for nm in order],
            *[out[nm][2] for nm in order], *[out[nm][3] for nm in order])
```
